```python
import jax, jax.numpy as jnp
from jax import lax
import numpy as np

D_MODEL = 1024
BATCH = 8
SEQ = 8192
DEPTH = 2

ATTN_GROUPS = ((128, 1), (512, 4), (2048, 16))
N_GROUPS = 3
ATTN_HEADS = 8
ATTN_HEAD_DIM = 128
ATTN_WIDTH = ATTN_HEADS * ATTN_HEAD_DIM
RET_HEADS = 4
RET_KEY_DIM = 256
RET_VAL_DIM = 512
RET_QK_WIDTH = RET_HEADS * RET_KEY_DIM
RET_V_WIDTH = RET_HEADS * RET_VAL_DIM
RET_CHUNK = 128
ROPE_BASE = 10000.0
D_FF = 2816
CONV_WIDTH = 3
PLE_DIM = 256
MAX_POS_OFFSET = 4096
IN_SIZES = (N_GROUPS * 3 * ATTN_WIDTH, RET_QK_WIDTH, RET_QK_WIDTH, RET_V_WIDTH, RET_V_WIDTH, D_MODEL, D_MODEL)
N_IN = sum(IN_SIZES)
DN_ALPHA = (2 * DEPTH) ** 0.25
DN_BETA = (8 * DEPTH) ** -0.25
LN_EPS = 1e-5
GN_EPS = 1e-6

kernel_name = 'hybrid_dilated_attn_retention_convffn_deepnorm'


def layer_norm(x, g, b):
    xf = x.astype(jnp.float32)
    mu = xf.mean(-1, keepdims=True)
    var = jnp.square(xf - mu).mean(-1, keepdims=True)
    return ((xf - mu) * lax.rsqrt(var + LN_EPS) * g + b).astype(x.dtype)


def dilated_window_attention(q, k, v, window, dilation):
    b, s, h, dh = q.shape
    span = window // dilation
    blk = span
    sub_len = -(-s // dilation)
    sub_len = -(-sub_len // blk) * blk
    s_pad = sub_len * dilation
    nb = sub_len // blk

    def to_blocks(t):
        t = jnp.pad(t.astype(jnp.float32), ((0, 0), (0, s_pad - s), (0, 0), (0, 0)))
        t = t.reshape(b, sub_len, dilation, h, dh).transpose(0, 2, 3, 1, 4)
        return t.reshape(b, dilation, h, nb, blk, dh)

    def with_prev(t):
        prev = jnp.pad(t, ((0, 0), (0, 0), (0, 0), (1, 0), (0, 0), (0, 0)))[:, :, :, :-1]
        return jnp.concatenate([prev, t], axis=4)

    qb = to_blocks(q)
    kc = with_prev(to_blocks(k))
    vc = with_prev(to_blocks(v))
    scores = jnp.einsum('brhnqd,brhnkd->brhnqk', qb, kc) * (dh ** -0.5)
    q_idx = jnp.arange(blk)[:, None] + blk
    k_idx = jnp.arange(2 * blk)[None, :]
    dist = q_idx - k_idx
    band = (dist >= 0) & (dist <= span)
    valid = (jnp.arange(nb)[:, None, None] * blk + k_idx[None] - blk) >= 0
    mask = band[None] & valid
    scores = jnp.where(mask, scores, -jnp.inf)
    m = scores.max(-1, keepdims=True)
    pr = jnp.exp(scores - m)
    denom = pr.sum(-1, keepdims=True)
    out = jnp.einsum('brhnqk,brhnkd->brhnqd', pr, vc) / denom
    lse = (m + jnp.log(denom))[..., 0]
    out = out.reshape(b, dilation, h, sub_len, dh).transpose(0, 3, 1, 2, 4).reshape(b, s_pad, h, dh)[:, :s]
    lse = lse.reshape(b, dilation, h, sub_len).transpose(0, 3, 1, 2).reshape(b, s_pad, h)[:, :s]
    return out, lse


def rotary(t, positions):
    half = t.shape[-1] // 2
    freq = jnp.power(ROPE_BASE, -jnp.arange(half, dtype=jnp.float32) / half)
    ang = positions.astype(jnp.float32)[:, :, None, None] * freq
    cos, sin = jnp.cos(ang), jnp.sin(ang)
    t1, t2 = t[..., :half], t[..., half:]
    return jnp.concatenate([t1 * cos - t2 * sin, t1 * sin + t2 * cos], axis=-1)


def chunkwise_retention(q, k, v, positions):
    b, s, h, dk = q.shape
    dv = v.shape[-1]
    c = RET_CHUNK
    nc = s // c
    q = rotary(q.astype(jnp.float32), positions)
    k = rotary(k.astype(jnp.float32), positions) * (dk ** -0.5)
    v = v.astype(jnp.float32)
    log_gamma = jnp.log1p(-jnp.exp2(-5.0 - jnp.arange(h, dtype=jnp.float32)))
    idx = jnp.arange(c, dtype=jnp.float32)
    rel = idx[:, None] - idx[None, :]
    intra_decay = jnp.where(rel >= 0, jnp.exp(log_gamma[:, None, None] * jnp.maximum(rel, 0.0)), 0.0)
    query_decay = jnp.exp(log_gamma[:, None] * (idx + 1.0))[:, :, None]
    key_decay = jnp.exp(log_gamma[:, None] * (c - 1.0 - idx))[:, :, None]
    chunk_decay = jnp.exp(log_gamma * c)[:, None, None]

    def chunks(t):
        return t.reshape(b, nc, c, h, t.shape[-1]).transpose(1, 0, 3, 2, 4)

    def step(state, qkv):
        qc, kc, vc = qkv
        inner = jnp.einsum('bhqk,bhkv->bhqv', jnp.einsum('bhqd,bhkd->bhqk', qc, kc) * intra_decay, vc)
        cross = jnp.einsum('bhqd,bhdv->bhqv', qc, state) * query_decay
        state = chunk_decay * state + jnp.einsum('bhkd,bhkv->bhdv', kc, vc * key_decay)
        return state, inner + cross

    state0 = jnp.zeros((b, h, dk, dv), jnp.float32)
    _, out = lax.scan(step, state0, (chunks(q), chunks(k), chunks(v)))
    return out.transpose(1, 0, 3, 2, 4).reshape(b, s, h, dv)


def head_group_norm(y, g, bias):
    b, s, h, dv = y.shape
    mu = y.mean(-1, keepdims=True)
    var = jnp.square(y - mu).mean(-1, keepdims=True)
    y = (y - mu) * lax.rsqrt(var + GN_EPS)
    return y.reshape(b, s, h * dv) * g + bias


def token_mixer(x, positions, w_in, w_attn_proj, w_ret_proj, ret_gn_g, ret_gn_b, w_out):
    b, s, _ = x.shape
    proj = x @ w_in
    offsets = [int(o) for o in np.cumsum(IN_SIZES)[:-1]]
    attn_qkv, ret_q, ret_k, ret_v, ret_g, gate_a, gate_r = jnp.split(proj, offsets, axis=-1)
    attn_qkv = attn_qkv.reshape(b, s, N_GROUPS, 3, ATTN_HEADS, ATTN_HEAD_DIM)
    outs, lses = [], []
    for gi, (window, dilation) in enumerate(ATTN_GROUPS):
        o, l = dilated_window_attention(attn_qkv[:, :, gi, 0], attn_qkv[:, :, gi, 1], attn_qkv[:, :, gi, 2], window, dilation)
        outs.append(o)
        lses.append(l)
    weights = jax.nn.softmax(jnp.stack(lses, 0), axis=0)[..., None]
    attn = (weights * jnp.stack(outs, 0)).sum(0).reshape(b, s, ATTN_WIDTH).astype(x.dtype)
    ret = chunkwise_retention(ret_q.reshape(b, s, RET_HEADS, RET_KEY_DIM), ret_k.reshape(b, s, RET_HEADS, RET_KEY_DIM), ret_v.reshape(b, s, RET_HEADS, RET_VAL_DIM), positions)
    ret = head_group_norm(ret, ret_gn_g, ret_gn_b)
    ret = (jax.nn.silu(ret_g.astype(jnp.float32)) * ret).astype(x.dtype)
    merged = jax.nn.sigmoid(gate_a) * (attn @ w_attn_proj) + jax.nn.sigmoid(gate_r) * (ret @ w_ret_proj)
    return merged @ w_out


def conv_ffn(x, w_up, conv_w, conv_b, w_down):
    h = x @ w_up
    h = lax.conv_general_dilated(h, conv_w[:, None, :].astype(h.dtype), window_strides=(1,), padding=((CONV_WIDTH - 1, 0),), dimension_numbers=('NWC', 'WIO', 'NWC'), feature_group_count=h.shape[-1]) + conv_b
    gate, up = jnp.split(h, 2, axis=-1)
    return (jax.nn.gelu(gate) * up) @ w_down


def _fwd_setup_inputs(seed: int = 0) -> dict:
    key = jax.random.key(seed)
    ks = jax.random.split(key, 20)
    f32 = jnp.float32

    def nrm(k, shape, scale):
        return jax.random.normal(k, shape, f32) * scale

    x = nrm(ks[0], (BATCH, SEQ, D_MODEL), 1.0)
    p = nrm(ks[1], (DEPTH, BATCH, SEQ, PLE_DIM), 1.0)
    offset = jax.random.randint(ks[2], (BATCH, 1), 0, MAX_POS_OFFSET, dtype=jnp.int32)
    positions = offset + jnp.arange(SEQ, dtype=jnp.int32)[None, :]
    w_in = nrm(ks[3], (DEPTH, D_MODEL, N_IN), D_MODEL ** -0.5)
    w_attn_proj = nrm(ks[4], (DEPTH, ATTN_WIDTH, D_MODEL), ATTN_WIDTH ** -0.5)
    w_ret_proj = nrm(ks[5], (DEPTH, RET_V_WIDTH, D_MODEL), RET_V_WIDTH ** -0.5)
    ret_gn_g = 1.0 + nrm(ks[6], (DEPTH, RET_V_WIDTH), 0.02)
    ret_gn_b = nrm(ks[7], (DEPTH, RET_V_WIDTH), 0.02)
    w_out = nrm(ks[8], (DEPTH, D_MODEL, D_MODEL), D_MODEL ** -0.5 * DN_BETA)
    ln1_g = 1.0 + nrm(ks[9], (DEPTH, D_MODEL), 0.02)
    ln1_b = nrm(ks[10], (DEPTH, D_MODEL), 0.02)
    w_up = nrm(ks[11], (DEPTH, D_MODEL, 2 * D_FF), D_MODEL ** -0.5)
    conv_w = nrm(ks[12], (DEPTH, CONV_WIDTH, 2 * D_FF), CONV_WIDTH ** -0.5)
    conv_b = nrm(ks[13], (DEPTH, 2 * D_FF), 0.02)
    w_down = nrm(ks[14], (DEPTH, D_FF, D_MODEL), D_FF ** -0.5 * DN_BETA)
    w_ple_gate = nrm(ks[15], (DEPTH, D_MODEL, D_MODEL), D_MODEL ** -0.5)
    w_ple_proj = nrm(ks[16], (DEPTH, PLE_DIM, D_MODEL), PLE_DIM ** -0.5 * DN_BETA)
    ln2_g = 1.0 + nrm(ks[17], (DEPTH, D_MODEL), 0.02)
    ln2_b = nrm(ks[18], (DEPTH, D_MODEL), 0.02)
    return {'x': x, 'p': p, 'positions': positions, 'w_in': w_in, 'w_attn_proj': w_attn_proj, 'w_ret_proj': w_ret_proj, 'ret_gn_g': ret_gn_g, 'ret_gn_b': ret_gn_b, 'w_out': w_out, 'ln1_g': ln1_g, 'ln1_b': ln1_b, 'w_up': w_up, 'conv_w': conv_w, 'conv_b': conv_b, 'w_down': w_down, 'w_ple_gate': w_ple_gate, 'w_ple_proj': w_ple_proj, 'ln2_g': ln2_g, 'ln2_b': ln2_b}


def _fwd_reference(x, p, positions, w_in, w_attn_proj, w_ret_proj, ret_gn_g, ret_gn_b, w_out, ln1_g, ln1_b, w_up, conv_w, conv_b, w_down, w_ple_gate, w_ple_proj, ln2_g, ln2_b):
    for i in range(DEPTH):
        mix = token_mixer(x, positions, w_in[i], w_attn_proj[i], w_ret_proj[i], ret_gn_g[i], ret_gn_b[i], w_out[i])
        x = layer_norm(DN_ALPHA * x + mix, ln1_g[i], ln1_b[i])
        ple = jax.nn.sigmoid(x @ w_ple_gate[i]) * (p[i] @ w_ple_proj[i])
        x = layer_norm(DN_ALPHA * x + conv_ffn(x, w_up[i], conv_w[i], conv_b[i], w_down[i]) + ple, ln2_g[i], ln2_b[i])
    return x


import jax as _jax
import jax.numpy as _jnp

TWIN_FORMAT = 'train_step'
FWD_PARAMS = ['x', 'p', 'positions', 'w_in', 'w_attn_proj', 'w_ret_proj', 'ret_gn_g', 'ret_gn_b', 'w_out', 'ln1_g', 'ln1_b', 'w_up', 'conv_w', 'conv_b', 'w_down', 'w_ple_gate', 'w_ple_proj', 'ln2_g', 'ln2_b']
TWIN_WEIGHTS = ['w_in', 'w_attn_proj', 'w_ret_proj', 'ret_gn_g', 'ret_gn_b', 'w_out', 'ln1_g', 'ln1_b', 'w_up', 'conv_w', 'conv_b', 'w_down', 'w_ple_gate', 'w_ple_proj', 'ln2_g', 'ln2_b']
TWIN_DIFF_INPUT = 'x'
TWIN_INPUTS = ['x', 'p', 'positions', 'w_in', 'w_attn_proj', 'w_ret_proj', 'ret_gn_g', 'ret_gn_b', 'w_out', 'ln1_g', 'ln1_b', 'w_up', 'conv_w', 'conv_b', 'w_down', 'w_ple_gate', 'w_ple_proj', 'ln2_g', 'ln2_b', 'loss_target', 'm_w_in', 'm_w_attn_proj', 'm_w_ret_proj', 'm_ret_gn_g', 'm_ret_gn_b', 'm_w_out', 'm_ln1_g', 'm_ln1_b', 'm_w_up', 'm_conv_w', 'm_conv_b', 'm_w_down', 'm_w_ple_gate', 'm_w_ple_proj', 'm_ln2_g', 'm_ln2_b', 'v_w_in', 'v_w_attn_proj', 'v_w_ret_proj', 'v_ret_gn_g', 'v_ret_gn_b', 'v_w_out', 'v_ln1_g', 'v_ln1_b', 'v_w_up', 'v_conv_w', 'v_conv_b', 'v_w_down', 'v_w_ple_gate', 'v_w_ple_proj', 'v_ln2_g', 'v_ln2_b']
TWIN_OUTPUTS = ['loss', 'grad_x', 'grad_w_in', 'grad_w_attn_proj', 'grad_w_ret_proj', 'grad_ret_gn_g', 'grad_ret_gn_b', 'grad_w_out', 'grad_ln1_g', 'grad_ln1_b', 'grad_w_up', 'grad_conv_w', 'grad_conv_b', 'grad_w_down', 'grad_w_ple_gate', 'grad_w_ple_proj', 'grad_ln2_g', 'grad_ln2_b', 'delta_w_in', 'delta_w_attn_proj', 'delta_w_ret_proj', 'delta_ret_gn_g', 'delta_ret_gn_b', 'delta_w_out', 'delta_ln1_g', 'delta_ln1_b', 'delta_w_up', 'delta_conv_w', 'delta_conv_b', 'delta_w_down', 'delta_w_ple_gate', 'delta_w_ple_proj', 'delta_ln2_g', 'delta_ln2_b', 'new_m_w_in', 'new_m_w_attn_proj', 'new_m_w_ret_proj', 'new_m_ret_gn_g', 'new_m_ret_gn_b', 'new_m_w_out', 'new_m_ln1_g', 'new_m_ln1_b', 'new_m_w_up', 'new_m_conv_w', 'new_m_conv_b', 'new_m_w_down', 'new_m_w_ple_gate', 'new_m_w_ple_proj', 'new_m_ln2_g', 'new_m_ln2_b', 'new_v_w_in', 'new_v_w_attn_proj', 'new_v_w_ret_proj', 'new_v_ret_gn_g', 'new_v_ret_gn_b', 'new_v_w_out', 'new_v_ln1_g', 'new_v_ln1_b', 'new_v_w_up', 'new_v_conv_w', 'new_v_conv_b', 'new_v_w_down', 'new_v_w_ple_gate', 'new_v_w_ple_proj', 'new_v_ln2_g', 'new_v_ln2_b']
TWIN_LEAF_KINDS = {'loss': 'loss', 'grad_x': 'grad_x', 'grad_w_in': 'grad_w', 'grad_w_attn_proj': 'grad_w', 'grad_w_ret_proj': 'grad_w', 'grad_ret_gn_g': 'grad_w', 'grad_ret_gn_b': 'grad_w', 'grad_w_out': 'grad_w', 'grad_ln1_g': 'grad_w', 'grad_ln1_b': 'grad_w', 'grad_w_up': 'grad_w', 'grad_conv_w': 'grad_w', 'grad_conv_b': 'grad_w', 'grad_w_down': 'grad_w', 'grad_w_ple_gate': 'grad_w', 'grad_w_ple_proj': 'grad_w', 'grad_ln2_g': 'grad_w', 'grad_ln2_b': 'grad_w', 'delta_w_in': 'delta_w', 'delta_w_attn_proj': 'delta_w', 'delta_w_ret_proj': 'delta_w', 'delta_ret_gn_g': 'delta_w', 'delta_ret_gn_b': 'delta_w', 'delta_w_out': 'delta_w', 'delta_ln1_g': 'delta_w', 'delta_ln1_b': 'delta_w', 'delta_w_up': 'delta_w', 'delta_conv_w': 'delta_w', 'delta_conv_b': 'delta_w', 'delta_w_down': 'delta_w', 'delta_w_ple_gate': 'delta_w', 'delta_w_ple_proj': 'delta_w', 'delta_ln2_g': 'delta_w', 'delta_ln2_b': 'delta_w', 'new_m_w_in': 'new_m', 'new_m_w_attn_proj': 'new_m', 'new_m_w_ret_proj': 'new_m', 'new_m_ret_gn_g': 'new_m', 'new_m_ret_gn_b': 'new_m', 'new_m_w_out': 'new_m', 'new_m_ln1_g': 'new_m', 'new_m_ln1_b': 'new_m', 'new_m_w_up': 'new_m', 'new_m_conv_w': 'new_m', 'new_m_conv_b': 'new_m', 'new_m_w_down': 'new_m', 'new_m_w_ple_gate': 'new_m', 'new_m_w_ple_proj': 'new_m', 'new_m_ln2_g': 'new_m', 'new_m_ln2_b': 'new_m', 'new_v_w_in': 'new_v', 'new_v_w_attn_proj': 'new_v', 'new_v_w_ret_proj': 'new_v', 'new_v_ret_gn_g': 'new_v', 'new_v_ret_gn_b': 'new_v', 'new_v_w_out': 'new_v', 'new_v_ln1_g': 'new_v', 'new_v_ln1_b': 'new_v', 'new_v_w_up': 'new_v', 'new_v_conv_w': 'new_v', 'new_v_conv_b': 'new_v', 'new_v_w_down': 'new_v', 'new_v_w_ple_gate': 'new_v', 'new_v_w_ple_proj': 'new_v', 'new_v_ln2_g': 'new_v', 'new_v_ln2_b': 'new_v'}


def _forward(args):
    return _fwd_reference(*[args[k] for k in FWD_PARAMS])


def _output_shape():
    def fwd():
        inp = _fwd_setup_inputs(0)
        return _fwd_reference(*[inp[k] for k in FWD_PARAMS])
    out = _jax.eval_shape(fwd)
    return out.shape, out.dtype

N_MICROBATCH = 1
ADAM_LR = 0.001
ADAM_B1 = 0.9
ADAM_B2 = 0.999
ADAM_EPS = 1e-08
ADAM_WD = 0.01
ADAM_STEP = 10
PER_EXAMPLE_BATCH_AXIS = {'x': 0, 'p': 1, 'positions': 0, 'loss_target': 0}
SHARED_INPUTS = []
_WEIGHT_DTYPES = {'w_in': _jnp.float32, 'w_attn_proj': _jnp.float32, 'w_ret_proj': _jnp.float32, 'ret_gn_g': _jnp.float32, 'ret_gn_b': _jnp.float32, 'w_out': _jnp.float32, 'ln1_g': _jnp.float32, 'ln1_b': _jnp.float32, 'w_up': _jnp.float32, 'conv_w': _jnp.float32, 'conv_b': _jnp.float32, 'w_down': _jnp.float32, 'w_ple_gate': _jnp.float32, 'w_ple_proj': _jnp.float32, 'ln2_g': _jnp.float32, 'ln2_b': _jnp.float32}
MOMENT_SCALE = {'w_in': 1.516977e-02, 'w_attn_proj': 9.403840e-03, 'w_ret_proj': 2.930097e-02, 'ret_gn_g': 2.152723e-02, 'ret_gn_b': 2.676930e-02, 'w_out': 6.149031e-02, 'ln1_g': 1.960249e+00, 'ln1_b': 9.679408e-01, 'w_up': 3.532655e-02, 'conv_w': 3.547580e-02, 'conv_b': 4.635041e-02, 'w_down': 1.153737e-01, 'w_ple_gate': 1.822583e-02, 'w_ple_proj': 9.322949e-02, 'ln2_g': 4.535492e+01, 'ln2_b': 1.577043e+00}


def _to_microbatches(a, axis):
    t = _jnp.moveaxis(a, axis, 0)
    t = t.reshape((N_MICROBATCH, t.shape[0] // N_MICROBATCH) + t.shape[1:])
    return _jnp.moveaxis(t, 1, axis + 1)


def setup_inputs(seed: int = 0) -> dict:
    inp = _fwd_setup_inputs(seed)
    key = _jax.random.fold_in(_jax.random.key(seed), 7919)
    shape, _ = _output_shape()
    out = dict(inp)
    out["loss_target"] = _jax.random.normal(_jax.random.fold_in(key, 0), shape, _jnp.float32)
    for i, name in enumerate(TWIN_WEIGHTS):
        w = inp[name].astype(_jnp.float32)
        if MOMENT_SCALE is None:
            s = _jnp.sqrt(_jnp.mean(_jnp.square(w)) + 1e-30)
        else:
            s = MOMENT_SCALE[name]
        km, kv = _jax.random.split(_jax.random.fold_in(key, i + 1))
        out[name] = w
        out["m_" + name] = s * _jax.random.normal(km, w.shape, _jnp.float32)
        out["v_" + name] = (s * s) * _jax.random.uniform(kv, w.shape, _jnp.float32, 0.5, 1.5)
    if N_MICROBATCH > 1:
        for name, axis in PER_EXAMPLE_BATCH_AXIS.items():
            out[name] = _to_microbatches(out[name], axis)
    return {'x': out['x'], 'p': out['p'], 'positions': out['positions'], 'w_in': out['w_in'], 'w_attn_proj': out['w_attn_proj'], 'w_ret_proj': out['w_ret_proj'], 'ret_gn_g': out['ret_gn_g'], 'ret_gn_b': out['ret_gn_b'], 'w_out': out['w_out'], 'ln1_g': out['ln1_g'], 'ln1_b': out['ln1_b'], 'w_up': out['w_up'], 'conv_w': out['conv_w'], 'conv_b': out['conv_b'], 'w_down': out['w_down'], 'w_ple_gate': out['w_ple_gate'], 'w_ple_proj': out['w_ple_proj'], 'ln2_g': out['ln2_g'], 'ln2_b': out['ln2_b'], 'loss_target': out['loss_target'], 'm_w_in': out['m_w_in'], 'm_w_attn_proj': out['m_w_attn_proj'], 'm_w_ret_proj': out['m_w_ret_proj'], 'm_ret_gn_g': out['m_ret_gn_g'], 'm_ret_gn_b': out['m_ret_gn_b'], 'm_w_out': out['m_w_out'], 'm_ln1_g': out['m_ln1_g'], 'm_ln1_b': out['m_ln1_b'], 'm_w_up': out['m_w_up'], 'm_conv_w': out['m_conv_w'], 'm_conv_b': out['m_conv_b'], 'm_w_down': out['m_w_down'], 'm_w_ple_gate': out['m_w_ple_gate'], 'm_w_ple_proj': out['m_w_ple_proj'], 'm_ln2_g': out['m_ln2_g'], 'm_ln2_b': out['m_ln2_b'], 'v_w_in': out['v_w_in'], 'v_w_attn_proj': out['v_w_attn_proj'], 'v_w_ret_proj': out['v_w_ret_proj'], 'v_ret_gn_g': out['v_ret_gn_g'], 'v_ret_gn_b': out['v_ret_gn_b'], 'v_w_out': out['v_w_out'], 'v_ln1_g': out['v_ln1_g'], 'v_ln1_b': out['v_ln1_b'], 'v_w_up': out['v_w_up'], 'v_conv_w': out['v_conv_w'], 'v_conv_b': out['v_conv_b'], 'v_w_down': out['v_w_down'], 'v_w_ple_gate': out['v_w_ple_gate'], 'v_w_ple_proj': out['v_w_ple_proj'], 'v_ln2_g': out['v_ln2_g'], 'v_ln2_b': out['v_ln2_b']}


def _loss(weights, diff, rest, loss_target):
    with _jax.named_scope("forward"):
        args = {**rest, TWIN_DIFF_INPUT: diff, **{k: w.astype(_WEIGHT_DTYPES[k]) for k, w in weights.items()}}
        y = _forward(args)
    with _jax.named_scope("loss_head"):
        err = _jnp.square(y.astype(_jnp.float32) - loss_target)
        return 0.5 * _jnp.sum(_jnp.mean(err, axis=-1)) if err.ndim else 0.5 * err


def _adamw(w, g, m, v):
    m = ADAM_B1 * m + (1.0 - ADAM_B1) * g
    v = ADAM_B2 * v + (1.0 - ADAM_B2) * _jnp.square(g)
    m_hat = m / (1.0 - ADAM_B1 ** ADAM_STEP)
    v_hat = v / (1.0 - ADAM_B2 ** ADAM_STEP)
    delta = -ADAM_LR * (m_hat / (_jnp.sqrt(v_hat) + ADAM_EPS) + ADAM_WD * w)
    return delta, m, v


def reference(x, p, positions, w_in, w_attn_proj, w_ret_proj, ret_gn_g, ret_gn_b, w_out, ln1_g, ln1_b, w_up, conv_w, conv_b, w_down, w_ple_gate, w_ple_proj, ln2_g, ln2_b, loss_target, m_w_in, m_w_attn_proj, m_w_ret_proj, m_ret_gn_g, m_ret_gn_b, m_w_out, m_ln1_g, m_ln1_b, m_w_up, m_conv_w, m_conv_b, m_w_down, m_w_ple_gate, m_w_ple_proj, m_ln2_g, m_ln2_b, v_w_in, v_w_attn_proj, v_w_ret_proj, v_ret_gn_g, v_ret_gn_b, v_w_out, v_ln1_g, v_ln1_b, v_w_up, v_conv_w, v_conv_b, v_w_down, v_w_ple_gate, v_w_ple_proj, v_ln2_g, v_ln2_b):
    given = dict(x=x, p=p, positions=positions, w_in=w_in, w_attn_proj=w_attn_proj, w_ret_proj=w_ret_proj, ret_gn_g=ret_gn_g, ret_gn_b=ret_gn_b, w_out=w_out, ln1_g=ln1_g, ln1_b=ln1_b, w_up=w_up, conv_w=conv_w, conv_b=conv_b, w_down=w_down, w_ple_gate=w_ple_gate, w_ple_proj=w_ple_proj, ln2_g=ln2_g, ln2_b=ln2_b, loss_target=loss_target, m_w_in=m_w_in, m_w_attn_proj=m_w_attn_proj, m_w_ret_proj=m_w_ret_proj, m_ret_gn_g=m_ret_gn_g, m_ret_gn_b=m_ret_gn_b, m_w_out=m_w_out, m_ln1_g=m_ln1_g, m_ln1_b=m_ln1_b, m_w_up=m_w_up, m_conv_w=m_conv_w, m_conv_b=m_conv_b, m_w_down=m_w_down, m_w_ple_gate=m_w_ple_gate, m_w_ple_proj=m_w_ple_proj, m_ln2_g=m_ln2_g, m_ln2_b=m_ln2_b, v_w_in=v_w_in, v_w_attn_proj=v_w_attn_proj, v_w_ret_proj=v_w_ret_proj, v_ret_gn_g=v_ret_gn_g, v_ret_gn_b=v_ret_gn_b, v_w_out=v_w_out, v_ln1_g=v_ln1_g, v_ln1_b=v_ln1_b, v_w_up=v_w_up, v_conv_w=v_conv_w, v_conv_b=v_conv_b, v_w_down=v_w_down, v_w_ple_gate=v_w_ple_gate, v_w_ple_proj=v_w_ple_proj, v_ln2_g=v_ln2_g, v_ln2_b=v_ln2_b)
    weights = {n: given[n] for n in TWIN_WEIGHTS}
    shared = {n: given[n] for n in SHARED_INPUTS}
    per_example = {n: given[n] for n in ['x', 'p', 'positions']}
    grad_fn = _jax.value_and_grad(_loss, argnums=(0, 1))

    def one_microbatch(ex, loss_target):
        ex = dict(ex)
        diff = ex.pop(TWIN_DIFF_INPUT)
        return grad_fn(weights, diff, {**shared, **ex}, loss_target)

    if N_MICROBATCH == 1:
        loss, (grad_w, grad_x) = one_microbatch(per_example, given["loss_target"])
    else:
        def body(carry, xs):
            loss_sum, grad_sum = carry
            l_k, (gw_k, gx_k) = one_microbatch(xs[0], xs[1])
            with _jax.named_scope("update"):
                return (loss_sum + l_k, _jax.tree.map(_jnp.add, grad_sum, gw_k)), gx_k

        init = (_jnp.zeros((), _jnp.float32), _jax.tree.map(_jnp.zeros_like, weights))
        (loss, grad_w), grad_x = _jax.lax.scan(body, init, (per_example, given["loss_target"]))
    with _jax.named_scope("update"):
        delta_w, new_m, new_v = {}, {}, {}
        for n in TWIN_WEIGHTS:
            delta_w[n], new_m[n], new_v[n] = _adamw(weights[n], grad_w[n], given["m_" + n], given["v_" + n])
    return (loss, grad_x, *[grad_w[n] for n in TWIN_WEIGHTS], *[delta_w[n] for n in TWIN_WEIGHTS],
            *[new_m[n] for n in TWIN_WEIGHTS], *[new_v[n] for n in TWIN_WEIGHTS])
```

```python
import functools
import math

import numpy as np
import jax
import jax.numpy as jnp
from jax import lax
from jax.experimental import pallas as pl
from jax.experimental.pallas import tpu as pltpu

F32, BF16 = jnp.float32, jnp.bfloat16

D = 1024
DEPTH = 2
N_DEV = 8
HD = 128
NH = 8
DILATIONS = (1, 4, 16)
SPAN = 128
RH, RDK, RDV = 4, 256, 512
CH = 128
DFF = 2816
PLE = 256
N_IN = 17408
OFF_RQ, OFF_RK, OFF_RV, OFF_RG, OFF_GA, OFF_GR = 9216, 10240, 11264, 13312, 15360, 16384
ALPHA = (2 * DEPTH) ** 0.25
LN_EPS, GN_EPS = 1e-5, 1e-6
ROPE_BASE = 10000.0
LR, B1, B2, EPS, WD, STEP = 0.001, 0.9, 0.999, 1e-8, 0.01, 10
VMEM_LIMIT = 48 * 1024 * 1024
NEG = -1e30

BIG = ("w_in", "w_attn_proj", "w_ret_proj", "w_out", "w_up", "w_down", "w_ple_gate", "w_ple_proj")
COL_SHARDED = ("w_in", "w_up", "w_ple_proj")
SMALL = ("ret_gn_g", "ret_gn_b", "ln1_g", "ln1_b", "conv_b", "ln2_g", "ln2_b")
WEIGHTS = ("w_in", "w_attn_proj", "w_ret_proj", "ret_gn_g", "ret_gn_b", "w_out", "ln1_g", "ln1_b", "w_up",
           "conv_w", "conv_b", "w_down", "w_ple_gate", "w_ple_proj", "ln2_g", "ln2_b")


def _tile(n, cap, mult=128):
    if n <= cap:
        return n
    t = (cap // mult) * mult
    while n % t:
        t -= mult
    return t


def _cparams(sem):
    return pltpu.CompilerParams(dimension_semantics=sem, vmem_limit_bytes=VMEM_LIMIT)


def _dot(a, b, ca, cb):
    return lax.dot_general(a, b, (((ca,), (cb,)), ((), ())), preferred_element_type=F32)


def _mm(a, b, *, name, ta=False, tb=False, out_dtype=F32, add=None, add_scale=1.0, tm=1024, tn=1024, tk=1024):
    M, K = (a.shape[1], a.shape[0]) if ta else a.shape
    N = b.shape[0] if tb else b.shape[1]
    assert K == (b.shape[1] if tb else b.shape[0])
    tm, tn, tk = _tile(M, tm), _tile(N, tn), _tile(K, tk)
    nk = K // tk

    def body(*refs):
        if add is None:
            a_ref, b_ref, o_ref, acc_ref = refs
        else:
            a_ref, b_ref, add_ref, o_ref, acc_ref = refs
        k = pl.program_id(2)

        @pl.when(k == 0)
        def _():
            acc_ref[...] = jnp.zeros_like(acc_ref)

        acc_ref[...] += _dot(a_ref[...].astype(BF16), b_ref[...].astype(BF16), 0 if ta else 1, 1 if tb else 0)

        @pl.when(k == nk - 1)
        def _():
            r = acc_ref[...]
            if add is not None:
                r = r + add_scale * add_ref[...].astype(F32)
            o_ref[...] = r.astype(out_dtype)

    a_spec = pl.BlockSpec((tk, tm), lambda i, j, k: (k, i)) if ta else pl.BlockSpec((tm, tk), lambda i, j, k: (i, k))
    b_spec = pl.BlockSpec((tn, tk), lambda i, j, k: (j, k)) if tb else pl.BlockSpec((tk, tn), lambda i, j, k: (k, j))
    o_spec = pl.BlockSpec((tm, tn), lambda i, j, k: (i, j))
    in_specs, args = [a_spec, b_spec], [a, b]
    if add is not None:
        in_specs.append(o_spec)
        args.append(add)
    return pl.pallas_call(
        body, name=name, grid=(M // tm, N // tn, nk), in_specs=in_specs, out_specs=o_spec,
        out_shape=jax.ShapeDtypeStruct((M, N), out_dtype), scratch_shapes=[pltpu.VMEM((tm, tn), F32)],
        compiler_params=_cparams(("parallel", "parallel", "arbitrary")),
    )(*args)


def _rowwise(fn, rows, pars, outs, accs, *, name, tm=256):
    first = rows[0][0] if isinstance(rows[0], tuple) else rows[0]
    S = first.shape[0]
    tm = min(tm, S)
    n_r, n_p, n_o = len(rows), len(pars), len(outs)

    def body(*refs):
        i = pl.program_id(0)
        vals = [r[...] for r in refs[:n_r + n_p]]
        res = fn(*vals)
        if not isinstance(res, (tuple, list)):
            res = (res,)
        o_refs = refs[n_r + n_p:n_r + n_p + n_o]
        a_refs = refs[n_r + n_p + n_o:]
        for r, v in zip(o_refs, res[:n_o]):
            r[...] = v.astype(r.dtype)
        if a_refs:
            @pl.when(i == 0)
            def _():
                for r in a_refs:
                    r[...] = jnp.zeros_like(r)

            for r, v in zip(a_refs, res[n_o:]):
                r[...] += v

    in_specs, args = [], []
    for r in rows:
        if isinstance(r, tuple):
            arr, w, cb = r
            in_specs.append(pl.BlockSpec((tm, w), lambda i, cb=cb: (i, cb)))
        else:
            arr = r
            in_specs.append(pl.BlockSpec((tm, arr.shape[1]), lambda i: (i, 0)))
        args.append(arr)
    for p_ in pars:
        in_specs.append(pl.BlockSpec(p_.shape, lambda i: (0, 0)))
        args.append(p_)
    out_shape = [jax.ShapeDtypeStruct((S, w), dt) for w, dt in outs] + [jax.ShapeDtypeStruct(a, F32) for a in accs]
    out_specs = [pl.BlockSpec((tm, w), lambda i: (i, 0)) for w, _ in outs] + [pl.BlockSpec(a, lambda i: (0, 0)) for a in accs]
    return pl.pallas_call(
        body, name=name, grid=(S // tm,), in_specs=in_specs, out_specs=out_specs, out_shape=out_shape,
        compiler_params=_cparams(("arbitrary",)),
    )(*args)


def _norm(h, eps):
    mu = jnp.mean(h, -1, keepdims=True)
    d = h - mu
    rstd = lax.rsqrt(jnp.mean(d * d, -1, keepdims=True) + eps)
    return d * rstd, rstd


def _norm_bwd(dxh, xh, rstd):
    return rstd * (dxh - jnp.mean(dxh, -1, keepdims=True) - xh * jnp.mean(dxh * xh, -1, keepdims=True))


def _sig(x):
    return 1.0 / (1.0 + jnp.exp(-x))


_GELU_C = math.sqrt(2.0 / math.pi)


def _gelu(x):
    t = jnp.tanh(_GELU_C * (x + 0.044715 * x * x * x))
    return 0.5 * x * (1.0 + t), t


def _gelu_grad(x, t):
    return 0.5 * (1.0 + t) + 0.5 * x * (1.0 - t * t) * _GELU_C * (1.0 + 3 * 0.044715 * x * x)


def _f_ln1(x, mix, g, b):
    h = ALPHA * x + mix
    xh, _ = _norm(h, LN_EPS)
    y = xh * g + b
    return h, y, y


def _f_ln2(x, ffn, z, pp, g, b):
    h = ALPHA * x + ffn + _sig(z) * pp
    xh, _ = _norm(h, LN_EPS)
    y = xh * g + b
    return h, y, y


def _f_ln_bwd(dy, h, g):
    xh, rstd = _norm(h, LN_EPS)
    dh = _norm_bwd(dy * g, xh, rstd)
    return dh, dh, jnp.sum(dy * xh, 0, keepdims=True), jnp.sum(dy, 0, keepdims=True)


def _f_loss(y, t):
    e = y - t
    return e * (1.0 / D), jnp.sum(e * e, 0, keepdims=True) * (0.5 / D)


def _f_combine(o0, o1, o2, l0, l1, l2):
    m = jnp.maximum(jnp.maximum(l0, l1), l2)
    e0, e1, e2 = jnp.exp(l0 - m), jnp.exp(l1 - m), jnp.exp(l2 - m)
    den = e0 + e1 + e2
    attn = (e0 * o0.astype(F32) + e1 * o1.astype(F32) + e2 * o2.astype(F32)) / den
    return attn, m + jnp.log(den)


def _f_delta(da, a):
    pr = da.astype(F32) * a.astype(F32)
    parts = []
    for h in range(NH):
        s = jnp.sum(pr[:, h * HD:(h + 1) * HD], -1, keepdims=True)
        parts.append(jnp.broadcast_to(s, (pr.shape[0], HD)))
    return jnp.concatenate(parts, axis=1)


def _f_gate(ap, rp, ga, gr):
    return _sig(ga.astype(F32)) * ap.astype(F32) + _sig(gr.astype(F32)) * rp.astype(F32)


def _f_gate_bwd(dm, ap, rp, ga, gr):
    dm = dm.astype(F32)
    sa, sr = _sig(ga.astype(F32)), _sig(gr.astype(F32))
    return dm * sa, dm * sr, dm * ap.astype(F32) * sa * (1.0 - sa), dm * rp.astype(F32) * sr * (1.0 - sr)


def _f_gn(y, rg0, rg1, rg2, rg3, g, b):
    y, rg = y.astype(F32), jnp.concatenate([rg0, rg1, rg2, rg3], axis=1).astype(F32)
    parts = []
    for h in range(RH):
        sl = slice(h * RDV, (h + 1) * RDV)
        xh, _ = _norm(y[:, sl], GN_EPS)
        parts.append(xh * g[:, sl] + b[:, sl])
    return rg * _sig(rg) * jnp.concatenate(parts, axis=1)


def _f_gn_bwd(dr, y, rg0, rg1, rg2, rg3, g, b):
    dr, y, rg = dr.astype(F32), y.astype(F32), jnp.concatenate([rg0, rg1, rg2, rg3], axis=1).astype(F32)
    s = _sig(rg)
    d_out = dr * rg * s
    dys, outs, xhs = [], [], []
    for h in range(RH):
        sl = slice(h * RDV, (h + 1) * RDV)
        xh, rstd = _norm(y[:, sl], GN_EPS)
        xhs.append(xh)
        outs.append(xh * g[:, sl] + b[:, sl])
        dys.append(_norm_bwd(d_out[:, sl] * g[:, sl], xh, rstd))
    xh, out = jnp.concatenate(xhs, axis=1), jnp.concatenate(outs, axis=1)
    d_rg = dr * out * s * (1.0 + rg * (1.0 - s))
    return jnp.concatenate(dys, axis=1), d_rg, jnp.sum(d_out * xh, 0, keepdims=True), jnp.sum(d_out, 0, keepdims=True)


def _f_ple_bwd(dh, z, pp):
    s = _sig(z)
    return dh * s, dh * pp * s * (1.0 - s)


def _band_masks(prev_valid=None):
    row = lax.broadcasted_iota(jnp.int32, (SPAN, SPAN), 0)
    col = lax.broadcasted_iota(jnp.int32, (SPAN, SPAN), 1)
    if prev_valid is None:
        return col <= row, col >= row
    return col <= row, col >= row + jnp.where(prev_valid, 0, 2 * SPAN)


def _attn_fwd(proj, g, dil):
    S = proj.shape[0]
    Sd = S // dil
    T = min(512, Sd)
    nsub, nib = T // SPAN, Sd // T
    scale = HD ** -0.5
    ncb = N_IN // HD

    def body(q_ref, k_ref, v_ref, kp_ref, vp_ref, o_ref, l_ref):
        ib = pl.program_id(2)
        m_cur, m_prev = _band_masks()
        for j in range(nsub):
            rows = slice(j * SPAN, (j + 1) * SPAN)
            q = q_ref[rows, :]
            kc, vc = k_ref[rows, :], v_ref[rows, :]
            if j == 0:
                kp, vp = kp_ref[...], vp_ref[...]
                mp = _band_masks(ib > 0)[1]
            else:
                prow = slice((j - 1) * SPAN, j * SPAN)
                kp, vp = k_ref[prow, :], v_ref[prow, :]
                mp = m_prev
            sc = jnp.where(m_cur, _dot(q, kc, 1, 1) * scale, NEG)
            sp = jnp.where(mp, _dot(q, kp, 1, 1) * scale, NEG)
            m = jnp.maximum(jnp.max(sc, -1, keepdims=True), jnp.max(sp, -1, keepdims=True))
            pc, pp = jnp.exp(sc - m), jnp.exp(sp - m)
            den = jnp.sum(pc, -1, keepdims=True) + jnp.sum(pp, -1, keepdims=True)
            o = (_dot(pc.astype(BF16), vc, 1, 0) + _dot(pp.astype(BF16), vp, 1, 0)) / den
            o_ref[rows, :] = o.astype(BF16)
            l_ref[rows, :] = jnp.broadcast_to(m + jnp.log(den), (SPAN, HD))

    def cur(t):
        return pl.BlockSpec((T, HD), lambda r, h, ib: (ib, r * ncb + g * 3 * NH + t * NH + h))

    def prev(t):
        return pl.BlockSpec((SPAN, HD), lambda r, h, ib: (jnp.maximum(ib * nsub - 1, 0), r * ncb + g * 3 * NH + t * NH + h))

    out_spec = pl.BlockSpec((T, HD), lambda r, h, ib: (ib, r * NH + h))
    pv = proj.reshape(Sd, dil * N_IN)
    o, l = pl.pallas_call(
        body, name=f"attn_fwd_g{g}", grid=(dil, NH, nib),
        in_specs=[cur(0), cur(1), cur(2), prev(1), prev(2)], out_specs=[out_spec, out_spec],
        out_shape=[jax.ShapeDtypeStruct((Sd, dil * D), BF16), jax.ShapeDtypeStruct((Sd, dil * D), F32)],
        compiler_params=_cparams(("parallel", "parallel", "parallel")),
    )(pv, pv, pv, pv, pv)
    return o.reshape(S, D), l.reshape(S, D)


def _attn_bwd(proj, d_attn, lse, delta, g, dil):
    S = proj.shape[0]
    Sd = S // dil
    T = min(512, Sd)
    nsub, nib = T // SPAN, Sd // T
    scale = HD ** -0.5
    ncb = N_IN // HD

    def body(q_ref, k_ref, v_ref, do_ref, l_ref, dl_ref, kp_ref, vp_ref, qn_ref, don_ref, ln_ref, dln_ref,
             dq_ref, dk_ref, dv_ref, dk_acc, dv_acc):
        ib = pl.program_id(2)
        m_cur, m_prev = _band_masks()
        dk_acc[...] = jnp.zeros_like(dk_acc)
        dv_acc[...] = jnp.zeros_like(dv_acc)
        for jq in range(nsub + 1):
            rows = slice(jq * SPAN, (jq + 1) * SPAN)
            prow = slice((jq - 1) * SPAN, jq * SPAN)
            if jq < nsub:
                q, do, l, dl = q_ref[rows, :], do_ref[rows, :], l_ref[rows, :], dl_ref[rows, :]
            else:
                q, do, l, dl = qn_ref[...], don_ref[...], ln_ref[...], dln_ref[...]
            if jq == 0:
                kp, vp = kp_ref[...], vp_ref[...]
                mp = _band_masks(ib > 0)[1]
            else:
                kp, vp = k_ref[prow, :], v_ref[prow, :]
                mp = m_prev if jq < nsub else _band_masks(ib < nib - 1)[1]
            p = jnp.where(mp, jnp.exp(_dot(q, kp, 1, 1) * scale - l), 0.0)
            ds = (p * (_dot(do, vp, 1, 1) - dl)).astype(BF16)
            if jq > 0:
                dk_acc[prow, :] += _dot(ds, q, 0, 0)
                dv_acc[prow, :] += _dot(p.astype(BF16), do, 0, 0)
            if jq < nsub:
                dq = _dot(ds, kp, 1, 0)
                kc, vc = k_ref[rows, :], v_ref[rows, :]
                p = jnp.where(m_cur, jnp.exp(_dot(q, kc, 1, 1) * scale - l), 0.0)
                ds = (p * (_dot(do, vc, 1, 1) - dl)).astype(BF16)
                dq = dq + _dot(ds, kc, 1, 0)
                dk_acc[rows, :] += _dot(ds, q, 0, 0)
                dv_acc[rows, :] += _dot(p.astype(BF16), do, 0, 0)
                dq_ref[rows, :] = (dq * scale).astype(BF16)
        dk_ref[...] = (dk_acc[...] * scale).astype(BF16)
        dv_ref[...] = dv_acc[...].astype(BF16)

    def pcol(t):
        return lambda r, h: r * ncb + g * 3 * NH + t * NH + h

    def cur(colf):
        return pl.BlockSpec((T, HD), lambda r, h, ib: (ib, colf(r, h)))

    def prev(colf):
        return pl.BlockSpec((SPAN, HD), lambda r, h, ib: (jnp.maximum(ib * nsub - 1, 0), colf(r, h)))

    def nxt(colf):
        return pl.BlockSpec((SPAN, HD), lambda r, h, ib: (jnp.minimum((ib + 1) * nsub, Sd // SPAN - 1), colf(r, h)))

    ocol = lambda r, h: r * NH + h
    pv = proj.reshape(Sd, dil * N_IN)
    dov, lv, dlv = (t.reshape(Sd, dil * D) for t in (d_attn, lse, delta))
    outs = pl.pallas_call(
        body, name=f"attn_bwd_g{g}", grid=(dil, NH, nib),
        in_specs=[cur(pcol(0)), cur(pcol(1)), cur(pcol(2)), cur(ocol), cur(ocol), cur(ocol), prev(pcol(1)), prev(pcol(2)),
                  nxt(pcol(0)), nxt(ocol), nxt(ocol), nxt(ocol)],
        out_specs=[cur(ocol)] * 3, out_shape=[jax.ShapeDtypeStruct((Sd, dil * D), BF16)] * 3,
        scratch_shapes=[pltpu.VMEM((T, HD), F32), pltpu.VMEM((T, HD), F32)],
        compiler_params=_cparams(("parallel", "parallel", "parallel")),
    )(pv, pv, pv, dov, lv, dlv, pv, pv, pv, dov, lv, dlv)
    return [t.reshape(S, D) for t in outs]


def _ret_consts():
    lg = np.log1p(-np.exp2(-5.0 - np.arange(RH, dtype=np.float64)))
    idx = np.arange(CH, dtype=np.float64)
    rel = idx[:, None] - idx[None, :]
    intra = np.where(rel >= 0, np.exp(lg[:, None, None] * np.maximum(rel, 0.0)), 0.0)
    qd = np.exp(lg[:, None] * (idx + 1.0))
    kd = np.exp(lg[:, None] * (CH - 1.0 - idx))
    cd = np.exp(lg * CH)
    wide = lambda t: np.broadcast_to(t[:, :, None], (RH, t.shape[1], RDV))
    return (jnp.asarray(intra, F32), jnp.asarray(wide(qd), F32), jnp.asarray(wide(kd), F32),
            jnp.asarray(np.broadcast_to(cd[:, None, None], (RH, 1, RDV)), F32))


def _rot(t, c, s):
    t1, t2 = t[:, :RDK // 2], t[:, RDK // 2:]
    return jnp.concatenate([t1 * c - t2 * s, t1 * s + t2 * c], axis=1)


def _unrot(d, c, s):
    d1, d2 = d[:, :RDK // 2], d[:, RDK // 2:]
    return jnp.concatenate([d1 * c + d2 * s, d2 * c - d1 * s], axis=1)


def _ret_specs(nmap):
    def blk(w, c0):
        return pl.BlockSpec((CH, w), lambda h, n: (nmap(n), c0 // w + h))
    q, k, v = blk(RDK, OFF_RQ), blk(RDK, OFF_RK), blk(RDV, OFF_RV)
    cs = pl.BlockSpec((CH, RDK // 2), lambda h, n: (nmap(n), 0))
    dmat = pl.BlockSpec((None, CH, CH), lambda h, n: (h, 0, 0))
    dvec = pl.BlockSpec((None, CH, RDV), lambda h, n: (h, 0, 0))
    cdv = pl.BlockSpec((None, 1, RDV), lambda h, n: (h, 0, 0))
    state = pl.BlockSpec((None, None, RDK, RDV), lambda h, n: (h, nmap(n), 0, 0))
    out = pl.BlockSpec((CH, RDV), lambda h, n: (nmap(n), h))
    return [q, k, v, cs, cs, dmat, dvec, dvec, cdv], state, out


def _ret_fwd(proj, cos, sin, consts):
    S = proj.shape[0]
    nc = S // CH

    def body(q_ref, k_ref, v_ref, c_ref, s_ref, d_ref, qd_ref, kd_ref, cd_ref, o_ref, st_ref, state):
        @pl.when(pl.program_id(1) == 0)
        def _():
            state[...] = jnp.zeros_like(state)

        c, s = c_ref[...], s_ref[...]
        qb = _rot(q_ref[...].astype(F32), c, s).astype(BF16)
        kb = (_rot(k_ref[...].astype(F32), c, s) * (RDK ** -0.5)).astype(BF16)
        vb = v_ref[...]
        sb = state[...].astype(BF16)
        st_ref[...] = sb
        a = (_dot(qb, kb, 1, 1) * d_ref[...]).astype(BF16)
        o_ref[...] = (_dot(a, vb, 1, 0) + _dot(qb, sb, 1, 0) * qd_ref[...]).astype(BF16)
        vk = (vb.astype(F32) * kd_ref[...]).astype(BF16)
        state[...] = cd_ref[...] * state[...] + _dot(kb, vk, 0, 0)

    ins, state_spec, out_spec = _ret_specs(lambda n: n)
    return pl.pallas_call(
        body, name="ret_fwd", grid=(RH, nc), in_specs=ins, out_specs=[out_spec, state_spec],
        out_shape=[jax.ShapeDtypeStruct((S, RH * RDV), BF16), jax.ShapeDtypeStruct((RH, nc, RDK, RDV), BF16)],
        scratch_shapes=[pltpu.VMEM((RDK, RDV), F32)],
        compiler_params=_cparams(("parallel", "arbitrary")),
    )(proj, proj, proj, cos, sin, *consts)


def _ret_bwd(proj, cos, sin, consts, states, d_ret):
    S = proj.shape[0]
    nc = S // CH

    def body(q_ref, k_ref, v_ref, c_ref, s_ref, d_ref, qd_ref, kd_ref, cd_ref, st_ref, do_ref,
             dq_ref, dk_ref, dv_ref, dstate):
        @pl.when(pl.program_id(1) == 0)
        def _():
            dstate[...] = jnp.zeros_like(dstate)

        c, s = c_ref[...], s_ref[...]
        qb = _rot(q_ref[...].astype(F32), c, s).astype(BF16)
        kb = (_rot(k_ref[...].astype(F32), c, s) * (RDK ** -0.5)).astype(BF16)
        vb, sb, do = v_ref[...], st_ref[...], do_ref[...]
        dmat, qd, kd = d_ref[...], qd_ref[...], kd_ref[...]
        a = (_dot(qb, kb, 1, 1) * dmat).astype(BF16)
        doq = (do.astype(F32) * qd).astype(BF16)
        dsb = dstate[...].astype(BF16)
        vk = (vb.astype(F32) * kd).astype(BF16)
        dv_ref[...] = (_dot(a, do, 0, 0) + _dot(kb, dsb, 1, 0) * kd).astype(BF16)
        da = (_dot(do, vb, 1, 1) * dmat).astype(BF16)
        dq = _dot(da, kb, 1, 0) + _dot(doq, sb, 1, 1)
        dk = (_dot(da, qb, 0, 0) + _dot(vk, dsb, 1, 1)) * (RDK ** -0.5)
        dq_ref[...] = _unrot(dq, c, s).astype(BF16)
        dk_ref[...] = _unrot(dk, c, s).astype(BF16)
        dstate[...] = cd_ref[...] * dstate[...] + _dot(qb, doq, 0, 0)

    rev = lambda n: nc - 1 - n
    ins, state_spec, out_spec = _ret_specs(rev)
    qk_out = pl.BlockSpec((CH, RDK), lambda h, n: (rev(n), h))
    return pl.pallas_call(
        body, name="ret_bwd", grid=(RH, nc), in_specs=ins + [state_spec, out_spec],
        out_specs=[qk_out, qk_out, out_spec],
        out_shape=[jax.ShapeDtypeStruct((S, RH * RDK), BF16)] * 2 + [jax.ShapeDtypeStruct((S, RH * RDV), BF16)],
        scratch_shapes=[pltpu.VMEM((RDK, RDV), F32)],
        compiler_params=_cparams(("parallel", "arbitrary")),
    )(proj, proj, proj, cos, sin, *consts, states, d_ret)


CW = 256
HALO = 16


def _shift_down(v, halo, k):
    rolled = pltpu.roll(v, k, 0)
    hr = pltpu.roll(halo, k, 0)[0:8]
    row = lax.broadcasted_iota(jnp.int32, hr.shape, 0)
    return jnp.concatenate([jnp.where(row < k, hr, rolled[0:8]), rolled[8:]], axis=0)


def _shift_up(v, halo, k):
    T = v.shape[0]
    rolled = pltpu.roll(v, T - k, 0)
    hr = pltpu.roll(halo, 8 - k, 0)[0:8]
    row = lax.broadcasted_iota(jnp.int32, hr.shape, 0)
    return jnp.concatenate([rolled[:T - 8], jnp.where(row >= 8 - k, hr, rolled[T - 8:])], axis=0)


def _conv_taps(h_ref, hp_ref, first):
    h = h_ref[...].astype(F32)
    hp = hp_ref[...].astype(F32) * jnp.where(first, 0.0, 1.0)
    return _shift_down(h, hp, 2), _shift_down(h, hp, 1), h


def _conv_specs(S, T):
    nj = DFF // CW
    cur = pl.BlockSpec((T, CW), lambda j, i: (i, j))
    prev = pl.BlockSpec((HALO, CW), lambda j, i: (jnp.maximum(i * (T // HALO) - 1, 0), j))
    nxt = pl.BlockSpec((HALO, CW), lambda j, i: (jnp.minimum((i + 1) * (T // HALO), S // HALO - 1), j))
    w = pl.BlockSpec((3, CW), lambda j, i: (0, j))
    b = pl.BlockSpec((1, CW), lambda j, i: (0, j))
    return nj, cur, prev, nxt, w, b


def _conv_fwd(hg, hu, wg, wu, bg, bu):
    S = hg.shape[0]
    T = min(1024, S)
    nj, cur, prev, _, w, b = _conv_specs(S, T)

    def body(hg_ref, hu_ref, hgp_ref, hup_ref, wg_ref, wu_ref, bg_ref, bu_ref, o_ref):
        first = pl.program_id(1) == 0
        g2, g1, g0 = _conv_taps(hg_ref, hgp_ref, first)
        u2, u1, u0 = _conv_taps(hu_ref, hup_ref, first)
        cg = wg_ref[0:1, :] * g2 + wg_ref[1:2, :] * g1 + wg_ref[2:3, :] * g0 + bg_ref[...]
        cu = wu_ref[0:1, :] * u2 + wu_ref[1:2, :] * u1 + wu_ref[2:3, :] * u0 + bu_ref[...]
        o_ref[...] = (_gelu(cg)[0] * cu).astype(BF16)

    return pl.pallas_call(
        body, name="conv_fwd", grid=(nj, S // T), in_specs=[cur, cur, prev, prev, w, w, b, b], out_specs=cur,
        out_shape=jax.ShapeDtypeStruct((S, DFF), BF16), compiler_params=_cparams(("parallel", "parallel")),
    )(hg, hu, hg, hu, wg, wu, bg, bu)


def _conv_bwd_pre(d_act, hg, hu, wg, wu, bg, bu):
    S = hg.shape[0]
    T = min(1024, S)
    nj, cur, prev, _, w, b = _conv_specs(S, T)

    def body(da_ref, hg_ref, hu_ref, hgp_ref, hup_ref, wg_ref, wu_ref, bg_ref, bu_ref,
             dcg_ref, dcu_ref, gwg_ref, gwu_ref, gbg_ref, gbu_ref):
        first = pl.program_id(1) == 0
        g2, g1, g0 = _conv_taps(hg_ref, hgp_ref, first)
        u2, u1, u0 = _conv_taps(hu_ref, hup_ref, first)
        cg = wg_ref[0:1, :] * g2 + wg_ref[1:2, :] * g1 + wg_ref[2:3, :] * g0 + bg_ref[...]
        cu = wu_ref[0:1, :] * u2 + wu_ref[1:2, :] * u1 + wu_ref[2:3, :] * u0 + bu_ref[...]
        da = da_ref[...].astype(F32)
        gl, t = _gelu(cg)
        dcg = da * cu * _gelu_grad(cg, t)
        dcu = da * gl
        dcg_ref[...] = dcg.astype(BF16)
        dcu_ref[...] = dcu.astype(BF16)

        @pl.when(first)
        def _():
            for r in (gwg_ref, gwu_ref, gbg_ref, gbu_ref):
                r[...] = jnp.zeros_like(r)

        for r, d, taps in ((gwg_ref, dcg, (g2, g1, g0)), (gwu_ref, dcu, (u2, u1, u0))):
            for j in range(3):
                r[j:j + 1, :] += jnp.sum(d * taps[j], 0, keepdims=True)
        gbg_ref[...] += jnp.sum(dcg, 0, keepdims=True)
        gbu_ref[...] += jnp.sum(dcu, 0, keepdims=True)

    return pl.pallas_call(
        body, name="conv_bwd_pre", grid=(nj, S // T), in_specs=[cur, cur, cur, prev, prev, w, w, b, b],
        out_specs=[cur, cur, w, w, b, b],
        out_shape=[jax.ShapeDtypeStruct((S, DFF), BF16)] * 2 + [jax.ShapeDtypeStruct((3, DFF), F32)] * 2
        + [jax.ShapeDtypeStruct((1, DFF), F32)] * 2,
        compiler_params=_cparams(("parallel", "arbitrary")),
    )(d_act, hg, hu, hg, hu, wg, wu, bg, bu)


def _conv_bwd_in(dc, w, name):
    S = dc.shape[0]
    T = min(1024, S)
    nj, cur, _, nxt, wspec, _ = _conv_specs(S, T)
    nt = S // T

    def body(dc_ref, dn_ref, w_ref, o_ref):
        d = dc_ref[...].astype(F32)
        dn = dn_ref[...].astype(F32) * jnp.where(pl.program_id(1) == nt - 1, 0.0, 1.0)
        o_ref[...] = (w_ref[2:3, :] * d + w_ref[1:2, :] * _shift_up(d, dn, 1) + w_ref[0:1, :] * _shift_up(d, dn, 2)).astype(BF16)

    return pl.pallas_call(
        body, name=name, grid=(nj, nt), in_specs=[cur, nxt, wspec], out_specs=cur,
        out_shape=jax.ShapeDtypeStruct((S, DFF), BF16), compiler_params=_cparams(("parallel", "parallel")),
    )(dc, dc, w)


def _adam_math(g, w, m, v):
    m = B1 * m + (1.0 - B1) * g
    v = B2 * v + (1.0 - B2) * (g * g)
    m_hat = m / (1.0 - B1 ** STEP)
    v_hat = v / (1.0 - B2 ** STEP)
    return -LR * (m_hat / (jnp.sqrt(v_hat) + EPS) + WD * w), m, v


def _adamw_slots(recv, w, m, v, name):
    n, R, C = recv.shape
    tr = _tile(R, 128, 8)

    def body(r_ref, w_ref, m_ref, v_ref, g_ref, d_ref, nm_ref, nv_ref):
        g = r_ref[0]
        for s in range(1, n):
            g = g + r_ref[s]
        g_ref[...] = g
        d_ref[...], nm_ref[...], nv_ref[...] = _adam_math(g, w_ref[...], m_ref[...], v_ref[...])

    blk = pl.BlockSpec((tr, C), lambda i: (i, 0))
    return pl.pallas_call(
        body, name=name, grid=(R // tr,), in_specs=[pl.BlockSpec((n, tr, C), lambda i: (0, i, 0)), blk, blk, blk],
        out_specs=[blk] * 4, out_shape=[jax.ShapeDtypeStruct((R, C), F32)] * 4, compiler_params=_cparams(("parallel",)),
    )(recv, w, m, v)


def _adamw_small(g, w, m, v, name):
    def body(g_ref, w_ref, m_ref, v_ref, d_ref, nm_ref, nv_ref):
        d_ref[...], nm_ref[...], nv_ref[...] = _adam_math(g_ref[...], w_ref[...], m_ref[...], v_ref[...])

    return pl.pallas_call(body, name=name, out_shape=[jax.ShapeDtypeStruct(g.shape, F32)] * 3)(g, w, m, v)


def _sum_slots(x, name):
    def body(x_ref, o_ref):
        g = x_ref[0]
        for s in range(1, x.shape[0]):
            g = g + x_ref[s]
        o_ref[...] = g

    return pl.pallas_call(body, name=name, out_shape=jax.ShapeDtypeStruct(x.shape[1:], F32))(x)


MESH = pl.DeviceIdType.MESH
_HBM = pl.BlockSpec(memory_space=pltpu.HBM)


def _all_gather(x, name):
    def body(x_ref, out_ref, send_sems, recv_sems, local_sem):
        ax, ay, ac = lax.axis_index("x"), lax.axis_index("y"), lax.axis_index("c")
        me, sibling = (ax, ay, ac), (ax, ay, 1 - ac)
        chips = [(1 - ax, ay), (ax, 1 - ay), (1 - ax, 1 - ay)]

        def slot(px, py, pc):
            return out_ref.at[4 * px + 2 * py + pc]

        def copy(k, block, to, src=None):
            return pltpu.make_async_remote_copy(
                src_ref=slot(*block) if src is None else src, dst_ref=slot(*block),
                send_sem=send_sems.at[k], recv_sem=recv_sems.at[k], device_id=to, device_id_type=MESH)

        mine = pltpu.make_async_copy(x_ref, slot(*me), local_sem)
        mine.start()
        first = [copy(0, me, sibling, src=x_ref)]
        first += [copy(1 + j, me, (*chip, ac), src=x_ref) for j, chip in enumerate(chips)]
        for cp in first:
            cp.start()
        passed = [copy(4 + j, (*chip, ac), sibling) for j, chip in enumerate(chips)]
        for j, chip in enumerate(chips):
            copy(1 + j, (*chip, ac), me).wait_recv()
            passed[j].start()
        copy(0, sibling, me).wait_recv()
        for j, chip in enumerate(chips):
            copy(4 + j, (*chip, 1 - ac), me).wait_recv()
        for cp in first + passed:
            cp.wait_send()
        mine.wait()

    return pl.pallas_call(
        body, name=name, out_shape=jax.ShapeDtypeStruct((N_DEV,) + x.shape, x.dtype), in_specs=[_HBM], out_specs=_HBM,
        scratch_shapes=[pltpu.SemaphoreType.DMA((7,)), pltpu.SemaphoreType.DMA((7,)), pltpu.SemaphoreType.DMA(())],
    )(x)


def _all_to_all(x, name):
    def body(x_ref, out_ref, send_sems, recv_sems, local_sem):
        ax, ay, ac = lax.axis_index("x"), lax.axis_index("y"), lax.axis_index("c")
        me = 4 * ax + 2 * ay + ac
        mine = pltpu.make_async_copy(x_ref.at[me], out_ref.at[me], local_sem)
        mine.start()
        copies = []
        for k in range(1, N_DEV):
            px = 1 - ax if k & 4 else ax
            py = 1 - ay if k & 2 else ay
            pc = 1 - ac if k & 1 else ac
            copies.append(pltpu.make_async_remote_copy(
                src_ref=x_ref.at[4 * px + 2 * py + pc], dst_ref=out_ref.at[me],
                send_sem=send_sems.at[k - 1], recv_sem=recv_sems.at[k - 1], device_id=(px, py, pc), device_id_type=MESH))
        for cp in copies:
            cp.start()
        for cp in copies:
            cp.wait_recv()
        for cp in copies:
            cp.wait_send()
        mine.wait()

    return pl.pallas_call(
        body, name=name, out_shape=jax.ShapeDtypeStruct(x.shape, x.dtype), in_specs=[_HBM], out_specs=_HBM,
        scratch_shapes=[pltpu.SemaphoreType.DMA((7,)), pltpu.SemaphoreType.DMA((7,)), pltpu.SemaphoreType.DMA(())],
    )(x)


def _layer_fwd(x, xb, p, w, cos, sin, rconsts):
    proj = _mm(xb, w["w_in"], name="mm_proj", out_dtype=BF16)
    ogs, lgs = zip(*[_attn_fwd(proj, g, dil) for g, dil in enumerate(DILATIONS)])
    attn, lse = _rowwise(_f_combine, list(ogs) + list(lgs), [], [(D, BF16), (D, F32)], [], name="attn_combine")
    ret_raw, states = _ret_fwd(proj, cos, sin, rconsts)
    rg_wins = [(proj, RDV, OFF_RG // RDV + h) for h in range(RH)]
    ga_win, gr_win = (proj, D, OFF_GA // D), (proj, D, OFF_GR // D)
    (r,) = _rowwise(_f_gn, [ret_raw] + rg_wins, [w["ret_gn_g"], w["ret_gn_b"]], [(RH * RDV, BF16)], [], name="gn_fwd")
    ap = _mm(attn, w["w_attn_proj"], name="mm_attn_proj", out_dtype=BF16)
    rp = _mm(r, w["w_ret_proj"], name="mm_ret_proj", out_dtype=BF16)
    (merged,) = _rowwise(_f_gate, [ap, rp, ga_win, gr_win], [], [(D, BF16)], [], name="gate_fwd")
    mix = _mm(merged, w["w_out"], name="mm_out")
    h1, x1, x1b = _rowwise(_f_ln1, [x, mix], [w["ln1_g"], w["ln1_b"]], [(D, F32), (D, F32), (D, BF16)], [], name="ln1_fwd")
    z = _mm(x1b, w["w_ple_gate"], name="mm_ple_gate")
    pp = _mm(p, w["w_ple_proj"], name="mm_ple_proj")
    hg = _mm(x1b, w["w_up_g"], name="mm_up_g", out_dtype=BF16)
    hu = _mm(x1b, w["w_up_u"], name="mm_up_u", out_dtype=BF16)
    act = _conv_fwd(hg, hu, w["conv_wg"], w["conv_wu"], w["conv_bg"], w["conv_bu"])
    ffn = _mm(act, w["w_down"], name="mm_down")
    h2, x2, x2b = _rowwise(_f_ln2, [x1, ffn, z, pp], [w["ln2_g"], w["ln2_b"]], [(D, F32), (D, F32), (D, BF16)], [], name="ln2_fwd")
    saved = dict(xb=xb, proj=proj, attn=attn, lse=lse, ret_raw=ret_raw, states=states, r=r, ap=ap, rp=rp, merged=merged,
                 h1=h1, x1b=x1b, z=z, pp=pp, hg=hg, hu=hu, act=act, h2=h2, p=p)
    return x2, x2b, saved


def _layer_bwd(dx2, w, sv, cos, sin, rconsts):
    gr = {}
    proj = sv["proj"]
    dh2, dh2b, gr["ln2_g"], gr["ln2_b"] = _rowwise(_f_ln_bwd, [dx2, sv["h2"]], [w["ln2_g"]], [(D, F32), (D, BF16)],
                                                   [(1, D), (1, D)], name="ln2_bwd")
    d_act = _mm(dh2b, w["w_down"], tb=True, name="mm_down_dx", out_dtype=BF16)
    gr["w_down"] = _mm(sv["act"], dh2b, ta=True, name="mm_down_dw")
    dcg, dcu, gwg, gwu, gbg, gbu = _conv_bwd_pre(d_act, sv["hg"], sv["hu"], w["conv_wg"], w["conv_wu"], w["conv_bg"], w["conv_bu"])
    gr["conv_w"] = jnp.concatenate([gwg, gwu], axis=1)
    gr["conv_b"] = jnp.concatenate([gbg, gbu], axis=1)
    dhg = _conv_bwd_in(dcg, w["conv_wg"], "conv_bwd_in_g")
    dhu = _conv_bwd_in(dcu, w["conv_wu"], "conv_bwd_in_u")
    gr["w_up"] = jnp.concatenate([_mm(sv["x1b"], dhg, ta=True, name="mm_up_g_dw"), _mm(sv["x1b"], dhu, ta=True, name="mm_up_u_dw")], axis=1)
    dx1 = _mm(dhg, w["w_up_g"], tb=True, name="mm_up_g_dx", add=dh2, add_scale=ALPHA)
    dx1 = _mm(dhu, w["w_up_u"], tb=True, name="mm_up_u_dx", add=dx1)
    dpp, dz = _rowwise(_f_ple_bwd, [dh2, sv["z"], sv["pp"]], [], [(D, BF16), (D, BF16)], [], name="ple_bwd")
    gr["w_ple_proj"] = _mm(sv["p"], dpp, ta=True, name="mm_ple_proj_dw")
    gr["w_ple_gate"] = _mm(sv["x1b"], dz, ta=True, name="mm_ple_gate_dw")
    dx1 = _mm(dz, w["w_ple_gate"], tb=True, name="mm_ple_gate_dx", add=dx1)
    dh1, dh1b, gr["ln1_g"], gr["ln1_b"] = _rowwise(_f_ln_bwd, [dx1, sv["h1"]], [w["ln1_g"]], [(D, F32), (D, BF16)],
                                                   [(1, D), (1, D)], name="ln1_bwd")
    d_merged = _mm(dh1b, w["w_out"], tb=True, name="mm_out_dx", out_dtype=BF16)
    gr["w_out"] = _mm(sv["merged"], dh1b, ta=True, name="mm_out_dw")
    rg_wins = [(proj, RDV, OFF_RG // RDV + h) for h in range(RH)]
    ga_win, gr_win = (proj, D, OFF_GA // D), (proj, D, OFF_GR // D)
    dap, drp, dga, dgr = _rowwise(_f_gate_bwd, [d_merged, sv["ap"], sv["rp"], ga_win, gr_win], [], [(D, BF16)] * 4, [], name="gate_bwd")
    d_attn = _mm(dap, w["w_attn_proj"], tb=True, name="mm_attn_proj_dx", out_dtype=BF16)
    gr["w_attn_proj"] = _mm(sv["attn"], dap, ta=True, name="mm_attn_proj_dw")
    d_r = _mm(drp, w["w_ret_proj"], tb=True, name="mm_ret_proj_dx", out_dtype=BF16)
    gr["w_ret_proj"] = _mm(sv["r"], drp, ta=True, name="mm_ret_proj_dw")
    d_ret, d_rg, gr["ret_gn_g"], gr["ret_gn_b"] = _rowwise(
        _f_gn_bwd, [d_r, sv["ret_raw"]] + rg_wins, [w["ret_gn_g"], w["ret_gn_b"]], [(RH * RDV, BF16)] * 2,
        [(1, RH * RDV), (1, RH * RDV)], name="gn_bwd")
    d_rq, d_rk, d_rv = _ret_bwd(proj, cos, sin, rconsts, sv["states"], d_ret)
    (delta,) = _rowwise(_f_delta, [d_attn, sv["attn"]], [], [(D, F32)], [], name="attn_delta")
    dqkv = []
    for g, dil in enumerate(DILATIONS):
        dqkv += _attn_bwd(proj, d_attn, sv["lse"], delta, g, dil)
    d_proj = jnp.concatenate(dqkv + [d_rq, d_rk, d_rv, d_rg, dga, dgr], axis=1)
    gr["w_in"] = _mm(sv["xb"], d_proj, ta=True, name="mm_proj_dw")
    dx0 = _mm(d_proj, w["w_in"], tb=True, name="mm_proj_dx", add=dh1, add_scale=ALPHA)
    return dx0, gr


def _local_step(x, p, positions, target, ws):
    half = RDK // 2
    freq = jnp.power(ROPE_BASE, -jnp.arange(half, dtype=F32) / half)
    ang = positions.astype(F32)[:, None] * freq[None, :]
    cos, sin = jnp.cos(ang), jnp.sin(ang)
    rconsts = _ret_consts()
    xb = x.astype(BF16)
    saved = []
    for l in range(DEPTH):
        x, xb, sv = _layer_fwd(x, xb, p[l], ws[l], cos, sin, rconsts)
        saved.append(sv)
    dy, loss_vec = _rowwise(_f_loss, [x, target], [], [(D, F32)], [(1, D)], name="loss")
    grads = [None] * DEPTH
    for l in reversed(range(DEPTH)):
        dy, grads[l] = _layer_bwd(dy, ws[l], saved[l], cos, sin, rconsts)
    return loss_vec, dy, grads


FLAT_C = 1024


def _shard_layout(shards):
    off, lay = 0, {}
    for n in BIG:
        size = int(np.prod(shards[n].shape))
        lay[n] = (off, size)
        off += size
    return lay, off


def _full_weight(piece, name):
    if name in COL_SHARDED:
        return piece.transpose(1, 0, 2).reshape(piece.shape[1], N_DEV * piece.shape[2])
    return piece.reshape(N_DEV * piece.shape[1], piece.shape[2])


def _shard_major(gw, name):
    if name in COL_SHARDED:
        return gw.reshape(gw.shape[0], N_DEV, gw.shape[1] // N_DEV).transpose(1, 0, 2)
    return gw.reshape(N_DEV, gw.shape[0] // N_DEV, gw.shape[1])


def kernel(x, p, positions, w_in, w_attn_proj, w_ret_proj, ret_gn_g, ret_gn_b, w_out, ln1_g, ln1_b, w_up, conv_w, conv_b, w_down, w_ple_gate, w_ple_proj, ln2_g, ln2_b, loss_target, m_w_in, m_w_attn_proj, m_w_ret_proj, m_ret_gn_g, m_ret_gn_b, m_w_out, m_ln1_g, m_ln1_b, m_w_up, m_conv_w, m_conv_b, m_w_down, m_w_ple_gate, m_w_ple_proj, m_ln2_g, m_ln2_b, v_w_in, v_w_attn_proj, v_w_ret_proj, v_ret_gn_g, v_ret_gn_b, v_w_out, v_ln1_g, v_ln1_b, v_w_up, v_conv_w, v_conv_b, v_w_down, v_w_ple_gate, v_w_ple_proj, v_ln2_g, v_ln2_b):
    W = dict(w_in=w_in, w_attn_proj=w_attn_proj, w_ret_proj=w_ret_proj, ret_gn_g=ret_gn_g, ret_gn_b=ret_gn_b, w_out=w_out,
             ln1_g=ln1_g, ln1_b=ln1_b, w_up=w_up, conv_w=conv_w, conv_b=conv_b, w_down=w_down, w_ple_gate=w_ple_gate,
             w_ple_proj=w_ple_proj, ln2_g=ln2_g, ln2_b=ln2_b)
    M = dict(w_in=m_w_in, w_attn_proj=m_w_attn_proj, w_ret_proj=m_w_ret_proj, ret_gn_g=m_ret_gn_g, ret_gn_b=m_ret_gn_b,
             w_out=m_w_out, ln1_g=m_ln1_g, ln1_b=m_ln1_b, w_up=m_w_up, conv_w=m_conv_w, conv_b=m_conv_b, w_down=m_w_down,
             w_ple_gate=m_w_ple_gate, w_ple_proj=m_w_ple_proj, ln2_g=m_ln2_g, ln2_b=m_ln2_b)
    V = dict(w_in=v_w_in, w_attn_proj=v_w_attn_proj, w_ret_proj=v_w_ret_proj, ret_gn_g=v_ret_gn_g, ret_gn_b=v_ret_gn_b,
             w_out=v_w_out, ln1_g=v_ln1_g, ln1_b=v_ln1_b, w_up=v_w_up, conv_w=v_conv_w, conv_b=v_conv_b, w_down=v_w_down,
             w_ple_gate=v_w_ple_gate, w_ple_proj=v_w_ple_proj, ln2_g=v_ln2_g, ln2_b=v_ln2_b)
    S = x.shape[1]
    lay, n_big = _shard_layout(W)
    assert n_big % FLAT_C == 0
    n_cw = int(np.prod(conv_w.shape))

    flat = jnp.concatenate([W[n].astype(BF16).reshape(-1) for n in BIG]
                           + [lax.bitcast_convert_type(conv_w.reshape(-1), BF16).reshape(-1)])
    rows = -(-flat.shape[0] // (16 * FLAT_C)) * 16
    flat = jnp.pad(flat, (0, rows * FLAT_C - flat.shape[0])).reshape(rows, FLAT_C)
    gathered = _all_gather(flat, "gather_weights").reshape(N_DEV, rows * FLAT_C)
    cw_all = lax.bitcast_convert_type(gathered[:, n_big:n_big + 2 * n_cw].reshape(N_DEV, n_cw, 2), F32)
    cw_all = cw_all.reshape((N_DEV,) + conv_w.shape).transpose(1, 2, 0, 3).reshape(DEPTH, 3, 2 * DFF)
    ws = []
    for l in range(DEPTH):
        w = {}
        for n in BIG:
            off, size = lay[n]
            piece = gathered[:, off:off + size].reshape((N_DEV,) + W[n].shape)[:, l]
            w[n] = _full_weight(piece, n)
        w["w_up_g"], w["w_up_u"] = w["w_up"][:, :DFF], w["w_up"][:, DFF:]
        w["conv_wg"], w["conv_wu"] = cw_all[l][:, :DFF], cw_all[l][:, DFF:]
        w["conv_bg"], w["conv_bu"] = conv_b[l][None, :DFF], conv_b[l][None, DFF:]
        for n in ("ret_gn_g", "ret_gn_b", "ln1_g", "ln1_b", "ln2_g", "ln2_b"):
            w[n] = W[n][l][None, :]
        ws.append(w)

    loss_vec, grad_x, grads = _local_step(x[0], p[:, 0], positions[0], loss_target[0], ws)
    loss = lax.psum(jnp.sum(loss_vec), ("x", "y", "c"))

    g_flat = jnp.concatenate(
        [jnp.stack([_shard_major(grads[l][n], n) for l in range(DEPTH)], axis=1).reshape(N_DEV, -1) for n in BIG], axis=1)
    recv = _all_to_all(g_flat.reshape(N_DEV, n_big // FLAT_C, FLAT_C), "exchange_grads")
    cat = lambda d: jnp.concatenate([d[n].reshape(-1) for n in BIG]).reshape(n_big // FLAT_C, FLAT_C)
    big_out = _adamw_slots(recv, cat(W), cat(M), cat(V), "adamw_big")
    G, DW, NM, NV = ({} for _ in range(4))
    for res, dst in zip(big_out, (G, DW, NM, NV)):
        res = res.reshape(-1)
        for n in BIG:
            off, size = lay[n]
            dst[n] = res[off:off + size].reshape(W[n].shape)

    small_names = SMALL + ("conv_w",)
    g_small = jnp.concatenate([jnp.stack([grads[l][n].reshape(-1) for l in range(DEPTH)]).reshape(-1) for n in small_names])
    n_small = g_small.shape[0]
    assert n_small % FLAT_C == 0
    g_small = _sum_slots(_all_gather(g_small.reshape(-1, FLAT_C), "gather_small_grads"), "sum_small_grads").reshape(-1)
    off = 0
    for n in SMALL:
        G[n] = g_small[off:off + W[n].size].reshape(W[n].shape)
        off += W[n].size
    g_cw = g_small[off:].reshape(DEPTH, 3, N_DEV, conv_w.shape[2])
    me = 4 * lax.axis_index("x") + 2 * lax.axis_index("y") + lax.axis_index("c")
    G["conv_w"] = lax.dynamic_index_in_dim(g_cw, me, axis=2, keepdims=False)
    pad = (-sum(W[n].size for n in SMALL)) % FLAT_C
    cat_s = lambda d: jnp.pad(jnp.concatenate([d[n].reshape(-1) for n in SMALL]), (0, pad)).reshape(-1, FLAT_C)
    small_out = _adamw_small(cat_s(G), cat_s(W), cat_s(M), cat_s(V), "adamw_small")
    for res, dst in zip(small_out, (DW, NM, NV)):
        res, off = res.reshape(-1), 0
        for n in SMALL:
            dst[n] = res[off:off + W[n].size].reshape(W[n].shape)
            off += W[n].size
    two_d = lambda t: t.reshape(DEPTH * 3, conv_w.shape[2])
    cw_out = _adamw_small(two_d(G["conv_w"]), two_d(conv_w), two_d(m_conv_w), two_d(v_conv_w), "adamw_conv_w")
    for res, dst in zip(cw_out, (DW, NM, NV)):
        dst["conv_w"] = res.reshape(conv_w.shape)

    return (loss, grad_x[None], *[G[n] for n in WEIGHTS], *[DW[n] for n in WEIGHTS], *[NM[n] for n in WEIGHTS],
            *[NV[n] for n in WEIGHTS])
```

```python
import math

import numpy as np
import jax
import jax.numpy as jnp
from jax import lax
from jax.experimental import pallas as pl
from jax.experimental.pallas import tpu as pltpu

F32, BF16 = jnp.float32, jnp.bfloat16

D = 1024
DEPTH = 2
N_DEV = 8
HD = 128
NH = 8
DILATIONS = (1, 4, 16)
SPAN = 128
N_ATT = 3 * 3 * NH * HD
RH, RDK, RDV = 4, 256, 512
CH = 128
DFF = 2816
PLE = 256
N_IN = 17408
N_REST = N_IN - N_ATT
OFF_RQ, OFF_RK, OFF_RV, OFF_RG, OFF_GA, OFF_GR = 0, 1024, 2048, 4096, 6144, 7168
ALPHA = (2 * DEPTH) ** 0.25
LN_EPS, GN_EPS = 1e-5, 1e-6
ROPE_BASE = 10000.0
LR, B1, B2, EPS, WD, STEP = 0.001, 0.9, 0.999, 1e-8, 0.01, 10
VMEM_LIMIT = 48 * 1024 * 1024
NEG = -1e30

BIG = ("w_in", "w_attn_proj", "w_ret_proj", "w_out", "w_up", "w_down", "w_ple_gate", "w_ple_proj")
COL_SHARDED = ("w_in", "w_up", "w_ple_proj")
SMALL = ("ret_gn_g", "ret_gn_b", "ln1_g", "ln1_b", "conv_b", "ln2_g", "ln2_b")
WEIGHTS = ("w_in", "w_attn_proj", "w_ret_proj", "ret_gn_g", "ret_gn_b", "w_out", "ln1_g", "ln1_b", "w_up",
           "conv_w", "conv_b", "w_down", "w_ple_gate", "w_ple_proj", "ln2_g", "ln2_b")


def _tile(n, cap, mult=128):
    if n <= cap:
        return n
    t = (cap // mult) * mult
    while n % t:
        t -= mult
    return t


def _cparams(sem):
    return pltpu.CompilerParams(dimension_semantics=sem, vmem_limit_bytes=VMEM_LIMIT)


def _dot(a, b, ca, cb):
    return lax.dot_general(a, b, (((ca,), (cb,)), ((), ())), preferred_element_type=F32)


def _bdot(a, b, ca, cb):
    return lax.dot_general(a, b, (((ca,), (cb,)), ((0,), (0,))), preferred_element_type=F32)


def _mm(a, b, *, name, ta=False, tb=False, out_dtype=F32, add=None, add_scale=1.0, a_res=1, b_res=1, o_res=1,
        tm=1024, tn=1024, tk=1024):
    if a_res > 1:
        assert not ta
        M, K = a.shape[0] * a_res, a.shape[1] // a_res
    else:
        M, K = (a.shape[1], a.shape[0]) if ta else a.shape
    if b_res > 1:
        assert ta and not tb
        N = b.shape[1] // b_res
        assert K == b.shape[0] * b_res
    else:
        N = b.shape[0] if tb else b.shape[1]
        assert K == (b.shape[1] if tb else b.shape[0])
    tm = _tile(M // max(a_res, o_res), tm)
    tn = _tile(N, tn)
    tk = _tile(K // b_res, tk)
    nk, nkb, nn = K // tk, K // tk // b_res, N // tn
    nma, nmo = M // tm // a_res, M // tm // o_res

    def body(*refs):
        if add is None:
            a_ref, b_ref, o_ref, acc_ref = refs
        else:
            a_ref, b_ref, add_ref, o_ref, acc_ref = refs
        k = pl.program_id(2)

        @pl.when(k == 0)
        def _():
            acc_ref[...] = jnp.zeros_like(acc_ref)

        acc_ref[...] += _dot(a_ref[...].astype(BF16), b_ref[...].astype(BF16), 0 if ta else 1, 1 if tb else 0)

        @pl.when(k == nk - 1)
        def _():
            r = acc_ref[...]
            if add is not None:
                r = r + add_scale * add_ref[...].astype(F32)
            o_ref[...] = r.astype(out_dtype)

    if ta:
        a_spec = pl.BlockSpec((tk, tm), lambda i, j, k: (k, i))
    elif a_res > 1:
        a_spec = pl.BlockSpec((tm, tk), lambda i, j, k: (i % nma, (i // nma) * nk + k))
    else:
        a_spec = pl.BlockSpec((tm, tk), lambda i, j, k: (i, k))
    if tb:
        b_spec = pl.BlockSpec((tn, tk), lambda i, j, k: (j, k))
    elif b_res > 1:
        b_spec = pl.BlockSpec((tk, tn), lambda i, j, k: (k % nkb, (k // nkb) * nn + j))
    else:
        b_spec = pl.BlockSpec((tk, tn), lambda i, j, k: (k, j))
    if o_res > 1:
        assert add is None
        o_spec = pl.BlockSpec((tm, tn), lambda i, j, k: (i % nmo, (i // nmo) * nn + j))
        o_shape = (M // o_res, o_res * N)
    else:
        o_spec = pl.BlockSpec((tm, tn), lambda i, j, k: (i, j))
        o_shape = (M, N)
    in_specs, args = [a_spec, b_spec], [a, b]
    if add is not None:
        in_specs.append(o_spec)
        args.append(add)
    return pl.pallas_call(
        body, name=name, grid=(M // tm, nn, nk), in_specs=in_specs, out_specs=o_spec,
        out_shape=jax.ShapeDtypeStruct(o_shape, out_dtype), scratch_shapes=[pltpu.VMEM((tm, tn), F32)],
        compiler_params=_cparams(("parallel", "parallel", "arbitrary")),
    )(*args)


def _rowwise(fn, rows, pars, outs, accs, *, name, tm=256):
    first = rows[0][0] if isinstance(rows[0], tuple) else rows[0]
    S = first.shape[0]
    tm = _tile(S, tm, 8)
    n_r, n_p, n_o = len(rows), len(pars), len(outs)

    def body(*refs):
        i = pl.program_id(0)
        vals = [r[...] for r in refs[:n_r + n_p]]
        res = fn(*vals)
        if not isinstance(res, (tuple, list)):
            res = (res,)
        o_refs = refs[n_r + n_p:n_r + n_p + n_o]
        a_refs = refs[n_r + n_p + n_o:]
        for r, v in zip(o_refs, res[:n_o]):
            r[...] = v.astype(r.dtype)
        if a_refs:
            @pl.when(i == 0)
            def _():
                for r in a_refs:
                    r[...] = jnp.zeros_like(r)

            for r, v in zip(a_refs, res[n_o:]):
                r[...] += v

    in_specs, args = [], []
    for r in rows:
        if isinstance(r, tuple):
            arr, w, cb = r
            in_specs.append(pl.BlockSpec((tm, w), lambda i, cb=cb: (i, cb)))
        else:
            arr = r
            in_specs.append(pl.BlockSpec((tm, arr.shape[1]), lambda i: (i, 0)))
        args.append(arr)
    for p_ in pars:
        in_specs.append(pl.BlockSpec(p_.shape, lambda i: (0, 0)))
        args.append(p_)
    out_shape = [jax.ShapeDtypeStruct((S, w), dt) for w, dt in outs] + [jax.ShapeDtypeStruct(a, F32) for a in accs]
    out_specs = [pl.BlockSpec((tm, w), lambda i: (i, 0)) for w, _ in outs] + [pl.BlockSpec(a, lambda i: (0, 0)) for a in accs]
    return pl.pallas_call(
        body, name=name, grid=(S // tm,), in_specs=in_specs, out_specs=out_specs, out_shape=out_shape,
        compiler_params=_cparams(("arbitrary",) if accs else ("parallel",)),
    )(*args)


def _norm(h, eps):
    mu = jnp.mean(h, -1, keepdims=True)
    d = h - mu
    rstd = lax.rsqrt(jnp.mean(d * d, -1, keepdims=True) + eps)
    return d * rstd, rstd


def _norm_bwd(dxh, xh, rstd):
    return rstd * (dxh - jnp.mean(dxh, -1, keepdims=True) - xh * jnp.mean(dxh * xh, -1, keepdims=True))


def _sig(x):
    return 1.0 / (1.0 + jnp.exp(-x))


_GELU_C = math.sqrt(2.0 / math.pi)


def _gelu(x):
    t = jnp.tanh(_GELU_C * (x + 0.044715 * x * x * x))
    return 0.5 * x * (1.0 + t), t


def _gelu_grad(x, t):
    return 0.5 * (1.0 + t) + 0.5 * x * (1.0 - t * t) * _GELU_C * (1.0 + 3 * 0.044715 * x * x)


def _f_ln1(x, mix, g, b):
    h = ALPHA * x + mix
    xh, _ = _norm(h, LN_EPS)
    y = xh * g + b
    return h, y, y


def _f_ln2(x, ffn, z, pp, g, b):
    h = ALPHA * x + ffn + _sig(z) * pp
    xh, _ = _norm(h, LN_EPS)
    y = xh * g + b
    return h, y, y


def _f_ln_bwd(*args):
    *dys, h, g = args
    dy = dys[0]
    for t in dys[1:]:
        dy = dy + t
    xh, rstd = _norm(h, LN_EPS)
    dh = _norm_bwd(dy * g, xh, rstd)
    return dh, dh, jnp.sum(dy * xh, 0, keepdims=True), jnp.sum(dy, 0, keepdims=True)


def _f_sum(*ts):
    r = ts[0]
    for t in ts[1:]:
        r = r + t
    return r


def _f_loss(y, t):
    e = y - t
    return e * (1.0 / D), jnp.sum(e * e, 0, keepdims=True) * (0.5 / D)


def _head_col(c, h):
    lane = lax.broadcasted_iota(jnp.int32, c.shape, 1)
    return jnp.sum(jnp.where(lane == h, c, 0.0), -1, keepdims=True)


def _f_combine(o0, o1, o2, l0, l1, l2):
    lane = lax.broadcasted_iota(jnp.int32, l0.shape, 1)
    parts, lse = [], jnp.zeros(l0.shape, F32)
    for h in range(NH):
        sl = slice(h * HD, (h + 1) * HD)
        a0, a1, a2 = _head_col(l0, h), _head_col(l1, h), _head_col(l2, h)
        m = jnp.maximum(jnp.maximum(a0, a1), a2)
        e0, e1, e2 = jnp.exp(a0 - m), jnp.exp(a1 - m), jnp.exp(a2 - m)
        den = e0 + e1 + e2
        parts.append((e0 * o0[:, sl].astype(F32) + e1 * o1[:, sl].astype(F32) + e2 * o2[:, sl].astype(F32)) / den)
        lse = jnp.where(lane == h, m + jnp.log(den), lse)
    return jnp.concatenate(parts, axis=1), lse


def _f_delta(da, a):
    lane = lax.broadcasted_iota(jnp.int32, (da.shape[0], HD), 1)
    out = jnp.zeros((da.shape[0], HD), F32)
    for h in range(NH):
        sl = slice(h * HD, (h + 1) * HD)
        s = jnp.sum(da[:, sl].astype(F32) * a[:, sl].astype(F32), -1, keepdims=True)
        out = jnp.where(lane == h, s, out)
    return out


def _f_gate(ap, rp, ga, gr):
    return _sig(ga.astype(F32)) * ap.astype(F32) + _sig(gr.astype(F32)) * rp.astype(F32)


def _f_gate_bwd(dm, ap, rp, ga, gr):
    dm = dm.astype(F32)
    sa, sr = _sig(ga.astype(F32)), _sig(gr.astype(F32))
    return dm * sa, dm * sr, dm * ap.astype(F32) * sa * (1.0 - sa), dm * rp.astype(F32) * sr * (1.0 - sr)


def _f_gn(y, rg, g, b):
    y, rg = y.astype(F32), rg.astype(F32)
    parts = []
    for h in range(RH):
        sl = slice(h * RDV, (h + 1) * RDV)
        xh, _ = _norm(y[:, sl], GN_EPS)
        parts.append(xh * g[:, sl] + b[:, sl])
    return rg * _sig(rg) * jnp.concatenate(parts, axis=1)


def _f_gn_bwd(dr, y, rg, g, b):
    dr, y, rg = dr.astype(F32), y.astype(F32), rg.astype(F32)
    s = _sig(rg)
    d_out = dr * rg * s
    dys, outs, xhs = [], [], []
    for h in range(RH):
        sl = slice(h * RDV, (h + 1) * RDV)
        xh, rstd = _norm(y[:, sl], GN_EPS)
        xhs.append(xh)
        outs.append(xh * g[:, sl] + b[:, sl])
        dys.append(_norm_bwd(d_out[:, sl] * g[:, sl], xh, rstd))
    xh, out = jnp.concatenate(xhs, axis=1), jnp.concatenate(outs, axis=1)
    d_rg = dr * out * s * (1.0 + rg * (1.0 - s))
    return jnp.concatenate(dys, axis=1), d_rg, jnp.sum(d_out * xh, 0, keepdims=True), jnp.sum(d_out, 0, keepdims=True)


def _f_ple_bwd(dh, z, pp):
    s = _sig(z)
    return dh * s, dh * pp * s * (1.0 - s)


def _to_tokens(t, d):
    if d == 1:
        return t
    S, C = t.shape
    return t.reshape(d, S // d, C).transpose(1, 0, 2).reshape(S, C)


def _to_residues(t, d):
    if d == 1:
        return t
    S, C = t.shape
    return t.reshape(S // d, d, C).transpose(1, 0, 2).reshape(S, C)


def _band(nb, first_valid, last_valid=None):
    b = lax.broadcasted_iota(jnp.int32, (nb, SPAN, SPAN), 0)
    row = lax.broadcasted_iota(jnp.int32, (nb, SPAN, SPAN), 1)
    col = lax.broadcasted_iota(jnp.int32, (nb, SPAN, SPAN), 2)
    off = jnp.where(b == 0, jnp.where(first_valid, 0, 2 * SPAN), 0)
    if last_valid is not None:
        off = off + jnp.where(b == nb - 1, jnp.where(last_valid, 0, 2 * SPAN), 0)
    return col <= row, col >= row + off


def _attn_tiles(S, dil):
    Sd = S // dil
    T = min(1024, Sd)
    return Sd, T, T // SPAN, Sd // T


def _attn_fwd(qkv, dil, name):
    S = qkv.shape[0]
    Sd, T, nsub, nib = _attn_tiles(S, dil)
    scale = HD ** -0.5

    def body(c_ref, p_ref, o_ref, l_ref):
        ib, h = pl.program_id(1), pl.program_id(2)
        blk, hal = c_ref[...], p_ref[...]
        q, k, v = blk[:, :HD], blk[:, HD:2 * HD], blk[:, 2 * HD:]
        if nsub > 1:
            kp = jnp.concatenate([hal[:, HD:2 * HD], k[:T - SPAN]], axis=0)
            vp = jnp.concatenate([hal[:, 2 * HD:], v[:T - SPAN]], axis=0)
        else:
            kp, vp = hal[:, HD:2 * HD], hal[:, 2 * HD:]
        q3, k3, v3, kp3, vp3 = (t.reshape(nsub, SPAN, HD) for t in (q, k, v, kp, vp))
        m_cur, m_prev = _band(nsub, ib > 0)
        sc = jnp.where(m_cur, _bdot(q3, k3, 2, 2) * scale, NEG)
        sp = jnp.where(m_prev, _bdot(q3, kp3, 2, 2) * scale, NEG)
        m = jnp.maximum(jnp.max(sc, -1, keepdims=True), jnp.max(sp, -1, keepdims=True))
        pc, pp = jnp.exp(sc - m), jnp.exp(sp - m)
        den = jnp.sum(pc, -1, keepdims=True) + jnp.sum(pp, -1, keepdims=True)
        o = (_bdot(pc.astype(BF16), v3, 2, 1) + _bdot(pp.astype(BF16), vp3, 2, 1)) / den
        o_ref[...] = o.reshape(T, HD).astype(BF16)
        lse = (m + jnp.log(den)).reshape(T, 1)
        lane = lax.broadcasted_iota(jnp.int32, (T, HD), 1)

        @pl.when(h == 0)
        def _():
            l_ref[...] = jnp.zeros_like(l_ref)

        l_ref[...] = jnp.where(lane == h, lse, l_ref[...])

    cur = pl.BlockSpec((T, 3 * HD), lambda r, ib, h: (r * nib + ib, h))
    prev = pl.BlockSpec((SPAN, 3 * HD), lambda r, ib, h: (r * (Sd // SPAN) + jnp.maximum(ib * nsub - 1, 0), h))
    return pl.pallas_call(
        body, name=name, grid=(dil, nib, NH), in_specs=[cur, prev],
        out_specs=[pl.BlockSpec((T, HD), lambda r, ib, h: (r * nib + ib, h)), pl.BlockSpec((T, HD), lambda r, ib, h: (r * nib + ib, 0))],
        out_shape=[jax.ShapeDtypeStruct((S, D), BF16), jax.ShapeDtypeStruct((S, HD), F32)],
        compiler_params=_cparams(("parallel", "parallel", "arbitrary")),
    )(qkv, qkv)


def _attn_bwd(qkv, d_attn, lse, delta, dil, name):
    S = qkv.shape[0]
    Sd, T, nsub, nib = _attn_tiles(S, dil)
    scale = HD ** -0.5
    ne = nsub + 1

    def body(c_ref, p_ref, n_ref, do_ref, don_ref, l_ref, ln_ref, dl_ref, dln_ref, o_ref):
        ib, h = pl.program_id(1), pl.program_id(2)
        blk, hal, nxt = c_ref[...], p_ref[...], n_ref[...]
        q, k, v = blk[:, :HD], blk[:, HD:2 * HD], blk[:, 2 * HD:]
        do = do_ref[...]
        l, dl = _head_col(l_ref[...], h), _head_col(dl_ref[...], h)
        qe = jnp.concatenate([q, nxt[:, :HD]], axis=0).reshape(ne, SPAN, HD)
        doe = jnp.concatenate([do, don_ref[...]], axis=0).reshape(ne, SPAN, HD)
        le = jnp.concatenate([l, _head_col(ln_ref[...], h)], axis=0).reshape(ne, SPAN, 1)
        dle = jnp.concatenate([dl, _head_col(dln_ref[...], h)], axis=0).reshape(ne, SPAN, 1)
        kpe = jnp.concatenate([hal[:, HD:2 * HD], k], axis=0).reshape(ne, SPAN, HD)
        vpe = jnp.concatenate([hal[:, 2 * HD:], v], axis=0).reshape(ne, SPAN, HD)
        _, m_prev = _band(ne, ib > 0, ib < nib - 1)
        p = jnp.where(m_prev, jnp.exp(_bdot(qe, kpe, 2, 2) * scale - le), 0.0)
        ds = (p * (_bdot(doe, vpe, 2, 2) - dle)).astype(BF16)
        dq = _bdot(ds, kpe, 2, 1)[:nsub]
        dk = _bdot(ds, qe, 1, 1)[1:]
        dv = _bdot(p.astype(BF16), doe, 1, 1)[1:]
        q3, k3, v3, do3 = (t.reshape(nsub, SPAN, HD) for t in (q, k, v, do))
        l3, dl3 = l.reshape(nsub, SPAN, 1), dl.reshape(nsub, SPAN, 1)
        m_cur, _ = _band(nsub, True)
        p = jnp.where(m_cur, jnp.exp(_bdot(q3, k3, 2, 2) * scale - l3), 0.0)
        ds = (p * (_bdot(do3, v3, 2, 2) - dl3)).astype(BF16)
        dq = (dq + _bdot(ds, k3, 2, 1)) * scale
        dk = (dk + _bdot(ds, q3, 1, 1)) * scale
        dv = dv + _bdot(p.astype(BF16), do3, 1, 1)
        o_ref[...] = jnp.concatenate([t.reshape(T, HD) for t in (dq, dk, dv)], axis=1).astype(BF16)

    nb = Sd // SPAN
    row = lambda r, ib: r * nib + ib
    prow = lambda r, ib: r * nb + jnp.maximum(ib * nsub - 1, 0)
    nrow = lambda r, ib: r * nb + jnp.minimum((ib + 1) * nsub, nb - 1)
    cur3 = pl.BlockSpec((T, 3 * HD), lambda r, ib, h: (row(r, ib), h))
    prev3 = pl.BlockSpec((SPAN, 3 * HD), lambda r, ib, h: (prow(r, ib), h))
    next3 = pl.BlockSpec((SPAN, 3 * HD), lambda r, ib, h: (nrow(r, ib), h))
    cur1 = pl.BlockSpec((T, HD), lambda r, ib, h: (row(r, ib), h))
    next1 = pl.BlockSpec((SPAN, HD), lambda r, ib, h: (nrow(r, ib), h))
    curc = pl.BlockSpec((T, HD), lambda r, ib, h: (row(r, ib), 0))
    nextc = pl.BlockSpec((SPAN, HD), lambda r, ib, h: (nrow(r, ib), 0))
    return pl.pallas_call(
        body, name=name, grid=(dil, nib, NH),
        in_specs=[cur3, prev3, next3, cur1, next1, curc, nextc, curc, nextc], out_specs=cur3,
        out_shape=jax.ShapeDtypeStruct((S, 3 * D), BF16),
        compiler_params=_cparams(("parallel", "parallel", "parallel")),
    )(qkv, qkv, qkv, d_attn, d_attn, lse, lse, delta, delta)


def _ret_consts():
    lg = np.log1p(-np.exp2(-5.0 - np.arange(RH, dtype=np.float64)))
    idx = np.arange(CH, dtype=np.float64)
    rel = idx[:, None] - idx[None, :]
    intra = np.where(rel >= 0, np.exp(lg[:, None, None] * np.maximum(rel, 0.0)), 0.0)
    qd = np.exp(lg[:, None] * (idx + 1.0))
    kd = np.exp(lg[:, None] * (CH - 1.0 - idx))
    cd = np.exp(lg * CH)
    wide = lambda t: np.broadcast_to(t[:, :, None], (RH, t.shape[1], RDV))
    return (jnp.asarray(intra, F32), jnp.asarray(wide(qd), F32), jnp.asarray(wide(kd), F32),
            jnp.asarray(np.broadcast_to(cd[:, None, None], (RH, 1, RDV)), F32))


def _rot(t, c, s):
    t1, t2 = t[:, :RDK // 2], t[:, RDK // 2:]
    return jnp.concatenate([t1 * c - t2 * s, t1 * s + t2 * c], axis=1)


def _unrot(d, c, s):
    d1, d2 = d[:, :RDK // 2], d[:, RDK // 2:]
    return jnp.concatenate([d1 * c + d2 * s, d2 * c - d1 * s], axis=1)


def _ret_specs(nmap):
    def blk(w, c0):
        return pl.BlockSpec((CH, w), lambda h, n: (nmap(n), c0 // w + h))
    q, k, v = blk(RDK, OFF_RQ), blk(RDK, OFF_RK), blk(RDV, OFF_RV)
    cs = pl.BlockSpec((CH, RDK // 2), lambda h, n: (nmap(n), 0))
    dmat = pl.BlockSpec((None, CH, CH), lambda h, n: (h, 0, 0))
    dvec = pl.BlockSpec((None, CH, RDV), lambda h, n: (h, 0, 0))
    cdv = pl.BlockSpec((None, 1, RDV), lambda h, n: (h, 0, 0))
    state = pl.BlockSpec((None, None, RDK, RDV), lambda h, n: (h, nmap(n), 0, 0))
    out = pl.BlockSpec((CH, RDV), lambda h, n: (nmap(n), h))
    return [q, k, v, cs, cs, dmat, dvec, dvec, cdv], state, out


def _ret_fwd(proj, cos, sin, consts):
    S = proj.shape[0]
    nc = S // CH

    def body(q_ref, k_ref, v_ref, c_ref, s_ref, d_ref, qd_ref, kd_ref, cd_ref, o_ref, st_ref, state):
        @pl.when(pl.program_id(1) == 0)
        def _():
            state[...] = jnp.zeros_like(state)

        c, s = c_ref[...], s_ref[...]
        qb = _rot(q_ref[...].astype(F32), c, s).astype(BF16)
        kb = (_rot(k_ref[...].astype(F32), c, s) * (RDK ** -0.5)).astype(BF16)
        vb = v_ref[...]
        sb = state[...].astype(BF16)
        st_ref[...] = sb
        a = (_dot(qb, kb, 1, 1) * d_ref[...]).astype(BF16)
        o_ref[...] = (_dot(a, vb, 1, 0) + _dot(qb, sb, 1, 0) * qd_ref[...]).astype(BF16)
        vk = (vb.astype(F32) * kd_ref[...]).astype(BF16)
        state[...] = cd_ref[...] * state[...] + _dot(kb, vk, 0, 0)

    ins, state_spec, out_spec = _ret_specs(lambda n: n)
    return pl.pallas_call(
        body, name="ret_fwd", grid=(RH, nc), in_specs=ins, out_specs=[out_spec, state_spec],
        out_shape=[jax.ShapeDtypeStruct((S, RH * RDV), BF16), jax.ShapeDtypeStruct((RH, nc, RDK, RDV), BF16)],
        scratch_shapes=[pltpu.VMEM((RDK, RDV), F32)],
        compiler_params=_cparams(("parallel", "arbitrary")),
    )(proj, proj, proj, cos, sin, *consts)


def _ret_bwd(proj, cos, sin, consts, states, d_ret):
    S = proj.shape[0]
    nc = S // CH

    def body(q_ref, k_ref, v_ref, c_ref, s_ref, d_ref, qd_ref, kd_ref, cd_ref, st_ref, do_ref,
             dq_ref, dk_ref, dv_ref, dstate):
        @pl.when(pl.program_id(1) == 0)
        def _():
            dstate[...] = jnp.zeros_like(dstate)

        c, s = c_ref[...], s_ref[...]
        qb = _rot(q_ref[...].astype(F32), c, s).astype(BF16)
        kb = (_rot(k_ref[...].astype(F32), c, s) * (RDK ** -0.5)).astype(BF16)
        vb, sb, do = v_ref[...], st_ref[...], do_ref[...]
        dmat, qd, kd = d_ref[...], qd_ref[...], kd_ref[...]
        a = (_dot(qb, kb, 1, 1) * dmat).astype(BF16)
        doq = (do.astype(F32) * qd).astype(BF16)
        dsb = dstate[...].astype(BF16)
        vk = (vb.astype(F32) * kd).astype(BF16)
        dv_ref[...] = (_dot(a, do, 0, 0) + _dot(kb, dsb, 1, 0) * kd).astype(BF16)
        da = (_dot(do, vb, 1, 1) * dmat).astype(BF16)
        dq = _dot(da, kb, 1, 0) + _dot(doq, sb, 1, 1)
        dk = (_dot(da, qb, 0, 0) + _dot(vk, dsb, 1, 1)) * (RDK ** -0.5)
        dq_ref[...] = _unrot(dq, c, s).astype(BF16)
        dk_ref[...] = _unrot(dk, c, s).astype(BF16)
        dstate[...] = cd_ref[...] * dstate[...] + _dot(qb, doq, 0, 0)

    rev = lambda n: nc - 1 - n
    ins, state_spec, out_spec = _ret_specs(rev)
    qk_out = pl.BlockSpec((CH, RDK), lambda h, n: (rev(n), h))
    return pl.pallas_call(
        body, name="ret_bwd", grid=(RH, nc), in_specs=ins + [state_spec, out_spec],
        out_specs=[qk_out, qk_out, out_spec],
        out_shape=[jax.ShapeDtypeStruct((S, RH * RDK), BF16)] * 2 + [jax.ShapeDtypeStruct((S, RH * RDV), BF16)],
        scratch_shapes=[pltpu.VMEM((RDK, RDV), F32)],
        compiler_params=_cparams(("parallel", "arbitrary")),
    )(proj, proj, proj, cos, sin, *consts, states, d_ret)


CW = 256
HALO = 16


def _shift_down(v, halo, k):
    rolled = pltpu.roll(v, k, 0)
    hr = pltpu.roll(halo, k, 0)[0:8]
    row = lax.broadcasted_iota(jnp.int32, hr.shape, 0)
    return jnp.concatenate([jnp.where(row < k, hr, rolled[0:8]), rolled[8:]], axis=0)


def _shift_up(v, halo, k):
    T = v.shape[0]
    rolled = pltpu.roll(v, T - k, 0)
    hr = pltpu.roll(halo, 8 - k, 0)[0:8]
    row = lax.broadcasted_iota(jnp.int32, hr.shape, 0)
    return jnp.concatenate([rolled[:T - 8], jnp.where(row >= 8 - k, hr, rolled[T - 8:])], axis=0)


def _conv_taps(h_ref, hp_ref, first):
    h = h_ref[...].astype(F32)
    hp = hp_ref[...].astype(F32) * jnp.where(first, 0.0, 1.0)
    return _shift_down(h, hp, 2), _shift_down(h, hp, 1), h


def _conv_specs(S, T):
    nj = DFF // CW
    cur = pl.BlockSpec((T, CW), lambda j, i: (i, j))
    prev = pl.BlockSpec((HALO, CW), lambda j, i: (jnp.maximum(i * (T // HALO) - 1, 0), j))
    nxt = pl.BlockSpec((HALO, CW), lambda j, i: (jnp.minimum((i + 1) * (T // HALO), S // HALO - 1), j))
    w = pl.BlockSpec((3, CW), lambda j, i: (0, j))
    b = pl.BlockSpec((1, CW), lambda j, i: (0, j))
    return nj, cur, prev, nxt, w, b


def _conv_fwd(hg, hu, wg, wu, bg, bu):
    S = hg.shape[0]
    T = min(1024, S)
    nj, cur, prev, _, w, b = _conv_specs(S, T)

    def body(hg_ref, hu_ref, hgp_ref, hup_ref, wg_ref, wu_ref, bg_ref, bu_ref, o_ref):
        first = pl.program_id(1) == 0
        g2, g1, g0 = _conv_taps(hg_ref, hgp_ref, first)
        u2, u1, u0 = _conv_taps(hu_ref, hup_ref, first)
        cg = wg_ref[0:1, :] * g2 + wg_ref[1:2, :] * g1 + wg_ref[2:3, :] * g0 + bg_ref[...]
        cu = wu_ref[0:1, :] * u2 + wu_ref[1:2, :] * u1 + wu_ref[2:3, :] * u0 + bu_ref[...]
        o_ref[...] = (_gelu(cg)[0] * cu).astype(BF16)

    return pl.pallas_call(
        body, name="conv_fwd", grid=(nj, S // T), in_specs=[cur, cur, prev, prev, w, w, b, b], out_specs=cur,
        out_shape=jax.ShapeDtypeStruct((S, DFF), BF16), compiler_params=_cparams(("parallel", "parallel")),
    )(hg, hu, hg, hu, wg, wu, bg, bu)


def _conv_bwd_pre(d_act, hg, hu, wg, wu, bg, bu):
    S = hg.shape[0]
    T = min(1024, S)
    nj, cur, prev, _, w, b = _conv_specs(S, T)

    def body(da_ref, hg_ref, hu_ref, hgp_ref, hup_ref, wg_ref, wu_ref, bg_ref, bu_ref,
             dcg_ref, dcu_ref, gwg_ref, gwu_ref, gbg_ref, gbu_ref):
        first = pl.program_id(1) == 0
        g2, g1, g0 = _conv_taps(hg_ref, hgp_ref, first)
        u2, u1, u0 = _conv_taps(hu_ref, hup_ref, first)
        cg = wg_ref[0:1, :] * g2 + wg_ref[1:2, :] * g1 + wg_ref[2:3, :] * g0 + bg_ref[...]
        cu = wu_ref[0:1, :] * u2 + wu_ref[1:2, :] * u1 + wu_ref[2:3, :] * u0 + bu_ref[...]
        da = da_ref[...].astype(F32)
        gl, t = _gelu(cg)
        dcg = da * cu * _gelu_grad(cg, t)
        dcu = da * gl
        dcg_ref[...] = dcg.astype(BF16)
        dcu_ref[...] = dcu.astype(BF16)

        @pl.when(first)
        def _():
            for r in (gwg_ref, gwu_ref, gbg_ref, gbu_ref):
                r[...] = jnp.zeros_like(r)

        for r, d, taps in ((gwg_ref, dcg, (g2, g1, g0)), (gwu_ref, dcu, (u2, u1, u0))):
            for j in range(3):
                r[j:j + 1, :] += jnp.sum(d * taps[j], 0, keepdims=True)
        gbg_ref[...] += jnp.sum(dcg, 0, keepdims=True)
        gbu_ref[...] += jnp.sum(dcu, 0, keepdims=True)

    return pl.pallas_call(
        body, name="conv_bwd_pre", grid=(nj, S // T), in_specs=[cur, cur, cur, prev, prev, w, w, b, b],
        out_specs=[cur, cur, w, w, b, b],
        out_shape=[jax.ShapeDtypeStruct((S, DFF), BF16)] * 2 + [jax.ShapeDtypeStruct((3, DFF), F32)] * 2
        + [jax.ShapeDtypeStruct((1, DFF), F32)] * 2,
        compiler_params=_cparams(("parallel", "arbitrary")),
    )(d_act, hg, hu, hg, hu, wg, wu, bg, bu)


def _conv_bwd_in(dc, w, name):
    S = dc.shape[0]
    T = min(1024, S)
    nj, cur, _, nxt, wspec, _ = _conv_specs(S, T)
    nt = S // T

    def body(dc_ref, dn_ref, w_ref, o_ref):
        d = dc_ref[...].astype(F32)
        dn = dn_ref[...].astype(F32) * jnp.where(pl.program_id(1) == nt - 1, 0.0, 1.0)
        o_ref[...] = (w_ref[2:3, :] * d + w_ref[1:2, :] * _shift_up(d, dn, 1) + w_ref[0:1, :] * _shift_up(d, dn, 2)).astype(BF16)

    return pl.pallas_call(
        body, name=name, grid=(nj, nt), in_specs=[cur, nxt, wspec], out_specs=cur,
        out_shape=jax.ShapeDtypeStruct((S, DFF), BF16), compiler_params=_cparams(("parallel", "parallel")),
    )(dc, dc, w)


def _adam_math(g, w, m, v):
    m = B1 * m + (1.0 - B1) * g
    v = B2 * v + (1.0 - B2) * (g * g)
    m_hat = m / (1.0 - B1 ** STEP)
    v_hat = v / (1.0 - B2 ** STEP)
    return -LR * (m_hat / (jnp.sqrt(v_hat) + EPS) + WD * w), m, v


def _grad_sum(own_ref, far_ref):
    g = own_ref[...]
    for s in range(far_ref.shape[0]):
        g = g + far_ref[s].astype(F32)
    return g


def _adamw_reduced(own, far, w, m, v, name):
    L, R, C = w.shape
    tr = _tile(R, 256, 16)

    def body(own_ref, far_ref, w_ref, m_ref, v_ref, g_ref, d_ref, nm_ref, nv_ref):
        g = _grad_sum(own_ref, far_ref)
        g_ref[...] = g
        d_ref[...], nm_ref[...], nv_ref[...] = _adam_math(g, w_ref[...], m_ref[...], v_ref[...])

    blk = pl.BlockSpec((None, tr, C), lambda l, i: (l, i, 0))
    far_blk = pl.BlockSpec((None, 3, tr, C), lambda l, i: (l, 0, i, 0))
    return pl.pallas_call(
        body, name=name, grid=(L, R // tr), in_specs=[blk, far_blk, blk, blk, blk], out_specs=[blk] * 4,
        out_shape=[jax.ShapeDtypeStruct(w.shape, F32)] * 4, compiler_params=_cparams(("parallel", "parallel")),
    )(own, far, w, m, v)


def _reduced(own, far, name):
    L, R, C = own.shape
    tr = _tile(R, 256, 16)

    def body(own_ref, far_ref, g_ref):
        g_ref[...] = _grad_sum(own_ref, far_ref)

    blk = pl.BlockSpec((None, tr, C), lambda l, i: (l, i, 0))
    far_blk = pl.BlockSpec((None, 3, tr, C), lambda l, i: (l, 0, i, 0))
    return pl.pallas_call(
        body, name=name, grid=(L, R // tr), in_specs=[blk, far_blk], out_specs=blk,
        out_shape=jax.ShapeDtypeStruct(own.shape, F32), compiler_params=_cparams(("parallel", "parallel")),
    )(own, far)


def _adamw(g, w, m, v, name):
    R, C = g.shape
    tr = _tile(R, 128, 8)

    def body(g_ref, w_ref, m_ref, v_ref, d_ref, nm_ref, nv_ref):
        d_ref[...], nm_ref[...], nv_ref[...] = _adam_math(g_ref[...], w_ref[...], m_ref[...], v_ref[...])

    blk = pl.BlockSpec((tr, C), lambda i: (i, 0))
    return pl.pallas_call(
        body, name=name, grid=(R // tr,), in_specs=[blk] * 4, out_specs=[blk] * 3,
        out_shape=[jax.ShapeDtypeStruct(g.shape, F32)] * 3, compiler_params=_cparams(("parallel",)),
    )(g, w, m, v)


def _pair_sum(x, name):
    _, R, C = x.shape
    tr = _tile(R, 512, 16)

    def body(x_ref, o32_ref, o16_ref):
        s = x_ref[0] + x_ref[1]
        o32_ref[...] = s
        o16_ref[...] = s.astype(BF16)

    blk = pl.BlockSpec((tr, C), lambda i: (i, 0))
    return pl.pallas_call(
        body, name=name, grid=(R // tr,), in_specs=[pl.BlockSpec((2, tr, C), lambda i: (0, i, 0))], out_specs=[blk, blk],
        out_shape=[jax.ShapeDtypeStruct((R, C), F32), jax.ShapeDtypeStruct((R, C), BF16)],
        compiler_params=_cparams(("parallel",)),
    )(x)


def _sum_slots(x, name):
    def body(x_ref, o_ref):
        g = x_ref[0]
        for s in range(1, x.shape[0]):
            g = g + x_ref[s]
        o_ref[...] = g

    return pl.pallas_call(body, name=name, out_shape=jax.ShapeDtypeStruct(x.shape[1:], F32))(x)


MESH = pl.DeviceIdType.MESH
_HBM = pl.BlockSpec(memory_space=pltpu.HBM)


def _dma_sems(n):
    return pltpu.SemaphoreType.DMA((n,))


def _gather_many(xs, name):
    n = len(xs)

    def body(*refs):
        x_refs, out_refs = refs[:n], refs[n:2 * n]
        send_sems, recv_sems, local_sems = refs[2 * n:]
        ax, ay, ac = lax.axis_index("x"), lax.axis_index("y"), lax.axis_index("c")
        me, sibling = (ax, ay, ac), (ax, ay, 1 - ac)
        chips = [(1 - ax, ay), (ax, 1 - ay), (1 - ax, 1 - ay)]

        def copy(a, k, block, to, own=False):
            slot = out_refs[a].at[4 * block[0] + 2 * block[1] + block[2]]
            return pltpu.make_async_remote_copy(
                src_ref=x_refs[a] if own else slot, dst_ref=slot, send_sem=send_sems.at[7 * a + k],
                recv_sem=recv_sems.at[7 * a + k], device_id=to, device_id_type=MESH)

        mine = [pltpu.make_async_copy(x_refs[a], out_refs[a].at[4 * ax + 2 * ay + ac], local_sems.at[a]) for a in range(n)]
        first = [copy(a, 0, me, sibling, own=True) for a in range(n)]
        first += [copy(a, 1 + j, me, (*chip, ac), own=True) for j, chip in enumerate(chips) for a in range(n)]
        for cp in mine + first:
            cp.start()
        passed = []
        for j, chip in enumerate(chips):
            for a in range(n):
                copy(a, 1 + j, (*chip, ac), me).wait_recv()
                cp = copy(a, 4 + j, (*chip, ac), sibling)
                cp.start()
                passed.append(cp)
        for a in range(n):
            copy(a, 0, sibling, me).wait_recv()
            for j, chip in enumerate(chips):
                copy(a, 4 + j, (*chip, 1 - ac), me).wait_recv()
        for cp in first + passed:
            cp.wait_send()
        for cp in mine:
            cp.wait()

    return pl.pallas_call(
        body, name=name, out_shape=[jax.ShapeDtypeStruct((N_DEV,) + x.shape, x.dtype) for x in xs],
        in_specs=[_HBM] * n, out_specs=[_HBM] * n, scratch_shapes=[_dma_sems(7 * n), _dma_sems(7 * n), _dma_sems(n)],
    )(*xs)


def _exchange_cores(xs, name):
    n = len(xs)

    def body(*refs):
        x_refs, out_refs = refs[:n], refs[n:2 * n]
        send_sems, recv_sems, local_sems = refs[2 * n:]
        ax, ay, ac = lax.axis_index("x"), lax.axis_index("y"), lax.axis_index("c")
        local, remote = [], []
        for a in range(n):
            for q in range(4):
                local.append(pltpu.make_async_copy(x_refs[a].at[2 * q + ac], out_refs[a].at[0, q], local_sems.at[4 * a + q]))
                remote.append(pltpu.make_async_remote_copy(
                    src_ref=x_refs[a].at[2 * q + 1 - ac], dst_ref=out_refs[a].at[1, q], send_sem=send_sems.at[4 * a + q],
                    recv_sem=recv_sems.at[4 * a + q], device_id=(ax, ay, 1 - ac), device_id_type=MESH))
        for cp in remote + local:
            cp.start()
        for cp in remote:
            cp.wait_recv()
        for cp in remote:
            cp.wait_send()
        for cp in local:
            cp.wait()

    return pl.pallas_call(
        body, name=name, out_shape=[jax.ShapeDtypeStruct((2, 4) + x.shape[1:], x.dtype) for x in xs],
        in_specs=[_HBM] * n, out_specs=[_HBM] * n, scratch_shapes=[_dma_sems(4 * n), _dma_sems(4 * n), _dma_sems(4 * n)],
    )(*xs)


def _exchange_chips(p16s, p32s, name):
    nw, L = len(p16s), len(p16s[0])

    def body(*refs):
        it = iter(refs)
        p16 = [[next(it) for _ in range(L)] for _ in range(nw)]
        p32 = [[next(it) for _ in range(L)] for _ in range(nw)]
        own = [next(it) for _ in range(nw)]
        far = [next(it) for _ in range(nw)]
        send_sems, recv_sems, local_sems = next(it), next(it), next(it)
        ax, ay, ac = lax.axis_index("x"), lax.axis_index("y"), lax.axis_index("c")
        local, remote = [], []
        for a in range(nw):
            for l in range(L):
                local.append(pltpu.make_async_copy(p32[a][l].at[2 * ax + ay], own[a].at[l], local_sems.at[L * a + l]))
                for k in range(1, 4):
                    px = 1 - ax if k & 2 else ax
                    py = 1 - ay if k & 1 else ay
                    s = 3 * (L * a + l) + k - 1
                    remote.append(pltpu.make_async_remote_copy(
                        src_ref=p16[a][l].at[2 * px + py], dst_ref=far[a].at[l, k - 1], send_sem=send_sems.at[s],
                        recv_sem=recv_sems.at[s], device_id=(px, py, ac), device_id_type=MESH))
        for cp in remote + local:
            cp.start()
        for cp in remote:
            cp.wait_recv()
        for cp in remote:
            cp.wait_send()
        for cp in local:
            cp.wait()

    flat16 = [t for ts in p16s for t in ts]
    flat32 = [t for ts in p32s for t in ts]
    shapes = [ts[0].shape[1:] for ts in p16s]
    outs = pl.pallas_call(
        body, name=name,
        out_shape=[jax.ShapeDtypeStruct((L,) + s, F32) for s in shapes] + [jax.ShapeDtypeStruct((L, 3) + s, BF16) for s in shapes],
        in_specs=[_HBM] * (2 * nw * L), out_specs=[_HBM] * (2 * nw),
        scratch_shapes=[_dma_sems(3 * nw * L), _dma_sems(3 * nw * L), _dma_sems(nw * L)],
    )(*flat16, *flat32)
    return outs[:nw], outs[nw:]


def _x_view(xb, d):
    return xb if d == 1 else xb.reshape(xb.shape[0] // d, d * xb.shape[1])


def _layer_fwd(x, xb, p, w, cos, sin, rconsts):
    S = x.shape[0]
    proj = _mm(xb, w["win_rest"], tb=True, name="mm_proj", out_dtype=BF16)
    qkvs, ogs, lgs = [], [], []
    for g, dil in enumerate(DILATIONS):
        qkv = _mm(_x_view(xb, dil), w["win_att"][g], tb=True, a_res=dil, name=f"mm_qkv{g}", out_dtype=BF16)
        o, l = _attn_fwd(qkv, dil, f"attn_fwd_g{g}")
        qkvs.append(qkv)
        ogs.append(_to_tokens(o, dil))
        lgs.append(_to_tokens(l, dil))
    attn, lse = _rowwise(_f_combine, ogs + lgs, [], [(D, BF16), (HD, F32)], [], name="attn_combine")
    ret_raw, states = _ret_fwd(proj, cos, sin, rconsts)
    rg_win = (proj, RH * RDV, OFF_RG // (RH * RDV))
    ga_win, gr_win = (proj, D, OFF_GA // D), (proj, D, OFF_GR // D)
    (r,) = _rowwise(_f_gn, [ret_raw, rg_win], [w["ret_gn_g"], w["ret_gn_b"]], [(RH * RDV, BF16)], [], name="gn_fwd")
    ap = _mm(attn, w["w_attn_proj"], name="mm_attn_proj", out_dtype=BF16)
    rp = _mm(r, w["w_ret_proj"], name="mm_ret_proj", out_dtype=BF16)
    (merged,) = _rowwise(_f_gate, [ap, rp, ga_win, gr_win], [], [(D, BF16)], [], name="gate_fwd")
    mix = _mm(merged, w["w_out"], name="mm_out")
    h1, x1, x1b = _rowwise(_f_ln1, [x, mix], [w["ln1_g"], w["ln1_b"]], [(D, F32), (D, F32), (D, BF16)], [], name="ln1_fwd")
    z = _mm(x1b, w["w_ple_gate"], name="mm_ple_gate")
    pp = _mm(p, w["w_ple_proj"], tb=True, name="mm_ple_proj")
    hg = _mm(x1b, w["w_upg"], tb=True, name="mm_up_g", out_dtype=BF16)
    hu = _mm(x1b, w["w_upu"], tb=True, name="mm_up_u", out_dtype=BF16)
    act = _conv_fwd(hg, hu, w["conv_wg"], w["conv_wu"], w["conv_bg"], w["conv_bu"])
    ffn = _mm(act, w["w_down"], name="mm_down")
    h2, x2, x2b = _rowwise(_f_ln2, [x1, ffn, z, pp], [w["ln2_g"], w["ln2_b"]], [(D, F32), (D, F32), (D, BF16)], [], name="ln2_fwd")
    saved = dict(xb=xb, proj=proj, qkvs=qkvs, attn=attn, lse=lse, ret_raw=ret_raw, states=states, r=r, ap=ap, rp=rp,
                 merged=merged, h1=h1, x1b=x1b, z=z, pp=pp, hg=hg, hu=hu, act=act, h2=h2, p=p)
    return x2, x2b, saved


def _layer_bwd(dys, w, sv, cos, sin, rconsts):
    gr = {}
    proj = sv["proj"]
    dh2, dh2b, gr["ln2_g"], gr["ln2_b"] = _rowwise(_f_ln_bwd, list(dys) + [sv["h2"]], [w["ln2_g"]], [(D, F32), (D, BF16)],
                                                   [(1, D), (1, D)], name="ln2_bwd")
    d_act = _mm(dh2b, w["w_down"], tb=True, name="mm_down_dx", out_dtype=BF16)
    gr["w_down"] = _mm(sv["act"], dh2b, ta=True, name="mm_down_dw")
    dcg, dcu, gwg, gwu, gbg, gbu = _conv_bwd_pre(d_act, sv["hg"], sv["hu"], w["conv_wg"], w["conv_wu"], w["conv_bg"], w["conv_bu"])
    gr["conv_w"] = jnp.concatenate([gwg, gwu], axis=1)
    gr["conv_b"] = jnp.concatenate([gbg, gbu], axis=1)
    dhg = _conv_bwd_in(dcg, w["conv_wg"], "conv_bwd_in_g")
    dhu = _conv_bwd_in(dcu, w["conv_wu"], "conv_bwd_in_u")
    gr["w_up"] = jnp.concatenate([_mm(dhg, sv["x1b"], ta=True, name="mm_up_g_dw"), _mm(dhu, sv["x1b"], ta=True, name="mm_up_u_dw")], axis=0)
    dx1 = _mm(dhg, w["w_upg"], name="mm_up_g_dx", add=dh2, add_scale=ALPHA)
    dx1 = _mm(dhu, w["w_upu"], name="mm_up_u_dx", add=dx1)
    dpp, dz = _rowwise(_f_ple_bwd, [dh2, sv["z"], sv["pp"]], [], [(D, BF16), (D, BF16)], [], name="ple_bwd")
    gr["w_ple_proj"] = _mm(dpp, sv["p"], ta=True, name="mm_ple_proj_dw")
    gr["w_ple_gate"] = _mm(sv["x1b"], dz, ta=True, name="mm_ple_gate_dw")
    dx1 = _mm(dz, w["w_ple_gate"], tb=True, name="mm_ple_gate_dx", add=dx1)
    dh1, dh1b, gr["ln1_g"], gr["ln1_b"] = _rowwise(_f_ln_bwd, [dx1, sv["h1"]], [w["ln1_g"]], [(D, F32), (D, BF16)],
                                                   [(1, D), (1, D)], name="ln1_bwd")
    d_merged = _mm(dh1b, w["w_out"], tb=True, name="mm_out_dx", out_dtype=BF16)
    gr["w_out"] = _mm(sv["merged"], dh1b, ta=True, name="mm_out_dw")
    rg_win = (proj, RH * RDV, OFF_RG // (RH * RDV))
    ga_win, gr_win = (proj, D, OFF_GA // D), (proj, D, OFF_GR // D)
    dap, drp, dga, dgr = _rowwise(_f_gate_bwd, [d_merged, sv["ap"], sv["rp"], ga_win, gr_win], [], [(D, BF16)] * 4, [], name="gate_bwd")
    d_attn = _mm(dap, w["w_attn_proj"], tb=True, name="mm_attn_proj_dx", out_dtype=BF16)
    gr["w_attn_proj"] = _mm(sv["attn"], dap, ta=True, name="mm_attn_proj_dw")
    d_r = _mm(drp, w["w_ret_proj"], tb=True, name="mm_ret_proj_dx", out_dtype=BF16)
    gr["w_ret_proj"] = _mm(sv["r"], drp, ta=True, name="mm_ret_proj_dw")
    d_ret, d_rg, gr["ret_gn_g"], gr["ret_gn_b"] = _rowwise(
        _f_gn_bwd, [d_r, sv["ret_raw"], rg_win], [w["ret_gn_g"], w["ret_gn_b"]], [(RH * RDV, BF16)] * 2,
        [(1, RH * RDV), (1, RH * RDV)], name="gn_bwd")
    d_rq, d_rk, d_rv = _ret_bwd(proj, cos, sin, rconsts, sv["states"], d_ret)
    d_rest = jnp.concatenate([d_rq, d_rk, d_rv, d_rg, dga, dgr], axis=1)
    gw_rest = _mm(d_rest, sv["xb"], ta=True, name="mm_proj_dw")
    dx0 = _mm(d_rest, w["win_rest"], name="mm_proj_dx", add=dh1, add_scale=ALPHA)
    (delta,) = _rowwise(_f_delta, [d_attn, sv["attn"]], [], [(HD, F32)], [], name="attn_delta")
    gw_att, dx_parts = [], []
    for g, dil in enumerate(DILATIONS):
        dqkv = _attn_bwd(sv["qkvs"][g], _to_residues(d_attn, dil), _to_residues(sv["lse"], dil), _to_residues(delta, dil),
                         dil, f"attn_bwd_g{g}")
        gw_att.append(_mm(dqkv, _x_view(sv["xb"], dil), ta=True, b_res=dil, name=f"mm_qkv{g}_dw"))
        if dil == 1:
            dx0 = _mm(dqkv, w["win_att"][g], name=f"mm_qkv{g}_dx", add=dx0)
        else:
            dx_parts.append(_mm(dqkv, w["win_att"][g], o_res=dil, name=f"mm_qkv{g}_dx").reshape(dx0.shape))
    gw_att = jnp.stack(gw_att).reshape(3, NH, 3, HD, D).transpose(0, 2, 1, 3, 4).reshape(N_ATT, D)
    gr["w_in"] = jnp.concatenate([gw_att, gw_rest], axis=0)
    return [dx0] + dx_parts, gr


def _local_step(x, p, positions, target, ws):
    half = RDK // 2
    freq = jnp.power(ROPE_BASE, -jnp.arange(half, dtype=F32) / half)
    ang = positions.astype(F32)[:, None] * freq[None, :]
    cos, sin = jnp.cos(ang), jnp.sin(ang)
    rconsts = _ret_consts()
    xb = x.astype(BF16)
    saved = []
    for l in range(DEPTH):
        x, xb, sv = _layer_fwd(x, xb, p[l], ws[l], cos, sin, rconsts)
        saved.append(sv)
    dy, loss_vec = _rowwise(_f_loss, [x, target], [], [(D, F32)], [(1, D)], name="loss")
    dys, grads = [dy], [None] * DEPTH
    for l in reversed(range(DEPTH)):
        dys, grads[l] = _layer_bwd(dys, ws[l], saved[l], cos, sin, rconsts)
    (grad_x,) = _rowwise(_f_sum, dys, [], [(D, F32)], [], name="grad_x_sum")
    return loss_vec, grad_x, grads


def _pack_rows(arrs):
    parts, where, off = [], [], 0
    for t in arrs:
        t = t.reshape(-1, D)
        rows = t.shape[0]
        padded = -(-rows // 8) * 8
        parts.append(jnp.pad(t, ((0, padded - rows), (0, 0))))
        where.append((off, rows))
        off += padded
    return jnp.concatenate(parts, axis=0), where


def _layer_weights(g, l, conv_w_all, conv_b, W):
    win = g["w_in"].reshape(N_IN, D)
    att = win[:N_ATT].reshape(3, 3, NH, HD, D).transpose(0, 2, 1, 3, 4).reshape(3, 3 * D, D)
    up = g["w_up"].reshape(2 * DFF, D)
    w = dict(win_att=[att[0], att[1], att[2]], win_rest=win[N_ATT:], w_upg=up[:DFF], w_upu=up[DFF:],
             w_ple_proj=g["w_ple_proj"].reshape(D, PLE), w_attn_proj=g["w_attn_proj"].reshape(D, D),
             w_ret_proj=g["w_ret_proj"].reshape(RH * RDV, D), w_out=g["w_out"].reshape(D, D),
             w_down=g["w_down"].reshape(DFF, D), w_ple_gate=g["w_ple_gate"].reshape(D, D))
    w["conv_wg"], w["conv_wu"] = conv_w_all[l][:, :DFF], conv_w_all[l][:, DFF:]
    w["conv_bg"], w["conv_bu"] = conv_b[l][None, :DFF], conv_b[l][None, DFF:]
    for n in ("ret_gn_g", "ret_gn_b", "ln1_g", "ln1_b", "ln2_g", "ln2_b"):
        w[n] = W[n][l][None, :]
    return w


def kernel(x, p, positions, w_in, w_attn_proj, w_ret_proj, ret_gn_g, ret_gn_b, w_out, ln1_g, ln1_b, w_up, conv_w, conv_b, w_down, w_ple_gate, w_ple_proj, ln2_g, ln2_b, loss_target, m_w_in, m_w_attn_proj, m_w_ret_proj, m_ret_gn_g, m_ret_gn_b, m_w_out, m_ln1_g, m_ln1_b, m_w_up, m_conv_w, m_conv_b, m_w_down, m_w_ple_gate, m_w_ple_proj, m_ln2_g, m_ln2_b, v_w_in, v_w_attn_proj, v_w_ret_proj, v_ret_gn_g, v_ret_gn_b, v_w_out, v_ln1_g, v_ln1_b, v_w_up, v_conv_w, v_conv_b, v_w_down, v_w_ple_gate, v_w_ple_proj, v_ln2_g, v_ln2_b):
    W = dict(w_in=w_in, w_attn_proj=w_attn_proj, w_ret_proj=w_ret_proj, ret_gn_g=ret_gn_g, ret_gn_b=ret_gn_b, w_out=w_out,
             ln1_g=ln1_g, ln1_b=ln1_b, w_up=w_up, conv_w=conv_w, conv_b=conv_b, w_down=w_down, w_ple_gate=w_ple_gate,
             w_ple_proj=w_ple_proj, ln2_g=ln2_g, ln2_b=ln2_b)
    M = dict(w_in=m_w_in, w_attn_proj=m_w_attn_proj, w_ret_proj=m_w_ret_proj, ret_gn_g=m_ret_gn_g, ret_gn_b=m_ret_gn_b,
             w_out=m_w_out, ln1_g=m_ln1_g, ln1_b=m_ln1_b, w_up=m_w_up, conv_w=m_conv_w, conv_b=m_conv_b, w_down=m_w_down,
             w_ple_gate=m_w_ple_gate, w_ple_proj=m_w_ple_proj, ln2_g=m_ln2_g, ln2_b=m_ln2_b)
    V = dict(w_in=v_w_in, w_attn_proj=v_w_attn_proj, w_ret_proj=v_w_ret_proj, ret_gn_g=v_ret_gn_g, ret_gn_b=v_ret_gn_b,
             w_out=v_w_out, ln1_g=v_ln1_g, ln1_b=v_ln1_b, w_up=v_w_up, conv_w=v_conv_w, conv_b=v_conv_b, w_down=v_w_down,
             w_ple_gate=v_w_ple_gate, w_ple_proj=v_w_ple_proj, ln2_g=v_ln2_g, ln2_b=v_ln2_b)

    ws, conv_w_all = [], None
    for l in range(DEPTH):
        shards = [(W[n][l].T if n in COL_SHARDED else W[n][l]).astype(BF16) for n in BIG]
        outs = _gather_many(shards + ([conv_w] if l == 0 else []), f"gather_weights_l{l}")
        if l == 0:
            conv_w_all = outs[-1].transpose(1, 2, 0, 3).reshape(DEPTH, 3, 2 * DFF)
        ws.append(_layer_weights(dict(zip(BIG, outs)), l, conv_w_all, conv_b, W))

    loss_vec, grad_x, grads = _local_step(x[0], p[:, 0], positions[0], loss_target[0], ws)
    loss = lax.psum(jnp.sum(loss_vec), ("x", "y", "c"))

    parts = _exchange_cores([grads[l][n].reshape((N_DEV, -1) + grads[l][n].shape[1:]) for n in BIG for l in range(DEPTH)],
                            "exchange_grads_cores")
    sums = [_pair_sum(t.reshape(2, -1, t.shape[-1]), f"pair_sum_{i}") for i, t in enumerate(parts)]
    chip32 = [[sums[DEPTH * a + l][0].reshape(parts[DEPTH * a + l].shape[1:]) for l in range(DEPTH)] for a in range(len(BIG))]
    chip16 = [[sums[DEPTH * a + l][1].reshape(parts[DEPTH * a + l].shape[1:]) for l in range(DEPTH)] for a in range(len(BIG))]
    owns, fars = _exchange_chips(chip16, chip32, "exchange_grads_chips")
    G, DW, NM, NV = ({} for _ in range(4))
    for n, own, far in zip(BIG, owns, fars):
        if n in COL_SHARDED:
            G[n] = _reduced(own, far, f"reduced_{n}").transpose(0, 2, 1)
            R2, C2 = DEPTH * W[n].shape[1], W[n].shape[2]
            res = _adamw(*(t.reshape(R2, C2) for t in (G[n], W[n], M[n], V[n])), f"adamw_{n}")
            DW[n], NM[n], NV[n] = (t.reshape(W[n].shape) for t in res)
        else:
            G[n], DW[n], NM[n], NV[n] = _adamw_reduced(own, far, W[n], M[n], V[n], f"adamw_{n}")

    small_names = SMALL + ("conv_w",)
    g_small, where = _pack_rows([jnp.stack([grads[l][n] for l in range(DEPTH)]) for n in small_names])
    (g_all,) = _gather_many([g_small], "gather_small_grads")
    g_small = _sum_slots(g_all, "sum_small_grads")
    for n, (off, rows) in zip(SMALL, where):
        G[n] = g_small[off:off + rows].reshape(W[n].shape)
    off, rows = where[-1]
    g_cw = g_small[off:off + rows].reshape(DEPTH, 3, N_DEV, conv_w.shape[2])
    me = 4 * lax.axis_index("x") + 2 * lax.axis_index("y") + lax.axis_index("c")
    G["conv_w"] = lax.dynamic_index_in_dim(g_cw, me, axis=2, keepdims=False)
    packed = [_pack_rows([d[n] for n in SMALL]) for d in (G, W, M, V)]
    small_out = _adamw(*(t for t, _ in packed), "adamw_small")
    for res, dst in zip(small_out, (DW, NM, NV)):
        for n, (off, rows) in zip(SMALL, packed[0][1]):
            dst[n] = res[off:off + rows].reshape(W[n].shape)
    two_d = lambda t: t.reshape(DEPTH * 3, conv_w.shape[2])
    cw_out = _adamw(two_d(G["conv_w"]), two_d(conv_w), two_d(m_conv_w), two_d(v_conv_w), "adamw_conv_w")
    for res, dst in zip(cw_out, (DW, NM, NV)):
        dst["conv_w"] = res.reshape(conv_w.shape)

    return (loss, grad_x[None], *[G[n] for n in WEIGHTS], *[DW[n] for n in WEIGHTS], *[NM[n] for n in WEIGHTS],
            *[NV[n] for n in WEIGHTS])
```

```python
import math

import numpy as np
import jax
import jax.numpy as jnp
from jax import lax
from jax.experimental import pallas as pl
from jax.experimental.pallas import tpu as pltpu

F32, BF16 = jnp.float32, jnp.bfloat16

D = 1024
DEPTH = 2
N_DEV = 8
HD = 128
NH = 8
DILATIONS = (1, 4, 16)
SPAN = 128
N_ATT = 3 * 3 * NH * HD
RH, RDK, RDV = 4, 256, 512
CH = 128
DFF = 2816
PLE = 256
N_IN = 17408
N_REST = N_IN - N_ATT
OFF_RQ, OFF_RK, OFF_RV, OFF_RG, OFF_GA, OFF_GR = 0, 1024, 2048, 4096, 6144, 7168
ALPHA = (2 * DEPTH) ** 0.25
LN_EPS, GN_EPS = 1e-5, 1e-6
ROPE_BASE = 10000.0
LR, B1, B2, EPS, WD, STEP = 0.001, 0.9, 0.999, 1e-8, 0.01, 10
VMEM_LIMIT = 48 * 1024 * 1024
NEG = -1e30

BIG = ("w_in", "w_attn_proj", "w_ret_proj", "w_out", "w_up", "w_down", "w_ple_gate", "w_ple_proj")
COL_SHARDED = ("w_in", "w_up", "w_ple_proj")
SMALL = ("ret_gn_g", "ret_gn_b", "ln1_g", "ln1_b", "conv_b", "ln2_g", "ln2_b")
WEIGHTS = ("w_in", "w_attn_proj", "w_ret_proj", "ret_gn_g", "ret_gn_b", "w_out", "ln1_g", "ln1_b", "w_up",
           "conv_w", "conv_b", "w_down", "w_ple_gate", "w_ple_proj", "ln2_g", "ln2_b")


def _tile(n, cap, mult=128):
    if n <= cap:
        return n
    t = (cap // mult) * mult
    while n % t:
        t -= mult
    return t


def _cparams(sem):
    return pltpu.CompilerParams(dimension_semantics=sem, vmem_limit_bytes=VMEM_LIMIT)


def _dot(a, b, ca, cb):
    return lax.dot_general(a, b, (((ca,), (cb,)), ((), ())), preferred_element_type=F32)


def _bdot(a, b, ca, cb):
    return lax.dot_general(a, b, (((ca,), (cb,)), ((0,), (0,))), preferred_element_type=F32)


def _mm(a, b, *, name, ta=False, tb=False, out_dtype=F32, add=None, add_scale=1.0, tm=1024, tn=1024, tk=1024):
    M, K = (a.shape[1], a.shape[0]) if ta else a.shape
    N = b.shape[0] if tb else b.shape[1]
    assert K == (b.shape[1] if tb else b.shape[0])
    tm, tn, tk = _tile(M, tm), _tile(N, tn), _tile(K, tk)
    nk = K // tk

    def body(*refs):
        if add is None:
            a_ref, b_ref, o_ref, acc_ref = refs
        else:
            a_ref, b_ref, add_ref, o_ref, acc_ref = refs
        k = pl.program_id(2)

        @pl.when(k == 0)
        def _():
            acc_ref[...] = jnp.zeros_like(acc_ref)

        acc_ref[...] += _dot(a_ref[...].astype(BF16), b_ref[...].astype(BF16), 0 if ta else 1, 1 if tb else 0)

        @pl.when(k == nk - 1)
        def _():
            r = acc_ref[...]
            if add is not None:
                r = r + add_scale * add_ref[...].astype(F32)
            o_ref[...] = r.astype(out_dtype)

    a_spec = pl.BlockSpec((tk, tm), lambda i, j, k: (k, i)) if ta else pl.BlockSpec((tm, tk), lambda i, j, k: (i, k))
    b_spec = pl.BlockSpec((tn, tk), lambda i, j, k: (j, k)) if tb else pl.BlockSpec((tk, tn), lambda i, j, k: (k, j))
    o_spec = pl.BlockSpec((tm, tn), lambda i, j, k: (i, j))
    in_specs, args = [a_spec, b_spec], [a, b]
    if add is not None:
        in_specs.append(o_spec)
        args.append(add)
    return pl.pallas_call(
        body, name=name, grid=(M // tm, N // tn, nk), in_specs=in_specs, out_specs=o_spec,
        out_shape=jax.ShapeDtypeStruct((M, N), out_dtype), scratch_shapes=[pltpu.VMEM((tm, tn), F32)],
        compiler_params=_cparams(("parallel", "parallel", "arbitrary")),
    )(*args)


def _rowwise(fn, rows, pars, outs, accs, *, name, tm=512):
    first = rows[0][0] if isinstance(rows[0], tuple) else rows[0]
    S = first.shape[-2]
    tm = _tile(S, tm, 16)
    n_r, n_p, n_o = len(rows), len(pars), len(outs)

    def body(*refs):
        i = pl.program_id(0)
        vals = [r[...] for r in refs[:n_r + n_p]]
        res = fn(*vals)
        if not isinstance(res, (tuple, list)):
            res = (res,)
        o_refs = refs[n_r + n_p:n_r + n_p + n_o]
        a_refs = refs[n_r + n_p + n_o:]
        for r, v in zip(o_refs, res[:n_o]):
            r[...] = v.astype(r.dtype)
        if a_refs:
            @pl.when(i == 0)
            def _():
                for r in a_refs:
                    r[...] = jnp.zeros_like(r)

            for r, v in zip(a_refs, res[n_o:]):
                r[...] += v

    in_specs, args = [], []
    for r in rows:
        if isinstance(r, tuple):
            arr, w, cb = r
            in_specs.append(pl.BlockSpec((tm, w), lambda i, cb=cb: (i, cb)))
        elif r.ndim == 3:
            arr = r
            in_specs.append(pl.BlockSpec((arr.shape[0], tm, arr.shape[2]), lambda i: (0, i, 0)))
        else:
            arr = r
            in_specs.append(pl.BlockSpec((tm, arr.shape[1]), lambda i: (i, 0)))
        args.append(arr)
    for p_ in pars:
        in_specs.append(pl.BlockSpec(p_.shape, lambda i: (0, 0)))
        args.append(p_)
    out_shape = [jax.ShapeDtypeStruct((S, w), dt) for w, dt in outs] + [jax.ShapeDtypeStruct(a, F32) for a in accs]
    out_specs = [pl.BlockSpec((tm, w), lambda i: (i, 0)) for w, _ in outs] + [pl.BlockSpec(a, lambda i: (0, 0)) for a in accs]
    return pl.pallas_call(
        body, name=name, grid=(S // tm,), in_specs=in_specs, out_specs=out_specs, out_shape=out_shape,
        compiler_params=_cparams(("arbitrary",) if accs else ("parallel",)),
    )(*args)


def _norm(h, eps):
    mu = jnp.mean(h, -1, keepdims=True)
    d = h - mu
    rstd = lax.rsqrt(jnp.mean(d * d, -1, keepdims=True) + eps)
    return d * rstd, rstd


def _norm_bwd(dxh, xh, rstd):
    return rstd * (dxh - jnp.mean(dxh, -1, keepdims=True) - xh * jnp.mean(dxh * xh, -1, keepdims=True))


def _sig(x):
    return 1.0 / (1.0 + jnp.exp(-x))


_GELU_C = math.sqrt(2.0 / math.pi)


def _gelu(x):
    t = jnp.tanh(_GELU_C * (x + 0.044715 * x * x * x))
    return 0.5 * x * (1.0 + t), t


def _gelu_grad(x, t):
    return 0.5 * (1.0 + t) + 0.5 * x * (1.0 - t * t) * _GELU_C * (1.0 + 3 * 0.044715 * x * x)


def _f_ln1(x, mix, g, b):
    h = ALPHA * x + mix
    xh, _ = _norm(h, LN_EPS)
    y = xh * g + b
    return h, y, y


def _f_ln2(x, ffn, z, pp, g, b):
    h = ALPHA * x + ffn + _sig(z) * pp
    xh, _ = _norm(h, LN_EPS)
    y = xh * g + b
    return h, y, y


def _f_ln_bwd(*args):
    *dys, h, g = args
    dy = dys[0]
    for t in dys[1:]:
        dy = dy + t
    xh, rstd = _norm(h, LN_EPS)
    dh = _norm_bwd(dy * g, xh, rstd)
    return dh, dh, jnp.sum(dy * xh, 0, keepdims=True), jnp.sum(dy, 0, keepdims=True)


def _f_sum(*ts):
    r = ts[0]
    for t in ts[1:]:
        r = r + t
    return r


def _f_loss(y, t):
    e = y - t
    return e * (1.0 / D), jnp.sum(e * e, 0, keepdims=True) * (0.5 / D)


def _head_col(c, h):
    lane = lax.broadcasted_iota(jnp.int32, c.shape, 1)
    return jnp.sum(jnp.where(lane == h, c, 0.0), -1, keepdims=True)


def _f_combine(o0, o1, o2, l0, l1, l2):
    lane = lax.broadcasted_iota(jnp.int32, l0.shape, 1)
    parts, lse = [], jnp.zeros(l0.shape, F32)
    for h in range(NH):
        a0, a1, a2 = _head_col(l0, h), _head_col(l1, h), _head_col(l2, h)
        m = jnp.maximum(jnp.maximum(a0, a1), a2)
        e0, e1, e2 = jnp.exp(a0 - m), jnp.exp(a1 - m), jnp.exp(a2 - m)
        den = e0 + e1 + e2
        parts.append((e0 * o0[h].astype(F32) + e1 * o1[h].astype(F32) + e2 * o2[h].astype(F32)) / den)
        lse = jnp.where(lane == h, m + jnp.log(den), lse)
    return jnp.concatenate(parts, axis=1), lse


def _f_delta(da, a):
    lane = lax.broadcasted_iota(jnp.int32, (da.shape[0], HD), 1)
    out = jnp.zeros((da.shape[0], HD), F32)
    for h in range(NH):
        sl = slice(h * HD, (h + 1) * HD)
        s = jnp.sum(da[:, sl].astype(F32) * a[:, sl].astype(F32), -1, keepdims=True)
        out = jnp.where(lane == h, s, out)
    return out


def _f_gate(ap, rp, ga, gr):
    return _sig(ga.astype(F32)) * ap.astype(F32) + _sig(gr.astype(F32)) * rp.astype(F32)


def _f_gate_bwd(dm, ap, rp, ga, gr):
    dm = dm.astype(F32)
    sa, sr = _sig(ga.astype(F32)), _sig(gr.astype(F32))
    return dm * sa, dm * sr, dm * ap.astype(F32) * sa * (1.0 - sa), dm * rp.astype(F32) * sr * (1.0 - sr)


def _f_gn(y, rg, g, b):
    y, rg = y.astype(F32), rg.astype(F32)
    parts = []
    for h in range(RH):
        sl = slice(h * RDV, (h + 1) * RDV)
        xh, _ = _norm(y[:, sl], GN_EPS)
        parts.append(xh * g[:, sl] + b[:, sl])
    return rg * _sig(rg) * jnp.concatenate(parts, axis=1)


def _f_gn_bwd(dr, y, rg, g, b):
    dr, y, rg = dr.astype(F32), y.astype(F32), rg.astype(F32)
    s = _sig(rg)
    d_out = dr * rg * s
    dys, outs, xhs = [], [], []
    for h in range(RH):
        sl = slice(h * RDV, (h + 1) * RDV)
        xh, rstd = _norm(y[:, sl], GN_EPS)
        xhs.append(xh)
        outs.append(xh * g[:, sl] + b[:, sl])
        dys.append(_norm_bwd(d_out[:, sl] * g[:, sl], xh, rstd))
    xh, out = jnp.concatenate(xhs, axis=1), jnp.concatenate(outs, axis=1)
    d_rg = dr * out * s * (1.0 + rg * (1.0 - s))
    return jnp.concatenate(dys, axis=1), d_rg, jnp.sum(d_out * xh, 0, keepdims=True), jnp.sum(d_out, 0, keepdims=True)


def _f_ple_bwd(dh, z, pp):
    s = _sig(z)
    return dh * s, dh * pp * s * (1.0 - s)


QKV = 3 * HD


def _to_tokens(t, d):
    if d == 1:
        return t
    *lead, S, C = t.shape
    n = len(lead)
    perm = tuple(range(n)) + (n + 1, n, n + 2)
    return t.reshape(*lead, d, S // d, C).transpose(perm).reshape(*lead, S, C)


def _to_residues(t, d):
    if d == 1:
        return t
    S, C = t.shape
    return t.reshape(S // d, d, C).transpose(1, 0, 2).reshape(S, C)


def _to_head_residues(t, d):
    S = t.shape[0]
    return t.reshape(S // d, d, NH, HD).transpose(2, 1, 0, 3).reshape(NH, S, HD)


def _qkv_fwd(xv, w, dil, name):
    Sd = xv.shape[0]
    S = Sd * dil
    tm = min(512, Sd)
    nma = Sd // tm

    def body(a_ref, b_ref, o_ref):
        r = _dot(a_ref[...], b_ref[...], 1, 1)
        for h in range(NH):
            o_ref[h] = r[:, h * QKV:(h + 1) * QKV].astype(BF16)

    return pl.pallas_call(
        body, name=name, grid=(S // tm,),
        in_specs=[pl.BlockSpec((tm, D), lambda i: (i % nma, i // nma)), pl.BlockSpec((NH * QKV, D), lambda i: (0, 0))],
        out_specs=pl.BlockSpec((NH, tm, QKV), lambda i: (0, i, 0)), out_shape=jax.ShapeDtypeStruct((NH, S, QKV), BF16),
        compiler_params=_cparams(("parallel",)),
    )(xv, w)


def _qkv_dx(dqkv, w, dil, name, add=None):
    S = dqkv.shape[1]
    Sd = S // dil
    tm = min(512, Sd)
    nmo = Sd // tm

    def body(*refs):
        a_ref, b_ref = refs[:2]
        o_ref = refs[-1]
        acc = _dot(a_ref[0], b_ref[0:QKV, :], 1, 0)
        for h in range(1, NH):
            acc = acc + _dot(a_ref[h], b_ref[h * QKV:(h + 1) * QKV, :], 1, 0)
        if add is not None:
            acc = acc + refs[2][...]
        o_ref[...] = acc

    o_spec = pl.BlockSpec((tm, D), lambda i: (i % nmo, i // nmo))
    in_specs = [pl.BlockSpec((NH, tm, QKV), lambda i: (0, i, 0)), pl.BlockSpec((NH * QKV, D), lambda i: (0, 0))]
    args = [dqkv, w]
    if add is not None:
        assert dil == 1
        in_specs.append(o_spec)
        args.append(add)
    return pl.pallas_call(
        body, name=name, grid=(S // tm,), in_specs=in_specs, out_specs=o_spec,
        out_shape=jax.ShapeDtypeStruct((Sd, dil * D), F32), compiler_params=_cparams(("parallel",)),
    )(*args)


def _qkv_dw(dqkv, xv, dil, name):
    S = dqkv.shape[1]
    Sd = S // dil
    tk = min(1024, Sd)
    nkb, nk = Sd // tk, S // tk
    hh = NH // 2

    def body(a_ref, b_ref, o_ref, acc_ref):
        k = pl.program_id(1)

        @pl.when(k == 0)
        def _():
            acc_ref[...] = jnp.zeros_like(acc_ref)

        b = b_ref[...]
        for h in range(hh):
            acc_ref[h * QKV:(h + 1) * QKV, :] += _dot(a_ref[h], b, 0, 0)

        @pl.when(k == nk - 1)
        def _():
            o_ref[...] = acc_ref[...]

    return pl.pallas_call(
        body, name=name, grid=(2, nk),
        in_specs=[pl.BlockSpec((hh, tk, QKV), lambda j, k: (j, k, 0)), pl.BlockSpec((tk, D), lambda j, k: (k % nkb, k // nkb))],
        out_specs=pl.BlockSpec((hh * QKV, D), lambda j, k: (j, 0)), out_shape=jax.ShapeDtypeStruct((NH * QKV, D), F32),
        scratch_shapes=[pltpu.VMEM((hh * QKV, D), F32)], compiler_params=_cparams(("parallel", "arbitrary")),
    )(dqkv, xv)


def _band(nb, first_valid, last_valid=None):
    b = lax.broadcasted_iota(jnp.int32, (nb, SPAN, SPAN), 0)
    row = lax.broadcasted_iota(jnp.int32, (nb, SPAN, SPAN), 1)
    col = lax.broadcasted_iota(jnp.int32, (nb, SPAN, SPAN), 2)
    off = jnp.where(b == 0, jnp.where(first_valid, 0, 2 * SPAN), 0)
    if last_valid is not None:
        off = off + jnp.where(b == nb - 1, jnp.where(last_valid, 0, 2 * SPAN), 0)
    return col <= row, col >= row + off


def _attn_tiles(S, dil):
    Sd = S // dil
    T = min(1024, Sd)
    return Sd, T, T // SPAN, Sd // T


def _attn_fwd(qkv, dil, name):
    S = qkv.shape[1]
    Sd, T, nsub, nib = _attn_tiles(S, dil)
    scale = HD ** -0.5

    def body(c_ref, p_ref, o_ref, l_ref):
        ib, h = pl.program_id(1), pl.program_id(2)
        blk, hal = c_ref[...], p_ref[...]
        q, k, v = blk[:, :HD], blk[:, HD:2 * HD], blk[:, 2 * HD:]
        if nsub > 1:
            kp = jnp.concatenate([hal[:, HD:2 * HD], k[:T - SPAN]], axis=0)
            vp = jnp.concatenate([hal[:, 2 * HD:], v[:T - SPAN]], axis=0)
        else:
            kp, vp = hal[:, HD:2 * HD], hal[:, 2 * HD:]
        q3, k3, v3, kp3, vp3 = (t.reshape(nsub, SPAN, HD) for t in (q, k, v, kp, vp))
        m_cur, m_prev = _band(nsub, ib > 0)
        sc = jnp.where(m_cur, _bdot(q3, k3, 2, 2) * scale, NEG)
        sp = jnp.where(m_prev, _bdot(q3, kp3, 2, 2) * scale, NEG)
        m = jnp.maximum(jnp.max(sc, -1, keepdims=True), jnp.max(sp, -1, keepdims=True))
        pc, pp = jnp.exp(sc - m), jnp.exp(sp - m)
        den = jnp.sum(pc, -1, keepdims=True) + jnp.sum(pp, -1, keepdims=True)
        o = (_bdot(pc.astype(BF16), v3, 2, 1) + _bdot(pp.astype(BF16), vp3, 2, 1)) / den
        o_ref[...] = o.reshape(T, HD).astype(BF16)
        lse = (m + jnp.log(den)).reshape(T, 1)
        lane = lax.broadcasted_iota(jnp.int32, (T, HD), 1)

        @pl.when(h == 0)
        def _():
            l_ref[...] = jnp.zeros_like(l_ref)

        l_ref[...] = jnp.where(lane == h, lse, l_ref[...])

    cur = pl.BlockSpec((None, T, QKV), lambda r, ib, h: (h, r * nib + ib, 0))
    prev = pl.BlockSpec((None, SPAN, QKV), lambda r, ib, h: (h, r * (Sd // SPAN) + jnp.maximum(ib * nsub - 1, 0), 0))
    return pl.pallas_call(
        body, name=name, grid=(dil, nib, NH), in_specs=[cur, prev],
        out_specs=[pl.BlockSpec((None, T, HD), lambda r, ib, h: (h, r * nib + ib, 0)),
                   pl.BlockSpec((T, HD), lambda r, ib, h: (r * nib + ib, 0))],
        out_shape=[jax.ShapeDtypeStruct((NH, S, HD), BF16), jax.ShapeDtypeStruct((S, HD), F32)],
        compiler_params=_cparams(("parallel", "parallel", "arbitrary")),
    )(qkv, qkv)


def _attn_bwd(qkv, d_attn, lse, delta, dil, name):
    S = qkv.shape[1]
    Sd, T, nsub, nib = _attn_tiles(S, dil)
    scale = HD ** -0.5
    ne = nsub + 1

    def body(c_ref, p_ref, n_ref, do_ref, don_ref, l_ref, ln_ref, dl_ref, dln_ref, o_ref):
        ib, h = pl.program_id(1), pl.program_id(2)
        blk, hal, nxt = c_ref[...], p_ref[...], n_ref[...]
        q, k, v = blk[:, :HD], blk[:, HD:2 * HD], blk[:, 2 * HD:]
        do = do_ref[...]
        l, dl = _head_col(l_ref[...], h), _head_col(dl_ref[...], h)
        qe = jnp.concatenate([q, nxt[:, :HD]], axis=0).reshape(ne, SPAN, HD)
        doe = jnp.concatenate([do, don_ref[...]], axis=0).reshape(ne, SPAN, HD)
        le = jnp.concatenate([l, _head_col(ln_ref[...], h)], axis=0).reshape(ne, SPAN, 1)
        dle = jnp.concatenate([dl, _head_col(dln_ref[...], h)], axis=0).reshape(ne, SPAN, 1)
        kpe = jnp.concatenate([hal[:, HD:2 * HD], k], axis=0).reshape(ne, SPAN, HD)
        vpe = jnp.concatenate([hal[:, 2 * HD:], v], axis=0).reshape(ne, SPAN, HD)
        _, m_prev = _band(ne, ib > 0, ib < nib - 1)
        p = jnp.where(m_prev, jnp.exp(_bdot(qe, kpe, 2, 2) * scale - le), 0.0)
        ds = (p * (_bdot(doe, vpe, 2, 2) - dle)).astype(BF16)
        dq = _bdot(ds, kpe, 2, 1)[:nsub]
        dk = _bdot(ds, qe, 1, 1)[1:]
        dv = _bdot(p.astype(BF16), doe, 1, 1)[1:]
        q3, k3, v3, do3 = (t.reshape(nsub, SPAN, HD) for t in (q, k, v, do))
        l3, dl3 = l.reshape(nsub, SPAN, 1), dl.reshape(nsub, SPAN, 1)
        m_cur, _ = _band(nsub, True)
        p = jnp.where(m_cur, jnp.exp(_bdot(q3, k3, 2, 2) * scale - l3), 0.0)
        ds = (p * (_bdot(do3, v3, 2, 2) - dl3)).astype(BF16)
        dq = (dq + _bdot(ds, k3, 2, 1)) * scale
        dk = (dk + _bdot(ds, q3, 1, 1)) * scale
        dv = dv + _bdot(p.astype(BF16), do3, 1, 1)
        o_ref[...] = jnp.concatenate([t.reshape(T, HD) for t in (dq, dk, dv)], axis=1).astype(BF16)

    nb = Sd // SPAN
    row = lambda r, ib: r * nib + ib
    prow = lambda r, ib: r * nb + jnp.maximum(ib * nsub - 1, 0)
    nrow = lambda r, ib: r * nb + jnp.minimum((ib + 1) * nsub, nb - 1)
    cur3 = pl.BlockSpec((None, T, QKV), lambda r, ib, h: (h, row(r, ib), 0))
    prev3 = pl.BlockSpec((None, SPAN, QKV), lambda r, ib, h: (h, prow(r, ib), 0))
    next3 = pl.BlockSpec((None, SPAN, QKV), lambda r, ib, h: (h, nrow(r, ib), 0))
    cur1 = pl.BlockSpec((None, T, HD), lambda r, ib, h: (h, row(r, ib), 0))
    next1 = pl.BlockSpec((None, SPAN, HD), lambda r, ib, h: (h, nrow(r, ib), 0))
    curc = pl.BlockSpec((T, HD), lambda r, ib, h: (row(r, ib), 0))
    nextc = pl.BlockSpec((SPAN, HD), lambda r, ib, h: (nrow(r, ib), 0))
    return pl.pallas_call(
        body, name=name, grid=(dil, nib, NH),
        in_specs=[cur3, prev3, next3, cur1, next1, curc, nextc, curc, nextc], out_specs=cur3,
        out_shape=jax.ShapeDtypeStruct((NH, S, QKV), BF16),
        compiler_params=_cparams(("parallel", "parallel", "parallel")),
    )(qkv, qkv, qkv, d_attn, d_attn, lse, lse, delta, delta)


def _ret_consts():
    lg = np.log1p(-np.exp2(-5.0 - np.arange(RH, dtype=np.float64)))
    idx = np.arange(CH, dtype=np.float64)
    rel = idx[:, None] - idx[None, :]
    intra = np.where(rel >= 0, np.exp(lg[:, None, None] * np.maximum(rel, 0.0)), 0.0)
    qd = np.exp(lg[:, None] * (idx + 1.0))
    kd = np.exp(lg[:, None] * (CH - 1.0 - idx))
    cd = np.exp(lg * CH)
    wide = lambda t: np.broadcast_to(t[:, :, None], (RH, t.shape[1], RDV))
    return (jnp.asarray(intra, F32), jnp.asarray(wide(qd), F32), jnp.asarray(wide(kd), F32),
            jnp.asarray(np.broadcast_to(cd[:, None, None], (RH, 1, RDV)), F32))


def _rot(t, c, s):
    t1, t2 = t[:, :RDK // 2], t[:, RDK // 2:]
    return jnp.concatenate([t1 * c - t2 * s, t1 * s + t2 * c], axis=1)


def _unrot(d, c, s):
    d1, d2 = d[:, :RDK // 2], d[:, RDK // 2:]
    return jnp.concatenate([d1 * c + d2 * s, d2 * c - d1 * s], axis=1)


RCH = 2


def _ret_specs(nmap):
    rows = RCH * CH
    q = pl.BlockSpec((rows, RH * RDK), lambda n: (nmap(n), OFF_RQ // (RH * RDK)))
    k = pl.BlockSpec((rows, RH * RDK), lambda n: (nmap(n), OFF_RK // (RH * RDK)))
    v = pl.BlockSpec((rows, RH * RDV), lambda n: (nmap(n), OFF_RV // (RH * RDV)))
    cs = pl.BlockSpec((rows, RDK // 2), lambda n: (nmap(n), 0))
    dmat = pl.BlockSpec((RH, CH, CH), lambda n: (0, 0, 0))
    dvec = pl.BlockSpec((RH, CH, RDV), lambda n: (0, 0, 0))
    cdv = pl.BlockSpec((RH, 1, RDV), lambda n: (0, 0, 0))
    state = pl.BlockSpec((RH, RCH, RDK, RDV), lambda n: (0, nmap(n), 0, 0))
    out = pl.BlockSpec((rows, RH * RDV), lambda n: (nmap(n), 0))
    return [q, k, v, cs, cs, dmat, dvec, dvec, cdv], state, out


def _ret_fwd(proj, cos, sin, consts):
    S = proj.shape[0]
    nc = S // CH

    def body(q_ref, k_ref, v_ref, c_ref, s_ref, d_ref, qd_ref, kd_ref, cd_ref, o_ref, st_ref, state):
        @pl.when(pl.program_id(0) == 0)
        def _():
            state[...] = jnp.zeros_like(state)

        for ci in range(RCH):
            rows = slice(ci * CH, (ci + 1) * CH)
            c, s = c_ref[rows, :], s_ref[rows, :]
            for h in range(RH):
                qk, vv = slice(h * RDK, (h + 1) * RDK), slice(h * RDV, (h + 1) * RDV)
                qb = _rot(q_ref[rows, qk].astype(F32), c, s).astype(BF16)
                kb = (_rot(k_ref[rows, qk].astype(F32), c, s) * (RDK ** -0.5)).astype(BF16)
                vb = v_ref[rows, vv]
                sb = state[h].astype(BF16)
                st_ref[h, ci] = sb
                a = (_dot(qb, kb, 1, 1) * d_ref[h]).astype(BF16)
                o_ref[rows, vv] = (_dot(a, vb, 1, 0) + _dot(qb, sb, 1, 0) * qd_ref[h]).astype(BF16)
                vk = (vb.astype(F32) * kd_ref[h]).astype(BF16)
                state[h] = cd_ref[h] * state[h] + _dot(kb, vk, 0, 0)

    ins, state_spec, out_spec = _ret_specs(lambda n: n)
    return pl.pallas_call(
        body, name="ret_fwd", grid=(nc // RCH,), in_specs=ins, out_specs=[out_spec, state_spec],
        out_shape=[jax.ShapeDtypeStruct((S, RH * RDV), BF16), jax.ShapeDtypeStruct((RH, nc, RDK, RDV), BF16)],
        scratch_shapes=[pltpu.VMEM((RH, RDK, RDV), F32)],
        compiler_params=_cparams(("arbitrary",)),
    )(proj, proj, proj, cos, sin, *consts)


def _ret_bwd(proj, cos, sin, consts, states, d_ret):
    S = proj.shape[0]
    nc = S // CH

    def body(q_ref, k_ref, v_ref, c_ref, s_ref, d_ref, qd_ref, kd_ref, cd_ref, st_ref, do_ref,
             dq_ref, dk_ref, dv_ref, dstate):
        @pl.when(pl.program_id(0) == 0)
        def _():
            dstate[...] = jnp.zeros_like(dstate)

        for ci in reversed(range(RCH)):
            rows = slice(ci * CH, (ci + 1) * CH)
            c, s = c_ref[rows, :], s_ref[rows, :]
            for h in range(RH):
                qk, vv = slice(h * RDK, (h + 1) * RDK), slice(h * RDV, (h + 1) * RDV)
                qb = _rot(q_ref[rows, qk].astype(F32), c, s).astype(BF16)
                kb = (_rot(k_ref[rows, qk].astype(F32), c, s) * (RDK ** -0.5)).astype(BF16)
                vb, sb, do = v_ref[rows, vv], st_ref[h, ci], do_ref[rows, vv]
                dmat, qd, kd = d_ref[h], qd_ref[h], kd_ref[h]
                a = (_dot(qb, kb, 1, 1) * dmat).astype(BF16)
                doq = (do.astype(F32) * qd).astype(BF16)
                dsb = dstate[h].astype(BF16)
                vk = (vb.astype(F32) * kd).astype(BF16)
                dv_ref[rows, vv] = (_dot(a, do, 0, 0) + _dot(kb, dsb, 1, 0) * kd).astype(BF16)
                da = (_dot(do, vb, 1, 1) * dmat).astype(BF16)
                dq = _dot(da, kb, 1, 0) + _dot(doq, sb, 1, 1)
                dk = (_dot(da, qb, 0, 0) + _dot(vk, dsb, 1, 1)) * (RDK ** -0.5)
                dq_ref[rows, qk] = _unrot(dq, c, s).astype(BF16)
                dk_ref[rows, qk] = _unrot(dk, c, s).astype(BF16)
                dstate[h] = cd_ref[h] * dstate[h] + _dot(qb, doq, 0, 0)

    nsteps = nc // RCH
    rev = lambda n: nsteps - 1 - n
    ins, state_spec, out_spec = _ret_specs(rev)
    qk_out = pl.BlockSpec((RCH * CH, RH * RDK), lambda n: (rev(n), 0))
    return pl.pallas_call(
        body, name="ret_bwd", grid=(nsteps,), in_specs=ins + [state_spec, out_spec],
        out_specs=[qk_out, qk_out, out_spec],
        out_shape=[jax.ShapeDtypeStruct((S, RH * RDK), BF16)] * 2 + [jax.ShapeDtypeStruct((S, RH * RDV), BF16)],
        scratch_shapes=[pltpu.VMEM((RH, RDK, RDV), F32)],
        compiler_params=_cparams(("arbitrary",)),
    )(proj, proj, proj, cos, sin, *consts, states, d_ret)


CW = 256
HALO = 16


def _shift_down(v, halo, k):
    rolled = pltpu.roll(v, k, 0)
    hr = pltpu.roll(halo, k, 0)[0:8]
    row = lax.broadcasted_iota(jnp.int32, hr.shape, 0)
    return jnp.concatenate([jnp.where(row < k, hr, rolled[0:8]), rolled[8:]], axis=0)


def _shift_up(v, halo, k):
    T = v.shape[0]
    rolled = pltpu.roll(v, T - k, 0)
    hr = pltpu.roll(halo, 8 - k, 0)[0:8]
    row = lax.broadcasted_iota(jnp.int32, hr.shape, 0)
    return jnp.concatenate([rolled[:T - 8], jnp.where(row >= 8 - k, hr, rolled[T - 8:])], axis=0)


def _conv_taps(h_ref, hp_ref, first):
    h = h_ref[...].astype(F32)
    hp = hp_ref[...].astype(F32) * jnp.where(first, 0.0, 1.0)
    return _shift_down(h, hp, 2), _shift_down(h, hp, 1), h


def _conv_specs(S, T):
    nj = DFF // CW
    cur = pl.BlockSpec((T, CW), lambda j, i: (i, j))
    prev = pl.BlockSpec((HALO, CW), lambda j, i: (jnp.maximum(i * (T // HALO) - 1, 0), j))
    nxt = pl.BlockSpec((HALO, CW), lambda j, i: (jnp.minimum((i + 1) * (T // HALO), S // HALO - 1), j))
    w = pl.BlockSpec((3, CW), lambda j, i: (0, j))
    b = pl.BlockSpec((1, CW), lambda j, i: (0, j))
    return nj, cur, prev, nxt, w, b


def _conv_fwd(hg, hu, wg, wu, bg, bu):
    S = hg.shape[0]
    T = min(1024, S)
    nj, cur, prev, _, w, b = _conv_specs(S, T)

    def body(hg_ref, hu_ref, hgp_ref, hup_ref, wg_ref, wu_ref, bg_ref, bu_ref, o_ref):
        first = pl.program_id(1) == 0
        g2, g1, g0 = _conv_taps(hg_ref, hgp_ref, first)
        u2, u1, u0 = _conv_taps(hu_ref, hup_ref, first)
        cg = wg_ref[0:1, :] * g2 + wg_ref[1:2, :] * g1 + wg_ref[2:3, :] * g0 + bg_ref[...]
        cu = wu_ref[0:1, :] * u2 + wu_ref[1:2, :] * u1 + wu_ref[2:3, :] * u0 + bu_ref[...]
        o_ref[...] = (_gelu(cg)[0] * cu).astype(BF16)

    return pl.pallas_call(
        body, name="conv_fwd", grid=(nj, S // T), in_specs=[cur, cur, prev, prev, w, w, b, b], out_specs=cur,
        out_shape=jax.ShapeDtypeStruct((S, DFF), BF16), compiler_params=_cparams(("parallel", "parallel")),
    )(hg, hu, hg, hu, wg, wu, bg, bu)


def _conv_bwd_pre(d_act, hg, hu, wg, wu, bg, bu):
    S = hg.shape[0]
    T = min(1024, S)
    nj, cur, prev, _, w, b = _conv_specs(S, T)

    def body(da_ref, hg_ref, hu_ref, hgp_ref, hup_ref, wg_ref, wu_ref, bg_ref, bu_ref,
             dcg_ref, dcu_ref, gwg_ref, gwu_ref, gbg_ref, gbu_ref):
        first = pl.program_id(1) == 0
        g2, g1, g0 = _conv_taps(hg_ref, hgp_ref, first)
        u2, u1, u0 = _conv_taps(hu_ref, hup_ref, first)
        cg = wg_ref[0:1, :] * g2 + wg_ref[1:2, :] * g1 + wg_ref[2:3, :] * g0 + bg_ref[...]
        cu = wu_ref[0:1, :] * u2 + wu_ref[1:2, :] * u1 + wu_ref[2:3, :] * u0 + bu_ref[...]
        da = da_ref[...].astype(F32)
        gl, t = _gelu(cg)
        dcg = da * cu * _gelu_grad(cg, t)
        dcu = da * gl
        dcg_ref[...] = dcg.astype(BF16)
        dcu_ref[...] = dcu.astype(BF16)

        @pl.when(first)
        def _():
            for r in (gwg_ref, gwu_ref, gbg_ref, gbu_ref):
                r[...] = jnp.zeros_like(r)

        for r, d, taps in ((gwg_ref, dcg, (g2, g1, g0)), (gwu_ref, dcu, (u2, u1, u0))):
            for j in range(3):
                r[j:j + 1, :] += jnp.sum(d * taps[j], 0, keepdims=True)
        gbg_ref[...] += jnp.sum(dcg, 0, keepdims=True)
        gbu_ref[...] += jnp.sum(dcu, 0, keepdims=True)

    return pl.pallas_call(
        body, name="conv_bwd_pre", grid=(nj, S // T), in_specs=[cur, cur, cur, prev, prev, w, w, b, b],
        out_specs=[cur, cur, w, w, b, b],
        out_shape=[jax.ShapeDtypeStruct((S, DFF), BF16)] * 2 + [jax.ShapeDtypeStruct((3, DFF), F32)] * 2
        + [jax.ShapeDtypeStruct((1, DFF), F32)] * 2,
        compiler_params=_cparams(("parallel", "arbitrary")),
    )(d_act, hg, hu, hg, hu, wg, wu, bg, bu)


def _conv_bwd_in(dc, w, name):
    S = dc.shape[0]
    T = min(1024, S)
    nj, cur, _, nxt, wspec, _ = _conv_specs(S, T)
    nt = S // T

    def body(dc_ref, dn_ref, w_ref, o_ref):
        d = dc_ref[...].astype(F32)
        dn = dn_ref[...].astype(F32) * jnp.where(pl.program_id(1) == nt - 1, 0.0, 1.0)
        o_ref[...] = (w_ref[2:3, :] * d + w_ref[1:2, :] * _shift_up(d, dn, 1) + w_ref[0:1, :] * _shift_up(d, dn, 2)).astype(BF16)

    return pl.pallas_call(
        body, name=name, grid=(nj, nt), in_specs=[cur, nxt, wspec], out_specs=cur,
        out_shape=jax.ShapeDtypeStruct((S, DFF), BF16), compiler_params=_cparams(("parallel", "parallel")),
    )(dc, dc, w)


def _adam_math(g, w, m, v):
    m = B1 * m + (1.0 - B1) * g
    v = B2 * v + (1.0 - B2) * (g * g)
    m_hat = m / (1.0 - B1 ** STEP)
    v_hat = v / (1.0 - B2 ** STEP)
    return -LR * (m_hat / (jnp.sqrt(v_hat) + EPS) + WD * w), m, v


def _reduce_tail(chip32, far, chip, name, wmv=None):
    L = len(chip32)
    _, R, C = chip32[0].shape
    tr = _tile(R, 256, 16)
    nr = R // tr

    def body(chip_ref, *refs):
        own_refs, far_refs, rest = refs[:L], refs[L:2 * L], refs[2 * L:]
        outs = rest[3:] if wmv else rest
        for ll in range(L):
            @pl.when(pl.program_id(0) == ll)
            def _(ll=ll):
                g = own_refs[ll][...]
                for s in range(3):
                    g = g + far_refs[ll][s].astype(F32)
                outs[0][...] = g
                if wmv:
                    outs[1][...], outs[2][...], outs[3][...] = _adam_math(g, rest[0][...], rest[1][...], rest[2][...])

    def rows(ll):
        return lambda l, i: jnp.where(l == ll, i, jnp.where(l < ll, 0, nr - 1))

    blk = pl.BlockSpec((None, tr, C), lambda l, i, ch: (l, i, 0))
    in_specs = [pl.BlockSpec((None, tr, C), lambda l, i, ch, f=rows(ll): (ch[0], f(l, i), 0)) for ll in range(L)]
    in_specs += [pl.BlockSpec((3, tr, C), lambda l, i, ch, f=rows(ll): (0, f(l, i), 0)) for ll in range(L)]
    args = list(chip32) + list(far)
    n_out = 1
    if wmv:
        in_specs += [blk] * 3
        args += list(wmv)
        n_out = 4
    return pl.pallas_call(
        body, name=name,
        grid_spec=pltpu.PrefetchScalarGridSpec(num_scalar_prefetch=1, grid=(L, nr), in_specs=in_specs, out_specs=[blk] * n_out),
        out_shape=[jax.ShapeDtypeStruct((L, R, C), F32)] * n_out, compiler_params=_cparams(("arbitrary", "arbitrary")),
    )(chip, *args)


def _adamw(g, w, m, v, name):
    R, C = g.shape
    tr = _tile(R, 128, 8)

    def body(g_ref, w_ref, m_ref, v_ref, d_ref, nm_ref, nv_ref):
        d_ref[...], nm_ref[...], nv_ref[...] = _adam_math(g_ref[...], w_ref[...], m_ref[...], v_ref[...])

    blk = pl.BlockSpec((tr, C), lambda i: (i, 0))
    return pl.pallas_call(
        body, name=name, grid=(R // tr,), in_specs=[blk] * 4, out_specs=[blk] * 3,
        out_shape=[jax.ShapeDtypeStruct(g.shape, F32)] * 3, compiler_params=_cparams(("parallel",)),
    )(g, w, m, v)


def _pair_sum(x, recv, core, name):
    _, R, C = x.shape
    tr = _tile(R, 600, 16)

    def body(core_ref, x_ref, r_ref, o32_ref, o16_ref):
        s = x_ref[...] + r_ref[...]
        o32_ref[...] = s
        o16_ref[...] = s.astype(BF16)

    blk = pl.BlockSpec((None, tr, C), lambda q, i, c: (q, i, 0))
    mine = pl.BlockSpec((None, None, tr, C), lambda q, i, c: (q, c[0], i, 0))
    return pl.pallas_call(
        body, name=name,
        grid_spec=pltpu.PrefetchScalarGridSpec(num_scalar_prefetch=1, grid=(4, R // tr), in_specs=[mine, blk], out_specs=[blk, blk]),
        out_shape=[jax.ShapeDtypeStruct((4, R, C), F32), jax.ShapeDtypeStruct((4, R, C), BF16)],
        compiler_params=_cparams(("parallel", "parallel")),
    )(core, x.reshape(4, 2, R, C), recv)


def _sum_slots(x, name):
    def body(x_ref, o_ref):
        g = x_ref[0]
        for s in range(1, x.shape[0]):
            g = g + x_ref[s]
        o_ref[...] = g

    return pl.pallas_call(body, name=name, out_shape=jax.ShapeDtypeStruct(x.shape[1:], F32))(x)


MESH = pl.DeviceIdType.MESH
_HBM = pl.BlockSpec(memory_space=pltpu.HBM)


def _dma_sems(n):
    return pltpu.SemaphoreType.DMA((n,))


def _gather_many(xs, name):
    n = len(xs)

    def body(*refs):
        x_refs, out_refs = refs[:n], refs[n:2 * n]
        send_sems, recv_sems, local_sems = refs[2 * n:]
        ax, ay, ac = lax.axis_index("x"), lax.axis_index("y"), lax.axis_index("c")
        me, sibling = (ax, ay, ac), (ax, ay, 1 - ac)
        chips = [(1 - ax, ay), (ax, 1 - ay), (1 - ax, 1 - ay)]

        def copy(a, k, block, to, own=False):
            slot = out_refs[a].at[4 * block[0] + 2 * block[1] + block[2]]
            return pltpu.make_async_remote_copy(
                src_ref=x_refs[a] if own else slot, dst_ref=slot, send_sem=send_sems.at[7 * a + k],
                recv_sem=recv_sems.at[7 * a + k], device_id=to, device_id_type=MESH)

        mine = [pltpu.make_async_copy(x_refs[a], out_refs[a].at[4 * ax + 2 * ay + ac], local_sems.at[a]) for a in range(n)]
        first = [copy(a, 0, me, sibling, own=True) for a in range(n)]
        first += [copy(a, 1 + j, me, (*chip, ac), own=True) for j, chip in enumerate(chips) for a in range(n)]
        for cp in mine + first:
            cp.start()
        passed = []
        for j, chip in enumerate(chips):
            for a in range(n):
                copy(a, 1 + j, (*chip, ac), me).wait_recv()
                cp = copy(a, 4 + j, (*chip, ac), sibling)
                cp.start()
                passed.append(cp)
        for a in range(n):
            copy(a, 0, sibling, me).wait_recv()
            for j, chip in enumerate(chips):
                copy(a, 4 + j, (*chip, 1 - ac), me).wait_recv()
        for cp in first + passed:
            cp.wait_send()
        for cp in mine:
            cp.wait()

    return pl.pallas_call(
        body, name=name, out_shape=[jax.ShapeDtypeStruct((N_DEV,) + x.shape, x.dtype) for x in xs],
        in_specs=[_HBM] * n, out_specs=[_HBM] * n, scratch_shapes=[_dma_sems(7 * n), _dma_sems(7 * n), _dma_sems(n)],
    )(*xs)


def _exchange_cores(xs, name):
    n = len(xs)

    def body(*refs):
        x_refs, out_refs = refs[:n], refs[n:2 * n]
        send_sems, recv_sems = refs[2 * n:]
        ax, ay, ac = lax.axis_index("x"), lax.axis_index("y"), lax.axis_index("c")
        copies = []
        for a in range(n):
            for q in range(4):
                copies.append(pltpu.make_async_remote_copy(
                    src_ref=x_refs[a].at[2 * q + 1 - ac], dst_ref=out_refs[a].at[q], send_sem=send_sems.at[4 * a + q],
                    recv_sem=recv_sems.at[4 * a + q], device_id=(ax, ay, 1 - ac), device_id_type=MESH))
        for cp in copies:
            cp.start()
        for cp in copies:
            cp.wait_recv()
        for cp in copies:
            cp.wait_send()

    return pl.pallas_call(
        body, name=name, out_shape=[jax.ShapeDtypeStruct((4,) + x.shape[1:], x.dtype) for x in xs],
        in_specs=[_HBM] * n, out_specs=[_HBM] * n, scratch_shapes=[_dma_sems(4 * n), _dma_sems(4 * n)],
    )(*xs)


def _exchange_chips(ps, name):
    n = len(ps)

    def body(*refs):
        p_refs, out_refs = refs[:n], refs[n:2 * n]
        send_sems, recv_sems = refs[2 * n:]
        ax, ay, ac = lax.axis_index("x"), lax.axis_index("y"), lax.axis_index("c")
        copies = []
        for a in range(n):
            for k in range(1, 4):
                px = 1 - ax if k & 2 else ax
                py = 1 - ay if k & 1 else ay
                copies.append(pltpu.make_async_remote_copy(
                    src_ref=p_refs[a].at[2 * px + py], dst_ref=out_refs[a].at[k - 1], send_sem=send_sems.at[3 * a + k - 1],
                    recv_sem=recv_sems.at[3 * a + k - 1], device_id=(px, py, ac), device_id_type=MESH))
        for cp in copies:
            cp.start()
        for cp in copies:
            cp.wait_recv()
        for cp in copies:
            cp.wait_send()

    return pl.pallas_call(
        body, name=name, out_shape=[jax.ShapeDtypeStruct((3,) + p.shape[1:], p.dtype) for p in ps],
        in_specs=[_HBM] * n, out_specs=[_HBM] * n, scratch_shapes=[_dma_sems(3 * n), _dma_sems(3 * n)],
    )(*ps)


def _x_view(xb, d):
    return xb if d == 1 else xb.reshape(xb.shape[0] // d, d * xb.shape[1])


def _layer_fwd(x, xb, p, w, cos, sin, rconsts):
    S = x.shape[0]
    proj = _mm(xb, w["win_rest"], tb=True, name="mm_proj", out_dtype=BF16)
    qkvs, ogs, lgs = [], [], []
    for g, dil in enumerate(DILATIONS):
        qkv = _qkv_fwd(_x_view(xb, dil), w["win_att"][g], dil, f"mm_qkv{g}")
        o, l = _attn_fwd(qkv, dil, f"attn_fwd_g{g}")
        qkvs.append(qkv)
        ogs.append(_to_tokens(o, dil))
        lgs.append(_to_tokens(l, dil))
    attn, lse = _rowwise(_f_combine, ogs + lgs, [], [(D, BF16), (HD, F32)], [], name="attn_combine")
    ret_raw, states = _ret_fwd(proj, cos, sin, rconsts)
    rg_win = (proj, RH * RDV, OFF_RG // (RH * RDV))
    ga_win, gr_win = (proj, D, OFF_GA // D), (proj, D, OFF_GR // D)
    (r,) = _rowwise(_f_gn, [ret_raw, rg_win], [w["ret_gn_g"], w["ret_gn_b"]], [(RH * RDV, BF16)], [], name="gn_fwd", tm=256)
    ap = _mm(attn, w["w_attn_proj"], name="mm_attn_proj", out_dtype=BF16)
    rp = _mm(r, w["w_ret_proj"], name="mm_ret_proj", out_dtype=BF16, tk=2048)
    (merged,) = _rowwise(_f_gate, [ap, rp, ga_win, gr_win], [], [(D, BF16)], [], name="gate_fwd")
    mix = _mm(merged, w["w_out"], name="mm_out")
    h1, x1, x1b = _rowwise(_f_ln1, [x, mix], [w["ln1_g"], w["ln1_b"]], [(D, F32), (D, F32), (D, BF16)], [], name="ln1_fwd")
    z = _mm(x1b, w["w_ple_gate"], name="mm_ple_gate")
    pp = _mm(p, w["w_ple_proj"], tb=True, name="mm_ple_proj")
    hg = _mm(x1b, w["w_upg"], tb=True, name="mm_up_g", out_dtype=BF16, tm=512, tn=DFF)
    hu = _mm(x1b, w["w_upu"], tb=True, name="mm_up_u", out_dtype=BF16, tm=512, tn=DFF)
    act = _conv_fwd(hg, hu, w["conv_wg"], w["conv_wu"], w["conv_bg"], w["conv_bu"])
    ffn = _mm(act, w["w_down"], name="mm_down", tm=512, tk=DFF)
    h2, x2, x2b = _rowwise(_f_ln2, [x1, ffn, z, pp], [w["ln2_g"], w["ln2_b"]], [(D, F32), (D, F32), (D, BF16)], [], name="ln2_fwd")
    saved = dict(xb=xb, proj=proj, qkvs=qkvs, attn=attn, lse=lse, ret_raw=ret_raw, states=states, r=r, ap=ap, rp=rp,
                 merged=merged, h1=h1, x1b=x1b, z=z, pp=pp, hg=hg, hu=hu, act=act, h2=h2, p=p)
    return x2, x2b, saved


def _layer_bwd(dys, w, sv, cos, sin, rconsts):
    gr = {}
    proj = sv["proj"]
    dh2, dh2b, gr["ln2_g"], gr["ln2_b"] = _rowwise(_f_ln_bwd, list(dys) + [sv["h2"]], [w["ln2_g"]], [(D, F32), (D, BF16)],
                                                   [(1, D), (1, D)], name="ln2_bwd")
    d_act = _mm(dh2b, w["w_down"], tb=True, name="mm_down_dx", out_dtype=BF16, tm=512, tn=DFF)
    gr["w_down"] = _mm(sv["act"], dh2b, ta=True, name="mm_down_dw", tm=DFF // 2)
    dcg, dcu, gwg, gwu, gbg, gbu = _conv_bwd_pre(d_act, sv["hg"], sv["hu"], w["conv_wg"], w["conv_wu"], w["conv_bg"], w["conv_bu"])
    gr["conv_w"] = jnp.concatenate([gwg, gwu], axis=1)
    gr["conv_b"] = jnp.concatenate([gbg, gbu], axis=1)
    dhg = _conv_bwd_in(dcg, w["conv_wg"], "conv_bwd_in_g")
    dhu = _conv_bwd_in(dcu, w["conv_wu"], "conv_bwd_in_u")
    gr["w_up"] = jnp.concatenate([_mm(dhg, sv["x1b"], ta=True, name="mm_up_g_dw", tm=DFF // 2),
                                _mm(dhu, sv["x1b"], ta=True, name="mm_up_u_dw", tm=DFF // 2)], axis=0)
    dx1 = _mm(dhg, w["w_upg"], name="mm_up_g_dx", add=dh2, add_scale=ALPHA, tm=512, tk=DFF)
    dx1 = _mm(dhu, w["w_upu"], name="mm_up_u_dx", add=dx1, tm=512, tk=DFF)
    dpp, dz = _rowwise(_f_ple_bwd, [dh2, sv["z"], sv["pp"]], [], [(D, BF16), (D, BF16)], [], name="ple_bwd")
    gr["w_ple_proj"] = _mm(dpp, sv["p"], ta=True, name="mm_ple_proj_dw")
    gr["w_ple_gate"] = _mm(sv["x1b"], dz, ta=True, name="mm_ple_gate_dw")
    dx1 = _mm(dz, w["w_ple_gate"], tb=True, name="mm_ple_gate_dx", add=dx1)
    dh1, dh1b, gr["ln1_g"], gr["ln1_b"] = _rowwise(_f_ln_bwd, [dx1, sv["h1"]], [w["ln1_g"]], [(D, F32), (D, BF16)],
                                                   [(1, D), (1, D)], name="ln1_bwd")
    d_merged = _mm(dh1b, w["w_out"], tb=True, name="mm_out_dx", out_dtype=BF16)
    gr["w_out"] = _mm(sv["merged"], dh1b, ta=True, name="mm_out_dw")
    rg_win = (proj, RH * RDV, OFF_RG // (RH * RDV))
    ga_win, gr_win = (proj, D, OFF_GA // D), (proj, D, OFF_GR // D)
    dap, drp, dga, dgr = _rowwise(_f_gate_bwd, [d_merged, sv["ap"], sv["rp"], ga_win, gr_win], [], [(D, BF16)] * 4, [], name="gate_bwd")
    d_attn = _mm(dap, w["w_attn_proj"], tb=True, name="mm_attn_proj_dx", out_dtype=BF16)
    gr["w_attn_proj"] = _mm(sv["attn"], dap, ta=True, name="mm_attn_proj_dw")
    d_r = _mm(drp, w["w_ret_proj"], tb=True, name="mm_ret_proj_dx", out_dtype=BF16, tn=2048)
    gr["w_ret_proj"] = _mm(sv["r"], drp, ta=True, name="mm_ret_proj_dw", tm=2048)
    d_ret, d_rg, gr["ret_gn_g"], gr["ret_gn_b"] = _rowwise(
        _f_gn_bwd, [d_r, sv["ret_raw"], rg_win], [w["ret_gn_g"], w["ret_gn_b"]], [(RH * RDV, BF16)] * 2,
        [(1, RH * RDV), (1, RH * RDV)], name="gn_bwd", tm=256)
    d_rq, d_rk, d_rv = _ret_bwd(proj, cos, sin, rconsts, sv["states"], d_ret)
    d_rest = jnp.concatenate([d_rq, d_rk, d_rv, d_rg, dga, dgr], axis=1)
    gw_rest = _mm(d_rest, sv["xb"], ta=True, name="mm_proj_dw", tm=2048)
    dx0 = _mm(d_rest, w["win_rest"], name="mm_proj_dx", add=dh1, add_scale=ALPHA, tk=2048)
    (delta,) = _rowwise(_f_delta, [d_attn, sv["attn"]], [], [(HD, F32)], [], name="attn_delta")
    gw_att, dx_parts = [], []
    for g, dil in enumerate(DILATIONS):
        dqkv = _attn_bwd(sv["qkvs"][g], _to_head_residues(d_attn, dil), _to_residues(sv["lse"], dil), _to_residues(delta, dil),
                         dil, f"attn_bwd_g{g}")
        gw_att.append(_qkv_dw(dqkv, _x_view(sv["xb"], dil), dil, f"mm_qkv{g}_dw"))
        if dil == 1:
            dx0 = _qkv_dx(dqkv, w["win_att"][g], dil, f"mm_qkv{g}_dx", add=dx0)
        else:
            dx_parts.append(_qkv_dx(dqkv, w["win_att"][g], dil, f"mm_qkv{g}_dx").reshape(dx0.shape))
    gw_att = jnp.stack(gw_att).reshape(3, NH, 3, HD, D).transpose(0, 2, 1, 3, 4).reshape(N_ATT, D)
    gr["w_in"] = jnp.concatenate([gw_att, gw_rest], axis=0)
    return [dx0] + dx_parts, gr


def _local_step(x, p, positions, target, ws):
    half = RDK // 2
    freq = jnp.power(ROPE_BASE, -jnp.arange(half, dtype=F32) / half)
    ang = positions.astype(F32)[:, None] * freq[None, :]
    cos, sin = jnp.cos(ang), jnp.sin(ang)
    rconsts = _ret_consts()
    xb = x.astype(BF16)
    saved = []
    for l in range(DEPTH):
        x, xb, sv = _layer_fwd(x, xb, p[l], ws[l], cos, sin, rconsts)
        saved.append(sv)
    dy, loss_vec = _rowwise(_f_loss, [x, target], [], [(D, F32)], [(1, D)], name="loss")
    dys, grads = [dy], [None] * DEPTH
    for l in reversed(range(DEPTH)):
        dys, grads[l] = _layer_bwd(dys, ws[l], saved[l], cos, sin, rconsts)
    (grad_x,) = _rowwise(_f_sum, dys, [], [(D, F32)], [], name="grad_x_sum")
    return loss_vec, grad_x, grads


def _pack_rows(arrs):
    parts, where, off = [], [], 0
    for t in arrs:
        t = t.reshape(-1, D)
        rows = t.shape[0]
        padded = -(-rows // 8) * 8
        parts.append(jnp.pad(t, ((0, padded - rows), (0, 0))))
        where.append((off, rows))
        off += padded
    return jnp.concatenate(parts, axis=0), where


def _layer_weights(g, l, conv_w_all, conv_b, W):
    win = g["w_in"].reshape(N_IN, D)
    att = win[:N_ATT].reshape(3, 3, NH, HD, D).transpose(0, 2, 1, 3, 4).reshape(3, 3 * D, D)
    up = g["w_up"].reshape(2 * DFF, D)
    w = dict(win_att=[att[0], att[1], att[2]], win_rest=win[N_ATT:], w_upg=up[:DFF], w_upu=up[DFF:],
             w_ple_proj=g["w_ple_proj"].reshape(D, PLE), w_attn_proj=g["w_attn_proj"].reshape(D, D),
             w_ret_proj=g["w_ret_proj"].reshape(RH * RDV, D), w_out=g["w_out"].reshape(D, D),
             w_down=g["w_down"].reshape(DFF, D), w_ple_gate=g["w_ple_gate"].reshape(D, D))
    w["conv_wg"], w["conv_wu"] = conv_w_all[l][:, :DFF], conv_w_all[l][:, DFF:]
    w["conv_bg"], w["conv_bu"] = conv_b[l][None, :DFF], conv_b[l][None, DFF:]
    for n in ("ret_gn_g", "ret_gn_b", "ln1_g", "ln1_b", "ln2_g", "ln2_b"):
        w[n] = W[n][l][None, :]
    return w


def kernel(x, p, positions, w_in, w_attn_proj, w_ret_proj, ret_gn_g, ret_gn_b, w_out, ln1_g, ln1_b, w_up, conv_w, conv_b, w_down, w_ple_gate, w_ple_proj, ln2_g, ln2_b, loss_target, m_w_in, m_w_attn_proj, m_w_ret_proj, m_ret_gn_g, m_ret_gn_b, m_w_out, m_ln1_g, m_ln1_b, m_w_up, m_conv_w, m_conv_b, m_w_down, m_w_ple_gate, m_w_ple_proj, m_ln2_g, m_ln2_b, v_w_in, v_w_attn_proj, v_w_ret_proj, v_ret_gn_g, v_ret_gn_b, v_w_out, v_ln1_g, v_ln1_b, v_w_up, v_conv_w, v_conv_b, v_w_down, v_w_ple_gate, v_w_ple_proj, v_ln2_g, v_ln2_b):
    W = dict(w_in=w_in, w_attn_proj=w_attn_proj, w_ret_proj=w_ret_proj, ret_gn_g=ret_gn_g, ret_gn_b=ret_gn_b, w_out=w_out,
             ln1_g=ln1_g, ln1_b=ln1_b, w_up=w_up, conv_w=conv_w, conv_b=conv_b, w_down=w_down, w_ple_gate=w_ple_gate,
             w_ple_proj=w_ple_proj, ln2_g=ln2_g, ln2_b=ln2_b)
    M = dict(w_in=m_w_in, w_attn_proj=m_w_attn_proj, w_ret_proj=m_w_ret_proj, ret_gn_g=m_ret_gn_g, ret_gn_b=m_ret_gn_b,
             w_out=m_w_out, ln1_g=m_ln1_g, ln1_b=m_ln1_b, w_up=m_w_up, conv_w=m_conv_w, conv_b=m_conv_b, w_down=m_w_down,
             w_ple_gate=m_w_ple_gate, w_ple_proj=m_w_ple_proj, ln2_g=m_ln2_g, ln2_b=m_ln2_b)
    V = dict(w_in=v_w_in, w_attn_proj=v_w_attn_proj, w_ret_proj=v_w_ret_proj, ret_gn_g=v_ret_gn_g, ret_gn_b=v_ret_gn_b,
             w_out=v_w_out, ln1_g=v_ln1_g, ln1_b=v_ln1_b, w_up=v_w_up, conv_w=v_conv_w, conv_b=v_conv_b, w_down=v_w_down,
             w_ple_gate=v_w_ple_gate, w_ple_proj=v_w_ple_proj, ln2_g=v_ln2_g, ln2_b=v_ln2_b)

    ws, conv_w_all = [], None
    for l in range(DEPTH):
        shards = [(W[n][l].T if n in COL_SHARDED else W[n][l]).astype(BF16) for n in BIG]
        outs = _gather_many(shards + ([conv_w] if l == 0 else []), f"gather_weights_l{l}")
        if l == 0:
            conv_w_all = outs[-1].transpose(1, 2, 0, 3).reshape(DEPTH, 3, 2 * DFF)
        ws.append(_layer_weights(dict(zip(BIG, outs)), l, conv_w_all, conv_b, W))

    loss_vec, grad_x, grads = _local_step(x[0], p[:, 0], positions[0], loss_target[0], ws)
    loss = lax.psum(jnp.sum(loss_vec), ("x", "y", "c"))

    core = lax.axis_index("c").astype(jnp.int32).reshape(1)
    chip = (2 * lax.axis_index("x") + lax.axis_index("y")).astype(jnp.int32).reshape(1)
    mine = [grads[l][n].reshape((N_DEV, -1) + grads[l][n].shape[1:]) for n in BIG for l in range(DEPTH)]
    theirs = _exchange_cores(mine, "exchange_grads_cores")
    sums = [_pair_sum(a, b, core, f"pair_sum_{i}") for i, (a, b) in enumerate(zip(mine, theirs))]
    fars = _exchange_chips([s16 for _, s16 in sums], "exchange_grads_chips")
    G, DW, NM, NV = ({} for _ in range(4))
    for a, n in enumerate(BIG):
        chip32 = [sums[DEPTH * a + l][0] for l in range(DEPTH)]
        far = [fars[DEPTH * a + l] for l in range(DEPTH)]
        if n in COL_SHARDED:
            G[n] = _reduce_tail(chip32, far, chip, f"reduced_{n}")[0].transpose(0, 2, 1)
            R2, C2 = DEPTH * W[n].shape[1], W[n].shape[2]
            res = _adamw(*(t.reshape(R2, C2) for t in (G[n], W[n], M[n], V[n])), f"adamw_{n}")
            DW[n], NM[n], NV[n] = (t.reshape(W[n].shape) for t in res)
        else:
            G[n], DW[n], NM[n], NV[n] = _reduce_tail(chip32, far, chip, f"adamw_{n}", wmv=(W[n], M[n], V[n]))

    small_names = SMALL + ("conv_w",)
    g_small, where = _pack_rows([jnp.stack([grads[l][n] for l in range(DEPTH)]) for n in small_names])
    (g_all,) = _gather_many([g_small], "gather_small_grads")
    g_small = _sum_slots(g_all, "sum_small_grads")
    for n, (off, rows) in zip(SMALL, where):
        G[n] = g_small[off:off + rows].reshape(W[n].shape)
    off, rows = where[-1]
    g_cw = g_small[off:off + rows].reshape(DEPTH, 3, N_DEV, conv_w.shape[2])
    me = 4 * lax.axis_index("x") + 2 * lax.axis_index("y") + lax.axis_index("c")
    G["conv_w"] = lax.dynamic_index_in_dim(g_cw, me, axis=2, keepdims=False)
    packed = [_pack_rows([d[n] for n in SMALL]) for d in (G, W, M, V)]
    small_out = _adamw(*(t for t, _ in packed), "adamw_small")
    for res, dst in zip(small_out, (DW, NM, NV)):
        for n, (off, rows) in zip(SMALL, packed[0][1]):
            dst[n] = res[off:off + rows].reshape(W[n].shape)
    two_d = lambda t: t.reshape(DEPTH * 3, conv_w.shape[2])
    cw_out = _adamw(two_d(G["conv_w"]), two_d(conv_w), two_d(m_conv_w), two_d(v_conv_w), "adamw_conv_w")
    for res, dst in zip(cw_out, (DW, NM, NV)):
        dst["conv_w"] = res.reshape(conv_w.shape)

    return (loss, grad_x[None], *[G[n] for n in WEIGHTS], *[DW[n] for n in WEIGHTS], *[NM[n] for n in WEIGHTS],
            *[NV[n] for n in WEIGHTS])
```

```python
import math

import numpy as np
import jax
import jax.numpy as jnp
from jax import lax
from jax.experimental import pallas as pl
from jax.experimental.pallas import tpu as pltpu

F32, BF16 = jnp.float32, jnp.bfloat16

D = 1024
DEPTH = 2
N_DEV = 8
HD = 128
NH = 8
DILATIONS = (1, 4, 16)
SPAN = 128
N_ATT = 3 * 3 * NH * HD
RH, RDK, RDV = 4, 256, 512
CH = 128
DFF = 2816
PLE = 256
N_IN = 17408
N_REST = N_IN - N_ATT
OFF_RQ, OFF_RK, OFF_RV, OFF_RG, OFF_GA, OFF_GR = 0, 1024, 2048, 4096, 6144, 7168
ALPHA = (2 * DEPTH) ** 0.25
LN_EPS, GN_EPS = 1e-5, 1e-6
ROPE_BASE = 10000.0
LR, B1, B2, EPS, WD, STEP = 0.001, 0.9, 0.999, 1e-8, 0.01, 10
VMEM_LIMIT = 48 * 1024 * 1024
NEG = -1e30

BIG = ("w_in", "w_attn_proj", "w_ret_proj", "w_out", "w_up", "w_down", "w_ple_gate", "w_ple_proj")
COL_SHARDED = ("w_in", "w_up", "w_ple_proj")
SMALL = ("ret_gn_g", "ret_gn_b", "ln1_g", "ln1_b", "conv_b", "ln2_g", "ln2_b")
WEIGHTS = ("w_in", "w_attn_proj", "w_ret_proj", "ret_gn_g", "ret_gn_b", "w_out", "ln1_g", "ln1_b", "w_up",
           "conv_w", "conv_b", "w_down", "w_ple_gate", "w_ple_proj", "ln2_g", "ln2_b")


def _tile(n, cap, mult=128):
    if n <= cap:
        return n
    t = (cap // mult) * mult
    while n % t:
        t -= mult
    return t


def _cparams(sem):
    return pltpu.CompilerParams(dimension_semantics=sem, vmem_limit_bytes=VMEM_LIMIT)


def _dot(a, b, ca, cb):
    return lax.dot_general(a, b, (((ca,), (cb,)), ((), ())), preferred_element_type=F32)


def _bdot(a, b, ca, cb):
    return lax.dot_general(a, b, (((ca,), (cb,)), ((0,), (0,))), preferred_element_type=F32)


def _mm(a, b, *, name, ta=False, tb=False, out_dtype=F32, add=None, add_scale=1.0, tm=1024, tn=1024, tk=1024,
        b_rows=None, out_rows=None, into=None, blocks8=False):
    M, K = (a.shape[1], a.shape[0]) if ta else a.shape
    b_first, b_count = b_rows if b_rows else (0, b.shape[0])
    N = b_count if tb else b.shape[1]
    assert K == (b.shape[1] if tb else b_count)
    tm, tn, tk = _tile(M, tm), _tile(N, tn), _tile(K, tk)
    nk = K // tk
    o_first, o_total = out_rows if out_rows else (0, M)
    jb, kb, io = (b_first // tn, 0, o_first // tm) if tb else (0, b_first // tk, o_first // tm)
    assert b_first % (tn if tb else tk) == 0 and o_first % tm == 0 and (add is None or out_rows is None)

    def body(*refs):
        if add is None:
            a_ref, b_ref = refs[:2]
        else:
            a_ref, b_ref, add_ref = refs[:3]
        o_ref, acc_ref = refs[-2:]
        k = pl.program_id(2)

        @pl.when(k == 0)
        def _():
            acc_ref[...] = jnp.zeros_like(acc_ref)

        acc_ref[...] += _dot(a_ref[...].astype(BF16), b_ref[...].astype(BF16), 0 if ta else 1, 1 if tb else 0)

        @pl.when(k == nk - 1)
        def _():
            r = acc_ref[...]
            if add is not None:
                r = r + add_scale * add_ref[...].astype(F32)
            o_ref[...] = r.astype(out_dtype).reshape(o_ref.shape)

    a_spec = pl.BlockSpec((tk, tm), lambda i, j, k: (k, i)) if ta else pl.BlockSpec((tm, tk), lambda i, j, k: (i, k))
    if tb:
        b_spec = pl.BlockSpec((tn, tk), lambda i, j, k: (j + jb, k))
    else:
        b_spec = pl.BlockSpec((tk, tn), lambda i, j, k: (k + kb, j))
    if blocks8:
        assert tm == 1024
        o_spec = pl.BlockSpec((1, 8, 128, tn), lambda i, j, k: (i + io, 0, 0, j))
        o_shape = (o_total // tm, 8, 128, N)
    else:
        o_spec = pl.BlockSpec((tm, tn), lambda i, j, k: (i + io, j))
        o_shape = (o_total, N)
    in_specs, args, aliases = [a_spec, b_spec], [a, b], {}
    if add is not None:
        in_specs.append(o_spec)
        args.append(add)
    if into is not None:
        aliases = {len(args): 0}
        in_specs.append(pl.BlockSpec(memory_space=pl.ANY))
        args.append(into)
    return pl.pallas_call(
        body, name=name, grid=(M // tm, N // tn, nk), in_specs=in_specs, out_specs=o_spec,
        out_shape=jax.ShapeDtypeStruct(o_shape, out_dtype), scratch_shapes=[pltpu.VMEM((tm, tn), F32)],
        input_output_aliases=aliases, compiler_params=_cparams(("parallel", "parallel", "arbitrary")),
    )(*args)


def _rowwise(fn, rows, pars, outs, accs, *, name, tm=512):
    first = rows[0][0] if isinstance(rows[0], tuple) else rows[0]
    S = first.shape[-2]
    tm = _tile(S, tm, 16)
    n_r, n_p, n_o = len(rows), len(pars), len(outs)
    outs = [o if len(o) == 5 else (o[0], o[1], o[0], 0, None) for o in outs]
    intos = [(k, o[4]) for k, o in enumerate(outs) if o[4] is not None]
    n_i = len(intos)

    def body(*refs):
        i = pl.program_id(0)
        vals = [r[...] for r in refs[:n_r + n_p]]
        res = fn(*vals)
        if not isinstance(res, (tuple, list)):
            res = (res,)
        o_refs = refs[n_r + n_p + n_i:n_r + n_p + n_i + n_o]
        a_refs = refs[n_r + n_p + n_i + n_o:]
        for r, v in zip(o_refs, res[:n_o]):
            r[...] = v.astype(r.dtype)
        if a_refs:
            @pl.when(i == 0)
            def _():
                for r in a_refs:
                    r[...] = jnp.zeros_like(r)

            for r, v in zip(a_refs, res[n_o:]):
                r[...] += v

    in_specs, args = [], []
    for r in rows:
        if isinstance(r, tuple):
            arr, w, cb = r
            in_specs.append(pl.BlockSpec((tm, w), lambda i, cb=cb: (i, cb)))
        elif r.ndim == 3:
            arr = r
            in_specs.append(pl.BlockSpec((arr.shape[0], tm, arr.shape[2]), lambda i: (0, i, 0)))
        else:
            arr = r
            in_specs.append(pl.BlockSpec((tm, arr.shape[1]), lambda i: (i, 0)))
        args.append(arr)
    for p_ in pars:
        in_specs.append(pl.BlockSpec(p_.shape, lambda i: (0, 0)))
        args.append(p_)
    aliases = {}
    for k, arr in intos:
        aliases[len(args)] = k
        in_specs.append(pl.BlockSpec(memory_space=pl.ANY))
        args.append(arr)
    out_shape = [jax.ShapeDtypeStruct((S, o[2]), o[1]) for o in outs] + [jax.ShapeDtypeStruct(a, F32) for a in accs]
    out_specs = [pl.BlockSpec((tm, o[0]), lambda i, cb=o[3]: (i, cb)) for o in outs] + [pl.BlockSpec(a, lambda i: (0, 0)) for a in accs]
    return pl.pallas_call(
        body, name=name, grid=(S // tm,), in_specs=in_specs, out_specs=out_specs, out_shape=out_shape,
        input_output_aliases=aliases, compiler_params=_cparams(("arbitrary",) if accs else ("parallel",)),
    )(*args)


def _norm(h, eps):
    mu = jnp.mean(h, -1, keepdims=True)
    d = h - mu
    rstd = lax.rsqrt(jnp.mean(d * d, -1, keepdims=True) + eps)
    return d * rstd, rstd


def _norm_bwd(dxh, xh, rstd):
    return rstd * (dxh - jnp.mean(dxh, -1, keepdims=True) - xh * jnp.mean(dxh * xh, -1, keepdims=True))


def _sig(x):
    return 1.0 / (1.0 + jnp.exp(-x))


_GELU_C = math.sqrt(2.0 / math.pi)


def _gelu(x):
    t = jnp.tanh(_GELU_C * (x + 0.044715 * x * x * x))
    return 0.5 * x * (1.0 + t), t


def _gelu_grad(x, t):
    return 0.5 * (1.0 + t) + 0.5 * x * (1.0 - t * t) * _GELU_C * (1.0 + 3 * 0.044715 * x * x)


def _f_ln1(x, mix, g, b):
    h = ALPHA * x + mix
    xh, _ = _norm(h, LN_EPS)
    y = xh * g + b
    return h, y, y


def _f_ln2(x, ffn, z, pp, g, b):
    h = ALPHA * x + ffn + _sig(z) * pp
    xh, _ = _norm(h, LN_EPS)
    y = xh * g + b
    return h, y, y


def _f_ln_bwd(*args):
    *dys, h, g = args
    dy = dys[0]
    for t in dys[1:]:
        dy = dy + t
    xh, rstd = _norm(h, LN_EPS)
    dh = _norm_bwd(dy * g, xh, rstd)
    return dh, dh, jnp.sum(dy * xh, 0, keepdims=True), jnp.sum(dy, 0, keepdims=True)


def _f_sum(*ts):
    r = ts[0]
    for t in ts[1:]:
        r = r + t
    return r


def _f_loss(y, t):
    e = y - t
    return e * (1.0 / D), jnp.sum(e * e, 0, keepdims=True) * (0.5 / D)


def _head_col(c, h):
    lane = lax.broadcasted_iota(jnp.int32, c.shape, 1)
    return jnp.sum(jnp.where(lane == h, c, 0.0), -1, keepdims=True)


def _f_combine(o0, o1, o2, l0, l1, l2):
    lane = lax.broadcasted_iota(jnp.int32, l0.shape, 1)
    parts, lse = [], jnp.zeros(l0.shape, F32)
    for h in range(NH):
        a0, a1, a2 = _head_col(l0, h), _head_col(l1, h), _head_col(l2, h)
        m = jnp.maximum(jnp.maximum(a0, a1), a2)
        e0, e1, e2 = jnp.exp(a0 - m), jnp.exp(a1 - m), jnp.exp(a2 - m)
        den = e0 + e1 + e2
        parts.append((e0 * o0[h].astype(F32) + e1 * o1[h].astype(F32) + e2 * o2[h].astype(F32)) / den)
        lse = jnp.where(lane == h, m + jnp.log(den), lse)
    return jnp.concatenate(parts, axis=1), lse


def _f_delta(da, a):
    lane = lax.broadcasted_iota(jnp.int32, (da.shape[0], HD), 1)
    out = jnp.zeros((da.shape[0], HD), F32)
    for h in range(NH):
        sl = slice(h * HD, (h + 1) * HD)
        s = jnp.sum(da[:, sl].astype(F32) * a[:, sl].astype(F32), -1, keepdims=True)
        out = jnp.where(lane == h, s, out)
    return out


def _f_gate(ap, rp, ga, gr):
    return _sig(ga.astype(F32)) * ap.astype(F32) + _sig(gr.astype(F32)) * rp.astype(F32)


def _f_gate_bwd(dm, ap, rp, ga, gr):
    dm = dm.astype(F32)
    sa, sr = _sig(ga.astype(F32)), _sig(gr.astype(F32))
    dga, dgr = dm * ap.astype(F32) * sa * (1.0 - sa), dm * rp.astype(F32) * sr * (1.0 - sr)
    return dm * sa, dm * sr, jnp.concatenate([dga, dgr], axis=1)


def _f_gn(y, rg, g, b):
    y, rg = y.astype(F32), rg.astype(F32)
    parts = []
    for h in range(RH):
        sl = slice(h * RDV, (h + 1) * RDV)
        xh, _ = _norm(y[:, sl], GN_EPS)
        parts.append(xh * g[:, sl] + b[:, sl])
    return rg * _sig(rg) * jnp.concatenate(parts, axis=1)


def _f_gn_bwd(dr, y, rg, g, b):
    dr, y, rg = dr.astype(F32), y.astype(F32), rg.astype(F32)
    s = _sig(rg)
    d_out = dr * rg * s
    dys, outs, xhs = [], [], []
    for h in range(RH):
        sl = slice(h * RDV, (h + 1) * RDV)
        xh, rstd = _norm(y[:, sl], GN_EPS)
        xhs.append(xh)
        outs.append(xh * g[:, sl] + b[:, sl])
        dys.append(_norm_bwd(d_out[:, sl] * g[:, sl], xh, rstd))
    xh, out = jnp.concatenate(xhs, axis=1), jnp.concatenate(outs, axis=1)
    d_rg = dr * out * s * (1.0 + rg * (1.0 - s))
    return jnp.concatenate(dys, axis=1), d_rg, jnp.sum(d_out * xh, 0, keepdims=True), jnp.sum(d_out, 0, keepdims=True)


def _f_ple_bwd(dh, z, pp):
    s = _sig(z)
    return dh * s, dh * pp * s * (1.0 - s)


QKV = 3 * HD


def _to_tokens(t, d):
    if d == 1:
        return t
    *lead, S, C = t.shape
    n = len(lead)
    perm = tuple(range(n)) + (n + 1, n, n + 2)
    return t.reshape(*lead, d, S // d, C).transpose(perm).reshape(*lead, S, C)


def _to_residues(t, d):
    if d == 1:
        return t
    S, C = t.shape
    return t.reshape(S // d, d, C).transpose(1, 0, 2).reshape(S, C)


def _to_head_residues(t, d):
    S = t.shape[0]
    return t.reshape(S // d, d, NH, HD).transpose(2, 1, 0, 3).reshape(NH, S, HD)


def _w_qkv_specs(g):
    return [pl.BlockSpec((D, D), lambda *i, t=t: (3 * g + t, 0)) for t in range(3)]


def _qkv_fwd(xv, win, g, dil, name):
    Sd = xv.shape[0]
    S = Sd * dil
    tm = min(512, Sd)
    nma = Sd // tm

    def body(a_ref, wq_ref, wk_ref, wv_ref, o_ref):
        a = a_ref[...]
        q, k, v = (_dot(a, w_ref[...], 1, 1).astype(BF16) for w_ref in (wq_ref, wk_ref, wv_ref))
        for h in range(NH):
            sl = slice(h * HD, (h + 1) * HD)
            o_ref[h] = jnp.concatenate([q[:, sl], k[:, sl], v[:, sl]], axis=1)

    return pl.pallas_call(
        body, name=name, grid=(S // tm,),
        in_specs=[pl.BlockSpec((tm, D), lambda i: (i % nma, i // nma))] + _w_qkv_specs(g),
        out_specs=pl.BlockSpec((NH, tm, QKV), lambda i: (0, i, 0)), out_shape=jax.ShapeDtypeStruct((NH, S, QKV), BF16),
        compiler_params=_cparams(("parallel",)),
    )(xv, win, win, win)


def _qkv_dx(dqkv, win, g, dil, name, out_dtype, add=None):
    S = dqkv.shape[1]
    Sd = S // dil
    tm = min(512, Sd)
    nmo = Sd // tm

    def body(*refs):
        a_ref, wq_ref, wk_ref, wv_ref = refs[:4]
        o_ref = refs[-1]
        acc = None
        for h in range(NH):
            sl = slice(h * HD, (h + 1) * HD)
            w = jnp.concatenate([wq_ref[sl, :], wk_ref[sl, :], wv_ref[sl, :]], axis=0)
            part = _dot(a_ref[h], w, 1, 0)
            acc = part if acc is None else acc + part
        if add is not None:
            acc = acc + refs[4][...]
        o_ref[...] = acc.astype(out_dtype)

    o_spec = pl.BlockSpec((tm, D), lambda i: (i % nmo, i // nmo))
    in_specs = [pl.BlockSpec((NH, tm, QKV), lambda i: (0, i, 0))] + _w_qkv_specs(g)
    args = [dqkv, win, win, win]
    if add is not None:
        assert dil == 1
        in_specs.append(o_spec)
        args.append(add)
    return pl.pallas_call(
        body, name=name, grid=(S // tm,), in_specs=in_specs, out_specs=o_spec,
        out_shape=jax.ShapeDtypeStruct((Sd, dil * D), out_dtype), compiler_params=_cparams(("parallel",)),
    )(*args)


GW_IN_BLOCKS = (N_IN // D, NH, HD, D)


def _qkv_dw(dqkv, xv, g, dil, name, into=None):
    S = dqkv.shape[1]
    Sd = S // dil
    tk = min(1024, Sd)
    nkb, nk = Sd // tk, S // tk
    hh = NH // 2

    def body(*refs):
        a_ref, b_ref = refs[:2]
        o_ref, acc_ref = refs[-2:]
        k = pl.program_id(1)

        @pl.when(k == 0)
        def _():
            acc_ref[...] = jnp.zeros_like(acc_ref)

        b = b_ref[...]
        for h in range(hh):
            acc_ref[h * QKV:(h + 1) * QKV, :] += _dot(a_ref[h], b, 0, 0)

        @pl.when(k == nk - 1)
        def _():
            for h in range(hh):
                for t in range(3):
                    o_ref[t, h] = acc_ref[h * QKV + t * HD:h * QKV + (t + 1) * HD, :]

    in_specs = [pl.BlockSpec((hh, tk, QKV), lambda j, k: (j, k, 0)), pl.BlockSpec((tk, D), lambda j, k: (k % nkb, k // nkb))]
    args, aliases = [dqkv, xv], {}
    if into is not None:
        aliases = {2: 0}
        in_specs.append(pl.BlockSpec(memory_space=pl.ANY))
        args.append(into)
    return pl.pallas_call(
        body, name=name, grid=(2, nk), in_specs=in_specs,
        out_specs=pl.BlockSpec((3, hh, HD, D), lambda j, k: (g, j, 0, 0)), out_shape=jax.ShapeDtypeStruct(GW_IN_BLOCKS, F32),
        input_output_aliases=aliases, scratch_shapes=[pltpu.VMEM((hh * QKV, D), F32)],
        compiler_params=_cparams(("parallel", "arbitrary")),
    )(*args)


def _band(nb, first_valid, last_valid=None):
    b = lax.broadcasted_iota(jnp.int32, (nb, SPAN, SPAN), 0)
    row = lax.broadcasted_iota(jnp.int32, (nb, SPAN, SPAN), 1)
    col = lax.broadcasted_iota(jnp.int32, (nb, SPAN, SPAN), 2)
    off = jnp.where(b == 0, jnp.where(first_valid, 0, 2 * SPAN), 0)
    if last_valid is not None:
        off = off + jnp.where(b == nb - 1, jnp.where(last_valid, 0, 2 * SPAN), 0)
    return col <= row, col >= row + off


def _attn_tiles(S, dil):
    Sd = S // dil
    T = min(1024, Sd)
    return Sd, T, T // SPAN, Sd // T


def _attn_fwd(qkv, dil, name):
    S = qkv.shape[1]
    Sd, T, nsub, nib = _attn_tiles(S, dil)
    scale = HD ** -0.5

    def body(c_ref, p_ref, o_ref, l_ref):
        ib, h = pl.program_id(1), pl.program_id(2)
        blk, hal = c_ref[...], p_ref[...]
        q, k, v = blk[:, :HD], blk[:, HD:2 * HD], blk[:, 2 * HD:]
        if nsub > 1:
            kp = jnp.concatenate([hal[:, HD:2 * HD], k[:T - SPAN]], axis=0)
            vp = jnp.concatenate([hal[:, 2 * HD:], v[:T - SPAN]], axis=0)
        else:
            kp, vp = hal[:, HD:2 * HD], hal[:, 2 * HD:]
        q3, k3, v3, kp3, vp3 = (t.reshape(nsub, SPAN, HD) for t in (q, k, v, kp, vp))
        m_cur, m_prev = _band(nsub, ib > 0)
        sc = jnp.where(m_cur, _bdot(q3, k3, 2, 2) * scale, NEG)
        sp = jnp.where(m_prev, _bdot(q3, kp3, 2, 2) * scale, NEG)
        m = jnp.maximum(jnp.max(sc, -1, keepdims=True), jnp.max(sp, -1, keepdims=True))
        pc, pp = jnp.exp(sc - m), jnp.exp(sp - m)
        den = jnp.sum(pc, -1, keepdims=True) + jnp.sum(pp, -1, keepdims=True)
        o = (_bdot(pc.astype(BF16), v3, 2, 1) + _bdot(pp.astype(BF16), vp3, 2, 1)) / den
        o_ref[...] = o.reshape(T, HD).astype(BF16)
        lse = (m + jnp.log(den)).reshape(T, 1)
        lane = lax.broadcasted_iota(jnp.int32, (T, HD), 1)

        @pl.when(h == 0)
        def _():
            l_ref[...] = jnp.zeros_like(l_ref)

        l_ref[...] = jnp.where(lane == h, lse, l_ref[...])

    cur = pl.BlockSpec((None, T, QKV), lambda r, ib, h: (h, r * nib + ib, 0))
    prev = pl.BlockSpec((None, SPAN, QKV), lambda r, ib, h: (h, r * (Sd // SPAN) + jnp.maximum(ib * nsub - 1, 0), 0))
    return pl.pallas_call(
        body, name=name, grid=(dil, nib, NH), in_specs=[cur, prev],
        out_specs=[pl.BlockSpec((None, T, HD), lambda r, ib, h: (h, r * nib + ib, 0)),
                   pl.BlockSpec((T, HD), lambda r, ib, h: (r * nib + ib, 0))],
        out_shape=[jax.ShapeDtypeStruct((NH, S, HD), BF16), jax.ShapeDtypeStruct((S, HD), F32)],
        compiler_params=_cparams(("parallel", "parallel", "arbitrary")),
    )(qkv, qkv)


def _attn_bwd(qkv, d_attn, lse, delta, dil, name):
    S = qkv.shape[1]
    Sd, T, nsub, nib = _attn_tiles(S, dil)
    scale = HD ** -0.5
    ne = nsub + 1

    def body(c_ref, p_ref, n_ref, do_ref, don_ref, l_ref, ln_ref, dl_ref, dln_ref, o_ref):
        ib, h = pl.program_id(1), pl.program_id(2)
        blk, hal, nxt = c_ref[...], p_ref[...], n_ref[...]
        q, k, v = blk[:, :HD], blk[:, HD:2 * HD], blk[:, 2 * HD:]
        do = do_ref[...]
        l, dl = _head_col(l_ref[...], h), _head_col(dl_ref[...], h)
        qe = jnp.concatenate([q, nxt[:, :HD]], axis=0).reshape(ne, SPAN, HD)
        doe = jnp.concatenate([do, don_ref[...]], axis=0).reshape(ne, SPAN, HD)
        le = jnp.concatenate([l, _head_col(ln_ref[...], h)], axis=0).reshape(ne, SPAN, 1)
        dle = jnp.concatenate([dl, _head_col(dln_ref[...], h)], axis=0).reshape(ne, SPAN, 1)
        kpe = jnp.concatenate([hal[:, HD:2 * HD], k], axis=0).reshape(ne, SPAN, HD)
        vpe = jnp.concatenate([hal[:, 2 * HD:], v], axis=0).reshape(ne, SPAN, HD)
        _, m_prev = _band(ne, ib > 0, ib < nib - 1)
        p = jnp.where(m_prev, jnp.exp(_bdot(qe, kpe, 2, 2) * scale - le), 0.0)
        ds = (p * (_bdot(doe, vpe, 2, 2) - dle)).astype(BF16)
        dq = _bdot(ds, kpe, 2, 1)[:nsub]
        dk = _bdot(ds, qe, 1, 1)[1:]
        dv = _bdot(p.astype(BF16), doe, 1, 1)[1:]
        q3, k3, v3, do3 = (t.reshape(nsub, SPAN, HD) for t in (q, k, v, do))
        l3, dl3 = l.reshape(nsub, SPAN, 1), dl.reshape(nsub, SPAN, 1)
        m_cur, _ = _band(nsub, True)
        p = jnp.where(m_cur, jnp.exp(_bdot(q3, k3, 2, 2) * scale - l3), 0.0)
        ds = (p * (_bdot(do3, v3, 2, 2) - dl3)).astype(BF16)
        dq = (dq + _bdot(ds, k3, 2, 1)) * scale
        dk = (dk + _bdot(ds, q3, 1, 1)) * scale
        dv = dv + _bdot(p.astype(BF16), do3, 1, 1)
        o_ref[...] = jnp.concatenate([t.reshape(T, HD) for t in (dq, dk, dv)], axis=1).astype(BF16)

    nb = Sd // SPAN
    row = lambda r, ib: r * nib + ib
    prow = lambda r, ib: r * nb + jnp.maximum(ib * nsub - 1, 0)
    nrow = lambda r, ib: r * nb + jnp.minimum((ib + 1) * nsub, nb - 1)
    cur3 = pl.BlockSpec((None, T, QKV), lambda r, ib, h: (h, row(r, ib), 0))
    prev3 = pl.BlockSpec((None, SPAN, QKV), lambda r, ib, h: (h, prow(r, ib), 0))
    next3 = pl.BlockSpec((None, SPAN, QKV), lambda r, ib, h: (h, nrow(r, ib), 0))
    cur1 = pl.BlockSpec((None, T, HD), lambda r, ib, h: (h, row(r, ib), 0))
    next1 = pl.BlockSpec((None, SPAN, HD), lambda r, ib, h: (h, nrow(r, ib), 0))
    curc = pl.BlockSpec((T, HD), lambda r, ib, h: (row(r, ib), 0))
    nextc = pl.BlockSpec((SPAN, HD), lambda r, ib, h: (nrow(r, ib), 0))
    return pl.pallas_call(
        body, name=name, grid=(dil, nib, NH),
        in_specs=[cur3, prev3, next3, cur1, next1, curc, nextc, curc, nextc], out_specs=cur3,
        out_shape=jax.ShapeDtypeStruct((NH, S, QKV), BF16),
        compiler_params=_cparams(("parallel", "parallel", "parallel")),
    )(qkv, qkv, qkv, d_attn, d_attn, lse, lse, delta, delta)


def _ret_consts():
    lg = np.log1p(-np.exp2(-5.0 - np.arange(RH, dtype=np.float64)))
    idx = np.arange(CH, dtype=np.float64)
    rel = idx[:, None] - idx[None, :]
    intra = np.where(rel >= 0, np.exp(lg[:, None, None] * np.maximum(rel, 0.0)), 0.0)
    qd = np.exp(lg[:, None] * (idx + 1.0))
    kd = np.exp(lg[:, None] * (CH - 1.0 - idx))
    cd = np.exp(lg * CH)
    wide = lambda t: np.broadcast_to(t[:, :, None], (RH, t.shape[1], RDV))
    return (jnp.asarray(intra, F32), jnp.asarray(wide(qd), F32), jnp.asarray(wide(kd), F32),
            jnp.asarray(np.broadcast_to(cd[:, None, None], (RH, 1, RDV)), F32))


def _rot(t, c, s):
    t1, t2 = t[:, :RDK // 2], t[:, RDK // 2:]
    return jnp.concatenate([t1 * c - t2 * s, t1 * s + t2 * c], axis=1)


def _unrot(d, c, s):
    d1, d2 = d[:, :RDK // 2], d[:, RDK // 2:]
    return jnp.concatenate([d1 * c + d2 * s, d2 * c - d1 * s], axis=1)


RCH = 2


def _ret_specs(nmap):
    rows = RCH * CH
    q = pl.BlockSpec((rows, RH * RDK), lambda n: (nmap(n), OFF_RQ // (RH * RDK)))
    k = pl.BlockSpec((rows, RH * RDK), lambda n: (nmap(n), OFF_RK // (RH * RDK)))
    v = pl.BlockSpec((rows, RH * RDV), lambda n: (nmap(n), OFF_RV // (RH * RDV)))
    cs = pl.BlockSpec((rows, RDK // 2), lambda n: (nmap(n), 0))
    dmat = pl.BlockSpec((RH, CH, CH), lambda n: (0, 0, 0))
    dvec = pl.BlockSpec((RH, CH, RDV), lambda n: (0, 0, 0))
    cdv = pl.BlockSpec((RH, 1, RDV), lambda n: (0, 0, 0))
    state = pl.BlockSpec((RH, RCH, RDK, RDV), lambda n: (0, nmap(n), 0, 0))
    out = pl.BlockSpec((rows, RH * RDV), lambda n: (nmap(n), 0))
    return [q, k, v, cs, cs, dmat, dvec, dvec, cdv], state, out


def _ret_fwd(proj, cos, sin, consts):
    S = proj.shape[0]
    nc = S // CH

    def body(q_ref, k_ref, v_ref, c_ref, s_ref, d_ref, qd_ref, kd_ref, cd_ref, o_ref, st_ref, state):
        @pl.when(pl.program_id(0) == 0)
        def _():
            state[...] = jnp.zeros_like(state)

        for ci in range(RCH):
            rows = slice(ci * CH, (ci + 1) * CH)
            c, s = c_ref[rows, :], s_ref[rows, :]
            for h in range(RH):
                qk, vv = slice(h * RDK, (h + 1) * RDK), slice(h * RDV, (h + 1) * RDV)
                qb = _rot(q_ref[rows, qk].astype(F32), c, s).astype(BF16)
                kb = (_rot(k_ref[rows, qk].astype(F32), c, s) * (RDK ** -0.5)).astype(BF16)
                vb = v_ref[rows, vv]
                sb = state[h].astype(BF16)
                st_ref[h, ci] = sb
                a = (_dot(qb, kb, 1, 1) * d_ref[h]).astype(BF16)
                o_ref[rows, vv] = (_dot(a, vb, 1, 0) + _dot(qb, sb, 1, 0) * qd_ref[h]).astype(BF16)
                vk = (vb.astype(F32) * kd_ref[h]).astype(BF16)
                state[h] = cd_ref[h] * state[h] + _dot(kb, vk, 0, 0)

    ins, state_spec, out_spec = _ret_specs(lambda n: n)
    return pl.pallas_call(
        body, name="ret_fwd", grid=(nc // RCH,), in_specs=ins, out_specs=[out_spec, state_spec],
        out_shape=[jax.ShapeDtypeStruct((S, RH * RDV), BF16), jax.ShapeDtypeStruct((RH, nc, RDK, RDV), BF16)],
        scratch_shapes=[pltpu.VMEM((RH, RDK, RDV), F32)],
        compiler_params=_cparams(("arbitrary",)),
    )(proj, proj, proj, cos, sin, *consts)


def _ret_bwd(proj, cos, sin, consts, states, d_ret, d_rest):
    S = proj.shape[0]
    nc = S // CH

    def body(q_ref, k_ref, v_ref, c_ref, s_ref, d_ref, qd_ref, kd_ref, cd_ref, st_ref, do_ref, _, o_ref, dstate):
        @pl.when(pl.program_id(0) == 0)
        def _():
            dstate[...] = jnp.zeros_like(dstate)

        for ci in reversed(range(RCH)):
            rows = slice(ci * CH, (ci + 1) * CH)
            c, s = c_ref[rows, :], s_ref[rows, :]
            for h in range(RH):
                qk, vv = slice(h * RDK, (h + 1) * RDK), slice(h * RDV, (h + 1) * RDV)
                qb = _rot(q_ref[rows, qk].astype(F32), c, s).astype(BF16)
                kb = (_rot(k_ref[rows, qk].astype(F32), c, s) * (RDK ** -0.5)).astype(BF16)
                vb, sb, do = v_ref[rows, vv], st_ref[h, ci], do_ref[rows, vv]
                dmat, qd, kd = d_ref[h], qd_ref[h], kd_ref[h]
                a = (_dot(qb, kb, 1, 1) * dmat).astype(BF16)
                doq = (do.astype(F32) * qd).astype(BF16)
                dsb = dstate[h].astype(BF16)
                vk = (vb.astype(F32) * kd).astype(BF16)
                o_ref[rows, OFF_RV + h * RDV:OFF_RV + (h + 1) * RDV] = (_dot(a, do, 0, 0) + _dot(kb, dsb, 1, 0) * kd).astype(BF16)
                da = (_dot(do, vb, 1, 1) * dmat).astype(BF16)
                dq = _dot(da, kb, 1, 0) + _dot(doq, sb, 1, 1)
                dk = (_dot(da, qb, 0, 0) + _dot(vk, dsb, 1, 1)) * (RDK ** -0.5)
                o_ref[rows, OFF_RQ + h * RDK:OFF_RQ + (h + 1) * RDK] = _unrot(dq, c, s).astype(BF16)
                o_ref[rows, OFF_RK + h * RDK:OFF_RK + (h + 1) * RDK] = _unrot(dk, c, s).astype(BF16)
                dstate[h] = cd_ref[h] * dstate[h] + _dot(qb, doq, 0, 0)

    nsteps = nc // RCH
    rev = lambda n: nsteps - 1 - n
    ins, state_spec, out_spec = _ret_specs(rev)
    return pl.pallas_call(
        body, name="ret_bwd", grid=(nsteps,), in_specs=ins + [state_spec, out_spec, pl.BlockSpec(memory_space=pl.ANY)],
        out_specs=pl.BlockSpec((RCH * CH, OFF_RG), lambda n: (rev(n), 0)),
        out_shape=jax.ShapeDtypeStruct(d_rest.shape, BF16), input_output_aliases={11: 0},
        scratch_shapes=[pltpu.VMEM((RH, RDK, RDV), F32)],
        compiler_params=_cparams(("arbitrary",)),
    )(proj, proj, proj, cos, sin, *consts, states, d_ret, d_rest)


CW = 256
HALO = 16


def _shift_down(v, halo, k):
    rolled = pltpu.roll(v, k, 0)
    hr = pltpu.roll(halo, k, 0)[0:8]
    row = lax.broadcasted_iota(jnp.int32, hr.shape, 0)
    return jnp.concatenate([jnp.where(row < k, hr, rolled[0:8]), rolled[8:]], axis=0)


def _shift_up(v, halo, k):
    T = v.shape[0]
    rolled = pltpu.roll(v, T - k, 0)
    hr = pltpu.roll(halo, 8 - k, 0)[0:8]
    row = lax.broadcasted_iota(jnp.int32, hr.shape, 0)
    return jnp.concatenate([rolled[:T - 8], jnp.where(row >= 8 - k, hr, rolled[T - 8:])], axis=0)


def _conv_taps(h_ref, hp_ref, first):
    h = h_ref[...].astype(F32)
    hp = hp_ref[...].astype(F32) * jnp.where(first, 0.0, 1.0)
    return _shift_down(h, hp, 2), _shift_down(h, hp, 1), h


def _conv_specs(S, T):
    nj = DFF // CW
    cur = pl.BlockSpec((T, CW), lambda j, i: (i, j))
    prev = pl.BlockSpec((HALO, CW), lambda j, i: (jnp.maximum(i * (T // HALO) - 1, 0), j))
    nxt = pl.BlockSpec((HALO, CW), lambda j, i: (jnp.minimum((i + 1) * (T // HALO), S // HALO - 1), j))
    w = pl.BlockSpec((3, CW), lambda j, i: (0, j))
    b = pl.BlockSpec((1, CW), lambda j, i: (0, j))
    return nj, cur, prev, nxt, w, b


def _conv_fwd(hg, hu, wg, wu, bg, bu):
    S = hg.shape[0]
    T = min(1024, S)
    nj, cur, prev, _, w, b = _conv_specs(S, T)

    def body(hg_ref, hu_ref, hgp_ref, hup_ref, wg_ref, wu_ref, bg_ref, bu_ref, o_ref):
        first = pl.program_id(1) == 0
        g2, g1, g0 = _conv_taps(hg_ref, hgp_ref, first)
        u2, u1, u0 = _conv_taps(hu_ref, hup_ref, first)
        cg = wg_ref[0:1, :] * g2 + wg_ref[1:2, :] * g1 + wg_ref[2:3, :] * g0 + bg_ref[...]
        cu = wu_ref[0:1, :] * u2 + wu_ref[1:2, :] * u1 + wu_ref[2:3, :] * u0 + bu_ref[...]
        o_ref[...] = (_gelu(cg)[0] * cu).astype(BF16)

    return pl.pallas_call(
        body, name="conv_fwd", grid=(nj, S // T), in_specs=[cur, cur, prev, prev, w, w, b, b], out_specs=cur,
        out_shape=jax.ShapeDtypeStruct((S, DFF), BF16), compiler_params=_cparams(("parallel", "parallel")),
    )(hg, hu, hg, hu, wg, wu, bg, bu)


def _conv_bwd_pre(d_act, hg, hu, wg, wu, bg, bu):
    S = hg.shape[0]
    T = min(1024, S)
    nj, cur, prev, _, w, b = _conv_specs(S, T)

    def body(da_ref, hg_ref, hu_ref, hgp_ref, hup_ref, wg_ref, wu_ref, bg_ref, bu_ref,
             dcg_ref, dcu_ref, gwg_ref, gwu_ref, gbg_ref, gbu_ref):
        first = pl.program_id(1) == 0
        g2, g1, g0 = _conv_taps(hg_ref, hgp_ref, first)
        u2, u1, u0 = _conv_taps(hu_ref, hup_ref, first)
        cg = wg_ref[0:1, :] * g2 + wg_ref[1:2, :] * g1 + wg_ref[2:3, :] * g0 + bg_ref[...]
        cu = wu_ref[0:1, :] * u2 + wu_ref[1:2, :] * u1 + wu_ref[2:3, :] * u0 + bu_ref[...]
        da = da_ref[...].astype(F32)
        gl, t = _gelu(cg)
        dcg = da * cu * _gelu_grad(cg, t)
        dcu = da * gl
        dcg_ref[...] = dcg.astype(BF16)
        dcu_ref[...] = dcu.astype(BF16)

        @pl.when(first)
        def _():
            for r in (gwg_ref, gwu_ref, gbg_ref, gbu_ref):
                r[...] = jnp.zeros_like(r)

        for r, d, taps in ((gwg_ref, dcg, (g2, g1, g0)), (gwu_ref, dcu, (u2, u1, u0))):
            for j in range(3):
                r[j:j + 1, :] += jnp.sum(d * taps[j], 0, keepdims=True)
        gbg_ref[...] += jnp.sum(dcg, 0, keepdims=True)
        gbu_ref[...] += jnp.sum(dcu, 0, keepdims=True)

    return pl.pallas_call(
        body, name="conv_bwd_pre", grid=(nj, S // T), in_specs=[cur, cur, cur, prev, prev, w, w, b, b],
        out_specs=[cur, cur, w, w, b, b],
        out_shape=[jax.ShapeDtypeStruct((S, DFF), BF16)] * 2 + [jax.ShapeDtypeStruct((3, DFF), F32)] * 2
        + [jax.ShapeDtypeStruct((1, DFF), F32)] * 2,
        compiler_params=_cparams(("parallel", "arbitrary")),
    )(d_act, hg, hu, hg, hu, wg, wu, bg, bu)


def _conv_bwd_in(dc, w, name):
    S = dc.shape[0]
    T = min(1024, S)
    nj, cur, _, nxt, wspec, _ = _conv_specs(S, T)
    nt = S // T

    def body(dc_ref, dn_ref, w_ref, o_ref):
        d = dc_ref[...].astype(F32)
        dn = dn_ref[...].astype(F32) * jnp.where(pl.program_id(1) == nt - 1, 0.0, 1.0)
        o_ref[...] = (w_ref[2:3, :] * d + w_ref[1:2, :] * _shift_up(d, dn, 1) + w_ref[0:1, :] * _shift_up(d, dn, 2)).astype(BF16)

    return pl.pallas_call(
        body, name=name, grid=(nj, nt), in_specs=[cur, nxt, wspec], out_specs=cur,
        out_shape=jax.ShapeDtypeStruct((S, DFF), BF16), compiler_params=_cparams(("parallel", "parallel")),
    )(dc, dc, w)


def _adam_math(g, w, m, v):
    m = B1 * m + (1.0 - B1) * g
    v = B2 * v + (1.0 - B2) * (g * g)
    m_hat = m / (1.0 - B1 ** STEP)
    v_hat = v / (1.0 - B2 ** STEP)
    return -LR * (m_hat / (jnp.sqrt(v_hat) + EPS) + WD * w), m, v


def _reduce_tail(chip32, far, chip, name, wmv=None):
    L = len(chip32)
    _, R, C = chip32[0].shape
    tr = _tile(R, 256, 16)
    nr = R // tr

    def body(chip_ref, *refs):
        own_refs, far_refs, rest = refs[:L], refs[L:2 * L], refs[2 * L:]
        outs = rest[3:] if wmv else rest
        for ll in range(L):
            @pl.when(pl.program_id(0) == ll)
            def _(ll=ll):
                g = own_refs[ll][...]
                for s in range(3):
                    g = g + far_refs[ll][s].astype(F32)
                outs[0][...] = g
                if wmv:
                    outs[1][...], outs[2][...], outs[3][...] = _adam_math(g, rest[0][...], rest[1][...], rest[2][...])

    def rows(ll):
        return lambda l, i: jnp.where(l == ll, i, jnp.where(l < ll, 0, nr - 1))

    blk = pl.BlockSpec((None, tr, C), lambda l, i, ch: (l, i, 0))
    in_specs = [pl.BlockSpec((None, tr, C), lambda l, i, ch, f=rows(ll): (ch[0], f(l, i), 0)) for ll in range(L)]
    in_specs += [pl.BlockSpec((3, tr, C), lambda l, i, ch, f=rows(ll): (0, f(l, i), 0)) for ll in range(L)]
    args = list(chip32) + list(far)
    n_out = 1
    if wmv:
        in_specs += [blk] * 3
        args += list(wmv)
        n_out = 4
    return pl.pallas_call(
        body, name=name,
        grid_spec=pltpu.PrefetchScalarGridSpec(num_scalar_prefetch=1, grid=(L, nr), in_specs=in_specs, out_specs=[blk] * n_out),
        out_shape=[jax.ShapeDtypeStruct((L, R, C), F32)] * n_out, compiler_params=_cparams(("arbitrary", "arbitrary")),
    )(chip, *args)


def _adamw(g, w, m, v, name):
    R, C = g.shape
    tr = _tile(R, 128, 8)

    def body(g_ref, w_ref, m_ref, v_ref, d_ref, nm_ref, nv_ref):
        d_ref[...], nm_ref[...], nv_ref[...] = _adam_math(g_ref[...], w_ref[...], m_ref[...], v_ref[...])

    blk = pl.BlockSpec((tr, C), lambda i: (i, 0))
    return pl.pallas_call(
        body, name=name, grid=(R // tr,), in_specs=[blk] * 4, out_specs=[blk] * 3,
        out_shape=[jax.ShapeDtypeStruct(g.shape, F32)] * 3, compiler_params=_cparams(("parallel",)),
    )(g, w, m, v)


def _pair_sum(x, recv, core, name):
    _, R, C = x.shape
    tr = _tile(R, 600, 16)

    def body(core_ref, x_ref, r_ref, o32_ref, o16_ref):
        s = x_ref[...] + r_ref[...]
        o32_ref[...] = s
        o16_ref[...] = s.astype(BF16)

    blk = pl.BlockSpec((None, tr, C), lambda q, i, c: (q, i, 0))
    mine = pl.BlockSpec((None, None, tr, C), lambda q, i, c: (q, c[0], i, 0))
    return pl.pallas_call(
        body, name=name,
        grid_spec=pltpu.PrefetchScalarGridSpec(num_scalar_prefetch=1, grid=(4, R // tr), in_specs=[mine, blk], out_specs=[blk, blk]),
        out_shape=[jax.ShapeDtypeStruct((4, R, C), F32), jax.ShapeDtypeStruct((4, R, C), BF16)],
        compiler_params=_cparams(("parallel", "parallel")),
    )(core, x.reshape(4, 2, R, C), recv)


def _sum_slots(x, name):
    def body(x_ref, o_ref):
        g = x_ref[0]
        for s in range(1, x.shape[0]):
            g = g + x_ref[s]
        o_ref[...] = g

    return pl.pallas_call(body, name=name, out_shape=jax.ShapeDtypeStruct(x.shape[1:], F32))(x)


MESH = pl.DeviceIdType.MESH
_HBM = pl.BlockSpec(memory_space=pltpu.HBM)


def _dma_sems(n):
    return pltpu.SemaphoreType.DMA((n,))


def _gather_many(xs, name):
    n = len(xs)

    def body(*refs):
        x_refs, out_refs = refs[:n], refs[n:2 * n]
        send_sems, recv_sems, local_sems = refs[2 * n:]
        ax, ay, ac = lax.axis_index("x"), lax.axis_index("y"), lax.axis_index("c")
        me, sibling = (ax, ay, ac), (ax, ay, 1 - ac)
        chips = [(1 - ax, ay), (ax, 1 - ay), (1 - ax, 1 - ay)]

        def copy(a, k, block, to, own=False):
            slot = out_refs[a].at[4 * block[0] + 2 * block[1] + block[2]]
            return pltpu.make_async_remote_copy(
                src_ref=x_refs[a] if own else slot, dst_ref=slot, send_sem=send_sems.at[7 * a + k],
                recv_sem=recv_sems.at[7 * a + k], device_id=to, device_id_type=MESH)

        mine = [pltpu.make_async_copy(x_refs[a], out_refs[a].at[4 * ax + 2 * ay + ac], local_sems.at[a]) for a in range(n)]
        first = [copy(a, 0, me, sibling, own=True) for a in range(n)]
        first += [copy(a, 1 + j, me, (*chip, ac), own=True) for j, chip in enumerate(chips) for a in range(n)]
        for cp in mine + first:
            cp.start()
        passed = []
        for j, chip in enumerate(chips):
            for a in range(n):
                copy(a, 1 + j, (*chip, ac), me).wait_recv()
                cp = copy(a, 4 + j, (*chip, ac), sibling)
                cp.start()
                passed.append(cp)
        for a in range(n):
            copy(a, 0, sibling, me).wait_recv()
            for j, chip in enumerate(chips):
                copy(a, 4 + j, (*chip, 1 - ac), me).wait_recv()
        for cp in first + passed:
            cp.wait_send()
        for cp in mine:
            cp.wait()

    return pl.pallas_call(
        body, name=name, out_shape=[jax.ShapeDtypeStruct((N_DEV,) + x.shape, x.dtype) for x in xs],
        in_specs=[_HBM] * n, out_specs=[_HBM] * n, scratch_shapes=[_dma_sems(7 * n), _dma_sems(7 * n), _dma_sems(n)],
    )(*xs)


def _exchange_cores(xs, name):
    n = len(xs)

    def body(*refs):
        x_refs, out_refs = refs[:n], refs[n:2 * n]
        send_sems, recv_sems = refs[2 * n:]
        ax, ay, ac = lax.axis_index("x"), lax.axis_index("y"), lax.axis_index("c")
        copies = []
        for a in range(n):
            for q in range(4):
                copies.append(pltpu.make_async_remote_copy(
                    src_ref=x_refs[a].at[2 * q + 1 - ac], dst_ref=out_refs[a].at[q], send_sem=send_sems.at[4 * a + q],
                    recv_sem=recv_sems.at[4 * a + q], device_id=(ax, ay, 1 - ac), device_id_type=MESH))
        for cp in copies:
            cp.start()
        for cp in copies:
            cp.wait_recv()
        for cp in copies:
            cp.wait_send()

    return pl.pallas_call(
        body, name=name, out_shape=[jax.ShapeDtypeStruct((4,) + x.shape[1:], x.dtype) for x in xs],
        in_specs=[_HBM] * n, out_specs=[_HBM] * n, scratch_shapes=[_dma_sems(4 * n), _dma_sems(4 * n)],
    )(*xs)


def _exchange_chips(ps, name):
    n = len(ps)

    def body(*refs):
        p_refs, out_refs = refs[:n], refs[n:2 * n]
        send_sems, recv_sems = refs[2 * n:]
        ax, ay, ac = lax.axis_index("x"), lax.axis_index("y"), lax.axis_index("c")
        copies = []
        for a in range(n):
            for k in range(1, 4):
                px = 1 - ax if k & 2 else ax
                py = 1 - ay if k & 1 else ay
                copies.append(pltpu.make_async_remote_copy(
                    src_ref=p_refs[a].at[2 * px + py], dst_ref=out_refs[a].at[k - 1], send_sem=send_sems.at[3 * a + k - 1],
                    recv_sem=recv_sems.at[3 * a + k - 1], device_id=(px, py, ac), device_id_type=MESH))
        for cp in copies:
            cp.start()
        for cp in copies:
            cp.wait_recv()
        for cp in copies:
            cp.wait_send()

    return pl.pallas_call(
        body, name=name, out_shape=[jax.ShapeDtypeStruct((3,) + p.shape[1:], p.dtype) for p in ps],
        in_specs=[_HBM] * n, out_specs=[_HBM] * n, scratch_shapes=[_dma_sems(3 * n), _dma_sems(3 * n)],
    )(*ps)


def _x_view(xb, d):
    return xb if d == 1 else xb.reshape(xb.shape[0] // d, d * xb.shape[1])


def _layer_fwd(x, xb, p, w, cos, sin, rconsts):
    S = x.shape[0]
    proj = _mm(xb, w["win"], tb=True, b_rows=(N_ATT, N_REST), name="mm_proj", out_dtype=BF16)
    qkvs, ogs, lgs = [], [], []
    for g, dil in enumerate(DILATIONS):
        qkv = _qkv_fwd(_x_view(xb, dil), w["win"], g, dil, f"mm_qkv{g}")
        o, l = _attn_fwd(qkv, dil, f"attn_fwd_g{g}")
        qkvs.append(qkv)
        ogs.append(_to_tokens(o, dil))
        lgs.append(_to_tokens(l, dil))
    attn, lse = _rowwise(_f_combine, ogs + lgs, [], [(D, BF16), (HD, F32)], [], name="attn_combine")
    ret_raw, states = _ret_fwd(proj, cos, sin, rconsts)
    rg_win = (proj, RH * RDV, OFF_RG // (RH * RDV))
    ga_win, gr_win = (proj, D, OFF_GA // D), (proj, D, OFF_GR // D)
    (r,) = _rowwise(_f_gn, [ret_raw, rg_win], [w["ret_gn_g"], w["ret_gn_b"]], [(RH * RDV, BF16)], [], name="gn_fwd", tm=256)
    ap = _mm(attn, w["w_attn_proj"], name="mm_attn_proj", out_dtype=BF16)
    rp = _mm(r, w["w_ret_proj"], name="mm_ret_proj", out_dtype=BF16, tk=2048)
    (merged,) = _rowwise(_f_gate, [ap, rp, ga_win, gr_win], [], [(D, BF16)], [], name="gate_fwd")
    mix = _mm(merged, w["w_out"], name="mm_out")
    h1, x1, x1b = _rowwise(_f_ln1, [x, mix], [w["ln1_g"], w["ln1_b"]], [(D, F32), (D, F32), (D, BF16)], [], name="ln1_fwd")
    z = _mm(x1b, w["w_ple_gate"], name="mm_ple_gate")
    pp = _mm(p, w["w_ple_proj"], tb=True, name="mm_ple_proj")
    hg = _mm(x1b, w["w_up"], tb=True, b_rows=(0, DFF), name="mm_up_g", out_dtype=BF16, tm=512, tn=DFF)
    hu = _mm(x1b, w["w_up"], tb=True, b_rows=(DFF, DFF), name="mm_up_u", out_dtype=BF16, tm=512, tn=DFF)
    act = _conv_fwd(hg, hu, w["conv_wg"], w["conv_wu"], w["conv_bg"], w["conv_bu"])
    ffn = _mm(act, w["w_down"], name="mm_down", tm=512, tk=DFF)
    h2, x2, x2b = _rowwise(_f_ln2, [x1, ffn, z, pp], [w["ln2_g"], w["ln2_b"]], [(D, F32), (D, F32), (D, BF16)], [], name="ln2_fwd")
    saved = dict(xb=xb, proj=proj, qkvs=qkvs, attn=attn, lse=lse, ret_raw=ret_raw, states=states, r=r, ap=ap, rp=rp,
                 merged=merged, h1=h1, x1b=x1b, z=z, pp=pp, hg=hg, hu=hu, act=act, h2=h2, p=p)
    return x2, x2b, saved


def _layer_bwd(dys, w, sv, cos, sin, rconsts):
    gr = {}
    proj = sv["proj"]
    dh2, dh2b, gr["ln2_g"], gr["ln2_b"] = _rowwise(_f_ln_bwd, list(dys) + [sv["h2"]], [w["ln2_g"]], [(D, F32), (D, BF16)],
                                                   [(1, D), (1, D)], name="ln2_bwd")
    d_act = _mm(dh2b, w["w_down"], tb=True, name="mm_down_dx", out_dtype=BF16, tm=512, tn=DFF)
    gr["w_down"] = _mm(sv["act"], dh2b, ta=True, name="mm_down_dw", tm=DFF // 2)
    dcg, dcu, gwg, gwu, gbg, gbu = _conv_bwd_pre(d_act, sv["hg"], sv["hu"], w["conv_wg"], w["conv_wu"], w["conv_bg"], w["conv_bu"])
    gr["conv_w"] = jnp.concatenate([gwg, gwu], axis=1)
    gr["conv_b"] = jnp.concatenate([gbg, gbu], axis=1)
    dhg = _conv_bwd_in(dcg, w["conv_wg"], "conv_bwd_in_g")
    dhu = _conv_bwd_in(dcu, w["conv_wu"], "conv_bwd_in_u")
    gw_up = _mm(dhg, sv["x1b"], ta=True, name="mm_up_g_dw", tm=DFF // 2, out_rows=(0, 2 * DFF))
    gr["w_up"] = _mm(dhu, sv["x1b"], ta=True, name="mm_up_u_dw", tm=DFF // 2, out_rows=(DFF, 2 * DFF), into=gw_up)
    dx1 = _mm(dhg, w["w_up"], b_rows=(0, DFF), name="mm_up_g_dx", add=dh2, add_scale=ALPHA, tm=512, tk=DFF)
    dx1 = _mm(dhu, w["w_up"], b_rows=(DFF, DFF), name="mm_up_u_dx", add=dx1, tm=512, tk=DFF)
    dpp, dz = _rowwise(_f_ple_bwd, [dh2, sv["z"], sv["pp"]], [], [(D, BF16), (D, BF16)], [], name="ple_bwd")
    gr["w_ple_proj"] = _mm(dpp, sv["p"], ta=True, name="mm_ple_proj_dw")
    gr["w_ple_gate"] = _mm(sv["x1b"], dz, ta=True, name="mm_ple_gate_dw")
    dx1 = _mm(dz, w["w_ple_gate"], tb=True, name="mm_ple_gate_dx", add=dx1)
    dh1, dh1b, gr["ln1_g"], gr["ln1_b"] = _rowwise(_f_ln_bwd, [dx1, sv["h1"]], [w["ln1_g"]], [(D, F32), (D, BF16)],
                                                   [(1, D), (1, D)], name="ln1_bwd")
    d_merged = _mm(dh1b, w["w_out"], tb=True, name="mm_out_dx", out_dtype=BF16)
    gr["w_out"] = _mm(sv["merged"], dh1b, ta=True, name="mm_out_dw")
    rg_win = (proj, RH * RDV, OFF_RG // (RH * RDV))
    ga_win, gr_win = (proj, D, OFF_GA // D), (proj, D, OFF_GR // D)
    dap, drp, d_rest = _rowwise(_f_gate_bwd, [d_merged, sv["ap"], sv["rp"], ga_win, gr_win], [],
                                [(D, BF16), (D, BF16), (2 * D, BF16, N_REST, OFF_GA // (2 * D), None)], [], name="gate_bwd")
    d_attn = _mm(dap, w["w_attn_proj"], tb=True, name="mm_attn_proj_dx", out_dtype=BF16)
    gr["w_attn_proj"] = _mm(sv["attn"], dap, ta=True, name="mm_attn_proj_dw")
    d_r = _mm(drp, w["w_ret_proj"], tb=True, name="mm_ret_proj_dx", out_dtype=BF16, tn=2048)
    gr["w_ret_proj"] = _mm(sv["r"], drp, ta=True, name="mm_ret_proj_dw", tm=2048)
    d_ret, d_rest, gr["ret_gn_g"], gr["ret_gn_b"] = _rowwise(
        _f_gn_bwd, [d_r, sv["ret_raw"], rg_win], [w["ret_gn_g"], w["ret_gn_b"]],
        [(RH * RDV, BF16), (RH * RDV, BF16, N_REST, OFF_RG // (RH * RDV), d_rest)],
        [(1, RH * RDV), (1, RH * RDV)], name="gn_bwd", tm=256)
    d_rest = _ret_bwd(proj, cos, sin, rconsts, sv["states"], d_ret, d_rest)
    dx0 = _mm(d_rest, w["win"], b_rows=(N_ATT, N_REST), name="mm_proj_dx", add=dh1, add_scale=ALPHA)
    (delta,) = _rowwise(_f_delta, [d_attn, sv["attn"]], [], [(HD, F32)], [], name="attn_delta")
    gw_in, dx_parts = None, []
    for g, dil in enumerate(DILATIONS):
        dqkv = _attn_bwd(sv["qkvs"][g], _to_head_residues(d_attn, dil), _to_residues(sv["lse"], dil), _to_residues(delta, dil),
                         dil, f"attn_bwd_g{g}")
        gw_in = _qkv_dw(dqkv, _x_view(sv["xb"], dil), g, dil, f"mm_qkv{g}_dw", into=gw_in)
        if dil == 1:
            dx0 = _qkv_dx(dqkv, w["win"], g, dil, f"mm_qkv{g}_dx", F32, add=dx0)
        else:
            dx_parts.append(_qkv_dx(dqkv, w["win"], g, dil, f"mm_qkv{g}_dx", BF16).reshape(dx0.shape))
    gw_in = _mm(d_rest, sv["xb"], ta=True, name="mm_proj_dw", out_rows=(N_ATT, N_IN), into=gw_in, blocks8=True)
    gr["w_in"] = gw_in.reshape(N_IN, D)
    return [dx0] + dx_parts, gr


def _local_step(x, p, positions, target, ws):
    half = RDK // 2
    freq = jnp.power(ROPE_BASE, -jnp.arange(half, dtype=F32) / half)
    ang = positions.astype(F32)[:, None] * freq[None, :]
    cos, sin = jnp.cos(ang), jnp.sin(ang)
    rconsts = _ret_consts()
    xb = x.astype(BF16)
    saved = []
    for l in range(DEPTH):
        x, xb, sv = _layer_fwd(x, xb, p[l], ws[l], cos, sin, rconsts)
        saved.append(sv)
    dy, loss_vec = _rowwise(_f_loss, [x, target], [], [(D, F32)], [(1, D)], name="loss")
    dys, grads = [dy], [None] * DEPTH
    for l in reversed(range(DEPTH)):
        dys, grads[l] = _layer_bwd(dys, ws[l], saved[l], cos, sin, rconsts)
    (grad_x,) = _rowwise(_f_sum, dys, [], [(D, F32)], [], name="grad_x_sum")
    return loss_vec, grad_x, grads


def _pack_rows(arrs):
    parts, where, off = [], [], 0
    for t in arrs:
        t = t.reshape(-1, D)
        rows = t.shape[0]
        padded = -(-rows // 8) * 8
        parts.append(jnp.pad(t, ((0, padded - rows), (0, 0))))
        where.append((off, rows))
        off += padded
    return jnp.concatenate(parts, axis=0), where


def _layer_weights(g, l, conv_w_all, conv_b, W):
    w = dict(win=g["w_in"].reshape(N_IN, D), w_up=g["w_up"].reshape(2 * DFF, D),
             w_ple_proj=g["w_ple_proj"].reshape(D, PLE), w_attn_proj=g["w_attn_proj"].reshape(D, D),
             w_ret_proj=g["w_ret_proj"].reshape(RH * RDV, D), w_out=g["w_out"].reshape(D, D),
             w_down=g["w_down"].reshape(DFF, D), w_ple_gate=g["w_ple_gate"].reshape(D, D))
    w["conv_wg"], w["conv_wu"] = conv_w_all[l][:, :DFF], conv_w_all[l][:, DFF:]
    w["conv_bg"], w["conv_bu"] = conv_b[l][None, :DFF], conv_b[l][None, DFF:]
    for n in ("ret_gn_g", "ret_gn_b", "ln1_g", "ln1_b", "ln2_g", "ln2_b"):
        w[n] = W[n][l][None, :]
    return w


def kernel(x, p, positions, w_in, w_attn_proj, w_ret_proj, ret_gn_g, ret_gn_b, w_out, ln1_g, ln1_b, w_up, conv_w, conv_b, w_down, w_ple_gate, w_ple_proj, ln2_g, ln2_b, loss_target, m_w_in, m_w_attn_proj, m_w_ret_proj, m_ret_gn_g, m_ret_gn_b, m_w_out, m_ln1_g, m_ln1_b, m_w_up, m_conv_w, m_conv_b, m_w_down, m_w_ple_gate, m_w_ple_proj, m_ln2_g, m_ln2_b, v_w_in, v_w_attn_proj, v_w_ret_proj, v_ret_gn_g, v_ret_gn_b, v_w_out, v_ln1_g, v_ln1_b, v_w_up, v_conv_w, v_conv_b, v_w_down, v_w_ple_gate, v_w_ple_proj, v_ln2_g, v_ln2_b):
    W = dict(w_in=w_in, w_attn_proj=w_attn_proj, w_ret_proj=w_ret_proj, ret_gn_g=ret_gn_g, ret_gn_b=ret_gn_b, w_out=w_out,
             ln1_g=ln1_g, ln1_b=ln1_b, w_up=w_up, conv_w=conv_w, conv_b=conv_b, w_down=w_down, w_ple_gate=w_ple_gate,
             w_ple_proj=w_ple_proj, ln2_g=ln2_g, ln2_b=ln2_b)
    M = dict(w_in=m_w_in, w_attn_proj=m_w_attn_proj, w_ret_proj=m_w_ret_proj, ret_gn_g=m_ret_gn_g, ret_gn_b=m_ret_gn_b,
             w_out=m_w_out, ln1_g=m_ln1_g, ln1_b=m_ln1_b, w_up=m_w_up, conv_w=m_conv_w, conv_b=m_conv_b, w_down=m_w_down,
             w_ple_gate=m_w_ple_gate, w_ple_proj=m_w_ple_proj, ln2_g=m_ln2_g, ln2_b=m_ln2_b)
    V = dict(w_in=v_w_in, w_attn_proj=v_w_attn_proj, w_ret_proj=v_w_ret_proj, ret_gn_g=v_ret_gn_g, ret_gn_b=v_ret_gn_b,
             w_out=v_w_out, ln1_g=v_ln1_g, ln1_b=v_ln1_b, w_up=v_w_up, conv_w=v_conv_w, conv_b=v_conv_b, w_down=v_w_down,
             w_ple_gate=v_w_ple_gate, w_ple_proj=v_w_ple_proj, ln2_g=v_ln2_g, ln2_b=v_ln2_b)

    ws, conv_w_all = [], None
    for l in range(DEPTH):
        shards = [(W[n][l].T if n in COL_SHARDED else W[n][l]).astype(BF16) for n in BIG]
        outs = _gather_many(shards + ([conv_w] if l == 0 else []), f"gather_weights_l{l}")
        if l == 0:
            conv_w_all = outs[-1].transpose(1, 2, 0, 3).reshape(DEPTH, 3, 2 * DFF)
        ws.append(_layer_weights(dict(zip(BIG, outs)), l, conv_w_all, conv_b, W))

    loss_vec, grad_x, grads = _local_step(x[0], p[:, 0], positions[0], loss_target[0], ws)
    loss = lax.psum(jnp.sum(loss_vec), ("x", "y", "c"))

    core = lax.axis_index("c").astype(jnp.int32).reshape(1)
    chip = (2 * lax.axis_index("x") + lax.axis_index("y")).astype(jnp.int32).reshape(1)
    mine = [grads[l][n].reshape((N_DEV, -1) + grads[l][n].shape[1:]) for n in BIG for l in range(DEPTH)]
    theirs = _exchange_cores(mine, "exchange_grads_cores")
    sums = [_pair_sum(a, b, core, f"pair_sum_{i}") for i, (a, b) in enumerate(zip(mine, theirs))]
    fars = _exchange_chips([s16 for _, s16 in sums], "exchange_grads_chips")
    G, DW, NM, NV = ({} for _ in range(4))
    for a, n in enumerate(BIG):
        chip32 = [sums[DEPTH * a + l][0] for l in range(DEPTH)]
        far = [fars[DEPTH * a + l] for l in range(DEPTH)]
        if n in COL_SHARDED:
            G[n] = _reduce_tail(chip32, far, chip, f"reduced_{n}")[0].transpose(0, 2, 1)
            R2, C2 = DEPTH * W[n].shape[1], W[n].shape[2]
            res = _adamw(*(t.reshape(R2, C2) for t in (G[n], W[n], M[n], V[n])), f"adamw_{n}")
            DW[n], NM[n], NV[n] = (t.reshape(W[n].shape) for t in res)
        else:
            G[n], DW[n], NM[n], NV[n] = _reduce_tail(chip32, far, chip, f"adamw_{n}", wmv=(W[n], M[n], V[n]))

    small_names = SMALL + ("conv_w",)
    g_small, where = _pack_rows([jnp.stack([grads[l][n] for l in range(DEPTH)]) for n in small_names])
    (g_all,) = _gather_many([g_small], "gather_small_grads")
    g_small = _sum_slots(g_all, "sum_small_grads")
    for n, (off, rows) in zip(SMALL, where):
        G[n] = g_small[off:off + rows].reshape(W[n].shape)
    off, rows = where[-1]
    g_cw = g_small[off:off + rows].reshape(DEPTH, 3, N_DEV, conv_w.shape[2])
    me = 4 * lax.axis_index("x") + 2 * lax.axis_index("y") + lax.axis_index("c")
    G["conv_w"] = lax.dynamic_index_in_dim(g_cw, me, axis=2, keepdims=False)
    packed = [_pack_rows([d[n] for n in SMALL]) for d in (G, W, M, V)]
    small_out = _adamw(*(t for t, _ in packed), "adamw_small")
    for res, dst in zip(small_out, (DW, NM, NV)):
        for n, (off, rows) in zip(SMALL, packed[0][1]):
            dst[n] = res[off:off + rows].reshape(W[n].shape)
    two_d = lambda t: t.reshape(DEPTH * 3, conv_w.shape[2])
    cw_out = _adamw(two_d(G["conv_w"]), two_d(conv_w), two_d(m_conv_w), two_d(v_conv_w), "adamw_conv_w")
    for res, dst in zip(cw_out, (DW, NM, NV)):
        dst["conv_w"] = res.reshape(conv_w.shape)

    return (loss, grad_x[None], *[G[n] for n in WEIGHTS], *[DW[n] for n in WEIGHTS], *[NM[n] for n in WEIGHTS],
            *[NV[n] for n in WEIGHTS])
```

```python
import math

import numpy as np
import jax
import jax.numpy as jnp
from jax import lax
from jax.experimental import pallas as pl
from jax.experimental.pallas import tpu as pltpu

F32, BF16 = jnp.float32, jnp.bfloat16

D = 1024
DEPTH = 2
N_DEV = 8
HD = 128
NH = 8
DILATIONS = (1, 4, 16)
SPAN = 128
N_ATT = 3 * 3 * NH * HD
RH, RDK, RDV = 4, 256, 512
CH = 128
DFF = 2816
PLE = 256
N_IN = 17408
N_REST = N_IN - N_ATT
OFF_RQ, OFF_RK, OFF_RV, OFF_RG, OFF_GA, OFF_GR = 0, 1024, 2048, 4096, 6144, 7168
ALPHA = (2 * DEPTH) ** 0.25
LN_EPS, GN_EPS = 1e-5, 1e-6
ROPE_BASE = 10000.0
LR, B1, B2, EPS, WD, STEP = 0.001, 0.9, 0.999, 1e-8, 0.01, 10
VMEM_LIMIT = 48 * 1024 * 1024
NEG = -1e30

BIG = ("w_in", "w_attn_proj", "w_ret_proj", "w_out", "w_up", "w_down", "w_ple_gate", "w_ple_proj")
COL_SHARDED = ("w_in", "w_up", "w_ple_proj")
F32_OVER_ICI = ("w_attn_proj", "w_out", "w_ple_gate", "w_ple_proj")
SMALL = ("ret_gn_g", "ret_gn_b", "ln1_g", "ln1_b", "conv_b", "ln2_g", "ln2_b")
WEIGHTS = ("w_in", "w_attn_proj", "w_ret_proj", "ret_gn_g", "ret_gn_b", "w_out", "ln1_g", "ln1_b", "w_up",
           "conv_w", "conv_b", "w_down", "w_ple_gate", "w_ple_proj", "ln2_g", "ln2_b")


def _tile(n, cap, mult=128):
    if n <= cap:
        return n
    t = (cap // mult) * mult
    while n % t:
        t -= mult
    return t


def _cparams(sem):
    return pltpu.CompilerParams(dimension_semantics=sem, vmem_limit_bytes=VMEM_LIMIT)


def _dot(a, b, ca, cb):
    return lax.dot_general(a, b, (((ca,), (cb,)), ((), ())), preferred_element_type=F32)


def _bdot(a, b, ca, cb):
    return lax.dot_general(a, b, (((ca,), (cb,)), ((0,), (0,))), preferred_element_type=F32)


def _mm(a, b, *, name, ta=False, tb=False, out_dtype=F32, add=None, add_scale=1.0, tm=1024, tn=1024, tk=1024,
        b_rows=None, out_rows=None, into=None, blocks8=False):
    M, K = (a.shape[1], a.shape[0]) if ta else a.shape
    b_first, b_count = b_rows if b_rows else (0, b.shape[0])
    N = b_count if tb else b.shape[1]
    assert K == (b.shape[1] if tb else b_count)
    tm, tn, tk = _tile(M, tm), _tile(N, tn), _tile(K, tk)
    nk = K // tk
    o_first, o_total = out_rows if out_rows else (0, M)
    jb, kb, io = (b_first // tn, 0, o_first // tm) if tb else (0, b_first // tk, o_first // tm)
    assert b_first % (tn if tb else tk) == 0 and o_first % tm == 0 and (add is None or out_rows is None)

    def body(*refs):
        if add is None:
            a_ref, b_ref = refs[:2]
        else:
            a_ref, b_ref, add_ref = refs[:3]
        o_ref, acc_ref = refs[-2:]
        k = pl.program_id(2)

        @pl.when(k == 0)
        def _():
            acc_ref[...] = jnp.zeros_like(acc_ref)

        acc_ref[...] += _dot(a_ref[...].astype(BF16), b_ref[...].astype(BF16), 0 if ta else 1, 1 if tb else 0)

        @pl.when(k == nk - 1)
        def _():
            r = acc_ref[...]
            if add is not None:
                r = r + add_scale * add_ref[...].astype(F32)
            o_ref[...] = r.astype(out_dtype).reshape(o_ref.shape)

    a_spec = pl.BlockSpec((tk, tm), lambda i, j, k: (k, i)) if ta else pl.BlockSpec((tm, tk), lambda i, j, k: (i, k))
    if tb:
        b_spec = pl.BlockSpec((tn, tk), lambda i, j, k: (j + jb, k))
    else:
        b_spec = pl.BlockSpec((tk, tn), lambda i, j, k: (k + kb, j))
    if blocks8:
        assert tm == 1024
        o_spec = pl.BlockSpec((1, 8, 128, tn), lambda i, j, k: (i + io, 0, 0, j))
        o_shape = (o_total // tm, 8, 128, N)
    else:
        o_spec = pl.BlockSpec((tm, tn), lambda i, j, k: (i + io, j))
        o_shape = (o_total, N)
    in_specs, args, aliases = [a_spec, b_spec], [a, b], {}
    if add is not None:
        in_specs.append(o_spec)
        args.append(add)
    if into is not None:
        aliases = {len(args): 0}
        in_specs.append(pl.BlockSpec(memory_space=pl.ANY))
        args.append(into)
    return pl.pallas_call(
        body, name=name, grid=(M // tm, N // tn, nk), in_specs=in_specs, out_specs=o_spec,
        out_shape=jax.ShapeDtypeStruct(o_shape, out_dtype), scratch_shapes=[pltpu.VMEM((tm, tn), F32)],
        input_output_aliases=aliases, compiler_params=_cparams(("parallel", "parallel", "arbitrary")),
    )(*args)


def _rowwise(fn, rows, pars, outs, accs, *, name, tm=512):
    first = rows[0][0] if isinstance(rows[0], tuple) else rows[0]
    S = first.shape[-2]
    tm = _tile(S, tm, 16)
    n_r, n_p, n_o = len(rows), len(pars), len(outs)
    outs = [o if len(o) == 5 else (o[0], o[1], o[0], 0, None) for o in outs]
    intos = [(k, o[4]) for k, o in enumerate(outs) if o[4] is not None]
    n_i = len(intos)

    def body(*refs):
        i = pl.program_id(0)
        vals = [r[...] for r in refs[:n_r + n_p]]
        res = fn(*vals)
        if not isinstance(res, (tuple, list)):
            res = (res,)
        o_refs = refs[n_r + n_p + n_i:n_r + n_p + n_i + n_o]
        a_refs = refs[n_r + n_p + n_i + n_o:]
        for r, v in zip(o_refs, res[:n_o]):
            r[...] = v.astype(r.dtype)
        if a_refs:
            @pl.when(i == 0)
            def _():
                for r in a_refs:
                    r[...] = jnp.zeros_like(r)

            for r, v in zip(a_refs, res[n_o:]):
                r[...] += v

    in_specs, args = [], []
    for r in rows:
        if isinstance(r, tuple):
            arr, w, cb = r
            in_specs.append(pl.BlockSpec((tm, w), lambda i, cb=cb: (i, cb)))
        elif r.ndim == 3:
            arr = r
            in_specs.append(pl.BlockSpec((arr.shape[0], tm, arr.shape[2]), lambda i: (0, i, 0)))
        else:
            arr = r
            in_specs.append(pl.BlockSpec((tm, arr.shape[1]), lambda i: (i, 0)))
        args.append(arr)
    for p_ in pars:
        in_specs.append(pl.BlockSpec(p_.shape, lambda i: (0, 0)))
        args.append(p_)
    aliases = {}
    for k, arr in intos:
        aliases[len(args)] = k
        in_specs.append(pl.BlockSpec(memory_space=pl.ANY))
        args.append(arr)
    out_shape = [jax.ShapeDtypeStruct((S, o[2]), o[1]) for o in outs] + [jax.ShapeDtypeStruct(a, F32) for a in accs]
    out_specs = [pl.BlockSpec((tm, o[0]), lambda i, cb=o[3]: (i, cb)) for o in outs] + [pl.BlockSpec(a, lambda i: (0, 0)) for a in accs]
    return pl.pallas_call(
        body, name=name, grid=(S // tm,), in_specs=in_specs, out_specs=out_specs, out_shape=out_shape,
        input_output_aliases=aliases, compiler_params=_cparams(("arbitrary",) if accs else ("parallel",)),
    )(*args)


def _norm(h, eps):
    mu = jnp.mean(h, -1, keepdims=True)
    d = h - mu
    rstd = lax.rsqrt(jnp.mean(d * d, -1, keepdims=True) + eps)
    return d * rstd, rstd


def _norm_bwd(dxh, xh, rstd):
    return rstd * (dxh - jnp.mean(dxh, -1, keepdims=True) - xh * jnp.mean(dxh * xh, -1, keepdims=True))


def _sig(x):
    return 1.0 / (1.0 + jnp.exp(-x))


_GELU_C = math.sqrt(2.0 / math.pi)


def _gelu(x):
    t = jnp.tanh(_GELU_C * (x + 0.044715 * x * x * x))
    return 0.5 * x * (1.0 + t), t


def _gelu_grad(x, t):
    return 0.5 * (1.0 + t) + 0.5 * x * (1.0 - t * t) * _GELU_C * (1.0 + 3 * 0.044715 * x * x)


def _f_ln1(x, mix, g, b):
    h = ALPHA * x + mix
    xh, _ = _norm(h, LN_EPS)
    y = xh * g + b
    return h, y, y


def _f_ln2(x, ffn, z, pp, g, b):
    h = ALPHA * x + ffn + _sig(z) * pp
    xh, _ = _norm(h, LN_EPS)
    y = xh * g + b
    return h, y, y


def _f_ln_bwd(*args):
    *dys, h, g = args
    dy = dys[0]
    for t in dys[1:]:
        dy = dy + t
    xh, rstd = _norm(h, LN_EPS)
    dh = _norm_bwd(dy * g, xh, rstd)
    return dh, dh, jnp.sum(dy * xh, 0, keepdims=True), jnp.sum(dy, 0, keepdims=True)


def _f_sum(*ts):
    r = ts[0]
    for t in ts[1:]:
        r = r + t
    return r


def _f_loss(y, t):
    e = y - t
    return e * (1.0 / D), jnp.sum(e * e, 0, keepdims=True) * (0.5 / D)


def _head_col(c, h):
    lane = lax.broadcasted_iota(jnp.int32, c.shape, 1)
    return jnp.sum(jnp.where(lane == h, c, 0.0), -1, keepdims=True)


def _f_combine(o0, o1, o2, l0, l1, l2):
    lane = lax.broadcasted_iota(jnp.int32, l0.shape, 1)
    parts, lse = [], jnp.zeros(l0.shape, F32)
    for h in range(NH):
        a0, a1, a2 = _head_col(l0, h), _head_col(l1, h), _head_col(l2, h)
        m = jnp.maximum(jnp.maximum(a0, a1), a2)
        e0, e1, e2 = jnp.exp(a0 - m), jnp.exp(a1 - m), jnp.exp(a2 - m)
        den = e0 + e1 + e2
        parts.append((e0 * o0[h].astype(F32) + e1 * o1[h].astype(F32) + e2 * o2[h].astype(F32)) / den)
        lse = jnp.where(lane == h, m + jnp.log(den), lse)
    return jnp.concatenate(parts, axis=1), lse


def _f_delta(da, a):
    lane = lax.broadcasted_iota(jnp.int32, (da.shape[0], HD), 1)
    out = jnp.zeros((da.shape[0], HD), F32)
    for h in range(NH):
        sl = slice(h * HD, (h + 1) * HD)
        s = jnp.sum(da[:, sl].astype(F32) * a[:, sl].astype(F32), -1, keepdims=True)
        out = jnp.where(lane == h, s, out)
    return out


def _f_gate(ap, rp, ga, gr):
    return _sig(ga.astype(F32)) * ap.astype(F32) + _sig(gr.astype(F32)) * rp.astype(F32)


def _f_gate_bwd(dm, ap, rp, ga, gr):
    dm = dm.astype(F32)
    sa, sr = _sig(ga.astype(F32)), _sig(gr.astype(F32))
    dga, dgr = dm * ap.astype(F32) * sa * (1.0 - sa), dm * rp.astype(F32) * sr * (1.0 - sr)
    return dm * sa, dm * sr, jnp.concatenate([dga, dgr], axis=1)


def _f_gn(y, rg, g, b):
    y, rg = y.astype(F32), rg.astype(F32)
    parts = []
    for h in range(RH):
        sl = slice(h * RDV, (h + 1) * RDV)
        xh, _ = _norm(y[:, sl], GN_EPS)
        parts.append(xh * g[:, sl] + b[:, sl])
    return rg * _sig(rg) * jnp.concatenate(parts, axis=1)


def _f_gn_bwd(dr, y, rg, g, b):
    dr, y, rg = dr.astype(F32), y.astype(F32), rg.astype(F32)
    s = _sig(rg)
    d_out = dr * rg * s
    dys, outs, xhs = [], [], []
    for h in range(RH):
        sl = slice(h * RDV, (h + 1) * RDV)
        xh, rstd = _norm(y[:, sl], GN_EPS)
        xhs.append(xh)
        outs.append(xh * g[:, sl] + b[:, sl])
        dys.append(_norm_bwd(d_out[:, sl] * g[:, sl], xh, rstd))
    xh, out = jnp.concatenate(xhs, axis=1), jnp.concatenate(outs, axis=1)
    d_rg = dr * out * s * (1.0 + rg * (1.0 - s))
    return jnp.concatenate(dys, axis=1), d_rg, jnp.sum(d_out * xh, 0, keepdims=True), jnp.sum(d_out, 0, keepdims=True)


def _f_ple_bwd(dh, z, pp):
    s = _sig(z)
    return dh * s, dh * pp * s * (1.0 - s)


QKV = 3 * HD


def _to_tokens(t, d):
    if d == 1:
        return t
    *lead, S, C = t.shape
    n = len(lead)
    perm = tuple(range(n)) + (n + 1, n, n + 2)
    return t.reshape(*lead, d, S // d, C).transpose(perm).reshape(*lead, S, C)


def _to_residues(t, d):
    if d == 1:
        return t
    S, C = t.shape
    return t.reshape(S // d, d, C).transpose(1, 0, 2).reshape(S, C)


def _to_head_residues(t, d):
    S = t.shape[0]
    return t.reshape(S // d, d, NH, HD).transpose(2, 1, 0, 3).reshape(NH, S, HD)


def _w_qkv_specs(g):
    return [pl.BlockSpec((D, D), lambda *i, t=t: (3 * g + t, 0)) for t in range(3)]


def _qkv_fwd(xv, win, g, dil, name):
    Sd = xv.shape[0]
    S = Sd * dil
    tm = min(512, Sd)
    nma = Sd // tm

    def body(a_ref, wq_ref, wk_ref, wv_ref, o_ref):
        a = a_ref[...]
        q, k, v = (_dot(a, w_ref[...], 1, 1).astype(BF16) for w_ref in (wq_ref, wk_ref, wv_ref))
        for h in range(NH):
            sl = slice(h * HD, (h + 1) * HD)
            o_ref[h] = jnp.concatenate([q[:, sl], k[:, sl], v[:, sl]], axis=1)

    return pl.pallas_call(
        body, name=name, grid=(S // tm,),
        in_specs=[pl.BlockSpec((tm, D), lambda i: (i % nma, i // nma))] + _w_qkv_specs(g),
        out_specs=pl.BlockSpec((NH, tm, QKV), lambda i: (0, i, 0)), out_shape=jax.ShapeDtypeStruct((NH, S, QKV), BF16),
        compiler_params=_cparams(("parallel",)),
    )(xv, win, win, win)


def _qkv_dx(dqkv, win, g, dil, name, out_dtype, add=None):
    S = dqkv.shape[1]
    Sd = S // dil
    tm = min(512, Sd)
    nmo = Sd // tm

    def body(*refs):
        a_ref, wq_ref, wk_ref, wv_ref = refs[:4]
        o_ref = refs[-1]
        acc = None
        for h in range(NH):
            sl = slice(h * HD, (h + 1) * HD)
            w = jnp.concatenate([wq_ref[sl, :], wk_ref[sl, :], wv_ref[sl, :]], axis=0)
            part = _dot(a_ref[h], w, 1, 0)
            acc = part if acc is None else acc + part
        if add is not None:
            acc = acc + refs[4][...]
        o_ref[...] = acc.astype(out_dtype)

    o_spec = pl.BlockSpec((tm, D), lambda i: (i % nmo, i // nmo))
    in_specs = [pl.BlockSpec((NH, tm, QKV), lambda i: (0, i, 0))] + _w_qkv_specs(g)
    args = [dqkv, win, win, win]
    if add is not None:
        assert dil == 1
        in_specs.append(o_spec)
        args.append(add)
    return pl.pallas_call(
        body, name=name, grid=(S // tm,), in_specs=in_specs, out_specs=o_spec,
        out_shape=jax.ShapeDtypeStruct((Sd, dil * D), out_dtype), compiler_params=_cparams(("parallel",)),
    )(*args)


GW_IN_BLOCKS = (N_IN // D, NH, HD, D)


def _qkv_dw(dqkv, xv, g, dil, name, into=None):
    S = dqkv.shape[1]
    Sd = S // dil
    tk = min(1024, Sd)
    nkb, nk = Sd // tk, S // tk
    hh = NH // 2

    def body(*refs):
        a_ref, b_ref = refs[:2]
        o_ref, acc_ref = refs[-2:]
        k = pl.program_id(1)

        @pl.when(k == 0)
        def _():
            acc_ref[...] = jnp.zeros_like(acc_ref)

        b = b_ref[...]
        for h in range(hh):
            acc_ref[h * QKV:(h + 1) * QKV, :] += _dot(a_ref[h], b, 0, 0)

        @pl.when(k == nk - 1)
        def _():
            for h in range(hh):
                for t in range(3):
                    o_ref[t, h] = acc_ref[h * QKV + t * HD:h * QKV + (t + 1) * HD, :]

    in_specs = [pl.BlockSpec((hh, tk, QKV), lambda j, k: (j, k, 0)), pl.BlockSpec((tk, D), lambda j, k: (k % nkb, k // nkb))]
    args, aliases = [dqkv, xv], {}
    if into is not None:
        aliases = {2: 0}
        in_specs.append(pl.BlockSpec(memory_space=pl.ANY))
        args.append(into)
    return pl.pallas_call(
        body, name=name, grid=(2, nk), in_specs=in_specs,
        out_specs=pl.BlockSpec((3, hh, HD, D), lambda j, k: (g, j, 0, 0)), out_shape=jax.ShapeDtypeStruct(GW_IN_BLOCKS, F32),
        input_output_aliases=aliases, scratch_shapes=[pltpu.VMEM((hh * QKV, D), F32)],
        compiler_params=_cparams(("parallel", "arbitrary")),
    )(*args)


def _band(nb, first_valid, last_valid=None):
    b = lax.broadcasted_iota(jnp.int32, (nb, SPAN, SPAN), 0)
    row = lax.broadcasted_iota(jnp.int32, (nb, SPAN, SPAN), 1)
    col = lax.broadcasted_iota(jnp.int32, (nb, SPAN, SPAN), 2)
    off = jnp.where(b == 0, jnp.where(first_valid, 0, 2 * SPAN), 0)
    if last_valid is not None:
        off = off + jnp.where(b == nb - 1, jnp.where(last_valid, 0, 2 * SPAN), 0)
    return col <= row, col >= row + off


def _attn_tiles(S, dil):
    Sd = S // dil
    T = min(1024, Sd)
    return Sd, T, T // SPAN, Sd // T


def _attn_fwd(qkv, dil, name):
    S = qkv.shape[1]
    Sd, T, nsub, nib = _attn_tiles(S, dil)
    scale = HD ** -0.5

    def body(c_ref, p_ref, o_ref, l_ref):
        ib, h = pl.program_id(1), pl.program_id(2)
        blk, hal = c_ref[...], p_ref[...]
        q, k, v = blk[:, :HD], blk[:, HD:2 * HD], blk[:, 2 * HD:]
        if nsub > 1:
            kp = jnp.concatenate([hal[:, HD:2 * HD], k[:T - SPAN]], axis=0)
            vp = jnp.concatenate([hal[:, 2 * HD:], v[:T - SPAN]], axis=0)
        else:
            kp, vp = hal[:, HD:2 * HD], hal[:, 2 * HD:]
        q3, k3, v3, kp3, vp3 = (t.reshape(nsub, SPAN, HD) for t in (q, k, v, kp, vp))
        m_cur, m_prev = _band(nsub, ib > 0)
        sc = jnp.where(m_cur, _bdot(q3, k3, 2, 2) * scale, NEG)
        sp = jnp.where(m_prev, _bdot(q3, kp3, 2, 2) * scale, NEG)
        m = jnp.maximum(jnp.max(sc, -1, keepdims=True), jnp.max(sp, -1, keepdims=True))
        pc, pp = jnp.exp(sc - m), jnp.exp(sp - m)
        den = jnp.sum(pc, -1, keepdims=True) + jnp.sum(pp, -1, keepdims=True)
        o = (_bdot(pc.astype(BF16), v3, 2, 1) + _bdot(pp.astype(BF16), vp3, 2, 1)) / den
        o_ref[...] = o.reshape(T, HD).astype(BF16)
        lse = (m + jnp.log(den)).reshape(T, 1)
        lane = lax.broadcasted_iota(jnp.int32, (T, HD), 1)

        @pl.when(h == 0)
        def _():
            l_ref[...] = jnp.zeros_like(l_ref)

        l_ref[...] = jnp.where(lane == h, lse, l_ref[...])

    cur = pl.BlockSpec((None, T, QKV), lambda r, ib, h: (h, r * nib + ib, 0))
    prev = pl.BlockSpec((None, SPAN, QKV), lambda r, ib, h: (h, r * (Sd // SPAN) + jnp.maximum(ib * nsub - 1, 0), 0))
    return pl.pallas_call(
        body, name=name, grid=(dil, nib, NH), in_specs=[cur, prev],
        out_specs=[pl.BlockSpec((None, T, HD), lambda r, ib, h: (h, r * nib + ib, 0)),
                   pl.BlockSpec((T, HD), lambda r, ib, h: (r * nib + ib, 0))],
        out_shape=[jax.ShapeDtypeStruct((NH, S, HD), BF16), jax.ShapeDtypeStruct((S, HD), F32)],
        compiler_params=_cparams(("parallel", "parallel", "arbitrary")),
    )(qkv, qkv)


def _attn_bwd(qkv, d_attn, lse, delta, dil, name):
    S = qkv.shape[1]
    Sd, T, nsub, nib = _attn_tiles(S, dil)
    scale = HD ** -0.5
    ne = nsub + 1

    def body(c_ref, p_ref, n_ref, do_ref, don_ref, l_ref, ln_ref, dl_ref, dln_ref, o_ref):
        ib, h = pl.program_id(1), pl.program_id(2)
        blk, hal, nxt = c_ref[...], p_ref[...], n_ref[...]
        q, k, v = blk[:, :HD], blk[:, HD:2 * HD], blk[:, 2 * HD:]
        do = do_ref[...]
        l, dl = _head_col(l_ref[...], h), _head_col(dl_ref[...], h)
        qe = jnp.concatenate([q, nxt[:, :HD]], axis=0).reshape(ne, SPAN, HD)
        doe = jnp.concatenate([do, don_ref[...]], axis=0).reshape(ne, SPAN, HD)
        le = jnp.concatenate([l, _head_col(ln_ref[...], h)], axis=0).reshape(ne, SPAN, 1)
        dle = jnp.concatenate([dl, _head_col(dln_ref[...], h)], axis=0).reshape(ne, SPAN, 1)
        kpe = jnp.concatenate([hal[:, HD:2 * HD], k], axis=0).reshape(ne, SPAN, HD)
        vpe = jnp.concatenate([hal[:, 2 * HD:], v], axis=0).reshape(ne, SPAN, HD)
        _, m_prev = _band(ne, ib > 0, ib < nib - 1)
        p = jnp.where(m_prev, jnp.exp(_bdot(qe, kpe, 2, 2) * scale - le), 0.0)
        ds = (p * (_bdot(doe, vpe, 2, 2) - dle)).astype(BF16)
        dq = _bdot(ds, kpe, 2, 1)[:nsub]
        dk = _bdot(ds, qe, 1, 1)[1:]
        dv = _bdot(p.astype(BF16), doe, 1, 1)[1:]
        q3, k3, v3, do3 = (t.reshape(nsub, SPAN, HD) for t in (q, k, v, do))
        l3, dl3 = l.reshape(nsub, SPAN, 1), dl.reshape(nsub, SPAN, 1)
        m_cur, _ = _band(nsub, True)
        p = jnp.where(m_cur, jnp.exp(_bdot(q3, k3, 2, 2) * scale - l3), 0.0)
        ds = (p * (_bdot(do3, v3, 2, 2) - dl3)).astype(BF16)
        dq = (dq + _bdot(ds, k3, 2, 1)) * scale
        dk = (dk + _bdot(ds, q3, 1, 1)) * scale
        dv = dv + _bdot(p.astype(BF16), do3, 1, 1)
        o_ref[...] = jnp.concatenate([t.reshape(T, HD) for t in (dq, dk, dv)], axis=1).astype(BF16)

    nb = Sd // SPAN
    row = lambda r, ib: r * nib + ib
    prow = lambda r, ib: r * nb + jnp.maximum(ib * nsub - 1, 0)
    nrow = lambda r, ib: r * nb + jnp.minimum((ib + 1) * nsub, nb - 1)
    cur3 = pl.BlockSpec((None, T, QKV), lambda r, ib, h: (h, row(r, ib), 0))
    prev3 = pl.BlockSpec((None, SPAN, QKV), lambda r, ib, h: (h, prow(r, ib), 0))
    next3 = pl.BlockSpec((None, SPAN, QKV), lambda r, ib, h: (h, nrow(r, ib), 0))
    cur1 = pl.BlockSpec((None, T, HD), lambda r, ib, h: (h, row(r, ib), 0))
    next1 = pl.BlockSpec((None, SPAN, HD), lambda r, ib, h: (h, nrow(r, ib), 0))
    curc = pl.BlockSpec((T, HD), lambda r, ib, h: (row(r, ib), 0))
    nextc = pl.BlockSpec((SPAN, HD), lambda r, ib, h: (nrow(r, ib), 0))
    return pl.pallas_call(
        body, name=name, grid=(dil, nib, NH),
        in_specs=[cur3, prev3, next3, cur1, next1, curc, nextc, curc, nextc], out_specs=cur3,
        out_shape=jax.ShapeDtypeStruct((NH, S, QKV), BF16),
        compiler_params=_cparams(("parallel", "parallel", "parallel")),
    )(qkv, qkv, qkv, d_attn, d_attn, lse, lse, delta, delta)


def _ret_consts():
    lg = np.log1p(-np.exp2(-5.0 - np.arange(RH, dtype=np.float64)))
    idx = np.arange(CH, dtype=np.float64)
    rel = idx[:, None] - idx[None, :]
    intra = np.where(rel >= 0, np.exp(lg[:, None, None] * np.maximum(rel, 0.0)), 0.0)
    qd = np.exp(lg[:, None] * (idx + 1.0))
    kd = np.exp(lg[:, None] * (CH - 1.0 - idx))
    cd = np.exp(lg * CH)
    wide = lambda t: np.broadcast_to(t[:, :, None], (RH, t.shape[1], RDV))
    return (jnp.asarray(intra, F32), jnp.asarray(wide(qd), F32), jnp.asarray(wide(kd), F32),
            jnp.asarray(np.broadcast_to(cd[:, None, None], (RH, 1, RDV)), F32))


def _rot(t, c, s):
    t1, t2 = t[:, :RDK // 2], t[:, RDK // 2:]
    return jnp.concatenate([t1 * c - t2 * s, t1 * s + t2 * c], axis=1)


def _unrot(d, c, s):
    d1, d2 = d[:, :RDK // 2], d[:, RDK // 2:]
    return jnp.concatenate([d1 * c + d2 * s, d2 * c - d1 * s], axis=1)


RCH = 2


def _ret_specs(nmap):
    rows = RCH * CH
    q = pl.BlockSpec((rows, RH * RDK), lambda n: (nmap(n), OFF_RQ // (RH * RDK)))
    k = pl.BlockSpec((rows, RH * RDK), lambda n: (nmap(n), OFF_RK // (RH * RDK)))
    v = pl.BlockSpec((rows, RH * RDV), lambda n: (nmap(n), OFF_RV // (RH * RDV)))
    cs = pl.BlockSpec((rows, RDK // 2), lambda n: (nmap(n), 0))
    dmat = pl.BlockSpec((RH, CH, CH), lambda n: (0, 0, 0))
    dvec = pl.BlockSpec((RH, CH, RDV), lambda n: (0, 0, 0))
    cdv = pl.BlockSpec((RH, 1, RDV), lambda n: (0, 0, 0))
    state = pl.BlockSpec((RH, RCH, RDK, RDV), lambda n: (0, nmap(n), 0, 0))
    out = pl.BlockSpec((rows, RH * RDV), lambda n: (nmap(n), 0))
    return [q, k, v, cs, cs, dmat, dvec, dvec, cdv], state, out


def _ret_fwd(proj, cos, sin, consts):
    S = proj.shape[0]
    nc = S // CH

    def body(q_ref, k_ref, v_ref, c_ref, s_ref, d_ref, qd_ref, kd_ref, cd_ref, o_ref, st_ref, state):
        @pl.when(pl.program_id(0) == 0)
        def _():
            state[...] = jnp.zeros_like(state)

        for ci in range(RCH):
            rows = slice(ci * CH, (ci + 1) * CH)
            c, s = c_ref[rows, :], s_ref[rows, :]
            for h in range(RH):
                qk, vv = slice(h * RDK, (h + 1) * RDK), slice(h * RDV, (h + 1) * RDV)
                qb = _rot(q_ref[rows, qk].astype(F32), c, s).astype(BF16)
                kb = (_rot(k_ref[rows, qk].astype(F32), c, s) * (RDK ** -0.5)).astype(BF16)
                vb = v_ref[rows, vv]
                sb = state[h].astype(BF16)
                st_ref[h, ci] = sb
                a = (_dot(qb, kb, 1, 1) * d_ref[h]).astype(BF16)
                o_ref[rows, vv] = (_dot(a, vb, 1, 0) + _dot(qb, sb, 1, 0) * qd_ref[h]).astype(BF16)
                vk = (vb.astype(F32) * kd_ref[h]).astype(BF16)
                state[h] = cd_ref[h] * state[h] + _dot(kb, vk, 0, 0)

    ins, state_spec, out_spec = _ret_specs(lambda n: n)
    return pl.pallas_call(
        body, name="ret_fwd", grid=(nc // RCH,), in_specs=ins, out_specs=[out_spec, state_spec],
        out_shape=[jax.ShapeDtypeStruct((S, RH * RDV), BF16), jax.ShapeDtypeStruct((RH, nc, RDK, RDV), BF16)],
        scratch_shapes=[pltpu.VMEM((RH, RDK, RDV), F32)],
        compiler_params=_cparams(("arbitrary",)),
    )(proj, proj, proj, cos, sin, *consts)


def _ret_bwd(proj, cos, sin, consts, states, d_ret, d_rest):
    S = proj.shape[0]
    nc = S // CH

    def body(q_ref, k_ref, v_ref, c_ref, s_ref, d_ref, qd_ref, kd_ref, cd_ref, st_ref, do_ref, _, o_ref, dstate):
        @pl.when(pl.program_id(0) == 0)
        def _():
            dstate[...] = jnp.zeros_like(dstate)

        for ci in reversed(range(RCH)):
            rows = slice(ci * CH, (ci + 1) * CH)
            c, s = c_ref[rows, :], s_ref[rows, :]
            for h in range(RH):
                qk, vv = slice(h * RDK, (h + 1) * RDK), slice(h * RDV, (h + 1) * RDV)
                qb = _rot(q_ref[rows, qk].astype(F32), c, s).astype(BF16)
                kb = (_rot(k_ref[rows, qk].astype(F32), c, s) * (RDK ** -0.5)).astype(BF16)
                vb, sb, do = v_ref[rows, vv], st_ref[h, ci], do_ref[rows, vv]
                dmat, qd, kd = d_ref[h], qd_ref[h], kd_ref[h]
                a = (_dot(qb, kb, 1, 1) * dmat).astype(BF16)
                doq = (do.astype(F32) * qd).astype(BF16)
                dsb = dstate[h].astype(BF16)
                vk = (vb.astype(F32) * kd).astype(BF16)
                o_ref[rows, OFF_RV + h * RDV:OFF_RV + (h + 1) * RDV] = (_dot(a, do, 0, 0) + _dot(kb, dsb, 1, 0) * kd).astype(BF16)
                da = (_dot(do, vb, 1, 1) * dmat).astype(BF16)
                dq = _dot(da, kb, 1, 0) + _dot(doq, sb, 1, 1)
                dk = (_dot(da, qb, 0, 0) + _dot(vk, dsb, 1, 1)) * (RDK ** -0.5)
                o_ref[rows, OFF_RQ + h * RDK:OFF_RQ + (h + 1) * RDK] = _unrot(dq, c, s).astype(BF16)
                o_ref[rows, OFF_RK + h * RDK:OFF_RK + (h + 1) * RDK] = _unrot(dk, c, s).astype(BF16)
                dstate[h] = cd_ref[h] * dstate[h] + _dot(qb, doq, 0, 0)

    nsteps = nc // RCH
    rev = lambda n: nsteps - 1 - n
    ins, state_spec, out_spec = _ret_specs(rev)
    return pl.pallas_call(
        body, name="ret_bwd", grid=(nsteps,), in_specs=ins + [state_spec, out_spec, pl.BlockSpec(memory_space=pl.ANY)],
        out_specs=pl.BlockSpec((RCH * CH, OFF_RG), lambda n: (rev(n), 0)),
        out_shape=jax.ShapeDtypeStruct(d_rest.shape, BF16), input_output_aliases={11: 0},
        scratch_shapes=[pltpu.VMEM((RH, RDK, RDV), F32)],
        compiler_params=_cparams(("arbitrary",)),
    )(proj, proj, proj, cos, sin, *consts, states, d_ret, d_rest)


CW = 256
HALO = 16


def _shift_down(v, halo, k):
    rolled = pltpu.roll(v, k, 0)
    hr = pltpu.roll(halo, k, 0)[0:8]
    row = lax.broadcasted_iota(jnp.int32, hr.shape, 0)
    return jnp.concatenate([jnp.where(row < k, hr, rolled[0:8]), rolled[8:]], axis=0)


def _shift_up(v, halo, k):
    T = v.shape[0]
    rolled = pltpu.roll(v, T - k, 0)
    hr = pltpu.roll(halo, 8 - k, 0)[0:8]
    row = lax.broadcasted_iota(jnp.int32, hr.shape, 0)
    return jnp.concatenate([rolled[:T - 8], jnp.where(row >= 8 - k, hr, rolled[T - 8:])], axis=0)


def _conv_taps(h_ref, hp_ref, first):
    h = h_ref[...].astype(F32)
    hp = hp_ref[...].astype(F32) * jnp.where(first, 0.0, 1.0)
    return _shift_down(h, hp, 2), _shift_down(h, hp, 1), h


def _conv_specs(S, T):
    nj = DFF // CW
    cur = pl.BlockSpec((T, CW), lambda j, i: (i, j))
    prev = pl.BlockSpec((HALO, CW), lambda j, i: (jnp.maximum(i * (T // HALO) - 1, 0), j))
    nxt = pl.BlockSpec((HALO, CW), lambda j, i: (jnp.minimum((i + 1) * (T // HALO), S // HALO - 1), j))
    w = pl.BlockSpec((3, CW), lambda j, i: (0, j))
    b = pl.BlockSpec((1, CW), lambda j, i: (0, j))
    return nj, cur, prev, nxt, w, b


def _conv_fwd(hg, hu, wg, wu, bg, bu):
    S = hg.shape[0]
    T = min(1024, S)
    nj, cur, prev, _, w, b = _conv_specs(S, T)

    def body(hg_ref, hu_ref, hgp_ref, hup_ref, wg_ref, wu_ref, bg_ref, bu_ref, o_ref):
        first = pl.program_id(1) == 0
        g2, g1, g0 = _conv_taps(hg_ref, hgp_ref, first)
        u2, u1, u0 = _conv_taps(hu_ref, hup_ref, first)
        cg = wg_ref[0:1, :] * g2 + wg_ref[1:2, :] * g1 + wg_ref[2:3, :] * g0 + bg_ref[...]
        cu = wu_ref[0:1, :] * u2 + wu_ref[1:2, :] * u1 + wu_ref[2:3, :] * u0 + bu_ref[...]
        o_ref[...] = (_gelu(cg)[0] * cu).astype(BF16)

    return pl.pallas_call(
        body, name="conv_fwd", grid=(nj, S // T), in_specs=[cur, cur, prev, prev, w, w, b, b], out_specs=cur,
        out_shape=jax.ShapeDtypeStruct((S, DFF), BF16), compiler_params=_cparams(("parallel", "parallel")),
    )(hg, hu, hg, hu, wg, wu, bg, bu)


def _conv_bwd_pre(d_act, hg, hu, wg, wu, bg, bu):
    S = hg.shape[0]
    T = min(1024, S)
    nj, cur, prev, _, w, b = _conv_specs(S, T)

    def body(da_ref, hg_ref, hu_ref, hgp_ref, hup_ref, wg_ref, wu_ref, bg_ref, bu_ref,
             dcg_ref, dcu_ref, gwg_ref, gwu_ref, gbg_ref, gbu_ref):
        first = pl.program_id(1) == 0
        g2, g1, g0 = _conv_taps(hg_ref, hgp_ref, first)
        u2, u1, u0 = _conv_taps(hu_ref, hup_ref, first)
        cg = wg_ref[0:1, :] * g2 + wg_ref[1:2, :] * g1 + wg_ref[2:3, :] * g0 + bg_ref[...]
        cu = wu_ref[0:1, :] * u2 + wu_ref[1:2, :] * u1 + wu_ref[2:3, :] * u0 + bu_ref[...]
        da = da_ref[...].astype(F32)
        gl, t = _gelu(cg)
        dcg = da * cu * _gelu_grad(cg, t)
        dcu = da * gl
        dcg_ref[...] = dcg.astype(BF16)
        dcu_ref[...] = dcu.astype(BF16)

        @pl.when(first)
        def _():
            for r in (gwg_ref, gwu_ref, gbg_ref, gbu_ref):
                r[...] = jnp.zeros_like(r)

        for r, d, taps in ((gwg_ref, dcg, (g2, g1, g0)), (gwu_ref, dcu, (u2, u1, u0))):
            for j in range(3):
                r[j:j + 1, :] += jnp.sum(d * taps[j], 0, keepdims=True)
        gbg_ref[...] += jnp.sum(dcg, 0, keepdims=True)
        gbu_ref[...] += jnp.sum(dcu, 0, keepdims=True)

    return pl.pallas_call(
        body, name="conv_bwd_pre", grid=(nj, S // T), in_specs=[cur, cur, cur, prev, prev, w, w, b, b],
        out_specs=[cur, cur, w, w, b, b],
        out_shape=[jax.ShapeDtypeStruct((S, DFF), BF16)] * 2 + [jax.ShapeDtypeStruct((3, DFF), F32)] * 2
        + [jax.ShapeDtypeStruct((1, DFF), F32)] * 2,
        compiler_params=_cparams(("parallel", "arbitrary")),
    )(d_act, hg, hu, hg, hu, wg, wu, bg, bu)


def _conv_bwd_in(dc, w, name):
    S = dc.shape[0]
    T = min(1024, S)
    nj, cur, _, nxt, wspec, _ = _conv_specs(S, T)
    nt = S // T

    def body(dc_ref, dn_ref, w_ref, o_ref):
        d = dc_ref[...].astype(F32)
        dn = dn_ref[...].astype(F32) * jnp.where(pl.program_id(1) == nt - 1, 0.0, 1.0)
        o_ref[...] = (w_ref[2:3, :] * d + w_ref[1:2, :] * _shift_up(d, dn, 1) + w_ref[0:1, :] * _shift_up(d, dn, 2)).astype(BF16)

    return pl.pallas_call(
        body, name=name, grid=(nj, nt), in_specs=[cur, nxt, wspec], out_specs=cur,
        out_shape=jax.ShapeDtypeStruct((S, DFF), BF16), compiler_params=_cparams(("parallel", "parallel")),
    )(dc, dc, w)


def _adam_math(g, w, m, v):
    m = B1 * m + (1.0 - B1) * g
    v = B2 * v + (1.0 - B2) * (g * g)
    m_hat = m / (1.0 - B1 ** STEP)
    v_hat = v / (1.0 - B2 ** STEP)
    return -LR * (m_hat / (jnp.sqrt(v_hat) + EPS) + WD * w), m, v


def _reduce_tail(chip32, far, chip, name, wmv=None):
    L = len(chip32)
    _, R, C = chip32[0].shape
    tr = _tile(R, 256, 16)
    nr = R // tr

    def body(chip_ref, *refs):
        own_refs, far_refs, rest = refs[:L], refs[L:2 * L], refs[2 * L:]
        outs = rest[3:] if wmv else rest
        for ll in range(L):
            @pl.when(pl.program_id(0) == ll)
            def _(ll=ll):
                g = own_refs[ll][...]
                for s in range(3):
                    g = g + far_refs[ll][s].astype(F32)
                outs[0][...] = g
                if wmv:
                    outs[1][...], outs[2][...], outs[3][...] = _adam_math(g, rest[0][...], rest[1][...], rest[2][...])

    def rows(ll):
        return lambda l, i: jnp.where(l == ll, i, jnp.where(l < ll, 0, nr - 1))

    blk = pl.BlockSpec((None, tr, C), lambda l, i, ch: (l, i, 0))
    in_specs = [pl.BlockSpec((None, tr, C), lambda l, i, ch, f=rows(ll): (ch[0], f(l, i), 0)) for ll in range(L)]
    in_specs += [pl.BlockSpec((3, tr, C), lambda l, i, ch, f=rows(ll): (0, f(l, i), 0)) for ll in range(L)]
    args = list(chip32) + list(far)
    n_out = 1
    if wmv:
        in_specs += [blk] * 3
        args += list(wmv)
        n_out = 4
    return pl.pallas_call(
        body, name=name,
        grid_spec=pltpu.PrefetchScalarGridSpec(num_scalar_prefetch=1, grid=(L, nr), in_specs=in_specs, out_specs=[blk] * n_out),
        out_shape=[jax.ShapeDtypeStruct((L, R, C), F32)] * n_out, compiler_params=_cparams(("arbitrary", "arbitrary")),
    )(chip, *args)


def _adamw(g, w, m, v, name):
    R, C = g.shape
    tr = _tile(R, 128, 8)

    def body(g_ref, w_ref, m_ref, v_ref, d_ref, nm_ref, nv_ref):
        d_ref[...], nm_ref[...], nv_ref[...] = _adam_math(g_ref[...], w_ref[...], m_ref[...], v_ref[...])

    blk = pl.BlockSpec((tr, C), lambda i: (i, 0))
    return pl.pallas_call(
        body, name=name, grid=(R // tr,), in_specs=[blk] * 4, out_specs=[blk] * 3,
        out_shape=[jax.ShapeDtypeStruct(g.shape, F32)] * 3, compiler_params=_cparams(("parallel",)),
    )(g, w, m, v)


def _pair_sum(x, recv, core, name):
    _, R, C = x.shape
    tr = _tile(R, 600, 16)

    def body(core_ref, x_ref, r_ref, o32_ref, o16_ref):
        s = x_ref[...] + r_ref[...]
        o32_ref[...] = s
        o16_ref[...] = s.astype(BF16)

    blk = pl.BlockSpec((None, tr, C), lambda q, i, c: (q, i, 0))
    mine = pl.BlockSpec((None, None, tr, C), lambda q, i, c: (q, c[0], i, 0))
    return pl.pallas_call(
        body, name=name,
        grid_spec=pltpu.PrefetchScalarGridSpec(num_scalar_prefetch=1, grid=(4, R // tr), in_specs=[mine, blk], out_specs=[blk, blk]),
        out_shape=[jax.ShapeDtypeStruct((4, R, C), F32), jax.ShapeDtypeStruct((4, R, C), BF16)],
        compiler_params=_cparams(("parallel", "parallel")),
    )(core, x.reshape(4, 2, R, C), recv)


def _sum_slots(x, name):
    def body(x_ref, o_ref):
        g = x_ref[0]
        for s in range(1, x.shape[0]):
            g = g + x_ref[s]
        o_ref[...] = g

    return pl.pallas_call(body, name=name, out_shape=jax.ShapeDtypeStruct(x.shape[1:], F32))(x)


MESH = pl.DeviceIdType.MESH
_HBM = pl.BlockSpec(memory_space=pltpu.HBM)


def _dma_sems(n):
    return pltpu.SemaphoreType.DMA((n,))


def _gather_many(xs, name):
    n = len(xs)

    def body(*refs):
        x_refs, out_refs = refs[:n], refs[n:2 * n]
        send_sems, recv_sems, local_sems = refs[2 * n:]
        ax, ay, ac = lax.axis_index("x"), lax.axis_index("y"), lax.axis_index("c")
        me, sibling = (ax, ay, ac), (ax, ay, 1 - ac)
        chips = [(1 - ax, ay), (ax, 1 - ay), (1 - ax, 1 - ay)]

        def copy(a, k, block, to, own=False):
            slot = out_refs[a].at[4 * block[0] + 2 * block[1] + block[2]]
            return pltpu.make_async_remote_copy(
                src_ref=x_refs[a] if own else slot, dst_ref=slot, send_sem=send_sems.at[7 * a + k],
                recv_sem=recv_sems.at[7 * a + k], device_id=to, device_id_type=MESH)

        mine = [pltpu.make_async_copy(x_refs[a], out_refs[a].at[4 * ax + 2 * ay + ac], local_sems.at[a]) for a in range(n)]
        first = [copy(a, 0, me, sibling, own=True) for a in range(n)]
        first += [copy(a, 1 + j, me, (*chip, ac), own=True) for j, chip in enumerate(chips) for a in range(n)]
        for cp in mine + first:
            cp.start()
        passed = []
        for j, chip in enumerate(chips):
            for a in range(n):
                copy(a, 1 + j, (*chip, ac), me).wait_recv()
                cp = copy(a, 4 + j, (*chip, ac), sibling)
                cp.start()
                passed.append(cp)
        for a in range(n):
            copy(a, 0, sibling, me).wait_recv()
            for j, chip in enumerate(chips):
                copy(a, 4 + j, (*chip, 1 - ac), me).wait_recv()
        for cp in first + passed:
            cp.wait_send()
        for cp in mine:
            cp.wait()

    return pl.pallas_call(
        body, name=name, out_shape=[jax.ShapeDtypeStruct((N_DEV,) + x.shape, x.dtype) for x in xs],
        in_specs=[_HBM] * n, out_specs=[_HBM] * n, scratch_shapes=[_dma_sems(7 * n), _dma_sems(7 * n), _dma_sems(n)],
    )(*xs)


_SEM = pl.BlockSpec(memory_space=pltpu.SEMAPHORE)
_EFFECT = pltpu.SideEffectType.DATAFLOW_SIDE_EFFECTING


def _peer(k):
    ax, ay, ac = lax.axis_index("x"), lax.axis_index("y"), lax.axis_index("c")
    px = 1 - ax if k & 4 else ax
    py = 1 - ay if k & 2 else ay
    pc = 1 - ac if k & 1 else ac
    return (px, py, pc), 4 * px + 2 * py + pc


def _gather_start(xs, lands, name):
    n = len(xs)

    def body(*refs):
        x_refs, land_refs, send_sems, recv_sems, token = refs[:n], refs[n:2 * n], refs[2 * n], refs[2 * n + 1], refs[4 * n + 2]
        _, me = _peer(0)
        for a in range(n):
            for k in range(1, N_DEV):
                peer, _ = _peer(k)
                pltpu.make_async_remote_copy(
                    src_ref=x_refs[a], dst_ref=land_refs[a].at[me], send_sem=send_sems.at[7 * a + k - 1],
                    recv_sem=recv_sems.at[7 * a + k - 1], device_id=peer, device_id_type=MESH).start()
        token[...] = jnp.zeros_like(token)

    hbm = lambda t: pltpu.HBM(t.shape, t.dtype)
    outs = pl.pallas_call(
        body, name=name,
        out_shape=(_dma_sems(7 * n), _dma_sems(7 * n), *[hbm(t) for t in xs], *[hbm(t) for t in lands],
                   jax.ShapeDtypeStruct((8, 128), F32)),
        in_specs=[_HBM] * (2 * n), out_specs=(_SEM, _SEM, *[_HBM] * (2 * n), pl.BlockSpec(memory_space=pltpu.VMEM)),
        input_output_aliases={a: 2 + a for a in range(2 * n)},
        compiler_params=pltpu.CompilerParams(has_side_effects=_EFFECT),
    )(*[pltpu.with_memory_space_constraint(t, pltpu.HBM) for t in list(xs) + list(lands)])
    return outs[0], outs[1], outs[2:2 + n], outs[2 + n:2 + 2 * n], outs[-1]


def _gather_wait(send_sems, recv_sems, xs, lands, after, name):
    n = len(xs)

    def body(*refs):
        x_refs, land_refs, send_sems, recv_sems = refs[:n], refs[n:2 * n], refs[2 * n], refs[2 * n + 1]
        for a in range(n):
            for k in range(1, N_DEV):
                peer, slot = _peer(k)
                copy = pltpu.make_async_remote_copy(
                    src_ref=x_refs[a], dst_ref=land_refs[a].at[slot], send_sem=send_sems.at[7 * a + k - 1],
                    recv_sem=recv_sems.at[7 * a + k - 1], device_id=peer, device_id_type=MESH)
                copy.wait_send()
                copy.wait_recv()

    hbm = lambda t: pltpu.HBM(t.shape, t.dtype)
    outs = pl.pallas_call(
        body, name=name, out_shape=(*[hbm(t) for t in xs], *[hbm(t) for t in lands]),
        in_specs=[_HBM] * (2 * n) + [_SEM, _SEM, pl.BlockSpec(memory_space=pl.ANY)], out_specs=[_HBM] * (2 * n),
        input_output_aliases={a: a for a in range(2 * n)}, compiler_params=pltpu.CompilerParams(has_side_effects=_EFFECT),
    )(*xs, *lands, send_sems, recv_sems, after)
    return outs[n:]


def _exchange_cores(xs, name):
    n = len(xs)

    def body(*refs):
        x_refs, out_refs = refs[:n], refs[n:2 * n]
        send_sems, recv_sems = refs[2 * n:]
        ax, ay, ac = lax.axis_index("x"), lax.axis_index("y"), lax.axis_index("c")
        copies = []
        for a in range(n):
            for q in range(4):
                copies.append(pltpu.make_async_remote_copy(
                    src_ref=x_refs[a].at[2 * q + 1 - ac], dst_ref=out_refs[a].at[q], send_sem=send_sems.at[4 * a + q],
                    recv_sem=recv_sems.at[4 * a + q], device_id=(ax, ay, 1 - ac), device_id_type=MESH))
        for cp in copies:
            cp.start()
        for cp in copies:
            cp.wait_recv()
        for cp in copies:
            cp.wait_send()

    return pl.pallas_call(
        body, name=name, out_shape=[jax.ShapeDtypeStruct((4,) + x.shape[1:], x.dtype) for x in xs],
        in_specs=[_HBM] * n, out_specs=[_HBM] * n, scratch_shapes=[_dma_sems(4 * n), _dma_sems(4 * n)],
    )(*xs)


def _exchange_chips(ps, name):
    n = len(ps)

    def body(*refs):
        p_refs, out_refs = refs[:n], refs[n:2 * n]
        send_sems, recv_sems = refs[2 * n:]
        ax, ay, ac = lax.axis_index("x"), lax.axis_index("y"), lax.axis_index("c")
        copies = []
        for a in range(n):
            for k in range(1, 4):
                px = 1 - ax if k & 2 else ax
                py = 1 - ay if k & 1 else ay
                copies.append(pltpu.make_async_remote_copy(
                    src_ref=p_refs[a].at[2 * px + py], dst_ref=out_refs[a].at[k - 1], send_sem=send_sems.at[3 * a + k - 1],
                    recv_sem=recv_sems.at[3 * a + k - 1], device_id=(px, py, ac), device_id_type=MESH))
        for cp in copies:
            cp.start()
        for cp in copies:
            cp.wait_recv()
        for cp in copies:
            cp.wait_send()

    return pl.pallas_call(
        body, name=name, out_shape=[jax.ShapeDtypeStruct((3,) + p.shape[1:], p.dtype) for p in ps],
        in_specs=[_HBM] * n, out_specs=[_HBM] * n, scratch_shapes=[_dma_sems(3 * n), _dma_sems(3 * n)],
    )(*ps)


def _x_view(xb, d):
    return xb if d == 1 else xb.reshape(xb.shape[0] // d, d * xb.shape[1])


def _layer_fwd(x, xb, p, w, cos, sin, rconsts):
    S = x.shape[0]
    proj = _mm(xb, w["win"], tb=True, b_rows=(N_ATT, N_REST), name="mm_proj", out_dtype=BF16)
    qkvs, ogs, lgs = [], [], []
    for g, dil in enumerate(DILATIONS):
        qkv = _qkv_fwd(_x_view(xb, dil), w["win"], g, dil, f"mm_qkv{g}")
        o, l = _attn_fwd(qkv, dil, f"attn_fwd_g{g}")
        qkvs.append(qkv)
        ogs.append(_to_tokens(o, dil))
        lgs.append(_to_tokens(l, dil))
    attn, lse = _rowwise(_f_combine, ogs + lgs, [], [(D, BF16), (HD, F32)], [], name="attn_combine")
    ret_raw, states = _ret_fwd(proj, cos, sin, rconsts)
    rg_win = (proj, RH * RDV, OFF_RG // (RH * RDV))
    ga_win, gr_win = (proj, D, OFF_GA // D), (proj, D, OFF_GR // D)
    (r,) = _rowwise(_f_gn, [ret_raw, rg_win], [w["ret_gn_g"], w["ret_gn_b"]], [(RH * RDV, BF16)], [], name="gn_fwd", tm=256)
    ap = _mm(attn, w["w_attn_proj"], name="mm_attn_proj", out_dtype=BF16)
    rp = _mm(r, w["w_ret_proj"], name="mm_ret_proj", out_dtype=BF16, tk=2048)
    (merged,) = _rowwise(_f_gate, [ap, rp, ga_win, gr_win], [], [(D, BF16)], [], name="gate_fwd")
    mix = _mm(merged, w["w_out"], name="mm_out")
    h1, x1, x1b = _rowwise(_f_ln1, [x, mix], [w["ln1_g"], w["ln1_b"]], [(D, F32), (D, F32), (D, BF16)], [], name="ln1_fwd")
    z = _mm(x1b, w["w_ple_gate"], name="mm_ple_gate")
    pp = _mm(p, w["w_ple_proj"], tb=True, name="mm_ple_proj")
    hg = _mm(x1b, w["w_up"], tb=True, b_rows=(0, DFF), name="mm_up_g", out_dtype=BF16, tm=512, tn=DFF)
    hu = _mm(x1b, w["w_up"], tb=True, b_rows=(DFF, DFF), name="mm_up_u", out_dtype=BF16, tm=512, tn=DFF)
    act = _conv_fwd(hg, hu, w["conv_wg"], w["conv_wu"], w["conv_bg"], w["conv_bu"])
    ffn = _mm(act, w["w_down"], name="mm_down", tm=512, tk=DFF)
    h2, x2, x2b = _rowwise(_f_ln2, [x1, ffn, z, pp], [w["ln2_g"], w["ln2_b"]], [(D, F32), (D, F32), (D, BF16)], [], name="ln2_fwd")
    saved = dict(xb=xb, proj=proj, qkvs=qkvs, attn=attn, lse=lse, ret_raw=ret_raw, states=states, r=r, ap=ap, rp=rp,
                 merged=merged, h1=h1, x1b=x1b, z=z, pp=pp, hg=hg, hu=hu, act=act, h2=h2, p=p)
    return x2, x2b, saved


def _layer_bwd(dys, w, sv, cos, sin, rconsts):
    gr = {}
    proj = sv["proj"]
    dh2, dh2b, gr["ln2_g"], gr["ln2_b"] = _rowwise(_f_ln_bwd, list(dys) + [sv["h2"]], [w["ln2_g"]], [(D, F32), (D, BF16)],
                                                   [(1, D), (1, D)], name="ln2_bwd")
    d_act = _mm(dh2b, w["w_down"], tb=True, name="mm_down_dx", out_dtype=BF16, tm=512, tn=DFF)
    gr["w_down"] = _mm(sv["act"], dh2b, ta=True, name="mm_down_dw", tm=DFF // 2)
    dcg, dcu, gwg, gwu, gbg, gbu = _conv_bwd_pre(d_act, sv["hg"], sv["hu"], w["conv_wg"], w["conv_wu"], w["conv_bg"], w["conv_bu"])
    gr["conv_w"] = jnp.concatenate([gwg, gwu], axis=1)
    gr["conv_b"] = jnp.concatenate([gbg, gbu], axis=1)
    dhg = _conv_bwd_in(dcg, w["conv_wg"], "conv_bwd_in_g")
    dhu = _conv_bwd_in(dcu, w["conv_wu"], "conv_bwd_in_u")
    gw_up = _mm(dhg, sv["x1b"], ta=True, name="mm_up_g_dw", tm=DFF // 2, out_rows=(0, 2 * DFF))
    gr["w_up"] = _mm(dhu, sv["x1b"], ta=True, name="mm_up_u_dw", tm=DFF // 2, out_rows=(DFF, 2 * DFF), into=gw_up)
    dx1 = _mm(dhg, w["w_up"], b_rows=(0, DFF), name="mm_up_g_dx", add=dh2, add_scale=ALPHA, tm=512, tk=DFF)
    dx1 = _mm(dhu, w["w_up"], b_rows=(DFF, DFF), name="mm_up_u_dx", add=dx1, tm=512, tk=DFF)
    dpp, dz = _rowwise(_f_ple_bwd, [dh2, sv["z"], sv["pp"]], [], [(D, BF16), (D, BF16)], [], name="ple_bwd")
    gr["w_ple_proj"] = _mm(dpp, sv["p"], ta=True, name="mm_ple_proj_dw")
    gr["w_ple_gate"] = _mm(sv["x1b"], dz, ta=True, name="mm_ple_gate_dw")
    dx1 = _mm(dz, w["w_ple_gate"], tb=True, name="mm_ple_gate_dx", add=dx1)
    dh1, dh1b, gr["ln1_g"], gr["ln1_b"] = _rowwise(_f_ln_bwd, [dx1, sv["h1"]], [w["ln1_g"]], [(D, F32), (D, BF16)],
                                                   [(1, D), (1, D)], name="ln1_bwd")
    d_merged = _mm(dh1b, w["w_out"], tb=True, name="mm_out_dx", out_dtype=BF16)
    gr["w_out"] = _mm(sv["merged"], dh1b, ta=True, name="mm_out_dw")
    rg_win = (proj, RH * RDV, OFF_RG // (RH * RDV))
    ga_win, gr_win = (proj, D, OFF_GA // D), (proj, D, OFF_GR // D)
    dap, drp, d_rest = _rowwise(_f_gate_bwd, [d_merged, sv["ap"], sv["rp"], ga_win, gr_win], [],
                                [(D, BF16), (D, BF16), (2 * D, BF16, N_REST, OFF_GA // (2 * D), None)], [], name="gate_bwd")
    d_attn = _mm(dap, w["w_attn_proj"], tb=True, name="mm_attn_proj_dx", out_dtype=BF16)
    gr["w_attn_proj"] = _mm(sv["attn"], dap, ta=True, name="mm_attn_proj_dw")
    d_r = _mm(drp, w["w_ret_proj"], tb=True, name="mm_ret_proj_dx", out_dtype=BF16, tn=2048)
    gr["w_ret_proj"] = _mm(sv["r"], drp, ta=True, name="mm_ret_proj_dw", tm=2048)
    d_ret, d_rest, gr["ret_gn_g"], gr["ret_gn_b"] = _rowwise(
        _f_gn_bwd, [d_r, sv["ret_raw"], rg_win], [w["ret_gn_g"], w["ret_gn_b"]],
        [(RH * RDV, BF16), (RH * RDV, BF16, N_REST, OFF_RG // (RH * RDV), d_rest)],
        [(1, RH * RDV), (1, RH * RDV)], name="gn_bwd", tm=256)
    d_rest = _ret_bwd(proj, cos, sin, rconsts, sv["states"], d_ret, d_rest)
    dx0 = _mm(d_rest, w["win"], b_rows=(N_ATT, N_REST), name="mm_proj_dx", add=dh1, add_scale=ALPHA)
    (delta,) = _rowwise(_f_delta, [d_attn, sv["attn"]], [], [(HD, F32)], [], name="attn_delta")
    gw_in, dx_parts = None, []
    for g, dil in enumerate(DILATIONS):
        dqkv = _attn_bwd(sv["qkvs"][g], _to_head_residues(d_attn, dil), _to_residues(sv["lse"], dil), _to_residues(delta, dil),
                         dil, f"attn_bwd_g{g}")
        gw_in = _qkv_dw(dqkv, _x_view(sv["xb"], dil), g, dil, f"mm_qkv{g}_dw", into=gw_in)
        if dil == 1:
            dx0 = _qkv_dx(dqkv, w["win"], g, dil, f"mm_qkv{g}_dx", F32, add=dx0)
        else:
            dx_parts.append(_qkv_dx(dqkv, w["win"], g, dil, f"mm_qkv{g}_dx", BF16).reshape(dx0.shape))
    gw_in = _mm(d_rest, sv["xb"], ta=True, name="mm_proj_dw", out_rows=(N_ATT, N_IN), into=gw_in, blocks8=True)
    gr["w_in"] = gw_in.reshape(N_IN, D)
    return [dx0] + dx_parts, gr


def _local_step(x, p, positions, target, ws):
    half = RDK // 2
    freq = jnp.power(ROPE_BASE, -jnp.arange(half, dtype=F32) / half)
    ang = positions.astype(F32)[:, None] * freq[None, :]
    cos, sin = jnp.cos(ang), jnp.sin(ang)
    rconsts = _ret_consts()
    xb = x.astype(BF16)
    saved, ws = [], list(ws)
    for l in range(DEPTH):
        if callable(ws[l]):
            ws[l] = ws[l](x)
        x, xb, sv = _layer_fwd(x, xb, p[l], ws[l], cos, sin, rconsts)
        saved.append(sv)
    dy, loss_vec = _rowwise(_f_loss, [x, target], [], [(D, F32)], [(1, D)], name="loss")
    dys, grads = [dy], [None] * DEPTH
    for l in reversed(range(DEPTH)):
        dys, grads[l] = _layer_bwd(dys, ws[l], saved[l], cos, sin, rconsts)
    (grad_x,) = _rowwise(_f_sum, dys, [], [(D, F32)], [], name="grad_x_sum")
    return loss_vec, grad_x, grads


def _pack_rows(arrs):
    parts, where, off = [], [], 0
    for t in arrs:
        t = t.reshape(-1, D)
        rows = t.shape[0]
        padded = -(-rows // 8) * 8
        parts.append(jnp.pad(t, ((0, padded - rows), (0, 0))))
        where.append((off, rows))
        off += padded
    return jnp.concatenate(parts, axis=0), where


def _layer_weights(g, l, conv_w_all, conv_b, W):
    w = dict(win=g["w_in"].reshape(N_IN, D), w_up=g["w_up"].reshape(2 * DFF, D),
             w_ple_proj=g["w_ple_proj"].reshape(D, PLE), w_attn_proj=g["w_attn_proj"].reshape(D, D),
             w_ret_proj=g["w_ret_proj"].reshape(RH * RDV, D), w_out=g["w_out"].reshape(D, D),
             w_down=g["w_down"].reshape(DFF, D), w_ple_gate=g["w_ple_gate"].reshape(D, D))
    w["conv_wg"], w["conv_wu"] = conv_w_all[l][:, :DFF], conv_w_all[l][:, DFF:]
    w["conv_bg"], w["conv_bu"] = conv_b[l][None, :DFF], conv_b[l][None, DFF:]
    for n in ("ret_gn_g", "ret_gn_b", "ln1_g", "ln1_b", "ln2_g", "ln2_b"):
        w[n] = W[n][l][None, :]
    return w


def kernel(x, p, positions, w_in, w_attn_proj, w_ret_proj, ret_gn_g, ret_gn_b, w_out, ln1_g, ln1_b, w_up, conv_w, conv_b, w_down, w_ple_gate, w_ple_proj, ln2_g, ln2_b, loss_target, m_w_in, m_w_attn_proj, m_w_ret_proj, m_ret_gn_g, m_ret_gn_b, m_w_out, m_ln1_g, m_ln1_b, m_w_up, m_conv_w, m_conv_b, m_w_down, m_w_ple_gate, m_w_ple_proj, m_ln2_g, m_ln2_b, v_w_in, v_w_attn_proj, v_w_ret_proj, v_ret_gn_g, v_ret_gn_b, v_w_out, v_ln1_g, v_ln1_b, v_w_up, v_conv_w, v_conv_b, v_w_down, v_w_ple_gate, v_w_ple_proj, v_ln2_g, v_ln2_b):
    W = dict(w_in=w_in, w_attn_proj=w_attn_proj, w_ret_proj=w_ret_proj, ret_gn_g=ret_gn_g, ret_gn_b=ret_gn_b, w_out=w_out,
             ln1_g=ln1_g, ln1_b=ln1_b, w_up=w_up, conv_w=conv_w, conv_b=conv_b, w_down=w_down, w_ple_gate=w_ple_gate,
             w_ple_proj=w_ple_proj, ln2_g=ln2_g, ln2_b=ln2_b)
    M = dict(w_in=m_w_in, w_attn_proj=m_w_attn_proj, w_ret_proj=m_w_ret_proj, ret_gn_g=m_ret_gn_g, ret_gn_b=m_ret_gn_b,
             w_out=m_w_out, ln1_g=m_ln1_g, ln1_b=m_ln1_b, w_up=m_w_up, conv_w=m_conv_w, conv_b=m_conv_b, w_down=m_w_down,
             w_ple_gate=m_w_ple_gate, w_ple_proj=m_w_ple_proj, ln2_g=m_ln2_g, ln2_b=m_ln2_b)
    V = dict(w_in=v_w_in, w_attn_proj=v_w_attn_proj, w_ret_proj=v_w_ret_proj, ret_gn_g=v_ret_gn_g, ret_gn_b=v_ret_gn_b,
             w_out=v_w_out, ln1_g=v_ln1_g, ln1_b=v_ln1_b, w_up=v_w_up, conv_w=v_conv_w, conv_b=v_conv_b, w_down=v_w_down,
             w_ple_gate=v_w_ple_gate, w_ple_proj=v_w_ple_proj, ln2_g=v_ln2_g, ln2_b=v_ln2_b)

    me = 4 * lax.axis_index("x") + 2 * lax.axis_index("y") + lax.axis_index("c")
    shards = [[(W[n][l].T if n in COL_SHARDED else W[n][l]).astype(BF16) for n in BIG] for l in range(DEPTH)]
    outs = _gather_many(shards[0] + [conv_w], "gather_weights_l0")
    conv_w_all = outs[-1].transpose(1, 2, 0, 3).reshape(DEPTH, 3, 2 * DFF)
    lands = [lax.dynamic_update_index_in_dim(lax.empty((N_DEV,) + t.shape, t.dtype), t, me, 0) for t in shards[1]]
    send_sems, recv_sems, sent, lands, token = _gather_start(shards[1], lands, "gather_weights_l1_start")

    def second_layer(after):
        got = _gather_wait(send_sems, recv_sems, sent, lands, after, "gather_weights_l1_wait")
        return _layer_weights(dict(zip(BIG, got)), 1, conv_w_all, conv_b, W)

    ws = [_layer_weights(dict(zip(BIG, outs)), 0, conv_w_all, conv_b, W), second_layer]
    loss_vec, grad_x, grads = _local_step(x[0] + token[0, 0], p[:, 0], positions[0], loss_target[0], ws)
    loss = lax.psum(jnp.sum(loss_vec), ("x", "y", "c"))

    core = lax.axis_index("c").astype(jnp.int32).reshape(1)
    chip = (2 * lax.axis_index("x") + lax.axis_index("y")).astype(jnp.int32).reshape(1)
    mine = [grads[l][n].reshape((N_DEV, -1) + grads[l][n].shape[1:]) for n in BIG for l in range(DEPTH)]
    theirs = _exchange_cores(mine, "exchange_grads_cores")
    sums = [_pair_sum(a, b, core, f"pair_sum_{i}") for i, (a, b) in enumerate(zip(mine, theirs))]
    fars = _exchange_chips([sums[DEPTH * a + l][0 if n in F32_OVER_ICI else 1] for a, n in enumerate(BIG) for l in range(DEPTH)],
                           "exchange_grads_chips")
    G, DW, NM, NV = ({} for _ in range(4))
    for a, n in enumerate(BIG):
        chip32 = [sums[DEPTH * a + l][0] for l in range(DEPTH)]
        far = [fars[DEPTH * a + l] for l in range(DEPTH)]
        if n in COL_SHARDED:
            G[n] = _reduce_tail(chip32, far, chip, f"reduced_{n}")[0].transpose(0, 2, 1)
            R2, C2 = DEPTH * W[n].shape[1], W[n].shape[2]
            res = _adamw(*(t.reshape(R2, C2) for t in (G[n], W[n], M[n], V[n])), f"adamw_{n}")
            DW[n], NM[n], NV[n] = (t.reshape(W[n].shape) for t in res)
        else:
            G[n], DW[n], NM[n], NV[n] = _reduce_tail(chip32, far, chip, f"adamw_{n}", wmv=(W[n], M[n], V[n]))

    small_names = SMALL + ("conv_w",)
    g_small, where = _pack_rows([jnp.stack([grads[l][n] for l in range(DEPTH)]) for n in small_names])
    (g_all,) = _gather_many([g_small], "gather_small_grads")
    g_small = _sum_slots(g_all, "sum_small_grads")
    for n, (off, rows) in zip(SMALL, where):
        G[n] = g_small[off:off + rows].reshape(W[n].shape)
    off, rows = where[-1]
    g_cw = g_small[off:off + rows].reshape(DEPTH, 3, N_DEV, conv_w.shape[2])
    G["conv_w"] = lax.dynamic_index_in_dim(g_cw, me, axis=2, keepdims=False)
    packed = [_pack_rows([d[n] for n in SMALL]) for d in (G, W, M, V)]
    small_out = _adamw(*(t for t, _ in packed), "adamw_small")
    for res, dst in zip(small_out, (DW, NM, NV)):
        for n, (off, rows) in zip(SMALL, packed[0][1]):
            dst[n] = res[off:off + rows].reshape(W[n].shape)
    two_d = lambda t: t.reshape(DEPTH * 3, conv_w.shape[2])
    cw_out = _adamw(two_d(G["conv_w"]), two_d(conv_w), two_d(m_conv_w), two_d(v_conv_w), "adamw_conv_w")
    for res, dst in zip(cw_out, (DW, NM, NV)):
        dst["conv_w"] = res.reshape(conv_w.shape)

    return (loss, grad_x[None], *[G[n] for n in WEIGHTS], *[DW[n] for n in WEIGHTS], *[NM[n] for n in WEIGHTS],
            *[NV[n] for n in WEIGHTS])
```

```python
import math

import numpy as np
import jax
import jax.numpy as jnp
from jax import lax
from jax.experimental import pallas as pl
from jax.experimental.pallas import tpu as pltpu

F32, BF16 = jnp.float32, jnp.bfloat16

D = 1024
DEPTH = 2
N_DEV = 8
HD = 128
NH = 8
DILATIONS = (1, 4, 16)
SPAN = 128
N_ATT = 3 * 3 * NH * HD
RH, RDK, RDV = 4, 256, 512
CH = 128
DFF = 2816
PLE = 256
N_IN = 17408
N_REST = N_IN - N_ATT
OFF_RQ, OFF_RK, OFF_RV, OFF_RG, OFF_GA, OFF_GR = 0, 1024, 2048, 4096, 6144, 7168
ALPHA = (2 * DEPTH) ** 0.25
LN_EPS, GN_EPS = 1e-5, 1e-6
ROPE_BASE = 10000.0
LR, B1, B2, EPS, WD, STEP = 0.001, 0.9, 0.999, 1e-8, 0.01, 10
VMEM_LIMIT = 48 * 1024 * 1024
NEG = -1e30

BIG = ("w_in", "w_attn_proj", "w_ret_proj", "w_out", "w_up", "w_down", "w_ple_gate", "w_ple_proj")
COL_SHARDED = ("w_in", "w_up", "w_ple_proj")
F32_OVER_ICI = ("w_attn_proj", "w_out", "w_ple_gate", "w_ple_proj")
SMALL = ("ret_gn_g", "ret_gn_b", "ln1_g", "ln1_b", "conv_b", "ln2_g", "ln2_b")
WEIGHTS = ("w_in", "w_attn_proj", "w_ret_proj", "ret_gn_g", "ret_gn_b", "w_out", "ln1_g", "ln1_b", "w_up",
           "conv_w", "conv_b", "w_down", "w_ple_gate", "w_ple_proj", "ln2_g", "ln2_b")


def _tile(n, cap, mult=128):
    if n <= cap:
        return n
    t = (cap // mult) * mult
    while n % t:
        t -= mult
    return t


def _cparams(sem):
    return pltpu.CompilerParams(dimension_semantics=sem, vmem_limit_bytes=VMEM_LIMIT)


def _dot(a, b, ca, cb):
    return lax.dot_general(a, b, (((ca,), (cb,)), ((), ())), preferred_element_type=F32)


def _bdot(a, b, ca, cb):
    return lax.dot_general(a, b, (((ca,), (cb,)), ((0,), (0,))), preferred_element_type=F32)


def _mm(a, b, *, name, ta=False, tb=False, out_dtype=F32, add=None, add_scale=1.0, tm=1024, tn=1024, tk=1024,
        b_rows=None, out_rows=None, into=None, blocks8=False):
    M, K = (a.shape[1], a.shape[0]) if ta else a.shape
    b_first, b_count = b_rows if b_rows else (0, b.shape[0])
    N = b_count if tb else b.shape[1]
    assert K == (b.shape[1] if tb else b_count)
    tm, tn, tk = _tile(M, tm), _tile(N, tn), _tile(K, tk)
    nk = K // tk
    o_first, o_total = out_rows if out_rows else (0, M)
    jb, kb, io = (b_first // tn, 0, o_first // tm) if tb else (0, b_first // tk, o_first // tm)
    assert b_first % (tn if tb else tk) == 0 and o_first % tm == 0 and (add is None or out_rows is None)

    def body(*refs):
        if add is None:
            a_ref, b_ref = refs[:2]
        else:
            a_ref, b_ref, add_ref = refs[:3]
        o_ref, acc_ref = refs[-2:]
        k = pl.program_id(2)

        @pl.when(k == 0)
        def _():
            acc_ref[...] = jnp.zeros_like(acc_ref)

        acc_ref[...] += _dot(a_ref[...].astype(BF16), b_ref[...].astype(BF16), 0 if ta else 1, 1 if tb else 0)

        @pl.when(k == nk - 1)
        def _():
            r = acc_ref[...]
            if add is not None:
                r = r + add_scale * add_ref[...].astype(F32)
            o_ref[...] = r.astype(out_dtype).reshape(o_ref.shape)

    a_spec = pl.BlockSpec((tk, tm), lambda i, j, k: (k, i)) if ta else pl.BlockSpec((tm, tk), lambda i, j, k: (i, k))
    if tb:
        b_spec = pl.BlockSpec((tn, tk), lambda i, j, k: (j + jb, k))
    else:
        b_spec = pl.BlockSpec((tk, tn), lambda i, j, k: (k + kb, j))
    if blocks8:
        assert tm == 1024
        o_spec = pl.BlockSpec((1, 8, 128, tn), lambda i, j, k: (i + io, 0, 0, j))
        o_shape = (o_total // tm, 8, 128, N)
    else:
        o_spec = pl.BlockSpec((tm, tn), lambda i, j, k: (i + io, j))
        o_shape = (o_total, N)
    in_specs, args, aliases = [a_spec, b_spec], [a, b], {}
    if add is not None:
        in_specs.append(o_spec)
        args.append(add)
    if into is not None:
        aliases = {len(args): 0}
        in_specs.append(pl.BlockSpec(memory_space=pl.ANY))
        args.append(into)
    return pl.pallas_call(
        body, name=name, grid=(M // tm, N // tn, nk), in_specs=in_specs, out_specs=o_spec,
        out_shape=jax.ShapeDtypeStruct(o_shape, out_dtype), scratch_shapes=[pltpu.VMEM((tm, tn), F32)],
        input_output_aliases=aliases, compiler_params=_cparams(("parallel", "parallel", "arbitrary")),
    )(*args)


def _rowwise(fn, rows, pars, outs, accs, *, name, tm=512):
    first = rows[0][0] if isinstance(rows[0], tuple) else rows[0]
    S = first.shape[-2]
    tm = _tile(S, tm, 16)
    n_r, n_p, n_o = len(rows), len(pars), len(outs)
    outs = [o if len(o) == 5 else (o[0], o[1], o[0], 0, None) for o in outs]
    intos = [(k, o[4]) for k, o in enumerate(outs) if o[4] is not None]
    n_i = len(intos)

    def body(*refs):
        i = pl.program_id(0)
        vals = [r[...] for r in refs[:n_r + n_p]]
        res = fn(*vals)
        if not isinstance(res, (tuple, list)):
            res = (res,)
        o_refs = refs[n_r + n_p + n_i:n_r + n_p + n_i + n_o]
        a_refs = refs[n_r + n_p + n_i + n_o:]
        for r, v in zip(o_refs, res[:n_o]):
            r[...] = v.astype(r.dtype)
        if a_refs:
            @pl.when(i == 0)
            def _():
                for r in a_refs:
                    r[...] = jnp.zeros_like(r)

            for r, v in zip(a_refs, res[n_o:]):
                r[...] += v

    in_specs, args = [], []
    for r in rows:
        if isinstance(r, tuple):
            arr, w, cb = r
            in_specs.append(pl.BlockSpec((tm, w), lambda i, cb=cb: (i, cb)))
        elif r.ndim == 3:
            arr = r
            in_specs.append(pl.BlockSpec((arr.shape[0], tm, arr.shape[2]), lambda i: (0, i, 0)))
        else:
            arr = r
            in_specs.append(pl.BlockSpec((tm, arr.shape[1]), lambda i: (i, 0)))
        args.append(arr)
    for p_ in pars:
        in_specs.append(pl.BlockSpec(p_.shape, lambda i: (0, 0)))
        args.append(p_)
    aliases = {}
    for k, arr in intos:
        aliases[len(args)] = k
        in_specs.append(pl.BlockSpec(memory_space=pl.ANY))
        args.append(arr)
    out_shape = [jax.ShapeDtypeStruct((S, o[2]), o[1]) for o in outs] + [jax.ShapeDtypeStruct(a, F32) for a in accs]
    out_specs = [pl.BlockSpec((tm, o[0]), lambda i, cb=o[3]: (i, cb)) for o in outs] + [pl.BlockSpec(a, lambda i: (0, 0)) for a in accs]
    return pl.pallas_call(
        body, name=name, grid=(S // tm,), in_specs=in_specs, out_specs=out_specs, out_shape=out_shape,
        input_output_aliases=aliases, compiler_params=_cparams(("arbitrary",) if accs else ("parallel",)),
    )(*args)


def _norm(h, eps):
    mu = jnp.mean(h, -1, keepdims=True)
    d = h - mu
    rstd = lax.rsqrt(jnp.mean(d * d, -1, keepdims=True) + eps)
    return d * rstd, rstd


def _norm_bwd(dxh, xh, rstd):
    return rstd * (dxh - jnp.mean(dxh, -1, keepdims=True) - xh * jnp.mean(dxh * xh, -1, keepdims=True))


def _sig(x):
    return 1.0 / (1.0 + jnp.exp(-x))


_GELU_C = math.sqrt(2.0 / math.pi)


def _gelu(x):
    t = jnp.tanh(_GELU_C * (x + 0.044715 * x * x * x))
    return 0.5 * x * (1.0 + t), t


def _gelu_grad(x, t):
    return 0.5 * (1.0 + t) + 0.5 * x * (1.0 - t * t) * _GELU_C * (1.0 + 3 * 0.044715 * x * x)


def _f_ln1(x, mix, g, b):
    h = ALPHA * x + mix
    xh, _ = _norm(h, LN_EPS)
    y = xh * g + b
    return h, y, y


def _f_ln2(x, ffn, z, pp, g, b):
    h = ALPHA * x + ffn + _sig(z) * pp
    xh, _ = _norm(h, LN_EPS)
    y = xh * g + b
    return h, y, y


def _f_ln_bwd(*args):
    *dys, h, g = args
    dy = dys[0]
    for t in dys[1:]:
        dy = dy + t
    xh, rstd = _norm(h, LN_EPS)
    dh = _norm_bwd(dy * g, xh, rstd)
    return dh, dh, jnp.sum(dy * xh, 0, keepdims=True), jnp.sum(dy, 0, keepdims=True)


def _f_sum(*ts):
    r = ts[0]
    for t in ts[1:]:
        r = r + t
    return r


def _f_loss(y, t):
    e = y - t
    return e * (1.0 / D), jnp.sum(e * e, 0, keepdims=True) * (0.5 / D)


def _head_col(c, h):
    lane = lax.broadcasted_iota(jnp.int32, c.shape, 1)
    return jnp.sum(jnp.where(lane == h, c, 0.0), -1, keepdims=True)


def _f_combine(o0, o1, o2, l0, l1, l2):
    lane = lax.broadcasted_iota(jnp.int32, l0.shape, 1)
    parts, lse = [], jnp.zeros(l0.shape, F32)
    for h in range(NH):
        a0, a1, a2 = _head_col(l0, h), _head_col(l1, h), _head_col(l2, h)
        m = jnp.maximum(jnp.maximum(a0, a1), a2)
        e0, e1, e2 = jnp.exp(a0 - m), jnp.exp(a1 - m), jnp.exp(a2 - m)
        den = e0 + e1 + e2
        parts.append((e0 * o0[h].astype(F32) + e1 * o1[h].astype(F32) + e2 * o2[h].astype(F32)) / den)
        lse = jnp.where(lane == h, m + jnp.log(den), lse)
    return jnp.concatenate(parts, axis=1), lse


def _f_delta(da, a):
    lane = lax.broadcasted_iota(jnp.int32, (da.shape[0], HD), 1)
    out = jnp.zeros((da.shape[0], HD), F32)
    for h in range(NH):
        sl = slice(h * HD, (h + 1) * HD)
        s = jnp.sum(da[:, sl].astype(F32) * a[:, sl].astype(F32), -1, keepdims=True)
        out = jnp.where(lane == h, s, out)
    return out


def _f_gate(ap, rp, ga, gr):
    return _sig(ga.astype(F32)) * ap.astype(F32) + _sig(gr.astype(F32)) * rp.astype(F32)


def _f_gate_bwd(dm, ap, rp, ga, gr):
    dm = dm.astype(F32)
    sa, sr = _sig(ga.astype(F32)), _sig(gr.astype(F32))
    dga, dgr = dm * ap.astype(F32) * sa * (1.0 - sa), dm * rp.astype(F32) * sr * (1.0 - sr)
    return dm * sa, dm * sr, jnp.concatenate([dga, dgr], axis=1)


def _f_gn(y, rg, g, b):
    y, rg = y.astype(F32), rg.astype(F32)
    parts = []
    for h in range(RH):
        sl = slice(h * RDV, (h + 1) * RDV)
        xh, _ = _norm(y[:, sl], GN_EPS)
        parts.append(xh * g[:, sl] + b[:, sl])
    return rg * _sig(rg) * jnp.concatenate(parts, axis=1)


def _f_gn_bwd(dr, y, rg, g, b):
    dr, y, rg = dr.astype(F32), y.astype(F32), rg.astype(F32)
    s = _sig(rg)
    d_out = dr * rg * s
    dys, outs, xhs = [], [], []
    for h in range(RH):
        sl = slice(h * RDV, (h + 1) * RDV)
        xh, rstd = _norm(y[:, sl], GN_EPS)
        xhs.append(xh)
        outs.append(xh * g[:, sl] + b[:, sl])
        dys.append(_norm_bwd(d_out[:, sl] * g[:, sl], xh, rstd))
    xh, out = jnp.concatenate(xhs, axis=1), jnp.concatenate(outs, axis=1)
    d_rg = dr * out * s * (1.0 + rg * (1.0 - s))
    return jnp.concatenate(dys, axis=1), d_rg, jnp.sum(d_out * xh, 0, keepdims=True), jnp.sum(d_out, 0, keepdims=True)


def _f_ple_bwd(dh, z, pp):
    s = _sig(z)
    return dh * s, dh * pp * s * (1.0 - s)


QKV = 3 * HD


def _to_tokens(t, d):
    if d == 1:
        return t
    *lead, S, C = t.shape
    n = len(lead)
    perm = tuple(range(n)) + (n + 1, n, n + 2)
    return t.reshape(*lead, d, S // d, C).transpose(perm).reshape(*lead, S, C)


def _to_residues(t, d):
    if d == 1:
        return t
    S, C = t.shape
    return t.reshape(S // d, d, C).transpose(1, 0, 2).reshape(S, C)


def _to_head_residues(t, d):
    S = t.shape[0]
    return t.reshape(S // d, d, NH, HD).transpose(2, 1, 0, 3).reshape(NH, S, HD)


def _w_qkv_specs(g):
    return [pl.BlockSpec((D, D), lambda *i, t=t: (3 * g + t, 0)) for t in range(3)]


def _qkv_fwd(xv, win, g, dil, name):
    Sd = xv.shape[0]
    S = Sd * dil
    tm = min(512, Sd)
    nma = Sd // tm

    def body(a_ref, wq_ref, wk_ref, wv_ref, o_ref):
        a = a_ref[...]
        q, k, v = (_dot(a, w_ref[...], 1, 1).astype(BF16) for w_ref in (wq_ref, wk_ref, wv_ref))
        for h in range(NH):
            sl = slice(h * HD, (h + 1) * HD)
            o_ref[h] = jnp.concatenate([q[:, sl], k[:, sl], v[:, sl]], axis=1)

    return pl.pallas_call(
        body, name=name, grid=(S // tm,),
        in_specs=[pl.BlockSpec((tm, D), lambda i: (i % nma, i // nma))] + _w_qkv_specs(g),
        out_specs=pl.BlockSpec((NH, tm, QKV), lambda i: (0, i, 0)), out_shape=jax.ShapeDtypeStruct((NH, S, QKV), BF16),
        compiler_params=_cparams(("parallel",)),
    )(xv, win, win, win)


def _qkv_dx(dqkv, win, g, dil, name, out_dtype, add=None):
    S = dqkv.shape[1]
    Sd = S // dil
    tm = min(512, Sd)
    nmo = Sd // tm

    def body(*refs):
        a_ref, wq_ref, wk_ref, wv_ref = refs[:4]
        o_ref = refs[-1]
        acc = None
        for h in range(NH):
            sl = slice(h * HD, (h + 1) * HD)
            w = jnp.concatenate([wq_ref[sl, :], wk_ref[sl, :], wv_ref[sl, :]], axis=0)
            part = _dot(a_ref[h], w, 1, 0)
            acc = part if acc is None else acc + part
        if add is not None:
            acc = acc + refs[4][...]
        o_ref[...] = acc.astype(out_dtype)

    o_spec = pl.BlockSpec((tm, D), lambda i: (i % nmo, i // nmo))
    in_specs = [pl.BlockSpec((NH, tm, QKV), lambda i: (0, i, 0))] + _w_qkv_specs(g)
    args = [dqkv, win, win, win]
    if add is not None:
        assert dil == 1
        in_specs.append(o_spec)
        args.append(add)
    return pl.pallas_call(
        body, name=name, grid=(S // tm,), in_specs=in_specs, out_specs=o_spec,
        out_shape=jax.ShapeDtypeStruct((Sd, dil * D), out_dtype), compiler_params=_cparams(("parallel",)),
    )(*args)


GW_IN_BLOCKS = (N_IN // D, NH, HD, D)


def _qkv_dw(dqkv, xv, g, dil, name, into=None):
    S = dqkv.shape[1]
    Sd = S // dil
    tk = min(1024, Sd)
    nkb, nk = Sd // tk, S // tk
    hh = NH // 2

    def body(*refs):
        a_ref, b_ref = refs[:2]
        o_ref, acc_ref = refs[-2:]
        k = pl.program_id(1)

        @pl.when(k == 0)
        def _():
            acc_ref[...] = jnp.zeros_like(acc_ref)

        b = b_ref[...]
        for h in range(hh):
            acc_ref[h * QKV:(h + 1) * QKV, :] += _dot(a_ref[h], b, 0, 0)

        @pl.when(k == nk - 1)
        def _():
            for h in range(hh):
                for t in range(3):
                    o_ref[t, h] = acc_ref[h * QKV + t * HD:h * QKV + (t + 1) * HD, :]

    in_specs = [pl.BlockSpec((hh, tk, QKV), lambda j, k: (j, k, 0)), pl.BlockSpec((tk, D), lambda j, k: (k % nkb, k // nkb))]
    args, aliases = [dqkv, xv], {}
    if into is not None:
        aliases = {2: 0}
        in_specs.append(pl.BlockSpec(memory_space=pl.ANY))
        args.append(into)
    return pl.pallas_call(
        body, name=name, grid=(2, nk), in_specs=in_specs,
        out_specs=pl.BlockSpec((3, hh, HD, D), lambda j, k: (g, j, 0, 0)), out_shape=jax.ShapeDtypeStruct(GW_IN_BLOCKS, F32),
        input_output_aliases=aliases, scratch_shapes=[pltpu.VMEM((hh * QKV, D), F32)],
        compiler_params=_cparams(("parallel", "arbitrary")),
    )(*args)


def _band(nb, first_valid, last_valid=None):
    b = lax.broadcasted_iota(jnp.int32, (nb, SPAN, SPAN), 0)
    row = lax.broadcasted_iota(jnp.int32, (nb, SPAN, SPAN), 1)
    col = lax.broadcasted_iota(jnp.int32, (nb, SPAN, SPAN), 2)
    off = jnp.where(b == 0, jnp.where(first_valid, 0, 2 * SPAN), 0)
    if last_valid is not None:
        off = off + jnp.where(b == nb - 1, jnp.where(last_valid, 0, 2 * SPAN), 0)
    return col <= row, col >= row + off


def _attn_tiles(S, dil):
    Sd = S // dil
    T = min(1024, Sd)
    return Sd, T, T // SPAN, Sd // T


def _attn_fwd(qkv, dil, name):
    S = qkv.shape[1]
    Sd, T, nsub, nib = _attn_tiles(S, dil)
    scale = HD ** -0.5

    def body(c_ref, p_ref, o_ref, l_ref):
        ib, h = pl.program_id(1), pl.program_id(2)
        blk, hal = c_ref[...], p_ref[...]
        q, k, v = blk[:, :HD], blk[:, HD:2 * HD], blk[:, 2 * HD:]
        if nsub > 1:
            kp = jnp.concatenate([hal[:, HD:2 * HD], k[:T - SPAN]], axis=0)
            vp = jnp.concatenate([hal[:, 2 * HD:], v[:T - SPAN]], axis=0)
        else:
            kp, vp = hal[:, HD:2 * HD], hal[:, 2 * HD:]
        q3, k3, v3, kp3, vp3 = (t.reshape(nsub, SPAN, HD) for t in (q, k, v, kp, vp))
        m_cur, m_prev = _band(nsub, ib > 0)
        sc = jnp.where(m_cur, _bdot(q3, k3, 2, 2) * scale, NEG)
        sp = jnp.where(m_prev, _bdot(q3, kp3, 2, 2) * scale, NEG)
        m = jnp.maximum(jnp.max(sc, -1, keepdims=True), jnp.max(sp, -1, keepdims=True))
        pc, pp = jnp.exp(sc - m), jnp.exp(sp - m)
        den = jnp.sum(pc, -1, keepdims=True) + jnp.sum(pp, -1, keepdims=True)
        o = (_bdot(pc.astype(BF16), v3, 2, 1) + _bdot(pp.astype(BF16), vp3, 2, 1)) / den
        o_ref[...] = o.reshape(T, HD).astype(BF16)
        lse = (m + jnp.log(den)).reshape(T, 1)
        lane = lax.broadcasted_iota(jnp.int32, (T, HD), 1)

        @pl.when(h == 0)
        def _():
            l_ref[...] = jnp.zeros_like(l_ref)

        l_ref[...] = jnp.where(lane == h, lse, l_ref[...])

    cur = pl.BlockSpec((None, T, QKV), lambda r, ib, h: (h, r * nib + ib, 0))
    prev = pl.BlockSpec((None, SPAN, QKV), lambda r, ib, h: (h, r * (Sd // SPAN) + jnp.maximum(ib * nsub - 1, 0), 0))
    return pl.pallas_call(
        body, name=name, grid=(dil, nib, NH), in_specs=[cur, prev],
        out_specs=[pl.BlockSpec((None, T, HD), lambda r, ib, h: (h, r * nib + ib, 0)),
                   pl.BlockSpec((T, HD), lambda r, ib, h: (r * nib + ib, 0))],
        out_shape=[jax.ShapeDtypeStruct((NH, S, HD), BF16), jax.ShapeDtypeStruct((S, HD), F32)],
        compiler_params=_cparams(("parallel", "parallel", "arbitrary")),
    )(qkv, qkv)


def _attn_bwd(qkv, d_attn, lse, delta, dil, name):
    S = qkv.shape[1]
    Sd, T, nsub, nib = _attn_tiles(S, dil)
    scale = HD ** -0.5
    ne = nsub + 1

    def body(c_ref, p_ref, n_ref, do_ref, don_ref, l_ref, ln_ref, dl_ref, dln_ref, o_ref):
        ib, h = pl.program_id(1), pl.program_id(2)
        blk, hal, nxt = c_ref[...], p_ref[...], n_ref[...]
        q, k, v = blk[:, :HD], blk[:, HD:2 * HD], blk[:, 2 * HD:]
        do = do_ref[...]
        l, dl = _head_col(l_ref[...], h), _head_col(dl_ref[...], h)
        qe = jnp.concatenate([q, nxt[:, :HD]], axis=0).reshape(ne, SPAN, HD)
        doe = jnp.concatenate([do, don_ref[...]], axis=0).reshape(ne, SPAN, HD)
        le = jnp.concatenate([l, _head_col(ln_ref[...], h)], axis=0).reshape(ne, SPAN, 1)
        dle = jnp.concatenate([dl, _head_col(dln_ref[...], h)], axis=0).reshape(ne, SPAN, 1)
        kpe = jnp.concatenate([hal[:, HD:2 * HD], k], axis=0).reshape(ne, SPAN, HD)
        vpe = jnp.concatenate([hal[:, 2 * HD:], v], axis=0).reshape(ne, SPAN, HD)
        _, m_prev = _band(ne, ib > 0, ib < nib - 1)
        p = jnp.where(m_prev, jnp.exp(_bdot(qe, kpe, 2, 2) * scale - le), 0.0)
        ds = (p * (_bdot(doe, vpe, 2, 2) - dle)).astype(BF16)
        dq = _bdot(ds, kpe, 2, 1)[:nsub]
        dk = _bdot(ds, qe, 1, 1)[1:]
        dv = _bdot(p.astype(BF16), doe, 1, 1)[1:]
        q3, k3, v3, do3 = (t.reshape(nsub, SPAN, HD) for t in (q, k, v, do))
        l3, dl3 = l.reshape(nsub, SPAN, 1), dl.reshape(nsub, SPAN, 1)
        m_cur, _ = _band(nsub, True)
        p = jnp.where(m_cur, jnp.exp(_bdot(q3, k3, 2, 2) * scale - l3), 0.0)
        ds = (p * (_bdot(do3, v3, 2, 2) - dl3)).astype(BF16)
        dq = (dq + _bdot(ds, k3, 2, 1)) * scale
        dk = (dk + _bdot(ds, q3, 1, 1)) * scale
        dv = dv + _bdot(p.astype(BF16), do3, 1, 1)
        o_ref[...] = jnp.concatenate([t.reshape(T, HD) for t in (dq, dk, dv)], axis=1).astype(BF16)

    nb = Sd // SPAN
    row = lambda r, ib: r * nib + ib
    prow = lambda r, ib: r * nb + jnp.maximum(ib * nsub - 1, 0)
    nrow = lambda r, ib: r * nb + jnp.minimum((ib + 1) * nsub, nb - 1)
    cur3 = pl.BlockSpec((None, T, QKV), lambda r, ib, h: (h, row(r, ib), 0))
    prev3 = pl.BlockSpec((None, SPAN, QKV), lambda r, ib, h: (h, prow(r, ib), 0))
    next3 = pl.BlockSpec((None, SPAN, QKV), lambda r, ib, h: (h, nrow(r, ib), 0))
    cur1 = pl.BlockSpec((None, T, HD), lambda r, ib, h: (h, row(r, ib), 0))
    next1 = pl.BlockSpec((None, SPAN, HD), lambda r, ib, h: (h, nrow(r, ib), 0))
    curc = pl.BlockSpec((T, HD), lambda r, ib, h: (row(r, ib), 0))
    nextc = pl.BlockSpec((SPAN, HD), lambda r, ib, h: (nrow(r, ib), 0))
    return pl.pallas_call(
        body, name=name, grid=(dil, nib, NH),
        in_specs=[cur3, prev3, next3, cur1, next1, curc, nextc, curc, nextc], out_specs=cur3,
        out_shape=jax.ShapeDtypeStruct((NH, S, QKV), BF16),
        compiler_params=_cparams(("parallel", "parallel", "parallel")),
    )(qkv, qkv, qkv, d_attn, d_attn, lse, lse, delta, delta)


def _ret_consts():
    lg = np.log1p(-np.exp2(-5.0 - np.arange(RH, dtype=np.float64)))
    idx = np.arange(CH, dtype=np.float64)
    rel = idx[:, None] - idx[None, :]
    intra = np.where(rel >= 0, np.exp(lg[:, None, None] * np.maximum(rel, 0.0)), 0.0)
    qd = np.exp(lg[:, None] * (idx + 1.0))
    kd = np.exp(lg[:, None] * (CH - 1.0 - idx))
    cd = np.exp(lg * CH)
    wide = lambda t: np.broadcast_to(t[:, :, None], (RH, t.shape[1], RDV))
    return (jnp.asarray(intra, F32), jnp.asarray(wide(qd), F32), jnp.asarray(wide(kd), F32),
            jnp.asarray(np.broadcast_to(cd[:, None, None], (RH, 1, RDV)), F32))


def _rot(t, c, s):
    t1, t2 = t[:, :RDK // 2], t[:, RDK // 2:]
    return jnp.concatenate([t1 * c - t2 * s, t1 * s + t2 * c], axis=1)


def _unrot(d, c, s):
    d1, d2 = d[:, :RDK // 2], d[:, RDK // 2:]
    return jnp.concatenate([d1 * c + d2 * s, d2 * c - d1 * s], axis=1)


RCH = 2


def _ret_specs(nmap):
    rows = RCH * CH
    q = pl.BlockSpec((rows, RH * RDK), lambda n: (nmap(n), OFF_RQ // (RH * RDK)))
    k = pl.BlockSpec((rows, RH * RDK), lambda n: (nmap(n), OFF_RK // (RH * RDK)))
    v = pl.BlockSpec((rows, RH * RDV), lambda n: (nmap(n), OFF_RV // (RH * RDV)))
    cs = pl.BlockSpec((rows, RDK // 2), lambda n: (nmap(n), 0))
    dmat = pl.BlockSpec((RH, CH, CH), lambda n: (0, 0, 0))
    dvec = pl.BlockSpec((RH, CH, RDV), lambda n: (0, 0, 0))
    cdv = pl.BlockSpec((RH, 1, RDV), lambda n: (0, 0, 0))
    state = pl.BlockSpec((RH, RCH, RDK, RDV), lambda n: (0, nmap(n), 0, 0))
    out = pl.BlockSpec((rows, RH * RDV), lambda n: (nmap(n), 0))
    return [q, k, v, cs, cs, dmat, dvec, dvec, cdv], state, out


def _ret_fwd(proj, cos, sin, consts):
    S = proj.shape[0]
    nc = S // CH

    def body(q_ref, k_ref, v_ref, c_ref, s_ref, d_ref, qd_ref, kd_ref, cd_ref, o_ref, st_ref, state):
        @pl.when(pl.program_id(0) == 0)
        def _():
            state[...] = jnp.zeros_like(state)

        for ci in range(RCH):
            rows = slice(ci * CH, (ci + 1) * CH)
            c, s = c_ref[rows, :], s_ref[rows, :]
            for h in range(RH):
                qk, vv = slice(h * RDK, (h + 1) * RDK), slice(h * RDV, (h + 1) * RDV)
                qb = _rot(q_ref[rows, qk].astype(F32), c, s).astype(BF16)
                kb = (_rot(k_ref[rows, qk].astype(F32), c, s) * (RDK ** -0.5)).astype(BF16)
                vb = v_ref[rows, vv]
                sb = state[h].astype(BF16)
                st_ref[h, ci] = sb
                a = (_dot(qb, kb, 1, 1) * d_ref[h]).astype(BF16)
                o_ref[rows, vv] = (_dot(a, vb, 1, 0) + _dot(qb, sb, 1, 0) * qd_ref[h]).astype(BF16)
                vk = (vb.astype(F32) * kd_ref[h]).astype(BF16)
                state[h] = cd_ref[h] * state[h] + _dot(kb, vk, 0, 0)

    ins, state_spec, out_spec = _ret_specs(lambda n: n)
    return pl.pallas_call(
        body, name="ret_fwd", grid=(nc // RCH,), in_specs=ins, out_specs=[out_spec, state_spec],
        out_shape=[jax.ShapeDtypeStruct((S, RH * RDV), BF16), jax.ShapeDtypeStruct((RH, nc, RDK, RDV), BF16)],
        scratch_shapes=[pltpu.VMEM((RH, RDK, RDV), F32)],
        compiler_params=_cparams(("arbitrary",)),
    )(proj, proj, proj, cos, sin, *consts)


def _ret_bwd(proj, cos, sin, consts, states, d_ret, d_rest):
    S = proj.shape[0]
    nc = S // CH

    def body(q_ref, k_ref, v_ref, c_ref, s_ref, d_ref, qd_ref, kd_ref, cd_ref, st_ref, do_ref, _, o_ref, dstate):
        @pl.when(pl.program_id(0) == 0)
        def _():
            dstate[...] = jnp.zeros_like(dstate)

        for ci in reversed(range(RCH)):
            rows = slice(ci * CH, (ci + 1) * CH)
            c, s = c_ref[rows, :], s_ref[rows, :]
            for h in range(RH):
                qk, vv = slice(h * RDK, (h + 1) * RDK), slice(h * RDV, (h + 1) * RDV)
                qb = _rot(q_ref[rows, qk].astype(F32), c, s).astype(BF16)
                kb = (_rot(k_ref[rows, qk].astype(F32), c, s) * (RDK ** -0.5)).astype(BF16)
                vb, sb, do = v_ref[rows, vv], st_ref[h, ci], do_ref[rows, vv]
                dmat, qd, kd = d_ref[h], qd_ref[h], kd_ref[h]
                a = (_dot(qb, kb, 1, 1) * dmat).astype(BF16)
                doq = (do.astype(F32) * qd).astype(BF16)
                dsb = dstate[h].astype(BF16)
                vk = (vb.astype(F32) * kd).astype(BF16)
                o_ref[rows, OFF_RV + h * RDV:OFF_RV + (h + 1) * RDV] = (_dot(a, do, 0, 0) + _dot(kb, dsb, 1, 0) * kd).astype(BF16)
                da = (_dot(do, vb, 1, 1) * dmat).astype(BF16)
                dq = _dot(da, kb, 1, 0) + _dot(doq, sb, 1, 1)
                dk = (_dot(da, qb, 0, 0) + _dot(vk, dsb, 1, 1)) * (RDK ** -0.5)
                o_ref[rows, OFF_RQ + h * RDK:OFF_RQ + (h + 1) * RDK] = _unrot(dq, c, s).astype(BF16)
                o_ref[rows, OFF_RK + h * RDK:OFF_RK + (h + 1) * RDK] = _unrot(dk, c, s).astype(BF16)
                dstate[h] = cd_ref[h] * dstate[h] + _dot(qb, doq, 0, 0)

    nsteps = nc // RCH
    rev = lambda n: nsteps - 1 - n
    ins, state_spec, out_spec = _ret_specs(rev)
    return pl.pallas_call(
        body, name="ret_bwd", grid=(nsteps,), in_specs=ins + [state_spec, out_spec, pl.BlockSpec(memory_space=pl.ANY)],
        out_specs=pl.BlockSpec((RCH * CH, OFF_RG), lambda n: (rev(n), 0)),
        out_shape=jax.ShapeDtypeStruct(d_rest.shape, BF16), input_output_aliases={11: 0},
        scratch_shapes=[pltpu.VMEM((RH, RDK, RDV), F32)],
        compiler_params=_cparams(("arbitrary",)),
    )(proj, proj, proj, cos, sin, *consts, states, d_ret, d_rest)


CW = 256
HALO = 16


def _shift_down(v, halo, k):
    rolled = pltpu.roll(v, k, 0)
    hr = pltpu.roll(halo, k, 0)[0:8]
    row = lax.broadcasted_iota(jnp.int32, hr.shape, 0)
    return jnp.concatenate([jnp.where(row < k, hr, rolled[0:8]), rolled[8:]], axis=0)


def _shift_up(v, halo, k):
    T = v.shape[0]
    rolled = pltpu.roll(v, T - k, 0)
    hr = pltpu.roll(halo, 8 - k, 0)[0:8]
    row = lax.broadcasted_iota(jnp.int32, hr.shape, 0)
    return jnp.concatenate([rolled[:T - 8], jnp.where(row >= 8 - k, hr, rolled[T - 8:])], axis=0)


def _conv_taps(h_ref, hp_ref, first):
    h = h_ref[...].astype(F32)
    hp = hp_ref[...].astype(F32) * jnp.where(first, 0.0, 1.0)
    return _shift_down(h, hp, 2), _shift_down(h, hp, 1), h


def _conv_specs(S, T):
    nj = DFF // CW
    cur = pl.BlockSpec((T, CW), lambda j, i: (i, j))
    prev = pl.BlockSpec((HALO, CW), lambda j, i: (jnp.maximum(i * (T // HALO) - 1, 0), j))
    nxt = pl.BlockSpec((HALO, CW), lambda j, i: (jnp.minimum((i + 1) * (T // HALO), S // HALO - 1), j))
    w = pl.BlockSpec((3, CW), lambda j, i: (0, j))
    b = pl.BlockSpec((1, CW), lambda j, i: (0, j))
    return nj, cur, prev, nxt, w, b


def _conv_fwd(hg, hu, wg, wu, bg, bu):
    S = hg.shape[0]
    T = min(1024, S)
    nj, cur, prev, _, w, b = _conv_specs(S, T)

    def body(hg_ref, hu_ref, hgp_ref, hup_ref, wg_ref, wu_ref, bg_ref, bu_ref, o_ref):
        first = pl.program_id(1) == 0
        g2, g1, g0 = _conv_taps(hg_ref, hgp_ref, first)
        u2, u1, u0 = _conv_taps(hu_ref, hup_ref, first)
        cg = wg_ref[0:1, :] * g2 + wg_ref[1:2, :] * g1 + wg_ref[2:3, :] * g0 + bg_ref[...]
        cu = wu_ref[0:1, :] * u2 + wu_ref[1:2, :] * u1 + wu_ref[2:3, :] * u0 + bu_ref[...]
        o_ref[...] = (_gelu(cg)[0] * cu).astype(BF16)

    return pl.pallas_call(
        body, name="conv_fwd", grid=(nj, S // T), in_specs=[cur, cur, prev, prev, w, w, b, b], out_specs=cur,
        out_shape=jax.ShapeDtypeStruct((S, DFF), BF16), compiler_params=_cparams(("parallel", "parallel")),
    )(hg, hu, hg, hu, wg, wu, bg, bu)


def _conv_bwd_pre(d_act, hg, hu, wg, wu, bg, bu):
    S = hg.shape[0]
    T = min(1024, S)
    nj, cur, prev, _, w, b = _conv_specs(S, T)

    def body(da_ref, hg_ref, hu_ref, hgp_ref, hup_ref, wg_ref, wu_ref, bg_ref, bu_ref,
             dcg_ref, dcu_ref, gwg_ref, gwu_ref, gbg_ref, gbu_ref):
        first = pl.program_id(1) == 0
        g2, g1, g0 = _conv_taps(hg_ref, hgp_ref, first)
        u2, u1, u0 = _conv_taps(hu_ref, hup_ref, first)
        cg = wg_ref[0:1, :] * g2 + wg_ref[1:2, :] * g1 + wg_ref[2:3, :] * g0 + bg_ref[...]
        cu = wu_ref[0:1, :] * u2 + wu_ref[1:2, :] * u1 + wu_ref[2:3, :] * u0 + bu_ref[...]
        da = da_ref[...].astype(F32)
        gl, t = _gelu(cg)
        dcg = da * cu * _gelu_grad(cg, t)
        dcu = da * gl
        dcg_ref[...] = dcg.astype(BF16)
        dcu_ref[...] = dcu.astype(BF16)

        @pl.when(first)
        def _():
            for r in (gwg_ref, gwu_ref, gbg_ref, gbu_ref):
                r[...] = jnp.zeros_like(r)

        for r, d, taps in ((gwg_ref, dcg, (g2, g1, g0)), (gwu_ref, dcu, (u2, u1, u0))):
            for j in range(3):
                r[j:j + 1, :] += jnp.sum(d * taps[j], 0, keepdims=True)
        gbg_ref[...] += jnp.sum(dcg, 0, keepdims=True)
        gbu_ref[...] += jnp.sum(dcu, 0, keepdims=True)

    return pl.pallas_call(
        body, name="conv_bwd_pre", grid=(nj, S // T), in_specs=[cur, cur, cur, prev, prev, w, w, b, b],
        out_specs=[cur, cur, w, w, b, b],
        out_shape=[jax.ShapeDtypeStruct((S, DFF), BF16)] * 2 + [jax.ShapeDtypeStruct((3, DFF), F32)] * 2
        + [jax.ShapeDtypeStruct((1, DFF), F32)] * 2,
        compiler_params=_cparams(("parallel", "arbitrary")),
    )(d_act, hg, hu, hg, hu, wg, wu, bg, bu)


def _conv_bwd_in(dc, w, name):
    S = dc.shape[0]
    T = min(1024, S)
    nj, cur, _, nxt, wspec, _ = _conv_specs(S, T)
    nt = S // T

    def body(dc_ref, dn_ref, w_ref, o_ref):
        d = dc_ref[...].astype(F32)
        dn = dn_ref[...].astype(F32) * jnp.where(pl.program_id(1) == nt - 1, 0.0, 1.0)
        o_ref[...] = (w_ref[2:3, :] * d + w_ref[1:2, :] * _shift_up(d, dn, 1) + w_ref[0:1, :] * _shift_up(d, dn, 2)).astype(BF16)

    return pl.pallas_call(
        body, name=name, grid=(nj, nt), in_specs=[cur, nxt, wspec], out_specs=cur,
        out_shape=jax.ShapeDtypeStruct((S, DFF), BF16), compiler_params=_cparams(("parallel", "parallel")),
    )(dc, dc, w)


def _adam_math(g, w, m, v):
    m = B1 * m + (1.0 - B1) * g
    v = B2 * v + (1.0 - B2) * (g * g)
    m_hat = m / (1.0 - B1 ** STEP)
    v_hat = v / (1.0 - B2 ** STEP)
    return -LR * (m_hat / (jnp.sqrt(v_hat) + EPS) + WD * w), m, v


def _reduce_tail(chip32, far, chip, name, wmv=None):
    L = len(chip32)
    _, R, C = chip32[0].shape
    tr = _tile(R, 256, 16)
    nr = R // tr

    def body(chip_ref, *refs):
        own_refs, far_refs, rest = refs[:L], refs[L:2 * L], refs[2 * L:]
        outs = rest[3:] if wmv else rest
        for ll in range(L):
            @pl.when(pl.program_id(0) == ll)
            def _(ll=ll):
                g = own_refs[ll][...]
                for s in range(3):
                    g = g + far_refs[ll][s].astype(F32)
                outs[0][...] = g
                if wmv:
                    outs[1][...], outs[2][...], outs[3][...] = _adam_math(g, rest[0][...], rest[1][...], rest[2][...])

    def rows(ll):
        return lambda l, i: jnp.where(l == ll, i, jnp.where(l < ll, 0, nr - 1))

    blk = pl.BlockSpec((None, tr, C), lambda l, i, ch: (l, i, 0))
    in_specs = [pl.BlockSpec((None, tr, C), lambda l, i, ch, f=rows(ll): (ch[0], f(l, i), 0)) for ll in range(L)]
    in_specs += [pl.BlockSpec((3, tr, C), lambda l, i, ch, f=rows(ll): (0, f(l, i), 0)) for ll in range(L)]
    args = list(chip32) + list(far)
    n_out = 1
    if wmv:
        in_specs += [blk] * 3
        args += list(wmv)
        n_out = 4
    return pl.pallas_call(
        body, name=name,
        grid_spec=pltpu.PrefetchScalarGridSpec(num_scalar_prefetch=1, grid=(L, nr), in_specs=in_specs, out_specs=[blk] * n_out),
        out_shape=[jax.ShapeDtypeStruct((L, R, C), F32)] * n_out, compiler_params=_cparams(("arbitrary", "arbitrary")),
    )(chip, *args)


def _adamw(g, w, m, v, name):
    R, C = g.shape
    tr = _tile(R, 128, 8)

    def body(g_ref, w_ref, m_ref, v_ref, d_ref, nm_ref, nv_ref):
        d_ref[...], nm_ref[...], nv_ref[...] = _adam_math(g_ref[...], w_ref[...], m_ref[...], v_ref[...])

    blk = pl.BlockSpec((tr, C), lambda i: (i, 0))
    return pl.pallas_call(
        body, name=name, grid=(R // tr,), in_specs=[blk] * 4, out_specs=[blk] * 3,
        out_shape=[jax.ShapeDtypeStruct(g.shape, F32)] * 3, compiler_params=_cparams(("parallel",)),
    )(g, w, m, v)


def _pair_sum(x, recv, core, name):
    _, R, C = x.shape
    tr = _tile(R, 600, 16)

    def body(core_ref, x_ref, r_ref, o32_ref, o16_ref):
        s = x_ref[...] + r_ref[...]
        o32_ref[...] = s
        o16_ref[...] = s.astype(BF16)

    blk = pl.BlockSpec((None, tr, C), lambda q, i, c: (q, i, 0))
    mine = pl.BlockSpec((None, None, tr, C), lambda q, i, c: (q, c[0], i, 0))
    return pl.pallas_call(
        body, name=name,
        grid_spec=pltpu.PrefetchScalarGridSpec(num_scalar_prefetch=1, grid=(4, R // tr), in_specs=[mine, blk], out_specs=[blk, blk]),
        out_shape=[jax.ShapeDtypeStruct((4, R, C), F32), jax.ShapeDtypeStruct((4, R, C), BF16)],
        compiler_params=_cparams(("parallel", "parallel")),
    )(core, x.reshape(4, 2, R, C), recv)


def _sum_slots(x, name):
    def body(x_ref, o_ref):
        g = x_ref[0]
        for s in range(1, x.shape[0]):
            g = g + x_ref[s]
        o_ref[...] = g

    return pl.pallas_call(body, name=name, out_shape=jax.ShapeDtypeStruct(x.shape[1:], F32))(x)


MESH = pl.DeviceIdType.MESH
_HBM = pl.BlockSpec(memory_space=pltpu.HBM)


def _dma_sems(n):
    return pltpu.SemaphoreType.DMA((n,))


def _gather_many(xs, name):
    n = len(xs)

    def body(*refs):
        x_refs, out_refs = refs[:n], refs[n:2 * n]
        send_sems, recv_sems, local_sems = refs[2 * n:]
        ax, ay, ac = lax.axis_index("x"), lax.axis_index("y"), lax.axis_index("c")
        me, sibling = (ax, ay, ac), (ax, ay, 1 - ac)
        chips = [(1 - ax, ay), (ax, 1 - ay), (1 - ax, 1 - ay)]

        def copy(a, k, block, to, own=False):
            slot = out_refs[a].at[4 * block[0] + 2 * block[1] + block[2]]
            return pltpu.make_async_remote_copy(
                src_ref=x_refs[a] if own else slot, dst_ref=slot, send_sem=send_sems.at[7 * a + k],
                recv_sem=recv_sems.at[7 * a + k], device_id=to, device_id_type=MESH)

        mine = [pltpu.make_async_copy(x_refs[a], out_refs[a].at[4 * ax + 2 * ay + ac], local_sems.at[a]) for a in range(n)]
        first = [copy(a, 0, me, sibling, own=True) for a in range(n)]
        first += [copy(a, 1 + j, me, (*chip, ac), own=True) for j, chip in enumerate(chips) for a in range(n)]
        for cp in mine + first:
            cp.start()
        passed = []
        for j, chip in enumerate(chips):
            for a in range(n):
                copy(a, 1 + j, (*chip, ac), me).wait_recv()
                cp = copy(a, 4 + j, (*chip, ac), sibling)
                cp.start()
                passed.append(cp)
        for a in range(n):
            copy(a, 0, sibling, me).wait_recv()
            for j, chip in enumerate(chips):
                copy(a, 4 + j, (*chip, 1 - ac), me).wait_recv()
        for cp in first + passed:
            cp.wait_send()
        for cp in mine:
            cp.wait()

    return pl.pallas_call(
        body, name=name, out_shape=[jax.ShapeDtypeStruct((N_DEV,) + x.shape, x.dtype) for x in xs],
        in_specs=[_HBM] * n, out_specs=[_HBM] * n, scratch_shapes=[_dma_sems(7 * n), _dma_sems(7 * n), _dma_sems(n)],
    )(*xs)


_SEM = pl.BlockSpec(memory_space=pltpu.SEMAPHORE)
_EFFECT = pltpu.SideEffectType.DATAFLOW_SIDE_EFFECTING


def _peer(k):
    ax, ay, ac = lax.axis_index("x"), lax.axis_index("y"), lax.axis_index("c")
    px = 1 - ax if k & 4 else ax
    py = 1 - ay if k & 2 else ay
    pc = 1 - ac if k & 1 else ac
    return (px, py, pc), 4 * px + 2 * py + pc


def _build_gather(x_refs, land_refs, send_sems, recv_sems, waiting):
    _, me = _peer(0)
    copies = []
    for a in range(len(x_refs)):
        for k in range(1, N_DEV):
            peer, slot = _peer(k)
            copies.append(pltpu.make_async_remote_copy(
                src_ref=x_refs[a], dst_ref=land_refs[a].at[slot if waiting else me], send_sem=send_sems.at[7 * a + k - 1],
                recv_sem=recv_sems.at[7 * a + k - 1], device_id=peer, device_id_type=MESH))
    return copies


def _build_cores(x_refs, land_refs, send_sems, recv_sems, waiting):
    ax, ay, ac = lax.axis_index("x"), lax.axis_index("y"), lax.axis_index("c")
    copies = []
    for a in range(len(x_refs)):
        for q in range(4):
            copies.append(pltpu.make_async_remote_copy(
                src_ref=x_refs[a].at[2 * q + 1 - ac], dst_ref=land_refs[a].at[q], send_sem=send_sems.at[4 * a + q],
                recv_sem=recv_sems.at[4 * a + q], device_id=(ax, ay, 1 - ac), device_id_type=MESH))
    return copies


def _build_chips(p_refs, land_refs, send_sems, recv_sems, waiting):
    ax, ay, ac = lax.axis_index("x"), lax.axis_index("y"), lax.axis_index("c")
    copies = []
    for a in range(len(p_refs)):
        for k in range(1, 4):
            px = 1 - ax if k & 2 else ax
            py = 1 - ay if k & 1 else ay
            copies.append(pltpu.make_async_remote_copy(
                src_ref=p_refs[a].at[2 * px + py], dst_ref=land_refs[a].at[k - 1], send_sem=send_sems.at[3 * a + k - 1],
                recv_sem=recv_sems.at[3 * a + k - 1], device_id=(px, py, ac), device_id_type=MESH))
    return copies


_EXCHANGES = {"gather": (_build_gather, 7, N_DEV), "cores": (_build_cores, 4, 4), "chips": (_build_chips, 3, 3)}


def _exchange(kind, xs, name):
    build, per, slots = _EXCHANGES[kind]
    n = len(xs)

    def body(*refs):
        copies = build(refs[:n], refs[n:2 * n], refs[2 * n], refs[2 * n + 1], False)
        for cp in copies:
            cp.start()
        for cp in copies:
            cp.wait_recv()
        for cp in copies:
            cp.wait_send()

    return pl.pallas_call(
        body, name=name, out_shape=[jax.ShapeDtypeStruct((slots,) + x.shape[1:], x.dtype) for x in xs],
        in_specs=[_HBM] * n, out_specs=[_HBM] * n, scratch_shapes=[_dma_sems(per * n), _dma_sems(per * n)],
    )(*xs)


def _exchange_start(kind, xs, lands, name, after=None):
    build, per, _ = _EXCHANGES[kind]
    n = len(xs)

    def body(*refs):
        for cp in build(refs[:n], refs[n:2 * n], refs[-2 * n - 3], refs[-2 * n - 2], False):
            cp.start()
        refs[-1][...] = jnp.zeros_like(refs[-1])

    hbm = lambda t: pltpu.HBM(t.shape, t.dtype)
    args = [pltpu.with_memory_space_constraint(t, pltpu.HBM) for t in list(xs) + list(lands)]
    in_specs = [_HBM] * (2 * n)
    if after is not None:
        args.append(after)
        in_specs.append(pl.BlockSpec(memory_space=pl.ANY))
    outs = pl.pallas_call(
        body, name=name,
        out_shape=(_dma_sems(per * n), _dma_sems(per * n), *[hbm(t) for t in xs], *[hbm(t) for t in lands],
                   jax.ShapeDtypeStruct((8, 128), F32)),
        in_specs=in_specs, out_specs=(_SEM, _SEM, *[_HBM] * (2 * n), pl.BlockSpec(memory_space=pltpu.VMEM)),
        input_output_aliases={a: 2 + a for a in range(2 * n)},
        compiler_params=pltpu.CompilerParams(has_side_effects=_EFFECT),
    )(*args)
    return (kind, outs[0], outs[1], outs[2:2 + n], outs[2 + n:2 + 2 * n]), outs[-1]


def _exchange_wait(flight, after, name):
    kind, send_sems, recv_sems, xs, lands = flight
    build = _EXCHANGES[kind][0]
    n = len(xs)

    def body(*refs):
        for cp in build(refs[:n], refs[n:2 * n], refs[2 * n], refs[2 * n + 1], True):
            cp.wait_send()
            cp.wait_recv()

    hbm = lambda t: pltpu.HBM(t.shape, t.dtype)
    outs = pl.pallas_call(
        body, name=name, out_shape=(*[hbm(t) for t in xs], *[hbm(t) for t in lands]),
        in_specs=[_HBM] * (2 * n) + [_SEM, _SEM, pl.BlockSpec(memory_space=pl.ANY)], out_specs=[_HBM] * (2 * n),
        input_output_aliases={a: a for a in range(2 * n)}, compiler_params=pltpu.CompilerParams(has_side_effects=_EFFECT),
    )(*xs, *lands, send_sems, recv_sems, after)
    return outs[:n], outs[n:]


def _x_view(xb, d):
    return xb if d == 1 else xb.reshape(xb.shape[0] // d, d * xb.shape[1])


def _layer_fwd(x, xb, p, w, cos, sin, rconsts):
    S = x.shape[0]
    proj = _mm(xb, w["win"], tb=True, b_rows=(N_ATT, N_REST), name="mm_proj", out_dtype=BF16)
    qkvs, ogs, lgs = [], [], []
    for g, dil in enumerate(DILATIONS):
        qkv = _qkv_fwd(_x_view(xb, dil), w["win"], g, dil, f"mm_qkv{g}")
        o, l = _attn_fwd(qkv, dil, f"attn_fwd_g{g}")
        qkvs.append(qkv)
        ogs.append(_to_tokens(o, dil))
        lgs.append(_to_tokens(l, dil))
    attn, lse = _rowwise(_f_combine, ogs + lgs, [], [(D, BF16), (HD, F32)], [], name="attn_combine")
    ret_raw, states = _ret_fwd(proj, cos, sin, rconsts)
    rg_win = (proj, RH * RDV, OFF_RG // (RH * RDV))
    ga_win, gr_win = (proj, D, OFF_GA // D), (proj, D, OFF_GR // D)
    (r,) = _rowwise(_f_gn, [ret_raw, rg_win], [w["ret_gn_g"], w["ret_gn_b"]], [(RH * RDV, BF16)], [], name="gn_fwd", tm=256)
    ap = _mm(attn, w["w_attn_proj"], name="mm_attn_proj", out_dtype=BF16)
    rp = _mm(r, w["w_ret_proj"], name="mm_ret_proj", out_dtype=BF16, tk=2048)
    (merged,) = _rowwise(_f_gate, [ap, rp, ga_win, gr_win], [], [(D, BF16)], [], name="gate_fwd")
    mix = _mm(merged, w["w_out"], name="mm_out")
    h1, x1, x1b = _rowwise(_f_ln1, [x, mix], [w["ln1_g"], w["ln1_b"]], [(D, F32), (D, F32), (D, BF16)], [], name="ln1_fwd")
    z = _mm(x1b, w["w_ple_gate"], name="mm_ple_gate")
    pp = _mm(p, w["w_ple_proj"], tb=True, name="mm_ple_proj")
    hg = _mm(x1b, w["w_up"], tb=True, b_rows=(0, DFF), name="mm_up_g", out_dtype=BF16, tm=512, tn=DFF)
    hu = _mm(x1b, w["w_up"], tb=True, b_rows=(DFF, DFF), name="mm_up_u", out_dtype=BF16, tm=512, tn=DFF)
    act = _conv_fwd(hg, hu, w["conv_wg"], w["conv_wu"], w["conv_bg"], w["conv_bu"])
    ffn = _mm(act, w["w_down"], name="mm_down", tm=512, tk=DFF)
    h2, x2, x2b = _rowwise(_f_ln2, [x1, ffn, z, pp], [w["ln2_g"], w["ln2_b"]], [(D, F32), (D, F32), (D, BF16)], [], name="ln2_fwd")
    saved = dict(xb=xb, proj=proj, qkvs=qkvs, attn=attn, lse=lse, ret_raw=ret_raw, states=states, r=r, ap=ap, rp=rp,
                 merged=merged, h1=h1, x1b=x1b, z=z, pp=pp, hg=hg, hu=hu, act=act, h2=h2, p=p)
    return x2, x2b, saved


def _after(fn, token):
    return fn if token is None else (lambda *a: fn(*a[:-1]))


def _layer_bwd(dys, w, sv, cos, sin, rconsts, token=None, midway=None):
    gr = {}
    proj = sv["proj"]
    held = [] if token is None else [token]
    dh2, dh2b, gr["ln2_g"], gr["ln2_b"] = _rowwise(_after(_f_ln_bwd, token), list(dys) + [sv["h2"]], [w["ln2_g"]] + held,
                                                   [(D, F32), (D, BF16)], [(1, D), (1, D)], name="ln2_bwd")
    d_act = _mm(dh2b, w["w_down"], tb=True, name="mm_down_dx", out_dtype=BF16, tm=512, tn=DFF)
    gr["w_down"] = _mm(sv["act"], dh2b, ta=True, name="mm_down_dw", tm=DFF // 2)
    dcg, dcu, gwg, gwu, gbg, gbu = _conv_bwd_pre(d_act, sv["hg"], sv["hu"], w["conv_wg"], w["conv_wu"], w["conv_bg"], w["conv_bu"])
    token = None if midway is None else midway(dcg)
    held = [] if token is None else [token]
    gr["conv_w"] = jnp.concatenate([gwg, gwu], axis=1)
    gr["conv_b"] = jnp.concatenate([gbg, gbu], axis=1)
    dhg = _conv_bwd_in(dcg, w["conv_wg"], "conv_bwd_in_g")
    dhu = _conv_bwd_in(dcu, w["conv_wu"], "conv_bwd_in_u")
    gw_up = _mm(dhg, sv["x1b"], ta=True, name="mm_up_g_dw", tm=DFF // 2, out_rows=(0, 2 * DFF))
    gr["w_up"] = _mm(dhu, sv["x1b"], ta=True, name="mm_up_u_dw", tm=DFF // 2, out_rows=(DFF, 2 * DFF), into=gw_up)
    dx1 = _mm(dhg, w["w_up"], b_rows=(0, DFF), name="mm_up_g_dx", add=dh2, add_scale=ALPHA, tm=512, tk=DFF)
    dx1 = _mm(dhu, w["w_up"], b_rows=(DFF, DFF), name="mm_up_u_dx", add=dx1, tm=512, tk=DFF)
    dpp, dz = _rowwise(_f_ple_bwd, [dh2, sv["z"], sv["pp"]], [], [(D, BF16), (D, BF16)], [], name="ple_bwd")
    gr["w_ple_proj"] = _mm(dpp, sv["p"], ta=True, name="mm_ple_proj_dw")
    gr["w_ple_gate"] = _mm(sv["x1b"], dz, ta=True, name="mm_ple_gate_dw")
    dx1 = _mm(dz, w["w_ple_gate"], tb=True, name="mm_ple_gate_dx", add=dx1)
    dh1, dh1b, gr["ln1_g"], gr["ln1_b"] = _rowwise(_after(_f_ln_bwd, token), [dx1, sv["h1"]], [w["ln1_g"]] + held,
                                                   [(D, F32), (D, BF16)], [(1, D), (1, D)], name="ln1_bwd")
    d_merged = _mm(dh1b, w["w_out"], tb=True, name="mm_out_dx", out_dtype=BF16)
    gr["w_out"] = _mm(sv["merged"], dh1b, ta=True, name="mm_out_dw")
    rg_win = (proj, RH * RDV, OFF_RG // (RH * RDV))
    ga_win, gr_win = (proj, D, OFF_GA // D), (proj, D, OFF_GR // D)
    dap, drp, d_rest = _rowwise(_f_gate_bwd, [d_merged, sv["ap"], sv["rp"], ga_win, gr_win], [],
                                [(D, BF16), (D, BF16), (2 * D, BF16, N_REST, OFF_GA // (2 * D), None)], [], name="gate_bwd")
    d_attn = _mm(dap, w["w_attn_proj"], tb=True, name="mm_attn_proj_dx", out_dtype=BF16)
    gr["w_attn_proj"] = _mm(sv["attn"], dap, ta=True, name="mm_attn_proj_dw")
    d_r = _mm(drp, w["w_ret_proj"], tb=True, name="mm_ret_proj_dx", out_dtype=BF16, tn=2048)
    gr["w_ret_proj"] = _mm(sv["r"], drp, ta=True, name="mm_ret_proj_dw", tm=2048)
    d_ret, d_rest, gr["ret_gn_g"], gr["ret_gn_b"] = _rowwise(
        _f_gn_bwd, [d_r, sv["ret_raw"], rg_win], [w["ret_gn_g"], w["ret_gn_b"]],
        [(RH * RDV, BF16), (RH * RDV, BF16, N_REST, OFF_RG // (RH * RDV), d_rest)],
        [(1, RH * RDV), (1, RH * RDV)], name="gn_bwd", tm=256)
    d_rest = _ret_bwd(proj, cos, sin, rconsts, sv["states"], d_ret, d_rest)
    dx0 = _mm(d_rest, w["win"], b_rows=(N_ATT, N_REST), name="mm_proj_dx", add=dh1, add_scale=ALPHA)
    (delta,) = _rowwise(_f_delta, [d_attn, sv["attn"]], [], [(HD, F32)], [], name="attn_delta")
    gw_in, dx_parts = None, []
    for g, dil in enumerate(DILATIONS):
        dqkv = _attn_bwd(sv["qkvs"][g], _to_head_residues(d_attn, dil), _to_residues(sv["lse"], dil), _to_residues(delta, dil),
                         dil, f"attn_bwd_g{g}")
        gw_in = _qkv_dw(dqkv, _x_view(sv["xb"], dil), g, dil, f"mm_qkv{g}_dw", into=gw_in)
        if dil == 1:
            dx0 = _qkv_dx(dqkv, w["win"], g, dil, f"mm_qkv{g}_dx", F32, add=dx0)
        else:
            dx_parts.append(_qkv_dx(dqkv, w["win"], g, dil, f"mm_qkv{g}_dx", BF16).reshape(dx0.shape))
    gw_in = _mm(d_rest, sv["xb"], ta=True, name="mm_proj_dw", out_rows=(N_ATT, N_IN), into=gw_in, blocks8=True)
    gr["w_in"] = gw_in.reshape(N_IN, D)
    return [dx0] + dx_parts, gr


def _local_step(x, p, positions, target, ws, on_grads=None):
    half = RDK // 2
    freq = jnp.power(ROPE_BASE, -jnp.arange(half, dtype=F32) / half)
    ang = positions.astype(F32)[:, None] * freq[None, :]
    cos, sin = jnp.cos(ang), jnp.sin(ang)
    rconsts = _ret_consts()
    xb = x.astype(BF16)
    saved, ws = [], list(ws)
    for l in range(DEPTH):
        if callable(ws[l]):
            ws[l] = ws[l](x)
        x, xb, sv = _layer_fwd(x, xb, p[l], ws[l], cos, sin, rconsts)
        saved.append(sv)
    dy, loss_vec = _rowwise(_f_loss, [x, target], [], [(D, F32)], [(1, D)], name="loss")
    dys, grads = [dy], [None] * DEPTH
    token = midway = None
    for l in reversed(range(DEPTH)):
        dys, grads[l] = _layer_bwd(dys, ws[l], saved[l], cos, sin, rconsts, token, midway)
        token, midway = on_grads(l, grads[l]) if on_grads else (None, None)
    (grad_x,) = _rowwise(_f_sum, dys, [], [(D, F32)], [], name="grad_x_sum")
    return loss_vec, grad_x, grads


def _pack_rows(arrs):
    parts, where, off = [], [], 0
    for t in arrs:
        t = t.reshape(-1, D)
        rows = t.shape[0]
        padded = -(-rows // 8) * 8
        parts.append(jnp.pad(t, ((0, padded - rows), (0, 0))))
        where.append((off, rows))
        off += padded
    return jnp.concatenate(parts, axis=0), where


def _layer_weights(g, l, conv_w_all, conv_b, W):
    w = dict(win=g["w_in"].reshape(N_IN, D), w_up=g["w_up"].reshape(2 * DFF, D),
             w_ple_proj=g["w_ple_proj"].reshape(D, PLE), w_attn_proj=g["w_attn_proj"].reshape(D, D),
             w_ret_proj=g["w_ret_proj"].reshape(RH * RDV, D), w_out=g["w_out"].reshape(D, D),
             w_down=g["w_down"].reshape(DFF, D), w_ple_gate=g["w_ple_gate"].reshape(D, D))
    w["conv_wg"], w["conv_wu"] = conv_w_all[l][:, :DFF], conv_w_all[l][:, DFF:]
    w["conv_bg"], w["conv_bu"] = conv_b[l][None, :DFF], conv_b[l][None, DFF:]
    for n in ("ret_gn_g", "ret_gn_b", "ln1_g", "ln1_b", "ln2_g", "ln2_b"):
        w[n] = W[n][l][None, :]
    return w


def kernel(x, p, positions, w_in, w_attn_proj, w_ret_proj, ret_gn_g, ret_gn_b, w_out, ln1_g, ln1_b, w_up, conv_w, conv_b, w_down, w_ple_gate, w_ple_proj, ln2_g, ln2_b, loss_target, m_w_in, m_w_attn_proj, m_w_ret_proj, m_ret_gn_g, m_ret_gn_b, m_w_out, m_ln1_g, m_ln1_b, m_w_up, m_conv_w, m_conv_b, m_w_down, m_w_ple_gate, m_w_ple_proj, m_ln2_g, m_ln2_b, v_w_in, v_w_attn_proj, v_w_ret_proj, v_ret_gn_g, v_ret_gn_b, v_w_out, v_ln1_g, v_ln1_b, v_w_up, v_conv_w, v_conv_b, v_w_down, v_w_ple_gate, v_w_ple_proj, v_ln2_g, v_ln2_b):
    W = dict(w_in=w_in, w_attn_proj=w_attn_proj, w_ret_proj=w_ret_proj, ret_gn_g=ret_gn_g, ret_gn_b=ret_gn_b, w_out=w_out,
             ln1_g=ln1_g, ln1_b=ln1_b, w_up=w_up, conv_w=conv_w, conv_b=conv_b, w_down=w_down, w_ple_gate=w_ple_gate,
             w_ple_proj=w_ple_proj, ln2_g=ln2_g, ln2_b=ln2_b)
    M = dict(w_in=m_w_in, w_attn_proj=m_w_attn_proj, w_ret_proj=m_w_ret_proj, ret_gn_g=m_ret_gn_g, ret_gn_b=m_ret_gn_b,
             w_out=m_w_out, ln1_g=m_ln1_g, ln1_b=m_ln1_b, w_up=m_w_up, conv_w=m_conv_w, conv_b=m_conv_b, w_down=m_w_down,
             w_ple_gate=m_w_ple_gate, w_ple_proj=m_w_ple_proj, ln2_g=m_ln2_g, ln2_b=m_ln2_b)
    V = dict(w_in=v_w_in, w_attn_proj=v_w_attn_proj, w_ret_proj=v_w_ret_proj, ret_gn_g=v_ret_gn_g, ret_gn_b=v_ret_gn_b,
             w_out=v_w_out, ln1_g=v_ln1_g, ln1_b=v_ln1_b, w_up=v_w_up, conv_w=v_conv_w, conv_b=v_conv_b, w_down=v_w_down,
             w_ple_gate=v_w_ple_gate, w_ple_proj=v_w_ple_proj, ln2_g=v_ln2_g, ln2_b=v_ln2_b)

    me = 4 * lax.axis_index("x") + 2 * lax.axis_index("y") + lax.axis_index("c")
    shards = [[(W[n][l].T if n in COL_SHARDED else W[n][l]).astype(BF16) for n in BIG] for l in range(DEPTH)]
    outs = _gather_many(shards[0] + [conv_w], "gather_weights_l0")
    conv_w_all = outs[-1].transpose(1, 2, 0, 3).reshape(DEPTH, 3, 2 * DFF)
    lands = [lax.dynamic_update_index_in_dim(lax.empty((N_DEV,) + t.shape, t.dtype), t, me, 0) for t in shards[1]]
    flight, token = _exchange_start("gather", shards[1], lands, "gather_weights_l1_start", after=outs[0])

    def second_layer(after):
        _, got = _exchange_wait(flight, after, "gather_weights_l1_wait")
        return _layer_weights(dict(zip(BIG, got)), 1, conv_w_all, conv_b, W)

    core = lax.axis_index("c").astype(jnp.int32).reshape(1)
    chip = (2 * lax.axis_index("x") + lax.axis_index("y")).astype(jnp.int32).reshape(1)
    shard_major = lambda g: [g[n].reshape((N_DEV, -1) + g[n].shape[1:]) for n in BIG]
    pair_sums = lambda mine, theirs, tag: [_pair_sum(a, b, core, f"pair_sum_{tag}_{n}") for a, b, n in zip(mine, theirs, BIG)]
    payload = lambda sums: [s[0 if n in F32_OVER_ICI else 1] for s, n in zip(sums, BIG)]
    empty_like = lambda ts, slots: [lax.empty((slots,) + t.shape[1:], t.dtype) for t in ts]
    chip32, far = [None] * DEPTH, [None] * DEPTH
    pending = []

    def on_grads(l, g):
        if l != DEPTH - 1 or DEPTH == 1:
            return None, None
        mine = shard_major(g)
        flight_a, tok_a = _exchange_start("cores", mine, empty_like(mine, 4), f"exchange_cores_l{l}_start")

        def midway(after):
            mine_back, theirs = _exchange_wait(flight_a, after, f"exchange_cores_l{l}_wait")
            sums = pair_sums(mine_back, theirs, f"l{l}")
            chip32[l] = [s[0] for s in sums]
            flight_b, tok_b = _exchange_start("chips", payload(sums), empty_like(payload(sums), 3), f"exchange_chips_l{l}_start")
            pending.append((l, flight_b))
            return tok_b

        return tok_a, midway

    ws = [_layer_weights(dict(zip(BIG, outs)), 0, conv_w_all, conv_b, W), second_layer]
    loss_vec, grad_x, grads = _local_step(x[0] + token[0, 0], p[:, 0], positions[0], loss_target[0], ws, on_grads)
    loss = lax.psum(jnp.sum(loss_vec), ("x", "y", "c"))
    for l, flight_b in pending:
        _, far[l] = _exchange_wait(flight_b, grad_x, f"exchange_chips_l{l}_wait")
    for l in range(DEPTH):
        if far[l] is None:
            mine = shard_major(grads[l])
            sums = pair_sums(mine, _exchange("cores", mine, f"exchange_cores_l{l}"), f"l{l}")
            chip32[l] = [s[0] for s in sums]
            far[l] = _exchange("chips", payload(sums), f"exchange_chips_l{l}")
    G, DW, NM, NV = ({} for _ in range(4))
    for a, n in enumerate(BIG):
        chip32_n = [chip32[l][a] for l in range(DEPTH)]
        far_n = [far[l][a] for l in range(DEPTH)]
        if n in COL_SHARDED:
            G[n] = _reduce_tail(chip32_n, far_n, chip, f"reduced_{n}")[0].transpose(0, 2, 1)
            R2, C2 = DEPTH * W[n].shape[1], W[n].shape[2]
            res = _adamw(*(t.reshape(R2, C2) for t in (G[n], W[n], M[n], V[n])), f"adamw_{n}")
            DW[n], NM[n], NV[n] = (t.reshape(W[n].shape) for t in res)
        else:
            G[n], DW[n], NM[n], NV[n] = _reduce_tail(chip32_n, far_n, chip, f"adamw_{n}", wmv=(W[n], M[n], V[n]))

    small_names = SMALL + ("conv_w",)
    g_small, where = _pack_rows([jnp.stack([grads[l][n] for l in range(DEPTH)]) for n in small_names])
    (g_all,) = _gather_many([g_small], "gather_small_grads")
    g_small = _sum_slots(g_all, "sum_small_grads")
    for n, (off, rows) in zip(SMALL, where):
        G[n] = g_small[off:off + rows].reshape(W[n].shape)
    off, rows = where[-1]
    g_cw = g_small[off:off + rows].reshape(DEPTH, 3, N_DEV, conv_w.shape[2])
    G["conv_w"] = lax.dynamic_index_in_dim(g_cw, me, axis=2, keepdims=False)
    packed = [_pack_rows([d[n] for n in SMALL]) for d in (G, W, M, V)]
    small_out = _adamw(*(t for t, _ in packed), "adamw_small")
    for res, dst in zip(small_out, (DW, NM, NV)):
        for n, (off, rows) in zip(SMALL, packed[0][1]):
            dst[n] = res[off:off + rows].reshape(W[n].shape)
    two_d = lambda t: t.reshape(DEPTH * 3, conv_w.shape[2])
    cw_out = _adamw(two_d(G["conv_w"]), two_d(conv_w), two_d(m_conv_w), two_d(v_conv_w), "adamw_conv_w")
    for res, dst in zip(cw_out, (DW, NM, NV)):
        dst["conv_w"] = res.reshape(conv_w.shape)

    return (loss, grad_x[None], *[G[n] for n in WEIGHTS], *[DW[n] for n in WEIGHTS], *[NM[n] for n in WEIGHTS],
            *[NV[n] for n in WEIGHTS])
```

```python
import math

import numpy as np
import jax
import jax.numpy as jnp
from jax import lax
from jax.experimental import pallas as pl
from jax.experimental.pallas import tpu as pltpu

F32, BF16 = jnp.float32, jnp.bfloat16

D = 1024
DEPTH = 2
N_DEV = 8
HD = 128
NH = 8
DILATIONS = (1, 4, 16)
SPAN = 128
N_ATT = 3 * 3 * NH * HD
RH, RDK, RDV = 4, 256, 512
CH = 128
DFF = 2816
PLE = 256
N_IN = 17408
N_REST = N_IN - N_ATT
OFF_RQ, OFF_RK, OFF_RV, OFF_RG, OFF_GA, OFF_GR = 0, 1024, 2048, 4096, 6144, 7168
ALPHA = (2 * DEPTH) ** 0.25
LN_EPS, GN_EPS = 1e-5, 1e-6
ROPE_BASE = 10000.0
LR, B1, B2, EPS, WD, STEP = 0.001, 0.9, 0.999, 1e-8, 0.01, 10
VMEM_LIMIT = 48 * 1024 * 1024
NEG = -1e30

BIG = ("w_in", "w_attn_proj", "w_ret_proj", "w_out", "w_up", "w_down", "w_ple_gate", "w_ple_proj")
COL_SHARDED = ("w_in", "w_up", "w_ple_proj")
F32_OVER_ICI = ("w_attn_proj", "w_out", "w_ple_gate", "w_ple_proj")
SMALL = ("ret_gn_g", "ret_gn_b", "ln1_g", "ln1_b", "conv_b", "ln2_g", "ln2_b")
WEIGHTS = ("w_in", "w_attn_proj", "w_ret_proj", "ret_gn_g", "ret_gn_b", "w_out", "ln1_g", "ln1_b", "w_up",
           "conv_w", "conv_b", "w_down", "w_ple_gate", "w_ple_proj", "ln2_g", "ln2_b")


def _tile(n, cap, mult=128):
    if n <= cap:
        return n
    t = (cap // mult) * mult
    while n % t:
        t -= mult
    return t


def _cparams(sem):
    return pltpu.CompilerParams(dimension_semantics=sem, vmem_limit_bytes=VMEM_LIMIT)


def _dot(a, b, ca, cb):
    return lax.dot_general(a, b, (((ca,), (cb,)), ((), ())), preferred_element_type=F32)


def _bdot(a, b, ca, cb):
    return lax.dot_general(a, b, (((ca,), (cb,)), ((0,), (0,))), preferred_element_type=F32)


def _mm(a, b, *, name, ta=False, tb=False, out_dtype=F32, add=None, add_scale=1.0, tm=1024, tn=1024, tk=1024,
        b_rows=None, out_rows=None, into=None, blocks8=False, after=None):
    M, K = (a.shape[1], a.shape[0]) if ta else a.shape
    b_first, b_count = b_rows if b_rows else (0, b.shape[0])
    N = b_count if tb else b.shape[1]
    assert K == (b.shape[1] if tb else b_count)
    tm, tn, tk = _tile(M, tm), _tile(N, tn), _tile(K, tk)
    nk = K // tk
    o_first, o_total = out_rows if out_rows else (0, M)
    jb, kb, io = (b_first // tn, 0, o_first // tm) if tb else (0, b_first // tk, o_first // tm)
    assert b_first % (tn if tb else tk) == 0 and o_first % tm == 0 and (add is None or out_rows is None)

    def body(*refs):
        if add is None:
            a_ref, b_ref = refs[:2]
        else:
            a_ref, b_ref, add_ref = refs[:3]
        o_ref, acc_ref = refs[-2:]
        k = pl.program_id(2)

        @pl.when(k == 0)
        def _():
            acc_ref[...] = jnp.zeros_like(acc_ref)

        acc_ref[...] += _dot(a_ref[...].astype(BF16), b_ref[...].astype(BF16), 0 if ta else 1, 1 if tb else 0)

        @pl.when(k == nk - 1)
        def _():
            r = acc_ref[...]
            if add is not None:
                r = r + add_scale * add_ref[...].astype(F32)
            o_ref[...] = r.astype(out_dtype).reshape(o_ref.shape)

    a_spec = pl.BlockSpec((tk, tm), lambda i, j, k: (k, i)) if ta else pl.BlockSpec((tm, tk), lambda i, j, k: (i, k))
    if tb:
        b_spec = pl.BlockSpec((tn, tk), lambda i, j, k: (j + jb, k))
    else:
        b_spec = pl.BlockSpec((tk, tn), lambda i, j, k: (k + kb, j))
    if blocks8:
        assert tm == 1024
        o_spec = pl.BlockSpec((1, 8, 128, tn), lambda i, j, k: (i + io, 0, 0, j))
        o_shape = (o_total // tm, 8, 128, N)
    else:
        o_spec = pl.BlockSpec((tm, tn), lambda i, j, k: (i + io, j))
        o_shape = (o_total, N)
    in_specs, args, aliases = [a_spec, b_spec], [a, b], {}
    if add is not None:
        in_specs.append(o_spec)
        args.append(add)
    if after is not None:
        in_specs.append(pl.BlockSpec(memory_space=pl.ANY))
        args.append(after)
    if into is not None:
        aliases = {len(args): 0}
        in_specs.append(pl.BlockSpec(memory_space=pl.ANY))
        args.append(into)
    return pl.pallas_call(
        body, name=name, grid=(M // tm, N // tn, nk), in_specs=in_specs, out_specs=o_spec,
        out_shape=jax.ShapeDtypeStruct(o_shape, out_dtype), scratch_shapes=[pltpu.VMEM((tm, tn), F32)],
        input_output_aliases=aliases, compiler_params=_cparams(("parallel", "parallel", "arbitrary")),
    )(*args)


def _rowwise(fn, rows, pars, outs, accs, *, name, tm=512):
    first = rows[0][0] if isinstance(rows[0], tuple) else rows[0]
    S = first.shape[-2]
    tm = _tile(S, tm, 16)
    n_r, n_p, n_o = len(rows), len(pars), len(outs)
    outs = [o if len(o) == 5 else (o[0], o[1], o[0], 0, None) for o in outs]
    intos = [(k, o[4]) for k, o in enumerate(outs) if o[4] is not None]
    n_i = len(intos)

    def body(*refs):
        i = pl.program_id(0)
        vals = [r[...] for r in refs[:n_r + n_p]]
        res = fn(*vals)
        if not isinstance(res, (tuple, list)):
            res = (res,)
        o_refs = refs[n_r + n_p + n_i:n_r + n_p + n_i + n_o]
        a_refs = refs[n_r + n_p + n_i + n_o:]
        for r, v in zip(o_refs, res[:n_o]):
            r[...] = v.astype(r.dtype)
        if a_refs:
            @pl.when(i == 0)
            def _():
                for r in a_refs:
                    r[...] = jnp.zeros_like(r)

            for r, v in zip(a_refs, res[n_o:]):
                r[...] += v

    in_specs, args = [], []
    for r in rows:
        if isinstance(r, tuple):
            arr, w, cb = r
            in_specs.append(pl.BlockSpec((tm, w), lambda i, cb=cb: (i, cb)))
        elif r.ndim == 3:
            arr = r
            in_specs.append(pl.BlockSpec((arr.shape[0], tm, arr.shape[2]), lambda i: (0, i, 0)))
        else:
            arr = r
            in_specs.append(pl.BlockSpec((tm, arr.shape[1]), lambda i: (i, 0)))
        args.append(arr)
    for p_ in pars:
        in_specs.append(pl.BlockSpec(p_.shape, lambda i: (0, 0)))
        args.append(p_)
    aliases = {}
    for k, arr in intos:
        aliases[len(args)] = k
        in_specs.append(pl.BlockSpec(memory_space=pl.ANY))
        args.append(arr)
    out_shape = [jax.ShapeDtypeStruct((S, o[2]), o[1]) for o in outs] + [jax.ShapeDtypeStruct(a, F32) for a in accs]
    out_specs = [pl.BlockSpec((tm, o[0]), lambda i, cb=o[3]: (i, cb)) for o in outs] + [pl.BlockSpec(a, lambda i: (0, 0)) for a in accs]
    return pl.pallas_call(
        body, name=name, grid=(S // tm,), in_specs=in_specs, out_specs=out_specs, out_shape=out_shape,
        input_output_aliases=aliases, compiler_params=_cparams(("arbitrary",) if accs else ("parallel",)),
    )(*args)


def _norm(h, eps):
    mu = jnp.mean(h, -1, keepdims=True)
    d = h - mu
    rstd = lax.rsqrt(jnp.mean(d * d, -1, keepdims=True) + eps)
    return d * rstd, rstd


def _norm_bwd(dxh, xh, rstd):
    return rstd * (dxh - jnp.mean(dxh, -1, keepdims=True) - xh * jnp.mean(dxh * xh, -1, keepdims=True))


def _sig(x):
    return 1.0 / (1.0 + jnp.exp(-x))


_GELU_C = math.sqrt(2.0 / math.pi)


def _gelu(x):
    t = jnp.tanh(_GELU_C * (x + 0.044715 * x * x * x))
    return 0.5 * x * (1.0 + t), t


def _gelu_grad(x, t):
    return 0.5 * (1.0 + t) + 0.5 * x * (1.0 - t * t) * _GELU_C * (1.0 + 3 * 0.044715 * x * x)


def _f_ln1(x, mix, g, b):
    h = ALPHA * x + mix
    xh, _ = _norm(h, LN_EPS)
    y = xh * g + b
    return h, y, y


def _f_ln2(x, ffn, z, pp, g, b):
    h = ALPHA * x + ffn + _sig(z) * pp
    xh, _ = _norm(h, LN_EPS)
    y = xh * g + b
    return h, y, y


def _f_ln_bwd(*args):
    *dys, h, g = args
    dy = dys[0]
    for t in dys[1:]:
        dy = dy + t
    xh, rstd = _norm(h, LN_EPS)
    dh = _norm_bwd(dy * g, xh, rstd)
    return dh, dh, jnp.sum(dy * xh, 0, keepdims=True), jnp.sum(dy, 0, keepdims=True)


def _f_sum(*ts):
    r = ts[0]
    for t in ts[1:]:
        r = r + t
    return r


def _f_loss(y, t):
    e = y - t
    return e * (1.0 / D), jnp.sum(e * e, 0, keepdims=True) * (0.5 / D)


def _head_col(c, h):
    lane = lax.broadcasted_iota(jnp.int32, c.shape, 1)
    return jnp.sum(jnp.where(lane == h, c, 0.0), -1, keepdims=True)


def _f_combine(o0, o1, o2, l0, l1, l2):
    lane = lax.broadcasted_iota(jnp.int32, l0.shape, 1)
    parts, lse = [], jnp.zeros(l0.shape, F32)
    for h in range(NH):
        a0, a1, a2 = _head_col(l0, h), _head_col(l1, h), _head_col(l2, h)
        m = jnp.maximum(jnp.maximum(a0, a1), a2)
        e0, e1, e2 = jnp.exp(a0 - m), jnp.exp(a1 - m), jnp.exp(a2 - m)
        den = e0 + e1 + e2
        parts.append((e0 * o0[h].astype(F32) + e1 * o1[h].astype(F32) + e2 * o2[h].astype(F32)) / den)
        lse = jnp.where(lane == h, m + jnp.log(den), lse)
    return jnp.concatenate(parts, axis=1), lse


def _f_delta(da, a):
    lane = lax.broadcasted_iota(jnp.int32, (da.shape[0], HD), 1)
    out = jnp.zeros((da.shape[0], HD), F32)
    for h in range(NH):
        sl = slice(h * HD, (h + 1) * HD)
        s = jnp.sum(da[:, sl].astype(F32) * a[:, sl].astype(F32), -1, keepdims=True)
        out = jnp.where(lane == h, s, out)
    return out


def _f_gate(ap, rp, ga, gr):
    return _sig(ga.astype(F32)) * ap.astype(F32) + _sig(gr.astype(F32)) * rp.astype(F32)


def _f_gate_bwd(dm, ap, rp, ga, gr):
    dm = dm.astype(F32)
    sa, sr = _sig(ga.astype(F32)), _sig(gr.astype(F32))
    dga, dgr = dm * ap.astype(F32) * sa * (1.0 - sa), dm * rp.astype(F32) * sr * (1.0 - sr)
    return dm * sa, dm * sr, jnp.concatenate([dga, dgr], axis=1)


def _f_gn(y, rg, g, b):
    y, rg = y.astype(F32), rg.astype(F32)
    parts = []
    for h in range(RH):
        sl = slice(h * RDV, (h + 1) * RDV)
        xh, _ = _norm(y[:, sl], GN_EPS)
        parts.append(xh * g[:, sl] + b[:, sl])
    return rg * _sig(rg) * jnp.concatenate(parts, axis=1)


def _f_gn_bwd(dr, y, rg, g, b):
    dr, y, rg = dr.astype(F32), y.astype(F32), rg.astype(F32)
    s = _sig(rg)
    d_out = dr * rg * s
    dys, outs, xhs = [], [], []
    for h in range(RH):
        sl = slice(h * RDV, (h + 1) * RDV)
        xh, rstd = _norm(y[:, sl], GN_EPS)
        xhs.append(xh)
        outs.append(xh * g[:, sl] + b[:, sl])
        dys.append(_norm_bwd(d_out[:, sl] * g[:, sl], xh, rstd))
    xh, out = jnp.concatenate(xhs, axis=1), jnp.concatenate(outs, axis=1)
    d_rg = dr * out * s * (1.0 + rg * (1.0 - s))
    return jnp.concatenate(dys, axis=1), d_rg, jnp.sum(d_out * xh, 0, keepdims=True), jnp.sum(d_out, 0, keepdims=True)


def _f_ple_bwd(dh, z, pp):
    s = _sig(z)
    return dh * s, dh * pp * s * (1.0 - s)


QKV = 3 * HD


def _to_tokens(t, d):
    if d == 1:
        return t
    *lead, S, C = t.shape
    n = len(lead)
    perm = tuple(range(n)) + (n + 1, n, n + 2)
    return t.reshape(*lead, d, S // d, C).transpose(perm).reshape(*lead, S, C)


def _to_residues(t, d):
    if d == 1:
        return t
    S, C = t.shape
    return t.reshape(S // d, d, C).transpose(1, 0, 2).reshape(S, C)


def _to_head_residues(t, d):
    S = t.shape[0]
    return t.reshape(S // d, d, NH, HD).transpose(2, 1, 0, 3).reshape(NH, S, HD)


def _w_qkv_specs(g):
    return [pl.BlockSpec((D, D), lambda *i, t=t: (3 * g + t, 0)) for t in range(3)]


def _qkv_fwd(xv, win, g, dil, name):
    Sd = xv.shape[0]
    S = Sd * dil
    tm = min(512, Sd)
    nma = Sd // tm

    def body(a_ref, wq_ref, wk_ref, wv_ref, o_ref):
        a = a_ref[...]
        q, k, v = (_dot(a, w_ref[...], 1, 1).astype(BF16) for w_ref in (wq_ref, wk_ref, wv_ref))
        for h in range(NH):
            sl = slice(h * HD, (h + 1) * HD)
            o_ref[h] = jnp.concatenate([q[:, sl], k[:, sl], v[:, sl]], axis=1)

    return pl.pallas_call(
        body, name=name, grid=(S // tm,),
        in_specs=[pl.BlockSpec((tm, D), lambda i: (i % nma, i // nma))] + _w_qkv_specs(g),
        out_specs=pl.BlockSpec((NH, tm, QKV), lambda i: (0, i, 0)), out_shape=jax.ShapeDtypeStruct((NH, S, QKV), BF16),
        compiler_params=_cparams(("parallel",)),
    )(xv, win, win, win)


def _qkv_dx(dqkv, win, g, dil, name, out_dtype, add=None, after=None):
    S = dqkv.shape[1]
    Sd = S // dil
    tm = min(512, Sd)
    nmo = Sd // tm

    def body(*refs):
        a_ref, wq_ref, wk_ref, wv_ref = refs[:4]
        o_ref = refs[-1]
        acc = None
        for h in range(NH):
            sl = slice(h * HD, (h + 1) * HD)
            w = jnp.concatenate([wq_ref[sl, :], wk_ref[sl, :], wv_ref[sl, :]], axis=0)
            part = _dot(a_ref[h], w, 1, 0)
            acc = part if acc is None else acc + part
        if add is not None:
            acc = acc + refs[4][...]
        o_ref[...] = acc.astype(out_dtype)

    o_spec = pl.BlockSpec((tm, D), lambda i: (i % nmo, i // nmo))
    in_specs = [pl.BlockSpec((NH, tm, QKV), lambda i: (0, i, 0))] + _w_qkv_specs(g)
    args = [dqkv, win, win, win]
    if add is not None:
        assert dil == 1
        in_specs.append(o_spec)
        args.append(add)
    if after is not None:
        in_specs.append(pl.BlockSpec(memory_space=pl.ANY))
        args.append(after)
    return pl.pallas_call(
        body, name=name, grid=(S // tm,), in_specs=in_specs, out_specs=o_spec,
        out_shape=jax.ShapeDtypeStruct((Sd, dil * D), out_dtype), compiler_params=_cparams(("parallel",)),
    )(*args)


GW_IN_BLOCKS = (N_IN // D, NH, HD, D)


def _qkv_dw(dqkv, xv, g, dil, name, into=None):
    S = dqkv.shape[1]
    Sd = S // dil
    tk = min(1024, Sd)
    nkb, nk = Sd // tk, S // tk
    hh = NH // 2

    def body(*refs):
        a_ref, b_ref = refs[:2]
        o_ref, acc_ref = refs[-2:]
        k = pl.program_id(1)

        @pl.when(k == 0)
        def _():
            acc_ref[...] = jnp.zeros_like(acc_ref)

        b = b_ref[...]
        for h in range(hh):
            acc_ref[h * QKV:(h + 1) * QKV, :] += _dot(a_ref[h], b, 0, 0)

        @pl.when(k == nk - 1)
        def _():
            for h in range(hh):
                for t in range(3):
                    o_ref[t, h] = acc_ref[h * QKV + t * HD:h * QKV + (t + 1) * HD, :]

    in_specs = [pl.BlockSpec((hh, tk, QKV), lambda j, k: (j, k, 0)), pl.BlockSpec((tk, D), lambda j, k: (k % nkb, k // nkb))]
    args, aliases = [dqkv, xv], {}
    if into is not None:
        aliases = {2: 0}
        in_specs.append(pl.BlockSpec(memory_space=pl.ANY))
        args.append(into)
    return pl.pallas_call(
        body, name=name, grid=(2, nk), in_specs=in_specs,
        out_specs=pl.BlockSpec((3, hh, HD, D), lambda j, k: (g, j, 0, 0)), out_shape=jax.ShapeDtypeStruct(GW_IN_BLOCKS, F32),
        input_output_aliases=aliases, scratch_shapes=[pltpu.VMEM((hh * QKV, D), F32)],
        compiler_params=_cparams(("parallel", "arbitrary")),
    )(*args)


def _band(nb, first_valid, last_valid=None):
    b = lax.broadcasted_iota(jnp.int32, (nb, SPAN, SPAN), 0)
    row = lax.broadcasted_iota(jnp.int32, (nb, SPAN, SPAN), 1)
    col = lax.broadcasted_iota(jnp.int32, (nb, SPAN, SPAN), 2)
    off = jnp.where(b == 0, jnp.where(first_valid, 0, 2 * SPAN), 0)
    if last_valid is not None:
        off = off + jnp.where(b == nb - 1, jnp.where(last_valid, 0, 2 * SPAN), 0)
    return col <= row, col >= row + off


def _attn_tiles(S, dil):
    Sd = S // dil
    T = min(1024, Sd)
    return Sd, T, T // SPAN, Sd // T


def _attn_fwd(qkv, dil, name):
    S = qkv.shape[1]
    Sd, T, nsub, nib = _attn_tiles(S, dil)
    scale = HD ** -0.5

    def body(c_ref, p_ref, o_ref, l_ref):
        ib, h = pl.program_id(1), pl.program_id(2)
        blk, hal = c_ref[...], p_ref[...]
        q, k, v = blk[:, :HD], blk[:, HD:2 * HD], blk[:, 2 * HD:]
        if nsub > 1:
            kp = jnp.concatenate([hal[:, HD:2 * HD], k[:T - SPAN]], axis=0)
            vp = jnp.concatenate([hal[:, 2 * HD:], v[:T - SPAN]], axis=0)
        else:
            kp, vp = hal[:, HD:2 * HD], hal[:, 2 * HD:]
        q3, k3, v3, kp3, vp3 = (t.reshape(nsub, SPAN, HD) for t in (q, k, v, kp, vp))
        m_cur, m_prev = _band(nsub, ib > 0)
        sc = jnp.where(m_cur, _bdot(q3, k3, 2, 2) * scale, NEG)
        sp = jnp.where(m_prev, _bdot(q3, kp3, 2, 2) * scale, NEG)
        m = jnp.maximum(jnp.max(sc, -1, keepdims=True), jnp.max(sp, -1, keepdims=True))
        pc, pp = jnp.exp(sc - m), jnp.exp(sp - m)
        den = jnp.sum(pc, -1, keepdims=True) + jnp.sum(pp, -1, keepdims=True)
        o = (_bdot(pc.astype(BF16), v3, 2, 1) + _bdot(pp.astype(BF16), vp3, 2, 1)) / den
        o_ref[...] = o.reshape(T, HD).astype(BF16)
        lse = (m + jnp.log(den)).reshape(T, 1)
        lane = lax.broadcasted_iota(jnp.int32, (T, HD), 1)

        @pl.when(h == 0)
        def _():
            l_ref[...] = jnp.zeros_like(l_ref)

        l_ref[...] = jnp.where(lane == h, lse, l_ref[...])

    cur = pl.BlockSpec((None, T, QKV), lambda r, ib, h: (h, r * nib + ib, 0))
    prev = pl.BlockSpec((None, SPAN, QKV), lambda r, ib, h: (h, r * (Sd // SPAN) + jnp.maximum(ib * nsub - 1, 0), 0))
    return pl.pallas_call(
        body, name=name, grid=(dil, nib, NH), in_specs=[cur, prev],
        out_specs=[pl.BlockSpec((None, T, HD), lambda r, ib, h: (h, r * nib + ib, 0)),
                   pl.BlockSpec((T, HD), lambda r, ib, h: (r * nib + ib, 0))],
        out_shape=[jax.ShapeDtypeStruct((NH, S, HD), BF16), jax.ShapeDtypeStruct((S, HD), F32)],
        compiler_params=_cparams(("parallel", "parallel", "arbitrary")),
    )(qkv, qkv)


def _attn_bwd(qkv, d_attn, lse, delta, dil, name):
    S = qkv.shape[1]
    Sd, T, nsub, nib = _attn_tiles(S, dil)
    scale = HD ** -0.5
    ne = nsub + 1

    def body(c_ref, p_ref, n_ref, do_ref, don_ref, l_ref, ln_ref, dl_ref, dln_ref, o_ref):
        ib, h = pl.program_id(1), pl.program_id(2)
        blk, hal, nxt = c_ref[...], p_ref[...], n_ref[...]
        q, k, v = blk[:, :HD], blk[:, HD:2 * HD], blk[:, 2 * HD:]
        do = do_ref[...]
        l, dl = _head_col(l_ref[...], h), _head_col(dl_ref[...], h)
        qe = jnp.concatenate([q, nxt[:, :HD]], axis=0).reshape(ne, SPAN, HD)
        doe = jnp.concatenate([do, don_ref[...]], axis=0).reshape(ne, SPAN, HD)
        le = jnp.concatenate([l, _head_col(ln_ref[...], h)], axis=0).reshape(ne, SPAN, 1)
        dle = jnp.concatenate([dl, _head_col(dln_ref[...], h)], axis=0).reshape(ne, SPAN, 1)
        kpe = jnp.concatenate([hal[:, HD:2 * HD], k], axis=0).reshape(ne, SPAN, HD)
        vpe = jnp.concatenate([hal[:, 2 * HD:], v], axis=0).reshape(ne, SPAN, HD)
        _, m_prev = _band(ne, ib > 0, ib < nib - 1)
        p = jnp.where(m_prev, jnp.exp(_bdot(qe, kpe, 2, 2) * scale - le), 0.0)
        ds = (p * (_bdot(doe, vpe, 2, 2) - dle)).astype(BF16)
        dq = _bdot(ds, kpe, 2, 1)[:nsub]
        dk = _bdot(ds, qe, 1, 1)[1:]
        dv = _bdot(p.astype(BF16), doe, 1, 1)[1:]
        q3, k3, v3, do3 = (t.reshape(nsub, SPAN, HD) for t in (q, k, v, do))
        l3, dl3 = l.reshape(nsub, SPAN, 1), dl.reshape(nsub, SPAN, 1)
        m_cur, _ = _band(nsub, True)
        p = jnp.where(m_cur, jnp.exp(_bdot(q3, k3, 2, 2) * scale - l3), 0.0)
        ds = (p * (_bdot(do3, v3, 2, 2) - dl3)).astype(BF16)
        dq = (dq + _bdot(ds, k3, 2, 1)) * scale
        dk = (dk + _bdot(ds, q3, 1, 1)) * scale
        dv = dv + _bdot(p.astype(BF16), do3, 1, 1)
        o_ref[...] = jnp.concatenate([t.reshape(T, HD) for t in (dq, dk, dv)], axis=1).astype(BF16)

    nb = Sd // SPAN
    row = lambda r, ib: r * nib + ib
    prow = lambda r, ib: r * nb + jnp.maximum(ib * nsub - 1, 0)
    nrow = lambda r, ib: r * nb + jnp.minimum((ib + 1) * nsub, nb - 1)
    cur3 = pl.BlockSpec((None, T, QKV), lambda r, ib, h: (h, row(r, ib), 0))
    prev3 = pl.BlockSpec((None, SPAN, QKV), lambda r, ib, h: (h, prow(r, ib), 0))
    next3 = pl.BlockSpec((None, SPAN, QKV), lambda r, ib, h: (h, nrow(r, ib), 0))
    cur1 = pl.BlockSpec((None, T, HD), lambda r, ib, h: (h, row(r, ib), 0))
    next1 = pl.BlockSpec((None, SPAN, HD), lambda r, ib, h: (h, nrow(r, ib), 0))
    curc = pl.BlockSpec((T, HD), lambda r, ib, h: (row(r, ib), 0))
    nextc = pl.BlockSpec((SPAN, HD), lambda r, ib, h: (nrow(r, ib), 0))
    return pl.pallas_call(
        body, name=name, grid=(dil, nib, NH),
        in_specs=[cur3, prev3, next3, cur1, next1, curc, nextc, curc, nextc], out_specs=cur3,
        out_shape=jax.ShapeDtypeStruct((NH, S, QKV), BF16),
        compiler_params=_cparams(("parallel", "parallel", "parallel")),
    )(qkv, qkv, qkv, d_attn, d_attn, lse, lse, delta, delta)


def _ret_consts():
    lg = np.log1p(-np.exp2(-5.0 - np.arange(RH, dtype=np.float64)))
    idx = np.arange(CH, dtype=np.float64)
    rel = idx[:, None] - idx[None, :]
    intra = np.where(rel >= 0, np.exp(lg[:, None, None] * np.maximum(rel, 0.0)), 0.0)
    qd = np.exp(lg[:, None] * (idx + 1.0))
    kd = np.exp(lg[:, None] * (CH - 1.0 - idx))
    cd = np.exp(lg * CH)
    wide = lambda t: np.broadcast_to(t[:, :, None], (RH, t.shape[1], RDV))
    return (jnp.asarray(intra, F32), jnp.asarray(wide(qd), F32), jnp.asarray(wide(kd), F32),
            jnp.asarray(np.broadcast_to(cd[:, None, None], (RH, 1, RDV)), F32))


def _rot(t, c, s):
    t1, t2 = t[:, :RDK // 2], t[:, RDK // 2:]
    return jnp.concatenate([t1 * c - t2 * s, t1 * s + t2 * c], axis=1)


def _unrot(d, c, s):
    d1, d2 = d[:, :RDK // 2], d[:, RDK // 2:]
    return jnp.concatenate([d1 * c + d2 * s, d2 * c - d1 * s], axis=1)


RCH = 2


def _ret_specs(nmap):
    rows = RCH * CH
    q = pl.BlockSpec((rows, RH * RDK), lambda n: (nmap(n), OFF_RQ // (RH * RDK)))
    k = pl.BlockSpec((rows, RH * RDK), lambda n: (nmap(n), OFF_RK // (RH * RDK)))
    v = pl.BlockSpec((rows, RH * RDV), lambda n: (nmap(n), OFF_RV // (RH * RDV)))
    cs = pl.BlockSpec((rows, RDK // 2), lambda n: (nmap(n), 0))
    dmat = pl.BlockSpec((RH, CH, CH), lambda n: (0, 0, 0))
    dvec = pl.BlockSpec((RH, CH, RDV), lambda n: (0, 0, 0))
    cdv = pl.BlockSpec((RH, 1, RDV), lambda n: (0, 0, 0))
    state = pl.BlockSpec((RH, RCH, RDK, RDV), lambda n: (0, nmap(n), 0, 0))
    out = pl.BlockSpec((rows, RH * RDV), lambda n: (nmap(n), 0))
    return [q, k, v, cs, cs, dmat, dvec, dvec, cdv], state, out


def _ret_fwd(proj, cos, sin, consts):
    S = proj.shape[0]
    nc = S // CH

    def body(q_ref, k_ref, v_ref, c_ref, s_ref, d_ref, qd_ref, kd_ref, cd_ref, o_ref, st_ref, state):
        @pl.when(pl.program_id(0) == 0)
        def _():
            state[...] = jnp.zeros_like(state)

        for ci in range(RCH):
            rows = slice(ci * CH, (ci + 1) * CH)
            c, s = c_ref[rows, :], s_ref[rows, :]
            for h in range(RH):
                qk, vv = slice(h * RDK, (h + 1) * RDK), slice(h * RDV, (h + 1) * RDV)
                qb = _rot(q_ref[rows, qk].astype(F32), c, s).astype(BF16)
                kb = (_rot(k_ref[rows, qk].astype(F32), c, s) * (RDK ** -0.5)).astype(BF16)
                vb = v_ref[rows, vv]
                sb = state[h].astype(BF16)
                st_ref[h, ci] = sb
                a = (_dot(qb, kb, 1, 1) * d_ref[h]).astype(BF16)
                o_ref[rows, vv] = (_dot(a, vb, 1, 0) + _dot(qb, sb, 1, 0) * qd_ref[h]).astype(BF16)
                vk = (vb.astype(F32) * kd_ref[h]).astype(BF16)
                state[h] = cd_ref[h] * state[h] + _dot(kb, vk, 0, 0)

    ins, state_spec, out_spec = _ret_specs(lambda n: n)
    return pl.pallas_call(
        body, name="ret_fwd", grid=(nc // RCH,), in_specs=ins, out_specs=[out_spec, state_spec],
        out_shape=[jax.ShapeDtypeStruct((S, RH * RDV), BF16), jax.ShapeDtypeStruct((RH, nc, RDK, RDV), BF16)],
        scratch_shapes=[pltpu.VMEM((RH, RDK, RDV), F32)],
        compiler_params=_cparams(("arbitrary",)),
    )(proj, proj, proj, cos, sin, *consts)


def _ret_bwd(proj, cos, sin, consts, states, d_ret, d_rest):
    S = proj.shape[0]
    nc = S // CH

    def body(q_ref, k_ref, v_ref, c_ref, s_ref, d_ref, qd_ref, kd_ref, cd_ref, st_ref, do_ref, _, o_ref, dstate):
        @pl.when(pl.program_id(0) == 0)
        def _():
            dstate[...] = jnp.zeros_like(dstate)

        for ci in reversed(range(RCH)):
            rows = slice(ci * CH, (ci + 1) * CH)
            c, s = c_ref[rows, :], s_ref[rows, :]
            for h in range(RH):
                qk, vv = slice(h * RDK, (h + 1) * RDK), slice(h * RDV, (h + 1) * RDV)
                qb = _rot(q_ref[rows, qk].astype(F32), c, s).astype(BF16)
                kb = (_rot(k_ref[rows, qk].astype(F32), c, s) * (RDK ** -0.5)).astype(BF16)
                vb, sb, do = v_ref[rows, vv], st_ref[h, ci], do_ref[rows, vv]
                dmat, qd, kd = d_ref[h], qd_ref[h], kd_ref[h]
                a = (_dot(qb, kb, 1, 1) * dmat).astype(BF16)
                doq = (do.astype(F32) * qd).astype(BF16)
                dsb = dstate[h].astype(BF16)
                vk = (vb.astype(F32) * kd).astype(BF16)
                o_ref[rows, OFF_RV + h * RDV:OFF_RV + (h + 1) * RDV] = (_dot(a, do, 0, 0) + _dot(kb, dsb, 1, 0) * kd).astype(BF16)
                da = (_dot(do, vb, 1, 1) * dmat).astype(BF16)
                dq = _dot(da, kb, 1, 0) + _dot(doq, sb, 1, 1)
                dk = (_dot(da, qb, 0, 0) + _dot(vk, dsb, 1, 1)) * (RDK ** -0.5)
                o_ref[rows, OFF_RQ + h * RDK:OFF_RQ + (h + 1) * RDK] = _unrot(dq, c, s).astype(BF16)
                o_ref[rows, OFF_RK + h * RDK:OFF_RK + (h + 1) * RDK] = _unrot(dk, c, s).astype(BF16)
                dstate[h] = cd_ref[h] * dstate[h] + _dot(qb, doq, 0, 0)

    nsteps = nc // RCH
    rev = lambda n: nsteps - 1 - n
    ins, state_spec, out_spec = _ret_specs(rev)
    return pl.pallas_call(
        body, name="ret_bwd", grid=(nsteps,), in_specs=ins + [state_spec, out_spec, pl.BlockSpec(memory_space=pl.ANY)],
        out_specs=pl.BlockSpec((RCH * CH, OFF_RG), lambda n: (rev(n), 0)),
        out_shape=jax.ShapeDtypeStruct(d_rest.shape, BF16), input_output_aliases={11: 0},
        scratch_shapes=[pltpu.VMEM((RH, RDK, RDV), F32)],
        compiler_params=_cparams(("arbitrary",)),
    )(proj, proj, proj, cos, sin, *consts, states, d_ret, d_rest)


CW = 256
HALO = 16


def _shift_down(v, halo, k):
    rolled = pltpu.roll(v, k, 0)
    hr = pltpu.roll(halo, k, 0)[0:8]
    row = lax.broadcasted_iota(jnp.int32, hr.shape, 0)
    return jnp.concatenate([jnp.where(row < k, hr, rolled[0:8]), rolled[8:]], axis=0)


def _shift_up(v, halo, k):
    T = v.shape[0]
    rolled = pltpu.roll(v, T - k, 0)
    hr = pltpu.roll(halo, 8 - k, 0)[0:8]
    row = lax.broadcasted_iota(jnp.int32, hr.shape, 0)
    return jnp.concatenate([rolled[:T - 8], jnp.where(row >= 8 - k, hr, rolled[T - 8:])], axis=0)


def _conv_taps(h_ref, hp_ref, first):
    h = h_ref[...].astype(F32)
    hp = hp_ref[...].astype(F32) * jnp.where(first, 0.0, 1.0)
    return _shift_down(h, hp, 2), _shift_down(h, hp, 1), h


def _conv_specs(S, T):
    nj = DFF // CW
    cur = pl.BlockSpec((T, CW), lambda j, i: (i, j))
    prev = pl.BlockSpec((HALO, CW), lambda j, i: (jnp.maximum(i * (T // HALO) - 1, 0), j))
    nxt = pl.BlockSpec((HALO, CW), lambda j, i: (jnp.minimum((i + 1) * (T // HALO), S // HALO - 1), j))
    w = pl.BlockSpec((3, CW), lambda j, i: (0, j))
    b = pl.BlockSpec((1, CW), lambda j, i: (0, j))
    return nj, cur, prev, nxt, w, b


def _conv_fwd(hg, hu, wg, wu, bg, bu):
    S = hg.shape[0]
    T = min(1024, S)
    nj, cur, prev, _, w, b = _conv_specs(S, T)

    def body(hg_ref, hu_ref, hgp_ref, hup_ref, wg_ref, wu_ref, bg_ref, bu_ref, o_ref):
        first = pl.program_id(1) == 0
        g2, g1, g0 = _conv_taps(hg_ref, hgp_ref, first)
        u2, u1, u0 = _conv_taps(hu_ref, hup_ref, first)
        cg = wg_ref[0:1, :] * g2 + wg_ref[1:2, :] * g1 + wg_ref[2:3, :] * g0 + bg_ref[...]
        cu = wu_ref[0:1, :] * u2 + wu_ref[1:2, :] * u1 + wu_ref[2:3, :] * u0 + bu_ref[...]
        o_ref[...] = (_gelu(cg)[0] * cu).astype(BF16)

    return pl.pallas_call(
        body, name="conv_fwd", grid=(nj, S // T), in_specs=[cur, cur, prev, prev, w, w, b, b], out_specs=cur,
        out_shape=jax.ShapeDtypeStruct((S, DFF), BF16), compiler_params=_cparams(("parallel", "parallel")),
    )(hg, hu, hg, hu, wg, wu, bg, bu)


def _conv_bwd_pre(d_act, hg, hu, wg, wu, bg, bu):
    S = hg.shape[0]
    T = min(1024, S)
    nj, cur, prev, _, w, b = _conv_specs(S, T)

    def body(da_ref, hg_ref, hu_ref, hgp_ref, hup_ref, wg_ref, wu_ref, bg_ref, bu_ref,
             dcg_ref, dcu_ref, gwg_ref, gwu_ref, gbg_ref, gbu_ref):
        first = pl.program_id(1) == 0
        g2, g1, g0 = _conv_taps(hg_ref, hgp_ref, first)
        u2, u1, u0 = _conv_taps(hu_ref, hup_ref, first)
        cg = wg_ref[0:1, :] * g2 + wg_ref[1:2, :] * g1 + wg_ref[2:3, :] * g0 + bg_ref[...]
        cu = wu_ref[0:1, :] * u2 + wu_ref[1:2, :] * u1 + wu_ref[2:3, :] * u0 + bu_ref[...]
        da = da_ref[...].astype(F32)
        gl, t = _gelu(cg)
        dcg = da * cu * _gelu_grad(cg, t)
        dcu = da * gl
        dcg_ref[...] = dcg.astype(BF16)
        dcu_ref[...] = dcu.astype(BF16)

        @pl.when(first)
        def _():
            for r in (gwg_ref, gwu_ref, gbg_ref, gbu_ref):
                r[...] = jnp.zeros_like(r)

        for r, d, taps in ((gwg_ref, dcg, (g2, g1, g0)), (gwu_ref, dcu, (u2, u1, u0))):
            for j in range(3):
                r[j:j + 1, :] += jnp.sum(d * taps[j], 0, keepdims=True)
        gbg_ref[...] += jnp.sum(dcg, 0, keepdims=True)
        gbu_ref[...] += jnp.sum(dcu, 0, keepdims=True)

    return pl.pallas_call(
        body, name="conv_bwd_pre", grid=(nj, S // T), in_specs=[cur, cur, cur, prev, prev, w, w, b, b],
        out_specs=[cur, cur, w, w, b, b],
        out_shape=[jax.ShapeDtypeStruct((S, DFF), BF16)] * 2 + [jax.ShapeDtypeStruct((3, DFF), F32)] * 2
        + [jax.ShapeDtypeStruct((1, DFF), F32)] * 2,
        compiler_params=_cparams(("parallel", "arbitrary")),
    )(d_act, hg, hu, hg, hu, wg, wu, bg, bu)


def _conv_bwd_in(dc, w, name):
    S = dc.shape[0]
    T = min(1024, S)
    nj, cur, _, nxt, wspec, _ = _conv_specs(S, T)
    nt = S // T

    def body(dc_ref, dn_ref, w_ref, o_ref):
        d = dc_ref[...].astype(F32)
        dn = dn_ref[...].astype(F32) * jnp.where(pl.program_id(1) == nt - 1, 0.0, 1.0)
        o_ref[...] = (w_ref[2:3, :] * d + w_ref[1:2, :] * _shift_up(d, dn, 1) + w_ref[0:1, :] * _shift_up(d, dn, 2)).astype(BF16)

    return pl.pallas_call(
        body, name=name, grid=(nj, nt), in_specs=[cur, nxt, wspec], out_specs=cur,
        out_shape=jax.ShapeDtypeStruct((S, DFF), BF16), compiler_params=_cparams(("parallel", "parallel")),
    )(dc, dc, w)


def _adam_math(g, w, m, v):
    m = B1 * m + (1.0 - B1) * g
    v = B2 * v + (1.0 - B2) * (g * g)
    m_hat = m / (1.0 - B1 ** STEP)
    v_hat = v / (1.0 - B2 ** STEP)
    return -LR * (m_hat / (jnp.sqrt(v_hat) + EPS) + WD * w), m, v


def _reduce_tail(chip32, far, chip, name, wmv=None):
    L = len(chip32)
    _, R, C = chip32[0].shape
    tr = _tile(R, 256, 16)
    nr = R // tr

    def body(chip_ref, *refs):
        own_refs, far_refs, rest = refs[:L], refs[L:2 * L], refs[2 * L:]
        outs = rest[3:] if wmv else rest
        for ll in range(L):
            @pl.when(pl.program_id(0) == ll)
            def _(ll=ll):
                g = own_refs[ll][...]
                for s in range(3):
                    g = g + far_refs[ll][s].astype(F32)
                outs[0][...] = g
                if wmv:
                    outs[1][...], outs[2][...], outs[3][...] = _adam_math(g, rest[0][...], rest[1][...], rest[2][...])

    def rows(ll):
        return lambda l, i: jnp.where(l == ll, i, jnp.where(l < ll, 0, nr - 1))

    blk = pl.BlockSpec((None, tr, C), lambda l, i, ch: (l, i, 0))
    in_specs = [pl.BlockSpec((None, tr, C), lambda l, i, ch, f=rows(ll): (ch[0], f(l, i), 0)) for ll in range(L)]
    in_specs += [pl.BlockSpec((3, tr, C), lambda l, i, ch, f=rows(ll): (0, f(l, i), 0)) for ll in range(L)]
    args = list(chip32) + list(far)
    n_out = 1
    if wmv:
        in_specs += [blk] * 3
        args += list(wmv)
        n_out = 4
    return pl.pallas_call(
        body, name=name,
        grid_spec=pltpu.PrefetchScalarGridSpec(num_scalar_prefetch=1, grid=(L, nr), in_specs=in_specs, out_specs=[blk] * n_out),
        out_shape=[jax.ShapeDtypeStruct((L, R, C), F32)] * n_out, compiler_params=_cparams(("arbitrary", "arbitrary")),
    )(chip, *args)


def _adamw(g, w, m, v, name):
    R, C = g.shape
    tr = _tile(R, 128, 8)

    def body(g_ref, w_ref, m_ref, v_ref, d_ref, nm_ref, nv_ref):
        d_ref[...], nm_ref[...], nv_ref[...] = _adam_math(g_ref[...], w_ref[...], m_ref[...], v_ref[...])

    blk = pl.BlockSpec((tr, C), lambda i: (i, 0))
    return pl.pallas_call(
        body, name=name, grid=(R // tr,), in_specs=[blk] * 4, out_specs=[blk] * 3,
        out_shape=[jax.ShapeDtypeStruct(g.shape, F32)] * 3, compiler_params=_cparams(("parallel",)),
    )(g, w, m, v)


def _pair_sum(x, recv, core, name):
    _, R, C = x.shape
    tr = _tile(R, 600, 16)

    def body(core_ref, x_ref, r_ref, o32_ref, o16_ref):
        s = x_ref[...] + r_ref[...]
        o32_ref[...] = s
        o16_ref[...] = s.astype(BF16)

    blk = pl.BlockSpec((None, tr, C), lambda q, i, c: (q, i, 0))
    mine = pl.BlockSpec((None, None, tr, C), lambda q, i, c: (q, c[0], i, 0))
    return pl.pallas_call(
        body, name=name,
        grid_spec=pltpu.PrefetchScalarGridSpec(num_scalar_prefetch=1, grid=(4, R // tr), in_specs=[mine, blk], out_specs=[blk, blk]),
        out_shape=[jax.ShapeDtypeStruct((4, R, C), F32), jax.ShapeDtypeStruct((4, R, C), BF16)],
        compiler_params=_cparams(("parallel", "parallel")),
    )(core, x.reshape(4, 2, R, C), recv)


def _sum_slots(x, name):
    def body(x_ref, o_ref):
        g = x_ref[0]
        for s in range(1, x.shape[0]):
            g = g + x_ref[s]
        o_ref[...] = g

    return pl.pallas_call(body, name=name, out_shape=jax.ShapeDtypeStruct(x.shape[1:], F32))(x)


MESH = pl.DeviceIdType.MESH
_HBM = pl.BlockSpec(memory_space=pltpu.HBM)


def _dma_sems(n):
    return pltpu.SemaphoreType.DMA((n,))


def _gather_many(xs, name):
    n = len(xs)

    def body(*refs):
        x_refs, out_refs = refs[:n], refs[n:2 * n]
        send_sems, recv_sems, local_sems = refs[2 * n:]
        ax, ay, ac = lax.axis_index("x"), lax.axis_index("y"), lax.axis_index("c")
        me, sibling = (ax, ay, ac), (ax, ay, 1 - ac)
        chips = [(1 - ax, ay), (ax, 1 - ay), (1 - ax, 1 - ay)]

        def copy(a, k, block, to, own=False):
            slot = out_refs[a].at[4 * block[0] + 2 * block[1] + block[2]]
            return pltpu.make_async_remote_copy(
                src_ref=x_refs[a] if own else slot, dst_ref=slot, send_sem=send_sems.at[7 * a + k],
                recv_sem=recv_sems.at[7 * a + k], device_id=to, device_id_type=MESH)

        mine = [pltpu.make_async_copy(x_refs[a], out_refs[a].at[4 * ax + 2 * ay + ac], local_sems.at[a]) for a in range(n)]
        first = [copy(a, 0, me, sibling, own=True) for a in range(n)]
        first += [copy(a, 1 + j, me, (*chip, ac), own=True) for j, chip in enumerate(chips) for a in range(n)]
        for cp in mine + first:
            cp.start()
        passed = []
        for j, chip in enumerate(chips):
            for a in range(n):
                copy(a, 1 + j, (*chip, ac), me).wait_recv()
                cp = copy(a, 4 + j, (*chip, ac), sibling)
                cp.start()
                passed.append(cp)
        for a in range(n):
            copy(a, 0, sibling, me).wait_recv()
            for j, chip in enumerate(chips):
                copy(a, 4 + j, (*chip, 1 - ac), me).wait_recv()
        for cp in first + passed:
            cp.wait_send()
        for cp in mine:
            cp.wait()

    return pl.pallas_call(
        body, name=name, out_shape=[jax.ShapeDtypeStruct((N_DEV,) + x.shape, x.dtype) for x in xs],
        in_specs=[_HBM] * n, out_specs=[_HBM] * n, scratch_shapes=[_dma_sems(7 * n), _dma_sems(7 * n), _dma_sems(n)],
    )(*xs)


_SEM = pl.BlockSpec(memory_space=pltpu.SEMAPHORE)
_EFFECT = pltpu.SideEffectType.DATAFLOW_SIDE_EFFECTING


def _peer(k):
    ax, ay, ac = lax.axis_index("x"), lax.axis_index("y"), lax.axis_index("c")
    px = 1 - ax if k & 4 else ax
    py = 1 - ay if k & 2 else ay
    pc = 1 - ac if k & 1 else ac
    return (px, py, pc), 4 * px + 2 * py + pc


def _build_gather(x_refs, land_refs, send_sems, recv_sems, waiting):
    _, me = _peer(0)
    copies = []
    for a in range(len(x_refs)):
        for k in range(1, N_DEV):
            peer, slot = _peer(k)
            copies.append(pltpu.make_async_remote_copy(
                src_ref=x_refs[a], dst_ref=land_refs[a].at[slot if waiting else me], send_sem=send_sems.at[7 * a + k - 1],
                recv_sem=recv_sems.at[7 * a + k - 1], device_id=peer, device_id_type=MESH))
    return copies


def _build_cores(x_refs, land_refs, send_sems, recv_sems, waiting):
    ax, ay, ac = lax.axis_index("x"), lax.axis_index("y"), lax.axis_index("c")
    copies = []
    for a in range(len(x_refs)):
        for q in range(4):
            copies.append(pltpu.make_async_remote_copy(
                src_ref=x_refs[a].at[2 * q + 1 - ac], dst_ref=land_refs[a].at[q], send_sem=send_sems.at[4 * a + q],
                recv_sem=recv_sems.at[4 * a + q], device_id=(ax, ay, 1 - ac), device_id_type=MESH))
    return copies


def _build_chips(p_refs, land_refs, send_sems, recv_sems, waiting):
    ax, ay, ac = lax.axis_index("x"), lax.axis_index("y"), lax.axis_index("c")
    copies = []
    for a in range(len(p_refs)):
        for k in range(1, 4):
            px = 1 - ax if k & 2 else ax
            py = 1 - ay if k & 1 else ay
            copies.append(pltpu.make_async_remote_copy(
                src_ref=p_refs[a].at[2 * px + py], dst_ref=land_refs[a].at[k - 1], send_sem=send_sems.at[3 * a + k - 1],
                recv_sem=recv_sems.at[3 * a + k - 1], device_id=(px, py, ac), device_id_type=MESH))
    return copies


_EXCHANGES = {"gather": (_build_gather, 7, N_DEV), "cores": (_build_cores, 4, 4), "chips": (_build_chips, 3, 3)}


def _exchange_start(kind, xs, lands, name, after=None):
    build, per, _ = _EXCHANGES[kind]
    n = len(xs)

    def body(*refs):
        for cp in build(refs[:n], refs[n:2 * n], refs[-2 * n - 3], refs[-2 * n - 2], False):
            cp.start()
        refs[-1][...] = jnp.zeros_like(refs[-1])

    hbm = lambda t: pltpu.HBM(t.shape, t.dtype)
    args = [pltpu.with_memory_space_constraint(t, pltpu.HBM) for t in list(xs) + list(lands)]
    in_specs = [_HBM] * (2 * n)
    if after is not None:
        args.append(after)
        in_specs.append(pl.BlockSpec(memory_space=pl.ANY))
    outs = pl.pallas_call(
        body, name=name,
        out_shape=(_dma_sems(per * n), _dma_sems(per * n), *[hbm(t) for t in xs], *[hbm(t) for t in lands],
                   jax.ShapeDtypeStruct((8, 128), F32)),
        in_specs=in_specs, out_specs=(_SEM, _SEM, *[_HBM] * (2 * n), pl.BlockSpec(memory_space=pltpu.VMEM)),
        input_output_aliases={a: 2 + a for a in range(2 * n)},
        compiler_params=pltpu.CompilerParams(has_side_effects=_EFFECT),
    )(*args)
    return (kind, outs[0], outs[1], outs[2:2 + n], outs[2 + n:2 + 2 * n]), outs[-1]


def _exchange_wait(flight, after, name):
    kind, send_sems, recv_sems, xs, lands = flight
    build = _EXCHANGES[kind][0]
    n = len(xs)

    def body(*refs):
        for cp in build(refs[:n], refs[n:2 * n], refs[2 * n], refs[2 * n + 1], True):
            cp.wait_send()
            cp.wait_recv()

    hbm = lambda t: pltpu.HBM(t.shape, t.dtype)
    outs = pl.pallas_call(
        body, name=name, out_shape=(*[hbm(t) for t in xs], *[hbm(t) for t in lands]),
        in_specs=[_HBM] * (2 * n) + [_SEM, _SEM, pl.BlockSpec(memory_space=pl.ANY)], out_specs=[_HBM] * (2 * n),
        input_output_aliases={a: a for a in range(2 * n)}, compiler_params=pltpu.CompilerParams(has_side_effects=_EFFECT),
    )(*xs, *lands, send_sems, recv_sems, after)
    return outs[:n], outs[n:]


def _x_view(xb, d):
    return xb if d == 1 else xb.reshape(xb.shape[0] // d, d * xb.shape[1])


def _layer_fwd(x, xb, p, w, cos, sin, rconsts, late=None):
    proj = _mm(xb, w["win"], tb=True, b_rows=(N_ATT, N_REST), name="mm_proj", out_dtype=BF16)
    qkvs, ogs, lgs = [], [], []
    for g, dil in enumerate(DILATIONS):
        qkv = _qkv_fwd(_x_view(xb, dil), w["win"], g, dil, f"mm_qkv{g}")
        o, l = _attn_fwd(qkv, dil, f"attn_fwd_g{g}")
        qkvs.append(qkv)
        ogs.append(_to_tokens(o, dil))
        lgs.append(_to_tokens(l, dil))
    attn, lse = _rowwise(_f_combine, ogs + lgs, [], [(D, BF16), (HD, F32)], [], name="attn_combine")
    ret_raw, states = _ret_fwd(proj, cos, sin, rconsts)
    rg_win = (proj, RH * RDV, OFF_RG // (RH * RDV))
    ga_win, gr_win = (proj, D, OFF_GA // D), (proj, D, OFF_GR // D)
    (r,) = _rowwise(_f_gn, [ret_raw, rg_win], [w["ret_gn_g"], w["ret_gn_b"]], [(RH * RDV, BF16)], [], name="gn_fwd", tm=256)
    if late is not None:
        w = {**w, **late(r)}
    ap = _mm(attn, w["w_attn_proj"], name="mm_attn_proj", out_dtype=BF16)
    rp = _mm(r, w["w_ret_proj"], name="mm_ret_proj", out_dtype=BF16, tk=2048)
    (merged,) = _rowwise(_f_gate, [ap, rp, ga_win, gr_win], [], [(D, BF16)], [], name="gate_fwd")
    mix = _mm(merged, w["w_out"], name="mm_out")
    h1, x1, x1b = _rowwise(_f_ln1, [x, mix], [w["ln1_g"], w["ln1_b"]], [(D, F32), (D, F32), (D, BF16)], [], name="ln1_fwd")
    z = _mm(x1b, w["w_ple_gate"], name="mm_ple_gate")
    pp = _mm(p, w["w_ple_proj"], tb=True, name="mm_ple_proj")
    hg = _mm(x1b, w["w_up"], tb=True, b_rows=(0, DFF), name="mm_up_g", out_dtype=BF16, tm=512, tn=DFF)
    hu = _mm(x1b, w["w_up"], tb=True, b_rows=(DFF, DFF), name="mm_up_u", out_dtype=BF16, tm=512, tn=DFF)
    act = _conv_fwd(hg, hu, w["conv_wg"], w["conv_wu"], w["conv_bg"], w["conv_bu"])
    ffn = _mm(act, w["w_down"], name="mm_down", tm=512, tk=DFF)
    h2, x2, x2b = _rowwise(_f_ln2, [x1, ffn, z, pp], [w["ln2_g"], w["ln2_b"]], [(D, F32), (D, F32), (D, BF16)], [], name="ln2_fwd")
    saved = dict(xb=xb, proj=proj, qkvs=qkvs, attn=attn, lse=lse, ret_raw=ret_raw, states=states, r=r, ap=ap, rp=rp,
                 merged=merged, h1=h1, x1b=x1b, z=z, pp=pp, hg=hg, hu=hu, act=act, h2=h2, p=p)
    return x2, x2b, saved, w


def _after(fn, token):
    return fn if token is None else (lambda *a: fn(*a[:-1]))


def _layer_bwd(dys, w, sv, cos, sin, rconsts, hooks):
    gr = {}
    proj = sv["proj"]
    call = lambda key, *a: hooks[key](*a) if key in hooks else None
    held = lambda token: [] if token is None else [token]
    token = hooks.get("token")
    dh2, dh2b, gr["ln2_g"], gr["ln2_b"] = _rowwise(_after(_f_ln_bwd, token), list(dys) + [sv["h2"]], [w["ln2_g"]] + held(token),
                                                   [(D, F32), (D, BF16)], [(1, D), (1, D)], name="ln2_bwd")
    d_act = _mm(dh2b, w["w_down"], tb=True, name="mm_down_dx", out_dtype=BF16, tm=512, tn=DFF)
    gr["w_down"] = _mm(sv["act"], dh2b, ta=True, name="mm_down_dw", tm=DFF // 2)
    dcg, dcu, gwg, gwu, gbg, gbu = _conv_bwd_pre(d_act, sv["hg"], sv["hu"], w["conv_wg"], w["conv_wu"], w["conv_bg"], w["conv_bu"])
    token = call("after_ffn", dcg)
    gr["conv_w"] = jnp.concatenate([gwg, gwu], axis=1)
    gr["conv_b"] = jnp.concatenate([gbg, gbu], axis=1)
    dhg = _conv_bwd_in(dcg, w["conv_wg"], "conv_bwd_in_g")
    dhu = _conv_bwd_in(dcu, w["conv_wu"], "conv_bwd_in_u")
    gw_up = _mm(dhg, sv["x1b"], ta=True, name="mm_up_g_dw", tm=DFF // 2, out_rows=(0, 2 * DFF))
    gr["w_up"] = _mm(dhu, sv["x1b"], ta=True, name="mm_up_u_dw", tm=DFF // 2, out_rows=(DFF, 2 * DFF), into=gw_up)
    dx1 = _mm(dhg, w["w_up"], b_rows=(0, DFF), name="mm_up_g_dx", add=dh2, add_scale=ALPHA, tm=512, tk=DFF)
    dx1 = _mm(dhu, w["w_up"], b_rows=(DFF, DFF), name="mm_up_u_dx", add=dx1, tm=512, tk=DFF)
    dpp, dz = _rowwise(_f_ple_bwd, [dh2, sv["z"], sv["pp"]], [], [(D, BF16), (D, BF16)], [], name="ple_bwd")
    gr["w_ple_proj"] = _mm(dpp, sv["p"], ta=True, name="mm_ple_proj_dw")
    gr["w_ple_gate"] = _mm(sv["x1b"], dz, ta=True, name="mm_ple_gate_dw")
    dx1 = _mm(dz, w["w_ple_gate"], tb=True, name="mm_ple_gate_dx", add=dx1)
    dh1, dh1b, gr["ln1_g"], gr["ln1_b"] = _rowwise(_after(_f_ln_bwd, token), [dx1, sv["h1"]], [w["ln1_g"]] + held(token),
                                                   [(D, F32), (D, BF16)], [(1, D), (1, D)], name="ln1_bwd")
    d_merged = _mm(dh1b, w["w_out"], tb=True, name="mm_out_dx", out_dtype=BF16)
    gr["w_out"] = _mm(sv["merged"], dh1b, ta=True, name="mm_out_dw")
    rg_win = (proj, RH * RDV, OFF_RG // (RH * RDV))
    ga_win, gr_win = (proj, D, OFF_GA // D), (proj, D, OFF_GR // D)
    dap, drp, d_rest = _rowwise(_f_gate_bwd, [d_merged, sv["ap"], sv["rp"], ga_win, gr_win], [],
                                [(D, BF16), (D, BF16), (2 * D, BF16, N_REST, OFF_GA // (2 * D), None)], [], name="gate_bwd")
    d_attn = _mm(dap, w["w_attn_proj"], tb=True, name="mm_attn_proj_dx", out_dtype=BF16)
    gr["w_attn_proj"] = _mm(sv["attn"], dap, ta=True, name="mm_attn_proj_dw")
    d_r = _mm(drp, w["w_ret_proj"], tb=True, name="mm_ret_proj_dx", out_dtype=BF16, tn=2048)
    gr["w_ret_proj"] = _mm(sv["r"], drp, ta=True, name="mm_ret_proj_dw", tm=2048)
    token = call("early_grads", gr)
    d_ret, d_rest, gr["ret_gn_g"], gr["ret_gn_b"] = _rowwise(
        _after(_f_gn_bwd, token), [d_r, sv["ret_raw"], rg_win], [w["ret_gn_g"], w["ret_gn_b"]] + held(token),
        [(RH * RDV, BF16), (RH * RDV, BF16, N_REST, OFF_RG // (RH * RDV), d_rest)],
        [(1, RH * RDV), (1, RH * RDV)], name="gn_bwd", tm=256)
    d_rest = _ret_bwd(proj, cos, sin, rconsts, sv["states"], d_ret, d_rest)
    token = call("after_ret", d_rest)
    (delta,) = _rowwise(_after(_f_delta, token), [d_attn, sv["attn"]], held(token), [(HD, F32)], [], name="attn_delta")
    gw_in, dqkvs = None, []
    for g, dil in enumerate(DILATIONS):
        dqkvs.append(_attn_bwd(sv["qkvs"][g], _to_head_residues(d_attn, dil), _to_residues(sv["lse"], dil),
                               _to_residues(delta, dil), dil, f"attn_bwd_g{g}"))
        gw_in = _qkv_dw(dqkvs[g], _x_view(sv["xb"], dil), g, dil, f"mm_qkv{g}_dw", into=gw_in)
    gw_in = _mm(d_rest, sv["xb"], ta=True, name="mm_proj_dw", out_rows=(N_ATT, N_IN), into=gw_in, blocks8=True)
    gr["w_in"] = gw_in.reshape(N_IN, D)
    token = call("w_in_ready", gr["w_in"])
    dx0 = _mm(d_rest, w["win"], b_rows=(N_ATT, N_REST), name="mm_proj_dx", add=dh1, add_scale=ALPHA, after=token)
    dx_parts = []
    for g, dil in enumerate(DILATIONS):
        if dil == 1:
            dx0 = _qkv_dx(dqkvs[g], w["win"], g, dil, f"mm_qkv{g}_dx", F32, add=dx0)
            token = call("after_dx0", dx0)
        else:
            dx_parts.append(_qkv_dx(dqkvs[g], w["win"], g, dil, f"mm_qkv{g}_dx", BF16, after=token).reshape(dx0.shape))
    return [dx0] + dx_parts, gr


def _local_step(x, p, positions, target, ws, own_hooks=None, on_grads=None):
    half = RDK // 2
    freq = jnp.power(ROPE_BASE, -jnp.arange(half, dtype=F32) / half)
    ang = positions.astype(F32)[:, None] * freq[None, :]
    cos, sin = jnp.cos(ang), jnp.sin(ang)
    rconsts = _ret_consts()
    xb = x.astype(BF16)
    saved, ws = [], list(ws)
    for l in range(DEPTH):
        first, late = ws[l] if isinstance(ws[l], tuple) else (ws[l], None)
        if callable(first):
            first = first(x)
        x, xb, sv, ws[l] = _layer_fwd(x, xb, p[l], first, cos, sin, rconsts, late)
        saved.append(sv)
    dy, loss_vec = _rowwise(_f_loss, [x, target], [], [(D, F32)], [(1, D)], name="loss")
    dys, grads = [dy], [None] * DEPTH
    from_above = {}
    for l in reversed(range(DEPTH)):
        hooks = {**from_above, **(own_hooks(l) if own_hooks else {})}
        dys, grads[l] = _layer_bwd(dys, ws[l], saved[l], cos, sin, rconsts, hooks)
        from_above = on_grads(l, grads[l]) if on_grads else {}
    (grad_x,) = _rowwise(_f_sum, dys, [], [(D, F32)], [], name="grad_x_sum")
    return loss_vec, grad_x, grads


def _pack_rows(arrs):
    parts, where, off = [], [], 0
    for t in arrs:
        t = t.reshape(-1, D)
        rows = t.shape[0]
        padded = -(-rows // 8) * 8
        parts.append(jnp.pad(t, ((0, padded - rows), (0, 0))))
        where.append((off, rows))
        off += padded
    return jnp.concatenate(parts, axis=0), where


FIRST = ("w_in",)
LATER = tuple(n for n in BIG if n not in FIRST)


def _first_weights(g, l, W):
    w = dict(win=g["w_in"].reshape(N_IN, D))
    for n in ("ret_gn_g", "ret_gn_b", "ln1_g", "ln1_b", "ln2_g", "ln2_b"):
        w[n] = W[n][l][None, :]
    return w


def _later_weights(g, l, conv_w_all, conv_b):
    w = dict(w_up=g["w_up"].reshape(2 * DFF, D), w_ple_proj=g["w_ple_proj"].reshape(D, PLE),
             w_attn_proj=g["w_attn_proj"].reshape(D, D), w_ret_proj=g["w_ret_proj"].reshape(RH * RDV, D),
             w_out=g["w_out"].reshape(D, D), w_down=g["w_down"].reshape(DFF, D), w_ple_gate=g["w_ple_gate"].reshape(D, D))
    w["conv_wg"], w["conv_wu"] = conv_w_all[l][:, :DFF], conv_w_all[l][:, DFF:]
    w["conv_bg"], w["conv_bu"] = conv_b[l][None, :DFF], conv_b[l][None, DFF:]
    return w


def _layer_weights(g, l, conv_w_all, conv_b, W):
    return {**_first_weights(g, l, W), **_later_weights(g, l, conv_w_all, conv_b)}


def kernel(x, p, positions, w_in, w_attn_proj, w_ret_proj, ret_gn_g, ret_gn_b, w_out, ln1_g, ln1_b, w_up, conv_w, conv_b, w_down, w_ple_gate, w_ple_proj, ln2_g, ln2_b, loss_target, m_w_in, m_w_attn_proj, m_w_ret_proj, m_ret_gn_g, m_ret_gn_b, m_w_out, m_ln1_g, m_ln1_b, m_w_up, m_conv_w, m_conv_b, m_w_down, m_w_ple_gate, m_w_ple_proj, m_ln2_g, m_ln2_b, v_w_in, v_w_attn_proj, v_w_ret_proj, v_ret_gn_g, v_ret_gn_b, v_w_out, v_ln1_g, v_ln1_b, v_w_up, v_conv_w, v_conv_b, v_w_down, v_w_ple_gate, v_w_ple_proj, v_ln2_g, v_ln2_b):
    W = dict(w_in=w_in, w_attn_proj=w_attn_proj, w_ret_proj=w_ret_proj, ret_gn_g=ret_gn_g, ret_gn_b=ret_gn_b, w_out=w_out,
             ln1_g=ln1_g, ln1_b=ln1_b, w_up=w_up, conv_w=conv_w, conv_b=conv_b, w_down=w_down, w_ple_gate=w_ple_gate,
             w_ple_proj=w_ple_proj, ln2_g=ln2_g, ln2_b=ln2_b)
    M = dict(w_in=m_w_in, w_attn_proj=m_w_attn_proj, w_ret_proj=m_w_ret_proj, ret_gn_g=m_ret_gn_g, ret_gn_b=m_ret_gn_b,
             w_out=m_w_out, ln1_g=m_ln1_g, ln1_b=m_ln1_b, w_up=m_w_up, conv_w=m_conv_w, conv_b=m_conv_b, w_down=m_w_down,
             w_ple_gate=m_w_ple_gate, w_ple_proj=m_w_ple_proj, ln2_g=m_ln2_g, ln2_b=m_ln2_b)
    V = dict(w_in=v_w_in, w_attn_proj=v_w_attn_proj, w_ret_proj=v_w_ret_proj, ret_gn_g=v_ret_gn_g, ret_gn_b=v_ret_gn_b,
             w_out=v_w_out, ln1_g=v_ln1_g, ln1_b=v_ln1_b, w_up=v_w_up, conv_w=v_conv_w, conv_b=v_conv_b, w_down=v_w_down,
             w_ple_gate=v_w_ple_gate, w_ple_proj=v_w_ple_proj, ln2_g=v_ln2_g, ln2_b=v_ln2_b)

    me = 4 * lax.axis_index("x") + 2 * lax.axis_index("y") + lax.axis_index("c")
    shard = lambda n, l: (W[n][l].T if n in COL_SHARDED else W[n][l]).astype(BF16)
    landing = lambda ts: [lax.dynamic_update_index_in_dim(lax.empty((N_DEV,) + t.shape, t.dtype), t, me, 0) for t in ts]
    first0 = _gather_many([shard(n, 0) for n in FIRST], "gather_first_l0")
    later0 = [shard(n, 0) for n in LATER] + [conv_w]
    flight0, token0 = _exchange_start("gather", later0, landing(later0), "gather_later_l0_start", after=first0[0])
    all1 = [shard(n, 1) for n in BIG]
    flight1, token1 = _exchange_start("gather", all1, landing(all1), "gather_weights_l1_start", after=token0)
    conv_w_all = []

    def later_first_layer(after):
        _, got = _exchange_wait(flight0, after, "gather_later_l0_wait")
        conv_w_all.append(got[-1].transpose(1, 2, 0, 3).reshape(DEPTH, 3, 2 * DFF))
        return _later_weights(dict(zip(LATER, got)), 0, conv_w_all[0], conv_b)

    def second_layer(after):
        _, got = _exchange_wait(flight1, after, "gather_weights_l1_wait")
        return _layer_weights(dict(zip(BIG, got)), 1, conv_w_all[0], conv_b, W)

    core = lax.axis_index("c").astype(jnp.int32).reshape(1)
    chip = (2 * lax.axis_index("x") + lax.axis_index("y")).astype(jnp.int32).reshape(1)
    empty_like = lambda ts, slots: [lax.empty((slots,) + t.shape[1:], t.dtype) for t in ts]
    chip32, far = [{} for _ in range(DEPTH)], [{} for _ in range(DEPTH)]
    pending = []

    def reduction(l, names, tag):
        state = {}

        def start(g):
            mine = [g[n].reshape((N_DEV, -1) + g[n].shape[1:]) for n in names]
            state["cores"], token = _exchange_start("cores", mine, empty_like(mine, 4), f"exchange_cores_{tag}_start")
            return token

        def onward(after):
            mine, theirs = _exchange_wait(state["cores"], after, f"exchange_cores_{tag}_wait")
            sums = [_pair_sum(a, b, core, f"pair_sum_l{l}_{n}") for a, b, n in zip(mine, theirs, names)]
            for n, s in zip(names, sums):
                chip32[l][n] = s[0]
            sent = [s[0 if n in F32_OVER_ICI else 1] for s, n in zip(sums, names)]
            flight, token = _exchange_start("chips", sent, empty_like(sent, 3), f"exchange_chips_{tag}_start")
            pending.append((l, names, flight, tag))
            return token

        return start, onward

    def on_grads(l, g):
        if l == 0:
            return {}
        start, onward = reduction(l, BIG, f"l{l}")
        return dict(token=start(g), after_ffn=onward)

    def own_hooks(l):
        if l != 0:
            return {}
        start_e, onward_e = reduction(0, LATER, "l0_later")
        start_w, onward_w = reduction(0, FIRST, "l0_first")
        return dict(early_grads=start_e, after_ret=onward_e, w_in_ready=lambda gw: start_w({"w_in": gw}), after_dx0=onward_w)

    ws = [(_first_weights(dict(zip(FIRST, first0)), 0, W), later_first_layer), second_layer]
    loss_vec, grad_x, grads = _local_step(x[0] + token1[0, 0], p[:, 0], positions[0], loss_target[0], ws, own_hooks, on_grads)
    loss = lax.psum(jnp.sum(loss_vec), ("x", "y", "c"))
    for l, names, flight, tag in pending:
        _, got = _exchange_wait(flight, grad_x, f"exchange_chips_{tag}_wait")
        far[l].update(zip(names, got))
    G, DW, NM, NV = ({} for _ in range(4))
    for n in BIG:
        chip32_n = [chip32[l][n] for l in range(DEPTH)]
        far_n = [far[l][n] for l in range(DEPTH)]
        if n in COL_SHARDED:
            G[n] = _reduce_tail(chip32_n, far_n, chip, f"reduced_{n}")[0].transpose(0, 2, 1)
            R2, C2 = DEPTH * W[n].shape[1], W[n].shape[2]
            res = _adamw(*(t.reshape(R2, C2) for t in (G[n], W[n], M[n], V[n])), f"adamw_{n}")
            DW[n], NM[n], NV[n] = (t.reshape(W[n].shape) for t in res)
        else:
            G[n], DW[n], NM[n], NV[n] = _reduce_tail(chip32_n, far_n, chip, f"adamw_{n}", wmv=(W[n], M[n], V[n]))

    small_names = SMALL + ("conv_w",)
    g_small, where = _pack_rows([jnp.stack([grads[l][n] for l in range(DEPTH)]) for n in small_names])
    (g_all,) = _gather_many([g_small], "gather_small_grads")
    g_small = _sum_slots(g_all, "sum_small_grads")
    for n, (off, rows) in zip(SMALL, where):
        G[n] = g_small[off:off + rows].reshape(W[n].shape)
    off, rows = where[-1]
    g_cw = g_small[off:off + rows].reshape(DEPTH, 3, N_DEV, conv_w.shape[2])
    G["conv_w"] = lax.dynamic_index_in_dim(g_cw, me, axis=2, keepdims=False)
    packed = [_pack_rows([d[n] for n in SMALL]) for d in (G, W, M, V)]
    small_out = _adamw(*(t for t, _ in packed), "adamw_small")
    for res, dst in zip(small_out, (DW, NM, NV)):
        for n, (off, rows) in zip(SMALL, packed[0][1]):
            dst[n] = res[off:off + rows].reshape(W[n].shape)
    two_d = lambda t: t.reshape(DEPTH * 3, conv_w.shape[2])
    cw_out = _adamw(two_d(G["conv_w"]), two_d(conv_w), two_d(m_conv_w), two_d(v_conv_w), "adamw_conv_w")
    for res, dst in zip(cw_out, (DW, NM, NV)):
        dst["conv_w"] = res.reshape(conv_w.shape)

    return (loss, grad_x[None], *[G[n] for n in WEIGHTS], *[DW[n] for n in WEIGHTS], *[NM[n] for n in WEIGHTS],
            *[NV[n] for n in WEIGHTS])
```

```python
import math

import numpy as np
import jax
import jax.numpy as jnp
from jax import lax
from jax.experimental import pallas as pl
from jax.experimental.pallas import tpu as pltpu

F32, BF16 = jnp.float32, jnp.bfloat16

D = 1024
DEPTH = 2
N_DEV = 8
HD = 128
NH = 8
DILATIONS = (1, 4, 16)
SPAN = 128
N_ATT = 3 * 3 * NH * HD
RH, RDK, RDV = 4, 256, 512
CH = 128
DFF = 2816
PLE = 256
N_IN = 17408
N_REST = N_IN - N_ATT
OFF_RQ, OFF_RK, OFF_RV, OFF_RG, OFF_GA, OFF_GR = 0, 1024, 2048, 4096, 6144, 7168
ALPHA = (2 * DEPTH) ** 0.25
LN_EPS, GN_EPS = 1e-5, 1e-6
ROPE_BASE = 10000.0
LR, B1, B2, EPS, WD, STEP = 0.001, 0.9, 0.999, 1e-8, 0.01, 10
VMEM_LIMIT = 48 * 1024 * 1024
NEG = -1e30

BIG = ("w_in", "w_attn_proj", "w_ret_proj", "w_out", "w_up", "w_down", "w_ple_gate", "w_ple_proj")
COL_SHARDED = ("w_in", "w_up", "w_ple_proj")
F32_OVER_ICI = ("w_attn_proj", "w_out", "w_ple_gate", "w_ple_proj")
SMALL = ("ret_gn_g", "ret_gn_b", "ln1_g", "ln1_b", "conv_b", "ln2_g", "ln2_b")
WEIGHTS = ("w_in", "w_attn_proj", "w_ret_proj", "ret_gn_g", "ret_gn_b", "w_out", "ln1_g", "ln1_b", "w_up",
           "conv_w", "conv_b", "w_down", "w_ple_gate", "w_ple_proj", "ln2_g", "ln2_b")


def _tile(n, cap, mult=128):
    if n <= cap:
        return n
    t = (cap // mult) * mult
    while n % t:
        t -= mult
    return t


def _cparams(sem):
    return pltpu.CompilerParams(dimension_semantics=sem, vmem_limit_bytes=VMEM_LIMIT)


def _dot(a, b, ca, cb):
    return lax.dot_general(a, b, (((ca,), (cb,)), ((), ())), preferred_element_type=F32)


def _bdot(a, b, ca, cb):
    return lax.dot_general(a, b, (((ca,), (cb,)), ((0,), (0,))), preferred_element_type=F32)


def _mm(a, b, *, name, ta=False, tb=False, out_dtype=F32, add=None, add_scale=1.0, tm=1024, tn=1024, tk=1024,
        b_rows=None, out_rows=None, into=None, blocks8=False, after=None):
    M, K = (a.shape[1], a.shape[0]) if ta else a.shape
    b_first, b_count = b_rows if b_rows else (0, b.shape[0])
    N = b_count if tb else b.shape[1]
    assert K == (b.shape[1] if tb else b_count)
    tm, tn, tk = _tile(M, tm), _tile(N, tn), _tile(K, tk)
    nk = K // tk
    o_first, o_total = out_rows if out_rows else (0, M)
    jb, kb, io = (b_first // tn, 0, o_first // tm) if tb else (0, b_first // tk, o_first // tm)
    assert b_first % (tn if tb else tk) == 0 and o_first % tm == 0 and (add is None or out_rows is None)

    def body(*refs):
        if add is None:
            a_ref, b_ref = refs[:2]
        else:
            a_ref, b_ref, add_ref = refs[:3]
        o_ref, acc_ref = refs[-2:]
        k = pl.program_id(2)

        @pl.when(k == 0)
        def _():
            acc_ref[...] = jnp.zeros_like(acc_ref)

        acc_ref[...] += _dot(a_ref[...].astype(BF16), b_ref[...].astype(BF16), 0 if ta else 1, 1 if tb else 0)

        @pl.when(k == nk - 1)
        def _():
            r = acc_ref[...]
            if add is not None:
                r = r + add_scale * add_ref[...].astype(F32)
            o_ref[...] = r.astype(out_dtype).reshape(o_ref.shape)

    a_spec = pl.BlockSpec((tk, tm), lambda i, j, k: (k, i)) if ta else pl.BlockSpec((tm, tk), lambda i, j, k: (i, k))
    if tb:
        b_spec = pl.BlockSpec((tn, tk), lambda i, j, k: (j + jb, k))
    else:
        b_spec = pl.BlockSpec((tk, tn), lambda i, j, k: (k + kb, j))
    if blocks8:
        assert tm == 1024
        o_spec = pl.BlockSpec((1, 8, 128, tn), lambda i, j, k: (i + io, 0, 0, j))
        o_shape = (o_total // tm, 8, 128, N)
    else:
        o_spec = pl.BlockSpec((tm, tn), lambda i, j, k: (i + io, j))
        o_shape = (o_total, N)
    in_specs, args, aliases = [a_spec, b_spec], [a, b], {}
    if add is not None:
        in_specs.append(o_spec)
        args.append(add)
    if after is not None:
        in_specs.append(pl.BlockSpec(memory_space=pl.ANY))
        args.append(after)
    if into is not None:
        aliases = {len(args): 0}
        in_specs.append(pl.BlockSpec(memory_space=pl.ANY))
        args.append(into)
    return pl.pallas_call(
        body, name=name, grid=(M // tm, N // tn, nk), in_specs=in_specs, out_specs=o_spec,
        out_shape=jax.ShapeDtypeStruct(o_shape, out_dtype), scratch_shapes=[pltpu.VMEM((tm, tn), F32)],
        input_output_aliases=aliases, compiler_params=_cparams(("parallel", "parallel", "arbitrary")),
    )(*args)


def _rowwise(fn, rows, pars, outs, accs, *, name, tm=512):
    first = rows[0][0] if isinstance(rows[0], tuple) else rows[0]
    S = first.shape[-2]
    tm = _tile(S, tm, 16)
    n_r, n_p, n_o = len(rows), len(pars), len(outs)
    outs = [o if len(o) == 5 else (o[0], o[1], o[0], 0, None) for o in outs]
    intos = [(k, o[4]) for k, o in enumerate(outs) if o[4] is not None]
    n_i = len(intos)

    def body(*refs):
        i = pl.program_id(0)
        vals = [r[...] for r in refs[:n_r + n_p]]
        res = fn(*vals)
        if not isinstance(res, (tuple, list)):
            res = (res,)
        o_refs = refs[n_r + n_p + n_i:n_r + n_p + n_i + n_o]
        a_refs = refs[n_r + n_p + n_i + n_o:]
        for r, v in zip(o_refs, res[:n_o]):
            r[...] = v.astype(r.dtype)
        if a_refs:
            @pl.when(i == 0)
            def _():
                for r in a_refs:
                    r[...] = jnp.zeros_like(r)

            for r, v in zip(a_refs, res[n_o:]):
                r[...] += v

    in_specs, args = [], []
    for r in rows:
        if isinstance(r, tuple):
            arr, w, cb = r
            in_specs.append(pl.BlockSpec((tm, w), lambda i, cb=cb: (i, cb)))
        elif r.ndim == 3:
            arr = r
            in_specs.append(pl.BlockSpec((arr.shape[0], tm, arr.shape[2]), lambda i: (0, i, 0)))
        else:
            arr = r
            in_specs.append(pl.BlockSpec((tm, arr.shape[1]), lambda i: (i, 0)))
        args.append(arr)
    for p_ in pars:
        in_specs.append(pl.BlockSpec(p_.shape, lambda i: (0, 0)))
        args.append(p_)
    aliases = {}
    for k, arr in intos:
        aliases[len(args)] = k
        in_specs.append(pl.BlockSpec(memory_space=pl.ANY))
        args.append(arr)
    out_shape = [jax.ShapeDtypeStruct((S, o[2]), o[1]) for o in outs] + [jax.ShapeDtypeStruct(a, F32) for a in accs]
    out_specs = [pl.BlockSpec((tm, o[0]), lambda i, cb=o[3]: (i, cb)) for o in outs] + [pl.BlockSpec(a, lambda i: (0, 0)) for a in accs]
    return pl.pallas_call(
        body, name=name, grid=(S // tm,), in_specs=in_specs, out_specs=out_specs, out_shape=out_shape,
        input_output_aliases=aliases, compiler_params=_cparams(("arbitrary",) if accs else ("parallel",)),
    )(*args)


def _norm(h, eps):
    mu = jnp.mean(h, -1, keepdims=True)
    d = h - mu
    rstd = lax.rsqrt(jnp.mean(d * d, -1, keepdims=True) + eps)
    return d * rstd, rstd


def _norm_bwd(dxh, xh, rstd):
    return rstd * (dxh - jnp.mean(dxh, -1, keepdims=True) - xh * jnp.mean(dxh * xh, -1, keepdims=True))


def _sig(x):
    return 1.0 / (1.0 + jnp.exp(-x))


_GELU_C = math.sqrt(2.0 / math.pi)


def _gelu(x):
    t = jnp.tanh(_GELU_C * (x + 0.044715 * x * x * x))
    return 0.5 * x * (1.0 + t), t


def _gelu_grad(x, t):
    return 0.5 * (1.0 + t) + 0.5 * x * (1.0 - t * t) * _GELU_C * (1.0 + 3 * 0.044715 * x * x)


def _f_ln1(x, mix, g, b):
    h = ALPHA * x + mix
    xh, _ = _norm(h, LN_EPS)
    y = xh * g + b
    return h, y, y


def _f_ln2(x, ffn, z, pp, g, b):
    h = ALPHA * x + ffn + _sig(z) * pp
    xh, _ = _norm(h, LN_EPS)
    y = xh * g + b
    return h, y, y


def _f_ln_bwd(*args):
    *dys, h, g = args
    dy = dys[0]
    for t in dys[1:]:
        dy = dy + t
    xh, rstd = _norm(h, LN_EPS)
    dh = _norm_bwd(dy * g, xh, rstd)
    return dh, dh, jnp.sum(dy * xh, 0, keepdims=True), jnp.sum(dy, 0, keepdims=True)


def _f_sum(*ts):
    r = ts[0]
    for t in ts[1:]:
        r = r + t
    return r


def _f_loss(y, t):
    e = y - t
    return e * (1.0 / D), jnp.sum(e * e, 0, keepdims=True) * (0.5 / D)


def _head_col(c, h):
    lane = lax.broadcasted_iota(jnp.int32, c.shape, 1)
    return jnp.sum(jnp.where(lane == h, c, 0.0), -1, keepdims=True)


def _f_combine(o0, o1, o2, l0, l1, l2):
    lane = lax.broadcasted_iota(jnp.int32, l0.shape, 1)
    parts, lse = [], jnp.zeros(l0.shape, F32)
    for h in range(NH):
        a0, a1, a2 = _head_col(l0, h), _head_col(l1, h), _head_col(l2, h)
        m = jnp.maximum(jnp.maximum(a0, a1), a2)
        e0, e1, e2 = jnp.exp(a0 - m), jnp.exp(a1 - m), jnp.exp(a2 - m)
        den = e0 + e1 + e2
        parts.append((e0 * o0[h].astype(F32) + e1 * o1[h].astype(F32) + e2 * o2[h].astype(F32)) / den)
        lse = jnp.where(lane == h, m + jnp.log(den), lse)
    return jnp.concatenate(parts, axis=1), lse


def _f_delta(da, a):
    lane = lax.broadcasted_iota(jnp.int32, (da.shape[0], HD), 1)
    out = jnp.zeros((da.shape[0], HD), F32)
    for h in range(NH):
        sl = slice(h * HD, (h + 1) * HD)
        s = jnp.sum(da[:, sl].astype(F32) * a[:, sl].astype(F32), -1, keepdims=True)
        out = jnp.where(lane == h, s, out)
    return out


def _f_gate(ap, rp, ga, gr):
    return _sig(ga.astype(F32)) * ap.astype(F32) + _sig(gr.astype(F32)) * rp.astype(F32)


def _f_gate_bwd(dm, ap, rp, ga, gr):
    dm = dm.astype(F32)
    sa, sr = _sig(ga.astype(F32)), _sig(gr.astype(F32))
    dga, dgr = dm * ap.astype(F32) * sa * (1.0 - sa), dm * rp.astype(F32) * sr * (1.0 - sr)
    return dm * sa, dm * sr, jnp.concatenate([dga, dgr], axis=1)


def _f_gn(y, rg, g, b):
    y, rg = y.astype(F32), rg.astype(F32)
    parts = []
    for h in range(RH):
        sl = slice(h * RDV, (h + 1) * RDV)
        xh, _ = _norm(y[:, sl], GN_EPS)
        parts.append(xh * g[:, sl] + b[:, sl])
    return rg * _sig(rg) * jnp.concatenate(parts, axis=1)


def _f_gn_bwd(dr, y, rg, g, b):
    dr, y, rg = dr.astype(F32), y.astype(F32), rg.astype(F32)
    s = _sig(rg)
    d_out = dr * rg * s
    dys, outs, xhs = [], [], []
    for h in range(RH):
        sl = slice(h * RDV, (h + 1) * RDV)
        xh, rstd = _norm(y[:, sl], GN_EPS)
        xhs.append(xh)
        outs.append(xh * g[:, sl] + b[:, sl])
        dys.append(_norm_bwd(d_out[:, sl] * g[:, sl], xh, rstd))
    xh, out = jnp.concatenate(xhs, axis=1), jnp.concatenate(outs, axis=1)
    d_rg = dr * out * s * (1.0 + rg * (1.0 - s))
    return jnp.concatenate(dys, axis=1), d_rg, jnp.sum(d_out * xh, 0, keepdims=True), jnp.sum(d_out, 0, keepdims=True)


def _f_ple_bwd(dh, z, pp):
    s = _sig(z)
    return dh * s, dh * pp * s * (1.0 - s)


QKV = 3 * HD


def _to_tokens(t, d):
    if d == 1:
        return t
    *lead, S, C = t.shape
    n = len(lead)
    perm = tuple(range(n)) + (n + 1, n, n + 2)
    return t.reshape(*lead, d, S // d, C).transpose(perm).reshape(*lead, S, C)


def _to_residues(t, d):
    if d == 1:
        return t
    S, C = t.shape
    return t.reshape(S // d, d, C).transpose(1, 0, 2).reshape(S, C)


def _to_head_residues(t, d):
    S = t.shape[0]
    return t.reshape(S // d, d, NH, HD).transpose(2, 1, 0, 3).reshape(NH, S, HD)


def _w_qkv_specs(g):
    return [pl.BlockSpec((D, D), lambda *i, t=t: (3 * g + t, 0)) for t in range(3)]


def _qkv_fwd(xv, win, g, dil, name):
    Sd = xv.shape[0]
    S = Sd * dil
    tm = min(512, Sd)
    nma = Sd // tm

    def body(a_ref, wq_ref, wk_ref, wv_ref, o_ref):
        a = a_ref[...]
        q, k, v = (_dot(a, w_ref[...], 1, 1).astype(BF16) for w_ref in (wq_ref, wk_ref, wv_ref))
        for h in range(NH):
            sl = slice(h * HD, (h + 1) * HD)
            o_ref[h] = jnp.concatenate([q[:, sl], k[:, sl], v[:, sl]], axis=1)

    return pl.pallas_call(
        body, name=name, grid=(S // tm,),
        in_specs=[pl.BlockSpec((tm, D), lambda i: (i % nma, i // nma))] + _w_qkv_specs(g),
        out_specs=pl.BlockSpec((NH, tm, QKV), lambda i: (0, i, 0)), out_shape=jax.ShapeDtypeStruct((NH, S, QKV), BF16),
        compiler_params=_cparams(("parallel",)),
    )(xv, win, win, win)


def _qkv_dx(dqkv, win, g, dil, name, out_dtype, add=None, after=None):
    S = dqkv.shape[1]
    Sd = S // dil
    tm = min(512, Sd)
    nmo = Sd // tm

    def body(*refs):
        a_ref, wq_ref, wk_ref, wv_ref = refs[:4]
        o_ref = refs[-1]
        acc = None
        for h in range(NH):
            sl = slice(h * HD, (h + 1) * HD)
            w = jnp.concatenate([wq_ref[sl, :], wk_ref[sl, :], wv_ref[sl, :]], axis=0)
            part = _dot(a_ref[h], w, 1, 0)
            acc = part if acc is None else acc + part
        if add is not None:
            acc = acc + refs[4][...]
        o_ref[...] = acc.astype(out_dtype)

    o_spec = pl.BlockSpec((tm, D), lambda i: (i % nmo, i // nmo))
    in_specs = [pl.BlockSpec((NH, tm, QKV), lambda i: (0, i, 0))] + _w_qkv_specs(g)
    args = [dqkv, win, win, win]
    if add is not None:
        assert dil == 1
        in_specs.append(o_spec)
        args.append(add)
    if after is not None:
        in_specs.append(pl.BlockSpec(memory_space=pl.ANY))
        args.append(after)
    return pl.pallas_call(
        body, name=name, grid=(S // tm,), in_specs=in_specs, out_specs=o_spec,
        out_shape=jax.ShapeDtypeStruct((Sd, dil * D), out_dtype), compiler_params=_cparams(("parallel",)),
    )(*args)


GW_IN_BLOCKS = (N_IN // D, NH, HD, D)


def _qkv_dw(dqkv, xv, g, dil, name, into=None):
    S = dqkv.shape[1]
    Sd = S // dil
    tk = min(1024, Sd)
    nkb, nk = Sd // tk, S // tk
    hh = NH // 2

    def body(*refs):
        a_ref, b_ref = refs[:2]
        o_ref, acc_ref = refs[-2:]
        k = pl.program_id(1)

        @pl.when(k == 0)
        def _():
            acc_ref[...] = jnp.zeros_like(acc_ref)

        b = b_ref[...]
        for h in range(hh):
            acc_ref[h * QKV:(h + 1) * QKV, :] += _dot(a_ref[h], b, 0, 0)

        @pl.when(k == nk - 1)
        def _():
            for h in range(hh):
                for t in range(3):
                    o_ref[t, h] = acc_ref[h * QKV + t * HD:h * QKV + (t + 1) * HD, :]

    in_specs = [pl.BlockSpec((hh, tk, QKV), lambda j, k: (j, k, 0)), pl.BlockSpec((tk, D), lambda j, k: (k % nkb, k // nkb))]
    args, aliases = [dqkv, xv], {}
    if into is not None:
        aliases = {2: 0}
        in_specs.append(pl.BlockSpec(memory_space=pl.ANY))
        args.append(into)
    return pl.pallas_call(
        body, name=name, grid=(2, nk), in_specs=in_specs,
        out_specs=pl.BlockSpec((3, hh, HD, D), lambda j, k: (g, j, 0, 0)), out_shape=jax.ShapeDtypeStruct(GW_IN_BLOCKS, F32),
        input_output_aliases=aliases, scratch_shapes=[pltpu.VMEM((hh * QKV, D), F32)],
        compiler_params=_cparams(("parallel", "arbitrary")),
    )(*args)


def _band(nb, first_valid, last_valid=None):
    b = lax.broadcasted_iota(jnp.int32, (nb, SPAN, SPAN), 0)
    row = lax.broadcasted_iota(jnp.int32, (nb, SPAN, SPAN), 1)
    col = lax.broadcasted_iota(jnp.int32, (nb, SPAN, SPAN), 2)
    off = jnp.where(b == 0, jnp.where(first_valid, 0, 2 * SPAN), 0)
    if last_valid is not None:
        off = off + jnp.where(b == nb - 1, jnp.where(last_valid, 0, 2 * SPAN), 0)
    return col <= row, col >= row + off


def _attn_tiles(S, dil):
    Sd = S // dil
    T = min(1024, Sd)
    return Sd, T, T // SPAN, Sd // T


def _attn_fwd(qkv, dil, name):
    S = qkv.shape[1]
    Sd, T, nsub, nib = _attn_tiles(S, dil)
    scale = HD ** -0.5

    def body(c_ref, p_ref, o_ref, l_ref):
        ib, h = pl.program_id(1), pl.program_id(2)
        blk, hal = c_ref[...], p_ref[...]
        q, k, v = blk[:, :HD], blk[:, HD:2 * HD], blk[:, 2 * HD:]
        if nsub > 1:
            kp = jnp.concatenate([hal[:, HD:2 * HD], k[:T - SPAN]], axis=0)
            vp = jnp.concatenate([hal[:, 2 * HD:], v[:T - SPAN]], axis=0)
        else:
            kp, vp = hal[:, HD:2 * HD], hal[:, 2 * HD:]
        q3, k3, v3, kp3, vp3 = (t.reshape(nsub, SPAN, HD) for t in (q, k, v, kp, vp))
        m_cur, m_prev = _band(nsub, ib > 0)
        sc = jnp.where(m_cur, _bdot(q3, k3, 2, 2) * scale, NEG)
        sp = jnp.where(m_prev, _bdot(q3, kp3, 2, 2) * scale, NEG)
        m = jnp.maximum(jnp.max(sc, -1, keepdims=True), jnp.max(sp, -1, keepdims=True))
        pc, pp = jnp.exp(sc - m), jnp.exp(sp - m)
        den = jnp.sum(pc, -1, keepdims=True) + jnp.sum(pp, -1, keepdims=True)
        o = (_bdot(pc.astype(BF16), v3, 2, 1) + _bdot(pp.astype(BF16), vp3, 2, 1)) / den
        o_ref[...] = o.reshape(T, HD).astype(BF16)
        lse = (m + jnp.log(den)).reshape(T, 1)
        lane = lax.broadcasted_iota(jnp.int32, (T, HD), 1)

        @pl.when(h == 0)
        def _():
            l_ref[...] = jnp.zeros_like(l_ref)

        l_ref[...] = jnp.where(lane == h, lse, l_ref[...])

    cur = pl.BlockSpec((None, T, QKV), lambda r, ib, h: (h, r * nib + ib, 0))
    prev = pl.BlockSpec((None, SPAN, QKV), lambda r, ib, h: (h, r * (Sd // SPAN) + jnp.maximum(ib * nsub - 1, 0), 0))
    return pl.pallas_call(
        body, name=name, grid=(dil, nib, NH), in_specs=[cur, prev],
        out_specs=[pl.BlockSpec((None, T, HD), lambda r, ib, h: (h, r * nib + ib, 0)),
                   pl.BlockSpec((T, HD), lambda r, ib, h: (r * nib + ib, 0))],
        out_shape=[jax.ShapeDtypeStruct((NH, S, HD), BF16), jax.ShapeDtypeStruct((S, HD), F32)],
        compiler_params=_cparams(("parallel", "parallel", "arbitrary")),
    )(qkv, qkv)


def _attn_bwd(qkv, d_attn, lse, delta, dil, name):
    S = qkv.shape[1]
    Sd, T, nsub, nib = _attn_tiles(S, dil)
    scale = HD ** -0.5
    ne = nsub + 1

    def body(c_ref, p_ref, n_ref, do_ref, don_ref, l_ref, ln_ref, dl_ref, dln_ref, o_ref):
        ib, h = pl.program_id(1), pl.program_id(2)
        blk, hal, nxt = c_ref[...], p_ref[...], n_ref[...]
        q, k, v = blk[:, :HD], blk[:, HD:2 * HD], blk[:, 2 * HD:]
        do = do_ref[...]
        l, dl = _head_col(l_ref[...], h), _head_col(dl_ref[...], h)
        qe = jnp.concatenate([q, nxt[:, :HD]], axis=0).reshape(ne, SPAN, HD)
        doe = jnp.concatenate([do, don_ref[...]], axis=0).reshape(ne, SPAN, HD)
        le = jnp.concatenate([l, _head_col(ln_ref[...], h)], axis=0).reshape(ne, SPAN, 1)
        dle = jnp.concatenate([dl, _head_col(dln_ref[...], h)], axis=0).reshape(ne, SPAN, 1)
        kpe = jnp.concatenate([hal[:, HD:2 * HD], k], axis=0).reshape(ne, SPAN, HD)
        vpe = jnp.concatenate([hal[:, 2 * HD:], v], axis=0).reshape(ne, SPAN, HD)
        _, m_prev = _band(ne, ib > 0, ib < nib - 1)
        p = jnp.where(m_prev, jnp.exp(_bdot(qe, kpe, 2, 2) * scale - le), 0.0)
        ds = (p * (_bdot(doe, vpe, 2, 2) - dle)).astype(BF16)
        dq = _bdot(ds, kpe, 2, 1)[:nsub]
        dk = _bdot(ds, qe, 1, 1)[1:]
        dv = _bdot(p.astype(BF16), doe, 1, 1)[1:]
        q3, k3, v3, do3 = (t.reshape(nsub, SPAN, HD) for t in (q, k, v, do))
        l3, dl3 = l.reshape(nsub, SPAN, 1), dl.reshape(nsub, SPAN, 1)
        m_cur, _ = _band(nsub, True)
        p = jnp.where(m_cur, jnp.exp(_bdot(q3, k3, 2, 2) * scale - l3), 0.0)
        ds = (p * (_bdot(do3, v3, 2, 2) - dl3)).astype(BF16)
        dq = (dq + _bdot(ds, k3, 2, 1)) * scale
        dk = (dk + _bdot(ds, q3, 1, 1)) * scale
        dv = dv + _bdot(p.astype(BF16), do3, 1, 1)
        o_ref[...] = jnp.concatenate([t.reshape(T, HD) for t in (dq, dk, dv)], axis=1).astype(BF16)

    nb = Sd // SPAN
    row = lambda r, ib: r * nib + ib
    prow = lambda r, ib: r * nb + jnp.maximum(ib * nsub - 1, 0)
    nrow = lambda r, ib: r * nb + jnp.minimum((ib + 1) * nsub, nb - 1)
    cur3 = pl.BlockSpec((None, T, QKV), lambda r, ib, h: (h, row(r, ib), 0))
    prev3 = pl.BlockSpec((None, SPAN, QKV), lambda r, ib, h: (h, prow(r, ib), 0))
    next3 = pl.BlockSpec((None, SPAN, QKV), lambda r, ib, h: (h, nrow(r, ib), 0))
    cur1 = pl.BlockSpec((None, T, HD), lambda r, ib, h: (h, row(r, ib), 0))
    next1 = pl.BlockSpec((None, SPAN, HD), lambda r, ib, h: (h, nrow(r, ib), 0))
    curc = pl.BlockSpec((T, HD), lambda r, ib, h: (row(r, ib), 0))
    nextc = pl.BlockSpec((SPAN, HD), lambda r, ib, h: (nrow(r, ib), 0))
    return pl.pallas_call(
        body, name=name, grid=(dil, nib, NH),
        in_specs=[cur3, prev3, next3, cur1, next1, curc, nextc, curc, nextc], out_specs=cur3,
        out_shape=jax.ShapeDtypeStruct((NH, S, QKV), BF16),
        compiler_params=_cparams(("parallel", "parallel", "parallel")),
    )(qkv, qkv, qkv, d_attn, d_attn, lse, lse, delta, delta)


def _ret_consts():
    lg = np.log1p(-np.exp2(-5.0 - np.arange(RH, dtype=np.float64)))
    idx = np.arange(CH, dtype=np.float64)
    rel = idx[:, None] - idx[None, :]
    intra = np.where(rel >= 0, np.exp(lg[:, None, None] * np.maximum(rel, 0.0)), 0.0)
    qd = np.exp(lg[:, None] * (idx + 1.0))
    kd = np.exp(lg[:, None] * (CH - 1.0 - idx))
    cd = np.exp(lg * CH)
    wide = lambda t: np.broadcast_to(t[:, :, None], (RH, t.shape[1], RDV))
    return (jnp.asarray(intra, F32), jnp.asarray(wide(qd), F32), jnp.asarray(wide(kd), F32),
            jnp.asarray(np.broadcast_to(cd[:, None, None], (RH, 1, RDV)), F32))


def _rot(t, c, s):
    t1, t2 = t[:, :RDK // 2], t[:, RDK // 2:]
    return jnp.concatenate([t1 * c - t2 * s, t1 * s + t2 * c], axis=1)


def _unrot(d, c, s):
    d1, d2 = d[:, :RDK // 2], d[:, RDK // 2:]
    return jnp.concatenate([d1 * c + d2 * s, d2 * c - d1 * s], axis=1)


RCH = 2


def _ret_specs(nmap):
    rows = RCH * CH
    q = pl.BlockSpec((rows, RH * RDK), lambda n: (nmap(n), OFF_RQ // (RH * RDK)))
    k = pl.BlockSpec((rows, RH * RDK), lambda n: (nmap(n), OFF_RK // (RH * RDK)))
    v = pl.BlockSpec((rows, RH * RDV), lambda n: (nmap(n), OFF_RV // (RH * RDV)))
    cs = pl.BlockSpec((rows, RDK // 2), lambda n: (nmap(n), 0))
    dmat = pl.BlockSpec((RH, CH, CH), lambda n: (0, 0, 0))
    dvec = pl.BlockSpec((RH, CH, RDV), lambda n: (0, 0, 0))
    cdv = pl.BlockSpec((RH, 1, RDV), lambda n: (0, 0, 0))
    state = pl.BlockSpec((RH, RCH, RDK, RDV), lambda n: (0, nmap(n), 0, 0))
    out = pl.BlockSpec((rows, RH * RDV), lambda n: (nmap(n), 0))
    return [q, k, v, cs, cs, dmat, dvec, dvec, cdv], state, out


def _ret_fwd(proj, cos, sin, consts):
    S = proj.shape[0]
    nc = S // CH

    def body(q_ref, k_ref, v_ref, c_ref, s_ref, d_ref, qd_ref, kd_ref, cd_ref, o_ref, st_ref, state):
        @pl.when(pl.program_id(0) == 0)
        def _():
            state[...] = jnp.zeros_like(state)

        for ci in range(RCH):
            rows = slice(ci * CH, (ci + 1) * CH)
            c, s = c_ref[rows, :], s_ref[rows, :]
            for h in range(RH):
                qk, vv = slice(h * RDK, (h + 1) * RDK), slice(h * RDV, (h + 1) * RDV)
                qb = _rot(q_ref[rows, qk].astype(F32), c, s).astype(BF16)
                kb = (_rot(k_ref[rows, qk].astype(F32), c, s) * (RDK ** -0.5)).astype(BF16)
                vb = v_ref[rows, vv]
                sb = state[h].astype(BF16)
                st_ref[h, ci] = sb
                a = (_dot(qb, kb, 1, 1) * d_ref[h]).astype(BF16)
                o_ref[rows, vv] = (_dot(a, vb, 1, 0) + _dot(qb, sb, 1, 0) * qd_ref[h]).astype(BF16)
                vk = (vb.astype(F32) * kd_ref[h]).astype(BF16)
                state[h] = cd_ref[h] * state[h] + _dot(kb, vk, 0, 0)

    ins, state_spec, out_spec = _ret_specs(lambda n: n)
    return pl.pallas_call(
        body, name="ret_fwd", grid=(nc // RCH,), in_specs=ins, out_specs=[out_spec, state_spec],
        out_shape=[jax.ShapeDtypeStruct((S, RH * RDV), BF16), jax.ShapeDtypeStruct((RH, nc, RDK, RDV), BF16)],
        scratch_shapes=[pltpu.VMEM((RH, RDK, RDV), F32)],
        compiler_params=_cparams(("arbitrary",)),
    )(proj, proj, proj, cos, sin, *consts)


def _ret_bwd(proj, cos, sin, consts, states, d_ret, d_rest):
    S = proj.shape[0]
    nc = S // CH

    def body(q_ref, k_ref, v_ref, c_ref, s_ref, d_ref, qd_ref, kd_ref, cd_ref, st_ref, do_ref, _, o_ref, dstate):
        @pl.when(pl.program_id(0) == 0)
        def _():
            dstate[...] = jnp.zeros_like(dstate)

        for ci in reversed(range(RCH)):
            rows = slice(ci * CH, (ci + 1) * CH)
            c, s = c_ref[rows, :], s_ref[rows, :]
            for h in range(RH):
                qk, vv = slice(h * RDK, (h + 1) * RDK), slice(h * RDV, (h + 1) * RDV)
                qb = _rot(q_ref[rows, qk].astype(F32), c, s).astype(BF16)
                kb = (_rot(k_ref[rows, qk].astype(F32), c, s) * (RDK ** -0.5)).astype(BF16)
                vb, sb, do = v_ref[rows, vv], st_ref[h, ci], do_ref[rows, vv]
                dmat, qd, kd = d_ref[h], qd_ref[h], kd_ref[h]
                a = (_dot(qb, kb, 1, 1) * dmat).astype(BF16)
                doq = (do.astype(F32) * qd).astype(BF16)
                dsb = dstate[h].astype(BF16)
                vk = (vb.astype(F32) * kd).astype(BF16)
                o_ref[rows, OFF_RV + h * RDV:OFF_RV + (h + 1) * RDV] = (_dot(a, do, 0, 0) + _dot(kb, dsb, 1, 0) * kd).astype(BF16)
                da = (_dot(do, vb, 1, 1) * dmat).astype(BF16)
                dq = _dot(da, kb, 1, 0) + _dot(doq, sb, 1, 1)
                dk = (_dot(da, qb, 0, 0) + _dot(vk, dsb, 1, 1)) * (RDK ** -0.5)
                o_ref[rows, OFF_RQ + h * RDK:OFF_RQ + (h + 1) * RDK] = _unrot(dq, c, s).astype(BF16)
                o_ref[rows, OFF_RK + h * RDK:OFF_RK + (h + 1) * RDK] = _unrot(dk, c, s).astype(BF16)
                dstate[h] = cd_ref[h] * dstate[h] + _dot(qb, doq, 0, 0)

    nsteps = nc // RCH
    rev = lambda n: nsteps - 1 - n
    ins, state_spec, out_spec = _ret_specs(rev)
    return pl.pallas_call(
        body, name="ret_bwd", grid=(nsteps,), in_specs=ins + [state_spec, out_spec, pl.BlockSpec(memory_space=pl.ANY)],
        out_specs=pl.BlockSpec((RCH * CH, OFF_RG), lambda n: (rev(n), 0)),
        out_shape=jax.ShapeDtypeStruct(d_rest.shape, BF16), input_output_aliases={11: 0},
        scratch_shapes=[pltpu.VMEM((RH, RDK, RDV), F32)],
        compiler_params=_cparams(("arbitrary",)),
    )(proj, proj, proj, cos, sin, *consts, states, d_ret, d_rest)


CW = 256
HALO = 16


def _shift_down(v, halo, k):
    rolled = pltpu.roll(v, k, 0)
    hr = pltpu.roll(halo, k, 0)[0:8]
    row = lax.broadcasted_iota(jnp.int32, hr.shape, 0)
    return jnp.concatenate([jnp.where(row < k, hr, rolled[0:8]), rolled[8:]], axis=0)


def _shift_up(v, halo, k):
    T = v.shape[0]
    rolled = pltpu.roll(v, T - k, 0)
    hr = pltpu.roll(halo, 8 - k, 0)[0:8]
    row = lax.broadcasted_iota(jnp.int32, hr.shape, 0)
    return jnp.concatenate([rolled[:T - 8], jnp.where(row >= 8 - k, hr, rolled[T - 8:])], axis=0)


def _conv_taps(h_ref, hp_ref, first):
    h = h_ref[...].astype(F32)
    hp = hp_ref[...].astype(F32) * jnp.where(first, 0.0, 1.0)
    return _shift_down(h, hp, 2), _shift_down(h, hp, 1), h


def _conv_specs(S, T, cw=CW):
    nj = DFF // cw
    cur = pl.BlockSpec((T, cw), lambda j, i: (i, j))
    prev = pl.BlockSpec((HALO, cw), lambda j, i: (jnp.maximum(i * (T // HALO) - 1, 0), j))
    nxt = pl.BlockSpec((HALO, cw), lambda j, i: (jnp.minimum((i + 1) * (T // HALO), S // HALO - 1), j))
    w = pl.BlockSpec((3, cw), lambda j, i: (0, j))
    b = pl.BlockSpec((1, cw), lambda j, i: (0, j))
    return nj, cur, prev, nxt, w, b


def _conv_fwd(hg, hu, wg, wu, bg, bu):
    S = hg.shape[0]
    T = min(1024, S)
    nj, cur, prev, _, w, b = _conv_specs(S, T)

    def body(hg_ref, hu_ref, hgp_ref, hup_ref, wg_ref, wu_ref, bg_ref, bu_ref, o_ref):
        first = pl.program_id(1) == 0
        g2, g1, g0 = _conv_taps(hg_ref, hgp_ref, first)
        u2, u1, u0 = _conv_taps(hu_ref, hup_ref, first)
        cg = wg_ref[0:1, :] * g2 + wg_ref[1:2, :] * g1 + wg_ref[2:3, :] * g0 + bg_ref[...]
        cu = wu_ref[0:1, :] * u2 + wu_ref[1:2, :] * u1 + wu_ref[2:3, :] * u0 + bu_ref[...]
        o_ref[...] = (_gelu(cg)[0] * cu).astype(BF16)

    return pl.pallas_call(
        body, name="conv_fwd", grid=(nj, S // T), in_specs=[cur, cur, prev, prev, w, w, b, b], out_specs=cur,
        out_shape=jax.ShapeDtypeStruct((S, DFF), BF16), compiler_params=_cparams(("parallel", "parallel")),
    )(hg, hu, hg, hu, wg, wu, bg, bu)


def _conv_bwd_pre(d_act, hg, hu, wg, wu, bg, bu):
    S = hg.shape[0]
    T = min(1024, S)
    nj, cur, prev, _, w, b = _conv_specs(S, T)

    def body(da_ref, hg_ref, hu_ref, hgp_ref, hup_ref, wg_ref, wu_ref, bg_ref, bu_ref,
             dcg_ref, dcu_ref, gwg_ref, gwu_ref, gbg_ref, gbu_ref):
        first = pl.program_id(1) == 0
        g2, g1, g0 = _conv_taps(hg_ref, hgp_ref, first)
        u2, u1, u0 = _conv_taps(hu_ref, hup_ref, first)
        cg = wg_ref[0:1, :] * g2 + wg_ref[1:2, :] * g1 + wg_ref[2:3, :] * g0 + bg_ref[...]
        cu = wu_ref[0:1, :] * u2 + wu_ref[1:2, :] * u1 + wu_ref[2:3, :] * u0 + bu_ref[...]
        da = da_ref[...].astype(F32)
        gl, t = _gelu(cg)
        dcg = da * cu * _gelu_grad(cg, t)
        dcu = da * gl
        dcg_ref[...] = dcg.astype(BF16)
        dcu_ref[...] = dcu.astype(BF16)

        @pl.when(first)
        def _():
            for r in (gwg_ref, gwu_ref, gbg_ref, gbu_ref):
                r[...] = jnp.zeros_like(r)

        for r, d, taps in ((gwg_ref, dcg, (g2, g1, g0)), (gwu_ref, dcu, (u2, u1, u0))):
            for j in range(3):
                r[j:j + 1, :] += jnp.sum(d * taps[j], 0, keepdims=True)
        gbg_ref[...] += jnp.sum(dcg, 0, keepdims=True)
        gbu_ref[...] += jnp.sum(dcu, 0, keepdims=True)

    return pl.pallas_call(
        body, name="conv_bwd_pre", grid=(nj, S // T), in_specs=[cur, cur, cur, prev, prev, w, w, b, b],
        out_specs=[cur, cur, w, w, b, b],
        out_shape=[jax.ShapeDtypeStruct((S, DFF), BF16)] * 2 + [jax.ShapeDtypeStruct((3, DFF), F32)] * 2
        + [jax.ShapeDtypeStruct((1, DFF), F32)] * 2,
        compiler_params=_cparams(("parallel", "arbitrary")),
    )(d_act, hg, hu, hg, hu, wg, wu, bg, bu)


def _conv_bwd_in(dc, w, name):
    S = dc.shape[0]
    T = min(512, S)
    nj, cur, _, nxt, wspec, _ = _conv_specs(S, T, DFF // 2)
    nt = S // T

    def body(dc_ref, dn_ref, w_ref, o_ref):
        d = dc_ref[...].astype(F32)
        dn = dn_ref[...].astype(F32) * jnp.where(pl.program_id(1) == nt - 1, 0.0, 1.0)
        o_ref[...] = (w_ref[2:3, :] * d + w_ref[1:2, :] * _shift_up(d, dn, 1) + w_ref[0:1, :] * _shift_up(d, dn, 2)).astype(BF16)

    return pl.pallas_call(
        body, name=name, grid=(nj, nt), in_specs=[cur, nxt, wspec], out_specs=cur,
        out_shape=jax.ShapeDtypeStruct((S, DFF), BF16), compiler_params=_cparams(("parallel", "parallel")),
    )(dc, dc, w)


def _adam_math(g, w, m, v):
    m = B1 * m + (1.0 - B1) * g
    v = B2 * v + (1.0 - B2) * (g * g)
    m_hat = m / (1.0 - B1 ** STEP)
    v_hat = v / (1.0 - B2 ** STEP)
    return -LR * (m_hat / (jnp.sqrt(v_hat) + EPS) + WD * w), m, v


def _reduce_tail(chip32, far, chip, name, wmv=None):
    L = len(chip32)
    _, R, C = chip32[0].shape
    tr = _tile(R, 256, 16)
    nr = R // tr

    def body(chip_ref, *refs):
        own_refs, far_refs, rest = refs[:L], refs[L:2 * L], refs[2 * L:]
        outs = rest[3:] if wmv else rest
        for ll in range(L):
            @pl.when(pl.program_id(0) == ll)
            def _(ll=ll):
                g = own_refs[ll][...]
                for s in range(3):
                    g = g + far_refs[ll][s].astype(F32)
                outs[0][...] = g
                if wmv:
                    outs[1][...], outs[2][...], outs[3][...] = _adam_math(g, rest[0][...], rest[1][...], rest[2][...])

    def rows(ll):
        return lambda l, i: jnp.where(l == ll, i, jnp.where(l < ll, 0, nr - 1))

    blk = pl.BlockSpec((None, tr, C), lambda l, i, ch: (l, i, 0))
    in_specs = [pl.BlockSpec((None, tr, C), lambda l, i, ch, f=rows(ll): (ch[0], f(l, i), 0)) for ll in range(L)]
    in_specs += [pl.BlockSpec((3, tr, C), lambda l, i, ch, f=rows(ll): (0, f(l, i), 0)) for ll in range(L)]
    args = list(chip32) + list(far)
    n_out = 1
    if wmv:
        in_specs += [blk] * 3
        args += list(wmv)
        n_out = 4
    return pl.pallas_call(
        body, name=name,
        grid_spec=pltpu.PrefetchScalarGridSpec(num_scalar_prefetch=1, grid=(L, nr), in_specs=in_specs, out_specs=[blk] * n_out),
        out_shape=[jax.ShapeDtypeStruct((L, R, C), F32)] * n_out, compiler_params=_cparams(("arbitrary", "arbitrary")),
    )(chip, *args)


def _adamw(g, w, m, v, name):
    R, C = g.shape
    tr = _tile(R, 128, 8)

    def body(g_ref, w_ref, m_ref, v_ref, d_ref, nm_ref, nv_ref):
        d_ref[...], nm_ref[...], nv_ref[...] = _adam_math(g_ref[...], w_ref[...], m_ref[...], v_ref[...])

    blk = pl.BlockSpec((tr, C), lambda i: (i, 0))
    return pl.pallas_call(
        body, name=name, grid=(R // tr,), in_specs=[blk] * 4, out_specs=[blk] * 3,
        out_shape=[jax.ShapeDtypeStruct(g.shape, F32)] * 3, compiler_params=_cparams(("parallel",)),
    )(g, w, m, v)


def _pair_sum(x, recv, core, name):
    _, R, C = x.shape
    tr = _tile(R, 600, 16)

    def body(core_ref, x_ref, r_ref, o32_ref, o16_ref):
        s = x_ref[...] + r_ref[...]
        o32_ref[...] = s
        o16_ref[...] = s.astype(BF16)

    blk = pl.BlockSpec((None, tr, C), lambda q, i, c: (q, i, 0))
    mine = pl.BlockSpec((None, None, tr, C), lambda q, i, c: (q, c[0], i, 0))
    return pl.pallas_call(
        body, name=name,
        grid_spec=pltpu.PrefetchScalarGridSpec(num_scalar_prefetch=1, grid=(4, R // tr), in_specs=[mine, blk], out_specs=[blk, blk]),
        out_shape=[jax.ShapeDtypeStruct((4, R, C), F32), jax.ShapeDtypeStruct((4, R, C), BF16)],
        compiler_params=_cparams(("parallel", "parallel")),
    )(core, x.reshape(4, 2, R, C), recv)


def _sum_slots(x, name):
    def body(x_ref, o_ref):
        g = x_ref[0]
        for s in range(1, x.shape[0]):
            g = g + x_ref[s]
        o_ref[...] = g

    return pl.pallas_call(body, name=name, out_shape=jax.ShapeDtypeStruct(x.shape[1:], F32))(x)


MESH = pl.DeviceIdType.MESH
_HBM = pl.BlockSpec(memory_space=pltpu.HBM)


def _dma_sems(n):
    return pltpu.SemaphoreType.DMA((n,))


def _gather_many(xs, name):
    n = len(xs)

    def body(*refs):
        x_refs, out_refs = refs[:n], refs[n:2 * n]
        send_sems, recv_sems, local_sems = refs[2 * n:]
        ax, ay, ac = lax.axis_index("x"), lax.axis_index("y"), lax.axis_index("c")
        me, sibling = (ax, ay, ac), (ax, ay, 1 - ac)
        chips = [(1 - ax, ay), (ax, 1 - ay), (1 - ax, 1 - ay)]

        def copy(a, k, block, to, own=False):
            slot = out_refs[a].at[4 * block[0] + 2 * block[1] + block[2]]
            return pltpu.make_async_remote_copy(
                src_ref=x_refs[a] if own else slot, dst_ref=slot, send_sem=send_sems.at[7 * a + k],
                recv_sem=recv_sems.at[7 * a + k], device_id=to, device_id_type=MESH)

        mine = [pltpu.make_async_copy(x_refs[a], out_refs[a].at[4 * ax + 2 * ay + ac], local_sems.at[a]) for a in range(n)]
        first = [copy(a, 0, me, sibling, own=True) for a in range(n)]
        first += [copy(a, 1 + j, me, (*chip, ac), own=True) for j, chip in enumerate(chips) for a in range(n)]
        for cp in mine + first:
            cp.start()
        passed = []
        for j, chip in enumerate(chips):
            for a in range(n):
                copy(a, 1 + j, (*chip, ac), me).wait_recv()
                cp = copy(a, 4 + j, (*chip, ac), sibling)
                cp.start()
                passed.append(cp)
        for a in range(n):
            copy(a, 0, sibling, me).wait_recv()
            for j, chip in enumerate(chips):
                copy(a, 4 + j, (*chip, 1 - ac), me).wait_recv()
        for cp in first + passed:
            cp.wait_send()
        for cp in mine:
            cp.wait()

    return pl.pallas_call(
        body, name=name, out_shape=[jax.ShapeDtypeStruct((N_DEV,) + x.shape, x.dtype) for x in xs],
        in_specs=[_HBM] * n, out_specs=[_HBM] * n, scratch_shapes=[_dma_sems(7 * n), _dma_sems(7 * n), _dma_sems(n)],
    )(*xs)


_SEM = pl.BlockSpec(memory_space=pltpu.SEMAPHORE)
_EFFECT = pltpu.SideEffectType.DATAFLOW_SIDE_EFFECTING


def _peer(k):
    ax, ay, ac = lax.axis_index("x"), lax.axis_index("y"), lax.axis_index("c")
    px = 1 - ax if k & 4 else ax
    py = 1 - ay if k & 2 else ay
    pc = 1 - ac if k & 1 else ac
    return (px, py, pc), 4 * px + 2 * py + pc


def _build_gather(x_refs, land_refs, send_sems, recv_sems, waiting):
    _, me = _peer(0)
    copies = []
    for a in range(len(x_refs)):
        for k in range(1, N_DEV):
            peer, slot = _peer(k)
            copies.append(pltpu.make_async_remote_copy(
                src_ref=x_refs[a], dst_ref=land_refs[a].at[slot if waiting else me], send_sem=send_sems.at[7 * a + k - 1],
                recv_sem=recv_sems.at[7 * a + k - 1], device_id=peer, device_id_type=MESH))
    return copies


def _build_cores(x_refs, land_refs, send_sems, recv_sems, waiting):
    ax, ay, ac = lax.axis_index("x"), lax.axis_index("y"), lax.axis_index("c")
    copies = []
    for a in range(len(x_refs)):
        for q in range(4):
            copies.append(pltpu.make_async_remote_copy(
                src_ref=x_refs[a].at[2 * q + 1 - ac], dst_ref=land_refs[a].at[q], send_sem=send_sems.at[4 * a + q],
                recv_sem=recv_sems.at[4 * a + q], device_id=(ax, ay, 1 - ac), device_id_type=MESH))
    return copies


def _build_chips(p_refs, land_refs, send_sems, recv_sems, waiting):
    ax, ay, ac = lax.axis_index("x"), lax.axis_index("y"), lax.axis_index("c")
    copies = []
    for a in range(len(p_refs)):
        for k in range(1, 4):
            px = 1 - ax if k & 2 else ax
            py = 1 - ay if k & 1 else ay
            copies.append(pltpu.make_async_remote_copy(
                src_ref=p_refs[a].at[2 * px + py], dst_ref=land_refs[a].at[k - 1], send_sem=send_sems.at[3 * a + k - 1],
                recv_sem=recv_sems.at[3 * a + k - 1], device_id=(px, py, ac), device_id_type=MESH))
    return copies


_EXCHANGES = {"gather": (_build_gather, 7, N_DEV), "cores": (_build_cores, 4, 4), "chips": (_build_chips, 3, 3)}


def _exchange_start(kind, xs, lands, name, after=None):
    build, per, _ = _EXCHANGES[kind]
    n = len(xs)

    def body(*refs):
        for cp in build(refs[:n], refs[n:2 * n], refs[-2 * n - 3], refs[-2 * n - 2], False):
            cp.start()
        refs[-1][...] = jnp.zeros_like(refs[-1])

    hbm = lambda t: pltpu.HBM(t.shape, t.dtype)
    args = [pltpu.with_memory_space_constraint(t, pltpu.HBM) for t in list(xs) + list(lands)]
    in_specs = [_HBM] * (2 * n)
    if after is not None:
        args.append(after)
        in_specs.append(pl.BlockSpec(memory_space=pl.ANY))
    outs = pl.pallas_call(
        body, name=name,
        out_shape=(_dma_sems(per * n), _dma_sems(per * n), *[hbm(t) for t in xs], *[hbm(t) for t in lands],
                   jax.ShapeDtypeStruct((8, 128), F32)),
        in_specs=in_specs, out_specs=(_SEM, _SEM, *[_HBM] * (2 * n), pl.BlockSpec(memory_space=pltpu.VMEM)),
        input_output_aliases={a: 2 + a for a in range(2 * n)},
        compiler_params=pltpu.CompilerParams(has_side_effects=_EFFECT),
    )(*args)
    return (kind, outs[0], outs[1], outs[2:2 + n], outs[2 + n:2 + 2 * n]), outs[-1]


def _exchange_wait(flight, after, name):
    kind, send_sems, recv_sems, xs, lands = flight
    build = _EXCHANGES[kind][0]
    n = len(xs)

    def body(*refs):
        for cp in build(refs[:n], refs[n:2 * n], refs[2 * n], refs[2 * n + 1], True):
            cp.wait_send()
            cp.wait_recv()

    hbm = lambda t: pltpu.HBM(t.shape, t.dtype)
    outs = pl.pallas_call(
        body, name=name, out_shape=(*[hbm(t) for t in xs], *[hbm(t) for t in lands]),
        in_specs=[_HBM] * (2 * n) + [_SEM, _SEM, pl.BlockSpec(memory_space=pl.ANY)], out_specs=[_HBM] * (2 * n),
        input_output_aliases={a: a for a in range(2 * n)}, compiler_params=pltpu.CompilerParams(has_side_effects=_EFFECT),
    )(*xs, *lands, send_sems, recv_sems, after)
    return outs[:n], outs[n:]


def _x_view(xb, d):
    return xb if d == 1 else xb.reshape(xb.shape[0] // d, d * xb.shape[1])


def _layer_fwd(x, xb, p, w, cos, sin, rconsts, late=None):
    proj = _mm(xb, w["win"], tb=True, b_rows=(N_ATT, N_REST), name="mm_proj", out_dtype=BF16)
    qkvs, ogs, lgs = [], [], []
    for g, dil in enumerate(DILATIONS):
        qkv = _qkv_fwd(_x_view(xb, dil), w["win"], g, dil, f"mm_qkv{g}")
        o, l = _attn_fwd(qkv, dil, f"attn_fwd_g{g}")
        qkvs.append(qkv)
        ogs.append(_to_tokens(o, dil))
        lgs.append(_to_tokens(l, dil))
    attn, lse = _rowwise(_f_combine, ogs + lgs, [], [(D, BF16), (HD, F32)], [], name="attn_combine")
    ret_raw, states = _ret_fwd(proj, cos, sin, rconsts)
    rg_win = (proj, RH * RDV, OFF_RG // (RH * RDV))
    ga_win, gr_win = (proj, D, OFF_GA // D), (proj, D, OFF_GR // D)
    (r,) = _rowwise(_f_gn, [ret_raw, rg_win], [w["ret_gn_g"], w["ret_gn_b"]], [(RH * RDV, BF16)], [], name="gn_fwd", tm=256)
    if late is not None:
        w = {**w, **late(r)}
    ap = _mm(attn, w["w_attn_proj"], name="mm_attn_proj", out_dtype=BF16)
    rp = _mm(r, w["w_ret_proj"], name="mm_ret_proj", out_dtype=BF16, tk=2048)
    (merged,) = _rowwise(_f_gate, [ap, rp, ga_win, gr_win], [], [(D, BF16)], [], name="gate_fwd")
    mix = _mm(merged, w["w_out"], name="mm_out")
    h1, x1, x1b = _rowwise(_f_ln1, [x, mix], [w["ln1_g"], w["ln1_b"]], [(D, F32), (D, F32), (D, BF16)], [], name="ln1_fwd")
    z = _mm(x1b, w["w_ple_gate"], name="mm_ple_gate")
    pp = _mm(p, w["w_ple_proj"], tb=True, name="mm_ple_proj")
    hg = _mm(x1b, w["w_up"], tb=True, b_rows=(0, DFF), name="mm_up_g", out_dtype=BF16, tm=512, tn=DFF)
    hu = _mm(x1b, w["w_up"], tb=True, b_rows=(DFF, DFF), name="mm_up_u", out_dtype=BF16, tm=512, tn=DFF)
    act = _conv_fwd(hg, hu, w["conv_wg"], w["conv_wu"], w["conv_bg"], w["conv_bu"])
    ffn = _mm(act, w["w_down"], name="mm_down", tm=512, tk=DFF)
    h2, x2, x2b = _rowwise(_f_ln2, [x1, ffn, z, pp], [w["ln2_g"], w["ln2_b"]], [(D, F32), (D, F32), (D, BF16)], [], name="ln2_fwd")
    saved = dict(xb=xb, proj=proj, qkvs=qkvs, attn=attn, lse=lse, ret_raw=ret_raw, states=states, r=r, ap=ap, rp=rp,
                 merged=merged, h1=h1, x1b=x1b, z=z, pp=pp, hg=hg, hu=hu, act=act, h2=h2, p=p)
    return x2, x2b, saved, w


def _after(fn, token):
    return fn if token is None else (lambda *a: fn(*a[:-1]))


def _layer_bwd(dys, w, sv, cos, sin, rconsts, hooks):
    gr = {}
    proj = sv["proj"]
    call = lambda key, *a: hooks[key](*a) if key in hooks else None
    held = lambda token: [] if token is None else [token]
    token = hooks.get("token")
    dh2, dh2b, gr["ln2_g"], gr["ln2_b"] = _rowwise(_after(_f_ln_bwd, token), list(dys) + [sv["h2"]], [w["ln2_g"]] + held(token),
                                                   [(D, F32), (D, BF16)], [(1, D), (1, D)], name="ln2_bwd")
    d_act = _mm(dh2b, w["w_down"], tb=True, name="mm_down_dx", out_dtype=BF16, tm=512, tn=DFF)
    gr["w_down"] = _mm(sv["act"], dh2b, ta=True, name="mm_down_dw", tm=DFF // 2)
    dcg, dcu, gwg, gwu, gbg, gbu = _conv_bwd_pre(d_act, sv["hg"], sv["hu"], w["conv_wg"], w["conv_wu"], w["conv_bg"], w["conv_bu"])
    token = call("after_ffn", dcg)
    gr["conv_w"] = jnp.concatenate([gwg, gwu], axis=1)
    gr["conv_b"] = jnp.concatenate([gbg, gbu], axis=1)
    dhg = _conv_bwd_in(dcg, w["conv_wg"], "conv_bwd_in_g")
    dhu = _conv_bwd_in(dcu, w["conv_wu"], "conv_bwd_in_u")
    gw_up = _mm(dhg, sv["x1b"], ta=True, name="mm_up_g_dw", tm=DFF // 2, out_rows=(0, 2 * DFF))
    gr["w_up"] = _mm(dhu, sv["x1b"], ta=True, name="mm_up_u_dw", tm=DFF // 2, out_rows=(DFF, 2 * DFF), into=gw_up)
    dx1 = _mm(dhg, w["w_up"], b_rows=(0, DFF), name="mm_up_g_dx", add=dh2, add_scale=ALPHA, tm=512, tk=DFF)
    dx1 = _mm(dhu, w["w_up"], b_rows=(DFF, DFF), name="mm_up_u_dx", add=dx1, tm=512, tk=DFF)
    dpp, dz = _rowwise(_f_ple_bwd, [dh2, sv["z"], sv["pp"]], [], [(D, BF16), (D, BF16)], [], name="ple_bwd")
    gr["w_ple_proj"] = _mm(dpp, sv["p"], ta=True, name="mm_ple_proj_dw")
    gr["w_ple_gate"] = _mm(sv["x1b"], dz, ta=True, name="mm_ple_gate_dw")
    dx1 = _mm(dz, w["w_ple_gate"], tb=True, name="mm_ple_gate_dx", add=dx1)
    dh1, dh1b, gr["ln1_g"], gr["ln1_b"] = _rowwise(_after(_f_ln_bwd, token), [dx1, sv["h1"]], [w["ln1_g"]] + held(token),
                                                   [(D, F32), (D, BF16)], [(1, D), (1, D)], name="ln1_bwd")
    d_merged = _mm(dh1b, w["w_out"], tb=True, name="mm_out_dx", out_dtype=BF16)
    gr["w_out"] = _mm(sv["merged"], dh1b, ta=True, name="mm_out_dw")
    rg_win = (proj, RH * RDV, OFF_RG // (RH * RDV))
    ga_win, gr_win = (proj, D, OFF_GA // D), (proj, D, OFF_GR // D)
    dap, drp, d_rest = _rowwise(_f_gate_bwd, [d_merged, sv["ap"], sv["rp"], ga_win, gr_win], [],
                                [(D, BF16), (D, BF16), (2 * D, BF16, N_REST, OFF_GA // (2 * D), None)], [], name="gate_bwd")
    d_attn = _mm(dap, w["w_attn_proj"], tb=True, name="mm_attn_proj_dx", out_dtype=BF16)
    gr["w_attn_proj"] = _mm(sv["attn"], dap, ta=True, name="mm_attn_proj_dw")
    d_r = _mm(drp, w["w_ret_proj"], tb=True, name="mm_ret_proj_dx", out_dtype=BF16, tn=2048)
    gr["w_ret_proj"] = _mm(sv["r"], drp, ta=True, name="mm_ret_proj_dw", tm=2048)
    token = call("early_grads", gr)
    d_ret, d_rest, gr["ret_gn_g"], gr["ret_gn_b"] = _rowwise(
        _after(_f_gn_bwd, token), [d_r, sv["ret_raw"], rg_win], [w["ret_gn_g"], w["ret_gn_b"]] + held(token),
        [(RH * RDV, BF16), (RH * RDV, BF16, N_REST, OFF_RG // (RH * RDV), d_rest)],
        [(1, RH * RDV), (1, RH * RDV)], name="gn_bwd", tm=256)
    d_rest = _ret_bwd(proj, cos, sin, rconsts, sv["states"], d_ret, d_rest)
    token = call("after_ret", d_rest)
    (delta,) = _rowwise(_after(_f_delta, token), [d_attn, sv["attn"]], held(token), [(HD, F32)], [], name="attn_delta")
    gw_in, dqkvs = None, []
    for g, dil in enumerate(DILATIONS):
        dqkvs.append(_attn_bwd(sv["qkvs"][g], _to_head_residues(d_attn, dil), _to_residues(sv["lse"], dil),
                               _to_residues(delta, dil), dil, f"attn_bwd_g{g}"))
        gw_in = _qkv_dw(dqkvs[g], _x_view(sv["xb"], dil), g, dil, f"mm_qkv{g}_dw", into=gw_in)
    gw_in = _mm(d_rest, sv["xb"], ta=True, name="mm_proj_dw", out_rows=(N_ATT, N_IN), into=gw_in, blocks8=True)
    gr["w_in"] = gw_in.reshape(N_IN, D)
    token = call("w_in_ready", gr["w_in"])
    dx0 = _mm(d_rest, w["win"], b_rows=(N_ATT, N_REST), name="mm_proj_dx", add=dh1, add_scale=ALPHA, after=token)
    dx_parts = []
    for g, dil in enumerate(DILATIONS):
        if dil == 1:
            dx0 = _qkv_dx(dqkvs[g], w["win"], g, dil, f"mm_qkv{g}_dx", F32, add=dx0)
            token = call("after_dx0", dx0)
        else:
            dx_parts.append(_qkv_dx(dqkvs[g], w["win"], g, dil, f"mm_qkv{g}_dx", BF16, after=token).reshape(dx0.shape))
    return [dx0] + dx_parts, gr


def _local_step(x, p, positions, target, ws, own_hooks=None, on_grads=None):
    half = RDK // 2
    freq = jnp.power(ROPE_BASE, -jnp.arange(half, dtype=F32) / half)
    ang = positions.astype(F32)[:, None] * freq[None, :]
    cos, sin = jnp.cos(ang), jnp.sin(ang)
    rconsts = _ret_consts()
    xb = x.astype(BF16)
    saved, ws = [], list(ws)
    for l in range(DEPTH):
        first, late = ws[l] if isinstance(ws[l], tuple) else (ws[l], None)
        if callable(first):
            first = first(x)
        x, xb, sv, ws[l] = _layer_fwd(x, xb, p[l], first, cos, sin, rconsts, late)
        saved.append(sv)
    dy, loss_vec = _rowwise(_f_loss, [x, target], [], [(D, F32)], [(1, D)], name="loss")
    dys, grads = [dy], [None] * DEPTH
    from_above = {}
    for l in reversed(range(DEPTH)):
        hooks = {**from_above, **(own_hooks(l) if own_hooks else {})}
        dys, grads[l] = _layer_bwd(dys, ws[l], saved[l], cos, sin, rconsts, hooks)
        from_above = on_grads(l, grads[l]) if on_grads else {}
    (grad_x,) = _rowwise(_f_sum, dys, [], [(D, F32)], [], name="grad_x_sum")
    return loss_vec, grad_x, grads


def _pack_rows(arrs):
    parts, where, off = [], [], 0
    for t in arrs:
        t = t.reshape(-1, D)
        rows = t.shape[0]
        padded = -(-rows // 8) * 8
        parts.append(jnp.pad(t, ((0, padded - rows), (0, 0))))
        where.append((off, rows))
        off += padded
    return jnp.concatenate(parts, axis=0), where


FIRST = ("w_in",)
LATER = tuple(n for n in BIG if n not in FIRST)


def _first_weights(g, l, W):
    w = dict(win=g["w_in"].reshape(N_IN, D))
    for n in ("ret_gn_g", "ret_gn_b", "ln1_g", "ln1_b", "ln2_g", "ln2_b"):
        w[n] = W[n][l][None, :]
    return w


def _later_weights(g, l, conv_w_all, conv_b):
    w = dict(w_up=g["w_up"].reshape(2 * DFF, D), w_ple_proj=g["w_ple_proj"].reshape(D, PLE),
             w_attn_proj=g["w_attn_proj"].reshape(D, D), w_ret_proj=g["w_ret_proj"].reshape(RH * RDV, D),
             w_out=g["w_out"].reshape(D, D), w_down=g["w_down"].reshape(DFF, D), w_ple_gate=g["w_ple_gate"].reshape(D, D))
    w["conv_wg"], w["conv_wu"] = conv_w_all[l][:, :DFF], conv_w_all[l][:, DFF:]
    w["conv_bg"], w["conv_bu"] = conv_b[l][None, :DFF], conv_b[l][None, DFF:]
    return w


def _layer_weights(g, l, conv_w_all, conv_b, W):
    return {**_first_weights(g, l, W), **_later_weights(g, l, conv_w_all, conv_b)}


def kernel(x, p, positions, w_in, w_attn_proj, w_ret_proj, ret_gn_g, ret_gn_b, w_out, ln1_g, ln1_b, w_up, conv_w, conv_b, w_down, w_ple_gate, w_ple_proj, ln2_g, ln2_b, loss_target, m_w_in, m_w_attn_proj, m_w_ret_proj, m_ret_gn_g, m_ret_gn_b, m_w_out, m_ln1_g, m_ln1_b, m_w_up, m_conv_w, m_conv_b, m_w_down, m_w_ple_gate, m_w_ple_proj, m_ln2_g, m_ln2_b, v_w_in, v_w_attn_proj, v_w_ret_proj, v_ret_gn_g, v_ret_gn_b, v_w_out, v_ln1_g, v_ln1_b, v_w_up, v_conv_w, v_conv_b, v_w_down, v_w_ple_gate, v_w_ple_proj, v_ln2_g, v_ln2_b):
    W = dict(w_in=w_in, w_attn_proj=w_attn_proj, w_ret_proj=w_ret_proj, ret_gn_g=ret_gn_g, ret_gn_b=ret_gn_b, w_out=w_out,
             ln1_g=ln1_g, ln1_b=ln1_b, w_up=w_up, conv_w=conv_w, conv_b=conv_b, w_down=w_down, w_ple_gate=w_ple_gate,
             w_ple_proj=w_ple_proj, ln2_g=ln2_g, ln2_b=ln2_b)
    M = dict(w_in=m_w_in, w_attn_proj=m_w_attn_proj, w_ret_proj=m_w_ret_proj, ret_gn_g=m_ret_gn_g, ret_gn_b=m_ret_gn_b,
             w_out=m_w_out, ln1_g=m_ln1_g, ln1_b=m_ln1_b, w_up=m_w_up, conv_w=m_conv_w, conv_b=m_conv_b, w_down=m_w_down,
             w_ple_gate=m_w_ple_gate, w_ple_proj=m_w_ple_proj, ln2_g=m_ln2_g, ln2_b=m_ln2_b)
    V = dict(w_in=v_w_in, w_attn_proj=v_w_attn_proj, w_ret_proj=v_w_ret_proj, ret_gn_g=v_ret_gn_g, ret_gn_b=v_ret_gn_b,
             w_out=v_w_out, ln1_g=v_ln1_g, ln1_b=v_ln1_b, w_up=v_w_up, conv_w=v_conv_w, conv_b=v_conv_b, w_down=v_w_down,
             w_ple_gate=v_w_ple_gate, w_ple_proj=v_w_ple_proj, ln2_g=v_ln2_g, ln2_b=v_ln2_b)

    me = 4 * lax.axis_index("x") + 2 * lax.axis_index("y") + lax.axis_index("c")
    shard = lambda n, l: (W[n][l].T if n in COL_SHARDED else W[n][l]).astype(BF16)
    landing = lambda ts: [lax.dynamic_update_index_in_dim(lax.empty((N_DEV,) + t.shape, t.dtype), t, me, 0) for t in ts]
    first0 = _gather_many([shard(n, 0) for n in FIRST], "gather_first_l0")
    later0 = [shard(n, 0) for n in LATER] + [conv_w]
    flight0, token0 = _exchange_start("gather", later0, landing(later0), "gather_later_l0_start", after=first0[0])
    all1 = [shard(n, 1) for n in BIG]
    flight1, token1 = _exchange_start("gather", all1, landing(all1), "gather_weights_l1_start", after=token0)
    conv_w_all = []

    def later_first_layer(after):
        _, got = _exchange_wait(flight0, after, "gather_later_l0_wait")
        conv_w_all.append(got[-1].transpose(1, 2, 0, 3).reshape(DEPTH, 3, 2 * DFF))
        return _later_weights(dict(zip(LATER, got)), 0, conv_w_all[0], conv_b)

    def second_layer(after):
        _, got = _exchange_wait(flight1, after, "gather_weights_l1_wait")
        return _layer_weights(dict(zip(BIG, got)), 1, conv_w_all[0], conv_b, W)

    core = lax.axis_index("c").astype(jnp.int32).reshape(1)
    chip = (2 * lax.axis_index("x") + lax.axis_index("y")).astype(jnp.int32).reshape(1)
    empty_like = lambda ts, slots: [lax.empty((slots,) + t.shape[1:], t.dtype) for t in ts]
    chip32, far = [{} for _ in range(DEPTH)], [{} for _ in range(DEPTH)]
    pending = []

    def reduction(l, names, tag):
        state = {}

        def start(g):
            mine = [g[n].reshape((N_DEV, -1) + g[n].shape[1:]) for n in names]
            state["cores"], token = _exchange_start("cores", mine, empty_like(mine, 4), f"exchange_cores_{tag}_start")
            return token

        def onward(after):
            mine, theirs = _exchange_wait(state["cores"], after, f"exchange_cores_{tag}_wait")
            sums = [_pair_sum(a, b, core, f"pair_sum_l{l}_{n}") for a, b, n in zip(mine, theirs, names)]
            for n, s in zip(names, sums):
                chip32[l][n] = s[0]
            sent = [s[0 if n in F32_OVER_ICI else 1] for s, n in zip(sums, names)]
            flight, token = _exchange_start("chips", sent, empty_like(sent, 3), f"exchange_chips_{tag}_start")
            pending.append((l, names, flight, tag))
            return token

        return start, onward

    def on_grads(l, g):
        if l == 0:
            return {}
        start, onward = reduction(l, BIG, f"l{l}")
        return dict(token=start(g), after_ffn=onward)

    def own_hooks(l):
        if l != 0:
            return {}
        start_e, onward_e = reduction(0, LATER, "l0_later")
        start_w, onward_w = reduction(0, FIRST, "l0_first")
        return dict(early_grads=start_e, after_ret=onward_e, w_in_ready=lambda gw: start_w({"w_in": gw}), after_dx0=onward_w)

    ws = [(_first_weights(dict(zip(FIRST, first0)), 0, W), later_first_layer), second_layer]
    loss_vec, grad_x, grads = _local_step(x[0] + token1[0, 0], p[:, 0], positions[0], loss_target[0], ws, own_hooks, on_grads)
    loss = lax.psum(jnp.sum(loss_vec), ("x", "y", "c"))
    for l, names, flight, tag in pending:
        _, got = _exchange_wait(flight, grad_x, f"exchange_chips_{tag}_wait")
        far[l].update(zip(names, got))
    G, DW, NM, NV = ({} for _ in range(4))
    for n in BIG:
        chip32_n = [chip32[l][n] for l in range(DEPTH)]
        far_n = [far[l][n] for l in range(DEPTH)]
        if n in COL_SHARDED:
            G[n] = _reduce_tail(chip32_n, far_n, chip, f"reduced_{n}")[0].transpose(0, 2, 1)
            R2, C2 = DEPTH * W[n].shape[1], W[n].shape[2]
            res = _adamw(*(t.reshape(R2, C2) for t in (G[n], W[n], M[n], V[n])), f"adamw_{n}")
            DW[n], NM[n], NV[n] = (t.reshape(W[n].shape) for t in res)
        else:
            G[n], DW[n], NM[n], NV[n] = _reduce_tail(chip32_n, far_n, chip, f"adamw_{n}", wmv=(W[n], M[n], V[n]))

    small_names = SMALL + ("conv_w",)
    g_small, where = _pack_rows([jnp.stack([grads[l][n] for l in range(DEPTH)]) for n in small_names])
    (g_all,) = _gather_many([g_small], "gather_small_grads")
    g_small = _sum_slots(g_all, "sum_small_grads")
    for n, (off, rows) in zip(SMALL, where):
        G[n] = g_small[off:off + rows].reshape(W[n].shape)
    off, rows = where[-1]
    g_cw = g_small[off:off + rows].reshape(DEPTH, 3, N_DEV, conv_w.shape[2])
    G["conv_w"] = lax.dynamic_index_in_dim(g_cw, me, axis=2, keepdims=False)
    packed = [_pack_rows([d[n] for n in SMALL]) for d in (G, W, M, V)]
    small_out = _adamw(*(t for t, _ in packed), "adamw_small")
    for res, dst in zip(small_out, (DW, NM, NV)):
        for n, (off, rows) in zip(SMALL, packed[0][1]):
            dst[n] = res[off:off + rows].reshape(W[n].shape)
    two_d = lambda t: t.reshape(DEPTH * 3, conv_w.shape[2])
    cw_out = _adamw(two_d(G["conv_w"]), two_d(conv_w), two_d(m_conv_w), two_d(v_conv_w), "adamw_conv_w")
    for res, dst in zip(cw_out, (DW, NM, NV)):
        dst["conv_w"] = res.reshape(conv_w.shape)

    return (loss, grad_x[None], *[G[n] for n in WEIGHTS], *[DW[n] for n in WEIGHTS], *[NM[n] for n in WEIGHTS],
            *[NV[n] for n in WEIGHTS])
```

```python
import math

import numpy as np
import jax
import jax.numpy as jnp
from jax import lax
from jax.experimental import pallas as pl
from jax.experimental.pallas import tpu as pltpu

F32, BF16 = jnp.float32, jnp.bfloat16

D = 1024
DEPTH = 2
N_DEV = 8
HD = 128
NH = 8
DILATIONS = (1, 4, 16)
SPAN = 128
N_ATT = 3 * 3 * NH * HD
RH, RDK, RDV = 4, 256, 512
CH = 128
DFF = 2816
PLE = 256
N_IN = 17408
N_REST = N_IN - N_ATT
OFF_RQ, OFF_RK, OFF_RV, OFF_RG, OFF_GA, OFF_GR = 0, 1024, 2048, 4096, 6144, 7168
ALPHA = (2 * DEPTH) ** 0.25
LN_EPS, GN_EPS = 1e-5, 1e-6
ROPE_BASE = 10000.0
LR, B1, B2, EPS, WD, STEP = 0.001, 0.9, 0.999, 1e-8, 0.01, 10
VMEM_LIMIT = 48 * 1024 * 1024
NEG = -1e30

BIG = ("w_in", "w_attn_proj", "w_ret_proj", "w_out", "w_up", "w_down", "w_ple_gate", "w_ple_proj")
COL_SHARDED = ("w_in", "w_up", "w_ple_proj")
F32_OVER_ICI = ("w_attn_proj", "w_out", "w_ple_gate", "w_ple_proj")
SMALL = ("ret_gn_g", "ret_gn_b", "ln1_g", "ln1_b", "conv_b", "ln2_g", "ln2_b")
WEIGHTS = ("w_in", "w_attn_proj", "w_ret_proj", "ret_gn_g", "ret_gn_b", "w_out", "ln1_g", "ln1_b", "w_up",
           "conv_w", "conv_b", "w_down", "w_ple_gate", "w_ple_proj", "ln2_g", "ln2_b")


def _tile(n, cap, mult=128):
    if n <= cap:
        return n
    t = (cap // mult) * mult
    while n % t:
        t -= mult
    return t


def _cparams(sem):
    return pltpu.CompilerParams(dimension_semantics=sem, vmem_limit_bytes=VMEM_LIMIT)


def _dot(a, b, ca, cb):
    return lax.dot_general(a, b, (((ca,), (cb,)), ((), ())), preferred_element_type=F32)


def _bdot(a, b, ca, cb):
    return lax.dot_general(a, b, (((ca,), (cb,)), ((0,), (0,))), preferred_element_type=F32)


def _mm(a, b, *, name, ta=False, tb=False, out_dtype=F32, add=None, add_scale=1.0, tm=1024, tn=1024, tk=1024,
        b_rows=None, out_rows=None, into=None, blocks8=False, after=None):
    M, K = (a.shape[1], a.shape[0]) if ta else a.shape
    b_first, b_count = b_rows if b_rows else (0, b.shape[0])
    N = b_count if tb else b.shape[1]
    assert K == (b.shape[1] if tb else b_count)
    tm, tn, tk = _tile(M, tm), _tile(N, tn), _tile(K, tk)
    nk = K // tk
    o_first, o_total = out_rows if out_rows else (0, M)
    jb, kb, io = (b_first // tn, 0, o_first // tm) if tb else (0, b_first // tk, o_first // tm)
    assert b_first % (tn if tb else tk) == 0 and o_first % tm == 0 and (add is None or out_rows is None)

    def body(*refs):
        if add is None:
            a_ref, b_ref = refs[:2]
        else:
            a_ref, b_ref, add_ref = refs[:3]
        o_ref, acc_ref = refs[-2:]
        k = pl.program_id(2)

        @pl.when(k == 0)
        def _():
            acc_ref[...] = jnp.zeros_like(acc_ref)

        acc_ref[...] += _dot(a_ref[...].astype(BF16), b_ref[...].astype(BF16), 0 if ta else 1, 1 if tb else 0)

        @pl.when(k == nk - 1)
        def _():
            r = acc_ref[...]
            if add is not None:
                r = r + add_scale * add_ref[...].astype(F32)
            o_ref[...] = r.astype(out_dtype).reshape(o_ref.shape)

    a_spec = pl.BlockSpec((tk, tm), lambda i, j, k: (k, i)) if ta else pl.BlockSpec((tm, tk), lambda i, j, k: (i, k))
    if tb:
        b_spec = pl.BlockSpec((tn, tk), lambda i, j, k: (j + jb, k))
    else:
        b_spec = pl.BlockSpec((tk, tn), lambda i, j, k: (k + kb, j))
    if blocks8:
        assert tm == 1024
        o_spec = pl.BlockSpec((1, 8, 128, tn), lambda i, j, k: (i + io, 0, 0, j))
        o_shape = (o_total // tm, 8, 128, N)
    else:
        o_spec = pl.BlockSpec((tm, tn), lambda i, j, k: (i + io, j))
        o_shape = (o_total, N)
    in_specs, args, aliases = [a_spec, b_spec], [a, b], {}
    if add is not None:
        in_specs.append(o_spec)
        args.append(add)
    if after is not None:
        in_specs.append(pl.BlockSpec(memory_space=pl.ANY))
        args.append(after)
    if into is not None:
        aliases = {len(args): 0}
        in_specs.append(pl.BlockSpec(memory_space=pl.ANY))
        args.append(into)
    return pl.pallas_call(
        body, name=name, grid=(M // tm, N // tn, nk), in_specs=in_specs, out_specs=o_spec,
        out_shape=jax.ShapeDtypeStruct(o_shape, out_dtype), scratch_shapes=[pltpu.VMEM((tm, tn), F32)],
        input_output_aliases=aliases, compiler_params=_cparams(("parallel", "parallel", "arbitrary")),
    )(*args)


def _rowwise(fn, rows, pars, outs, accs, *, name, tm=512):
    first = rows[0][0] if isinstance(rows[0], tuple) else rows[0]
    S = first.shape[-2]
    tm = _tile(S, tm, 16)
    n_r, n_p, n_o = len(rows), len(pars), len(outs)
    outs = [o if len(o) == 5 else (o[0], o[1], o[0], 0, None) for o in outs]
    intos = [(k, o[4]) for k, o in enumerate(outs) if o[4] is not None]
    n_i = len(intos)

    def body(*refs):
        i = pl.program_id(0)
        vals = [r[...] for r in refs[:n_r + n_p]]
        res = fn(*vals)
        if not isinstance(res, (tuple, list)):
            res = (res,)
        o_refs = refs[n_r + n_p + n_i:n_r + n_p + n_i + n_o]
        a_refs = refs[n_r + n_p + n_i + n_o:]
        for r, v in zip(o_refs, res[:n_o]):
            r[...] = v.astype(r.dtype)
        if a_refs:
            @pl.when(i == 0)
            def _():
                for r in a_refs:
                    r[...] = jnp.zeros_like(r)

            for r, v in zip(a_refs, res[n_o:]):
                r[...] += v

    in_specs, args = [], []
    for r in rows:
        if isinstance(r, tuple):
            arr, w, cb = r
            in_specs.append(pl.BlockSpec((tm, w), lambda i, cb=cb: (i, cb)))
        elif r.ndim == 3:
            arr = r
            in_specs.append(pl.BlockSpec((arr.shape[0], tm, arr.shape[2]), lambda i: (0, i, 0)))
        else:
            arr = r
            in_specs.append(pl.BlockSpec((tm, arr.shape[1]), lambda i: (i, 0)))
        args.append(arr)
    for p_ in pars:
        in_specs.append(pl.BlockSpec(p_.shape, lambda i: (0, 0)))
        args.append(p_)
    aliases = {}
    for k, arr in intos:
        aliases[len(args)] = k
        in_specs.append(pl.BlockSpec(memory_space=pl.ANY))
        args.append(arr)
    out_shape = [jax.ShapeDtypeStruct((S, o[2]), o[1]) for o in outs] + [jax.ShapeDtypeStruct(a, F32) for a in accs]
    out_specs = [pl.BlockSpec((tm, o[0]), lambda i, cb=o[3]: (i, cb)) for o in outs] + [pl.BlockSpec(a, lambda i: (0, 0)) for a in accs]
    return pl.pallas_call(
        body, name=name, grid=(S // tm,), in_specs=in_specs, out_specs=out_specs, out_shape=out_shape,
        input_output_aliases=aliases, compiler_params=_cparams(("arbitrary",) if accs else ("parallel",)),
    )(*args)


def _norm(h, eps):
    mu = jnp.mean(h, -1, keepdims=True)
    d = h - mu
    rstd = lax.rsqrt(jnp.mean(d * d, -1, keepdims=True) + eps)
    return d * rstd, rstd


def _norm_bwd(dxh, xh, rstd):
    return rstd * (dxh - jnp.mean(dxh, -1, keepdims=True) - xh * jnp.mean(dxh * xh, -1, keepdims=True))


def _sig(x):
    return 1.0 / (1.0 + jnp.exp(-x))


_GELU_C = math.sqrt(2.0 / math.pi)


def _gelu(x):
    t = jnp.tanh(_GELU_C * (x + 0.044715 * x * x * x))
    return 0.5 * x * (1.0 + t), t


def _gelu_grad(x, t):
    return 0.5 * (1.0 + t) + 0.5 * x * (1.0 - t * t) * _GELU_C * (1.0 + 3 * 0.044715 * x * x)


def _f_ln1(x, mix, g, b):
    h = ALPHA * x + mix
    xh, _ = _norm(h, LN_EPS)
    y = xh * g + b
    return h, y, y


def _f_ln2(x, ffn, z, pp, g, b):
    h = ALPHA * x + ffn + _sig(z) * pp
    xh, _ = _norm(h, LN_EPS)
    y = xh * g + b
    return h, y, y


def _f_ln_bwd(*args):
    *dys, h, g = args
    dy = dys[0]
    for t in dys[1:]:
        dy = dy + t
    xh, rstd = _norm(h, LN_EPS)
    dh = _norm_bwd(dy * g, xh, rstd)
    return dh, dh, jnp.sum(dy * xh, 0, keepdims=True), jnp.sum(dy, 0, keepdims=True)


def _f_sum(*ts):
    r = ts[0]
    for t in ts[1:]:
        r = r + t
    return r


def _f_loss(y, t):
    e = y - t
    return e * (1.0 / D), jnp.sum(e * e, 0, keepdims=True) * (0.5 / D)


def _head_col(c, h):
    lane = lax.broadcasted_iota(jnp.int32, c.shape, 1)
    return jnp.sum(jnp.where(lane == h, c, 0.0), -1, keepdims=True)


def _f_combine(o0, o1, o2, l0, l1, l2):
    lane = lax.broadcasted_iota(jnp.int32, l0.shape, 1)
    parts, lse = [], jnp.zeros(l0.shape, F32)
    for h in range(NH):
        a0, a1, a2 = _head_col(l0, h), _head_col(l1, h), _head_col(l2, h)
        m = jnp.maximum(jnp.maximum(a0, a1), a2)
        e0, e1, e2 = jnp.exp(a0 - m), jnp.exp(a1 - m), jnp.exp(a2 - m)
        den = e0 + e1 + e2
        parts.append((e0 * o0[h].astype(F32) + e1 * o1[h].astype(F32) + e2 * o2[h].astype(F32)) / den)
        lse = jnp.where(lane == h, m + jnp.log(den), lse)
    return jnp.concatenate(parts, axis=1), lse


def _f_delta(da, a):
    lane = lax.broadcasted_iota(jnp.int32, (da.shape[0], HD), 1)
    out = jnp.zeros((da.shape[0], HD), F32)
    for h in range(NH):
        sl = slice(h * HD, (h + 1) * HD)
        s = jnp.sum(da[:, sl].astype(F32) * a[:, sl].astype(F32), -1, keepdims=True)
        out = jnp.where(lane == h, s, out)
    return out


def _f_gate(ap, rp, ga, gr):
    return _sig(ga.astype(F32)) * ap.astype(F32) + _sig(gr.astype(F32)) * rp.astype(F32)


def _f_gate_bwd(dm, ap, rp, ga, gr):
    dm = dm.astype(F32)
    sa, sr = _sig(ga.astype(F32)), _sig(gr.astype(F32))
    dga, dgr = dm * ap.astype(F32) * sa * (1.0 - sa), dm * rp.astype(F32) * sr * (1.0 - sr)
    return dm * sa, dm * sr, jnp.concatenate([dga, dgr], axis=1)


def _f_gn(y, rg, g, b):
    y, rg = y.astype(F32), rg.astype(F32)
    parts = []
    for h in range(RH):
        sl = slice(h * RDV, (h + 1) * RDV)
        xh, _ = _norm(y[:, sl], GN_EPS)
        parts.append(xh * g[:, sl] + b[:, sl])
    return rg * _sig(rg) * jnp.concatenate(parts, axis=1)


def _f_gn_bwd(dr, y, rg, g, b):
    dr, y, rg = dr.astype(F32), y.astype(F32), rg.astype(F32)
    s = _sig(rg)
    d_out = dr * rg * s
    dys, outs, xhs = [], [], []
    for h in range(RH):
        sl = slice(h * RDV, (h + 1) * RDV)
        xh, rstd = _norm(y[:, sl], GN_EPS)
        xhs.append(xh)
        outs.append(xh * g[:, sl] + b[:, sl])
        dys.append(_norm_bwd(d_out[:, sl] * g[:, sl], xh, rstd))
    xh, out = jnp.concatenate(xhs, axis=1), jnp.concatenate(outs, axis=1)
    d_rg = dr * out * s * (1.0 + rg * (1.0 - s))
    return jnp.concatenate(dys, axis=1), d_rg, jnp.sum(d_out * xh, 0, keepdims=True), jnp.sum(d_out, 0, keepdims=True)


def _f_ple_bwd(dh, z, pp):
    s = _sig(z)
    return dh * s, dh * pp * s * (1.0 - s)


QKV = 3 * HD


def _to_tokens(t, d):
    if d == 1:
        return t
    *lead, S, C = t.shape
    n = len(lead)
    perm = tuple(range(n)) + (n + 1, n, n + 2)
    return t.reshape(*lead, d, S // d, C).transpose(perm).reshape(*lead, S, C)


def _to_residues(t, d):
    if d == 1:
        return t
    S, C = t.shape
    return t.reshape(S // d, d, C).transpose(1, 0, 2).reshape(S, C)


def _to_head_residues(t, d):
    S = t.shape[0]
    return t.reshape(S // d, d, NH, HD).transpose(2, 1, 0, 3).reshape(NH, S, HD)


def _w_qkv_specs(g):
    return [pl.BlockSpec((D, D), lambda *i, t=t: (3 * g + t, 0)) for t in range(3)]


def _qkv_fwd(xv, win, g, dil, name):
    Sd = xv.shape[0]
    S = Sd * dil
    tm = min(512, Sd)
    nma = Sd // tm

    def body(a_ref, wq_ref, wk_ref, wv_ref, o_ref):
        a = a_ref[...]
        q, k, v = (_dot(a, w_ref[...], 1, 1).astype(BF16) for w_ref in (wq_ref, wk_ref, wv_ref))
        for h in range(NH):
            sl = slice(h * HD, (h + 1) * HD)
            o_ref[h] = jnp.concatenate([q[:, sl], k[:, sl], v[:, sl]], axis=1)

    return pl.pallas_call(
        body, name=name, grid=(S // tm,),
        in_specs=[pl.BlockSpec((tm, D), lambda i: (i % nma, i // nma))] + _w_qkv_specs(g),
        out_specs=pl.BlockSpec((NH, tm, QKV), lambda i: (0, i, 0)), out_shape=jax.ShapeDtypeStruct((NH, S, QKV), BF16),
        compiler_params=_cparams(("parallel",)),
    )(xv, win, win, win)


def _qkv_dx(dqkv, win, g, dil, name, out_dtype, add=None, after=None):
    S = dqkv.shape[1]
    Sd = S // dil
    tm = min(512, Sd)
    nmo = Sd // tm

    def body(*refs):
        a_ref, wq_ref, wk_ref, wv_ref = refs[:4]
        o_ref = refs[-1]
        acc = None
        for h in range(NH):
            sl = slice(h * HD, (h + 1) * HD)
            w = jnp.concatenate([wq_ref[sl, :], wk_ref[sl, :], wv_ref[sl, :]], axis=0)
            part = _dot(a_ref[h], w, 1, 0)
            acc = part if acc is None else acc + part
        if add is not None:
            acc = acc + refs[4][...]
        o_ref[...] = acc.astype(out_dtype)

    o_spec = pl.BlockSpec((tm, D), lambda i: (i % nmo, i // nmo))
    in_specs = [pl.BlockSpec((NH, tm, QKV), lambda i: (0, i, 0))] + _w_qkv_specs(g)
    args = [dqkv, win, win, win]
    if add is not None:
        assert dil == 1
        in_specs.append(o_spec)
        args.append(add)
    if after is not None:
        in_specs.append(pl.BlockSpec(memory_space=pl.ANY))
        args.append(after)
    return pl.pallas_call(
        body, name=name, grid=(S // tm,), in_specs=in_specs, out_specs=o_spec,
        out_shape=jax.ShapeDtypeStruct((Sd, dil * D), out_dtype), compiler_params=_cparams(("parallel",)),
    )(*args)


GW_IN_BLOCKS = (N_IN // D, NH, HD, D)


def _qkv_dw(dqkv, xv, g, dil, name, into=None):
    S = dqkv.shape[1]
    Sd = S // dil
    tk = min(1024, Sd)
    nkb, nk = Sd // tk, S // tk
    hh = NH // 2

    def body(*refs):
        a_ref, b_ref = refs[:2]
        o_ref, acc_ref = refs[-2:]
        k = pl.program_id(1)

        @pl.when(k == 0)
        def _():
            acc_ref[...] = jnp.zeros_like(acc_ref)

        b = b_ref[...]
        for h in range(hh):
            acc_ref[h * QKV:(h + 1) * QKV, :] += _dot(a_ref[h], b, 0, 0)

        @pl.when(k == nk - 1)
        def _():
            for h in range(hh):
                for t in range(3):
                    o_ref[t, h] = acc_ref[h * QKV + t * HD:h * QKV + (t + 1) * HD, :]

    in_specs = [pl.BlockSpec((hh, tk, QKV), lambda j, k: (j, k, 0)), pl.BlockSpec((tk, D), lambda j, k: (k % nkb, k // nkb))]
    args, aliases = [dqkv, xv], {}
    if into is not None:
        aliases = {2: 0}
        in_specs.append(pl.BlockSpec(memory_space=pl.ANY))
        args.append(into)
    return pl.pallas_call(
        body, name=name, grid=(2, nk), in_specs=in_specs,
        out_specs=pl.BlockSpec((3, hh, HD, D), lambda j, k: (g, j, 0, 0)), out_shape=jax.ShapeDtypeStruct(GW_IN_BLOCKS, F32),
        input_output_aliases=aliases, scratch_shapes=[pltpu.VMEM((hh * QKV, D), F32)],
        compiler_params=_cparams(("parallel", "arbitrary")),
    )(*args)


def _band(nb, first_valid, last_valid=None):
    b = lax.broadcasted_iota(jnp.int32, (nb, SPAN, SPAN), 0)
    row = lax.broadcasted_iota(jnp.int32, (nb, SPAN, SPAN), 1)
    col = lax.broadcasted_iota(jnp.int32, (nb, SPAN, SPAN), 2)
    off = jnp.where(b == 0, jnp.where(first_valid, 0, 2 * SPAN), 0)
    if last_valid is not None:
        off = off + jnp.where(b == nb - 1, jnp.where(last_valid, 0, 2 * SPAN), 0)
    return col <= row, col >= row + off


def _attn_tiles(S, dil):
    Sd = S // dil
    T = min(1024, Sd)
    hp = min(NH, max(1, (S // T) * NH // 32))
    return Sd, T, T // SPAN, Sd // T, hp


def _attn_fwd(qkv, dil, name):
    S = qkv.shape[1]
    Sd, T, nsub, nib, hp = _attn_tiles(S, dil)
    scale = HD ** -0.5

    def body(c_ref, p_ref, o_ref, l_ref):
        ib, hb = pl.program_id(1), pl.program_id(2)
        m_cur, m_prev = _band(nsub, ib > 0)
        lane = lax.broadcasted_iota(jnp.int32, (T, HD), 1)

        @pl.when(hb == 0)
        def _():
            l_ref[...] = jnp.zeros_like(l_ref)

        lses = l_ref[...]
        for hh in range(hp):
            blk, hal = c_ref[hh], p_ref[hh]
            q, k, v = blk[:, :HD], blk[:, HD:2 * HD], blk[:, 2 * HD:]
            if nsub > 1:
                kp = jnp.concatenate([hal[:, HD:2 * HD], k[:T - SPAN]], axis=0)
                vp = jnp.concatenate([hal[:, 2 * HD:], v[:T - SPAN]], axis=0)
            else:
                kp, vp = hal[:, HD:2 * HD], hal[:, 2 * HD:]
            q3, k3, v3, kp3, vp3 = (t.reshape(nsub, SPAN, HD) for t in (q, k, v, kp, vp))
            sc = jnp.where(m_cur, _bdot(q3, k3, 2, 2) * scale, NEG)
            sp = jnp.where(m_prev, _bdot(q3, kp3, 2, 2) * scale, NEG)
            m = jnp.maximum(jnp.max(sc, -1, keepdims=True), jnp.max(sp, -1, keepdims=True))
            pc, pp = jnp.exp(sc - m), jnp.exp(sp - m)
            den = jnp.sum(pc, -1, keepdims=True) + jnp.sum(pp, -1, keepdims=True)
            o = (_bdot(pc.astype(BF16), v3, 2, 1) + _bdot(pp.astype(BF16), vp3, 2, 1)) / den
            o_ref[hh] = o.reshape(T, HD).astype(BF16)
            lses = jnp.where(lane == hb * hp + hh, (m + jnp.log(den)).reshape(T, 1), lses)
        l_ref[...] = lses

    cur = pl.BlockSpec((hp, T, QKV), lambda r, ib, h: (h, r * nib + ib, 0))
    prev = pl.BlockSpec((hp, SPAN, QKV), lambda r, ib, h: (h, r * (Sd // SPAN) + jnp.maximum(ib * nsub - 1, 0), 0))
    return pl.pallas_call(
        body, name=name, grid=(dil, nib, NH // hp), in_specs=[cur, prev],
        out_specs=[pl.BlockSpec((hp, T, HD), lambda r, ib, h: (h, r * nib + ib, 0)),
                   pl.BlockSpec((T, HD), lambda r, ib, h: (r * nib + ib, 0))],
        out_shape=[jax.ShapeDtypeStruct((NH, S, HD), BF16), jax.ShapeDtypeStruct((S, HD), F32)],
        compiler_params=_cparams(("parallel", "parallel", "arbitrary")),
    )(qkv, qkv)


def _attn_bwd(qkv, d_attn, lse, delta, dil, name):
    S = qkv.shape[1]
    Sd, T, nsub, nib, hp = _attn_tiles(S, dil)
    scale = HD ** -0.5
    ne = nsub + 1

    def body(c_ref, p_ref, n_ref, do_ref, don_ref, l_ref, ln_ref, dl_ref, dln_ref, o_ref):
        ib, hb = pl.program_id(1), pl.program_id(2)
        _, m_prev = _band(ne, ib > 0, ib < nib - 1)
        m_cur, _ = _band(nsub, True)
        for hh in range(hp):
            h = hb * hp + hh
            blk, hal, nxt = c_ref[hh], p_ref[hh], n_ref[hh]
            q, k, v = blk[:, :HD], blk[:, HD:2 * HD], blk[:, 2 * HD:]
            do = do_ref[hh]
            l, dl = _head_col(l_ref[...], h), _head_col(dl_ref[...], h)
            qe = jnp.concatenate([q, nxt[:, :HD]], axis=0).reshape(ne, SPAN, HD)
            doe = jnp.concatenate([do, don_ref[hh]], axis=0).reshape(ne, SPAN, HD)
            le = jnp.concatenate([l, _head_col(ln_ref[...], h)], axis=0).reshape(ne, SPAN, 1)
            dle = jnp.concatenate([dl, _head_col(dln_ref[...], h)], axis=0).reshape(ne, SPAN, 1)
            kpe = jnp.concatenate([hal[:, HD:2 * HD], k], axis=0).reshape(ne, SPAN, HD)
            vpe = jnp.concatenate([hal[:, 2 * HD:], v], axis=0).reshape(ne, SPAN, HD)
            p = jnp.where(m_prev, jnp.exp(_bdot(qe, kpe, 2, 2) * scale - le), 0.0)
            ds = (p * (_bdot(doe, vpe, 2, 2) - dle)).astype(BF16)
            dq = _bdot(ds, kpe, 2, 1)[:nsub]
            dk = _bdot(ds, qe, 1, 1)[1:]
            dv = _bdot(p.astype(BF16), doe, 1, 1)[1:]
            q3, k3, v3, do3 = (t.reshape(nsub, SPAN, HD) for t in (q, k, v, do))
            l3, dl3 = l.reshape(nsub, SPAN, 1), dl.reshape(nsub, SPAN, 1)
            p = jnp.where(m_cur, jnp.exp(_bdot(q3, k3, 2, 2) * scale - l3), 0.0)
            ds = (p * (_bdot(do3, v3, 2, 2) - dl3)).astype(BF16)
            dq = (dq + _bdot(ds, k3, 2, 1)) * scale
            dk = (dk + _bdot(ds, q3, 1, 1)) * scale
            dv = dv + _bdot(p.astype(BF16), do3, 1, 1)
            o_ref[hh] = jnp.concatenate([t.reshape(T, HD) for t in (dq, dk, dv)], axis=1).astype(BF16)

    nb = Sd // SPAN
    row = lambda r, ib: r * nib + ib
    prow = lambda r, ib: r * nb + jnp.maximum(ib * nsub - 1, 0)
    nrow = lambda r, ib: r * nb + jnp.minimum((ib + 1) * nsub, nb - 1)
    cur3 = pl.BlockSpec((hp, T, QKV), lambda r, ib, h: (h, row(r, ib), 0))
    prev3 = pl.BlockSpec((hp, SPAN, QKV), lambda r, ib, h: (h, prow(r, ib), 0))
    next3 = pl.BlockSpec((hp, SPAN, QKV), lambda r, ib, h: (h, nrow(r, ib), 0))
    cur1 = pl.BlockSpec((hp, T, HD), lambda r, ib, h: (h, row(r, ib), 0))
    next1 = pl.BlockSpec((hp, SPAN, HD), lambda r, ib, h: (h, nrow(r, ib), 0))
    curc = pl.BlockSpec((T, HD), lambda r, ib, h: (row(r, ib), 0))
    nextc = pl.BlockSpec((SPAN, HD), lambda r, ib, h: (nrow(r, ib), 0))
    return pl.pallas_call(
        body, name=name, grid=(dil, nib, NH // hp),
        in_specs=[cur3, prev3, next3, cur1, next1, curc, nextc, curc, nextc], out_specs=cur3,
        out_shape=jax.ShapeDtypeStruct((NH, S, QKV), BF16),
        compiler_params=_cparams(("parallel", "parallel", "parallel")),
    )(qkv, qkv, qkv, d_attn, d_attn, lse, lse, delta, delta)


def _ret_consts():
    lg = np.log1p(-np.exp2(-5.0 - np.arange(RH, dtype=np.float64)))
    idx = np.arange(CH, dtype=np.float64)
    rel = idx[:, None] - idx[None, :]
    intra = np.where(rel >= 0, np.exp(lg[:, None, None] * np.maximum(rel, 0.0)), 0.0)
    qd = np.exp(lg[:, None] * (idx + 1.0))
    kd = np.exp(lg[:, None] * (CH - 1.0 - idx))
    cd = np.exp(lg * CH)
    wide = lambda t: np.broadcast_to(t[:, :, None], (RH, t.shape[1], RDV))
    return (jnp.asarray(intra, F32), jnp.asarray(wide(qd), F32), jnp.asarray(wide(kd), F32),
            jnp.asarray(np.broadcast_to(cd[:, None, None], (RH, 1, RDV)), F32))


def _rot(t, c, s):
    t1, t2 = t[:, :RDK // 2], t[:, RDK // 2:]
    return jnp.concatenate([t1 * c - t2 * s, t1 * s + t2 * c], axis=1)


def _unrot(d, c, s):
    d1, d2 = d[:, :RDK // 2], d[:, RDK // 2:]
    return jnp.concatenate([d1 * c + d2 * s, d2 * c - d1 * s], axis=1)


RCH = 2


def _ret_specs(nmap):
    rows = RCH * CH
    q = pl.BlockSpec((rows, RH * RDK), lambda n: (nmap(n), OFF_RQ // (RH * RDK)))
    k = pl.BlockSpec((rows, RH * RDK), lambda n: (nmap(n), OFF_RK // (RH * RDK)))
    v = pl.BlockSpec((rows, RH * RDV), lambda n: (nmap(n), OFF_RV // (RH * RDV)))
    cs = pl.BlockSpec((rows, RDK // 2), lambda n: (nmap(n), 0))
    dmat = pl.BlockSpec((RH, CH, CH), lambda n: (0, 0, 0))
    dvec = pl.BlockSpec((RH, CH, RDV), lambda n: (0, 0, 0))
    cdv = pl.BlockSpec((RH, 1, RDV), lambda n: (0, 0, 0))
    state = pl.BlockSpec((RH, RCH, RDK, RDV), lambda n: (0, nmap(n), 0, 0))
    out = pl.BlockSpec((rows, RH * RDV), lambda n: (nmap(n), 0))
    return [q, k, v, cs, cs, dmat, dvec, dvec, cdv], state, out


def _ret_fwd(proj, cos, sin, consts):
    S = proj.shape[0]
    nc = S // CH

    def body(q_ref, k_ref, v_ref, c_ref, s_ref, d_ref, qd_ref, kd_ref, cd_ref, o_ref, st_ref, state):
        @pl.when(pl.program_id(0) == 0)
        def _():
            state[...] = jnp.zeros_like(state)

        for ci in range(RCH):
            rows = slice(ci * CH, (ci + 1) * CH)
            c, s = c_ref[rows, :], s_ref[rows, :]
            for h in range(RH):
                qk, vv = slice(h * RDK, (h + 1) * RDK), slice(h * RDV, (h + 1) * RDV)
                qb = _rot(q_ref[rows, qk].astype(F32), c, s).astype(BF16)
                kb = (_rot(k_ref[rows, qk].astype(F32), c, s) * (RDK ** -0.5)).astype(BF16)
                vb = v_ref[rows, vv]
                sb = state[h].astype(BF16)
                st_ref[h, ci] = sb
                a = (_dot(qb, kb, 1, 1) * d_ref[h]).astype(BF16)
                o_ref[rows, vv] = (_dot(a, vb, 1, 0) + _dot(qb, sb, 1, 0) * qd_ref[h]).astype(BF16)
                vk = (vb.astype(F32) * kd_ref[h]).astype(BF16)
                state[h] = cd_ref[h] * state[h] + _dot(kb, vk, 0, 0)

    ins, state_spec, out_spec = _ret_specs(lambda n: n)
    return pl.pallas_call(
        body, name="ret_fwd", grid=(nc // RCH,), in_specs=ins, out_specs=[out_spec, state_spec],
        out_shape=[jax.ShapeDtypeStruct((S, RH * RDV), BF16), jax.ShapeDtypeStruct((RH, nc, RDK, RDV), BF16)],
        scratch_shapes=[pltpu.VMEM((RH, RDK, RDV), F32)],
        compiler_params=_cparams(("arbitrary",)),
    )(proj, proj, proj, cos, sin, *consts)


def _ret_bwd(proj, cos, sin, consts, states, d_ret, d_rest):
    S = proj.shape[0]
    nc = S // CH

    def body(q_ref, k_ref, v_ref, c_ref, s_ref, d_ref, qd_ref, kd_ref, cd_ref, st_ref, do_ref, _, o_ref, dstate):
        @pl.when(pl.program_id(0) == 0)
        def _():
            dstate[...] = jnp.zeros_like(dstate)

        for ci in reversed(range(RCH)):
            rows = slice(ci * CH, (ci + 1) * CH)
            c, s = c_ref[rows, :], s_ref[rows, :]
            for h in range(RH):
                qk, vv = slice(h * RDK, (h + 1) * RDK), slice(h * RDV, (h + 1) * RDV)
                qb = _rot(q_ref[rows, qk].astype(F32), c, s).astype(BF16)
                kb = (_rot(k_ref[rows, qk].astype(F32), c, s) * (RDK ** -0.5)).astype(BF16)
                vb, sb, do = v_ref[rows, vv], st_ref[h, ci], do_ref[rows, vv]
                dmat, qd, kd = d_ref[h], qd_ref[h], kd_ref[h]
                a = (_dot(qb, kb, 1, 1) * dmat).astype(BF16)
                doq = (do.astype(F32) * qd).astype(BF16)
                dsb = dstate[h].astype(BF16)
                vk = (vb.astype(F32) * kd).astype(BF16)
                o_ref[rows, OFF_RV + h * RDV:OFF_RV + (h + 1) * RDV] = (_dot(a, do, 0, 0) + _dot(kb, dsb, 1, 0) * kd).astype(BF16)
                da = (_dot(do, vb, 1, 1) * dmat).astype(BF16)
                dq = _dot(da, kb, 1, 0) + _dot(doq, sb, 1, 1)
                dk = (_dot(da, qb, 0, 0) + _dot(vk, dsb, 1, 1)) * (RDK ** -0.5)
                o_ref[rows, OFF_RQ + h * RDK:OFF_RQ + (h + 1) * RDK] = _unrot(dq, c, s).astype(BF16)
                o_ref[rows, OFF_RK + h * RDK:OFF_RK + (h + 1) * RDK] = _unrot(dk, c, s).astype(BF16)
                dstate[h] = cd_ref[h] * dstate[h] + _dot(qb, doq, 0, 0)

    nsteps = nc // RCH
    rev = lambda n: nsteps - 1 - n
    ins, state_spec, out_spec = _ret_specs(rev)
    return pl.pallas_call(
        body, name="ret_bwd", grid=(nsteps,), in_specs=ins + [state_spec, out_spec, pl.BlockSpec(memory_space=pl.ANY)],
        out_specs=pl.BlockSpec((RCH * CH, OFF_RG), lambda n: (rev(n), 0)),
        out_shape=jax.ShapeDtypeStruct(d_rest.shape, BF16), input_output_aliases={11: 0},
        scratch_shapes=[pltpu.VMEM((RH, RDK, RDV), F32)],
        compiler_params=_cparams(("arbitrary",)),
    )(proj, proj, proj, cos, sin, *consts, states, d_ret, d_rest)


CW = 256
HALO = 16


def _shift_down(v, halo, k):
    rolled = pltpu.roll(v, k, 0)
    hr = pltpu.roll(halo, k, 0)[0:8]
    row = lax.broadcasted_iota(jnp.int32, hr.shape, 0)
    return jnp.concatenate([jnp.where(row < k, hr, rolled[0:8]), rolled[8:]], axis=0)


def _shift_up(v, halo, k):
    T = v.shape[0]
    rolled = pltpu.roll(v, T - k, 0)
    hr = pltpu.roll(halo, 8 - k, 0)[0:8]
    row = lax.broadcasted_iota(jnp.int32, hr.shape, 0)
    return jnp.concatenate([rolled[:T - 8], jnp.where(row >= 8 - k, hr, rolled[T - 8:])], axis=0)


def _conv_taps(h_ref, hp_ref, first):
    h = h_ref[...].astype(F32)
    hp = hp_ref[...].astype(F32) * jnp.where(first, 0.0, 1.0)
    return _shift_down(h, hp, 2), _shift_down(h, hp, 1), h


def _conv_specs(S, T, cw=CW):
    nj = DFF // cw
    cur = pl.BlockSpec((T, cw), lambda j, i: (i, j))
    prev = pl.BlockSpec((HALO, cw), lambda j, i: (jnp.maximum(i * (T // HALO) - 1, 0), j))
    nxt = pl.BlockSpec((HALO, cw), lambda j, i: (jnp.minimum((i + 1) * (T // HALO), S // HALO - 1), j))
    w = pl.BlockSpec((3, cw), lambda j, i: (0, j))
    b = pl.BlockSpec((1, cw), lambda j, i: (0, j))
    return nj, cur, prev, nxt, w, b


def _conv_fwd(hg, hu, wg, wu, bg, bu):
    S = hg.shape[0]
    T = min(1024, S)
    nj, cur, prev, _, w, b = _conv_specs(S, T)

    def body(hg_ref, hu_ref, hgp_ref, hup_ref, wg_ref, wu_ref, bg_ref, bu_ref, o_ref):
        first = pl.program_id(1) == 0
        g2, g1, g0 = _conv_taps(hg_ref, hgp_ref, first)
        u2, u1, u0 = _conv_taps(hu_ref, hup_ref, first)
        cg = wg_ref[0:1, :] * g2 + wg_ref[1:2, :] * g1 + wg_ref[2:3, :] * g0 + bg_ref[...]
        cu = wu_ref[0:1, :] * u2 + wu_ref[1:2, :] * u1 + wu_ref[2:3, :] * u0 + bu_ref[...]
        o_ref[...] = (_gelu(cg)[0] * cu).astype(BF16)

    return pl.pallas_call(
        body, name="conv_fwd", grid=(nj, S // T), in_specs=[cur, cur, prev, prev, w, w, b, b], out_specs=cur,
        out_shape=jax.ShapeDtypeStruct((S, DFF), BF16), compiler_params=_cparams(("parallel", "parallel")),
    )(hg, hu, hg, hu, wg, wu, bg, bu)


def _conv_bwd_pre(d_act, hg, hu, wg, wu, bg, bu):
    S = hg.shape[0]
    T = min(1024, S)
    nj, cur, prev, _, w, b = _conv_specs(S, T)

    def body(da_ref, hg_ref, hu_ref, hgp_ref, hup_ref, wg_ref, wu_ref, bg_ref, bu_ref,
             dcg_ref, dcu_ref, gwg_ref, gwu_ref, gbg_ref, gbu_ref):
        first = pl.program_id(1) == 0
        g2, g1, g0 = _conv_taps(hg_ref, hgp_ref, first)
        u2, u1, u0 = _conv_taps(hu_ref, hup_ref, first)
        cg = wg_ref[0:1, :] * g2 + wg_ref[1:2, :] * g1 + wg_ref[2:3, :] * g0 + bg_ref[...]
        cu = wu_ref[0:1, :] * u2 + wu_ref[1:2, :] * u1 + wu_ref[2:3, :] * u0 + bu_ref[...]
        da = da_ref[...].astype(F32)
        gl, t = _gelu(cg)
        dcg = da * cu * _gelu_grad(cg, t)
        dcu = da * gl
        dcg_ref[...] = dcg.astype(BF16)
        dcu_ref[...] = dcu.astype(BF16)

        @pl.when(first)
        def _():
            for r in (gwg_ref, gwu_ref, gbg_ref, gbu_ref):
                r[...] = jnp.zeros_like(r)

        for r, d, taps in ((gwg_ref, dcg, (g2, g1, g0)), (gwu_ref, dcu, (u2, u1, u0))):
            for j in range(3):
                r[j:j + 1, :] += jnp.sum(d * taps[j], 0, keepdims=True)
        gbg_ref[...] += jnp.sum(dcg, 0, keepdims=True)
        gbu_ref[...] += jnp.sum(dcu, 0, keepdims=True)

    return pl.pallas_call(
        body, name="conv_bwd_pre", grid=(nj, S // T), in_specs=[cur, cur, cur, prev, prev, w, w, b, b],
        out_specs=[cur, cur, w, w, b, b],
        out_shape=[jax.ShapeDtypeStruct((S, DFF), BF16)] * 2 + [jax.ShapeDtypeStruct((3, DFF), F32)] * 2
        + [jax.ShapeDtypeStruct((1, DFF), F32)] * 2,
        compiler_params=_cparams(("parallel", "arbitrary")),
    )(d_act, hg, hu, hg, hu, wg, wu, bg, bu)


def _conv_bwd_in(dc, w, name):
    S = dc.shape[0]
    T = min(512, S)
    nj, cur, _, nxt, wspec, _ = _conv_specs(S, T, DFF // 2)
    nt = S // T

    def body(dc_ref, dn_ref, w_ref, o_ref):
        d = dc_ref[...].astype(F32)
        dn = dn_ref[...].astype(F32) * jnp.where(pl.program_id(1) == nt - 1, 0.0, 1.0)
        o_ref[...] = (w_ref[2:3, :] * d + w_ref[1:2, :] * _shift_up(d, dn, 1) + w_ref[0:1, :] * _shift_up(d, dn, 2)).astype(BF16)

    return pl.pallas_call(
        body, name=name, grid=(nj, nt), in_specs=[cur, nxt, wspec], out_specs=cur,
        out_shape=jax.ShapeDtypeStruct((S, DFF), BF16), compiler_params=_cparams(("parallel", "parallel")),
    )(dc, dc, w)


def _adam_math(g, w, m, v):
    m = B1 * m + (1.0 - B1) * g
    v = B2 * v + (1.0 - B2) * (g * g)
    m_hat = m / (1.0 - B1 ** STEP)
    v_hat = v / (1.0 - B2 ** STEP)
    return -LR * (m_hat / (jnp.sqrt(v_hat) + EPS) + WD * w), m, v


def _reduce_tail(chip32, far, chip, name, wmv=None):
    L = len(chip32)
    _, R, C = chip32[0].shape
    tr = _tile(R, 256, 16)
    nr = R // tr

    def body(chip_ref, *refs):
        own_refs, far_refs, rest = refs[:L], refs[L:2 * L], refs[2 * L:]
        outs = rest[3:] if wmv else rest
        for ll in range(L):
            @pl.when(pl.program_id(0) == ll)
            def _(ll=ll):
                g = own_refs[ll][...]
                for s in range(3):
                    g = g + far_refs[ll][s].astype(F32)
                outs[0][...] = g
                if wmv:
                    outs[1][...], outs[2][...], outs[3][...] = _adam_math(g, rest[0][...], rest[1][...], rest[2][...])

    def rows(ll):
        return lambda l, i: jnp.where(l == ll, i, jnp.where(l < ll, 0, nr - 1))

    blk = pl.BlockSpec((None, tr, C), lambda l, i, ch: (l, i, 0))
    in_specs = [pl.BlockSpec((None, tr, C), lambda l, i, ch, f=rows(ll): (ch[0], f(l, i), 0)) for ll in range(L)]
    in_specs += [pl.BlockSpec((3, tr, C), lambda l, i, ch, f=rows(ll): (0, f(l, i), 0)) for ll in range(L)]
    args = list(chip32) + list(far)
    n_out = 1
    if wmv:
        in_specs += [blk] * 3
        args += list(wmv)
        n_out = 4
    return pl.pallas_call(
        body, name=name,
        grid_spec=pltpu.PrefetchScalarGridSpec(num_scalar_prefetch=1, grid=(L, nr), in_specs=in_specs, out_specs=[blk] * n_out),
        out_shape=[jax.ShapeDtypeStruct((L, R, C), F32)] * n_out, compiler_params=_cparams(("arbitrary", "arbitrary")),
    )(chip, *args)


def _adamw(g, w, m, v, name):
    R, C = g.shape
    tr = _tile(R, 128, 8)

    def body(g_ref, w_ref, m_ref, v_ref, d_ref, nm_ref, nv_ref):
        d_ref[...], nm_ref[...], nv_ref[...] = _adam_math(g_ref[...], w_ref[...], m_ref[...], v_ref[...])

    blk = pl.BlockSpec((tr, C), lambda i: (i, 0))
    return pl.pallas_call(
        body, name=name, grid=(R // tr,), in_specs=[blk] * 4, out_specs=[blk] * 3,
        out_shape=[jax.ShapeDtypeStruct(g.shape, F32)] * 3, compiler_params=_cparams(("parallel",)),
    )(g, w, m, v)


def _pair_sum(x, recv, core, name):
    _, R, C = x.shape
    tr = _tile(R, 600, 16)

    def body(core_ref, x_ref, r_ref, o32_ref, o16_ref):
        s = x_ref[...] + r_ref[...]
        o32_ref[...] = s
        o16_ref[...] = s.astype(BF16)

    blk = pl.BlockSpec((None, tr, C), lambda q, i, c: (q, i, 0))
    mine = pl.BlockSpec((None, None, tr, C), lambda q, i, c: (q, c[0], i, 0))
    return pl.pallas_call(
        body, name=name,
        grid_spec=pltpu.PrefetchScalarGridSpec(num_scalar_prefetch=1, grid=(4, R // tr), in_specs=[mine, blk], out_specs=[blk, blk]),
        out_shape=[jax.ShapeDtypeStruct((4, R, C), F32), jax.ShapeDtypeStruct((4, R, C), BF16)],
        compiler_params=_cparams(("parallel", "parallel")),
    )(core, x.reshape(4, 2, R, C), recv)


def _sum_slots(x, name):
    def body(x_ref, o_ref):
        g = x_ref[0]
        for s in range(1, x.shape[0]):
            g = g + x_ref[s]
        o_ref[...] = g

    return pl.pallas_call(body, name=name, out_shape=jax.ShapeDtypeStruct(x.shape[1:], F32))(x)


MESH = pl.DeviceIdType.MESH
_HBM = pl.BlockSpec(memory_space=pltpu.HBM)


def _dma_sems(n):
    return pltpu.SemaphoreType.DMA((n,))


def _gather_many(xs, name):
    n = len(xs)

    def body(*refs):
        x_refs, out_refs = refs[:n], refs[n:2 * n]
        send_sems, recv_sems, local_sems = refs[2 * n:]
        ax, ay, ac = lax.axis_index("x"), lax.axis_index("y"), lax.axis_index("c")
        me, sibling = (ax, ay, ac), (ax, ay, 1 - ac)
        chips = [(1 - ax, ay), (ax, 1 - ay), (1 - ax, 1 - ay)]

        def copy(a, k, block, to, own=False):
            slot = out_refs[a].at[4 * block[0] + 2 * block[1] + block[2]]
            return pltpu.make_async_remote_copy(
                src_ref=x_refs[a] if own else slot, dst_ref=slot, send_sem=send_sems.at[7 * a + k],
                recv_sem=recv_sems.at[7 * a + k], device_id=to, device_id_type=MESH)

        mine = [pltpu.make_async_copy(x_refs[a], out_refs[a].at[4 * ax + 2 * ay + ac], local_sems.at[a]) for a in range(n)]
        first = [copy(a, 0, me, sibling, own=True) for a in range(n)]
        first += [copy(a, 1 + j, me, (*chip, ac), own=True) for j, chip in enumerate(chips) for a in range(n)]
        for cp in mine + first:
            cp.start()
        passed = []
        for j, chip in enumerate(chips):
            for a in range(n):
                copy(a, 1 + j, (*chip, ac), me).wait_recv()
                cp = copy(a, 4 + j, (*chip, ac), sibling)
                cp.start()
                passed.append(cp)
        for a in range(n):
            copy(a, 0, sibling, me).wait_recv()
            for j, chip in enumerate(chips):
                copy(a, 4 + j, (*chip, 1 - ac), me).wait_recv()
        for cp in first + passed:
            cp.wait_send()
        for cp in mine:
            cp.wait()

    return pl.pallas_call(
        body, name=name, out_shape=[jax.ShapeDtypeStruct((N_DEV,) + x.shape, x.dtype) for x in xs],
        in_specs=[_HBM] * n, out_specs=[_HBM] * n, scratch_shapes=[_dma_sems(7 * n), _dma_sems(7 * n), _dma_sems(n)],
    )(*xs)


_SEM = pl.BlockSpec(memory_space=pltpu.SEMAPHORE)
_EFFECT = pltpu.SideEffectType.DATAFLOW_SIDE_EFFECTING


def _peer(k):
    ax, ay, ac = lax.axis_index("x"), lax.axis_index("y"), lax.axis_index("c")
    px = 1 - ax if k & 4 else ax
    py = 1 - ay if k & 2 else ay
    pc = 1 - ac if k & 1 else ac
    return (px, py, pc), 4 * px + 2 * py + pc


def _build_gather(x_refs, land_refs, send_sems, recv_sems, waiting):
    _, me = _peer(0)
    copies = []
    for a in range(len(x_refs)):
        for k in range(1, N_DEV):
            peer, slot = _peer(k)
            copies.append(pltpu.make_async_remote_copy(
                src_ref=x_refs[a], dst_ref=land_refs[a].at[slot if waiting else me], send_sem=send_sems.at[7 * a + k - 1],
                recv_sem=recv_sems.at[7 * a + k - 1], device_id=peer, device_id_type=MESH))
    return copies


def _build_cores(x_refs, land_refs, send_sems, recv_sems, waiting):
    ax, ay, ac = lax.axis_index("x"), lax.axis_index("y"), lax.axis_index("c")
    copies = []
    for a in range(len(x_refs)):
        for q in range(4):
            copies.append(pltpu.make_async_remote_copy(
                src_ref=x_refs[a].at[2 * q + 1 - ac], dst_ref=land_refs[a].at[q], send_sem=send_sems.at[4 * a + q],
                recv_sem=recv_sems.at[4 * a + q], device_id=(ax, ay, 1 - ac), device_id_type=MESH))
    return copies


def _build_chips(p_refs, land_refs, send_sems, recv_sems, waiting):
    ax, ay, ac = lax.axis_index("x"), lax.axis_index("y"), lax.axis_index("c")
    copies = []
    for a in range(len(p_refs)):
        for k in range(1, 4):
            px = 1 - ax if k & 2 else ax
            py = 1 - ay if k & 1 else ay
            copies.append(pltpu.make_async_remote_copy(
                src_ref=p_refs[a].at[2 * px + py], dst_ref=land_refs[a].at[k - 1], send_sem=send_sems.at[3 * a + k - 1],
                recv_sem=recv_sems.at[3 * a + k - 1], device_id=(px, py, ac), device_id_type=MESH))
    return copies


_EXCHANGES = {"gather": (_build_gather, 7, N_DEV), "cores": (_build_cores, 4, 4), "chips": (_build_chips, 3, 3)}


def _exchange_start(kind, xs, lands, name, after=None):
    build, per, _ = _EXCHANGES[kind]
    n = len(xs)

    def body(*refs):
        for cp in build(refs[:n], refs[n:2 * n], refs[-2 * n - 3], refs[-2 * n - 2], False):
            cp.start()
        refs[-1][...] = jnp.zeros_like(refs[-1])

    hbm = lambda t: pltpu.HBM(t.shape, t.dtype)
    args = [pltpu.with_memory_space_constraint(t, pltpu.HBM) for t in list(xs) + list(lands)]
    in_specs = [_HBM] * (2 * n)
    if after is not None:
        args.append(after)
        in_specs.append(pl.BlockSpec(memory_space=pl.ANY))
    outs = pl.pallas_call(
        body, name=name,
        out_shape=(_dma_sems(per * n), _dma_sems(per * n), *[hbm(t) for t in xs], *[hbm(t) for t in lands],
                   jax.ShapeDtypeStruct((8, 128), F32)),
        in_specs=in_specs, out_specs=(_SEM, _SEM, *[_HBM] * (2 * n), pl.BlockSpec(memory_space=pltpu.VMEM)),
        input_output_aliases={a: 2 + a for a in range(2 * n)},
        compiler_params=pltpu.CompilerParams(has_side_effects=_EFFECT),
    )(*args)
    return (kind, outs[0], outs[1], outs[2:2 + n], outs[2 + n:2 + 2 * n]), outs[-1]


def _exchange_wait(flight, after, name):
    kind, send_sems, recv_sems, xs, lands = flight
    build = _EXCHANGES[kind][0]
    n = len(xs)

    def body(*refs):
        for cp in build(refs[:n], refs[n:2 * n], refs[2 * n], refs[2 * n + 1], True):
            cp.wait_send()
            cp.wait_recv()

    hbm = lambda t: pltpu.HBM(t.shape, t.dtype)
    outs = pl.pallas_call(
        body, name=name, out_shape=(*[hbm(t) for t in xs], *[hbm(t) for t in lands]),
        in_specs=[_HBM] * (2 * n) + [_SEM, _SEM, pl.BlockSpec(memory_space=pl.ANY)], out_specs=[_HBM] * (2 * n),
        input_output_aliases={a: a for a in range(2 * n)}, compiler_params=pltpu.CompilerParams(has_side_effects=_EFFECT),
    )(*xs, *lands, send_sems, recv_sems, after)
    return outs[:n], outs[n:]


def _x_view(xb, d):
    return xb if d == 1 else xb.reshape(xb.shape[0] // d, d * xb.shape[1])


def _layer_fwd(x, xb, p, w, cos, sin, rconsts, late=None):
    proj = _mm(xb, w["win"], tb=True, b_rows=(N_ATT, N_REST), name="mm_proj", out_dtype=BF16)
    qkvs, ogs, lgs = [], [], []
    for g, dil in enumerate(DILATIONS):
        qkv = _qkv_fwd(_x_view(xb, dil), w["win"], g, dil, f"mm_qkv{g}")
        o, l = _attn_fwd(qkv, dil, f"attn_fwd_g{g}")
        qkvs.append(qkv)
        ogs.append(_to_tokens(o, dil))
        lgs.append(_to_tokens(l, dil))
    attn, lse = _rowwise(_f_combine, ogs + lgs, [], [(D, BF16), (HD, F32)], [], name="attn_combine")
    ret_raw, states = _ret_fwd(proj, cos, sin, rconsts)
    rg_win = (proj, RH * RDV, OFF_RG // (RH * RDV))
    ga_win, gr_win = (proj, D, OFF_GA // D), (proj, D, OFF_GR // D)
    (r,) = _rowwise(_f_gn, [ret_raw, rg_win], [w["ret_gn_g"], w["ret_gn_b"]], [(RH * RDV, BF16)], [], name="gn_fwd", tm=256)
    if late is not None:
        w = {**w, **late(r)}
    ap = _mm(attn, w["w_attn_proj"], name="mm_attn_proj", out_dtype=BF16)
    rp = _mm(r, w["w_ret_proj"], name="mm_ret_proj", out_dtype=BF16, tk=2048)
    (merged,) = _rowwise(_f_gate, [ap, rp, ga_win, gr_win], [], [(D, BF16)], [], name="gate_fwd")
    mix = _mm(merged, w["w_out"], name="mm_out")
    h1, x1, x1b = _rowwise(_f_ln1, [x, mix], [w["ln1_g"], w["ln1_b"]], [(D, F32), (D, F32), (D, BF16)], [], name="ln1_fwd")
    z = _mm(x1b, w["w_ple_gate"], name="mm_ple_gate")
    pp = _mm(p, w["w_ple_proj"], tb=True, name="mm_ple_proj")
    hg = _mm(x1b, w["w_up"], tb=True, b_rows=(0, DFF), name="mm_up_g", out_dtype=BF16, tm=512, tn=DFF)
    hu = _mm(x1b, w["w_up"], tb=True, b_rows=(DFF, DFF), name="mm_up_u", out_dtype=BF16, tm=512, tn=DFF)
    act = _conv_fwd(hg, hu, w["conv_wg"], w["conv_wu"], w["conv_bg"], w["conv_bu"])
    ffn = _mm(act, w["w_down"], name="mm_down", tm=512, tk=DFF)
    h2, x2, x2b = _rowwise(_f_ln2, [x1, ffn, z, pp], [w["ln2_g"], w["ln2_b"]], [(D, F32), (D, F32), (D, BF16)], [], name="ln2_fwd")
    saved = dict(xb=xb, proj=proj, qkvs=qkvs, attn=attn, lse=lse, ret_raw=ret_raw, states=states, r=r, ap=ap, rp=rp,
                 merged=merged, h1=h1, x1b=x1b, z=z, pp=pp, hg=hg, hu=hu, act=act, h2=h2, p=p)
    return x2, x2b, saved, w


def _after(fn, token):
    return fn if token is None else (lambda *a: fn(*a[:-1]))


def _layer_bwd(dys, w, sv, cos, sin, rconsts, hooks):
    gr = {}
    proj = sv["proj"]
    call = lambda key, *a: hooks[key](*a) if key in hooks else None
    held = lambda token: [] if token is None else [token]
    token = hooks.get("token")
    dh2, dh2b, gr["ln2_g"], gr["ln2_b"] = _rowwise(_after(_f_ln_bwd, token), list(dys) + [sv["h2"]], [w["ln2_g"]] + held(token),
                                                   [(D, F32), (D, BF16)], [(1, D), (1, D)], name="ln2_bwd")
    d_act = _mm(dh2b, w["w_down"], tb=True, name="mm_down_dx", out_dtype=BF16, tm=512, tn=DFF)
    gr["w_down"] = _mm(sv["act"], dh2b, ta=True, name="mm_down_dw", tm=DFF // 2)
    dcg, dcu, gwg, gwu, gbg, gbu = _conv_bwd_pre(d_act, sv["hg"], sv["hu"], w["conv_wg"], w["conv_wu"], w["conv_bg"], w["conv_bu"])
    token = call("after_ffn", dcg)
    gr["conv_w"] = jnp.concatenate([gwg, gwu], axis=1)
    gr["conv_b"] = jnp.concatenate([gbg, gbu], axis=1)
    dhg = _conv_bwd_in(dcg, w["conv_wg"], "conv_bwd_in_g")
    dhu = _conv_bwd_in(dcu, w["conv_wu"], "conv_bwd_in_u")
    gw_up = _mm(dhg, sv["x1b"], ta=True, name="mm_up_g_dw", tm=DFF // 2, out_rows=(0, 2 * DFF))
    gr["w_up"] = _mm(dhu, sv["x1b"], ta=True, name="mm_up_u_dw", tm=DFF // 2, out_rows=(DFF, 2 * DFF), into=gw_up)
    dx1 = _mm(dhg, w["w_up"], b_rows=(0, DFF), name="mm_up_g_dx", add=dh2, add_scale=ALPHA, tm=512, tk=DFF)
    dx1 = _mm(dhu, w["w_up"], b_rows=(DFF, DFF), name="mm_up_u_dx", add=dx1, tm=512, tk=DFF)
    dpp, dz = _rowwise(_f_ple_bwd, [dh2, sv["z"], sv["pp"]], [], [(D, BF16), (D, BF16)], [], name="ple_bwd")
    gr["w_ple_proj"] = _mm(dpp, sv["p"], ta=True, name="mm_ple_proj_dw")
    gr["w_ple_gate"] = _mm(sv["x1b"], dz, ta=True, name="mm_ple_gate_dw")
    dx1 = _mm(dz, w["w_ple_gate"], tb=True, name="mm_ple_gate_dx", add=dx1)
    dh1, dh1b, gr["ln1_g"], gr["ln1_b"] = _rowwise(_after(_f_ln_bwd, token), [dx1, sv["h1"]], [w["ln1_g"]] + held(token),
                                                   [(D, F32), (D, BF16)], [(1, D), (1, D)], name="ln1_bwd")
    d_merged = _mm(dh1b, w["w_out"], tb=True, name="mm_out_dx", out_dtype=BF16)
    gr["w_out"] = _mm(sv["merged"], dh1b, ta=True, name="mm_out_dw")
    rg_win = (proj, RH * RDV, OFF_RG // (RH * RDV))
    ga_win, gr_win = (proj, D, OFF_GA // D), (proj, D, OFF_GR // D)
    dap, drp, d_rest = _rowwise(_f_gate_bwd, [d_merged, sv["ap"], sv["rp"], ga_win, gr_win], [],
                                [(D, BF16), (D, BF16), (2 * D, BF16, N_REST, OFF_GA // (2 * D), None)], [], name="gate_bwd")
    d_attn = _mm(dap, w["w_attn_proj"], tb=True, name="mm_attn_proj_dx", out_dtype=BF16)
    gr["w_attn_proj"] = _mm(sv["attn"], dap, ta=True, name="mm_attn_proj_dw")
    d_r = _mm(drp, w["w_ret_proj"], tb=True, name="mm_ret_proj_dx", out_dtype=BF16, tn=2048)
    gr["w_ret_proj"] = _mm(sv["r"], drp, ta=True, name="mm_ret_proj_dw", tm=2048)
    token = call("early_grads", gr)
    d_ret, d_rest, gr["ret_gn_g"], gr["ret_gn_b"] = _rowwise(
        _after(_f_gn_bwd, token), [d_r, sv["ret_raw"], rg_win], [w["ret_gn_g"], w["ret_gn_b"]] + held(token),
        [(RH * RDV, BF16), (RH * RDV, BF16, N_REST, OFF_RG // (RH * RDV), d_rest)],
        [(1, RH * RDV), (1, RH * RDV)], name="gn_bwd", tm=256)
    d_rest = _ret_bwd(proj, cos, sin, rconsts, sv["states"], d_ret, d_rest)
    token = call("after_ret", d_rest)
    (delta,) = _rowwise(_after(_f_delta, token), [d_attn, sv["attn"]], held(token), [(HD, F32)], [], name="attn_delta")
    gw_in, dqkvs = None, []
    for g, dil in enumerate(DILATIONS):
        dqkvs.append(_attn_bwd(sv["qkvs"][g], _to_head_residues(d_attn, dil), _to_residues(sv["lse"], dil),
                               _to_residues(delta, dil), dil, f"attn_bwd_g{g}"))
        gw_in = _qkv_dw(dqkvs[g], _x_view(sv["xb"], dil), g, dil, f"mm_qkv{g}_dw", into=gw_in)
    gw_in = _mm(d_rest, sv["xb"], ta=True, name="mm_proj_dw", out_rows=(N_ATT, N_IN), into=gw_in, blocks8=True)
    gr["w_in"] = gw_in.reshape(N_IN, D)
    token = call("w_in_ready", gr["w_in"])
    dx0 = _mm(d_rest, w["win"], b_rows=(N_ATT, N_REST), name="mm_proj_dx", add=dh1, add_scale=ALPHA, after=token)
    dx_parts = []
    for g, dil in enumerate(DILATIONS):
        if dil == 1:
            dx0 = _qkv_dx(dqkvs[g], w["win"], g, dil, f"mm_qkv{g}_dx", F32, add=dx0)
            token = call("after_dx0", dx0)
        else:
            dx_parts.append(_qkv_dx(dqkvs[g], w["win"], g, dil, f"mm_qkv{g}_dx", BF16, after=token).reshape(dx0.shape))
    return [dx0] + dx_parts, gr


def _local_step(x, p, positions, target, ws, own_hooks=None, on_grads=None):
    half = RDK // 2
    freq = jnp.power(ROPE_BASE, -jnp.arange(half, dtype=F32) / half)
    ang = positions.astype(F32)[:, None] * freq[None, :]
    cos, sin = jnp.cos(ang), jnp.sin(ang)
    rconsts = _ret_consts()
    xb = x.astype(BF16)
    saved, ws = [], list(ws)
    for l in range(DEPTH):
        first, late = ws[l] if isinstance(ws[l], tuple) else (ws[l], None)
        if callable(first):
            first = first(x)
        x, xb, sv, ws[l] = _layer_fwd(x, xb, p[l], first, cos, sin, rconsts, late)
        saved.append(sv)
    dy, loss_vec = _rowwise(_f_loss, [x, target], [], [(D, F32)], [(1, D)], name="loss")
    dys, grads = [dy], [None] * DEPTH
    from_above = {}
    for l in reversed(range(DEPTH)):
        hooks = {**from_above, **(own_hooks(l) if own_hooks else {})}
        dys, grads[l] = _layer_bwd(dys, ws[l], saved[l], cos, sin, rconsts, hooks)
        from_above = on_grads(l, grads[l]) if on_grads else {}
    (grad_x,) = _rowwise(_f_sum, dys, [], [(D, F32)], [], name="grad_x_sum")
    return loss_vec, grad_x, grads


def _pack_rows(arrs):
    parts, where, off = [], [], 0
    for t in arrs:
        t = t.reshape(-1, D)
        rows = t.shape[0]
        padded = -(-rows // 8) * 8
        parts.append(jnp.pad(t, ((0, padded - rows), (0, 0))))
        where.append((off, rows))
        off += padded
    return jnp.concatenate(parts, axis=0), where


FIRST = ("w_in",)
LATER = tuple(n for n in BIG if n not in FIRST)


def _first_weights(g, l, W):
    w = dict(win=g["w_in"].reshape(N_IN, D))
    for n in ("ret_gn_g", "ret_gn_b", "ln1_g", "ln1_b", "ln2_g", "ln2_b"):
        w[n] = W[n][l][None, :]
    return w


def _later_weights(g, l, conv_w_all, conv_b):
    w = dict(w_up=g["w_up"].reshape(2 * DFF, D), w_ple_proj=g["w_ple_proj"].reshape(D, PLE),
             w_attn_proj=g["w_attn_proj"].reshape(D, D), w_ret_proj=g["w_ret_proj"].reshape(RH * RDV, D),
             w_out=g["w_out"].reshape(D, D), w_down=g["w_down"].reshape(DFF, D), w_ple_gate=g["w_ple_gate"].reshape(D, D))
    w["conv_wg"], w["conv_wu"] = conv_w_all[l][:, :DFF], conv_w_all[l][:, DFF:]
    w["conv_bg"], w["conv_bu"] = conv_b[l][None, :DFF], conv_b[l][None, DFF:]
    return w


def _layer_weights(g, l, conv_w_all, conv_b, W):
    return {**_first_weights(g, l, W), **_later_weights(g, l, conv_w_all, conv_b)}


def kernel(x, p, positions, w_in, w_attn_proj, w_ret_proj, ret_gn_g, ret_gn_b, w_out, ln1_g, ln1_b, w_up, conv_w, conv_b, w_down, w_ple_gate, w_ple_proj, ln2_g, ln2_b, loss_target, m_w_in, m_w_attn_proj, m_w_ret_proj, m_ret_gn_g, m_ret_gn_b, m_w_out, m_ln1_g, m_ln1_b, m_w_up, m_conv_w, m_conv_b, m_w_down, m_w_ple_gate, m_w_ple_proj, m_ln2_g, m_ln2_b, v_w_in, v_w_attn_proj, v_w_ret_proj, v_ret_gn_g, v_ret_gn_b, v_w_out, v_ln1_g, v_ln1_b, v_w_up, v_conv_w, v_conv_b, v_w_down, v_w_ple_gate, v_w_ple_proj, v_ln2_g, v_ln2_b):
    W = dict(w_in=w_in, w_attn_proj=w_attn_proj, w_ret_proj=w_ret_proj, ret_gn_g=ret_gn_g, ret_gn_b=ret_gn_b, w_out=w_out,
             ln1_g=ln1_g, ln1_b=ln1_b, w_up=w_up, conv_w=conv_w, conv_b=conv_b, w_down=w_down, w_ple_gate=w_ple_gate,
             w_ple_proj=w_ple_proj, ln2_g=ln2_g, ln2_b=ln2_b)
    M = dict(w_in=m_w_in, w_attn_proj=m_w_attn_proj, w_ret_proj=m_w_ret_proj, ret_gn_g=m_ret_gn_g, ret_gn_b=m_ret_gn_b,
             w_out=m_w_out, ln1_g=m_ln1_g, ln1_b=m_ln1_b, w_up=m_w_up, conv_w=m_conv_w, conv_b=m_conv_b, w_down=m_w_down,
             w_ple_gate=m_w_ple_gate, w_ple_proj=m_w_ple_proj, ln2_g=m_ln2_g, ln2_b=m_ln2_b)
    V = dict(w_in=v_w_in, w_attn_proj=v_w_attn_proj, w_ret_proj=v_w_ret_proj, ret_gn_g=v_ret_gn_g, ret_gn_b=v_ret_gn_b,
             w_out=v_w_out, ln1_g=v_ln1_g, ln1_b=v_ln1_b, w_up=v_w_up, conv_w=v_conv_w, conv_b=v_conv_b, w_down=v_w_down,
             w_ple_gate=v_w_ple_gate, w_ple_proj=v_w_ple_proj, ln2_g=v_ln2_g, ln2_b=v_ln2_b)

    me = 4 * lax.axis_index("x") + 2 * lax.axis_index("y") + lax.axis_index("c")
    shard = lambda n, l: (W[n][l].T if n in COL_SHARDED else W[n][l]).astype(BF16)
    landing = lambda ts: [lax.dynamic_update_index_in_dim(lax.empty((N_DEV,) + t.shape, t.dtype), t, me, 0) for t in ts]
    first0 = _gather_many([shard(n, 0) for n in FIRST], "gather_first_l0")
    later0 = [shard(n, 0) for n in LATER] + [conv_w]
    flight0, token0 = _exchange_start("gather", later0, landing(later0), "gather_later_l0_start", after=first0[0])
    all1 = [shard(n, 1) for n in BIG]
    flight1, token1 = _exchange_start("gather", all1, landing(all1), "gather_weights_l1_start", after=token0)
    conv_w_all = []

    def later_first_layer(after):
        _, got = _exchange_wait(flight0, after, "gather_later_l0_wait")
        conv_w_all.append(got[-1].transpose(1, 2, 0, 3).reshape(DEPTH, 3, 2 * DFF))
        return _later_weights(dict(zip(LATER, got)), 0, conv_w_all[0], conv_b)

    def second_layer(after):
        _, got = _exchange_wait(flight1, after, "gather_weights_l1_wait")
        return _layer_weights(dict(zip(BIG, got)), 1, conv_w_all[0], conv_b, W)

    core = lax.axis_index("c").astype(jnp.int32).reshape(1)
    chip = (2 * lax.axis_index("x") + lax.axis_index("y")).astype(jnp.int32).reshape(1)
    empty_like = lambda ts, slots: [lax.empty((slots,) + t.shape[1:], t.dtype) for t in ts]
    chip32, far = [{} for _ in range(DEPTH)], [{} for _ in range(DEPTH)]
    pending = []

    def reduction(l, names, tag):
        state = {}

        def start(g):
            mine = [g[n].reshape((N_DEV, -1) + g[n].shape[1:]) for n in names]
            state["cores"], token = _exchange_start("cores", mine, empty_like(mine, 4), f"exchange_cores_{tag}_start")
            return token

        def onward(after):
            mine, theirs = _exchange_wait(state["cores"], after, f"exchange_cores_{tag}_wait")
            sums = [_pair_sum(a, b, core, f"pair_sum_l{l}_{n}") for a, b, n in zip(mine, theirs, names)]
            for n, s in zip(names, sums):
                chip32[l][n] = s[0]
            sent = [s[0 if n in F32_OVER_ICI else 1] for s, n in zip(sums, names)]
            flight, token = _exchange_start("chips", sent, empty_like(sent, 3), f"exchange_chips_{tag}_start")
            pending.append((l, names, flight, tag))
            return token

        return start, onward

    def on_grads(l, g):
        if l == 0:
            return {}
        start, onward = reduction(l, BIG, f"l{l}")
        return dict(token=start(g), after_ffn=onward)

    def own_hooks(l):
        if l != 0:
            return {}
        start_e, onward_e = reduction(0, LATER, "l0_later")
        start_w, onward_w = reduction(0, FIRST, "l0_first")
        return dict(early_grads=start_e, after_ret=onward_e, w_in_ready=lambda gw: start_w({"w_in": gw}), after_dx0=onward_w)

    ws = [(_first_weights(dict(zip(FIRST, first0)), 0, W), later_first_layer), second_layer]
    loss_vec, grad_x, grads = _local_step(x[0] + token1[0, 0], p[:, 0], positions[0], loss_target[0], ws, own_hooks, on_grads)
    loss = lax.psum(jnp.sum(loss_vec), ("x", "y", "c"))
    for l, names, flight, tag in pending:
        _, got = _exchange_wait(flight, grad_x, f"exchange_chips_{tag}_wait")
        far[l].update(zip(names, got))
    G, DW, NM, NV = ({} for _ in range(4))
    for n in BIG:
        chip32_n = [chip32[l][n] for l in range(DEPTH)]
        far_n = [far[l][n] for l in range(DEPTH)]
        if n in COL_SHARDED:
            G[n] = _reduce_tail(chip32_n, far_n, chip, f"reduced_{n}")[0].transpose(0, 2, 1)
            R2, C2 = DEPTH * W[n].shape[1], W[n].shape[2]
            res = _adamw(*(t.reshape(R2, C2) for t in (G[n], W[n], M[n], V[n])), f"adamw_{n}")
            DW[n], NM[n], NV[n] = (t.reshape(W[n].shape) for t in res)
        else:
            G[n], DW[n], NM[n], NV[n] = _reduce_tail(chip32_n, far_n, chip, f"adamw_{n}", wmv=(W[n], M[n], V[n]))

    small_names = SMALL + ("conv_w",)
    g_small, where = _pack_rows([jnp.stack([grads[l][n] for l in range(DEPTH)]) for n in small_names])
    (g_all,) = _gather_many([g_small], "gather_small_grads")
    g_small = _sum_slots(g_all, "sum_small_grads")
    for n, (off, rows) in zip(SMALL, where):
        G[n] = g_small[off:off + rows].reshape(W[n].shape)
    off, rows = where[-1]
    g_cw = g_small[off:off + rows].reshape(DEPTH, 3, N_DEV, conv_w.shape[2])
    G["conv_w"] = lax.dynamic_index_in_dim(g_cw, me, axis=2, keepdims=False)
    packed = [_pack_rows([d[n] for n in SMALL]) for d in (G, W, M, V)]
    small_out = _adamw(*(t for t, _ in packed), "adamw_small")
    for res, dst in zip(small_out, (DW, NM, NV)):
        for n, (off, rows) in zip(SMALL, packed[0][1]):
            dst[n] = res[off:off + rows].reshape(W[n].shape)
    two_d = lambda t: t.reshape(DEPTH * 3, conv_w.shape[2])
    cw_out = _adamw(two_d(G["conv_w"]), two_d(conv_w), two_d(m_conv_w), two_d(v_conv_w), "adamw_conv_w")
    for res, dst in zip(cw_out, (DW, NM, NV)):
        dst["conv_w"] = res.reshape(conv_w.shape)

    return (loss, grad_x[None], *[G[n] for n in WEIGHTS], *[DW[n] for n in WEIGHTS], *[NM[n] for n in WEIGHTS],
            *[NV[n] for n in WEIGHTS])
```

```python
import math

import numpy as np
import jax
import jax.numpy as jnp
from jax import lax
from jax.experimental import pallas as pl
from jax.experimental.pallas import tpu as pltpu

F32, BF16 = jnp.float32, jnp.bfloat16

D = 1024
DEPTH = 2
N_DEV = 8
HD = 128
NH = 8
DILATIONS = (1, 4, 16)
SPAN = 128
N_ATT = 3 * 3 * NH * HD
RH, RDK, RDV = 4, 256, 512
CH = 128
DFF = 2816
PLE = 256
N_IN = 17408
N_REST = N_IN - N_ATT
OFF_RQ, OFF_RK, OFF_RV, OFF_RG, OFF_GA, OFF_GR = 0, 1024, 2048, 4096, 6144, 7168
ALPHA = (2 * DEPTH) ** 0.25
LN_EPS, GN_EPS = 1e-5, 1e-6
ROPE_BASE = 10000.0
LR, B1, B2, EPS, WD, STEP = 0.001, 0.9, 0.999, 1e-8, 0.01, 10
VMEM_LIMIT = 48 * 1024 * 1024
NEG = -1e30

BIG = ("w_in", "w_attn_proj", "w_ret_proj", "w_out", "w_up", "w_down", "w_ple_gate", "w_ple_proj")
COL_SHARDED = ("w_in", "w_up", "w_ple_proj")
F32_OVER_ICI = ("w_attn_proj", "w_out", "w_ple_gate", "w_ple_proj")
SMALL = ("ret_gn_g", "ret_gn_b", "ln1_g", "ln1_b", "conv_b", "ln2_g", "ln2_b")
WEIGHTS = ("w_in", "w_attn_proj", "w_ret_proj", "ret_gn_g", "ret_gn_b", "w_out", "ln1_g", "ln1_b", "w_up",
           "conv_w", "conv_b", "w_down", "w_ple_gate", "w_ple_proj", "ln2_g", "ln2_b")


def _tile(n, cap, mult=128):
    if n <= cap:
        return n
    t = (cap // mult) * mult
    while n % t:
        t -= mult
    return t


def _cparams(sem):
    return pltpu.CompilerParams(dimension_semantics=sem, vmem_limit_bytes=VMEM_LIMIT)


def _dot(a, b, ca, cb):
    return lax.dot_general(a, b, (((ca,), (cb,)), ((), ())), preferred_element_type=F32)


def _bdot(a, b, ca, cb):
    return lax.dot_general(a, b, (((ca,), (cb,)), ((0,), (0,))), preferred_element_type=F32)


def _mm(a, b, *, name, ta=False, tb=False, out_dtype=F32, add=None, add_scale=1.0, tm=1024, tn=1024, tk=1024,
        b_rows=None, out_rows=None, into=None, blocks8=False, after=None):
    M, K = (a.shape[1], a.shape[0]) if ta else a.shape
    b_first, b_count = b_rows if b_rows else (0, b.shape[0])
    N = b_count if tb else b.shape[1]
    assert K == (b.shape[1] if tb else b_count)
    tm, tn, tk = _tile(M, tm), _tile(N, tn), _tile(K, tk)
    nk = K // tk
    o_first, o_total = out_rows if out_rows else (0, M)
    jb, kb, io = (b_first // tn, 0, o_first // tm) if tb else (0, b_first // tk, o_first // tm)
    assert b_first % (tn if tb else tk) == 0 and o_first % tm == 0 and (add is None or out_rows is None)

    def body(*refs):
        if add is None:
            a_ref, b_ref = refs[:2]
        else:
            a_ref, b_ref, add_ref = refs[:3]
        o_ref, acc_ref = refs[-2:]
        k = pl.program_id(2)

        @pl.when(k == 0)
        def _():
            acc_ref[...] = jnp.zeros_like(acc_ref)

        acc_ref[...] += _dot(a_ref[...].astype(BF16), b_ref[...].astype(BF16), 0 if ta else 1, 1 if tb else 0)

        @pl.when(k == nk - 1)
        def _():
            r = acc_ref[...]
            if add is not None:
                r = r + add_scale * add_ref[...].astype(F32)
            o_ref[...] = r.astype(out_dtype).reshape(o_ref.shape)

    a_spec = pl.BlockSpec((tk, tm), lambda i, j, k: (k, i)) if ta else pl.BlockSpec((tm, tk), lambda i, j, k: (i, k))
    if tb:
        b_spec = pl.BlockSpec((tn, tk), lambda i, j, k: (j + jb, k))
    else:
        b_spec = pl.BlockSpec((tk, tn), lambda i, j, k: (k + kb, j))
    if blocks8:
        assert tm == 1024
        o_spec = pl.BlockSpec((1, 8, 128, tn), lambda i, j, k: (i + io, 0, 0, j))
        o_shape = (o_total // tm, 8, 128, N)
    else:
        o_spec = pl.BlockSpec((tm, tn), lambda i, j, k: (i + io, j))
        o_shape = (o_total, N)
    in_specs, args, aliases = [a_spec, b_spec], [a, b], {}
    if add is not None:
        in_specs.append(o_spec)
        args.append(add)
    if after is not None:
        in_specs.append(pl.BlockSpec(memory_space=pl.ANY))
        args.append(after)
    if into is not None:
        aliases = {len(args): 0}
        in_specs.append(pl.BlockSpec(memory_space=pl.ANY))
        args.append(into)
    return pl.pallas_call(
        body, name=name, grid=(M // tm, N // tn, nk), in_specs=in_specs, out_specs=o_spec,
        out_shape=jax.ShapeDtypeStruct(o_shape, out_dtype), scratch_shapes=[pltpu.VMEM((tm, tn), F32)],
        input_output_aliases=aliases, compiler_params=_cparams(("parallel", "parallel", "arbitrary")),
    )(*args)


def _rowwise(fn, rows, pars, outs, accs, *, name, tm=512):
    first = rows[0][0] if isinstance(rows[0], tuple) else rows[0]
    S = first.shape[-2]
    tm = _tile(S, tm, 16)
    n_r, n_p, n_o = len(rows), len(pars), len(outs)
    outs = [o if len(o) == 5 else (o[0], o[1], o[0], 0, None) for o in outs]
    intos = [(k, o[4]) for k, o in enumerate(outs) if o[4] is not None]
    n_i = len(intos)

    def body(*refs):
        i = pl.program_id(0)
        vals = [r[...] for r in refs[:n_r + n_p]]
        res = fn(*vals)
        if not isinstance(res, (tuple, list)):
            res = (res,)
        o_refs = refs[n_r + n_p + n_i:n_r + n_p + n_i + n_o]
        a_refs = refs[n_r + n_p + n_i + n_o:]
        for r, v in zip(o_refs, res[:n_o]):
            r[...] = v.astype(r.dtype)
        if a_refs:
            @pl.when(i == 0)
            def _():
                for r in a_refs:
                    r[...] = jnp.zeros_like(r)

            for r, v in zip(a_refs, res[n_o:]):
                r[...] += v

    in_specs, args = [], []
    for r in rows:
        if isinstance(r, tuple):
            arr, w, cb = r
            in_specs.append(pl.BlockSpec((tm, w), lambda i, cb=cb: (i, cb)))
        elif r.ndim == 3:
            arr = r
            in_specs.append(pl.BlockSpec((arr.shape[0], tm, arr.shape[2]), lambda i: (0, i, 0)))
        else:
            arr = r
            in_specs.append(pl.BlockSpec((tm, arr.shape[1]), lambda i: (i, 0)))
        args.append(arr)
    for p_ in pars:
        in_specs.append(pl.BlockSpec(p_.shape, lambda i: (0, 0)))
        args.append(p_)
    aliases = {}
    for k, arr in intos:
        aliases[len(args)] = k
        in_specs.append(pl.BlockSpec(memory_space=pl.ANY))
        args.append(arr)
    out_shape = [jax.ShapeDtypeStruct((S, o[2]), o[1]) for o in outs] + [jax.ShapeDtypeStruct(a, F32) for a in accs]
    out_specs = [pl.BlockSpec((tm, o[0]), lambda i, cb=o[3]: (i, cb)) for o in outs] + [pl.BlockSpec(a, lambda i: (0, 0)) for a in accs]
    return pl.pallas_call(
        body, name=name, grid=(S // tm,), in_specs=in_specs, out_specs=out_specs, out_shape=out_shape,
        input_output_aliases=aliases, compiler_params=_cparams(("arbitrary",) if accs else ("parallel",)),
    )(*args)


def _norm(h, eps):
    mu = jnp.mean(h, -1, keepdims=True)
    d = h - mu
    rstd = lax.rsqrt(jnp.mean(d * d, -1, keepdims=True) + eps)
    return d * rstd, rstd


def _norm_bwd(dxh, xh, rstd):
    return rstd * (dxh - jnp.mean(dxh, -1, keepdims=True) - xh * jnp.mean(dxh * xh, -1, keepdims=True))


def _sig(x):
    return 1.0 / (1.0 + jnp.exp(-x))


_GELU_C = math.sqrt(2.0 / math.pi)


def _gelu(x):
    t = jnp.tanh(_GELU_C * (x + 0.044715 * x * x * x))
    return 0.5 * x * (1.0 + t), t


def _gelu_grad(x, t):
    return 0.5 * (1.0 + t) + 0.5 * x * (1.0 - t * t) * _GELU_C * (1.0 + 3 * 0.044715 * x * x)


def _f_ln1(x, mix, g, b):
    h = ALPHA * x + mix
    xh, _ = _norm(h, LN_EPS)
    y = xh * g + b
    return h, y, y


def _f_ln2(x, ffn, z, pp, g, b):
    h = ALPHA * x + ffn + _sig(z) * pp
    xh, _ = _norm(h, LN_EPS)
    y = xh * g + b
    return h, y, y


def _f_ln_bwd(*args):
    *dys, h, g = args
    dy = dys[0]
    for t in dys[1:]:
        dy = dy + t
    xh, rstd = _norm(h, LN_EPS)
    dh = _norm_bwd(dy * g, xh, rstd)
    return dh, dh, jnp.sum(dy * xh, 0, keepdims=True), jnp.sum(dy, 0, keepdims=True)


def _f_sum(*ts):
    r = ts[0]
    for t in ts[1:]:
        r = r + t
    return r


def _f_loss(y, t):
    e = y - t
    return e * (1.0 / D), jnp.sum(e * e, 0, keepdims=True) * (0.5 / D)


def _head_col(c, h):
    lane = lax.broadcasted_iota(jnp.int32, c.shape, 1)
    return jnp.sum(jnp.where(lane == h, c, 0.0), -1, keepdims=True)


def _f_combine(o0, o1, o2, l0, l1, l2):
    lane = lax.broadcasted_iota(jnp.int32, l0.shape, 1)
    parts, lse = [], jnp.zeros(l0.shape, F32)
    for h in range(NH):
        a0, a1, a2 = _head_col(l0, h), _head_col(l1, h), _head_col(l2, h)
        m = jnp.maximum(jnp.maximum(a0, a1), a2)
        e0, e1, e2 = jnp.exp(a0 - m), jnp.exp(a1 - m), jnp.exp(a2 - m)
        den = e0 + e1 + e2
        parts.append((e0 * o0[h].astype(F32) + e1 * o1[h].astype(F32) + e2 * o2[h].astype(F32)) / den)
        lse = jnp.where(lane == h, m + jnp.log(den), lse)
    return jnp.concatenate(parts, axis=1), lse


def _f_delta(da, a):
    lane = lax.broadcasted_iota(jnp.int32, (da.shape[0], HD), 1)
    out = jnp.zeros((da.shape[0], HD), F32)
    for h in range(NH):
        sl = slice(h * HD, (h + 1) * HD)
        s = jnp.sum(da[:, sl].astype(F32) * a[:, sl].astype(F32), -1, keepdims=True)
        out = jnp.where(lane == h, s, out)
    return out


def _f_gate(ap, rp, ga, gr):
    return _sig(ga.astype(F32)) * ap.astype(F32) + _sig(gr.astype(F32)) * rp.astype(F32)


def _f_gate_bwd(dm, ap, rp, ga, gr):
    dm = dm.astype(F32)
    sa, sr = _sig(ga.astype(F32)), _sig(gr.astype(F32))
    dga, dgr = dm * ap.astype(F32) * sa * (1.0 - sa), dm * rp.astype(F32) * sr * (1.0 - sr)
    return dm * sa, dm * sr, jnp.concatenate([dga, dgr], axis=1)


def _f_gn(y, rg, g, b):
    y, rg = y.astype(F32), rg.astype(F32)
    parts = []
    for h in range(RH):
        sl = slice(h * RDV, (h + 1) * RDV)
        xh, _ = _norm(y[:, sl], GN_EPS)
        parts.append(xh * g[:, sl] + b[:, sl])
    return rg * _sig(rg) * jnp.concatenate(parts, axis=1)


def _f_gn_bwd(dr, y, rg, g, b):
    dr, y, rg = dr.astype(F32), y.astype(F32), rg.astype(F32)
    s = _sig(rg)
    d_out = dr * rg * s
    dys, outs, xhs = [], [], []
    for h in range(RH):
        sl = slice(h * RDV, (h + 1) * RDV)
        xh, rstd = _norm(y[:, sl], GN_EPS)
        xhs.append(xh)
        outs.append(xh * g[:, sl] + b[:, sl])
        dys.append(_norm_bwd(d_out[:, sl] * g[:, sl], xh, rstd))
    xh, out = jnp.concatenate(xhs, axis=1), jnp.concatenate(outs, axis=1)
    d_rg = dr * out * s * (1.0 + rg * (1.0 - s))
    return jnp.concatenate(dys, axis=1), d_rg, jnp.sum(d_out * xh, 0, keepdims=True), jnp.sum(d_out, 0, keepdims=True)


def _f_ple_bwd(dh, z, pp):
    s = _sig(z)
    return dh * s, dh * pp * s * (1.0 - s)


QKV = 3 * HD


def _to_tokens(t, d):
    if d == 1:
        return t
    *lead, S, C = t.shape
    n = len(lead)
    perm = tuple(range(n)) + (n + 1, n, n + 2)
    return t.reshape(*lead, d, S // d, C).transpose(perm).reshape(*lead, S, C)


def _to_residues(t, d):
    if d == 1:
        return t
    S, C = t.shape
    return t.reshape(S // d, d, C).transpose(1, 0, 2).reshape(S, C)


def _to_head_residues(t, d):
    S = t.shape[0]
    return t.reshape(S // d, d, NH, HD).transpose(2, 1, 0, 3).reshape(NH, S, HD)


def _w_qkv_specs(g):
    return [pl.BlockSpec((D, D), lambda *i, t=t: (3 * g + t, 0)) for t in range(3)]


def _qkv_fwd(xv, win, g, dil, name):
    Sd = xv.shape[0]
    S = Sd * dil
    tm = min(512, Sd)
    nma = Sd // tm

    def body(a_ref, wq_ref, wk_ref, wv_ref, o_ref):
        a = a_ref[...]
        q, k, v = (_dot(a, w_ref[...], 1, 1).astype(BF16) for w_ref in (wq_ref, wk_ref, wv_ref))
        for h in range(NH):
            sl = slice(h * HD, (h + 1) * HD)
            o_ref[h] = jnp.concatenate([q[:, sl], k[:, sl], v[:, sl]], axis=1)

    return pl.pallas_call(
        body, name=name, grid=(S // tm,),
        in_specs=[pl.BlockSpec((tm, D), lambda i: (i % nma, i // nma))] + _w_qkv_specs(g),
        out_specs=pl.BlockSpec((NH, tm, QKV), lambda i: (0, i, 0)), out_shape=jax.ShapeDtypeStruct((NH, S, QKV), BF16),
        compiler_params=_cparams(("parallel",)),
    )(xv, win, win, win)


def _qkv_dx(dqkv, win, g, dil, name, out_dtype, add=None, after=None):
    S = dqkv.shape[1]
    Sd = S // dil
    tm = min(512, Sd)
    nmo = Sd // tm

    def body(*refs):
        a_ref, wq_ref, wk_ref, wv_ref = refs[:4]
        o_ref = refs[-1]
        acc = None
        for t, w_ref in enumerate((wq_ref, wk_ref, wv_ref)):
            d = jnp.concatenate([a_ref[h][:, t * HD:(t + 1) * HD] for h in range(NH)], axis=1)
            part = _dot(d, w_ref[...], 1, 0)
            acc = part if acc is None else acc + part
        if add is not None:
            acc = acc + refs[4][...]
        o_ref[...] = acc.astype(out_dtype)

    o_spec = pl.BlockSpec((tm, D), lambda i: (i % nmo, i // nmo))
    in_specs = [pl.BlockSpec((NH, tm, QKV), lambda i: (0, i, 0))] + _w_qkv_specs(g)
    args = [dqkv, win, win, win]
    if add is not None:
        assert dil == 1
        in_specs.append(o_spec)
        args.append(add)
    if after is not None:
        in_specs.append(pl.BlockSpec(memory_space=pl.ANY))
        args.append(after)
    return pl.pallas_call(
        body, name=name, grid=(S // tm,), in_specs=in_specs, out_specs=o_spec,
        out_shape=jax.ShapeDtypeStruct((Sd, dil * D), out_dtype), compiler_params=_cparams(("parallel",)),
    )(*args)


GW_IN_BLOCKS = (N_IN // D, NH, HD, D)


def _qkv_dw(dqkv, xv, g, dil, name, into=None):
    S = dqkv.shape[1]
    Sd = S // dil
    tk = min(1024, Sd)
    nkb, nk = Sd // tk, S // tk
    hh = NH // 2

    def body(*refs):
        a_ref, b_ref = refs[:2]
        o_ref, acc_ref = refs[-2:]
        k = pl.program_id(1)

        @pl.when(k == 0)
        def _():
            acc_ref[...] = jnp.zeros_like(acc_ref)

        b = b_ref[...]
        for h in range(hh):
            acc_ref[h * QKV:(h + 1) * QKV, :] += _dot(a_ref[h], b, 0, 0)

        @pl.when(k == nk - 1)
        def _():
            for h in range(hh):
                for t in range(3):
                    o_ref[t, h] = acc_ref[h * QKV + t * HD:h * QKV + (t + 1) * HD, :]

    in_specs = [pl.BlockSpec((hh, tk, QKV), lambda j, k: (j, k, 0)), pl.BlockSpec((tk, D), lambda j, k: (k % nkb, k // nkb))]
    args, aliases = [dqkv, xv], {}
    if into is not None:
        aliases = {2: 0}
        in_specs.append(pl.BlockSpec(memory_space=pl.ANY))
        args.append(into)
    return pl.pallas_call(
        body, name=name, grid=(2, nk), in_specs=in_specs,
        out_specs=pl.BlockSpec((3, hh, HD, D), lambda j, k: (g, j, 0, 0)), out_shape=jax.ShapeDtypeStruct(GW_IN_BLOCKS, F32),
        input_output_aliases=aliases, scratch_shapes=[pltpu.VMEM((hh * QKV, D), F32)],
        compiler_params=_cparams(("parallel", "arbitrary")),
    )(*args)


def _band(nb, first_valid, last_valid=None):
    b = lax.broadcasted_iota(jnp.int32, (nb, SPAN, SPAN), 0)
    row = lax.broadcasted_iota(jnp.int32, (nb, SPAN, SPAN), 1)
    col = lax.broadcasted_iota(jnp.int32, (nb, SPAN, SPAN), 2)
    off = jnp.where(b == 0, jnp.where(first_valid, 0, 2 * SPAN), 0)
    if last_valid is not None:
        off = off + jnp.where(b == nb - 1, jnp.where(last_valid, 0, 2 * SPAN), 0)
    return col <= row, col >= row + off


def _attn_tiles(S, dil):
    Sd = S // dil
    T = min(1024, Sd)
    hp = min(NH, max(1, (S // T) * NH // 16))
    return Sd, T, T // SPAN, Sd // T, hp


def _attn_fwd(qkv, dil, name):
    S = qkv.shape[1]
    Sd, T, nsub, nib, hp = _attn_tiles(S, dil)
    scale = HD ** -0.5

    def body(c_ref, p_ref, o_ref, l_ref):
        ib, hb = pl.program_id(1), pl.program_id(2)
        m_cur, m_prev = _band(nsub, ib > 0)
        lane = lax.broadcasted_iota(jnp.int32, (T, HD), 1)

        @pl.when(hb == 0)
        def _():
            l_ref[...] = jnp.zeros_like(l_ref)

        lses = l_ref[...]
        for hh in range(hp):
            blk, hal = c_ref[hh], p_ref[hh]
            q, k, v = blk[:, :HD], blk[:, HD:2 * HD], blk[:, 2 * HD:]
            if nsub > 1:
                kp = jnp.concatenate([hal[:, HD:2 * HD], k[:T - SPAN]], axis=0)
                vp = jnp.concatenate([hal[:, 2 * HD:], v[:T - SPAN]], axis=0)
            else:
                kp, vp = hal[:, HD:2 * HD], hal[:, 2 * HD:]
            q3, k3, v3, kp3, vp3 = (t.reshape(nsub, SPAN, HD) for t in (q, k, v, kp, vp))
            sc = jnp.where(m_cur, _bdot(q3, k3, 2, 2) * scale, NEG)
            sp = jnp.where(m_prev, _bdot(q3, kp3, 2, 2) * scale, NEG)
            m = jnp.maximum(jnp.max(sc, -1, keepdims=True), jnp.max(sp, -1, keepdims=True))
            pc, pp = jnp.exp(sc - m), jnp.exp(sp - m)
            den = jnp.sum(pc, -1, keepdims=True) + jnp.sum(pp, -1, keepdims=True)
            o = (_bdot(pc.astype(BF16), v3, 2, 1) + _bdot(pp.astype(BF16), vp3, 2, 1)) / den
            o_ref[hh] = o.reshape(T, HD).astype(BF16)
            lses = jnp.where(lane == hb * hp + hh, (m + jnp.log(den)).reshape(T, 1), lses)
        l_ref[...] = lses

    cur = pl.BlockSpec((hp, T, QKV), lambda r, ib, h: (h, r * nib + ib, 0))
    prev = pl.BlockSpec((hp, SPAN, QKV), lambda r, ib, h: (h, r * (Sd // SPAN) + jnp.maximum(ib * nsub - 1, 0), 0))
    return pl.pallas_call(
        body, name=name, grid=(dil, nib, NH // hp), in_specs=[cur, prev],
        out_specs=[pl.BlockSpec((hp, T, HD), lambda r, ib, h: (h, r * nib + ib, 0)),
                   pl.BlockSpec((T, HD), lambda r, ib, h: (r * nib + ib, 0))],
        out_shape=[jax.ShapeDtypeStruct((NH, S, HD), BF16), jax.ShapeDtypeStruct((S, HD), F32)],
        compiler_params=_cparams(("parallel", "parallel", "arbitrary")),
    )(qkv, qkv)


def _attn_bwd(qkv, d_attn, lse, delta, dil, name):
    S = qkv.shape[1]
    Sd, T, nsub, nib, hp = _attn_tiles(S, dil)
    scale = HD ** -0.5
    ne = nsub + 1

    def body(c_ref, p_ref, n_ref, do_ref, don_ref, l_ref, ln_ref, dl_ref, dln_ref, o_ref):
        ib, hb = pl.program_id(1), pl.program_id(2)
        _, m_prev = _band(ne, ib > 0, ib < nib - 1)
        m_cur, _ = _band(nsub, True)
        for hh in range(hp):
            h = hb * hp + hh
            blk, hal, nxt = c_ref[hh], p_ref[hh], n_ref[hh]
            q, k, v = blk[:, :HD], blk[:, HD:2 * HD], blk[:, 2 * HD:]
            do = do_ref[hh]
            l, dl = _head_col(l_ref[...], h), _head_col(dl_ref[...], h)
            qe = jnp.concatenate([q, nxt[:, :HD]], axis=0).reshape(ne, SPAN, HD)
            doe = jnp.concatenate([do, don_ref[hh]], axis=0).reshape(ne, SPAN, HD)
            le = jnp.concatenate([l, _head_col(ln_ref[...], h)], axis=0).reshape(ne, SPAN, 1)
            dle = jnp.concatenate([dl, _head_col(dln_ref[...], h)], axis=0).reshape(ne, SPAN, 1)
            kpe = jnp.concatenate([hal[:, HD:2 * HD], k], axis=0).reshape(ne, SPAN, HD)
            vpe = jnp.concatenate([hal[:, 2 * HD:], v], axis=0).reshape(ne, SPAN, HD)
            p = jnp.where(m_prev, jnp.exp(_bdot(qe, kpe, 2, 2) * scale - le), 0.0)
            ds = (p * (_bdot(doe, vpe, 2, 2) - dle)).astype(BF16)
            dq = _bdot(ds, kpe, 2, 1)[:nsub]
            dk = _bdot(ds, qe, 1, 1)[1:]
            dv = _bdot(p.astype(BF16), doe, 1, 1)[1:]
            q3, k3, v3, do3 = (t.reshape(nsub, SPAN, HD) for t in (q, k, v, do))
            l3, dl3 = l.reshape(nsub, SPAN, 1), dl.reshape(nsub, SPAN, 1)
            p = jnp.where(m_cur, jnp.exp(_bdot(q3, k3, 2, 2) * scale - l3), 0.0)
            ds = (p * (_bdot(do3, v3, 2, 2) - dl3)).astype(BF16)
            dq = (dq + _bdot(ds, k3, 2, 1)) * scale
            dk = (dk + _bdot(ds, q3, 1, 1)) * scale
            dv = dv + _bdot(p.astype(BF16), do3, 1, 1)
            o_ref[hh] = jnp.concatenate([t.reshape(T, HD) for t in (dq, dk, dv)], axis=1).astype(BF16)

    nb = Sd // SPAN
    row = lambda r, ib: r * nib + ib
    prow = lambda r, ib: r * nb + jnp.maximum(ib * nsub - 1, 0)
    nrow = lambda r, ib: r * nb + jnp.minimum((ib + 1) * nsub, nb - 1)
    cur3 = pl.BlockSpec((hp, T, QKV), lambda r, ib, h: (h, row(r, ib), 0))
    prev3 = pl.BlockSpec((hp, SPAN, QKV), lambda r, ib, h: (h, prow(r, ib), 0))
    next3 = pl.BlockSpec((hp, SPAN, QKV), lambda r, ib, h: (h, nrow(r, ib), 0))
    cur1 = pl.BlockSpec((hp, T, HD), lambda r, ib, h: (h, row(r, ib), 0))
    next1 = pl.BlockSpec((hp, SPAN, HD), lambda r, ib, h: (h, nrow(r, ib), 0))
    curc = pl.BlockSpec((T, HD), lambda r, ib, h: (row(r, ib), 0))
    nextc = pl.BlockSpec((SPAN, HD), lambda r, ib, h: (nrow(r, ib), 0))
    return pl.pallas_call(
        body, name=name, grid=(dil, nib, NH // hp),
        in_specs=[cur3, prev3, next3, cur1, next1, curc, nextc, curc, nextc], out_specs=cur3,
        out_shape=jax.ShapeDtypeStruct((NH, S, QKV), BF16),
        compiler_params=_cparams(("parallel", "parallel", "parallel")),
    )(qkv, qkv, qkv, d_attn, d_attn, lse, lse, delta, delta)


def _ret_consts():
    lg = np.log1p(-np.exp2(-5.0 - np.arange(RH, dtype=np.float64)))
    idx = np.arange(CH, dtype=np.float64)
    rel = idx[:, None] - idx[None, :]
    intra = np.where(rel >= 0, np.exp(lg[:, None, None] * np.maximum(rel, 0.0)), 0.0)
    qd = np.exp(lg[:, None] * (idx + 1.0))
    kd = np.exp(lg[:, None] * (CH - 1.0 - idx))
    cd = np.exp(lg * CH)
    wide = lambda t: np.broadcast_to(t[:, :, None], (RH, t.shape[1], RDV))
    return (jnp.asarray(intra, F32), jnp.asarray(wide(qd), F32), jnp.asarray(wide(kd), F32),
            jnp.asarray(np.broadcast_to(cd[:, None, None], (RH, 1, RDV)), F32))


def _rot(t, c, s):
    t1, t2 = t[:, :RDK // 2], t[:, RDK // 2:]
    return jnp.concatenate([t1 * c - t2 * s, t1 * s + t2 * c], axis=1)


def _unrot(d, c, s):
    d1, d2 = d[:, :RDK // 2], d[:, RDK // 2:]
    return jnp.concatenate([d1 * c + d2 * s, d2 * c - d1 * s], axis=1)


RCH = 2


def _ret_specs(nmap):
    rows = RCH * CH
    q = pl.BlockSpec((rows, RH * RDK), lambda n: (nmap(n), OFF_RQ // (RH * RDK)))
    k = pl.BlockSpec((rows, RH * RDK), lambda n: (nmap(n), OFF_RK // (RH * RDK)))
    v = pl.BlockSpec((rows, RH * RDV), lambda n: (nmap(n), OFF_RV // (RH * RDV)))
    cs = pl.BlockSpec((rows, RDK // 2), lambda n: (nmap(n), 0))
    dmat = pl.BlockSpec((RH, CH, CH), lambda n: (0, 0, 0))
    dvec = pl.BlockSpec((RH, CH, RDV), lambda n: (0, 0, 0))
    cdv = pl.BlockSpec((RH, 1, RDV), lambda n: (0, 0, 0))
    state = pl.BlockSpec((RH, RCH, RDK, RDV), lambda n: (0, nmap(n), 0, 0))
    out = pl.BlockSpec((rows, RH * RDV), lambda n: (nmap(n), 0))
    return [q, k, v, cs, cs, dmat, dvec, dvec, cdv], state, out


def _ret_fwd(proj, cos, sin, consts):
    S = proj.shape[0]
    nc = S // CH

    def body(q_ref, k_ref, v_ref, c_ref, s_ref, d_ref, qd_ref, kd_ref, cd_ref, o_ref, st_ref, state):
        @pl.when(pl.program_id(0) == 0)
        def _():
            state[...] = jnp.zeros_like(state)

        for ci in range(RCH):
            rows = slice(ci * CH, (ci + 1) * CH)
            c, s = c_ref[rows, :], s_ref[rows, :]
            for h in range(RH):
                qk, vv = slice(h * RDK, (h + 1) * RDK), slice(h * RDV, (h + 1) * RDV)
                qb = _rot(q_ref[rows, qk].astype(F32), c, s).astype(BF16)
                kb = (_rot(k_ref[rows, qk].astype(F32), c, s) * (RDK ** -0.5)).astype(BF16)
                vb = v_ref[rows, vv]
                sb = state[h].astype(BF16)
                st_ref[h, ci] = sb
                a = (_dot(qb, kb, 1, 1) * d_ref[h]).astype(BF16)
                o_ref[rows, vv] = (_dot(a, vb, 1, 0) + _dot(qb, sb, 1, 0) * qd_ref[h]).astype(BF16)
                vk = (vb.astype(F32) * kd_ref[h]).astype(BF16)
                state[h] = cd_ref[h] * state[h] + _dot(kb, vk, 0, 0)

    ins, state_spec, out_spec = _ret_specs(lambda n: n)
    return pl.pallas_call(
        body, name="ret_fwd", grid=(nc // RCH,), in_specs=ins, out_specs=[out_spec, state_spec],
        out_shape=[jax.ShapeDtypeStruct((S, RH * RDV), BF16), jax.ShapeDtypeStruct((RH, nc, RDK, RDV), BF16)],
        scratch_shapes=[pltpu.VMEM((RH, RDK, RDV), F32)],
        compiler_params=_cparams(("arbitrary",)),
    )(proj, proj, proj, cos, sin, *consts)


def _ret_bwd(proj, cos, sin, consts, states, d_ret, d_rest):
    S = proj.shape[0]
    nc = S // CH

    def body(q_ref, k_ref, v_ref, c_ref, s_ref, d_ref, qd_ref, kd_ref, cd_ref, st_ref, do_ref, _, o_ref, dstate):
        @pl.when(pl.program_id(0) == 0)
        def _():
            dstate[...] = jnp.zeros_like(dstate)

        for ci in reversed(range(RCH)):
            rows = slice(ci * CH, (ci + 1) * CH)
            c, s = c_ref[rows, :], s_ref[rows, :]
            for h in range(RH):
                qk, vv = slice(h * RDK, (h + 1) * RDK), slice(h * RDV, (h + 1) * RDV)
                qb = _rot(q_ref[rows, qk].astype(F32), c, s).astype(BF16)
                kb = (_rot(k_ref[rows, qk].astype(F32), c, s) * (RDK ** -0.5)).astype(BF16)
                vb, sb, do = v_ref[rows, vv], st_ref[h, ci], do_ref[rows, vv]
                dmat, qd, kd = d_ref[h], qd_ref[h], kd_ref[h]
                a = (_dot(qb, kb, 1, 1) * dmat).astype(BF16)
                doq = (do.astype(F32) * qd).astype(BF16)
                dsb = dstate[h].astype(BF16)
                vk = (vb.astype(F32) * kd).astype(BF16)
                o_ref[rows, OFF_RV + h * RDV:OFF_RV + (h + 1) * RDV] = (_dot(a, do, 0, 0) + _dot(kb, dsb, 1, 0) * kd).astype(BF16)
                da = (_dot(do, vb, 1, 1) * dmat).astype(BF16)
                dq = _dot(da, kb, 1, 0) + _dot(doq, sb, 1, 1)
                dk = (_dot(da, qb, 0, 0) + _dot(vk, dsb, 1, 1)) * (RDK ** -0.5)
                o_ref[rows, OFF_RQ + h * RDK:OFF_RQ + (h + 1) * RDK] = _unrot(dq, c, s).astype(BF16)
                o_ref[rows, OFF_RK + h * RDK:OFF_RK + (h + 1) * RDK] = _unrot(dk, c, s).astype(BF16)
                dstate[h] = cd_ref[h] * dstate[h] + _dot(qb, doq, 0, 0)

    nsteps = nc // RCH
    rev = lambda n: nsteps - 1 - n
    ins, state_spec, out_spec = _ret_specs(rev)
    return pl.pallas_call(
        body, name="ret_bwd", grid=(nsteps,), in_specs=ins + [state_spec, out_spec, pl.BlockSpec(memory_space=pl.ANY)],
        out_specs=pl.BlockSpec((RCH * CH, OFF_RG), lambda n: (rev(n), 0)),
        out_shape=jax.ShapeDtypeStruct(d_rest.shape, BF16), input_output_aliases={11: 0},
        scratch_shapes=[pltpu.VMEM((RH, RDK, RDV), F32)],
        compiler_params=_cparams(("arbitrary",)),
    )(proj, proj, proj, cos, sin, *consts, states, d_ret, d_rest)


CW = 256
HALO = 16


def _shift_down(v, halo, k):
    rolled = pltpu.roll(v, k, 0)
    hr = pltpu.roll(halo, k, 0)[0:8]
    row = lax.broadcasted_iota(jnp.int32, hr.shape, 0)
    return jnp.concatenate([jnp.where(row < k, hr, rolled[0:8]), rolled[8:]], axis=0)


def _shift_up(v, halo, k):
    T = v.shape[0]
    rolled = pltpu.roll(v, T - k, 0)
    hr = pltpu.roll(halo, 8 - k, 0)[0:8]
    row = lax.broadcasted_iota(jnp.int32, hr.shape, 0)
    return jnp.concatenate([rolled[:T - 8], jnp.where(row >= 8 - k, hr, rolled[T - 8:])], axis=0)


def _conv_taps(h_ref, hp_ref, first):
    h = h_ref[...].astype(F32)
    hp = hp_ref[...].astype(F32) * jnp.where(first, 0.0, 1.0)
    return _shift_down(h, hp, 2), _shift_down(h, hp, 1), h


def _conv_specs(S, T, cw=CW):
    nj = DFF // cw
    cur = pl.BlockSpec((T, cw), lambda j, i: (i, j))
    prev = pl.BlockSpec((HALO, cw), lambda j, i: (jnp.maximum(i * (T // HALO) - 1, 0), j))
    nxt = pl.BlockSpec((HALO, cw), lambda j, i: (jnp.minimum((i + 1) * (T // HALO), S // HALO - 1), j))
    w = pl.BlockSpec((3, cw), lambda j, i: (0, j))
    b = pl.BlockSpec((1, cw), lambda j, i: (0, j))
    return nj, cur, prev, nxt, w, b


def _conv_fwd(hg, hu, wg, wu, bg, bu):
    S = hg.shape[0]
    T = min(1024, S)
    nj, cur, prev, _, w, b = _conv_specs(S, T)

    def body(hg_ref, hu_ref, hgp_ref, hup_ref, wg_ref, wu_ref, bg_ref, bu_ref, o_ref):
        first = pl.program_id(1) == 0
        g2, g1, g0 = _conv_taps(hg_ref, hgp_ref, first)
        u2, u1, u0 = _conv_taps(hu_ref, hup_ref, first)
        cg = wg_ref[0:1, :] * g2 + wg_ref[1:2, :] * g1 + wg_ref[2:3, :] * g0 + bg_ref[...]
        cu = wu_ref[0:1, :] * u2 + wu_ref[1:2, :] * u1 + wu_ref[2:3, :] * u0 + bu_ref[...]
        o_ref[...] = (_gelu(cg)[0] * cu).astype(BF16)

    return pl.pallas_call(
        body, name="conv_fwd", grid=(nj, S // T), in_specs=[cur, cur, prev, prev, w, w, b, b], out_specs=cur,
        out_shape=jax.ShapeDtypeStruct((S, DFF), BF16), compiler_params=_cparams(("parallel", "parallel")),
    )(hg, hu, hg, hu, wg, wu, bg, bu)


def _conv_bwd_pre(d_act, hg, hu, wg, wu, bg, bu):
    S = hg.shape[0]
    T = min(1024, S)
    nj, cur, prev, _, w, b = _conv_specs(S, T)

    def body(da_ref, hg_ref, hu_ref, hgp_ref, hup_ref, wg_ref, wu_ref, bg_ref, bu_ref,
             dcg_ref, dcu_ref, gwg_ref, gwu_ref, gbg_ref, gbu_ref):
        first = pl.program_id(1) == 0
        g2, g1, g0 = _conv_taps(hg_ref, hgp_ref, first)
        u2, u1, u0 = _conv_taps(hu_ref, hup_ref, first)
        cg = wg_ref[0:1, :] * g2 + wg_ref[1:2, :] * g1 + wg_ref[2:3, :] * g0 + bg_ref[...]
        cu = wu_ref[0:1, :] * u2 + wu_ref[1:2, :] * u1 + wu_ref[2:3, :] * u0 + bu_ref[...]
        da = da_ref[...].astype(F32)
        gl, t = _gelu(cg)
        dcg = da * cu * _gelu_grad(cg, t)
        dcu = da * gl
        dcg_ref[...] = dcg.astype(BF16)
        dcu_ref[...] = dcu.astype(BF16)

        @pl.when(first)
        def _():
            for r in (gwg_ref, gwu_ref, gbg_ref, gbu_ref):
                r[...] = jnp.zeros_like(r)

        for r, d, taps in ((gwg_ref, dcg, (g2, g1, g0)), (gwu_ref, dcu, (u2, u1, u0))):
            for j in range(3):
                r[j:j + 1, :] += jnp.sum(d * taps[j], 0, keepdims=True)
        gbg_ref[...] += jnp.sum(dcg, 0, keepdims=True)
        gbu_ref[...] += jnp.sum(dcu, 0, keepdims=True)

    return pl.pallas_call(
        body, name="conv_bwd_pre", grid=(nj, S // T), in_specs=[cur, cur, cur, prev, prev, w, w, b, b],
        out_specs=[cur, cur, w, w, b, b],
        out_shape=[jax.ShapeDtypeStruct((S, DFF), BF16)] * 2 + [jax.ShapeDtypeStruct((3, DFF), F32)] * 2
        + [jax.ShapeDtypeStruct((1, DFF), F32)] * 2,
        compiler_params=_cparams(("parallel", "arbitrary")),
    )(d_act, hg, hu, hg, hu, wg, wu, bg, bu)


def _conv_bwd_in(dc, w, name):
    S = dc.shape[0]
    T = min(512, S)
    nj, cur, _, nxt, wspec, _ = _conv_specs(S, T, DFF // 2)
    nt = S // T

    def body(dc_ref, dn_ref, w_ref, o_ref):
        d = dc_ref[...].astype(F32)
        dn = dn_ref[...].astype(F32) * jnp.where(pl.program_id(1) == nt - 1, 0.0, 1.0)
        o_ref[...] = (w_ref[2:3, :] * d + w_ref[1:2, :] * _shift_up(d, dn, 1) + w_ref[0:1, :] * _shift_up(d, dn, 2)).astype(BF16)

    return pl.pallas_call(
        body, name=name, grid=(nj, nt), in_specs=[cur, nxt, wspec], out_specs=cur,
        out_shape=jax.ShapeDtypeStruct((S, DFF), BF16), compiler_params=_cparams(("parallel", "parallel")),
    )(dc, dc, w)


def _adam_math(g, w, m, v):
    m = B1 * m + (1.0 - B1) * g
    v = B2 * v + (1.0 - B2) * (g * g)
    m_hat = m / (1.0 - B1 ** STEP)
    v_hat = v / (1.0 - B2 ** STEP)
    return -LR * (m_hat / (jnp.sqrt(v_hat) + EPS) + WD * w), m, v


def _reduce_tail(chip32, far, chip, name, wmv=None):
    L = len(chip32)
    _, R, C = chip32[0].shape
    tr = _tile(R, 256, 16)
    nr = R // tr

    def body(chip_ref, *refs):
        own_refs, far_refs, rest = refs[:L], refs[L:2 * L], refs[2 * L:]
        outs = rest[3:] if wmv else rest
        for ll in range(L):
            @pl.when(pl.program_id(0) == ll)
            def _(ll=ll):
                g = own_refs[ll][...]
                for s in range(3):
                    g = g + far_refs[ll][s].astype(F32)
                outs[0][...] = g
                if wmv:
                    outs[1][...], outs[2][...], outs[3][...] = _adam_math(g, rest[0][...], rest[1][...], rest[2][...])

    def rows(ll):
        return lambda l, i: jnp.where(l == ll, i, jnp.where(l < ll, 0, nr - 1))

    blk = pl.BlockSpec((None, tr, C), lambda l, i, ch: (l, i, 0))
    in_specs = [pl.BlockSpec((None, tr, C), lambda l, i, ch, f=rows(ll): (ch[0], f(l, i), 0)) for ll in range(L)]
    in_specs += [pl.BlockSpec((3, tr, C), lambda l, i, ch, f=rows(ll): (0, f(l, i), 0)) for ll in range(L)]
    args = list(chip32) + list(far)
    n_out = 1
    if wmv:
        in_specs += [blk] * 3
        args += list(wmv)
        n_out = 4
    return pl.pallas_call(
        body, name=name,
        grid_spec=pltpu.PrefetchScalarGridSpec(num_scalar_prefetch=1, grid=(L, nr), in_specs=in_specs, out_specs=[blk] * n_out),
        out_shape=[jax.ShapeDtypeStruct((L, R, C), F32)] * n_out, compiler_params=_cparams(("arbitrary", "arbitrary")),
    )(chip, *args)


def _adamw(g, w, m, v, name):
    R, C = g.shape
    tr = _tile(R, 128, 8)

    def body(g_ref, w_ref, m_ref, v_ref, d_ref, nm_ref, nv_ref):
        d_ref[...], nm_ref[...], nv_ref[...] = _adam_math(g_ref[...], w_ref[...], m_ref[...], v_ref[...])

    blk = pl.BlockSpec((tr, C), lambda i: (i, 0))
    return pl.pallas_call(
        body, name=name, grid=(R // tr,), in_specs=[blk] * 4, out_specs=[blk] * 3,
        out_shape=[jax.ShapeDtypeStruct(g.shape, F32)] * 3, compiler_params=_cparams(("parallel",)),
    )(g, w, m, v)


def _pair_sum(x, recv, core, name):
    _, R, C = x.shape
    tr = _tile(R, 600, 16)

    def body(core_ref, x_ref, r_ref, o32_ref, o16_ref):
        s = x_ref[...] + r_ref[...]
        o32_ref[...] = s
        o16_ref[...] = s.astype(BF16)

    blk = pl.BlockSpec((None, tr, C), lambda q, i, c: (q, i, 0))
    mine = pl.BlockSpec((None, None, tr, C), lambda q, i, c: (q, c[0], i, 0))
    return pl.pallas_call(
        body, name=name,
        grid_spec=pltpu.PrefetchScalarGridSpec(num_scalar_prefetch=1, grid=(4, R // tr), in_specs=[mine, blk], out_specs=[blk, blk]),
        out_shape=[jax.ShapeDtypeStruct((4, R, C), F32), jax.ShapeDtypeStruct((4, R, C), BF16)],
        compiler_params=_cparams(("parallel", "parallel")),
    )(core, x.reshape(4, 2, R, C), recv)


def _sum_slots(x, name):
    def body(x_ref, o_ref):
        g = x_ref[0]
        for s in range(1, x.shape[0]):
            g = g + x_ref[s]
        o_ref[...] = g

    return pl.pallas_call(body, name=name, out_shape=jax.ShapeDtypeStruct(x.shape[1:], F32))(x)


MESH = pl.DeviceIdType.MESH
_HBM = pl.BlockSpec(memory_space=pltpu.HBM)


def _dma_sems(n):
    return pltpu.SemaphoreType.DMA((n,))


def _gather_many(xs, name):
    n = len(xs)

    def body(*refs):
        x_refs, out_refs = refs[:n], refs[n:2 * n]
        send_sems, recv_sems, local_sems = refs[2 * n:]
        ax, ay, ac = lax.axis_index("x"), lax.axis_index("y"), lax.axis_index("c")
        me, sibling = (ax, ay, ac), (ax, ay, 1 - ac)
        chips = [(1 - ax, ay), (ax, 1 - ay), (1 - ax, 1 - ay)]

        def copy(a, k, block, to, own=False):
            slot = out_refs[a].at[4 * block[0] + 2 * block[1] + block[2]]
            return pltpu.make_async_remote_copy(
                src_ref=x_refs[a] if own else slot, dst_ref=slot, send_sem=send_sems.at[7 * a + k],
                recv_sem=recv_sems.at[7 * a + k], device_id=to, device_id_type=MESH)

        mine = [pltpu.make_async_copy(x_refs[a], out_refs[a].at[4 * ax + 2 * ay + ac], local_sems.at[a]) for a in range(n)]
        first = [copy(a, 0, me, sibling, own=True) for a in range(n)]
        first += [copy(a, 1 + j, me, (*chip, ac), own=True) for j, chip in enumerate(chips) for a in range(n)]
        for cp in mine + first:
            cp.start()
        passed = []
        for j, chip in enumerate(chips):
            for a in range(n):
                copy(a, 1 + j, (*chip, ac), me).wait_recv()
                cp = copy(a, 4 + j, (*chip, ac), sibling)
                cp.start()
                passed.append(cp)
        for a in range(n):
            copy(a, 0, sibling, me).wait_recv()
            for j, chip in enumerate(chips):
                copy(a, 4 + j, (*chip, 1 - ac), me).wait_recv()
        for cp in first + passed:
            cp.wait_send()
        for cp in mine:
            cp.wait()

    return pl.pallas_call(
        body, name=name, out_shape=[jax.ShapeDtypeStruct((N_DEV,) + x.shape, x.dtype) for x in xs],
        in_specs=[_HBM] * n, out_specs=[_HBM] * n, scratch_shapes=[_dma_sems(7 * n), _dma_sems(7 * n), _dma_sems(n)],
    )(*xs)


_SEM = pl.BlockSpec(memory_space=pltpu.SEMAPHORE)
_EFFECT = pltpu.SideEffectType.DATAFLOW_SIDE_EFFECTING


def _peer(k):
    ax, ay, ac = lax.axis_index("x"), lax.axis_index("y"), lax.axis_index("c")
    px = 1 - ax if k & 4 else ax
    py = 1 - ay if k & 2 else ay
    pc = 1 - ac if k & 1 else ac
    return (px, py, pc), 4 * px + 2 * py + pc


def _build_gather(x_refs, land_refs, send_sems, recv_sems, waiting):
    _, me = _peer(0)
    copies = []
    for a in range(len(x_refs)):
        for k in range(1, N_DEV):
            peer, slot = _peer(k)
            copies.append(pltpu.make_async_remote_copy(
                src_ref=x_refs[a], dst_ref=land_refs[a].at[slot if waiting else me], send_sem=send_sems.at[7 * a + k - 1],
                recv_sem=recv_sems.at[7 * a + k - 1], device_id=peer, device_id_type=MESH))
    return copies


def _build_cores(x_refs, land_refs, send_sems, recv_sems, waiting):
    ax, ay, ac = lax.axis_index("x"), lax.axis_index("y"), lax.axis_index("c")
    copies = []
    for a in range(len(x_refs)):
        for q in range(4):
            copies.append(pltpu.make_async_remote_copy(
                src_ref=x_refs[a].at[2 * q + 1 - ac], dst_ref=land_refs[a].at[q], send_sem=send_sems.at[4 * a + q],
                recv_sem=recv_sems.at[4 * a + q], device_id=(ax, ay, 1 - ac), device_id_type=MESH))
    return copies


def _build_chips(p_refs, land_refs, send_sems, recv_sems, waiting):
    ax, ay, ac = lax.axis_index("x"), lax.axis_index("y"), lax.axis_index("c")
    copies = []
    for a in range(len(p_refs)):
        for k in range(1, 4):
            px = 1 - ax if k & 2 else ax
            py = 1 - ay if k & 1 else ay
            copies.append(pltpu.make_async_remote_copy(
                src_ref=p_refs[a].at[2 * px + py], dst_ref=land_refs[a].at[k - 1], send_sem=send_sems.at[3 * a + k - 1],
                recv_sem=recv_sems.at[3 * a + k - 1], device_id=(px, py, ac), device_id_type=MESH))
    return copies


_EXCHANGES = {"gather": (_build_gather, 7, N_DEV), "cores": (_build_cores, 4, 4), "chips": (_build_chips, 3, 3)}


def _exchange_start(kind, xs, lands, name, after=None):
    build, per, _ = _EXCHANGES[kind]
    n = len(xs)

    def body(*refs):
        for cp in build(refs[:n], refs[n:2 * n], refs[-2 * n - 3], refs[-2 * n - 2], False):
            cp.start()
        refs[-1][...] = jnp.zeros_like(refs[-1])

    hbm = lambda t: pltpu.HBM(t.shape, t.dtype)
    args = [pltpu.with_memory_space_constraint(t, pltpu.HBM) for t in list(xs) + list(lands)]
    in_specs = [_HBM] * (2 * n)
    if after is not None:
        args.append(after)
        in_specs.append(pl.BlockSpec(memory_space=pl.ANY))
    outs = pl.pallas_call(
        body, name=name,
        out_shape=(_dma_sems(per * n), _dma_sems(per * n), *[hbm(t) for t in xs], *[hbm(t) for t in lands],
                   jax.ShapeDtypeStruct((8, 128), F32)),
        in_specs=in_specs, out_specs=(_SEM, _SEM, *[_HBM] * (2 * n), pl.BlockSpec(memory_space=pltpu.VMEM)),
        input_output_aliases={a: 2 + a for a in range(2 * n)},
        compiler_params=pltpu.CompilerParams(has_side_effects=_EFFECT),
    )(*args)
    return (kind, outs[0], outs[1], outs[2:2 + n], outs[2 + n:2 + 2 * n]), outs[-1]


def _exchange_wait(flight, after, name):
    kind, send_sems, recv_sems, xs, lands = flight
    build = _EXCHANGES[kind][0]
    n = len(xs)

    def body(*refs):
        for cp in build(refs[:n], refs[n:2 * n], refs[2 * n], refs[2 * n + 1], True):
            cp.wait_send()
            cp.wait_recv()

    hbm = lambda t: pltpu.HBM(t.shape, t.dtype)
    outs = pl.pallas_call(
        body, name=name, out_shape=(*[hbm(t) for t in xs], *[hbm(t) for t in lands]),
        in_specs=[_HBM] * (2 * n) + [_SEM, _SEM, pl.BlockSpec(memory_space=pl.ANY)], out_specs=[_HBM] * (2 * n),
        input_output_aliases={a: a for a in range(2 * n)}, compiler_params=pltpu.CompilerParams(has_side_effects=_EFFECT),
    )(*xs, *lands, send_sems, recv_sems, after)
    return outs[:n], outs[n:]


def _x_view(xb, d):
    return xb if d == 1 else xb.reshape(xb.shape[0] // d, d * xb.shape[1])


def _layer_fwd(x, xb, p, w, cos, sin, rconsts, late=None):
    proj = _mm(xb, w["win"], tb=True, b_rows=(N_ATT, N_REST), name="mm_proj", out_dtype=BF16)
    qkvs, ogs, lgs = [], [], []
    for g, dil in enumerate(DILATIONS):
        qkv = _qkv_fwd(_x_view(xb, dil), w["win"], g, dil, f"mm_qkv{g}")
        o, l = _attn_fwd(qkv, dil, f"attn_fwd_g{g}")
        qkvs.append(qkv)
        ogs.append(_to_tokens(o, dil))
        lgs.append(_to_tokens(l, dil))
    attn, lse = _rowwise(_f_combine, ogs + lgs, [], [(D, BF16), (HD, F32)], [], name="attn_combine")
    ret_raw, states = _ret_fwd(proj, cos, sin, rconsts)
    rg_win = (proj, RH * RDV, OFF_RG // (RH * RDV))
    ga_win, gr_win = (proj, D, OFF_GA // D), (proj, D, OFF_GR // D)
    (r,) = _rowwise(_f_gn, [ret_raw, rg_win], [w["ret_gn_g"], w["ret_gn_b"]], [(RH * RDV, BF16)], [], name="gn_fwd", tm=256)
    if late is not None:
        w = {**w, **late(r)}
    ap = _mm(attn, w["w_attn_proj"], name="mm_attn_proj", out_dtype=BF16)
    rp = _mm(r, w["w_ret_proj"], name="mm_ret_proj", out_dtype=BF16, tk=2048)
    (merged,) = _rowwise(_f_gate, [ap, rp, ga_win, gr_win], [], [(D, BF16)], [], name="gate_fwd")
    mix = _mm(merged, w["w_out"], name="mm_out")
    h1, x1, x1b = _rowwise(_f_ln1, [x, mix], [w["ln1_g"], w["ln1_b"]], [(D, F32), (D, F32), (D, BF16)], [], name="ln1_fwd")
    z = _mm(x1b, w["w_ple_gate"], name="mm_ple_gate")
    pp = _mm(p, w["w_ple_proj"], tb=True, name="mm_ple_proj")
    hg = _mm(x1b, w["w_up"], tb=True, b_rows=(0, DFF), name="mm_up_g", out_dtype=BF16, tm=512, tn=DFF)
    hu = _mm(x1b, w["w_up"], tb=True, b_rows=(DFF, DFF), name="mm_up_u", out_dtype=BF16, tm=512, tn=DFF)
    act = _conv_fwd(hg, hu, w["conv_wg"], w["conv_wu"], w["conv_bg"], w["conv_bu"])
    ffn = _mm(act, w["w_down"], name="mm_down", tm=512, tk=DFF)
    h2, x2, x2b = _rowwise(_f_ln2, [x1, ffn, z, pp], [w["ln2_g"], w["ln2_b"]], [(D, F32), (D, F32), (D, BF16)], [], name="ln2_fwd")
    saved = dict(xb=xb, proj=proj, qkvs=qkvs, attn=attn, lse=lse, ret_raw=ret_raw, states=states, r=r, ap=ap, rp=rp,
                 merged=merged, h1=h1, x1b=x1b, z=z, pp=pp, hg=hg, hu=hu, act=act, h2=h2, p=p)
    return x2, x2b, saved, w


def _after(fn, token):
    return fn if token is None else (lambda *a: fn(*a[:-1]))


def _layer_bwd(dys, w, sv, cos, sin, rconsts, hooks):
    gr = {}
    proj = sv["proj"]
    call = lambda key, *a: hooks[key](*a) if key in hooks else None
    held = lambda token: [] if token is None else [token]
    token = hooks.get("token")
    dh2, dh2b, gr["ln2_g"], gr["ln2_b"] = _rowwise(_after(_f_ln_bwd, token), list(dys) + [sv["h2"]], [w["ln2_g"]] + held(token),
                                                   [(D, F32), (D, BF16)], [(1, D), (1, D)], name="ln2_bwd")
    d_act = _mm(dh2b, w["w_down"], tb=True, name="mm_down_dx", out_dtype=BF16, tm=512, tn=DFF)
    gr["w_down"] = _mm(sv["act"], dh2b, ta=True, name="mm_down_dw", tm=DFF // 2)
    dcg, dcu, gwg, gwu, gbg, gbu = _conv_bwd_pre(d_act, sv["hg"], sv["hu"], w["conv_wg"], w["conv_wu"], w["conv_bg"], w["conv_bu"])
    token = call("after_ffn", dcg)
    gr["conv_w"] = jnp.concatenate([gwg, gwu], axis=1)
    gr["conv_b"] = jnp.concatenate([gbg, gbu], axis=1)
    dhg = _conv_bwd_in(dcg, w["conv_wg"], "conv_bwd_in_g")
    dhu = _conv_bwd_in(dcu, w["conv_wu"], "conv_bwd_in_u")
    gw_up = _mm(dhg, sv["x1b"], ta=True, name="mm_up_g_dw", tm=DFF // 2, out_rows=(0, 2 * DFF))
    gr["w_up"] = _mm(dhu, sv["x1b"], ta=True, name="mm_up_u_dw", tm=DFF // 2, out_rows=(DFF, 2 * DFF), into=gw_up)
    dx1 = _mm(dhg, w["w_up"], b_rows=(0, DFF), name="mm_up_g_dx", add=dh2, add_scale=ALPHA, tm=512, tk=DFF)
    dx1 = _mm(dhu, w["w_up"], b_rows=(DFF, DFF), name="mm_up_u_dx", add=dx1, tm=512, tk=DFF)
    dpp, dz = _rowwise(_f_ple_bwd, [dh2, sv["z"], sv["pp"]], [], [(D, BF16), (D, BF16)], [], name="ple_bwd")
    gr["w_ple_proj"] = _mm(dpp, sv["p"], ta=True, name="mm_ple_proj_dw")
    gr["w_ple_gate"] = _mm(sv["x1b"], dz, ta=True, name="mm_ple_gate_dw")
    dx1 = _mm(dz, w["w_ple_gate"], tb=True, name="mm_ple_gate_dx", add=dx1)
    dh1, dh1b, gr["ln1_g"], gr["ln1_b"] = _rowwise(_after(_f_ln_bwd, token), [dx1, sv["h1"]], [w["ln1_g"]] + held(token),
                                                   [(D, F32), (D, BF16)], [(1, D), (1, D)], name="ln1_bwd")
    d_merged = _mm(dh1b, w["w_out"], tb=True, name="mm_out_dx", out_dtype=BF16)
    gr["w_out"] = _mm(sv["merged"], dh1b, ta=True, name="mm_out_dw")
    rg_win = (proj, RH * RDV, OFF_RG // (RH * RDV))
    ga_win, gr_win = (proj, D, OFF_GA // D), (proj, D, OFF_GR // D)
    dap, drp, d_rest = _rowwise(_f_gate_bwd, [d_merged, sv["ap"], sv["rp"], ga_win, gr_win], [],
                                [(D, BF16), (D, BF16), (2 * D, BF16, N_REST, OFF_GA // (2 * D), None)], [], name="gate_bwd")
    d_attn = _mm(dap, w["w_attn_proj"], tb=True, name="mm_attn_proj_dx", out_dtype=BF16)
    gr["w_attn_proj"] = _mm(sv["attn"], dap, ta=True, name="mm_attn_proj_dw")
    d_r = _mm(drp, w["w_ret_proj"], tb=True, name="mm_ret_proj_dx", out_dtype=BF16, tn=2048)
    gr["w_ret_proj"] = _mm(sv["r"], drp, ta=True, name="mm_ret_proj_dw", tm=2048)
    token = call("early_grads", gr)
    d_ret, d_rest, gr["ret_gn_g"], gr["ret_gn_b"] = _rowwise(
        _after(_f_gn_bwd, token), [d_r, sv["ret_raw"], rg_win], [w["ret_gn_g"], w["ret_gn_b"]] + held(token),
        [(RH * RDV, BF16), (RH * RDV, BF16, N_REST, OFF_RG // (RH * RDV), d_rest)],
        [(1, RH * RDV), (1, RH * RDV)], name="gn_bwd", tm=256)
    d_rest = _ret_bwd(proj, cos, sin, rconsts, sv["states"], d_ret, d_rest)
    token = call("after_ret", d_rest)
    (delta,) = _rowwise(_after(_f_delta, token), [d_attn, sv["attn"]], held(token), [(HD, F32)], [], name="attn_delta")
    gw_in, dqkvs = None, []
    for g, dil in enumerate(DILATIONS):
        dqkvs.append(_attn_bwd(sv["qkvs"][g], _to_head_residues(d_attn, dil), _to_residues(sv["lse"], dil),
                               _to_residues(delta, dil), dil, f"attn_bwd_g{g}"))
        gw_in = _qkv_dw(dqkvs[g], _x_view(sv["xb"], dil), g, dil, f"mm_qkv{g}_dw", into=gw_in)
    gw_in = _mm(d_rest, sv["xb"], ta=True, name="mm_proj_dw", out_rows=(N_ATT, N_IN), into=gw_in, blocks8=True)
    gr["w_in"] = gw_in.reshape(N_IN, D)
    token = call("w_in_ready", gr["w_in"])
    dx0 = _mm(d_rest, w["win"], b_rows=(N_ATT, N_REST), name="mm_proj_dx", add=dh1, add_scale=ALPHA, after=token)
    dx_parts = []
    for g, dil in enumerate(DILATIONS):
        if dil == 1:
            dx0 = _qkv_dx(dqkvs[g], w["win"], g, dil, f"mm_qkv{g}_dx", F32, add=dx0)
            token = call("after_dx0", dx0)
        else:
            dx_parts.append(_qkv_dx(dqkvs[g], w["win"], g, dil, f"mm_qkv{g}_dx", BF16, after=token).reshape(dx0.shape))
    return [dx0] + dx_parts, gr


def _local_step(x, p, positions, target, ws, own_hooks=None, on_grads=None):
    half = RDK // 2
    freq = jnp.power(ROPE_BASE, -jnp.arange(half, dtype=F32) / half)
    ang = positions.astype(F32)[:, None] * freq[None, :]
    cos, sin = jnp.cos(ang), jnp.sin(ang)
    rconsts = _ret_consts()
    xb = x.astype(BF16)
    saved, ws = [], list(ws)
    for l in range(DEPTH):
        first, late = ws[l] if isinstance(ws[l], tuple) else (ws[l], None)
        if callable(first):
            first = first(x)
        x, xb, sv, ws[l] = _layer_fwd(x, xb, p[l], first, cos, sin, rconsts, late)
        saved.append(sv)
    dy, loss_vec = _rowwise(_f_loss, [x, target], [], [(D, F32)], [(1, D)], name="loss")
    dys, grads = [dy], [None] * DEPTH
    from_above = {}
    for l in reversed(range(DEPTH)):
        hooks = {**from_above, **(own_hooks(l) if own_hooks else {})}
        dys, grads[l] = _layer_bwd(dys, ws[l], saved[l], cos, sin, rconsts, hooks)
        from_above = on_grads(l, grads[l]) if on_grads else {}
    (grad_x,) = _rowwise(_f_sum, dys, [], [(D, F32)], [], name="grad_x_sum")
    return loss_vec, grad_x, grads


def _pack_rows(arrs):
    parts, where, off = [], [], 0
    for t in arrs:
        t = t.reshape(-1, D)
        rows = t.shape[0]
        padded = -(-rows // 8) * 8
        parts.append(jnp.pad(t, ((0, padded - rows), (0, 0))))
        where.append((off, rows))
        off += padded
    return jnp.concatenate(parts, axis=0), where


FIRST = ("w_in",)
LATER = tuple(n for n in BIG if n not in FIRST)


def _first_weights(g, l, W):
    w = dict(win=g["w_in"].reshape(N_IN, D))
    for n in ("ret_gn_g", "ret_gn_b", "ln1_g", "ln1_b", "ln2_g", "ln2_b"):
        w[n] = W[n][l][None, :]
    return w


def _later_weights(g, l, conv_w_all, conv_b):
    w = dict(w_up=g["w_up"].reshape(2 * DFF, D), w_ple_proj=g["w_ple_proj"].reshape(D, PLE),
             w_attn_proj=g["w_attn_proj"].reshape(D, D), w_ret_proj=g["w_ret_proj"].reshape(RH * RDV, D),
             w_out=g["w_out"].reshape(D, D), w_down=g["w_down"].reshape(DFF, D), w_ple_gate=g["w_ple_gate"].reshape(D, D))
    w["conv_wg"], w["conv_wu"] = conv_w_all[l][:, :DFF], conv_w_all[l][:, DFF:]
    w["conv_bg"], w["conv_bu"] = conv_b[l][None, :DFF], conv_b[l][None, DFF:]
    return w


def _layer_weights(g, l, conv_w_all, conv_b, W):
    return {**_first_weights(g, l, W), **_later_weights(g, l, conv_w_all, conv_b)}


def kernel(x, p, positions, w_in, w_attn_proj, w_ret_proj, ret_gn_g, ret_gn_b, w_out, ln1_g, ln1_b, w_up, conv_w, conv_b, w_down, w_ple_gate, w_ple_proj, ln2_g, ln2_b, loss_target, m_w_in, m_w_attn_proj, m_w_ret_proj, m_ret_gn_g, m_ret_gn_b, m_w_out, m_ln1_g, m_ln1_b, m_w_up, m_conv_w, m_conv_b, m_w_down, m_w_ple_gate, m_w_ple_proj, m_ln2_g, m_ln2_b, v_w_in, v_w_attn_proj, v_w_ret_proj, v_ret_gn_g, v_ret_gn_b, v_w_out, v_ln1_g, v_ln1_b, v_w_up, v_conv_w, v_conv_b, v_w_down, v_w_ple_gate, v_w_ple_proj, v_ln2_g, v_ln2_b):
    W = dict(w_in=w_in, w_attn_proj=w_attn_proj, w_ret_proj=w_ret_proj, ret_gn_g=ret_gn_g, ret_gn_b=ret_gn_b, w_out=w_out,
             ln1_g=ln1_g, ln1_b=ln1_b, w_up=w_up, conv_w=conv_w, conv_b=conv_b, w_down=w_down, w_ple_gate=w_ple_gate,
             w_ple_proj=w_ple_proj, ln2_g=ln2_g, ln2_b=ln2_b)
    M = dict(w_in=m_w_in, w_attn_proj=m_w_attn_proj, w_ret_proj=m_w_ret_proj, ret_gn_g=m_ret_gn_g, ret_gn_b=m_ret_gn_b,
             w_out=m_w_out, ln1_g=m_ln1_g, ln1_b=m_ln1_b, w_up=m_w_up, conv_w=m_conv_w, conv_b=m_conv_b, w_down=m_w_down,
             w_ple_gate=m_w_ple_gate, w_ple_proj=m_w_ple_proj, ln2_g=m_ln2_g, ln2_b=m_ln2_b)
    V = dict(w_in=v_w_in, w_attn_proj=v_w_attn_proj, w_ret_proj=v_w_ret_proj, ret_gn_g=v_ret_gn_g, ret_gn_b=v_ret_gn_b,
             w_out=v_w_out, ln1_g=v_ln1_g, ln1_b=v_ln1_b, w_up=v_w_up, conv_w=v_conv_w, conv_b=v_conv_b, w_down=v_w_down,
             w_ple_gate=v_w_ple_gate, w_ple_proj=v_w_ple_proj, ln2_g=v_ln2_g, ln2_b=v_ln2_b)

    me = 4 * lax.axis_index("x") + 2 * lax.axis_index("y") + lax.axis_index("c")
    shard = lambda n, l: (W[n][l].T if n in COL_SHARDED else W[n][l]).astype(BF16)
    landing = lambda ts: [lax.dynamic_update_index_in_dim(lax.empty((N_DEV,) + t.shape, t.dtype), t, me, 0) for t in ts]
    first0 = _gather_many([shard(n, 0) for n in FIRST], "gather_first_l0")
    later0 = [shard(n, 0) for n in LATER] + [conv_w]
    flight0, token0 = _exchange_start("gather", later0, landing(later0), "gather_later_l0_start", after=first0[0])
    all1 = [shard(n, 1) for n in BIG]
    flight1, token1 = _exchange_start("gather", all1, landing(all1), "gather_weights_l1_start", after=token0)
    conv_w_all = []

    def later_first_layer(after):
        _, got = _exchange_wait(flight0, after, "gather_later_l0_wait")
        conv_w_all.append(got[-1].transpose(1, 2, 0, 3).reshape(DEPTH, 3, 2 * DFF))
        return _later_weights(dict(zip(LATER, got)), 0, conv_w_all[0], conv_b)

    def second_layer(after):
        _, got = _exchange_wait(flight1, after, "gather_weights_l1_wait")
        return _layer_weights(dict(zip(BIG, got)), 1, conv_w_all[0], conv_b, W)

    core = lax.axis_index("c").astype(jnp.int32).reshape(1)
    chip = (2 * lax.axis_index("x") + lax.axis_index("y")).astype(jnp.int32).reshape(1)
    empty_like = lambda ts, slots: [lax.empty((slots,) + t.shape[1:], t.dtype) for t in ts]
    chip32, far = [{} for _ in range(DEPTH)], [{} for _ in range(DEPTH)]
    pending = []

    def reduction(l, names, tag):
        state = {}

        def start(g):
            mine = [g[n].reshape((N_DEV, -1) + g[n].shape[1:]) for n in names]
            state["cores"], token = _exchange_start("cores", mine, empty_like(mine, 4), f"exchange_cores_{tag}_start")
            return token

        def onward(after):
            mine, theirs = _exchange_wait(state["cores"], after, f"exchange_cores_{tag}_wait")
            sums = [_pair_sum(a, b, core, f"pair_sum_l{l}_{n}") for a, b, n in zip(mine, theirs, names)]
            for n, s in zip(names, sums):
                chip32[l][n] = s[0]
            sent = [s[0 if n in F32_OVER_ICI else 1] for s, n in zip(sums, names)]
            flight, token = _exchange_start("chips", sent, empty_like(sent, 3), f"exchange_chips_{tag}_start")
            pending.append((l, names, flight, tag))
            return token

        return start, onward

    def on_grads(l, g):
        if l == 0:
            return {}
        start, onward = reduction(l, BIG, f"l{l}")
        return dict(token=start(g), after_ffn=onward)

    def own_hooks(l):
        if l != 0:
            return {}
        start_e, onward_e = reduction(0, LATER, "l0_later")
        start_w, onward_w = reduction(0, FIRST, "l0_first")
        return dict(early_grads=start_e, after_ret=onward_e, w_in_ready=lambda gw: start_w({"w_in": gw}), after_dx0=onward_w)

    ws = [(_first_weights(dict(zip(FIRST, first0)), 0, W), later_first_layer), second_layer]
    loss_vec, grad_x, grads = _local_step(x[0] + token1[0, 0], p[:, 0], positions[0], loss_target[0], ws, own_hooks, on_grads)
    loss = lax.psum(jnp.sum(loss_vec), ("x", "y", "c"))
    for l, names, flight, tag in pending:
        _, got = _exchange_wait(flight, grad_x, f"exchange_chips_{tag}_wait")
        far[l].update(zip(names, got))
    G, DW, NM, NV = ({} for _ in range(4))
    for n in BIG:
        chip32_n = [chip32[l][n] for l in range(DEPTH)]
        far_n = [far[l][n] for l in range(DEPTH)]
        if n in COL_SHARDED:
            G[n] = _reduce_tail(chip32_n, far_n, chip, f"reduced_{n}")[0].transpose(0, 2, 1)
            R2, C2 = DEPTH * W[n].shape[1], W[n].shape[2]
            res = _adamw(*(t.reshape(R2, C2) for t in (G[n], W[n], M[n], V[n])), f"adamw_{n}")
            DW[n], NM[n], NV[n] = (t.reshape(W[n].shape) for t in res)
        else:
            G[n], DW[n], NM[n], NV[n] = _reduce_tail(chip32_n, far_n, chip, f"adamw_{n}", wmv=(W[n], M[n], V[n]))

    small_names = SMALL + ("conv_w",)
    g_small, where = _pack_rows([jnp.stack([grads[l][n] for l in range(DEPTH)]) for n in small_names])
    (g_all,) = _gather_many([g_small], "gather_small_grads")
    g_small = _sum_slots(g_all, "sum_small_grads")
    for n, (off, rows) in zip(SMALL, where):
        G[n] = g_small[off:off + rows].reshape(W[n].shape)
    off, rows = where[-1]
    g_cw = g_small[off:off + rows].reshape(DEPTH, 3, N_DEV, conv_w.shape[2])
    G["conv_w"] = lax.dynamic_index_in_dim(g_cw, me, axis=2, keepdims=False)
    packed = [_pack_rows([d[n] for n in SMALL]) for d in (G, W, M, V)]
    small_out = _adamw(*(t for t, _ in packed), "adamw_small")
    for res, dst in zip(small_out, (DW, NM, NV)):
        for n, (off, rows) in zip(SMALL, packed[0][1]):
            dst[n] = res[off:off + rows].reshape(W[n].shape)
    two_d = lambda t: t.reshape(DEPTH * 3, conv_w.shape[2])
    cw_out = _adamw(two_d(G["conv_w"]), two_d(conv_w), two_d(m_conv_w), two_d(v_conv_w), "adamw_conv_w")
    for res, dst in zip(cw_out, (DW, NM, NV)):
        dst["conv_w"] = res.reshape(conv_w.shape)

    return (loss, grad_x[None], *[G[n] for n in WEIGHTS], *[DW[n] for n in WEIGHTS], *[NM[n] for n in WEIGHTS],
            *[NV[n] for n in WEIGHTS])
```

```python
import math

import numpy as np
import jax
import jax.numpy as jnp
from jax import lax
from jax.experimental import pallas as pl
from jax.experimental.pallas import tpu as pltpu

F32, BF16 = jnp.float32, jnp.bfloat16

D = 1024
DEPTH = 2
N_DEV = 8
HD = 128
NH = 8
DILATIONS = (1, 4, 16)
SPAN = 128
N_ATT = 3 * 3 * NH * HD
RH, RDK, RDV = 4, 256, 512
CH = 128
DFF = 2816
PLE = 256
N_IN = 17408
N_REST = N_IN - N_ATT
OFF_RQ, OFF_RK, OFF_RV, OFF_RG, OFF_GA, OFF_GR = 0, 1024, 2048, 4096, 6144, 7168
ALPHA = (2 * DEPTH) ** 0.25
LN_EPS, GN_EPS = 1e-5, 1e-6
ROPE_BASE = 10000.0
LR, B1, B2, EPS, WD, STEP = 0.001, 0.9, 0.999, 1e-8, 0.01, 10
VMEM_LIMIT = 48 * 1024 * 1024
NEG = -1e30

BIG = ("w_in", "w_attn_proj", "w_ret_proj", "w_out", "w_up", "w_down", "w_ple_gate", "w_ple_proj")
COL_SHARDED = ("w_in", "w_up", "w_ple_proj")
F32_OVER_ICI = ("w_attn_proj", "w_out", "w_ple_gate", "w_ple_proj")
SMALL = ("ret_gn_g", "ret_gn_b", "ln1_g", "ln1_b", "conv_b", "ln2_g", "ln2_b")
WEIGHTS = ("w_in", "w_attn_proj", "w_ret_proj", "ret_gn_g", "ret_gn_b", "w_out", "ln1_g", "ln1_b", "w_up",
           "conv_w", "conv_b", "w_down", "w_ple_gate", "w_ple_proj", "ln2_g", "ln2_b")


def _tile(n, cap, mult=128):
    if n <= cap:
        return n
    t = (cap // mult) * mult
    while n % t:
        t -= mult
    return t


def _cparams(sem):
    return pltpu.CompilerParams(dimension_semantics=sem, vmem_limit_bytes=VMEM_LIMIT)


def _dot(a, b, ca, cb):
    return lax.dot_general(a, b, (((ca,), (cb,)), ((), ())), preferred_element_type=F32)


def _bdot(a, b, ca, cb):
    return lax.dot_general(a, b, (((ca,), (cb,)), ((0,), (0,))), preferred_element_type=F32)


def _mm(a, b, *, name, ta=False, tb=False, out_dtype=F32, add=None, add_scale=1.0, tm=1024, tn=1024, tk=1024,
        b_rows=None, out_rows=None, into=None, blocks8=False, after=None):
    M, K = (a.shape[1], a.shape[0]) if ta else a.shape
    b_first, b_count = b_rows if b_rows else (0, b.shape[0])
    N = b_count if tb else b.shape[1]
    assert K == (b.shape[1] if tb else b_count)
    tm, tn, tk = _tile(M, tm), _tile(N, tn), _tile(K, tk)
    nk = K // tk
    o_first, o_total = out_rows if out_rows else (0, M)
    jb, kb, io = (b_first // tn, 0, o_first // tm) if tb else (0, b_first // tk, o_first // tm)
    assert b_first % (tn if tb else tk) == 0 and o_first % tm == 0 and (add is None or out_rows is None)

    def body(*refs):
        if add is None:
            a_ref, b_ref = refs[:2]
        else:
            a_ref, b_ref, add_ref = refs[:3]
        o_ref, acc_ref = refs[-2:]
        k = pl.program_id(2)

        @pl.when(k == 0)
        def _():
            acc_ref[...] = jnp.zeros_like(acc_ref)

        acc_ref[...] += _dot(a_ref[...].astype(BF16), b_ref[...].astype(BF16), 0 if ta else 1, 1 if tb else 0)

        @pl.when(k == nk - 1)
        def _():
            r = acc_ref[...]
            if add is not None:
                r = r + add_scale * add_ref[...].astype(F32)
            o_ref[...] = r.astype(out_dtype).reshape(o_ref.shape)

    a_spec = pl.BlockSpec((tk, tm), lambda i, j, k: (k, i)) if ta else pl.BlockSpec((tm, tk), lambda i, j, k: (i, k))
    if tb:
        b_spec = pl.BlockSpec((tn, tk), lambda i, j, k: (j + jb, k))
    else:
        b_spec = pl.BlockSpec((tk, tn), lambda i, j, k: (k + kb, j))
    if blocks8:
        assert tm == 1024
        o_spec = pl.BlockSpec((1, 8, 128, tn), lambda i, j, k: (i + io, 0, 0, j))
        o_shape = (o_total // tm, 8, 128, N)
    else:
        o_spec = pl.BlockSpec((tm, tn), lambda i, j, k: (i + io, j))
        o_shape = (o_total, N)
    in_specs, args, aliases = [a_spec, b_spec], [a, b], {}
    if add is not None:
        in_specs.append(o_spec)
        args.append(add)
    if after is not None:
        in_specs.append(pl.BlockSpec(memory_space=pl.ANY))
        args.append(after)
    if into is not None:
        aliases = {len(args): 0}
        in_specs.append(pl.BlockSpec(memory_space=pl.ANY))
        args.append(into)
    return pl.pallas_call(
        body, name=name, grid=(M // tm, N // tn, nk), in_specs=in_specs, out_specs=o_spec,
        out_shape=jax.ShapeDtypeStruct(o_shape, out_dtype), scratch_shapes=[pltpu.VMEM((tm, tn), F32)],
        input_output_aliases=aliases, compiler_params=_cparams(("parallel", "parallel", "arbitrary")),
    )(*args)


def _rowwise(fn, rows, pars, outs, accs, *, name, tm=512):
    first = rows[0][0] if isinstance(rows[0], tuple) else rows[0]
    S = first.shape[-2]
    tm = _tile(S, tm, 16)
    n_r, n_p, n_o = len(rows), len(pars), len(outs)
    outs = [o if len(o) == 5 else (o[0], o[1], o[0], 0, None) for o in outs]
    intos = [(k, o[4]) for k, o in enumerate(outs) if o[4] is not None]
    n_i = len(intos)

    def body(*refs):
        i = pl.program_id(0)
        vals = [r[...] for r in refs[:n_r + n_p]]
        res = fn(*vals)
        if not isinstance(res, (tuple, list)):
            res = (res,)
        o_refs = refs[n_r + n_p + n_i:n_r + n_p + n_i + n_o]
        a_refs = refs[n_r + n_p + n_i + n_o:]
        for r, v in zip(o_refs, res[:n_o]):
            r[...] = v.astype(r.dtype)
        if a_refs:
            @pl.when(i == 0)
            def _():
                for r in a_refs:
                    r[...] = jnp.zeros_like(r)

            for r, v in zip(a_refs, res[n_o:]):
                r[...] += v

    in_specs, args = [], []
    for r in rows:
        if isinstance(r, tuple):
            arr, w, cb = r
            in_specs.append(pl.BlockSpec((tm, w), lambda i, cb=cb: (i, cb)))
        elif r.ndim == 3:
            arr = r
            in_specs.append(pl.BlockSpec((arr.shape[0], tm, arr.shape[2]), lambda i: (0, i, 0)))
        else:
            arr = r
            in_specs.append(pl.BlockSpec((tm, arr.shape[1]), lambda i: (i, 0)))
        args.append(arr)
    for p_ in pars:
        in_specs.append(pl.BlockSpec(p_.shape, lambda i: (0, 0)))
        args.append(p_)
    aliases = {}
    for k, arr in intos:
        aliases[len(args)] = k
        in_specs.append(pl.BlockSpec(memory_space=pl.ANY))
        args.append(arr)
    out_shape = [jax.ShapeDtypeStruct((S, o[2]), o[1]) for o in outs] + [jax.ShapeDtypeStruct(a, F32) for a in accs]
    out_specs = [pl.BlockSpec((tm, o[0]), lambda i, cb=o[3]: (i, cb)) for o in outs] + [pl.BlockSpec(a, lambda i: (0, 0)) for a in accs]
    return pl.pallas_call(
        body, name=name, grid=(S // tm,), in_specs=in_specs, out_specs=out_specs, out_shape=out_shape,
        input_output_aliases=aliases, compiler_params=_cparams(("arbitrary",) if accs else ("parallel",)),
    )(*args)


def _norm(h, eps):
    mu = jnp.mean(h, -1, keepdims=True)
    d = h - mu
    rstd = lax.rsqrt(jnp.mean(d * d, -1, keepdims=True) + eps)
    return d * rstd, rstd


def _norm_bwd(dxh, xh, rstd):
    return rstd * (dxh - jnp.mean(dxh, -1, keepdims=True) - xh * jnp.mean(dxh * xh, -1, keepdims=True))


def _sig(x):
    return 1.0 / (1.0 + jnp.exp(-x))


_GELU_C = math.sqrt(2.0 / math.pi)


def _gelu(x):
    t = jnp.tanh(_GELU_C * (x + 0.044715 * x * x * x))
    return 0.5 * x * (1.0 + t), t


def _gelu_grad(x, t):
    return 0.5 * (1.0 + t) + 0.5 * x * (1.0 - t * t) * _GELU_C * (1.0 + 3 * 0.044715 * x * x)


def _f_ln1(x, mix, g, b):
    h = ALPHA * x + mix
    xh, _ = _norm(h, LN_EPS)
    y = xh * g + b
    return h, y, y


def _f_ln2(x, ffn, z, pp, g, b):
    h = ALPHA * x + ffn + _sig(z) * pp
    xh, _ = _norm(h, LN_EPS)
    y = xh * g + b
    return h, y, y


def _f_ln_bwd(*args):
    *dys, h, g = args
    dy = dys[0]
    for t in dys[1:]:
        dy = dy + t
    xh, rstd = _norm(h, LN_EPS)
    dh = _norm_bwd(dy * g, xh, rstd)
    return dh, dh, jnp.sum(dy * xh, 0, keepdims=True), jnp.sum(dy, 0, keepdims=True)


def _f_sum(*ts):
    r = ts[0]
    for t in ts[1:]:
        r = r + t
    return r


def _f_loss(y, t):
    e = y - t
    return e * (1.0 / D), jnp.sum(e * e, 0, keepdims=True) * (0.5 / D)


def _head_col(c, h):
    lane = lax.broadcasted_iota(jnp.int32, c.shape, 1)
    return jnp.sum(jnp.where(lane == h, c, 0.0), -1, keepdims=True)


def _f_combine(o0, o1, o2, l0, l1, l2):
    lane = lax.broadcasted_iota(jnp.int32, l0.shape, 1)
    parts, lse = [], jnp.zeros(l0.shape, F32)
    for h in range(NH):
        a0, a1, a2 = _head_col(l0, h), _head_col(l1, h), _head_col(l2, h)
        m = jnp.maximum(jnp.maximum(a0, a1), a2)
        e0, e1, e2 = jnp.exp(a0 - m), jnp.exp(a1 - m), jnp.exp(a2 - m)
        den = e0 + e1 + e2
        parts.append((e0 * o0[h].astype(F32) + e1 * o1[h].astype(F32) + e2 * o2[h].astype(F32)) / den)
        lse = jnp.where(lane == h, m + jnp.log(den), lse)
    return jnp.concatenate(parts, axis=1), lse


def _f_delta(da, a):
    lane = lax.broadcasted_iota(jnp.int32, (da.shape[0], HD), 1)
    out = jnp.zeros((da.shape[0], HD), F32)
    for h in range(NH):
        sl = slice(h * HD, (h + 1) * HD)
        s = jnp.sum(da[:, sl].astype(F32) * a[:, sl].astype(F32), -1, keepdims=True)
        out = jnp.where(lane == h, s, out)
    return out


def _f_gate(ap, rp, ga, gr):
    return _sig(ga.astype(F32)) * ap.astype(F32) + _sig(gr.astype(F32)) * rp.astype(F32)


def _f_gate_bwd(dm, ap, rp, ga, gr):
    dm = dm.astype(F32)
    sa, sr = _sig(ga.astype(F32)), _sig(gr.astype(F32))
    dga, dgr = dm * ap.astype(F32) * sa * (1.0 - sa), dm * rp.astype(F32) * sr * (1.0 - sr)
    return dm * sa, dm * sr, jnp.concatenate([dga, dgr], axis=1)


def _f_gn(y, rg, g, b):
    y, rg = y.astype(F32), rg.astype(F32)
    parts = []
    for h in range(RH):
        sl = slice(h * RDV, (h + 1) * RDV)
        xh, _ = _norm(y[:, sl], GN_EPS)
        parts.append(xh * g[:, sl] + b[:, sl])
    return rg * _sig(rg) * jnp.concatenate(parts, axis=1)


def _f_gn_bwd(dr, y, rg, g, b):
    dr, y, rg = dr.astype(F32), y.astype(F32), rg.astype(F32)
    s = _sig(rg)
    d_out = dr * rg * s
    dys, outs, xhs = [], [], []
    for h in range(RH):
        sl = slice(h * RDV, (h + 1) * RDV)
        xh, rstd = _norm(y[:, sl], GN_EPS)
        xhs.append(xh)
        outs.append(xh * g[:, sl] + b[:, sl])
        dys.append(_norm_bwd(d_out[:, sl] * g[:, sl], xh, rstd))
    xh, out = jnp.concatenate(xhs, axis=1), jnp.concatenate(outs, axis=1)
    d_rg = dr * out * s * (1.0 + rg * (1.0 - s))
    return jnp.concatenate(dys, axis=1), d_rg, jnp.sum(d_out * xh, 0, keepdims=True), jnp.sum(d_out, 0, keepdims=True)


def _f_ple_bwd(dh, z, pp):
    s = _sig(z)
    return dh * s, dh * pp * s * (1.0 - s)


QKV = 3 * HD


def _to_tokens(t, d):
    if d == 1:
        return t
    *lead, S, C = t.shape
    n = len(lead)
    perm = tuple(range(n)) + (n + 1, n, n + 2)
    return t.reshape(*lead, d, S // d, C).transpose(perm).reshape(*lead, S, C)


def _to_residues(t, d):
    if d == 1:
        return t
    S, C = t.shape
    return t.reshape(S // d, d, C).transpose(1, 0, 2).reshape(S, C)


def _to_head_residues(t, d):
    S = t.shape[0]
    return t.reshape(S // d, d, NH, HD).transpose(2, 1, 0, 3).reshape(NH, S, HD)


def _w_qkv_specs(g):
    return [pl.BlockSpec((D, D), lambda *i, t=t: (3 * g + t, 0)) for t in range(3)]


def _qkv_fwd(xv, win, g, dil, name):
    Sd = xv.shape[0]
    S = Sd * dil
    tm = min(512, Sd)
    nma = Sd // tm

    def body(a_ref, wq_ref, wk_ref, wv_ref, o_ref):
        a = a_ref[...]
        q, k, v = (_dot(a, w_ref[...], 1, 1).astype(BF16) for w_ref in (wq_ref, wk_ref, wv_ref))
        for h in range(NH):
            sl = slice(h * HD, (h + 1) * HD)
            o_ref[h] = jnp.concatenate([q[:, sl], k[:, sl], v[:, sl]], axis=1)

    return pl.pallas_call(
        body, name=name, grid=(S // tm,),
        in_specs=[pl.BlockSpec((tm, D), lambda i: (i % nma, i // nma))] + _w_qkv_specs(g),
        out_specs=pl.BlockSpec((NH, tm, QKV), lambda i: (0, i, 0)), out_shape=jax.ShapeDtypeStruct((NH, S, QKV), BF16),
        compiler_params=_cparams(("parallel",)),
    )(xv, win, win, win)


def _qkv_dx(dqkv, win, g, dil, name, out_dtype, add=None, after=None):
    S = dqkv.shape[1]
    Sd = S // dil
    tm = min(512, Sd)
    nmo = Sd // tm

    def body(*refs):
        a_ref, wq_ref, wk_ref, wv_ref = refs[:4]
        o_ref = refs[-1]
        acc = None
        for t, w_ref in enumerate((wq_ref, wk_ref, wv_ref)):
            d = jnp.concatenate([a_ref[h][:, t * HD:(t + 1) * HD] for h in range(NH)], axis=1)
            part = _dot(d, w_ref[...], 1, 0)
            acc = part if acc is None else acc + part
        if add is not None:
            acc = acc + refs[4][...]
        o_ref[...] = acc.astype(out_dtype)

    o_spec = pl.BlockSpec((tm, D), lambda i: (i % nmo, i // nmo))
    in_specs = [pl.BlockSpec((NH, tm, QKV), lambda i: (0, i, 0))] + _w_qkv_specs(g)
    args = [dqkv, win, win, win]
    if add is not None:
        assert dil == 1
        in_specs.append(o_spec)
        args.append(add)
    if after is not None:
        in_specs.append(pl.BlockSpec(memory_space=pl.ANY))
        args.append(after)
    return pl.pallas_call(
        body, name=name, grid=(S // tm,), in_specs=in_specs, out_specs=o_spec,
        out_shape=jax.ShapeDtypeStruct((Sd, dil * D), out_dtype), compiler_params=_cparams(("parallel",)),
    )(*args)


GW_IN_BLOCKS = (N_IN // D, NH, HD, D)


def _qkv_dw(dqkv, xv, g, dil, name, into=None):
    S = dqkv.shape[1]
    Sd = S // dil
    tk = min(1024, Sd)
    nkb, nk = Sd // tk, S // tk
    hh = NH // 2

    def body(*refs):
        a_ref, b_ref = refs[:2]
        o_ref, acc_ref = refs[-2:]
        k = pl.program_id(1)

        @pl.when(k == 0)
        def _():
            acc_ref[...] = jnp.zeros_like(acc_ref)

        b = b_ref[...]
        for h in range(hh):
            acc_ref[h * QKV:(h + 1) * QKV, :] += _dot(a_ref[h], b, 0, 0)

        @pl.when(k == nk - 1)
        def _():
            for h in range(hh):
                for t in range(3):
                    o_ref[t, h] = acc_ref[h * QKV + t * HD:h * QKV + (t + 1) * HD, :]

    in_specs = [pl.BlockSpec((hh, tk, QKV), lambda j, k: (j, k, 0)), pl.BlockSpec((tk, D), lambda j, k: (k % nkb, k // nkb))]
    args, aliases = [dqkv, xv], {}
    if into is not None:
        aliases = {2: 0}
        in_specs.append(pl.BlockSpec(memory_space=pl.ANY))
        args.append(into)
    return pl.pallas_call(
        body, name=name, grid=(2, nk), in_specs=in_specs,
        out_specs=pl.BlockSpec((3, hh, HD, D), lambda j, k: (g, j, 0, 0)), out_shape=jax.ShapeDtypeStruct(GW_IN_BLOCKS, F32),
        input_output_aliases=aliases, scratch_shapes=[pltpu.VMEM((hh * QKV, D), F32)],
        compiler_params=_cparams(("parallel", "arbitrary")),
    )(*args)


def _band(nb, first_valid, last_valid=None):
    b = lax.broadcasted_iota(jnp.int32, (nb, SPAN, SPAN), 0)
    row = lax.broadcasted_iota(jnp.int32, (nb, SPAN, SPAN), 1)
    col = lax.broadcasted_iota(jnp.int32, (nb, SPAN, SPAN), 2)
    off = jnp.where(b == 0, jnp.where(first_valid, 0, 2 * SPAN), 0)
    if last_valid is not None:
        off = off + jnp.where(b == nb - 1, jnp.where(last_valid, 0, 2 * SPAN), 0)
    return col <= row, col >= row + off


def _attn_tiles(S, dil):
    Sd = S // dil
    T = min(1024, Sd)
    hp = min(NH, max(1, (S // T) * NH // 16))
    return Sd, T, T // SPAN, Sd // T, hp


def _attn_fwd(qkv, dil, name):
    S = qkv.shape[1]
    Sd, T, nsub, nib, hp = _attn_tiles(S, dil)
    scale = HD ** -0.5

    def body(c_ref, p_ref, o_ref, l_ref):
        ib, hb = pl.program_id(1), pl.program_id(2)
        m_cur, m_prev = _band(nsub, ib > 0)
        lane = lax.broadcasted_iota(jnp.int32, (T, HD), 1)

        @pl.when(hb == 0)
        def _():
            l_ref[...] = jnp.zeros_like(l_ref)

        lses = l_ref[...]
        for hh in range(hp):
            blk, hal = c_ref[hh], p_ref[hh]
            q, k, v = blk[:, :HD], blk[:, HD:2 * HD], blk[:, 2 * HD:]
            if nsub > 1:
                kp = jnp.concatenate([hal[:, HD:2 * HD], k[:T - SPAN]], axis=0)
                vp = jnp.concatenate([hal[:, 2 * HD:], v[:T - SPAN]], axis=0)
            else:
                kp, vp = hal[:, HD:2 * HD], hal[:, 2 * HD:]
            q3, k3, v3, kp3, vp3 = (t.reshape(nsub, SPAN, HD) for t in (q, k, v, kp, vp))
            sc = jnp.where(m_cur, _bdot(q3, k3, 2, 2) * scale, NEG)
            sp = jnp.where(m_prev, _bdot(q3, kp3, 2, 2) * scale, NEG)
            m = jnp.maximum(jnp.max(sc, -1, keepdims=True), jnp.max(sp, -1, keepdims=True))
            pc, pp = jnp.exp(sc - m), jnp.exp(sp - m)
            den = jnp.sum(pc, -1, keepdims=True) + jnp.sum(pp, -1, keepdims=True)
            o = (_bdot(pc.astype(BF16), v3, 2, 1) + _bdot(pp.astype(BF16), vp3, 2, 1)) / den
            o_ref[hh] = o.reshape(T, HD).astype(BF16)
            lses = jnp.where(lane == hb * hp + hh, (m + jnp.log(den)).reshape(T, 1), lses)
        l_ref[...] = lses

    cur = pl.BlockSpec((hp, T, QKV), lambda r, ib, h: (h, r * nib + ib, 0))
    prev = pl.BlockSpec((hp, SPAN, QKV), lambda r, ib, h: (h, r * (Sd // SPAN) + jnp.maximum(ib * nsub - 1, 0), 0))
    return pl.pallas_call(
        body, name=name, grid=(dil, nib, NH // hp), in_specs=[cur, prev],
        out_specs=[pl.BlockSpec((hp, T, HD), lambda r, ib, h: (h, r * nib + ib, 0)),
                   pl.BlockSpec((T, HD), lambda r, ib, h: (r * nib + ib, 0))],
        out_shape=[jax.ShapeDtypeStruct((NH, S, HD), BF16), jax.ShapeDtypeStruct((S, HD), F32)],
        compiler_params=_cparams(("parallel", "parallel", "arbitrary")),
    )(qkv, qkv)


def _attn_bwd(qkv, d_attn, lse, delta, dil, name):
    S = qkv.shape[1]
    Sd, T, nsub, nib, hp = _attn_tiles(S, dil)
    scale = HD ** -0.5
    ne = nsub + 1

    def body(c_ref, p_ref, n_ref, do_ref, don_ref, l_ref, ln_ref, dl_ref, dln_ref, o_ref):
        ib, hb = pl.program_id(1), pl.program_id(2)
        _, m_prev = _band(ne, ib > 0, ib < nib - 1)
        m_cur, _ = _band(nsub, True)
        for hh in range(hp):
            h = hb * hp + hh
            blk, hal, nxt = c_ref[hh], p_ref[hh], n_ref[hh]
            q, k, v = blk[:, :HD], blk[:, HD:2 * HD], blk[:, 2 * HD:]
            do = do_ref[hh]
            l, dl = _head_col(l_ref[...], h), _head_col(dl_ref[...], h)
            qe = jnp.concatenate([q, nxt[:, :HD]], axis=0).reshape(ne, SPAN, HD)
            doe = jnp.concatenate([do, don_ref[hh]], axis=0).reshape(ne, SPAN, HD)
            le = jnp.concatenate([l, _head_col(ln_ref[...], h)], axis=0).reshape(ne, SPAN, 1)
            dle = jnp.concatenate([dl, _head_col(dln_ref[...], h)], axis=0).reshape(ne, SPAN, 1)
            kpe = jnp.concatenate([hal[:, HD:2 * HD], k], axis=0).reshape(ne, SPAN, HD)
            vpe = jnp.concatenate([hal[:, 2 * HD:], v], axis=0).reshape(ne, SPAN, HD)
            p = jnp.where(m_prev, jnp.exp(_bdot(qe, kpe, 2, 2) * scale - le), 0.0)
            ds = (p * (_bdot(doe, vpe, 2, 2) - dle)).astype(BF16)
            dq = _bdot(ds, kpe, 2, 1)[:nsub]
            dk = _bdot(ds, qe, 1, 1)[1:]
            dv = _bdot(p.astype(BF16), doe, 1, 1)[1:]
            q3, k3, v3, do3 = (t.reshape(nsub, SPAN, HD) for t in (q, k, v, do))
            l3, dl3 = l.reshape(nsub, SPAN, 1), dl.reshape(nsub, SPAN, 1)
            p = jnp.where(m_cur, jnp.exp(_bdot(q3, k3, 2, 2) * scale - l3), 0.0)
            ds = (p * (_bdot(do3, v3, 2, 2) - dl3)).astype(BF16)
            dq = (dq + _bdot(ds, k3, 2, 1)) * scale
            dk = (dk + _bdot(ds, q3, 1, 1)) * scale
            dv = dv + _bdot(p.astype(BF16), do3, 1, 1)
            o_ref[hh] = jnp.concatenate([t.reshape(T, HD) for t in (dq, dk, dv)], axis=1).astype(BF16)

    nb = Sd // SPAN
    row = lambda r, ib: r * nib + ib
    prow = lambda r, ib: r * nb + jnp.maximum(ib * nsub - 1, 0)
    nrow = lambda r, ib: r * nb + jnp.minimum((ib + 1) * nsub, nb - 1)
    cur3 = pl.BlockSpec((hp, T, QKV), lambda r, ib, h: (h, row(r, ib), 0))
    prev3 = pl.BlockSpec((hp, SPAN, QKV), lambda r, ib, h: (h, prow(r, ib), 0))
    next3 = pl.BlockSpec((hp, SPAN, QKV), lambda r, ib, h: (h, nrow(r, ib), 0))
    cur1 = pl.BlockSpec((hp, T, HD), lambda r, ib, h: (h, row(r, ib), 0))
    next1 = pl.BlockSpec((hp, SPAN, HD), lambda r, ib, h: (h, nrow(r, ib), 0))
    curc = pl.BlockSpec((T, HD), lambda r, ib, h: (row(r, ib), 0))
    nextc = pl.BlockSpec((SPAN, HD), lambda r, ib, h: (nrow(r, ib), 0))
    return pl.pallas_call(
        body, name=name, grid=(dil, nib, NH // hp),
        in_specs=[cur3, prev3, next3, cur1, next1, curc, nextc, curc, nextc], out_specs=cur3,
        out_shape=jax.ShapeDtypeStruct((NH, S, QKV), BF16),
        compiler_params=_cparams(("parallel", "parallel", "parallel")),
    )(qkv, qkv, qkv, d_attn, d_attn, lse, lse, delta, delta)


def _ret_consts():
    lg = np.log1p(-np.exp2(-5.0 - np.arange(RH, dtype=np.float64)))
    idx = np.arange(CH, dtype=np.float64)
    rel = idx[:, None] - idx[None, :]
    intra = np.where(rel >= 0, np.exp(lg[:, None, None] * np.maximum(rel, 0.0)), 0.0)
    qd = np.exp(lg[:, None] * (idx + 1.0))
    kd = np.exp(lg[:, None] * (CH - 1.0 - idx))
    cd = np.exp(lg * CH)
    wide = lambda t: np.broadcast_to(t[:, :, None], (RH, t.shape[1], RDV))
    return (jnp.asarray(intra, F32), jnp.asarray(wide(qd), F32), jnp.asarray(wide(kd), F32),
            jnp.asarray(np.broadcast_to(cd[:, None, None], (RH, 1, RDV)), F32))


def _rot(t, c, s):
    t1, t2 = t[:, :RDK // 2], t[:, RDK // 2:]
    return jnp.concatenate([t1 * c - t2 * s, t1 * s + t2 * c], axis=1)


def _unrot(d, c, s):
    d1, d2 = d[:, :RDK // 2], d[:, RDK // 2:]
    return jnp.concatenate([d1 * c + d2 * s, d2 * c - d1 * s], axis=1)


RCH = 4


def _ret_specs(nmap):
    rows = RCH * CH
    q = pl.BlockSpec((rows, RH * RDK), lambda n: (nmap(n), OFF_RQ // (RH * RDK)))
    k = pl.BlockSpec((rows, RH * RDK), lambda n: (nmap(n), OFF_RK // (RH * RDK)))
    v = pl.BlockSpec((rows, RH * RDV), lambda n: (nmap(n), OFF_RV // (RH * RDV)))
    cs = pl.BlockSpec((rows, RDK // 2), lambda n: (nmap(n), 0))
    dmat = pl.BlockSpec((RH, CH, CH), lambda n: (0, 0, 0))
    dvec = pl.BlockSpec((RH, CH, RDV), lambda n: (0, 0, 0))
    cdv = pl.BlockSpec((RH, 1, RDV), lambda n: (0, 0, 0))
    state = pl.BlockSpec((RH, RCH, RDK, RDV), lambda n: (0, nmap(n), 0, 0))
    out = pl.BlockSpec((rows, RH * RDV), lambda n: (nmap(n), 0))
    return [q, k, v, cs, cs, dmat, dvec, dvec, cdv], state, out


def _ret_fwd(proj, cos, sin, consts):
    S = proj.shape[0]
    nc = S // CH

    def body(q_ref, k_ref, v_ref, c_ref, s_ref, d_ref, qd_ref, kd_ref, cd_ref, o_ref, st_ref, state):
        @pl.when(pl.program_id(0) == 0)
        def _():
            state[...] = jnp.zeros_like(state)

        for ci in range(RCH):
            rows = slice(ci * CH, (ci + 1) * CH)
            c, s = c_ref[rows, :], s_ref[rows, :]
            for h in range(RH):
                qk, vv = slice(h * RDK, (h + 1) * RDK), slice(h * RDV, (h + 1) * RDV)
                qb = _rot(q_ref[rows, qk].astype(F32), c, s).astype(BF16)
                kb = (_rot(k_ref[rows, qk].astype(F32), c, s) * (RDK ** -0.5)).astype(BF16)
                vb = v_ref[rows, vv]
                sb = state[h].astype(BF16)
                st_ref[h, ci] = sb
                a = (_dot(qb, kb, 1, 1) * d_ref[h]).astype(BF16)
                o_ref[rows, vv] = (_dot(a, vb, 1, 0) + _dot(qb, sb, 1, 0) * qd_ref[h]).astype(BF16)
                vk = (vb.astype(F32) * kd_ref[h]).astype(BF16)
                state[h] = cd_ref[h] * state[h] + _dot(kb, vk, 0, 0)

    ins, state_spec, out_spec = _ret_specs(lambda n: n)
    return pl.pallas_call(
        body, name="ret_fwd", grid=(nc // RCH,), in_specs=ins, out_specs=[out_spec, state_spec],
        out_shape=[jax.ShapeDtypeStruct((S, RH * RDV), BF16), jax.ShapeDtypeStruct((RH, nc, RDK, RDV), BF16)],
        scratch_shapes=[pltpu.VMEM((RH, RDK, RDV), F32)],
        compiler_params=_cparams(("arbitrary",)),
    )(proj, proj, proj, cos, sin, *consts)


def _ret_bwd(proj, cos, sin, consts, states, d_ret, d_rest):
    S = proj.shape[0]
    nc = S // CH

    def body(q_ref, k_ref, v_ref, c_ref, s_ref, d_ref, qd_ref, kd_ref, cd_ref, st_ref, do_ref, _, o_ref, dstate):
        @pl.when(pl.program_id(0) == 0)
        def _():
            dstate[...] = jnp.zeros_like(dstate)

        for ci in reversed(range(RCH)):
            rows = slice(ci * CH, (ci + 1) * CH)
            c, s = c_ref[rows, :], s_ref[rows, :]
            for h in range(RH):
                qk, vv = slice(h * RDK, (h + 1) * RDK), slice(h * RDV, (h + 1) * RDV)
                qb = _rot(q_ref[rows, qk].astype(F32), c, s).astype(BF16)
                kb = (_rot(k_ref[rows, qk].astype(F32), c, s) * (RDK ** -0.5)).astype(BF16)
                vb, sb, do = v_ref[rows, vv], st_ref[h, ci], do_ref[rows, vv]
                dmat, qd, kd = d_ref[h], qd_ref[h], kd_ref[h]
                a = (_dot(qb, kb, 1, 1) * dmat).astype(BF16)
                doq = (do.astype(F32) * qd).astype(BF16)
                dsb = dstate[h].astype(BF16)
                vk = (vb.astype(F32) * kd).astype(BF16)
                o_ref[rows, OFF_RV + h * RDV:OFF_RV + (h + 1) * RDV] = (_dot(a, do, 0, 0) + _dot(kb, dsb, 1, 0) * kd).astype(BF16)
                da = (_dot(do, vb, 1, 1) * dmat).astype(BF16)
                dq = _dot(da, kb, 1, 0) + _dot(doq, sb, 1, 1)
                dk = (_dot(da, qb, 0, 0) + _dot(vk, dsb, 1, 1)) * (RDK ** -0.5)
                o_ref[rows, OFF_RQ + h * RDK:OFF_RQ + (h + 1) * RDK] = _unrot(dq, c, s).astype(BF16)
                o_ref[rows, OFF_RK + h * RDK:OFF_RK + (h + 1) * RDK] = _unrot(dk, c, s).astype(BF16)
                dstate[h] = cd_ref[h] * dstate[h] + _dot(qb, doq, 0, 0)

    nsteps = nc // RCH
    rev = lambda n: nsteps - 1 - n
    ins, state_spec, out_spec = _ret_specs(rev)
    return pl.pallas_call(
        body, name="ret_bwd", grid=(nsteps,), in_specs=ins + [state_spec, out_spec, pl.BlockSpec(memory_space=pl.ANY)],
        out_specs=pl.BlockSpec((RCH * CH, OFF_RG), lambda n: (rev(n), 0)),
        out_shape=jax.ShapeDtypeStruct(d_rest.shape, BF16), input_output_aliases={11: 0},
        scratch_shapes=[pltpu.VMEM((RH, RDK, RDV), F32)],
        compiler_params=_cparams(("arbitrary",)),
    )(proj, proj, proj, cos, sin, *consts, states, d_ret, d_rest)


CW = 256
HALO = 16


def _shift_down(v, halo, k):
    rolled = pltpu.roll(v, k, 0)
    hr = pltpu.roll(halo, k, 0)[0:8]
    row = lax.broadcasted_iota(jnp.int32, hr.shape, 0)
    return jnp.concatenate([jnp.where(row < k, hr, rolled[0:8]), rolled[8:]], axis=0)


def _shift_up(v, halo, k):
    T = v.shape[0]
    rolled = pltpu.roll(v, T - k, 0)
    hr = pltpu.roll(halo, 8 - k, 0)[0:8]
    row = lax.broadcasted_iota(jnp.int32, hr.shape, 0)
    return jnp.concatenate([rolled[:T - 8], jnp.where(row >= 8 - k, hr, rolled[T - 8:])], axis=0)


def _conv_taps(h_ref, hp_ref, first):
    h = h_ref[...].astype(F32)
    hp = hp_ref[...].astype(F32) * jnp.where(first, 0.0, 1.0)
    return _shift_down(h, hp, 2), _shift_down(h, hp, 1), h


def _conv_specs(S, T, cw=CW):
    nj = DFF // cw
    cur = pl.BlockSpec((T, cw), lambda j, i: (i, j))
    prev = pl.BlockSpec((HALO, cw), lambda j, i: (jnp.maximum(i * (T // HALO) - 1, 0), j))
    nxt = pl.BlockSpec((HALO, cw), lambda j, i: (jnp.minimum((i + 1) * (T // HALO), S // HALO - 1), j))
    w = pl.BlockSpec((3, cw), lambda j, i: (0, j))
    b = pl.BlockSpec((1, cw), lambda j, i: (0, j))
    return nj, cur, prev, nxt, w, b


def _conv_fwd(hg, hu, wg, wu, bg, bu):
    S = hg.shape[0]
    T = min(1024, S)
    nj, cur, prev, _, w, b = _conv_specs(S, T)

    def body(hg_ref, hu_ref, hgp_ref, hup_ref, wg_ref, wu_ref, bg_ref, bu_ref, o_ref):
        first = pl.program_id(1) == 0
        g2, g1, g0 = _conv_taps(hg_ref, hgp_ref, first)
        u2, u1, u0 = _conv_taps(hu_ref, hup_ref, first)
        cg = wg_ref[0:1, :] * g2 + wg_ref[1:2, :] * g1 + wg_ref[2:3, :] * g0 + bg_ref[...]
        cu = wu_ref[0:1, :] * u2 + wu_ref[1:2, :] * u1 + wu_ref[2:3, :] * u0 + bu_ref[...]
        o_ref[...] = (_gelu(cg)[0] * cu).astype(BF16)

    return pl.pallas_call(
        body, name="conv_fwd", grid=(nj, S // T), in_specs=[cur, cur, prev, prev, w, w, b, b], out_specs=cur,
        out_shape=jax.ShapeDtypeStruct((S, DFF), BF16), compiler_params=_cparams(("parallel", "parallel")),
    )(hg, hu, hg, hu, wg, wu, bg, bu)


def _conv_bwd_pre(d_act, hg, hu, wg, wu, bg, bu):
    S = hg.shape[0]
    T = min(1024, S)
    nj, cur, prev, _, w, b = _conv_specs(S, T)

    def body(da_ref, hg_ref, hu_ref, hgp_ref, hup_ref, wg_ref, wu_ref, bg_ref, bu_ref,
             dcg_ref, dcu_ref, gwg_ref, gwu_ref, gbg_ref, gbu_ref):
        first = pl.program_id(1) == 0
        g2, g1, g0 = _conv_taps(hg_ref, hgp_ref, first)
        u2, u1, u0 = _conv_taps(hu_ref, hup_ref, first)
        cg = wg_ref[0:1, :] * g2 + wg_ref[1:2, :] * g1 + wg_ref[2:3, :] * g0 + bg_ref[...]
        cu = wu_ref[0:1, :] * u2 + wu_ref[1:2, :] * u1 + wu_ref[2:3, :] * u0 + bu_ref[...]
        da = da_ref[...].astype(F32)
        gl, t = _gelu(cg)
        dcg = da * cu * _gelu_grad(cg, t)
        dcu = da * gl
        dcg_ref[...] = dcg.astype(BF16)
        dcu_ref[...] = dcu.astype(BF16)

        @pl.when(first)
        def _():
            for r in (gwg_ref, gwu_ref, gbg_ref, gbu_ref):
                r[...] = jnp.zeros_like(r)

        for r, d, taps in ((gwg_ref, dcg, (g2, g1, g0)), (gwu_ref, dcu, (u2, u1, u0))):
            for j in range(3):
                r[j:j + 1, :] += jnp.sum(d * taps[j], 0, keepdims=True)
        gbg_ref[...] += jnp.sum(dcg, 0, keepdims=True)
        gbu_ref[...] += jnp.sum(dcu, 0, keepdims=True)

    return pl.pallas_call(
        body, name="conv_bwd_pre", grid=(nj, S // T), in_specs=[cur, cur, cur, prev, prev, w, w, b, b],
        out_specs=[cur, cur, w, w, b, b],
        out_shape=[jax.ShapeDtypeStruct((S, DFF), BF16)] * 2 + [jax.ShapeDtypeStruct((3, DFF), F32)] * 2
        + [jax.ShapeDtypeStruct((1, DFF), F32)] * 2,
        compiler_params=_cparams(("parallel", "arbitrary")),
    )(d_act, hg, hu, hg, hu, wg, wu, bg, bu)


def _conv_bwd_in(dc, w, name):
    S = dc.shape[0]
    T = min(512, S)
    nj, cur, _, nxt, wspec, _ = _conv_specs(S, T, DFF // 2)
    nt = S // T

    def body(dc_ref, dn_ref, w_ref, o_ref):
        d = dc_ref[...].astype(F32)
        dn = dn_ref[...].astype(F32) * jnp.where(pl.program_id(1) == nt - 1, 0.0, 1.0)
        o_ref[...] = (w_ref[2:3, :] * d + w_ref[1:2, :] * _shift_up(d, dn, 1) + w_ref[0:1, :] * _shift_up(d, dn, 2)).astype(BF16)

    return pl.pallas_call(
        body, name=name, grid=(nj, nt), in_specs=[cur, nxt, wspec], out_specs=cur,
        out_shape=jax.ShapeDtypeStruct((S, DFF), BF16), compiler_params=_cparams(("parallel", "parallel")),
    )(dc, dc, w)


def _adam_math(g, w, m, v):
    m = B1 * m + (1.0 - B1) * g
    v = B2 * v + (1.0 - B2) * (g * g)
    m_hat = m / (1.0 - B1 ** STEP)
    v_hat = v / (1.0 - B2 ** STEP)
    return -LR * (m_hat / (jnp.sqrt(v_hat) + EPS) + WD * w), m, v


def _reduce_tail(chip32, far, chip, name, wmv=None):
    L = len(chip32)
    _, R, C = chip32[0].shape
    tr = _tile(R, 256, 16)
    nr = R // tr

    def body(chip_ref, *refs):
        own_refs, far_refs, rest = refs[:L], refs[L:2 * L], refs[2 * L:]
        outs = rest[3:] if wmv else rest
        for ll in range(L):
            @pl.when(pl.program_id(0) == ll)
            def _(ll=ll):
                g = own_refs[ll][...]
                for s in range(3):
                    g = g + far_refs[ll][s].astype(F32)
                outs[0][...] = g
                if wmv:
                    outs[1][...], outs[2][...], outs[3][...] = _adam_math(g, rest[0][...], rest[1][...], rest[2][...])

    def rows(ll):
        return lambda l, i: jnp.where(l == ll, i, jnp.where(l < ll, 0, nr - 1))

    blk = pl.BlockSpec((None, tr, C), lambda l, i, ch: (l, i, 0))
    in_specs = [pl.BlockSpec((None, tr, C), lambda l, i, ch, f=rows(ll): (ch[0], f(l, i), 0)) for ll in range(L)]
    in_specs += [pl.BlockSpec((3, tr, C), lambda l, i, ch, f=rows(ll): (0, f(l, i), 0)) for ll in range(L)]
    args = list(chip32) + list(far)
    n_out = 1
    if wmv:
        in_specs += [blk] * 3
        args += list(wmv)
        n_out = 4
    return pl.pallas_call(
        body, name=name,
        grid_spec=pltpu.PrefetchScalarGridSpec(num_scalar_prefetch=1, grid=(L, nr), in_specs=in_specs, out_specs=[blk] * n_out),
        out_shape=[jax.ShapeDtypeStruct((L, R, C), F32)] * n_out, compiler_params=_cparams(("arbitrary", "arbitrary")),
    )(chip, *args)


def _adamw(g, w, m, v, name):
    R, C = g.shape
    tr = _tile(R, 128, 8)

    def body(g_ref, w_ref, m_ref, v_ref, d_ref, nm_ref, nv_ref):
        d_ref[...], nm_ref[...], nv_ref[...] = _adam_math(g_ref[...], w_ref[...], m_ref[...], v_ref[...])

    blk = pl.BlockSpec((tr, C), lambda i: (i, 0))
    return pl.pallas_call(
        body, name=name, grid=(R // tr,), in_specs=[blk] * 4, out_specs=[blk] * 3,
        out_shape=[jax.ShapeDtypeStruct(g.shape, F32)] * 3, compiler_params=_cparams(("parallel",)),
    )(g, w, m, v)


def _pair_sum(x, recv, core, name):
    _, R, C = x.shape
    tr = _tile(R, 600, 16)

    def body(core_ref, x_ref, r_ref, o32_ref, o16_ref):
        s = x_ref[...] + r_ref[...]
        o32_ref[...] = s
        o16_ref[...] = s.astype(BF16)

    blk = pl.BlockSpec((None, tr, C), lambda q, i, c: (q, i, 0))
    mine = pl.BlockSpec((None, None, tr, C), lambda q, i, c: (q, c[0], i, 0))
    return pl.pallas_call(
        body, name=name,
        grid_spec=pltpu.PrefetchScalarGridSpec(num_scalar_prefetch=1, grid=(4, R // tr), in_specs=[mine, blk], out_specs=[blk, blk]),
        out_shape=[jax.ShapeDtypeStruct((4, R, C), F32), jax.ShapeDtypeStruct((4, R, C), BF16)],
        compiler_params=_cparams(("parallel", "parallel")),
    )(core, x.reshape(4, 2, R, C), recv)


def _sum_slots(x, name):
    def body(x_ref, o_ref):
        g = x_ref[0]
        for s in range(1, x.shape[0]):
            g = g + x_ref[s]
        o_ref[...] = g

    return pl.pallas_call(body, name=name, out_shape=jax.ShapeDtypeStruct(x.shape[1:], F32))(x)


MESH = pl.DeviceIdType.MESH
_HBM = pl.BlockSpec(memory_space=pltpu.HBM)


def _dma_sems(n):
    return pltpu.SemaphoreType.DMA((n,))


def _gather_many(xs, name):
    n = len(xs)

    def body(*refs):
        x_refs, out_refs = refs[:n], refs[n:2 * n]
        send_sems, recv_sems, local_sems = refs[2 * n:]
        ax, ay, ac = lax.axis_index("x"), lax.axis_index("y"), lax.axis_index("c")
        me, sibling = (ax, ay, ac), (ax, ay, 1 - ac)
        chips = [(1 - ax, ay), (ax, 1 - ay), (1 - ax, 1 - ay)]

        def copy(a, k, block, to, own=False):
            slot = out_refs[a].at[4 * block[0] + 2 * block[1] + block[2]]
            return pltpu.make_async_remote_copy(
                src_ref=x_refs[a] if own else slot, dst_ref=slot, send_sem=send_sems.at[7 * a + k],
                recv_sem=recv_sems.at[7 * a + k], device_id=to, device_id_type=MESH)

        mine = [pltpu.make_async_copy(x_refs[a], out_refs[a].at[4 * ax + 2 * ay + ac], local_sems.at[a]) for a in range(n)]
        first = [copy(a, 0, me, sibling, own=True) for a in range(n)]
        first += [copy(a, 1 + j, me, (*chip, ac), own=True) for j, chip in enumerate(chips) for a in range(n)]
        for cp in mine + first:
            cp.start()
        passed = []
        for j, chip in enumerate(chips):
            for a in range(n):
                copy(a, 1 + j, (*chip, ac), me).wait_recv()
                cp = copy(a, 4 + j, (*chip, ac), sibling)
                cp.start()
                passed.append(cp)
        for a in range(n):
            copy(a, 0, sibling, me).wait_recv()
            for j, chip in enumerate(chips):
                copy(a, 4 + j, (*chip, 1 - ac), me).wait_recv()
        for cp in first + passed:
            cp.wait_send()
        for cp in mine:
            cp.wait()

    return pl.pallas_call(
        body, name=name, out_shape=[jax.ShapeDtypeStruct((N_DEV,) + x.shape, x.dtype) for x in xs],
        in_specs=[_HBM] * n, out_specs=[_HBM] * n, scratch_shapes=[_dma_sems(7 * n), _dma_sems(7 * n), _dma_sems(n)],
    )(*xs)


_SEM = pl.BlockSpec(memory_space=pltpu.SEMAPHORE)
_EFFECT = pltpu.SideEffectType.DATAFLOW_SIDE_EFFECTING


def _peer(k):
    ax, ay, ac = lax.axis_index("x"), lax.axis_index("y"), lax.axis_index("c")
    px = 1 - ax if k & 4 else ax
    py = 1 - ay if k & 2 else ay
    pc = 1 - ac if k & 1 else ac
    return (px, py, pc), 4 * px + 2 * py + pc


def _build_gather(x_refs, land_refs, send_sems, recv_sems, waiting):
    _, me = _peer(0)
    copies = []
    for a in range(len(x_refs)):
        for k in range(1, N_DEV):
            peer, slot = _peer(k)
            copies.append(pltpu.make_async_remote_copy(
                src_ref=x_refs[a], dst_ref=land_refs[a].at[slot if waiting else me], send_sem=send_sems.at[7 * a + k - 1],
                recv_sem=recv_sems.at[7 * a + k - 1], device_id=peer, device_id_type=MESH))
    return copies


def _build_cores(x_refs, land_refs, send_sems, recv_sems, waiting):
    ax, ay, ac = lax.axis_index("x"), lax.axis_index("y"), lax.axis_index("c")
    copies = []
    for a in range(len(x_refs)):
        for q in range(4):
            copies.append(pltpu.make_async_remote_copy(
                src_ref=x_refs[a].at[2 * q + 1 - ac], dst_ref=land_refs[a].at[q], send_sem=send_sems.at[4 * a + q],
                recv_sem=recv_sems.at[4 * a + q], device_id=(ax, ay, 1 - ac), device_id_type=MESH))
    return copies


def _build_chips(p_refs, land_refs, send_sems, recv_sems, waiting):
    ax, ay, ac = lax.axis_index("x"), lax.axis_index("y"), lax.axis_index("c")
    copies = []
    for a in range(len(p_refs)):
        for k in range(1, 4):
            px = 1 - ax if k & 2 else ax
            py = 1 - ay if k & 1 else ay
            copies.append(pltpu.make_async_remote_copy(
                src_ref=p_refs[a].at[2 * px + py], dst_ref=land_refs[a].at[k - 1], send_sem=send_sems.at[3 * a + k - 1],
                recv_sem=recv_sems.at[3 * a + k - 1], device_id=(px, py, ac), device_id_type=MESH))
    return copies


_EXCHANGES = {"gather": (_build_gather, 7, N_DEV), "cores": (_build_cores, 4, 4), "chips": (_build_chips, 3, 3)}


def _exchange_start(kind, xs, lands, name, after=None):
    build, per, _ = _EXCHANGES[kind]
    n = len(xs)

    def body(*refs):
        for cp in build(refs[:n], refs[n:2 * n], refs[-2 * n - 3], refs[-2 * n - 2], False):
            cp.start()
        refs[-1][...] = jnp.zeros_like(refs[-1])

    hbm = lambda t: pltpu.HBM(t.shape, t.dtype)
    args = [pltpu.with_memory_space_constraint(t, pltpu.HBM) for t in list(xs) + list(lands)]
    in_specs = [_HBM] * (2 * n)
    if after is not None:
        args.append(after)
        in_specs.append(pl.BlockSpec(memory_space=pl.ANY))
    outs = pl.pallas_call(
        body, name=name,
        out_shape=(_dma_sems(per * n), _dma_sems(per * n), *[hbm(t) for t in xs], *[hbm(t) for t in lands],
                   jax.ShapeDtypeStruct((8, 128), F32)),
        in_specs=in_specs, out_specs=(_SEM, _SEM, *[_HBM] * (2 * n), pl.BlockSpec(memory_space=pltpu.VMEM)),
        input_output_aliases={a: 2 + a for a in range(2 * n)},
        compiler_params=pltpu.CompilerParams(has_side_effects=_EFFECT),
    )(*args)
    return (kind, outs[0], outs[1], outs[2:2 + n], outs[2 + n:2 + 2 * n]), outs[-1]


def _exchange_wait(flight, after, name):
    kind, send_sems, recv_sems, xs, lands = flight
    build = _EXCHANGES[kind][0]
    n = len(xs)

    def body(*refs):
        for cp in build(refs[:n], refs[n:2 * n], refs[2 * n], refs[2 * n + 1], True):
            cp.wait_send()
            cp.wait_recv()

    hbm = lambda t: pltpu.HBM(t.shape, t.dtype)
    outs = pl.pallas_call(
        body, name=name, out_shape=(*[hbm(t) for t in xs], *[hbm(t) for t in lands]),
        in_specs=[_HBM] * (2 * n) + [_SEM, _SEM, pl.BlockSpec(memory_space=pl.ANY)], out_specs=[_HBM] * (2 * n),
        input_output_aliases={a: a for a in range(2 * n)}, compiler_params=pltpu.CompilerParams(has_side_effects=_EFFECT),
    )(*xs, *lands, send_sems, recv_sems, after)
    return outs[:n], outs[n:]


def _x_view(xb, d):
    return xb if d == 1 else xb.reshape(xb.shape[0] // d, d * xb.shape[1])


def _layer_fwd(x, xb, p, w, cos, sin, rconsts, late=None):
    proj = _mm(xb, w["win"], tb=True, b_rows=(N_ATT, N_REST), name="mm_proj", out_dtype=BF16)
    qkvs, ogs, lgs = [], [], []
    for g, dil in enumerate(DILATIONS):
        qkv = _qkv_fwd(_x_view(xb, dil), w["win"], g, dil, f"mm_qkv{g}")
        o, l = _attn_fwd(qkv, dil, f"attn_fwd_g{g}")
        qkvs.append(qkv)
        ogs.append(_to_tokens(o, dil))
        lgs.append(_to_tokens(l, dil))
    attn, lse = _rowwise(_f_combine, ogs + lgs, [], [(D, BF16), (HD, F32)], [], name="attn_combine")
    ret_raw, states = _ret_fwd(proj, cos, sin, rconsts)
    rg_win = (proj, RH * RDV, OFF_RG // (RH * RDV))
    ga_win, gr_win = (proj, D, OFF_GA // D), (proj, D, OFF_GR // D)
    (r,) = _rowwise(_f_gn, [ret_raw, rg_win], [w["ret_gn_g"], w["ret_gn_b"]], [(RH * RDV, BF16)], [], name="gn_fwd", tm=256)
    if late is not None:
        w = {**w, **late(r)}
    ap = _mm(attn, w["w_attn_proj"], name="mm_attn_proj", out_dtype=BF16)
    rp = _mm(r, w["w_ret_proj"], name="mm_ret_proj", out_dtype=BF16, tk=2048)
    (merged,) = _rowwise(_f_gate, [ap, rp, ga_win, gr_win], [], [(D, BF16)], [], name="gate_fwd")
    mix = _mm(merged, w["w_out"], name="mm_out")
    h1, x1, x1b = _rowwise(_f_ln1, [x, mix], [w["ln1_g"], w["ln1_b"]], [(D, F32), (D, F32), (D, BF16)], [], name="ln1_fwd")
    z = _mm(x1b, w["w_ple_gate"], name="mm_ple_gate")
    pp = _mm(p, w["w_ple_proj"], tb=True, name="mm_ple_proj")
    hg = _mm(x1b, w["w_up"], tb=True, b_rows=(0, DFF), name="mm_up_g", out_dtype=BF16, tm=512, tn=DFF)
    hu = _mm(x1b, w["w_up"], tb=True, b_rows=(DFF, DFF), name="mm_up_u", out_dtype=BF16, tm=512, tn=DFF)
    act = _conv_fwd(hg, hu, w["conv_wg"], w["conv_wu"], w["conv_bg"], w["conv_bu"])
    ffn = _mm(act, w["w_down"], name="mm_down", tm=512, tk=DFF)
    h2, x2, x2b = _rowwise(_f_ln2, [x1, ffn, z, pp], [w["ln2_g"], w["ln2_b"]], [(D, F32), (D, F32), (D, BF16)], [], name="ln2_fwd")
    saved = dict(xb=xb, proj=proj, qkvs=qkvs, attn=attn, lse=lse, ret_raw=ret_raw, states=states, r=r, ap=ap, rp=rp,
                 merged=merged, h1=h1, x1b=x1b, z=z, pp=pp, hg=hg, hu=hu, act=act, h2=h2, p=p)
    return x2, x2b, saved, w


def _after(fn, token):
    return fn if token is None else (lambda *a: fn(*a[:-1]))


def _layer_bwd(dys, w, sv, cos, sin, rconsts, hooks):
    gr = {}
    proj = sv["proj"]
    call = lambda key, *a: hooks[key](*a) if key in hooks else None
    held = lambda token: [] if token is None else [token]
    token = hooks.get("token")
    dh2, dh2b, gr["ln2_g"], gr["ln2_b"] = _rowwise(_after(_f_ln_bwd, token), list(dys) + [sv["h2"]], [w["ln2_g"]] + held(token),
                                                   [(D, F32), (D, BF16)], [(1, D), (1, D)], name="ln2_bwd")
    d_act = _mm(dh2b, w["w_down"], tb=True, name="mm_down_dx", out_dtype=BF16, tm=512, tn=DFF)
    gr["w_down"] = _mm(sv["act"], dh2b, ta=True, name="mm_down_dw", tm=DFF // 2)
    dcg, dcu, gwg, gwu, gbg, gbu = _conv_bwd_pre(d_act, sv["hg"], sv["hu"], w["conv_wg"], w["conv_wu"], w["conv_bg"], w["conv_bu"])
    token = call("after_ffn", dcg)
    gr["conv_w"] = jnp.concatenate([gwg, gwu], axis=1)
    gr["conv_b"] = jnp.concatenate([gbg, gbu], axis=1)
    dhg = _conv_bwd_in(dcg, w["conv_wg"], "conv_bwd_in_g")
    dhu = _conv_bwd_in(dcu, w["conv_wu"], "conv_bwd_in_u")
    gw_up = _mm(dhg, sv["x1b"], ta=True, name="mm_up_g_dw", tm=DFF // 2, out_rows=(0, 2 * DFF))
    gr["w_up"] = _mm(dhu, sv["x1b"], ta=True, name="mm_up_u_dw", tm=DFF // 2, out_rows=(DFF, 2 * DFF), into=gw_up)
    dx1 = _mm(dhg, w["w_up"], b_rows=(0, DFF), name="mm_up_g_dx", add=dh2, add_scale=ALPHA, tm=512, tk=DFF)
    dx1 = _mm(dhu, w["w_up"], b_rows=(DFF, DFF), name="mm_up_u_dx", add=dx1, tm=512, tk=DFF)
    dpp, dz = _rowwise(_f_ple_bwd, [dh2, sv["z"], sv["pp"]], [], [(D, BF16), (D, BF16)], [], name="ple_bwd")
    gr["w_ple_proj"] = _mm(dpp, sv["p"], ta=True, name="mm_ple_proj_dw")
    gr["w_ple_gate"] = _mm(sv["x1b"], dz, ta=True, name="mm_ple_gate_dw")
    dx1 = _mm(dz, w["w_ple_gate"], tb=True, name="mm_ple_gate_dx", add=dx1)
    dh1, dh1b, gr["ln1_g"], gr["ln1_b"] = _rowwise(_after(_f_ln_bwd, token), [dx1, sv["h1"]], [w["ln1_g"]] + held(token),
                                                   [(D, F32), (D, BF16)], [(1, D), (1, D)], name="ln1_bwd")
    d_merged = _mm(dh1b, w["w_out"], tb=True, name="mm_out_dx", out_dtype=BF16)
    gr["w_out"] = _mm(sv["merged"], dh1b, ta=True, name="mm_out_dw")
    rg_win = (proj, RH * RDV, OFF_RG // (RH * RDV))
    ga_win, gr_win = (proj, D, OFF_GA // D), (proj, D, OFF_GR // D)
    dap, drp, d_rest = _rowwise(_f_gate_bwd, [d_merged, sv["ap"], sv["rp"], ga_win, gr_win], [],
                                [(D, BF16), (D, BF16), (2 * D, BF16, N_REST, OFF_GA // (2 * D), None)], [], name="gate_bwd")
    d_attn = _mm(dap, w["w_attn_proj"], tb=True, name="mm_attn_proj_dx", out_dtype=BF16)
    gr["w_attn_proj"] = _mm(sv["attn"], dap, ta=True, name="mm_attn_proj_dw")
    d_r = _mm(drp, w["w_ret_proj"], tb=True, name="mm_ret_proj_dx", out_dtype=BF16, tn=2048)
    gr["w_ret_proj"] = _mm(sv["r"], drp, ta=True, name="mm_ret_proj_dw", tm=2048)
    token = call("early_grads", gr)
    d_ret, d_rest, gr["ret_gn_g"], gr["ret_gn_b"] = _rowwise(
        _after(_f_gn_bwd, token), [d_r, sv["ret_raw"], rg_win], [w["ret_gn_g"], w["ret_gn_b"]] + held(token),
        [(RH * RDV, BF16), (RH * RDV, BF16, N_REST, OFF_RG // (RH * RDV), d_rest)],
        [(1, RH * RDV), (1, RH * RDV)], name="gn_bwd", tm=256)
    d_rest = _ret_bwd(proj, cos, sin, rconsts, sv["states"], d_ret, d_rest)
    token = call("after_ret", d_rest)
    (delta,) = _rowwise(_after(_f_delta, token), [d_attn, sv["attn"]], held(token), [(HD, F32)], [], name="attn_delta")
    gw_in, dqkvs = None, []
    for g, dil in enumerate(DILATIONS):
        dqkvs.append(_attn_bwd(sv["qkvs"][g], _to_head_residues(d_attn, dil), _to_residues(sv["lse"], dil),
                               _to_residues(delta, dil), dil, f"attn_bwd_g{g}"))
        gw_in = _qkv_dw(dqkvs[g], _x_view(sv["xb"], dil), g, dil, f"mm_qkv{g}_dw", into=gw_in)
    gw_in = _mm(d_rest, sv["xb"], ta=True, name="mm_proj_dw", out_rows=(N_ATT, N_IN), into=gw_in, blocks8=True)
    gr["w_in"] = gw_in.reshape(N_IN, D)
    token = call("w_in_ready", gr["w_in"])
    dx0 = _mm(d_rest, w["win"], b_rows=(N_ATT, N_REST), name="mm_proj_dx", add=dh1, add_scale=ALPHA, after=token)
    dx_parts = []
    for g, dil in enumerate(DILATIONS):
        if dil == 1:
            dx0 = _qkv_dx(dqkvs[g], w["win"], g, dil, f"mm_qkv{g}_dx", F32, add=dx0)
            token = call("after_dx0", dx0)
        else:
            dx_parts.append(_qkv_dx(dqkvs[g], w["win"], g, dil, f"mm_qkv{g}_dx", BF16, after=token).reshape(dx0.shape))
    return [dx0] + dx_parts, gr


def _local_step(x, p, positions, target, ws, own_hooks=None, on_grads=None):
    half = RDK // 2
    freq = jnp.power(ROPE_BASE, -jnp.arange(half, dtype=F32) / half)
    ang = positions.astype(F32)[:, None] * freq[None, :]
    cos, sin = jnp.cos(ang), jnp.sin(ang)
    rconsts = _ret_consts()
    xb = x.astype(BF16)
    saved, ws = [], list(ws)
    for l in range(DEPTH):
        first, late = ws[l] if isinstance(ws[l], tuple) else (ws[l], None)
        if callable(first):
            first = first(x)
        x, xb, sv, ws[l] = _layer_fwd(x, xb, p[l], first, cos, sin, rconsts, late)
        saved.append(sv)
    dy, loss_vec = _rowwise(_f_loss, [x, target], [], [(D, F32)], [(1, D)], name="loss")
    dys, grads = [dy], [None] * DEPTH
    from_above = {}
    for l in reversed(range(DEPTH)):
        hooks = {**from_above, **(own_hooks(l) if own_hooks else {})}
        dys, grads[l] = _layer_bwd(dys, ws[l], saved[l], cos, sin, rconsts, hooks)
        from_above = on_grads(l, grads[l]) if on_grads else {}
    (grad_x,) = _rowwise(_f_sum, dys, [], [(D, F32)], [], name="grad_x_sum")
    return loss_vec, grad_x, grads


def _pack_rows(arrs):
    parts, where, off = [], [], 0
    for t in arrs:
        t = t.reshape(-1, D)
        rows = t.shape[0]
        padded = -(-rows // 8) * 8
        parts.append(jnp.pad(t, ((0, padded - rows), (0, 0))))
        where.append((off, rows))
        off += padded
    return jnp.concatenate(parts, axis=0), where


FIRST = ("w_in",)
LATER = tuple(n for n in BIG if n not in FIRST)


def _first_weights(g, l, W):
    w = dict(win=g["w_in"].reshape(N_IN, D))
    for n in ("ret_gn_g", "ret_gn_b", "ln1_g", "ln1_b", "ln2_g", "ln2_b"):
        w[n] = W[n][l][None, :]
    return w


def _later_weights(g, l, conv_w_all, conv_b):
    w = dict(w_up=g["w_up"].reshape(2 * DFF, D), w_ple_proj=g["w_ple_proj"].reshape(D, PLE),
             w_attn_proj=g["w_attn_proj"].reshape(D, D), w_ret_proj=g["w_ret_proj"].reshape(RH * RDV, D),
             w_out=g["w_out"].reshape(D, D), w_down=g["w_down"].reshape(DFF, D), w_ple_gate=g["w_ple_gate"].reshape(D, D))
    w["conv_wg"], w["conv_wu"] = conv_w_all[l][:, :DFF], conv_w_all[l][:, DFF:]
    w["conv_bg"], w["conv_bu"] = conv_b[l][None, :DFF], conv_b[l][None, DFF:]
    return w


def _layer_weights(g, l, conv_w_all, conv_b, W):
    return {**_first_weights(g, l, W), **_later_weights(g, l, conv_w_all, conv_b)}


def kernel(x, p, positions, w_in, w_attn_proj, w_ret_proj, ret_gn_g, ret_gn_b, w_out, ln1_g, ln1_b, w_up, conv_w, conv_b, w_down, w_ple_gate, w_ple_proj, ln2_g, ln2_b, loss_target, m_w_in, m_w_attn_proj, m_w_ret_proj, m_ret_gn_g, m_ret_gn_b, m_w_out, m_ln1_g, m_ln1_b, m_w_up, m_conv_w, m_conv_b, m_w_down, m_w_ple_gate, m_w_ple_proj, m_ln2_g, m_ln2_b, v_w_in, v_w_attn_proj, v_w_ret_proj, v_ret_gn_g, v_ret_gn_b, v_w_out, v_ln1_g, v_ln1_b, v_w_up, v_conv_w, v_conv_b, v_w_down, v_w_ple_gate, v_w_ple_proj, v_ln2_g, v_ln2_b):
    W = dict(w_in=w_in, w_attn_proj=w_attn_proj, w_ret_proj=w_ret_proj, ret_gn_g=ret_gn_g, ret_gn_b=ret_gn_b, w_out=w_out,
             ln1_g=ln1_g, ln1_b=ln1_b, w_up=w_up, conv_w=conv_w, conv_b=conv_b, w_down=w_down, w_ple_gate=w_ple_gate,
             w_ple_proj=w_ple_proj, ln2_g=ln2_g, ln2_b=ln2_b)
    M = dict(w_in=m_w_in, w_attn_proj=m_w_attn_proj, w_ret_proj=m_w_ret_proj, ret_gn_g=m_ret_gn_g, ret_gn_b=m_ret_gn_b,
             w_out=m_w_out, ln1_g=m_ln1_g, ln1_b=m_ln1_b, w_up=m_w_up, conv_w=m_conv_w, conv_b=m_conv_b, w_down=m_w_down,
             w_ple_gate=m_w_ple_gate, w_ple_proj=m_w_ple_proj, ln2_g=m_ln2_g, ln2_b=m_ln2_b)
    V = dict(w_in=v_w_in, w_attn_proj=v_w_attn_proj, w_ret_proj=v_w_ret_proj, ret_gn_g=v_ret_gn_g, ret_gn_b=v_ret_gn_b,
             w_out=v_w_out, ln1_g=v_ln1_g, ln1_b=v_ln1_b, w_up=v_w_up, conv_w=v_conv_w, conv_b=v_conv_b, w_down=v_w_down,
             w_ple_gate=v_w_ple_gate, w_ple_proj=v_w_ple_proj, ln2_g=v_ln2_g, ln2_b=v_ln2_b)

    me = 4 * lax.axis_index("x") + 2 * lax.axis_index("y") + lax.axis_index("c")
    shard = lambda n, l: (W[n][l].T if n in COL_SHARDED else W[n][l]).astype(BF16)
    landing = lambda ts: [lax.dynamic_update_index_in_dim(lax.empty((N_DEV,) + t.shape, t.dtype), t, me, 0) for t in ts]
    first0 = _gather_many([shard(n, 0) for n in FIRST], "gather_first_l0")
    later0 = [shard(n, 0) for n in LATER] + [conv_w]
    flight0, token0 = _exchange_start("gather", later0, landing(later0), "gather_later_l0_start", after=first0[0])
    all1 = [shard(n, 1) for n in BIG]
    flight1, token1 = _exchange_start("gather", all1, landing(all1), "gather_weights_l1_start", after=token0)
    conv_w_all = []

    def later_first_layer(after):
        _, got = _exchange_wait(flight0, after, "gather_later_l0_wait")
        conv_w_all.append(got[-1].transpose(1, 2, 0, 3).reshape(DEPTH, 3, 2 * DFF))
        return _later_weights(dict(zip(LATER, got)), 0, conv_w_all[0], conv_b)

    def second_layer(after):
        _, got = _exchange_wait(flight1, after, "gather_weights_l1_wait")
        return _layer_weights(dict(zip(BIG, got)), 1, conv_w_all[0], conv_b, W)

    core = lax.axis_index("c").astype(jnp.int32).reshape(1)
    chip = (2 * lax.axis_index("x") + lax.axis_index("y")).astype(jnp.int32).reshape(1)
    empty_like = lambda ts, slots: [lax.empty((slots,) + t.shape[1:], t.dtype) for t in ts]
    chip32, far = [{} for _ in range(DEPTH)], [{} for _ in range(DEPTH)]
    pending = []

    def reduction(l, names, tag):
        state = {}

        def start(g):
            mine = [g[n].reshape((N_DEV, -1) + g[n].shape[1:]) for n in names]
            state["cores"], token = _exchange_start("cores", mine, empty_like(mine, 4), f"exchange_cores_{tag}_start")
            return token

        def onward(after):
            mine, theirs = _exchange_wait(state["cores"], after, f"exchange_cores_{tag}_wait")
            sums = [_pair_sum(a, b, core, f"pair_sum_l{l}_{n}") for a, b, n in zip(mine, theirs, names)]
            for n, s in zip(names, sums):
                chip32[l][n] = s[0]
            sent = [s[0 if n in F32_OVER_ICI else 1] for s, n in zip(sums, names)]
            flight, token = _exchange_start("chips", sent, empty_like(sent, 3), f"exchange_chips_{tag}_start")
            pending.append((l, names, flight, tag))
            return token

        return start, onward

    def on_grads(l, g):
        if l == 0:
            return {}
        start, onward = reduction(l, BIG, f"l{l}")
        return dict(token=start(g), after_ffn=onward)

    def own_hooks(l):
        if l != 0:
            return {}
        start_e, onward_e = reduction(0, LATER, "l0_later")
        start_w, onward_w = reduction(0, FIRST, "l0_first")
        return dict(early_grads=start_e, after_ret=onward_e, w_in_ready=lambda gw: start_w({"w_in": gw}), after_dx0=onward_w)

    ws = [(_first_weights(dict(zip(FIRST, first0)), 0, W), later_first_layer), second_layer]
    loss_vec, grad_x, grads = _local_step(x[0] + token1[0, 0], p[:, 0], positions[0], loss_target[0], ws, own_hooks, on_grads)
    loss = lax.psum(jnp.sum(loss_vec), ("x", "y", "c"))
    for l, names, flight, tag in pending:
        _, got = _exchange_wait(flight, grad_x, f"exchange_chips_{tag}_wait")
        far[l].update(zip(names, got))
    G, DW, NM, NV = ({} for _ in range(4))
    for n in BIG:
        chip32_n = [chip32[l][n] for l in range(DEPTH)]
        far_n = [far[l][n] for l in range(DEPTH)]
        if n in COL_SHARDED:
            G[n] = _reduce_tail(chip32_n, far_n, chip, f"reduced_{n}")[0].transpose(0, 2, 1)
            R2, C2 = DEPTH * W[n].shape[1], W[n].shape[2]
            res = _adamw(*(t.reshape(R2, C2) for t in (G[n], W[n], M[n], V[n])), f"adamw_{n}")
            DW[n], NM[n], NV[n] = (t.reshape(W[n].shape) for t in res)
        else:
            G[n], DW[n], NM[n], NV[n] = _reduce_tail(chip32_n, far_n, chip, f"adamw_{n}", wmv=(W[n], M[n], V[n]))

    small_names = SMALL + ("conv_w",)
    g_small, where = _pack_rows([jnp.stack([grads[l][n] for l in range(DEPTH)]) for n in small_names])
    (g_all,) = _gather_many([g_small], "gather_small_grads")
    g_small = _sum_slots(g_all, "sum_small_grads")
    for n, (off, rows) in zip(SMALL, where):
        G[n] = g_small[off:off + rows].reshape(W[n].shape)
    off, rows = where[-1]
    g_cw = g_small[off:off + rows].reshape(DEPTH, 3, N_DEV, conv_w.shape[2])
    G["conv_w"] = lax.dynamic_index_in_dim(g_cw, me, axis=2, keepdims=False)
    packed = [_pack_rows([d[n] for n in SMALL]) for d in (G, W, M, V)]
    small_out = _adamw(*(t for t, _ in packed), "adamw_small")
    for res, dst in zip(small_out, (DW, NM, NV)):
        for n, (off, rows) in zip(SMALL, packed[0][1]):
            dst[n] = res[off:off + rows].reshape(W[n].shape)
    two_d = lambda t: t.reshape(DEPTH * 3, conv_w.shape[2])
    cw_out = _adamw(two_d(G["conv_w"]), two_d(conv_w), two_d(m_conv_w), two_d(v_conv_w), "adamw_conv_w")
    for res, dst in zip(cw_out, (DW, NM, NV)):
        dst["conv_w"] = res.reshape(conv_w.shape)

    return (loss, grad_x[None], *[G[n] for n in WEIGHTS], *[DW[n] for n in WEIGHTS], *[NM[n] for n in WEIGHTS],
            *[NV[n] for n in WEIGHTS])
```

```python
import math

import numpy as np
import jax
import jax.numpy as jnp
from jax import lax
from jax.experimental import pallas as pl
from jax.experimental.pallas import tpu as pltpu

F32, BF16 = jnp.float32, jnp.bfloat16

D = 1024
DEPTH = 2
N_DEV = 8
HD = 128
NH = 8
DILATIONS = (1, 4, 16)
SPAN = 128
N_ATT = 3 * 3 * NH * HD
RH, RDK, RDV = 4, 256, 512
CH = 128
DFF = 2816
PLE = 256
N_IN = 17408
N_REST = N_IN - N_ATT
OFF_RQ, OFF_RK, OFF_RV, OFF_RG, OFF_GA, OFF_GR = 0, 1024, 2048, 4096, 6144, 7168
ALPHA = (2 * DEPTH) ** 0.25
LN_EPS, GN_EPS = 1e-5, 1e-6
ROPE_BASE = 10000.0
LR, B1, B2, EPS, WD, STEP = 0.001, 0.9, 0.999, 1e-8, 0.01, 10
VMEM_LIMIT = 48 * 1024 * 1024
NEG = -1e30

BIG = ("w_in", "w_attn_proj", "w_ret_proj", "w_out", "w_up", "w_down", "w_ple_gate", "w_ple_proj")
COL_SHARDED = ("w_in", "w_up", "w_ple_proj")
F32_OVER_ICI = ("w_attn_proj", "w_out", "w_ple_gate", "w_ple_proj")
SMALL = ("ret_gn_g", "ret_gn_b", "ln1_g", "ln1_b", "conv_b", "ln2_g", "ln2_b")
WEIGHTS = ("w_in", "w_attn_proj", "w_ret_proj", "ret_gn_g", "ret_gn_b", "w_out", "ln1_g", "ln1_b", "w_up",
           "conv_w", "conv_b", "w_down", "w_ple_gate", "w_ple_proj", "ln2_g", "ln2_b")


def _tile(n, cap, mult=128):
    if n <= cap:
        return n
    t = (cap // mult) * mult
    while n % t:
        t -= mult
    return t


def _cparams(sem):
    return pltpu.CompilerParams(dimension_semantics=sem, vmem_limit_bytes=VMEM_LIMIT)


def _dot(a, b, ca, cb):
    return lax.dot_general(a, b, (((ca,), (cb,)), ((), ())), preferred_element_type=F32)


def _bdot(a, b, ca, cb):
    return lax.dot_general(a, b, (((ca,), (cb,)), ((0,), (0,))), preferred_element_type=F32)


def _mm(a, b, *, name, ta=False, tb=False, out_dtype=F32, add=None, add_scale=1.0, tm=1024, tn=1024, tk=1024,
        b_rows=None, out_rows=None, into=None, blocks8=False, after=None):
    M, K = (a.shape[1], a.shape[0]) if ta else a.shape
    b_first, b_count = b_rows if b_rows else (0, b.shape[0])
    N = b_count if tb else b.shape[1]
    assert K == (b.shape[1] if tb else b_count)
    tm, tn, tk = _tile(M, tm), _tile(N, tn), _tile(K, tk)
    nk = K // tk
    o_first, o_total = out_rows if out_rows else (0, M)
    jb, kb, io = (b_first // tn, 0, o_first // tm) if tb else (0, b_first // tk, o_first // tm)
    assert b_first % (tn if tb else tk) == 0 and o_first % tm == 0 and (add is None or out_rows is None)

    def body(*refs):
        if add is None:
            a_ref, b_ref = refs[:2]
        else:
            a_ref, b_ref, add_ref = refs[:3]
        o_ref, acc_ref = refs[-2:]
        k = pl.program_id(2)

        @pl.when(k == 0)
        def _():
            acc_ref[...] = jnp.zeros_like(acc_ref)

        acc_ref[...] += _dot(a_ref[...].astype(BF16), b_ref[...].astype(BF16), 0 if ta else 1, 1 if tb else 0)

        @pl.when(k == nk - 1)
        def _():
            r = acc_ref[...]
            if add is not None:
                r = r + add_scale * add_ref[...].astype(F32)
            o_ref[...] = r.astype(out_dtype).reshape(o_ref.shape)

    a_spec = pl.BlockSpec((tk, tm), lambda i, j, k: (k, i)) if ta else pl.BlockSpec((tm, tk), lambda i, j, k: (i, k))
    if tb:
        b_spec = pl.BlockSpec((tn, tk), lambda i, j, k: (j + jb, k))
    else:
        b_spec = pl.BlockSpec((tk, tn), lambda i, j, k: (k + kb, j))
    if blocks8:
        assert tm == 1024
        o_spec = pl.BlockSpec((1, 8, 128, tn), lambda i, j, k: (i + io, 0, 0, j))
        o_shape = (o_total // tm, 8, 128, N)
    else:
        o_spec = pl.BlockSpec((tm, tn), lambda i, j, k: (i + io, j))
        o_shape = (o_total, N)
    in_specs, args, aliases = [a_spec, b_spec], [a, b], {}
    if add is not None:
        in_specs.append(o_spec)
        args.append(add)
    if after is not None:
        in_specs.append(pl.BlockSpec(memory_space=pl.ANY))
        args.append(after)
    if into is not None:
        aliases = {len(args): 0}
        in_specs.append(pl.BlockSpec(memory_space=pl.ANY))
        args.append(into)
    return pl.pallas_call(
        body, name=name, grid=(M // tm, N // tn, nk), in_specs=in_specs, out_specs=o_spec,
        out_shape=jax.ShapeDtypeStruct(o_shape, out_dtype), scratch_shapes=[pltpu.VMEM((tm, tn), F32)],
        input_output_aliases=aliases, compiler_params=_cparams(("parallel", "parallel", "arbitrary")),
    )(*args)


def _rowwise(fn, rows, pars, outs, accs, *, name, tm=512):
    first = rows[0][0] if isinstance(rows[0], tuple) else rows[0]
    S = first.shape[-2]
    tm = _tile(S, tm, 16)
    n_r, n_p, n_o = len(rows), len(pars), len(outs)
    outs = [o if len(o) == 5 else (o[0], o[1], o[0], 0, None) for o in outs]
    intos = [(k, o[4]) for k, o in enumerate(outs) if o[4] is not None]
    n_i = len(intos)

    def body(*refs):
        i = pl.program_id(0)
        vals = [r[...] for r in refs[:n_r + n_p]]
        res = fn(*vals)
        if not isinstance(res, (tuple, list)):
            res = (res,)
        o_refs = refs[n_r + n_p + n_i:n_r + n_p + n_i + n_o]
        a_refs = refs[n_r + n_p + n_i + n_o:]
        for r, v in zip(o_refs, res[:n_o]):
            r[...] = v.astype(r.dtype)
        if a_refs:
            @pl.when(i == 0)
            def _():
                for r in a_refs:
                    r[...] = jnp.zeros_like(r)

            for r, v in zip(a_refs, res[n_o:]):
                r[...] += v

    in_specs, args = [], []
    for r in rows:
        if isinstance(r, tuple):
            arr, w, cb = r
            in_specs.append(pl.BlockSpec((tm, w), lambda i, cb=cb: (i, cb)))
        elif r.ndim == 3:
            arr = r
            in_specs.append(pl.BlockSpec((arr.shape[0], tm, arr.shape[2]), lambda i: (0, i, 0)))
        else:
            arr = r
            in_specs.append(pl.BlockSpec((tm, arr.shape[1]), lambda i: (i, 0)))
        args.append(arr)
    for p_ in pars:
        in_specs.append(pl.BlockSpec(p_.shape, lambda i: (0, 0)))
        args.append(p_)
    aliases = {}
    for k, arr in intos:
        aliases[len(args)] = k
        in_specs.append(pl.BlockSpec(memory_space=pl.ANY))
        args.append(arr)
    out_shape = [jax.ShapeDtypeStruct((S, o[2]), o[1]) for o in outs] + [jax.ShapeDtypeStruct(a, F32) for a in accs]
    out_specs = [pl.BlockSpec((tm, o[0]), lambda i, cb=o[3]: (i, cb)) for o in outs] + [pl.BlockSpec(a, lambda i: (0, 0)) for a in accs]
    return pl.pallas_call(
        body, name=name, grid=(S // tm,), in_specs=in_specs, out_specs=out_specs, out_shape=out_shape,
        input_output_aliases=aliases, compiler_params=_cparams(("arbitrary",) if accs else ("parallel",)),
    )(*args)


def _norm(h, eps):
    mu = jnp.mean(h, -1, keepdims=True)
    d = h - mu
    rstd = lax.rsqrt(jnp.mean(d * d, -1, keepdims=True) + eps)
    return d * rstd, rstd


def _norm_bwd(dxh, xh, rstd):
    return rstd * (dxh - jnp.mean(dxh, -1, keepdims=True) - xh * jnp.mean(dxh * xh, -1, keepdims=True))


def _sig(x):
    return 1.0 / (1.0 + jnp.exp(-x))


_GELU_C = math.sqrt(2.0 / math.pi)


def _gelu(x):
    t = jnp.tanh(_GELU_C * (x + 0.044715 * x * x * x))
    return 0.5 * x * (1.0 + t), t


def _gelu_grad(x, t):
    return 0.5 * (1.0 + t) + 0.5 * x * (1.0 - t * t) * _GELU_C * (1.0 + 3 * 0.044715 * x * x)


def _f_ln1(x, mix, g, b):
    h = ALPHA * x + mix.astype(F32)
    xh, _ = _norm(h, LN_EPS)
    y = xh * g + b
    return h, y, y


def _f_ln2(x, ffn, z, pp, g, b):
    h = ALPHA * x + ffn.astype(F32) + _sig(z.astype(F32)) * pp.astype(F32)
    xh, _ = _norm(h, LN_EPS)
    y = xh * g + b
    return h, y, y


def _f_ln_bwd(*args):
    *dys, h, g = args
    dy = dys[0]
    for t in dys[1:]:
        dy = dy + t
    xh, rstd = _norm(h, LN_EPS)
    dh = _norm_bwd(dy * g, xh, rstd)
    return dh, dh, jnp.sum(dy * xh, 0, keepdims=True), jnp.sum(dy, 0, keepdims=True)


def _f_sum(*ts):
    r = ts[0]
    for t in ts[1:]:
        r = r + t
    return r


def _f_loss(y, t):
    e = y - t
    return e * (1.0 / D), jnp.sum(e * e, 0, keepdims=True) * (0.5 / D)


def _head_col(c, h):
    lane = lax.broadcasted_iota(jnp.int32, c.shape, 1)
    return jnp.sum(jnp.where(lane == h, c, 0.0), -1, keepdims=True)


def _f_combine(o0, o1, o2, l0, l1, l2):
    lane = lax.broadcasted_iota(jnp.int32, l0.shape, 1)
    parts, lse = [], jnp.zeros(l0.shape, F32)
    for h in range(NH):
        a0, a1, a2 = _head_col(l0, h), _head_col(l1, h), _head_col(l2, h)
        m = jnp.maximum(jnp.maximum(a0, a1), a2)
        e0, e1, e2 = jnp.exp(a0 - m), jnp.exp(a1 - m), jnp.exp(a2 - m)
        den = e0 + e1 + e2
        parts.append((e0 * o0[h].astype(F32) + e1 * o1[h].astype(F32) + e2 * o2[h].astype(F32)) / den)
        lse = jnp.where(lane == h, m + jnp.log(den), lse)
    return jnp.concatenate(parts, axis=1), lse


def _f_delta(da, a):
    lane = lax.broadcasted_iota(jnp.int32, (da.shape[0], HD), 1)
    out = jnp.zeros((da.shape[0], HD), F32)
    for h in range(NH):
        sl = slice(h * HD, (h + 1) * HD)
        s = jnp.sum(da[:, sl].astype(F32) * a[:, sl].astype(F32), -1, keepdims=True)
        out = jnp.where(lane == h, s, out)
    return out


def _f_gate(ap, rp, ga, gr):
    return _sig(ga.astype(F32)) * ap.astype(F32) + _sig(gr.astype(F32)) * rp.astype(F32)


def _f_gate_bwd(dm, ap, rp, ga, gr):
    dm = dm.astype(F32)
    sa, sr = _sig(ga.astype(F32)), _sig(gr.astype(F32))
    dga, dgr = dm * ap.astype(F32) * sa * (1.0 - sa), dm * rp.astype(F32) * sr * (1.0 - sr)
    return dm * sa, dm * sr, jnp.concatenate([dga, dgr], axis=1)


def _f_gn(y, rg, g, b):
    y, rg = y.astype(F32), rg.astype(F32)
    parts = []
    for h in range(RH):
        sl = slice(h * RDV, (h + 1) * RDV)
        xh, _ = _norm(y[:, sl], GN_EPS)
        parts.append(xh * g[:, sl] + b[:, sl])
    return rg * _sig(rg) * jnp.concatenate(parts, axis=1)


def _f_gn_bwd(dr, y, rg, g, b):
    dr, y, rg = dr.astype(F32), y.astype(F32), rg.astype(F32)
    s = _sig(rg)
    d_out = dr * rg * s
    dys, outs, xhs = [], [], []
    for h in range(RH):
        sl = slice(h * RDV, (h + 1) * RDV)
        xh, rstd = _norm(y[:, sl], GN_EPS)
        xhs.append(xh)
        outs.append(xh * g[:, sl] + b[:, sl])
        dys.append(_norm_bwd(d_out[:, sl] * g[:, sl], xh, rstd))
    xh, out = jnp.concatenate(xhs, axis=1), jnp.concatenate(outs, axis=1)
    d_rg = dr * out * s * (1.0 + rg * (1.0 - s))
    return jnp.concatenate(dys, axis=1), d_rg, jnp.sum(d_out * xh, 0, keepdims=True), jnp.sum(d_out, 0, keepdims=True)


def _f_ln2_bwd(*args):
    *dys, h, z, pp, g = args
    dh, dhb, dg, db = _f_ln_bwd(*dys, h, g)
    s = _sig(z.astype(F32))
    return dh, dhb, dh * s, dh * pp.astype(F32) * s * (1.0 - s), dg, db


QKV = 3 * HD


def _to_tokens(t, d):
    if d == 1:
        return t
    *lead, S, C = t.shape
    n = len(lead)
    perm = tuple(range(n)) + (n + 1, n, n + 2)
    return t.reshape(*lead, d, S // d, C).transpose(perm).reshape(*lead, S, C)


def _to_residues(t, d):
    if d == 1:
        return t
    S, C = t.shape
    return t.reshape(S // d, d, C).transpose(1, 0, 2).reshape(S, C)


def _to_head_residues(t, d):
    S = t.shape[0]
    return t.reshape(S // d, d, NH, HD).transpose(2, 1, 0, 3).reshape(NH, S, HD)


def _w_qkv_specs(g):
    return [pl.BlockSpec((D, D), lambda *i, t=t: (3 * g + t, 0)) for t in range(3)]


def _qkv_fwd(xv, win, g, dil, name):
    Sd = xv.shape[0]
    S = Sd * dil
    tm = min(512, Sd)
    nma = Sd // tm

    def body(a_ref, wq_ref, wk_ref, wv_ref, o_ref):
        a = a_ref[...]
        q, k, v = (_dot(a, w_ref[...], 1, 1).astype(BF16) for w_ref in (wq_ref, wk_ref, wv_ref))
        for h in range(NH):
            sl = slice(h * HD, (h + 1) * HD)
            o_ref[h] = jnp.concatenate([q[:, sl], k[:, sl], v[:, sl]], axis=1)

    return pl.pallas_call(
        body, name=name, grid=(S // tm,),
        in_specs=[pl.BlockSpec((tm, D), lambda i: (i % nma, i // nma))] + _w_qkv_specs(g),
        out_specs=pl.BlockSpec((NH, tm, QKV), lambda i: (0, i, 0)), out_shape=jax.ShapeDtypeStruct((NH, S, QKV), BF16),
        compiler_params=_cparams(("parallel",)),
    )(xv, win, win, win)


def _qkv_dx(dqkv, win, g, dil, name, out_dtype, add=None, after=None):
    S = dqkv.shape[1]
    Sd = S // dil
    tm = min(512, Sd)
    nmo = Sd // tm

    def body(*refs):
        a_ref, wq_ref, wk_ref, wv_ref = refs[:4]
        o_ref = refs[-1]
        acc = None
        for t, w_ref in enumerate((wq_ref, wk_ref, wv_ref)):
            d = jnp.concatenate([a_ref[h][:, t * HD:(t + 1) * HD] for h in range(NH)], axis=1)
            part = _dot(d, w_ref[...], 1, 0)
            acc = part if acc is None else acc + part
        if add is not None:
            acc = acc + refs[4][...]
        o_ref[...] = acc.astype(out_dtype)

    o_spec = pl.BlockSpec((tm, D), lambda i: (i % nmo, i // nmo))
    in_specs = [pl.BlockSpec((NH, tm, QKV), lambda i: (0, i, 0))] + _w_qkv_specs(g)
    args = [dqkv, win, win, win]
    if add is not None:
        assert dil == 1
        in_specs.append(o_spec)
        args.append(add)
    if after is not None:
        in_specs.append(pl.BlockSpec(memory_space=pl.ANY))
        args.append(after)
    return pl.pallas_call(
        body, name=name, grid=(S // tm,), in_specs=in_specs, out_specs=o_spec,
        out_shape=jax.ShapeDtypeStruct((Sd, dil * D), out_dtype), compiler_params=_cparams(("parallel",)),
    )(*args)


GW_IN_BLOCKS = (N_IN // D, NH, HD, D)


def _qkv_dw(dqkv, xv, g, dil, name, into=None):
    S = dqkv.shape[1]
    Sd = S // dil
    tk = min(1024, Sd)
    nkb, nk = Sd // tk, S // tk
    hh = NH // 2

    def body(*refs):
        a_ref, b_ref = refs[:2]
        o_ref, acc_ref = refs[-2:]
        k = pl.program_id(1)

        @pl.when(k == 0)
        def _():
            acc_ref[...] = jnp.zeros_like(acc_ref)

        b = b_ref[...]
        for h in range(hh):
            acc_ref[h * QKV:(h + 1) * QKV, :] += _dot(a_ref[h], b, 0, 0)

        @pl.when(k == nk - 1)
        def _():
            for h in range(hh):
                for t in range(3):
                    o_ref[t, h] = acc_ref[h * QKV + t * HD:h * QKV + (t + 1) * HD, :]

    in_specs = [pl.BlockSpec((hh, tk, QKV), lambda j, k: (j, k, 0)), pl.BlockSpec((tk, D), lambda j, k: (k % nkb, k // nkb))]
    args, aliases = [dqkv, xv], {}
    if into is not None:
        aliases = {2: 0}
        in_specs.append(pl.BlockSpec(memory_space=pl.ANY))
        args.append(into)
    return pl.pallas_call(
        body, name=name, grid=(2, nk), in_specs=in_specs,
        out_specs=pl.BlockSpec((3, hh, HD, D), lambda j, k: (g, j, 0, 0)), out_shape=jax.ShapeDtypeStruct(GW_IN_BLOCKS, F32),
        input_output_aliases=aliases, scratch_shapes=[pltpu.VMEM((hh * QKV, D), F32)],
        compiler_params=_cparams(("parallel", "arbitrary")),
    )(*args)


def _band(nb, first_valid, last_valid=None):
    b = lax.broadcasted_iota(jnp.int32, (nb, SPAN, SPAN), 0)
    row = lax.broadcasted_iota(jnp.int32, (nb, SPAN, SPAN), 1)
    col = lax.broadcasted_iota(jnp.int32, (nb, SPAN, SPAN), 2)
    off = jnp.where(b == 0, jnp.where(first_valid, 0, 2 * SPAN), 0)
    if last_valid is not None:
        off = off + jnp.where(b == nb - 1, jnp.where(last_valid, 0, 2 * SPAN), 0)
    return col <= row, col >= row + off


def _attn_tiles(S, dil):
    Sd = S // dil
    T = min(1024, Sd)
    hp = min(NH, max(1, (S // T) * NH // 16))
    return Sd, T, T // SPAN, Sd // T, hp


def _attn_fwd(qkv, dil, name):
    S = qkv.shape[1]
    Sd, T, nsub, nib, hp = _attn_tiles(S, dil)
    scale = HD ** -0.5

    def body(c_ref, p_ref, o_ref, l_ref):
        ib, hb = pl.program_id(1), pl.program_id(2)
        m_cur, m_prev = _band(nsub, ib > 0)
        lane = lax.broadcasted_iota(jnp.int32, (T, HD), 1)

        @pl.when(hb == 0)
        def _():
            l_ref[...] = jnp.zeros_like(l_ref)

        lses = l_ref[...]
        for hh in range(hp):
            blk, hal = c_ref[hh], p_ref[hh]
            q, k, v = blk[:, :HD], blk[:, HD:2 * HD], blk[:, 2 * HD:]
            if nsub > 1:
                kp = jnp.concatenate([hal[:, HD:2 * HD], k[:T - SPAN]], axis=0)
                vp = jnp.concatenate([hal[:, 2 * HD:], v[:T - SPAN]], axis=0)
            else:
                kp, vp = hal[:, HD:2 * HD], hal[:, 2 * HD:]
            q3, k3, v3, kp3, vp3 = (t.reshape(nsub, SPAN, HD) for t in (q, k, v, kp, vp))
            sc = jnp.where(m_cur, _bdot(q3, k3, 2, 2) * scale, NEG)
            sp = jnp.where(m_prev, _bdot(q3, kp3, 2, 2) * scale, NEG)
            m = jnp.maximum(jnp.max(sc, -1, keepdims=True), jnp.max(sp, -1, keepdims=True))
            pc, pp = jnp.exp(sc - m), jnp.exp(sp - m)
            den = jnp.sum(pc, -1, keepdims=True) + jnp.sum(pp, -1, keepdims=True)
            o = (_bdot(pc.astype(BF16), v3, 2, 1) + _bdot(pp.astype(BF16), vp3, 2, 1)) / den
            o_ref[hh] = o.reshape(T, HD).astype(BF16)
            lses = jnp.where(lane == hb * hp + hh, (m + jnp.log(den)).reshape(T, 1), lses)
        l_ref[...] = lses

    cur = pl.BlockSpec((hp, T, QKV), lambda r, ib, h: (h, r * nib + ib, 0))
    prev = pl.BlockSpec((hp, SPAN, QKV), lambda r, ib, h: (h, r * (Sd // SPAN) + jnp.maximum(ib * nsub - 1, 0), 0))
    return pl.pallas_call(
        body, name=name, grid=(dil, nib, NH // hp), in_specs=[cur, prev],
        out_specs=[pl.BlockSpec((hp, T, HD), lambda r, ib, h: (h, r * nib + ib, 0)),
                   pl.BlockSpec((T, HD), lambda r, ib, h: (r * nib + ib, 0))],
        out_shape=[jax.ShapeDtypeStruct((NH, S, HD), BF16), jax.ShapeDtypeStruct((S, HD), F32)],
        compiler_params=_cparams(("parallel", "parallel", "arbitrary")),
    )(qkv, qkv)


def _attn_bwd(qkv, d_attn, lse, delta, dil, name):
    S = qkv.shape[1]
    Sd, T, nsub, nib, hp = _attn_tiles(S, dil)
    scale = HD ** -0.5
    ne = nsub + 1

    def body(c_ref, p_ref, n_ref, do_ref, don_ref, l_ref, ln_ref, dl_ref, dln_ref, o_ref):
        ib, hb = pl.program_id(1), pl.program_id(2)
        _, m_prev = _band(ne, ib > 0, ib < nib - 1)
        m_cur, _ = _band(nsub, True)
        for hh in range(hp):
            h = hb * hp + hh
            blk, hal, nxt = c_ref[hh], p_ref[hh], n_ref[hh]
            q, k, v = blk[:, :HD], blk[:, HD:2 * HD], blk[:, 2 * HD:]
            do = do_ref[hh]
            l, dl = _head_col(l_ref[...], h), _head_col(dl_ref[...], h)
            qe = jnp.concatenate([q, nxt[:, :HD]], axis=0).reshape(ne, SPAN, HD)
            doe = jnp.concatenate([do, don_ref[hh]], axis=0).reshape(ne, SPAN, HD)
            le = jnp.concatenate([l, _head_col(ln_ref[...], h)], axis=0).reshape(ne, SPAN, 1)
            dle = jnp.concatenate([dl, _head_col(dln_ref[...], h)], axis=0).reshape(ne, SPAN, 1)
            kpe = jnp.concatenate([hal[:, HD:2 * HD], k], axis=0).reshape(ne, SPAN, HD)
            vpe = jnp.concatenate([hal[:, 2 * HD:], v], axis=0).reshape(ne, SPAN, HD)
            p = jnp.where(m_prev, jnp.exp(_bdot(qe, kpe, 2, 2) * scale - le), 0.0)
            ds = (p * (_bdot(doe, vpe, 2, 2) - dle)).astype(BF16)
            dq = _bdot(ds, kpe, 2, 1)[:nsub]
            dk = _bdot(ds, qe, 1, 1)[1:]
            dv = _bdot(p.astype(BF16), doe, 1, 1)[1:]
            q3, k3, v3, do3 = (t.reshape(nsub, SPAN, HD) for t in (q, k, v, do))
            l3, dl3 = l.reshape(nsub, SPAN, 1), dl.reshape(nsub, SPAN, 1)
            p = jnp.where(m_cur, jnp.exp(_bdot(q3, k3, 2, 2) * scale - l3), 0.0)
            ds = (p * (_bdot(do3, v3, 2, 2) - dl3)).astype(BF16)
            dq = (dq + _bdot(ds, k3, 2, 1)) * scale
            dk = (dk + _bdot(ds, q3, 1, 1)) * scale
            dv = dv + _bdot(p.astype(BF16), do3, 1, 1)
            o_ref[hh] = jnp.concatenate([t.reshape(T, HD) for t in (dq, dk, dv)], axis=1).astype(BF16)

    nb = Sd // SPAN
    row = lambda r, ib: r * nib + ib
    prow = lambda r, ib: r * nb + jnp.maximum(ib * nsub - 1, 0)
    nrow = lambda r, ib: r * nb + jnp.minimum((ib + 1) * nsub, nb - 1)
    cur3 = pl.BlockSpec((hp, T, QKV), lambda r, ib, h: (h, row(r, ib), 0))
    prev3 = pl.BlockSpec((hp, SPAN, QKV), lambda r, ib, h: (h, prow(r, ib), 0))
    next3 = pl.BlockSpec((hp, SPAN, QKV), lambda r, ib, h: (h, nrow(r, ib), 0))
    cur1 = pl.BlockSpec((hp, T, HD), lambda r, ib, h: (h, row(r, ib), 0))
    next1 = pl.BlockSpec((hp, SPAN, HD), lambda r, ib, h: (h, nrow(r, ib), 0))
    curc = pl.BlockSpec((T, HD), lambda r, ib, h: (row(r, ib), 0))
    nextc = pl.BlockSpec((SPAN, HD), lambda r, ib, h: (nrow(r, ib), 0))
    return pl.pallas_call(
        body, name=name, grid=(dil, nib, NH // hp),
        in_specs=[cur3, prev3, next3, cur1, next1, curc, nextc, curc, nextc], out_specs=cur3,
        out_shape=jax.ShapeDtypeStruct((NH, S, QKV), BF16),
        compiler_params=_cparams(("parallel", "parallel", "parallel")),
    )(qkv, qkv, qkv, d_attn, d_attn, lse, lse, delta, delta)


def _ret_consts():
    lg = np.log1p(-np.exp2(-5.0 - np.arange(RH, dtype=np.float64)))
    idx = np.arange(CH, dtype=np.float64)
    rel = idx[:, None] - idx[None, :]
    intra = np.where(rel >= 0, np.exp(lg[:, None, None] * np.maximum(rel, 0.0)), 0.0)
    qd = np.exp(lg[:, None] * (idx + 1.0))
    kd = np.exp(lg[:, None] * (CH - 1.0 - idx))
    cd = np.exp(lg * CH)
    wide = lambda t: np.broadcast_to(t[:, :, None], (RH, t.shape[1], RDV))
    return (jnp.asarray(intra, F32), jnp.asarray(wide(qd), F32), jnp.asarray(wide(kd), F32),
            jnp.asarray(np.broadcast_to(cd[:, None, None], (RH, 1, RDV)), F32))


def _rot(t, c, s):
    t1, t2 = t[:, :RDK // 2], t[:, RDK // 2:]
    return jnp.concatenate([t1 * c - t2 * s, t1 * s + t2 * c], axis=1)


def _unrot(d, c, s):
    d1, d2 = d[:, :RDK // 2], d[:, RDK // 2:]
    return jnp.concatenate([d1 * c + d2 * s, d2 * c - d1 * s], axis=1)


RCH = 4


def _ret_specs(nmap):
    rows = RCH * CH
    q = pl.BlockSpec((rows, RH * RDK), lambda n: (nmap(n), OFF_RQ // (RH * RDK)))
    k = pl.BlockSpec((rows, RH * RDK), lambda n: (nmap(n), OFF_RK // (RH * RDK)))
    v = pl.BlockSpec((rows, RH * RDV), lambda n: (nmap(n), OFF_RV // (RH * RDV)))
    cs = pl.BlockSpec((rows, RDK // 2), lambda n: (nmap(n), 0))
    dmat = pl.BlockSpec((RH, CH, CH), lambda n: (0, 0, 0))
    dvec = pl.BlockSpec((RH, CH, RDV), lambda n: (0, 0, 0))
    cdv = pl.BlockSpec((RH, 1, RDV), lambda n: (0, 0, 0))
    state = pl.BlockSpec((RH, RCH, RDK, RDV), lambda n: (0, nmap(n), 0, 0))
    out = pl.BlockSpec((rows, RH * RDV), lambda n: (nmap(n), 0))
    return [q, k, v, cs, cs, dmat, dvec, dvec, cdv], state, out


def _ret_fwd(proj, cos, sin, consts):
    S = proj.shape[0]
    nc = S // CH

    def body(q_ref, k_ref, v_ref, c_ref, s_ref, d_ref, qd_ref, kd_ref, cd_ref, o_ref, st_ref, state):
        @pl.when(pl.program_id(0) == 0)
        def _():
            state[...] = jnp.zeros_like(state)

        for ci in range(RCH):
            rows = slice(ci * CH, (ci + 1) * CH)
            c, s = c_ref[rows, :], s_ref[rows, :]
            for h in range(RH):
                qk, vv = slice(h * RDK, (h + 1) * RDK), slice(h * RDV, (h + 1) * RDV)
                qb = _rot(q_ref[rows, qk].astype(F32), c, s).astype(BF16)
                kb = (_rot(k_ref[rows, qk].astype(F32), c, s) * (RDK ** -0.5)).astype(BF16)
                vb = v_ref[rows, vv]
                sb = state[h].astype(BF16)
                st_ref[h, ci] = sb
                a = (_dot(qb, kb, 1, 1) * d_ref[h]).astype(BF16)
                o_ref[rows, vv] = (_dot(a, vb, 1, 0) + _dot(qb, sb, 1, 0) * qd_ref[h]).astype(BF16)
                vk = (vb.astype(F32) * kd_ref[h]).astype(BF16)
                state[h] = cd_ref[h] * state[h] + _dot(kb, vk, 0, 0)

    ins, state_spec, out_spec = _ret_specs(lambda n: n)
    return pl.pallas_call(
        body, name="ret_fwd", grid=(nc // RCH,), in_specs=ins, out_specs=[out_spec, state_spec],
        out_shape=[jax.ShapeDtypeStruct((S, RH * RDV), BF16), jax.ShapeDtypeStruct((RH, nc, RDK, RDV), BF16)],
        scratch_shapes=[pltpu.VMEM((RH, RDK, RDV), F32)],
        compiler_params=_cparams(("arbitrary",)),
    )(proj, proj, proj, cos, sin, *consts)


def _ret_bwd(proj, cos, sin, consts, states, d_ret, d_rest):
    S = proj.shape[0]
    nc = S // CH

    def body(q_ref, k_ref, v_ref, c_ref, s_ref, d_ref, qd_ref, kd_ref, cd_ref, st_ref, do_ref, _, o_ref, dstate):
        @pl.when(pl.program_id(0) == 0)
        def _():
            dstate[...] = jnp.zeros_like(dstate)

        for ci in reversed(range(RCH)):
            rows = slice(ci * CH, (ci + 1) * CH)
            c, s = c_ref[rows, :], s_ref[rows, :]
            for h in range(RH):
                qk, vv = slice(h * RDK, (h + 1) * RDK), slice(h * RDV, (h + 1) * RDV)
                qb = _rot(q_ref[rows, qk].astype(F32), c, s).astype(BF16)
                kb = (_rot(k_ref[rows, qk].astype(F32), c, s) * (RDK ** -0.5)).astype(BF16)
                vb, sb, do = v_ref[rows, vv], st_ref[h, ci], do_ref[rows, vv]
                dmat, qd, kd = d_ref[h], qd_ref[h], kd_ref[h]
                a = (_dot(qb, kb, 1, 1) * dmat).astype(BF16)
                doq = (do.astype(F32) * qd).astype(BF16)
                dsb = dstate[h].astype(BF16)
                vk = (vb.astype(F32) * kd).astype(BF16)
                o_ref[rows, OFF_RV + h * RDV:OFF_RV + (h + 1) * RDV] = (_dot(a, do, 0, 0) + _dot(kb, dsb, 1, 0) * kd).astype(BF16)
                da = (_dot(do, vb, 1, 1) * dmat).astype(BF16)
                dq = _dot(da, kb, 1, 0) + _dot(doq, sb, 1, 1)
                dk = (_dot(da, qb, 0, 0) + _dot(vk, dsb, 1, 1)) * (RDK ** -0.5)
                o_ref[rows, OFF_RQ + h * RDK:OFF_RQ + (h + 1) * RDK] = _unrot(dq, c, s).astype(BF16)
                o_ref[rows, OFF_RK + h * RDK:OFF_RK + (h + 1) * RDK] = _unrot(dk, c, s).astype(BF16)
                dstate[h] = cd_ref[h] * dstate[h] + _dot(qb, doq, 0, 0)

    nsteps = nc // RCH
    rev = lambda n: nsteps - 1 - n
    ins, state_spec, out_spec = _ret_specs(rev)
    return pl.pallas_call(
        body, name="ret_bwd", grid=(nsteps,), in_specs=ins + [state_spec, out_spec, pl.BlockSpec(memory_space=pl.ANY)],
        out_specs=pl.BlockSpec((RCH * CH, OFF_RG), lambda n: (rev(n), 0)),
        out_shape=jax.ShapeDtypeStruct(d_rest.shape, BF16), input_output_aliases={11: 0},
        scratch_shapes=[pltpu.VMEM((RH, RDK, RDV), F32)],
        compiler_params=_cparams(("arbitrary",)),
    )(proj, proj, proj, cos, sin, *consts, states, d_ret, d_rest)


CW = 256
HALO = 16


def _shift_down(v, halo, k):
    rolled = pltpu.roll(v, k, 0)
    hr = pltpu.roll(halo, k, 0)[0:8]
    row = lax.broadcasted_iota(jnp.int32, hr.shape, 0)
    return jnp.concatenate([jnp.where(row < k, hr, rolled[0:8]), rolled[8:]], axis=0)


def _shift_up(v, halo, k):
    T = v.shape[0]
    rolled = pltpu.roll(v, T - k, 0)
    hr = pltpu.roll(halo, 8 - k, 0)[0:8]
    row = lax.broadcasted_iota(jnp.int32, hr.shape, 0)
    return jnp.concatenate([rolled[:T - 8], jnp.where(row >= 8 - k, hr, rolled[T - 8:])], axis=0)


def _conv_taps(h_ref, hp_ref, first):
    h = h_ref[...].astype(F32)
    hp = hp_ref[...].astype(F32) * jnp.where(first, 0.0, 1.0)
    return _shift_down(h, hp, 2), _shift_down(h, hp, 1), h


def _conv_specs(S, T, cw=CW):
    nj = DFF // cw
    cur = pl.BlockSpec((T, cw), lambda j, i: (i, j))
    prev = pl.BlockSpec((HALO, cw), lambda j, i: (jnp.maximum(i * (T // HALO) - 1, 0), j))
    nxt = pl.BlockSpec((HALO, cw), lambda j, i: (jnp.minimum((i + 1) * (T // HALO), S // HALO - 1), j))
    w = pl.BlockSpec((3, cw), lambda j, i: (0, j))
    b = pl.BlockSpec((1, cw), lambda j, i: (0, j))
    return nj, cur, prev, nxt, w, b


def _conv_fwd(hg, hu, wg, wu, bg, bu):
    S = hg.shape[0]
    T = min(1024, S)
    nj, cur, prev, _, w, b = _conv_specs(S, T)

    def body(hg_ref, hu_ref, hgp_ref, hup_ref, wg_ref, wu_ref, bg_ref, bu_ref, o_ref):
        first = pl.program_id(1) == 0
        g2, g1, g0 = _conv_taps(hg_ref, hgp_ref, first)
        u2, u1, u0 = _conv_taps(hu_ref, hup_ref, first)
        cg = wg_ref[0:1, :] * g2 + wg_ref[1:2, :] * g1 + wg_ref[2:3, :] * g0 + bg_ref[...]
        cu = wu_ref[0:1, :] * u2 + wu_ref[1:2, :] * u1 + wu_ref[2:3, :] * u0 + bu_ref[...]
        o_ref[...] = (_gelu(cg)[0] * cu).astype(BF16)

    return pl.pallas_call(
        body, name="conv_fwd", grid=(nj, S // T), in_specs=[cur, cur, prev, prev, w, w, b, b], out_specs=cur,
        out_shape=jax.ShapeDtypeStruct((S, DFF), BF16), compiler_params=_cparams(("parallel", "parallel")),
    )(hg, hu, hg, hu, wg, wu, bg, bu)


def _conv_bwd_pre(d_act, hg, hu, wg, wu, bg, bu):
    S = hg.shape[0]
    T = min(1024, S)
    nj, cur, prev, _, w, b = _conv_specs(S, T)

    def body(da_ref, hg_ref, hu_ref, hgp_ref, hup_ref, wg_ref, wu_ref, bg_ref, bu_ref,
             dcg_ref, dcu_ref, gwg_ref, gwu_ref, gbg_ref, gbu_ref):
        first = pl.program_id(1) == 0
        g2, g1, g0 = _conv_taps(hg_ref, hgp_ref, first)
        u2, u1, u0 = _conv_taps(hu_ref, hup_ref, first)
        cg = wg_ref[0:1, :] * g2 + wg_ref[1:2, :] * g1 + wg_ref[2:3, :] * g0 + bg_ref[...]
        cu = wu_ref[0:1, :] * u2 + wu_ref[1:2, :] * u1 + wu_ref[2:3, :] * u0 + bu_ref[...]
        da = da_ref[...].astype(F32)
        gl, t = _gelu(cg)
        dcg = da * cu * _gelu_grad(cg, t)
        dcu = da * gl
        dcg_ref[...] = dcg.astype(BF16)
        dcu_ref[...] = dcu.astype(BF16)

        @pl.when(first)
        def _():
            for r in (gwg_ref, gwu_ref, gbg_ref, gbu_ref):
                r[...] = jnp.zeros_like(r)

        for r, d, taps in ((gwg_ref, dcg, (g2, g1, g0)), (gwu_ref, dcu, (u2, u1, u0))):
            for j in range(3):
                r[j:j + 1, :] += jnp.sum(d * taps[j], 0, keepdims=True)
        gbg_ref[...] += jnp.sum(dcg, 0, keepdims=True)
        gbu_ref[...] += jnp.sum(dcu, 0, keepdims=True)

    return pl.pallas_call(
        body, name="conv_bwd_pre", grid=(nj, S // T), in_specs=[cur, cur, cur, prev, prev, w, w, b, b],
        out_specs=[cur, cur, w, w, b, b],
        out_shape=[jax.ShapeDtypeStruct((S, DFF), BF16)] * 2 + [jax.ShapeDtypeStruct((3, DFF), F32)] * 2
        + [jax.ShapeDtypeStruct((1, DFF), F32)] * 2,
        compiler_params=_cparams(("parallel", "arbitrary")),
    )(d_act, hg, hu, hg, hu, wg, wu, bg, bu)


def _conv_bwd_in(dc, w, name):
    S = dc.shape[0]
    T = min(512, S)
    nj, cur, _, nxt, wspec, _ = _conv_specs(S, T, DFF // 2)
    nt = S // T

    def body(dc_ref, dn_ref, w_ref, o_ref):
        d = dc_ref[...].astype(F32)
        dn = dn_ref[...].astype(F32) * jnp.where(pl.program_id(1) == nt - 1, 0.0, 1.0)
        o_ref[...] = (w_ref[2:3, :] * d + w_ref[1:2, :] * _shift_up(d, dn, 1) + w_ref[0:1, :] * _shift_up(d, dn, 2)).astype(BF16)

    return pl.pallas_call(
        body, name=name, grid=(nj, nt), in_specs=[cur, nxt, wspec], out_specs=cur,
        out_shape=jax.ShapeDtypeStruct((S, DFF), BF16), compiler_params=_cparams(("parallel", "parallel")),
    )(dc, dc, w)


def _adam_math(g, w, m, v):
    m = B1 * m + (1.0 - B1) * g
    v = B2 * v + (1.0 - B2) * (g * g)
    m_hat = m / (1.0 - B1 ** STEP)
    v_hat = v / (1.0 - B2 ** STEP)
    return -LR * (m_hat / (jnp.sqrt(v_hat) + EPS) + WD * w), m, v


def _reduce_tail(chip32, far, chip, name, wmv=None):
    L = len(chip32)
    _, R, C = chip32[0].shape
    tr = _tile(R, 256, 16)
    nr = R // tr

    def body(chip_ref, *refs):
        own_refs, far_refs, rest = refs[:L], refs[L:2 * L], refs[2 * L:]
        outs = rest[3:] if wmv else rest
        for ll in range(L):
            @pl.when(pl.program_id(0) == ll)
            def _(ll=ll):
                g = own_refs[ll][...]
                for s in range(3):
                    g = g + far_refs[ll][s].astype(F32)
                outs[0][...] = g
                if wmv:
                    outs[1][...], outs[2][...], outs[3][...] = _adam_math(g, rest[0][...], rest[1][...], rest[2][...])

    def rows(ll):
        return lambda l, i: jnp.where(l == ll, i, jnp.where(l < ll, 0, nr - 1))

    blk = pl.BlockSpec((None, tr, C), lambda l, i, ch: (l, i, 0))
    in_specs = [pl.BlockSpec((None, tr, C), lambda l, i, ch, f=rows(ll): (ch[0], f(l, i), 0)) for ll in range(L)]
    in_specs += [pl.BlockSpec((3, tr, C), lambda l, i, ch, f=rows(ll): (0, f(l, i), 0)) for ll in range(L)]
    args = list(chip32) + list(far)
    n_out = 1
    if wmv:
        in_specs += [blk] * 3
        args += list(wmv)
        n_out = 4
    return pl.pallas_call(
        body, name=name,
        grid_spec=pltpu.PrefetchScalarGridSpec(num_scalar_prefetch=1, grid=(L, nr), in_specs=in_specs, out_specs=[blk] * n_out),
        out_shape=[jax.ShapeDtypeStruct((L, R, C), F32)] * n_out, compiler_params=_cparams(("arbitrary", "arbitrary")),
    )(chip, *args)


def _adamw(g, w, m, v, name):
    R, C = g.shape
    tr = _tile(R, 128, 8)

    def body(g_ref, w_ref, m_ref, v_ref, d_ref, nm_ref, nv_ref):
        d_ref[...], nm_ref[...], nv_ref[...] = _adam_math(g_ref[...], w_ref[...], m_ref[...], v_ref[...])

    blk = pl.BlockSpec((tr, C), lambda i: (i, 0))
    return pl.pallas_call(
        body, name=name, grid=(R // tr,), in_specs=[blk] * 4, out_specs=[blk] * 3,
        out_shape=[jax.ShapeDtypeStruct(g.shape, F32)] * 3, compiler_params=_cparams(("parallel",)),
    )(g, w, m, v)


def _pair_sum(x, recv, core, name):
    _, R, C = x.shape
    tr = _tile(R, 600, 16)

    def body(core_ref, x_ref, r_ref, o32_ref, o16_ref):
        s = x_ref[...] + r_ref[...]
        o32_ref[...] = s
        o16_ref[...] = s.astype(BF16)

    blk = pl.BlockSpec((None, tr, C), lambda q, i, c: (q, i, 0))
    mine = pl.BlockSpec((None, None, tr, C), lambda q, i, c: (q, c[0], i, 0))
    return pl.pallas_call(
        body, name=name,
        grid_spec=pltpu.PrefetchScalarGridSpec(num_scalar_prefetch=1, grid=(4, R // tr), in_specs=[mine, blk], out_specs=[blk, blk]),
        out_shape=[jax.ShapeDtypeStruct((4, R, C), F32), jax.ShapeDtypeStruct((4, R, C), BF16)],
        compiler_params=_cparams(("parallel", "parallel")),
    )(core, x.reshape(4, 2, R, C), recv)


def _sum_slots(x, name):
    def body(x_ref, o_ref):
        g = x_ref[0]
        for s in range(1, x.shape[0]):
            g = g + x_ref[s]
        o_ref[...] = g

    return pl.pallas_call(body, name=name, out_shape=jax.ShapeDtypeStruct(x.shape[1:], F32))(x)


MESH = pl.DeviceIdType.MESH
_HBM = pl.BlockSpec(memory_space=pltpu.HBM)


def _dma_sems(n):
    return pltpu.SemaphoreType.DMA((n,))


def _gather_many(xs, name):
    n = len(xs)

    def body(*refs):
        x_refs, out_refs = refs[:n], refs[n:2 * n]
        send_sems, recv_sems, local_sems = refs[2 * n:]
        ax, ay, ac = lax.axis_index("x"), lax.axis_index("y"), lax.axis_index("c")
        me, sibling = (ax, ay, ac), (ax, ay, 1 - ac)
        chips = [(1 - ax, ay), (ax, 1 - ay), (1 - ax, 1 - ay)]

        def copy(a, k, block, to, own=False):
            slot = out_refs[a].at[4 * block[0] + 2 * block[1] + block[2]]
            return pltpu.make_async_remote_copy(
                src_ref=x_refs[a] if own else slot, dst_ref=slot, send_sem=send_sems.at[7 * a + k],
                recv_sem=recv_sems.at[7 * a + k], device_id=to, device_id_type=MESH)

        mine = [pltpu.make_async_copy(x_refs[a], out_refs[a].at[4 * ax + 2 * ay + ac], local_sems.at[a]) for a in range(n)]
        first = [copy(a, 0, me, sibling, own=True) for a in range(n)]
        first += [copy(a, 1 + j, me, (*chip, ac), own=True) for j, chip in enumerate(chips) for a in range(n)]
        for cp in mine + first:
            cp.start()
        passed = []
        for j, chip in enumerate(chips):
            for a in range(n):
                copy(a, 1 + j, (*chip, ac), me).wait_recv()
                cp = copy(a, 4 + j, (*chip, ac), sibling)
                cp.start()
                passed.append(cp)
        for a in range(n):
            copy(a, 0, sibling, me).wait_recv()
            for j, chip in enumerate(chips):
                copy(a, 4 + j, (*chip, 1 - ac), me).wait_recv()
        for cp in first + passed:
            cp.wait_send()
        for cp in mine:
            cp.wait()

    return pl.pallas_call(
        body, name=name, out_shape=[jax.ShapeDtypeStruct((N_DEV,) + x.shape, x.dtype) for x in xs],
        in_specs=[_HBM] * n, out_specs=[_HBM] * n, scratch_shapes=[_dma_sems(7 * n), _dma_sems(7 * n), _dma_sems(n)],
    )(*xs)


_SEM = pl.BlockSpec(memory_space=pltpu.SEMAPHORE)
_EFFECT = pltpu.SideEffectType.DATAFLOW_SIDE_EFFECTING


def _peer(k):
    ax, ay, ac = lax.axis_index("x"), lax.axis_index("y"), lax.axis_index("c")
    px = 1 - ax if k & 4 else ax
    py = 1 - ay if k & 2 else ay
    pc = 1 - ac if k & 1 else ac
    return (px, py, pc), 4 * px + 2 * py + pc


def _build_gather(x_refs, land_refs, send_sems, recv_sems, waiting):
    _, me = _peer(0)
    copies = []
    for a in range(len(x_refs)):
        for k in range(1, N_DEV):
            peer, slot = _peer(k)
            copies.append(pltpu.make_async_remote_copy(
                src_ref=x_refs[a], dst_ref=land_refs[a].at[slot if waiting else me], send_sem=send_sems.at[7 * a + k - 1],
                recv_sem=recv_sems.at[7 * a + k - 1], device_id=peer, device_id_type=MESH))
    return copies


def _build_cores(x_refs, land_refs, send_sems, recv_sems, waiting):
    ax, ay, ac = lax.axis_index("x"), lax.axis_index("y"), lax.axis_index("c")
    copies = []
    for a in range(len(x_refs)):
        for q in range(4):
            copies.append(pltpu.make_async_remote_copy(
                src_ref=x_refs[a].at[2 * q + 1 - ac], dst_ref=land_refs[a].at[q], send_sem=send_sems.at[4 * a + q],
                recv_sem=recv_sems.at[4 * a + q], device_id=(ax, ay, 1 - ac), device_id_type=MESH))
    return copies


def _build_chips(p_refs, land_refs, send_sems, recv_sems, waiting):
    ax, ay, ac = lax.axis_index("x"), lax.axis_index("y"), lax.axis_index("c")
    copies = []
    for a in range(len(p_refs)):
        for k in range(1, 4):
            px = 1 - ax if k & 2 else ax
            py = 1 - ay if k & 1 else ay
            copies.append(pltpu.make_async_remote_copy(
                src_ref=p_refs[a].at[2 * px + py], dst_ref=land_refs[a].at[k - 1], send_sem=send_sems.at[3 * a + k - 1],
                recv_sem=recv_sems.at[3 * a + k - 1], device_id=(px, py, ac), device_id_type=MESH))
    return copies


_EXCHANGES = {"gather": (_build_gather, 7, N_DEV), "cores": (_build_cores, 4, 4), "chips": (_build_chips, 3, 3)}


def _exchange_start(kind, xs, lands, name, after=None):
    build, per, _ = _EXCHANGES[kind]
    n = len(xs)

    def body(*refs):
        for cp in build(refs[:n], refs[n:2 * n], refs[-2 * n - 3], refs[-2 * n - 2], False):
            cp.start()
        refs[-1][...] = jnp.zeros_like(refs[-1])

    hbm = lambda t: pltpu.HBM(t.shape, t.dtype)
    args = [pltpu.with_memory_space_constraint(t, pltpu.HBM) for t in list(xs) + list(lands)]
    in_specs = [_HBM] * (2 * n)
    if after is not None:
        args.append(after)
        in_specs.append(pl.BlockSpec(memory_space=pl.ANY))
    outs = pl.pallas_call(
        body, name=name,
        out_shape=(_dma_sems(per * n), _dma_sems(per * n), *[hbm(t) for t in xs], *[hbm(t) for t in lands],
                   jax.ShapeDtypeStruct((8, 128), F32)),
        in_specs=in_specs, out_specs=(_SEM, _SEM, *[_HBM] * (2 * n), pl.BlockSpec(memory_space=pltpu.VMEM)),
        input_output_aliases={a: 2 + a for a in range(2 * n)},
        compiler_params=pltpu.CompilerParams(has_side_effects=_EFFECT),
    )(*args)
    return (kind, outs[0], outs[1], outs[2:2 + n], outs[2 + n:2 + 2 * n]), outs[-1]


def _exchange_wait(flight, after, name):
    kind, send_sems, recv_sems, xs, lands = flight
    build = _EXCHANGES[kind][0]
    n = len(xs)

    def body(*refs):
        for cp in build(refs[:n], refs[n:2 * n], refs[2 * n], refs[2 * n + 1], True):
            cp.wait_send()
            cp.wait_recv()

    hbm = lambda t: pltpu.HBM(t.shape, t.dtype)
    outs = pl.pallas_call(
        body, name=name, out_shape=(*[hbm(t) for t in xs], *[hbm(t) for t in lands]),
        in_specs=[_HBM] * (2 * n) + [_SEM, _SEM, pl.BlockSpec(memory_space=pl.ANY)], out_specs=[_HBM] * (2 * n),
        input_output_aliases={a: a for a in range(2 * n)}, compiler_params=pltpu.CompilerParams(has_side_effects=_EFFECT),
    )(*xs, *lands, send_sems, recv_sems, after)
    return outs[:n], outs[n:]


def _x_view(xb, d):
    return xb if d == 1 else xb.reshape(xb.shape[0] // d, d * xb.shape[1])


def _layer_fwd(x, xb, p, w, cos, sin, rconsts, late=None):
    proj = _mm(xb, w["win"], tb=True, b_rows=(N_ATT, N_REST), name="mm_proj", out_dtype=BF16)
    qkvs, ogs, lgs = [], [], []
    for g, dil in enumerate(DILATIONS):
        qkv = _qkv_fwd(_x_view(xb, dil), w["win"], g, dil, f"mm_qkv{g}")
        o, l = _attn_fwd(qkv, dil, f"attn_fwd_g{g}")
        qkvs.append(qkv)
        ogs.append(_to_tokens(o, dil))
        lgs.append(_to_tokens(l, dil))
    attn, lse = _rowwise(_f_combine, ogs + lgs, [], [(D, BF16), (HD, F32)], [], name="attn_combine")
    ret_raw, states = _ret_fwd(proj, cos, sin, rconsts)
    rg_win = (proj, RH * RDV, OFF_RG // (RH * RDV))
    ga_win, gr_win = (proj, D, OFF_GA // D), (proj, D, OFF_GR // D)
    (r,) = _rowwise(_f_gn, [ret_raw, rg_win], [w["ret_gn_g"], w["ret_gn_b"]], [(RH * RDV, BF16)], [], name="gn_fwd", tm=256)
    if late is not None:
        w = {**w, **late(r)}
    ap = _mm(attn, w["w_attn_proj"], name="mm_attn_proj", out_dtype=BF16)
    rp = _mm(r, w["w_ret_proj"], name="mm_ret_proj", out_dtype=BF16, tk=2048)
    (merged,) = _rowwise(_f_gate, [ap, rp, ga_win, gr_win], [], [(D, BF16)], [], name="gate_fwd")
    mix = _mm(merged, w["w_out"], name="mm_out", out_dtype=BF16)
    h1, x1, x1b = _rowwise(_f_ln1, [x, mix], [w["ln1_g"], w["ln1_b"]], [(D, F32), (D, F32), (D, BF16)], [], name="ln1_fwd")
    z = _mm(x1b, w["w_ple_gate"], name="mm_ple_gate", out_dtype=BF16)
    pp = _mm(p, w["w_ple_proj"], tb=True, name="mm_ple_proj", out_dtype=BF16)
    hg = _mm(x1b, w["w_up"], tb=True, b_rows=(0, DFF), name="mm_up_g", out_dtype=BF16, tm=512, tn=DFF)
    hu = _mm(x1b, w["w_up"], tb=True, b_rows=(DFF, DFF), name="mm_up_u", out_dtype=BF16, tm=512, tn=DFF)
    act = _conv_fwd(hg, hu, w["conv_wg"], w["conv_wu"], w["conv_bg"], w["conv_bu"])
    ffn = _mm(act, w["w_down"], name="mm_down", tm=512, tk=DFF, out_dtype=BF16)
    h2, x2, x2b = _rowwise(_f_ln2, [x1, ffn, z, pp], [w["ln2_g"], w["ln2_b"]], [(D, F32), (D, F32), (D, BF16)], [], name="ln2_fwd")
    saved = dict(xb=xb, proj=proj, qkvs=qkvs, attn=attn, lse=lse, ret_raw=ret_raw, states=states, r=r, ap=ap, rp=rp,
                 merged=merged, h1=h1, x1b=x1b, z=z, pp=pp, hg=hg, hu=hu, act=act, h2=h2, p=p)
    return x2, x2b, saved, w


def _after(fn, token):
    return fn if token is None else (lambda *a: fn(*a[:-1]))


def _layer_bwd(dys, w, sv, cos, sin, rconsts, hooks):
    gr = {}
    proj = sv["proj"]
    call = lambda key, *a: hooks[key](*a) if key in hooks else None
    held = lambda token: [] if token is None else [token]
    token = hooks.get("token")
    dh2, dh2b, dpp, dz, gr["ln2_g"], gr["ln2_b"] = _rowwise(
        _after(_f_ln2_bwd, token), list(dys) + [sv["h2"], sv["z"], sv["pp"]], [w["ln2_g"]] + held(token),
        [(D, F32), (D, BF16), (D, BF16), (D, BF16)], [(1, D), (1, D)], name="ln2_bwd")
    d_act = _mm(dh2b, w["w_down"], tb=True, name="mm_down_dx", out_dtype=BF16, tm=512, tn=DFF)
    gr["w_down"] = _mm(sv["act"], dh2b, ta=True, name="mm_down_dw", tm=DFF // 2)
    dcg, dcu, gwg, gwu, gbg, gbu = _conv_bwd_pre(d_act, sv["hg"], sv["hu"], w["conv_wg"], w["conv_wu"], w["conv_bg"], w["conv_bu"])
    token = call("after_ffn", dcg)
    gr["conv_w"] = jnp.concatenate([gwg, gwu], axis=1)
    gr["conv_b"] = jnp.concatenate([gbg, gbu], axis=1)
    dhg = _conv_bwd_in(dcg, w["conv_wg"], "conv_bwd_in_g")
    dhu = _conv_bwd_in(dcu, w["conv_wu"], "conv_bwd_in_u")
    gw_up = _mm(dhg, sv["x1b"], ta=True, name="mm_up_g_dw", tm=DFF // 2, out_rows=(0, 2 * DFF))
    gr["w_up"] = _mm(dhu, sv["x1b"], ta=True, name="mm_up_u_dw", tm=DFF // 2, out_rows=(DFF, 2 * DFF), into=gw_up)
    dx1 = _mm(dhg, w["w_up"], b_rows=(0, DFF), name="mm_up_g_dx", add=dh2, add_scale=ALPHA, tm=512, tk=DFF)
    dx1 = _mm(dhu, w["w_up"], b_rows=(DFF, DFF), name="mm_up_u_dx", add=dx1, tm=512, tk=DFF)
    gr["w_ple_proj"] = _mm(dpp, sv["p"], ta=True, name="mm_ple_proj_dw")
    gr["w_ple_gate"] = _mm(sv["x1b"], dz, ta=True, name="mm_ple_gate_dw")
    dx1 = _mm(dz, w["w_ple_gate"], tb=True, name="mm_ple_gate_dx", add=dx1)
    dh1, dh1b, gr["ln1_g"], gr["ln1_b"] = _rowwise(_after(_f_ln_bwd, token), [dx1, sv["h1"]], [w["ln1_g"]] + held(token),
                                                   [(D, F32), (D, BF16)], [(1, D), (1, D)], name="ln1_bwd")
    d_merged = _mm(dh1b, w["w_out"], tb=True, name="mm_out_dx", out_dtype=BF16)
    gr["w_out"] = _mm(sv["merged"], dh1b, ta=True, name="mm_out_dw")
    rg_win = (proj, RH * RDV, OFF_RG // (RH * RDV))
    ga_win, gr_win = (proj, D, OFF_GA // D), (proj, D, OFF_GR // D)
    dap, drp, d_rest = _rowwise(_f_gate_bwd, [d_merged, sv["ap"], sv["rp"], ga_win, gr_win], [],
                                [(D, BF16), (D, BF16), (2 * D, BF16, N_REST, OFF_GA // (2 * D), None)], [], name="gate_bwd")
    d_attn = _mm(dap, w["w_attn_proj"], tb=True, name="mm_attn_proj_dx", out_dtype=BF16)
    gr["w_attn_proj"] = _mm(sv["attn"], dap, ta=True, name="mm_attn_proj_dw")
    d_r = _mm(drp, w["w_ret_proj"], tb=True, name="mm_ret_proj_dx", out_dtype=BF16, tn=2048)
    gr["w_ret_proj"] = _mm(sv["r"], drp, ta=True, name="mm_ret_proj_dw", tm=2048)
    token = call("early_grads", gr)
    d_ret, d_rest, gr["ret_gn_g"], gr["ret_gn_b"] = _rowwise(
        _after(_f_gn_bwd, token), [d_r, sv["ret_raw"], rg_win], [w["ret_gn_g"], w["ret_gn_b"]] + held(token),
        [(RH * RDV, BF16), (RH * RDV, BF16, N_REST, OFF_RG // (RH * RDV), d_rest)],
        [(1, RH * RDV), (1, RH * RDV)], name="gn_bwd", tm=256)
    d_rest = _ret_bwd(proj, cos, sin, rconsts, sv["states"], d_ret, d_rest)
    token = call("after_ret", d_rest)
    (delta,) = _rowwise(_after(_f_delta, token), [d_attn, sv["attn"]], held(token), [(HD, F32)], [], name="attn_delta")
    gw_in, dqkvs = None, []
    for g, dil in enumerate(DILATIONS):
        dqkvs.append(_attn_bwd(sv["qkvs"][g], _to_head_residues(d_attn, dil), _to_residues(sv["lse"], dil),
                               _to_residues(delta, dil), dil, f"attn_bwd_g{g}"))
        gw_in = _qkv_dw(dqkvs[g], _x_view(sv["xb"], dil), g, dil, f"mm_qkv{g}_dw", into=gw_in)
    gw_in = _mm(d_rest, sv["xb"], ta=True, name="mm_proj_dw", out_rows=(N_ATT, N_IN), into=gw_in, blocks8=True)
    gr["w_in"] = gw_in.reshape(N_IN, D)
    token = call("w_in_ready", gr["w_in"])
    dx0 = _mm(d_rest, w["win"], b_rows=(N_ATT, N_REST), name="mm_proj_dx", add=dh1, add_scale=ALPHA, after=token)
    dx_parts = []
    for g, dil in enumerate(DILATIONS):
        if dil == 1:
            dx0 = _qkv_dx(dqkvs[g], w["win"], g, dil, f"mm_qkv{g}_dx", F32, add=dx0)
            token = call("after_dx0", dx0)
        else:
            dx_parts.append(_qkv_dx(dqkvs[g], w["win"], g, dil, f"mm_qkv{g}_dx", BF16, after=token).reshape(dx0.shape))
    return [dx0] + dx_parts, gr


def _local_step(x, p, positions, target, ws, own_hooks=None, on_grads=None):
    half = RDK // 2
    freq = jnp.power(ROPE_BASE, -jnp.arange(half, dtype=F32) / half)
    ang = positions.astype(F32)[:, None] * freq[None, :]
    cos, sin = jnp.cos(ang), jnp.sin(ang)
    rconsts = _ret_consts()
    xb = x.astype(BF16)
    saved, ws = [], list(ws)
    for l in range(DEPTH):
        first, late = ws[l] if isinstance(ws[l], tuple) else (ws[l], None)
        if callable(first):
            first = first(x)
        x, xb, sv, ws[l] = _layer_fwd(x, xb, p[l], first, cos, sin, rconsts, late)
        saved.append(sv)
    dy, loss_vec = _rowwise(_f_loss, [x, target], [], [(D, F32)], [(1, D)], name="loss")
    dys, grads = [dy], [None] * DEPTH
    from_above = {}
    for l in reversed(range(DEPTH)):
        hooks = {**from_above, **(own_hooks(l) if own_hooks else {})}
        dys, grads[l] = _layer_bwd(dys, ws[l], saved[l], cos, sin, rconsts, hooks)
        from_above = on_grads(l, grads[l]) if on_grads else {}
    (grad_x,) = _rowwise(_f_sum, dys, [], [(D, F32)], [], name="grad_x_sum")
    return loss_vec, grad_x, grads


def _pack_rows(arrs):
    parts, where, off = [], [], 0
    for t in arrs:
        t = t.reshape(-1, D)
        rows = t.shape[0]
        padded = -(-rows // 8) * 8
        parts.append(jnp.pad(t, ((0, padded - rows), (0, 0))))
        where.append((off, rows))
        off += padded
    return jnp.concatenate(parts, axis=0), where


FIRST = ("w_in",)
LATER = tuple(n for n in BIG if n not in FIRST)


def _first_weights(g, l, W):
    w = dict(win=g["w_in"].reshape(N_IN, D))
    for n in ("ret_gn_g", "ret_gn_b", "ln1_g", "ln1_b", "ln2_g", "ln2_b"):
        w[n] = W[n][l][None, :]
    return w


def _later_weights(g, l, conv_w_all, conv_b):
    w = dict(w_up=g["w_up"].reshape(2 * DFF, D), w_ple_proj=g["w_ple_proj"].reshape(D, PLE),
             w_attn_proj=g["w_attn_proj"].reshape(D, D), w_ret_proj=g["w_ret_proj"].reshape(RH * RDV, D),
             w_out=g["w_out"].reshape(D, D), w_down=g["w_down"].reshape(DFF, D), w_ple_gate=g["w_ple_gate"].reshape(D, D))
    w["conv_wg"], w["conv_wu"] = conv_w_all[l][:, :DFF], conv_w_all[l][:, DFF:]
    w["conv_bg"], w["conv_bu"] = conv_b[l][None, :DFF], conv_b[l][None, DFF:]
    return w


def _layer_weights(g, l, conv_w_all, conv_b, W):
    return {**_first_weights(g, l, W), **_later_weights(g, l, conv_w_all, conv_b)}


def kernel(x, p, positions, w_in, w_attn_proj, w_ret_proj, ret_gn_g, ret_gn_b, w_out, ln1_g, ln1_b, w_up, conv_w, conv_b, w_down, w_ple_gate, w_ple_proj, ln2_g, ln2_b, loss_target, m_w_in, m_w_attn_proj, m_w_ret_proj, m_ret_gn_g, m_ret_gn_b, m_w_out, m_ln1_g, m_ln1_b, m_w_up, m_conv_w, m_conv_b, m_w_down, m_w_ple_gate, m_w_ple_proj, m_ln2_g, m_ln2_b, v_w_in, v_w_attn_proj, v_w_ret_proj, v_ret_gn_g, v_ret_gn_b, v_w_out, v_ln1_g, v_ln1_b, v_w_up, v_conv_w, v_conv_b, v_w_down, v_w_ple_gate, v_w_ple_proj, v_ln2_g, v_ln2_b):
    W = dict(w_in=w_in, w_attn_proj=w_attn_proj, w_ret_proj=w_ret_proj, ret_gn_g=ret_gn_g, ret_gn_b=ret_gn_b, w_out=w_out,
             ln1_g=ln1_g, ln1_b=ln1_b, w_up=w_up, conv_w=conv_w, conv_b=conv_b, w_down=w_down, w_ple_gate=w_ple_gate,
             w_ple_proj=w_ple_proj, ln2_g=ln2_g, ln2_b=ln2_b)
    M = dict(w_in=m_w_in, w_attn_proj=m_w_attn_proj, w_ret_proj=m_w_ret_proj, ret_gn_g=m_ret_gn_g, ret_gn_b=m_ret_gn_b,
             w_out=m_w_out, ln1_g=m_ln1_g, ln1_b=m_ln1_b, w_up=m_w_up, conv_w=m_conv_w, conv_b=m_conv_b, w_down=m_w_down,
             w_ple_gate=m_w_ple_gate, w_ple_proj=m_w_ple_proj, ln2_g=m_ln2_g, ln2_b=m_ln2_b)
    V = dict(w_in=v_w_in, w_attn_proj=v_w_attn_proj, w_ret_proj=v_w_ret_proj, ret_gn_g=v_ret_gn_g, ret_gn_b=v_ret_gn_b,
             w_out=v_w_out, ln1_g=v_ln1_g, ln1_b=v_ln1_b, w_up=v_w_up, conv_w=v_conv_w, conv_b=v_conv_b, w_down=v_w_down,
             w_ple_gate=v_w_ple_gate, w_ple_proj=v_w_ple_proj, ln2_g=v_ln2_g, ln2_b=v_ln2_b)

    me = 4 * lax.axis_index("x") + 2 * lax.axis_index("y") + lax.axis_index("c")
    shard = lambda n, l: (W[n][l].T if n in COL_SHARDED else W[n][l]).astype(BF16)
    landing = lambda ts: [lax.dynamic_update_index_in_dim(lax.empty((N_DEV,) + t.shape, t.dtype), t, me, 0) for t in ts]
    first0 = _gather_many([shard(n, 0) for n in FIRST], "gather_first_l0")
    later0 = [shard(n, 0) for n in LATER] + [conv_w]
    flight0, token0 = _exchange_start("gather", later0, landing(later0), "gather_later_l0_start", after=first0[0])
    all1 = [shard(n, 1) for n in BIG]
    flight1, token1 = _exchange_start("gather", all1, landing(all1), "gather_weights_l1_start", after=token0)
    conv_w_all = []

    def later_first_layer(after):
        _, got = _exchange_wait(flight0, after, "gather_later_l0_wait")
        conv_w_all.append(got[-1].transpose(1, 2, 0, 3).reshape(DEPTH, 3, 2 * DFF))
        return _later_weights(dict(zip(LATER, got)), 0, conv_w_all[0], conv_b)

    def second_layer(after):
        _, got = _exchange_wait(flight1, after, "gather_weights_l1_wait")
        return _layer_weights(dict(zip(BIG, got)), 1, conv_w_all[0], conv_b, W)

    core = lax.axis_index("c").astype(jnp.int32).reshape(1)
    chip = (2 * lax.axis_index("x") + lax.axis_index("y")).astype(jnp.int32).reshape(1)
    empty_like = lambda ts, slots: [lax.empty((slots,) + t.shape[1:], t.dtype) for t in ts]
    chip32, far = [{} for _ in range(DEPTH)], [{} for _ in range(DEPTH)]
    pending = []

    def reduction(l, names, tag):
        state = {}

        def start(g):
            mine = [g[n].reshape((N_DEV, -1) + g[n].shape[1:]) for n in names]
            state["cores"], token = _exchange_start("cores", mine, empty_like(mine, 4), f"exchange_cores_{tag}_start")
            return token

        def onward(after):
            mine, theirs = _exchange_wait(state["cores"], after, f"exchange_cores_{tag}_wait")
            sums = [_pair_sum(a, b, core, f"pair_sum_l{l}_{n}") for a, b, n in zip(mine, theirs, names)]
            for n, s in zip(names, sums):
                chip32[l][n] = s[0]
            sent = [s[0 if n in F32_OVER_ICI else 1] for s, n in zip(sums, names)]
            flight, token = _exchange_start("chips", sent, empty_like(sent, 3), f"exchange_chips_{tag}_start")
            pending.append((l, names, flight, tag))
            return token

        return start, onward

    def on_grads(l, g):
        if l == 0:
            return {}
        start, onward = reduction(l, BIG, f"l{l}")
        return dict(token=start(g), after_ffn=onward)

    def own_hooks(l):
        if l != 0:
            return {}
        start_e, onward_e = reduction(0, LATER, "l0_later")
        start_w, onward_w = reduction(0, FIRST, "l0_first")
        return dict(early_grads=start_e, after_ret=onward_e, w_in_ready=lambda gw: start_w({"w_in": gw}), after_dx0=onward_w)

    ws = [(_first_weights(dict(zip(FIRST, first0)), 0, W), later_first_layer), second_layer]
    loss_vec, grad_x, grads = _local_step(x[0] + token1[0, 0], p[:, 0], positions[0], loss_target[0], ws, own_hooks, on_grads)
    loss = lax.psum(jnp.sum(loss_vec), ("x", "y", "c"))
    for l, names, flight, tag in pending:
        _, got = _exchange_wait(flight, grad_x, f"exchange_chips_{tag}_wait")
        far[l].update(zip(names, got))
    G, DW, NM, NV = ({} for _ in range(4))
    for n in BIG:
        chip32_n = [chip32[l][n] for l in range(DEPTH)]
        far_n = [far[l][n] for l in range(DEPTH)]
        if n in COL_SHARDED:
            G[n] = _reduce_tail(chip32_n, far_n, chip, f"reduced_{n}")[0].transpose(0, 2, 1)
            R2, C2 = DEPTH * W[n].shape[1], W[n].shape[2]
            res = _adamw(*(t.reshape(R2, C2) for t in (G[n], W[n], M[n], V[n])), f"adamw_{n}")
            DW[n], NM[n], NV[n] = (t.reshape(W[n].shape) for t in res)
        else:
            G[n], DW[n], NM[n], NV[n] = _reduce_tail(chip32_n, far_n, chip, f"adamw_{n}", wmv=(W[n], M[n], V[n]))

    small_names = SMALL + ("conv_w",)
    g_small, where = _pack_rows([jnp.stack([grads[l][n] for l in range(DEPTH)]) for n in small_names])
    (g_all,) = _gather_many([g_small], "gather_small_grads")
    g_small = _sum_slots(g_all, "sum_small_grads")
    for n, (off, rows) in zip(SMALL, where):
        G[n] = g_small[off:off + rows].reshape(W[n].shape)
    off, rows = where[-1]
    g_cw = g_small[off:off + rows].reshape(DEPTH, 3, N_DEV, conv_w.shape[2])
    G["conv_w"] = lax.dynamic_index_in_dim(g_cw, me, axis=2, keepdims=False)
    packed = [_pack_rows([d[n] for n in SMALL]) for d in (G, W, M, V)]
    small_out = _adamw(*(t for t, _ in packed), "adamw_small")
    for res, dst in zip(small_out, (DW, NM, NV)):
        for n, (off, rows) in zip(SMALL, packed[0][1]):
            dst[n] = res[off:off + rows].reshape(W[n].shape)
    two_d = lambda t: t.reshape(DEPTH * 3, conv_w.shape[2])
    cw_out = _adamw(two_d(G["conv_w"]), two_d(conv_w), two_d(m_conv_w), two_d(v_conv_w), "adamw_conv_w")
    for res, dst in zip(cw_out, (DW, NM, NV)):
        dst["conv_w"] = res.reshape(conv_w.shape)

    return (loss, grad_x[None], *[G[n] for n in WEIGHTS], *[DW[n] for n in WEIGHTS], *[NM[n] for n in WEIGHTS],
            *[NV[n] for n in WEIGHTS])
```

```python
import math

import numpy as np
import jax
import jax.numpy as jnp
from jax import lax
from jax.experimental import pallas as pl
from jax.experimental.pallas import tpu as pltpu

F32, BF16 = jnp.float32, jnp.bfloat16

D = 1024
DEPTH = 2
N_DEV = 8
HD = 128
NH = 8
DILATIONS = (1, 4, 16)
SPAN = 128
N_ATT = 3 * 3 * NH * HD
RH, RDK, RDV = 4, 256, 512
CH = 128
DFF = 2816
PLE = 256
N_IN = 17408
N_REST = N_IN - N_ATT
OFF_RQ, OFF_RK, OFF_RV, OFF_RG, OFF_GA, OFF_GR = 0, 1024, 2048, 4096, 6144, 7168
ALPHA = (2 * DEPTH) ** 0.25
LN_EPS, GN_EPS = 1e-5, 1e-6
ROPE_BASE = 10000.0
LR, B1, B2, EPS, WD, STEP = 0.001, 0.9, 0.999, 1e-8, 0.01, 10
VMEM_LIMIT = 48 * 1024 * 1024
NEG = -1e30

BIG = ("w_in", "w_attn_proj", "w_ret_proj", "w_out", "w_up", "w_down", "w_ple_gate", "w_ple_proj")
COL_SHARDED = ("w_in", "w_up", "w_ple_proj")
F32_OVER_ICI = ("w_attn_proj", "w_out", "w_ple_gate", "w_ple_proj")
SMALL = ("ret_gn_g", "ret_gn_b", "ln1_g", "ln1_b", "conv_b", "ln2_g", "ln2_b")
WEIGHTS = ("w_in", "w_attn_proj", "w_ret_proj", "ret_gn_g", "ret_gn_b", "w_out", "ln1_g", "ln1_b", "w_up",
           "conv_w", "conv_b", "w_down", "w_ple_gate", "w_ple_proj", "ln2_g", "ln2_b")


def _tile(n, cap, mult=128):
    if n <= cap:
        return n
    t = (cap // mult) * mult
    while n % t:
        t -= mult
    return t


def _cparams(sem):
    return pltpu.CompilerParams(dimension_semantics=sem, vmem_limit_bytes=VMEM_LIMIT)


def _dot(a, b, ca, cb):
    return lax.dot_general(a, b, (((ca,), (cb,)), ((), ())), preferred_element_type=F32)


def _bdot(a, b, ca, cb):
    return lax.dot_general(a, b, (((ca,), (cb,)), ((0,), (0,))), preferred_element_type=F32)


def _mm(a, b, *, name, ta=False, tb=False, out_dtype=F32, add=None, add_scale=1.0, tm=1024, tn=1024, tk=1024,
        b_rows=None, out_rows=None, into=None, blocks8=False, after=None):
    M, K = (a.shape[1], a.shape[0]) if ta else a.shape
    b_first, b_count = b_rows if b_rows else (0, b.shape[0])
    N = b_count if tb else b.shape[1]
    assert K == (b.shape[1] if tb else b_count)
    tm, tn, tk = _tile(M, tm), _tile(N, tn), _tile(K, tk)
    nk = K // tk
    o_first, o_total = out_rows if out_rows else (0, M)
    jb, kb, io = (b_first // tn, 0, o_first // tm) if tb else (0, b_first // tk, o_first // tm)
    assert b_first % (tn if tb else tk) == 0 and o_first % tm == 0 and (add is None or out_rows is None)

    def body(*refs):
        if add is None:
            a_ref, b_ref = refs[:2]
        else:
            a_ref, b_ref, add_ref = refs[:3]
        o_ref, acc_ref = refs[-2:]
        k = pl.program_id(2)

        @pl.when(k == 0)
        def _():
            acc_ref[...] = jnp.zeros_like(acc_ref)

        acc_ref[...] += _dot(a_ref[...].astype(BF16), b_ref[...].astype(BF16), 0 if ta else 1, 1 if tb else 0)

        @pl.when(k == nk - 1)
        def _():
            r = acc_ref[...]
            if add is not None:
                r = r + add_scale * add_ref[...].astype(F32)
            o_ref[...] = r.astype(out_dtype).reshape(o_ref.shape)

    a_spec = pl.BlockSpec((tk, tm), lambda i, j, k: (k, i)) if ta else pl.BlockSpec((tm, tk), lambda i, j, k: (i, k))
    if tb:
        b_spec = pl.BlockSpec((tn, tk), lambda i, j, k: (j + jb, k))
    else:
        b_spec = pl.BlockSpec((tk, tn), lambda i, j, k: (k + kb, j))
    if blocks8:
        assert tm == 1024
        o_spec = pl.BlockSpec((1, 8, 128, tn), lambda i, j, k: (i + io, 0, 0, j))
        o_shape = (o_total // tm, 8, 128, N)
    else:
        o_spec = pl.BlockSpec((tm, tn), lambda i, j, k: (i + io, j))
        o_shape = (o_total, N)
    in_specs, args, aliases = [a_spec, b_spec], [a, b], {}
    if add is not None:
        in_specs.append(o_spec)
        args.append(add)
    if after is not None:
        in_specs.append(pl.BlockSpec(memory_space=pl.ANY))
        args.append(after)
    if into is not None:
        aliases = {len(args): 0}
        in_specs.append(pl.BlockSpec(memory_space=pl.ANY))
        args.append(into)
    return pl.pallas_call(
        body, name=name, grid=(M // tm, N // tn, nk), in_specs=in_specs, out_specs=o_spec,
        out_shape=jax.ShapeDtypeStruct(o_shape, out_dtype), scratch_shapes=[pltpu.VMEM((tm, tn), F32)],
        input_output_aliases=aliases, compiler_params=_cparams(("parallel", "parallel", "arbitrary")),
    )(*args)


def _rowwise(fn, rows, pars, outs, accs, *, name, tm=512):
    first = rows[0][0] if isinstance(rows[0], tuple) else rows[0]
    S = first.shape[-2]
    tm = _tile(S, tm, 16)
    n_r, n_p, n_o = len(rows), len(pars), len(outs)
    outs = [o if len(o) == 5 else (o[0], o[1], o[0], 0, None) for o in outs]
    intos = [(k, o[4]) for k, o in enumerate(outs) if o[4] is not None]
    n_i = len(intos)

    def body(*refs):
        i = pl.program_id(0)
        vals = [r[...] for r in refs[:n_r + n_p]]
        res = fn(*vals)
        if not isinstance(res, (tuple, list)):
            res = (res,)
        o_refs = refs[n_r + n_p + n_i:n_r + n_p + n_i + n_o]
        a_refs = refs[n_r + n_p + n_i + n_o:]
        for r, v in zip(o_refs, res[:n_o]):
            r[...] = v.astype(r.dtype)
        if a_refs:
            @pl.when(i == 0)
            def _():
                for r in a_refs:
                    r[...] = jnp.zeros_like(r)

            for r, v in zip(a_refs, res[n_o:]):
                r[...] += v

    in_specs, args = [], []
    for r in rows:
        if isinstance(r, tuple):
            arr, w, cb = r
            in_specs.append(pl.BlockSpec((tm, w), lambda i, cb=cb: (i, cb)))
        elif r.ndim == 3:
            arr = r
            in_specs.append(pl.BlockSpec((arr.shape[0], tm, arr.shape[2]), lambda i: (0, i, 0)))
        else:
            arr = r
            in_specs.append(pl.BlockSpec((tm, arr.shape[1]), lambda i: (i, 0)))
        args.append(arr)
    for p_ in pars:
        in_specs.append(pl.BlockSpec(p_.shape, lambda i: (0, 0)))
        args.append(p_)
    aliases = {}
    for k, arr in intos:
        aliases[len(args)] = k
        in_specs.append(pl.BlockSpec(memory_space=pl.ANY))
        args.append(arr)
    out_shape = [jax.ShapeDtypeStruct((S, o[2]), o[1]) for o in outs] + [jax.ShapeDtypeStruct(a, F32) for a in accs]
    out_specs = [pl.BlockSpec((tm, o[0]), lambda i, cb=o[3]: (i, cb)) for o in outs] + [pl.BlockSpec(a, lambda i: (0, 0)) for a in accs]
    return pl.pallas_call(
        body, name=name, grid=(S // tm,), in_specs=in_specs, out_specs=out_specs, out_shape=out_shape,
        input_output_aliases=aliases, compiler_params=_cparams(("arbitrary",) if accs else ("parallel",)),
    )(*args)


def _norm(h, eps):
    mu = jnp.mean(h, -1, keepdims=True)
    d = h - mu
    rstd = lax.rsqrt(jnp.mean(d * d, -1, keepdims=True) + eps)
    return d * rstd, rstd


def _norm_bwd(dxh, xh, rstd):
    return rstd * (dxh - jnp.mean(dxh, -1, keepdims=True) - xh * jnp.mean(dxh * xh, -1, keepdims=True))


def _sig(x):
    return 1.0 / (1.0 + jnp.exp(-x))


_GELU_C = math.sqrt(2.0 / math.pi)


def _gelu(x, with_grad=False):
    x2 = x * x
    t = jnp.tanh(x * (_GELU_C + (_GELU_C * 0.044715) * x2))
    half_x, one_t = 0.5 * x, 1.0 + t
    if not with_grad:
        return half_x * one_t
    return half_x * one_t, 0.5 * one_t + half_x * (1.0 - t * t) * (_GELU_C + (3 * _GELU_C * 0.044715) * x2)


def _f_ln1(x, mix, g, b):
    h = ALPHA * x + mix
    xh, _ = _norm(h, LN_EPS)
    y = xh * g + b
    return h, y, y


def _f_ln2(x, ffn, z, pp, g, b):
    h = ALPHA * x + ffn + _sig(z) * pp
    xh, _ = _norm(h, LN_EPS)
    y = xh * g + b
    return h, y, y


def _f_ln_bwd(*args):
    *dys, h, g = args
    dy = dys[0]
    for t in dys[1:]:
        dy = dy + t
    xh, rstd = _norm(h, LN_EPS)
    dh = _norm_bwd(dy * g, xh, rstd)
    return dh, dh, jnp.sum(dy * xh, 0, keepdims=True), jnp.sum(dy, 0, keepdims=True)


def _f_sum(*ts):
    r = ts[0]
    for t in ts[1:]:
        r = r + t
    return r


def _f_loss(y, t):
    e = y - t
    return e * (1.0 / D), jnp.sum(e * e, 0, keepdims=True) * (0.5 / D)


def _head_col(c, h):
    lane = lax.broadcasted_iota(jnp.int32, c.shape, 1)
    return jnp.sum(jnp.where(lane == h, c, 0.0), -1, keepdims=True)


def _f_combine(o0, o1, o2, l0, l1, l2):
    m = jnp.maximum(jnp.maximum(l0, l1), l2)
    e0, e1, e2 = jnp.exp(l0 - m), jnp.exp(l1 - m), jnp.exp(l2 - m)
    den = e0 + e1 + e2
    inv = 1.0 / den
    w0, w1, w2 = e0 * inv, e1 * inv, e2 * inv
    parts = [_head_col(w0, h) * o0[h].astype(F32) + _head_col(w1, h) * o1[h].astype(F32) + _head_col(w2, h) * o2[h].astype(F32)
             for h in range(NH)]
    return jnp.concatenate(parts, axis=1), m + jnp.log(den)


def _f_delta(da, a):
    lane = lax.broadcasted_iota(jnp.int32, (da.shape[0], HD), 1)
    out = jnp.zeros((da.shape[0], HD), F32)
    for h in range(NH):
        sl = slice(h * HD, (h + 1) * HD)
        s = jnp.sum(da[:, sl].astype(F32) * a[:, sl].astype(F32), -1, keepdims=True)
        out = jnp.where(lane == h, s, out)
    return out


def _f_gate(ap, rp, ga, gr):
    return _sig(ga.astype(F32)) * ap.astype(F32) + _sig(gr.astype(F32)) * rp.astype(F32)


def _f_gate_bwd(dm, ap, rp, ga, gr):
    dm = dm.astype(F32)
    sa, sr = _sig(ga.astype(F32)), _sig(gr.astype(F32))
    dga, dgr = dm * ap.astype(F32) * sa * (1.0 - sa), dm * rp.astype(F32) * sr * (1.0 - sr)
    return dm * sa, dm * sr, jnp.concatenate([dga, dgr], axis=1)


def _f_gn(y, rg, g, b):
    y, rg = y.astype(F32), rg.astype(F32)
    parts = []
    for h in range(RH):
        sl = slice(h * RDV, (h + 1) * RDV)
        xh, _ = _norm(y[:, sl], GN_EPS)
        parts.append(xh * g[:, sl] + b[:, sl])
    return rg * _sig(rg) * jnp.concatenate(parts, axis=1)


def _f_gn_bwd(dr, y, rg, g, b):
    dr, y, rg = dr.astype(F32), y.astype(F32), rg.astype(F32)
    s = _sig(rg)
    d_out = dr * rg * s
    dys, outs, xhs = [], [], []
    for h in range(RH):
        sl = slice(h * RDV, (h + 1) * RDV)
        xh, rstd = _norm(y[:, sl], GN_EPS)
        xhs.append(xh)
        outs.append(xh * g[:, sl] + b[:, sl])
        dys.append(_norm_bwd(d_out[:, sl] * g[:, sl], xh, rstd))
    xh, out = jnp.concatenate(xhs, axis=1), jnp.concatenate(outs, axis=1)
    d_rg = dr * out * s * (1.0 + rg * (1.0 - s))
    return jnp.concatenate(dys, axis=1), d_rg, jnp.sum(d_out * xh, 0, keepdims=True), jnp.sum(d_out, 0, keepdims=True)


def _f_ln2_bwd(*args):
    *dys, h, z, pp, g = args
    dh, dhb, dg, db = _f_ln_bwd(*dys, h, g)
    s = _sig(z)
    return dh, dhb, dh * s, dh * pp * s * (1.0 - s), dg, db


QKV = 3 * HD


def _to_tokens(t, d):
    if d == 1:
        return t
    *lead, S, C = t.shape
    n = len(lead)
    perm = tuple(range(n)) + (n + 1, n, n + 2)
    return t.reshape(*lead, d, S // d, C).transpose(perm).reshape(*lead, S, C)


def _to_residues(t, d):
    if d == 1:
        return t
    S, C = t.shape
    return t.reshape(S // d, d, C).transpose(1, 0, 2).reshape(S, C)


def _to_head_residues(t, d):
    S = t.shape[0]
    return t.reshape(S // d, d, NH, HD).transpose(2, 1, 0, 3).reshape(NH, S, HD)


def _w_qkv_specs(g):
    return [pl.BlockSpec((D, D), lambda *i, t=t: (3 * g + t, 0)) for t in range(3)]


def _qkv_fwd(xv, win, g, dil, name):
    Sd = xv.shape[0]
    S = Sd * dil
    tm = min(512, Sd)
    nma = Sd // tm

    def body(a_ref, wq_ref, wk_ref, wv_ref, o_ref):
        a = a_ref[...]
        q, k, v = (_dot(a, w_ref[...], 1, 1).astype(BF16) for w_ref in (wq_ref, wk_ref, wv_ref))
        for h in range(NH):
            sl = slice(h * HD, (h + 1) * HD)
            o_ref[h] = jnp.concatenate([q[:, sl], k[:, sl], v[:, sl]], axis=1)

    return pl.pallas_call(
        body, name=name, grid=(S // tm,),
        in_specs=[pl.BlockSpec((tm, D), lambda i: (i % nma, i // nma))] + _w_qkv_specs(g),
        out_specs=pl.BlockSpec((NH, tm, QKV), lambda i: (0, i, 0)), out_shape=jax.ShapeDtypeStruct((NH, S, QKV), BF16),
        compiler_params=_cparams(("parallel",)),
    )(xv, win, win, win)


def _qkv_dx(dqkv, win, g, dil, name, out_dtype, add=None, after=None):
    S = dqkv.shape[1]
    Sd = S // dil
    tm = min(512, Sd)
    nmo = Sd // tm

    def body(*refs):
        a_ref, wq_ref, wk_ref, wv_ref = refs[:4]
        o_ref = refs[-1]
        acc = None
        for t, w_ref in enumerate((wq_ref, wk_ref, wv_ref)):
            d = jnp.concatenate([a_ref[h][:, t * HD:(t + 1) * HD] for h in range(NH)], axis=1)
            part = _dot(d, w_ref[...], 1, 0)
            acc = part if acc is None else acc + part
        if add is not None:
            acc = acc + refs[4][...]
        o_ref[...] = acc.astype(out_dtype)

    o_spec = pl.BlockSpec((tm, D), lambda i: (i % nmo, i // nmo))
    in_specs = [pl.BlockSpec((NH, tm, QKV), lambda i: (0, i, 0))] + _w_qkv_specs(g)
    args = [dqkv, win, win, win]
    if add is not None:
        assert dil == 1
        in_specs.append(o_spec)
        args.append(add)
    if after is not None:
        in_specs.append(pl.BlockSpec(memory_space=pl.ANY))
        args.append(after)
    return pl.pallas_call(
        body, name=name, grid=(S // tm,), in_specs=in_specs, out_specs=o_spec,
        out_shape=jax.ShapeDtypeStruct((Sd, dil * D), out_dtype), compiler_params=_cparams(("parallel",)),
    )(*args)


GW_IN_BLOCKS = (N_IN // D, NH, HD, D)


def _qkv_dw(dqkv, xv, g, dil, name, into=None):
    S = dqkv.shape[1]
    Sd = S // dil
    tk = min(1024, Sd)
    nkb, nk = Sd // tk, S // tk
    hh = NH // 2

    def body(*refs):
        a_ref, b_ref = refs[:2]
        o_ref, acc_ref = refs[-2:]
        k = pl.program_id(1)

        @pl.when(k == 0)
        def _():
            acc_ref[...] = jnp.zeros_like(acc_ref)

        b = b_ref[...]
        for h in range(hh):
            acc_ref[h * QKV:(h + 1) * QKV, :] += _dot(a_ref[h], b, 0, 0)

        @pl.when(k == nk - 1)
        def _():
            for h in range(hh):
                for t in range(3):
                    o_ref[t, h] = acc_ref[h * QKV + t * HD:h * QKV + (t + 1) * HD, :]

    in_specs = [pl.BlockSpec((hh, tk, QKV), lambda j, k: (j, k, 0)), pl.BlockSpec((tk, D), lambda j, k: (k % nkb, k // nkb))]
    args, aliases = [dqkv, xv], {}
    if into is not None:
        aliases = {2: 0}
        in_specs.append(pl.BlockSpec(memory_space=pl.ANY))
        args.append(into)
    return pl.pallas_call(
        body, name=name, grid=(2, nk), in_specs=in_specs,
        out_specs=pl.BlockSpec((3, hh, HD, D), lambda j, k: (g, j, 0, 0)), out_shape=jax.ShapeDtypeStruct(GW_IN_BLOCKS, F32),
        input_output_aliases=aliases, scratch_shapes=[pltpu.VMEM((hh * QKV, D), F32)],
        compiler_params=_cparams(("parallel", "arbitrary")),
    )(*args)


def _band(nb, first_valid, last_valid=None):
    b = lax.broadcasted_iota(jnp.int32, (nb, SPAN, SPAN), 0)
    row = lax.broadcasted_iota(jnp.int32, (nb, SPAN, SPAN), 1)
    col = lax.broadcasted_iota(jnp.int32, (nb, SPAN, SPAN), 2)
    off = jnp.where(b == 0, jnp.where(first_valid, 0, 2 * SPAN), 0)
    if last_valid is not None:
        off = off + jnp.where(b == nb - 1, jnp.where(last_valid, 0, 2 * SPAN), 0)
    return col <= row, col >= row + off


def _attn_tiles(S, dil):
    Sd = S // dil
    T = min(1024, Sd)
    hp = min(NH, max(1, (S // T) * NH // 16))
    return Sd, T, T // SPAN, Sd // T, hp


def _attn_fwd(qkv, dil, name):
    S = qkv.shape[1]
    Sd, T, nsub, nib, hp = _attn_tiles(S, dil)
    scale = HD ** -0.5

    def body(c_ref, p_ref, o_ref, l_ref):
        ib, hb = pl.program_id(1), pl.program_id(2)
        m_cur, m_prev = _band(nsub, ib > 0)
        lane = lax.broadcasted_iota(jnp.int32, (T, HD), 1)

        @pl.when(hb == 0)
        def _():
            l_ref[...] = jnp.zeros_like(l_ref)

        lses = l_ref[...]
        for hh in range(hp):
            blk, hal = c_ref[hh], p_ref[hh]
            q, k, v = blk[:, :HD], blk[:, HD:2 * HD], blk[:, 2 * HD:]
            if nsub > 1:
                kp = jnp.concatenate([hal[:, HD:2 * HD], k[:T - SPAN]], axis=0)
                vp = jnp.concatenate([hal[:, 2 * HD:], v[:T - SPAN]], axis=0)
            else:
                kp, vp = hal[:, HD:2 * HD], hal[:, 2 * HD:]
            q3, k3, v3, kp3, vp3 = (t.reshape(nsub, SPAN, HD) for t in (q, k, v, kp, vp))
            sc = jnp.where(m_cur, _bdot(q3, k3, 2, 2) * scale, NEG)
            sp = jnp.where(m_prev, _bdot(q3, kp3, 2, 2) * scale, NEG)
            m = jnp.maximum(jnp.max(sc, -1, keepdims=True), jnp.max(sp, -1, keepdims=True))
            pc, pp = jnp.exp(sc - m), jnp.exp(sp - m)
            den = jnp.sum(pc, -1, keepdims=True) + jnp.sum(pp, -1, keepdims=True)
            o = (_bdot(pc.astype(BF16), v3, 2, 1) + _bdot(pp.astype(BF16), vp3, 2, 1)) * (1.0 / den)
            o_ref[hh] = o.reshape(T, HD).astype(BF16)
            lses = jnp.where(lane == hb * hp + hh, (m + jnp.log(den)).reshape(T, 1), lses)
        l_ref[...] = lses

    cur = pl.BlockSpec((hp, T, QKV), lambda r, ib, h: (h, r * nib + ib, 0))
    prev = pl.BlockSpec((hp, SPAN, QKV), lambda r, ib, h: (h, r * (Sd // SPAN) + jnp.maximum(ib * nsub - 1, 0), 0))
    return pl.pallas_call(
        body, name=name, grid=(dil, nib, NH // hp), in_specs=[cur, prev],
        out_specs=[pl.BlockSpec((hp, T, HD), lambda r, ib, h: (h, r * nib + ib, 0)),
                   pl.BlockSpec((T, HD), lambda r, ib, h: (r * nib + ib, 0))],
        out_shape=[jax.ShapeDtypeStruct((NH, S, HD), BF16), jax.ShapeDtypeStruct((S, HD), F32)],
        compiler_params=_cparams(("parallel", "parallel", "arbitrary")),
    )(qkv, qkv)


def _attn_bwd(qkv, d_attn, lse, delta, dil, name):
    S = qkv.shape[1]
    Sd, T, nsub, nib, hp = _attn_tiles(S, dil)
    scale = HD ** -0.5
    ne = nsub + 1

    def body(c_ref, p_ref, n_ref, do_ref, don_ref, l_ref, ln_ref, dl_ref, dln_ref, o_ref):
        ib, hb = pl.program_id(1), pl.program_id(2)
        _, m_prev = _band(ne, ib > 0, ib < nib - 1)
        m_cur, _ = _band(nsub, True)
        for hh in range(hp):
            h = hb * hp + hh
            blk, hal, nxt = c_ref[hh], p_ref[hh], n_ref[hh]
            q, k, v = blk[:, :HD], blk[:, HD:2 * HD], blk[:, 2 * HD:]
            do = do_ref[hh]
            l, dl = _head_col(l_ref[...], h), _head_col(dl_ref[...], h)
            qe = jnp.concatenate([q, nxt[:, :HD]], axis=0).reshape(ne, SPAN, HD)
            doe = jnp.concatenate([do, don_ref[hh]], axis=0).reshape(ne, SPAN, HD)
            le = jnp.concatenate([l, _head_col(ln_ref[...], h)], axis=0).reshape(ne, SPAN, 1)
            dle = jnp.concatenate([dl, _head_col(dln_ref[...], h)], axis=0).reshape(ne, SPAN, 1)
            kpe = jnp.concatenate([hal[:, HD:2 * HD], k], axis=0).reshape(ne, SPAN, HD)
            vpe = jnp.concatenate([hal[:, 2 * HD:], v], axis=0).reshape(ne, SPAN, HD)
            p = jnp.where(m_prev, jnp.exp(_bdot(qe, kpe, 2, 2) * scale - le), 0.0)
            ds = (p * (_bdot(doe, vpe, 2, 2) - dle)).astype(BF16)
            dq = _bdot(ds, kpe, 2, 1)[:nsub]
            dk = _bdot(ds, qe, 1, 1)[1:]
            dv = _bdot(p.astype(BF16), doe, 1, 1)[1:]
            q3, k3, v3, do3 = (t.reshape(nsub, SPAN, HD) for t in (q, k, v, do))
            l3, dl3 = l.reshape(nsub, SPAN, 1), dl.reshape(nsub, SPAN, 1)
            p = jnp.where(m_cur, jnp.exp(_bdot(q3, k3, 2, 2) * scale - l3), 0.0)
            ds = (p * (_bdot(do3, v3, 2, 2) - dl3)).astype(BF16)
            dq = (dq + _bdot(ds, k3, 2, 1)) * scale
            dk = (dk + _bdot(ds, q3, 1, 1)) * scale
            dv = dv + _bdot(p.astype(BF16), do3, 1, 1)
            o_ref[hh] = jnp.concatenate([t.reshape(T, HD) for t in (dq, dk, dv)], axis=1).astype(BF16)

    nb = Sd // SPAN
    row = lambda r, ib: r * nib + ib
    prow = lambda r, ib: r * nb + jnp.maximum(ib * nsub - 1, 0)
    nrow = lambda r, ib: r * nb + jnp.minimum((ib + 1) * nsub, nb - 1)
    cur3 = pl.BlockSpec((hp, T, QKV), lambda r, ib, h: (h, row(r, ib), 0))
    prev3 = pl.BlockSpec((hp, SPAN, QKV), lambda r, ib, h: (h, prow(r, ib), 0))
    next3 = pl.BlockSpec((hp, SPAN, QKV), lambda r, ib, h: (h, nrow(r, ib), 0))
    cur1 = pl.BlockSpec((hp, T, HD), lambda r, ib, h: (h, row(r, ib), 0))
    next1 = pl.BlockSpec((hp, SPAN, HD), lambda r, ib, h: (h, nrow(r, ib), 0))
    curc = pl.BlockSpec((T, HD), lambda r, ib, h: (row(r, ib), 0))
    nextc = pl.BlockSpec((SPAN, HD), lambda r, ib, h: (nrow(r, ib), 0))
    return pl.pallas_call(
        body, name=name, grid=(dil, nib, NH // hp),
        in_specs=[cur3, prev3, next3, cur1, next1, curc, nextc, curc, nextc], out_specs=cur3,
        out_shape=jax.ShapeDtypeStruct((NH, S, QKV), BF16),
        compiler_params=_cparams(("parallel", "parallel", "parallel")),
    )(qkv, qkv, qkv, d_attn, d_attn, lse, lse, delta, delta)


def _ret_consts():
    lg = np.log1p(-np.exp2(-5.0 - np.arange(RH, dtype=np.float64)))
    idx = np.arange(CH, dtype=np.float64)
    rel = idx[:, None] - idx[None, :]
    intra = np.where(rel >= 0, np.exp(lg[:, None, None] * np.maximum(rel, 0.0)), 0.0)
    qd = np.exp(lg[:, None] * (idx + 1.0))
    kd = np.exp(lg[:, None] * (CH - 1.0 - idx))
    cd = np.exp(lg * CH)
    wide = lambda t: np.broadcast_to(t[:, :, None], (RH, t.shape[1], RDV))
    return (jnp.asarray(intra, F32), jnp.asarray(wide(qd), F32), jnp.asarray(wide(kd), F32),
            jnp.asarray(np.broadcast_to(cd[:, None, None], (RH, 1, RDV)), F32))


def _rot(t, c, s):
    t1, t2 = t[:, :RDK // 2], t[:, RDK // 2:]
    return jnp.concatenate([t1 * c - t2 * s, t1 * s + t2 * c], axis=1)


def _unrot(d, c, s):
    d1, d2 = d[:, :RDK // 2], d[:, RDK // 2:]
    return jnp.concatenate([d1 * c + d2 * s, d2 * c - d1 * s], axis=1)


RCH = 4


def _ret_specs(nmap):
    rows = RCH * CH
    q = pl.BlockSpec((rows, RH * RDK), lambda n: (nmap(n), OFF_RQ // (RH * RDK)))
    k = pl.BlockSpec((rows, RH * RDK), lambda n: (nmap(n), OFF_RK // (RH * RDK)))
    v = pl.BlockSpec((rows, RH * RDV), lambda n: (nmap(n), OFF_RV // (RH * RDV)))
    cs = pl.BlockSpec((rows, RDK // 2), lambda n: (nmap(n), 0))
    dmat = pl.BlockSpec((RH, CH, CH), lambda n: (0, 0, 0))
    dvec = pl.BlockSpec((RH, CH, RDV), lambda n: (0, 0, 0))
    cdv = pl.BlockSpec((RH, 1, RDV), lambda n: (0, 0, 0))
    state = pl.BlockSpec((RH, RCH, RDK, RDV), lambda n: (0, nmap(n), 0, 0))
    out = pl.BlockSpec((rows, RH * RDV), lambda n: (nmap(n), 0))
    return [q, k, v, cs, cs, dmat, dvec, dvec, cdv], state, out


def _ret_fwd(proj, cos, sin, consts):
    S = proj.shape[0]
    nc = S // CH

    def body(q_ref, k_ref, v_ref, c_ref, s_ref, d_ref, qd_ref, kd_ref, cd_ref, o_ref, st_ref, state):
        @pl.when(pl.program_id(0) == 0)
        def _():
            state[...] = jnp.zeros_like(state)

        for ci in range(RCH):
            rows = slice(ci * CH, (ci + 1) * CH)
            c, s = c_ref[rows, :], s_ref[rows, :]
            for h in range(RH):
                qk, vv = slice(h * RDK, (h + 1) * RDK), slice(h * RDV, (h + 1) * RDV)
                qb = _rot(q_ref[rows, qk].astype(F32), c, s).astype(BF16)
                kb = (_rot(k_ref[rows, qk].astype(F32), c, s) * (RDK ** -0.5)).astype(BF16)
                vb = v_ref[rows, vv]
                sb = state[h].astype(BF16)
                st_ref[h, ci] = sb
                a = (_dot(qb, kb, 1, 1) * d_ref[h]).astype(BF16)
                o_ref[rows, vv] = (_dot(a, vb, 1, 0) + _dot(qb, sb, 1, 0) * qd_ref[h]).astype(BF16)
                vk = (vb.astype(F32) * kd_ref[h]).astype(BF16)
                state[h] = cd_ref[h] * state[h] + _dot(kb, vk, 0, 0)

    ins, state_spec, out_spec = _ret_specs(lambda n: n)
    return pl.pallas_call(
        body, name="ret_fwd", grid=(nc // RCH,), in_specs=ins, out_specs=[out_spec, state_spec],
        out_shape=[jax.ShapeDtypeStruct((S, RH * RDV), BF16), jax.ShapeDtypeStruct((RH, nc, RDK, RDV), BF16)],
        scratch_shapes=[pltpu.VMEM((RH, RDK, RDV), F32)],
        compiler_params=_cparams(("arbitrary",)),
    )(proj, proj, proj, cos, sin, *consts)


def _ret_bwd(proj, cos, sin, consts, states, d_ret, d_rest):
    S = proj.shape[0]
    nc = S // CH

    def body(q_ref, k_ref, v_ref, c_ref, s_ref, d_ref, qd_ref, kd_ref, cd_ref, st_ref, do_ref, _, o_ref, dstate):
        @pl.when(pl.program_id(0) == 0)
        def _():
            dstate[...] = jnp.zeros_like(dstate)

        for ci in reversed(range(RCH)):
            rows = slice(ci * CH, (ci + 1) * CH)
            c, s = c_ref[rows, :], s_ref[rows, :]
            for h in range(RH):
                qk, vv = slice(h * RDK, (h + 1) * RDK), slice(h * RDV, (h + 1) * RDV)
                qb = _rot(q_ref[rows, qk].astype(F32), c, s).astype(BF16)
                kb = (_rot(k_ref[rows, qk].astype(F32), c, s) * (RDK ** -0.5)).astype(BF16)
                vb, sb, do = v_ref[rows, vv], st_ref[h, ci], do_ref[rows, vv]
                dmat, qd, kd = d_ref[h], qd_ref[h], kd_ref[h]
                a = (_dot(qb, kb, 1, 1) * dmat).astype(BF16)
                doq = (do.astype(F32) * qd).astype(BF16)
                dsb = dstate[h].astype(BF16)
                vk = (vb.astype(F32) * kd).astype(BF16)
                o_ref[rows, OFF_RV + h * RDV:OFF_RV + (h + 1) * RDV] = (_dot(a, do, 0, 0) + _dot(kb, dsb, 1, 0) * kd).astype(BF16)
                da = (_dot(do, vb, 1, 1) * dmat).astype(BF16)
                dq = _dot(da, kb, 1, 0) + _dot(doq, sb, 1, 1)
                dk = (_dot(da, qb, 0, 0) + _dot(vk, dsb, 1, 1)) * (RDK ** -0.5)
                o_ref[rows, OFF_RQ + h * RDK:OFF_RQ + (h + 1) * RDK] = _unrot(dq, c, s).astype(BF16)
                o_ref[rows, OFF_RK + h * RDK:OFF_RK + (h + 1) * RDK] = _unrot(dk, c, s).astype(BF16)
                dstate[h] = cd_ref[h] * dstate[h] + _dot(qb, doq, 0, 0)

    nsteps = nc // RCH
    rev = lambda n: nsteps - 1 - n
    ins, state_spec, out_spec = _ret_specs(rev)
    return pl.pallas_call(
        body, name="ret_bwd", grid=(nsteps,), in_specs=ins + [state_spec, out_spec, pl.BlockSpec(memory_space=pl.ANY)],
        out_specs=pl.BlockSpec((RCH * CH, OFF_RG), lambda n: (rev(n), 0)),
        out_shape=jax.ShapeDtypeStruct(d_rest.shape, BF16), input_output_aliases={11: 0},
        scratch_shapes=[pltpu.VMEM((RH, RDK, RDV), F32)],
        compiler_params=_cparams(("arbitrary",)),
    )(proj, proj, proj, cos, sin, *consts, states, d_ret, d_rest)


CW = 256
HALO = 16


def _shift_down(v, halo, k):
    rolled = pltpu.roll(v, k, 0)
    hr = pltpu.roll(halo, k, 0)[0:8]
    row = lax.broadcasted_iota(jnp.int32, hr.shape, 0)
    return jnp.concatenate([jnp.where(row < k, hr, rolled[0:8]), rolled[8:]], axis=0)


def _shift_up(v, halo, k):
    T = v.shape[0]
    rolled = pltpu.roll(v, T - k, 0)
    hr = pltpu.roll(halo, 8 - k, 0)[0:8]
    row = lax.broadcasted_iota(jnp.int32, hr.shape, 0)
    return jnp.concatenate([rolled[:T - 8], jnp.where(row >= 8 - k, hr, rolled[T - 8:])], axis=0)


def _conv_taps(h_ref, hp_ref, first):
    h = h_ref[...].astype(F32)
    hp = hp_ref[...].astype(F32) * jnp.where(first, 0.0, 1.0)
    return _shift_down(h, hp, 2), _shift_down(h, hp, 1), h


def _conv_specs(S, T, cw=CW):
    nj = DFF // cw
    cur = pl.BlockSpec((T, cw), lambda j, i: (i, j))
    prev = pl.BlockSpec((HALO, cw), lambda j, i: (jnp.maximum(i * (T // HALO) - 1, 0), j))
    nxt = pl.BlockSpec((HALO, cw), lambda j, i: (jnp.minimum((i + 1) * (T // HALO), S // HALO - 1), j))
    w = pl.BlockSpec((3, cw), lambda j, i: (0, j))
    b = pl.BlockSpec((1, cw), lambda j, i: (0, j))
    return nj, cur, prev, nxt, w, b


def _conv_fwd(hg, hu, wg, wu, bg, bu):
    S = hg.shape[0]
    T = min(1024, S)
    nj, cur, prev, _, w, b = _conv_specs(S, T)

    def body(hg_ref, hu_ref, hgp_ref, hup_ref, wg_ref, wu_ref, bg_ref, bu_ref, o_ref):
        first = pl.program_id(1) == 0
        g2, g1, g0 = _conv_taps(hg_ref, hgp_ref, first)
        u2, u1, u0 = _conv_taps(hu_ref, hup_ref, first)
        cg = wg_ref[0:1, :] * g2 + wg_ref[1:2, :] * g1 + wg_ref[2:3, :] * g0 + bg_ref[...]
        cu = wu_ref[0:1, :] * u2 + wu_ref[1:2, :] * u1 + wu_ref[2:3, :] * u0 + bu_ref[...]
        o_ref[...] = (_gelu(cg) * cu).astype(BF16)

    return pl.pallas_call(
        body, name="conv_fwd", grid=(nj, S // T), in_specs=[cur, cur, prev, prev, w, w, b, b], out_specs=cur,
        out_shape=jax.ShapeDtypeStruct((S, DFF), BF16), compiler_params=_cparams(("parallel", "parallel")),
    )(hg, hu, hg, hu, wg, wu, bg, bu)


def _conv_bwd_pre(d_act, hg, hu, wg, wu, bg, bu):
    S = hg.shape[0]
    T = min(1024, S)
    nj, cur, prev, _, w, b = _conv_specs(S, T)

    def body(da_ref, hg_ref, hu_ref, hgp_ref, hup_ref, wg_ref, wu_ref, bg_ref, bu_ref,
             dcg_ref, dcu_ref, gwg_ref, gwu_ref, gbg_ref, gbu_ref):
        first = pl.program_id(1) == 0
        g2, g1, g0 = _conv_taps(hg_ref, hgp_ref, first)
        u2, u1, u0 = _conv_taps(hu_ref, hup_ref, first)
        cg = wg_ref[0:1, :] * g2 + wg_ref[1:2, :] * g1 + wg_ref[2:3, :] * g0 + bg_ref[...]
        cu = wu_ref[0:1, :] * u2 + wu_ref[1:2, :] * u1 + wu_ref[2:3, :] * u0 + bu_ref[...]
        da = da_ref[...].astype(F32)
        gl, dgl = _gelu(cg, with_grad=True)
        dcg = da * cu * dgl
        dcu = da * gl
        dcg_ref[...] = dcg.astype(BF16)
        dcu_ref[...] = dcu.astype(BF16)

        @pl.when(first)
        def _():
            for r in (gwg_ref, gwu_ref, gbg_ref, gbu_ref):
                r[...] = jnp.zeros_like(r)

        for r, d, taps in ((gwg_ref, dcg, (g2, g1, g0)), (gwu_ref, dcu, (u2, u1, u0))):
            for j in range(3):
                r[j:j + 1, :] += jnp.sum(d * taps[j], 0, keepdims=True)
        gbg_ref[...] += jnp.sum(dcg, 0, keepdims=True)
        gbu_ref[...] += jnp.sum(dcu, 0, keepdims=True)

    return pl.pallas_call(
        body, name="conv_bwd_pre", grid=(nj, S // T), in_specs=[cur, cur, cur, prev, prev, w, w, b, b],
        out_specs=[cur, cur, w, w, b, b],
        out_shape=[jax.ShapeDtypeStruct((S, DFF), BF16)] * 2 + [jax.ShapeDtypeStruct((3, DFF), F32)] * 2
        + [jax.ShapeDtypeStruct((1, DFF), F32)] * 2,
        compiler_params=_cparams(("parallel", "arbitrary")),
    )(d_act, hg, hu, hg, hu, wg, wu, bg, bu)


def _conv_bwd_in(dc, w, name):
    S = dc.shape[0]
    T = min(512, S)
    nj, cur, _, nxt, wspec, _ = _conv_specs(S, T, DFF // 2)
    nt = S // T

    def body(dc_ref, dn_ref, w_ref, o_ref):
        d = dc_ref[...].astype(F32)
        dn = dn_ref[...].astype(F32) * jnp.where(pl.program_id(1) == nt - 1, 0.0, 1.0)
        o_ref[...] = (w_ref[2:3, :] * d + w_ref[1:2, :] * _shift_up(d, dn, 1) + w_ref[0:1, :] * _shift_up(d, dn, 2)).astype(BF16)

    return pl.pallas_call(
        body, name=name, grid=(nj, nt), in_specs=[cur, nxt, wspec], out_specs=cur,
        out_shape=jax.ShapeDtypeStruct((S, DFF), BF16), compiler_params=_cparams(("parallel", "parallel")),
    )(dc, dc, w)


def _adam_math(g, w, m, v):
    m = B1 * m + (1.0 - B1) * g
    v = B2 * v + (1.0 - B2) * (g * g)
    m_hat = m / (1.0 - B1 ** STEP)
    v_hat = v / (1.0 - B2 ** STEP)
    return -LR * (m_hat / (jnp.sqrt(v_hat) + EPS) + WD * w), m, v


def _reduce_tail(chip32, far, chip, name, wmv=None):
    L = len(chip32)
    _, R, C = chip32[0].shape
    tr = _tile(R, 256, 16)
    nr = R // tr

    def body(chip_ref, *refs):
        own_refs, far_refs, rest = refs[:L], refs[L:2 * L], refs[2 * L:]
        outs = rest[3:] if wmv else rest
        for ll in range(L):
            @pl.when(pl.program_id(0) == ll)
            def _(ll=ll):
                g = own_refs[ll][...]
                for s in range(3):
                    g = g + far_refs[ll][s].astype(F32)
                outs[0][...] = g
                if wmv:
                    outs[1][...], outs[2][...], outs[3][...] = _adam_math(g, rest[0][...], rest[1][...], rest[2][...])

    def rows(ll):
        return lambda l, i: jnp.where(l == ll, i, jnp.where(l < ll, 0, nr - 1))

    blk = pl.BlockSpec((None, tr, C), lambda l, i, ch: (l, i, 0))
    in_specs = [pl.BlockSpec((None, tr, C), lambda l, i, ch, f=rows(ll): (ch[0], f(l, i), 0)) for ll in range(L)]
    in_specs += [pl.BlockSpec((3, tr, C), lambda l, i, ch, f=rows(ll): (0, f(l, i), 0)) for ll in range(L)]
    args = list(chip32) + list(far)
    n_out = 1
    if wmv:
        in_specs += [blk] * 3
        args += list(wmv)
        n_out = 4
    return pl.pallas_call(
        body, name=name,
        grid_spec=pltpu.PrefetchScalarGridSpec(num_scalar_prefetch=1, grid=(L, nr), in_specs=in_specs, out_specs=[blk] * n_out),
        out_shape=[jax.ShapeDtypeStruct((L, R, C), F32)] * n_out, compiler_params=_cparams(("arbitrary", "arbitrary")),
    )(chip, *args)


def _adamw(g, w, m, v, name):
    R, C = g.shape
    tr = _tile(R, 128, 8)

    def body(g_ref, w_ref, m_ref, v_ref, d_ref, nm_ref, nv_ref):
        d_ref[...], nm_ref[...], nv_ref[...] = _adam_math(g_ref[...], w_ref[...], m_ref[...], v_ref[...])

    blk = pl.BlockSpec((tr, C), lambda i: (i, 0))
    return pl.pallas_call(
        body, name=name, grid=(R // tr,), in_specs=[blk] * 4, out_specs=[blk] * 3,
        out_shape=[jax.ShapeDtypeStruct(g.shape, F32)] * 3, compiler_params=_cparams(("parallel",)),
    )(g, w, m, v)


def _pair_sum(x, recv, core, name):
    _, R, C = x.shape
    tr = _tile(R, 600, 16)

    def body(core_ref, x_ref, r_ref, o32_ref, o16_ref):
        s = x_ref[...] + r_ref[...]
        o32_ref[...] = s
        o16_ref[...] = s.astype(BF16)

    blk = pl.BlockSpec((None, tr, C), lambda q, i, c: (q, i, 0))
    mine = pl.BlockSpec((None, None, tr, C), lambda q, i, c: (q, c[0], i, 0))
    return pl.pallas_call(
        body, name=name,
        grid_spec=pltpu.PrefetchScalarGridSpec(num_scalar_prefetch=1, grid=(4, R // tr), in_specs=[mine, blk], out_specs=[blk, blk]),
        out_shape=[jax.ShapeDtypeStruct((4, R, C), F32), jax.ShapeDtypeStruct((4, R, C), BF16)],
        compiler_params=_cparams(("parallel", "parallel")),
    )(core, x.reshape(4, 2, R, C), recv)


def _sum_slots(x, name):
    def body(x_ref, o_ref):
        g = x_ref[0]
        for s in range(1, x.shape[0]):
            g = g + x_ref[s]
        o_ref[...] = g

    return pl.pallas_call(body, name=name, out_shape=jax.ShapeDtypeStruct(x.shape[1:], F32))(x)


MESH = pl.DeviceIdType.MESH
_HBM = pl.BlockSpec(memory_space=pltpu.HBM)


def _dma_sems(n):
    return pltpu.SemaphoreType.DMA((n,))


def _gather_many(xs, name):
    n = len(xs)

    def body(*refs):
        x_refs, out_refs = refs[:n], refs[n:2 * n]
        send_sems, recv_sems, local_sems = refs[2 * n:]
        ax, ay, ac = lax.axis_index("x"), lax.axis_index("y"), lax.axis_index("c")
        me, sibling = (ax, ay, ac), (ax, ay, 1 - ac)
        chips = [(1 - ax, ay), (ax, 1 - ay), (1 - ax, 1 - ay)]

        def copy(a, k, block, to, own=False):
            slot = out_refs[a].at[4 * block[0] + 2 * block[1] + block[2]]
            return pltpu.make_async_remote_copy(
                src_ref=x_refs[a] if own else slot, dst_ref=slot, send_sem=send_sems.at[7 * a + k],
                recv_sem=recv_sems.at[7 * a + k], device_id=to, device_id_type=MESH)

        mine = [pltpu.make_async_copy(x_refs[a], out_refs[a].at[4 * ax + 2 * ay + ac], local_sems.at[a]) for a in range(n)]
        first = [copy(a, 0, me, sibling, own=True) for a in range(n)]
        first += [copy(a, 1 + j, me, (*chip, ac), own=True) for j, chip in enumerate(chips) for a in range(n)]
        for cp in mine + first:
            cp.start()
        passed = []
        for j, chip in enumerate(chips):
            for a in range(n):
                copy(a, 1 + j, (*chip, ac), me).wait_recv()
                cp = copy(a, 4 + j, (*chip, ac), sibling)
                cp.start()
                passed.append(cp)
        for a in range(n):
            copy(a, 0, sibling, me).wait_recv()
            for j, chip in enumerate(chips):
                copy(a, 4 + j, (*chip, 1 - ac), me).wait_recv()
        for cp in first + passed:
            cp.wait_send()
        for cp in mine:
            cp.wait()

    return pl.pallas_call(
        body, name=name, out_shape=[jax.ShapeDtypeStruct((N_DEV,) + x.shape, x.dtype) for x in xs],
        in_specs=[_HBM] * n, out_specs=[_HBM] * n, scratch_shapes=[_dma_sems(7 * n), _dma_sems(7 * n), _dma_sems(n)],
    )(*xs)


_SEM = pl.BlockSpec(memory_space=pltpu.SEMAPHORE)
_EFFECT = pltpu.SideEffectType.DATAFLOW_SIDE_EFFECTING


def _peer(k):
    ax, ay, ac = lax.axis_index("x"), lax.axis_index("y"), lax.axis_index("c")
    px = 1 - ax if k & 4 else ax
    py = 1 - ay if k & 2 else ay
    pc = 1 - ac if k & 1 else ac
    return (px, py, pc), 4 * px + 2 * py + pc


def _build_gather(x_refs, land_refs, send_sems, recv_sems, waiting):
    _, me = _peer(0)
    copies = []
    for a in range(len(x_refs)):
        for k in range(1, N_DEV):
            peer, slot = _peer(k)
            copies.append(pltpu.make_async_remote_copy(
                src_ref=x_refs[a], dst_ref=land_refs[a].at[slot if waiting else me], send_sem=send_sems.at[7 * a + k - 1],
                recv_sem=recv_sems.at[7 * a + k - 1], device_id=peer, device_id_type=MESH))
    return copies


def _build_cores(x_refs, land_refs, send_sems, recv_sems, waiting):
    ax, ay, ac = lax.axis_index("x"), lax.axis_index("y"), lax.axis_index("c")
    copies = []
    for a in range(len(x_refs)):
        for q in range(4):
            copies.append(pltpu.make_async_remote_copy(
                src_ref=x_refs[a].at[2 * q + 1 - ac], dst_ref=land_refs[a].at[q], send_sem=send_sems.at[4 * a + q],
                recv_sem=recv_sems.at[4 * a + q], device_id=(ax, ay, 1 - ac), device_id_type=MESH))
    return copies


def _build_chips(p_refs, land_refs, send_sems, recv_sems, waiting):
    ax, ay, ac = lax.axis_index("x"), lax.axis_index("y"), lax.axis_index("c")
    copies = []
    for a in range(len(p_refs)):
        for k in range(1, 4):
            px = 1 - ax if k & 2 else ax
            py = 1 - ay if k & 1 else ay
            copies.append(pltpu.make_async_remote_copy(
                src_ref=p_refs[a].at[2 * px + py], dst_ref=land_refs[a].at[k - 1], send_sem=send_sems.at[3 * a + k - 1],
                recv_sem=recv_sems.at[3 * a + k - 1], device_id=(px, py, ac), device_id_type=MESH))
    return copies


_EXCHANGES = {"gather": (_build_gather, 7, N_DEV), "cores": (_build_cores, 4, 4), "chips": (_build_chips, 3, 3)}


def _exchange_start(kind, xs, lands, name, after=None):
    build, per, _ = _EXCHANGES[kind]
    n = len(xs)

    def body(*refs):
        for cp in build(refs[:n], refs[n:2 * n], refs[-2 * n - 3], refs[-2 * n - 2], False):
            cp.start()
        refs[-1][...] = jnp.zeros_like(refs[-1])

    hbm = lambda t: pltpu.HBM(t.shape, t.dtype)
    args = [pltpu.with_memory_space_constraint(t, pltpu.HBM) for t in list(xs) + list(lands)]
    in_specs = [_HBM] * (2 * n)
    if after is not None:
        args.append(after)
        in_specs.append(pl.BlockSpec(memory_space=pl.ANY))
    outs = pl.pallas_call(
        body, name=name,
        out_shape=(_dma_sems(per * n), _dma_sems(per * n), *[hbm(t) for t in xs], *[hbm(t) for t in lands],
                   jax.ShapeDtypeStruct((8, 128), F32)),
        in_specs=in_specs, out_specs=(_SEM, _SEM, *[_HBM] * (2 * n), pl.BlockSpec(memory_space=pltpu.VMEM)),
        input_output_aliases={a: 2 + a for a in range(2 * n)},
        compiler_params=pltpu.CompilerParams(has_side_effects=_EFFECT),
    )(*args)
    return (kind, outs[0], outs[1], outs[2:2 + n], outs[2 + n:2 + 2 * n]), outs[-1]


def _exchange_wait(flight, after, name):
    kind, send_sems, recv_sems, xs, lands = flight
    build = _EXCHANGES[kind][0]
    n = len(xs)

    def body(*refs):
        for cp in build(refs[:n], refs[n:2 * n], refs[2 * n], refs[2 * n + 1], True):
            cp.wait_send()
            cp.wait_recv()

    hbm = lambda t: pltpu.HBM(t.shape, t.dtype)
    outs = pl.pallas_call(
        body, name=name, out_shape=(*[hbm(t) for t in xs], *[hbm(t) for t in lands]),
        in_specs=[_HBM] * (2 * n) + [_SEM, _SEM, pl.BlockSpec(memory_space=pl.ANY)], out_specs=[_HBM] * (2 * n),
        input_output_aliases={a: a for a in range(2 * n)}, compiler_params=pltpu.CompilerParams(has_side_effects=_EFFECT),
    )(*xs, *lands, send_sems, recv_sems, after)
    return outs[:n], outs[n:]


def _x_view(xb, d):
    return xb if d == 1 else xb.reshape(xb.shape[0] // d, d * xb.shape[1])


def _layer_fwd(x, xb, p, w, cos, sin, rconsts, late=None):
    proj = _mm(xb, w["win"], tb=True, b_rows=(N_ATT, N_REST), name="mm_proj", out_dtype=BF16)
    qkvs, ogs, lgs = [], [], []
    for g, dil in enumerate(DILATIONS):
        qkv = _qkv_fwd(_x_view(xb, dil), w["win"], g, dil, f"mm_qkv{g}")
        o, l = _attn_fwd(qkv, dil, f"attn_fwd_g{g}")
        qkvs.append(qkv)
        ogs.append(_to_tokens(o, dil))
        lgs.append(_to_tokens(l, dil))
    attn, lse = _rowwise(_f_combine, ogs + lgs, [], [(D, BF16), (HD, F32)], [], name="attn_combine")
    ret_raw, states = _ret_fwd(proj, cos, sin, rconsts)
    rg_win = (proj, RH * RDV, OFF_RG // (RH * RDV))
    ga_win, gr_win = (proj, D, OFF_GA // D), (proj, D, OFF_GR // D)
    (r,) = _rowwise(_f_gn, [ret_raw, rg_win], [w["ret_gn_g"], w["ret_gn_b"]], [(RH * RDV, BF16)], [], name="gn_fwd", tm=256)
    if late is not None:
        w = {**w, **late(r)}
    ap = _mm(attn, w["w_attn_proj"], name="mm_attn_proj", out_dtype=BF16)
    rp = _mm(r, w["w_ret_proj"], name="mm_ret_proj", out_dtype=BF16, tk=2048)
    (merged,) = _rowwise(_f_gate, [ap, rp, ga_win, gr_win], [], [(D, BF16)], [], name="gate_fwd")
    mix = _mm(merged, w["w_out"], name="mm_out")
    h1, x1, x1b = _rowwise(_f_ln1, [x, mix], [w["ln1_g"], w["ln1_b"]], [(D, F32), (D, F32), (D, BF16)], [], name="ln1_fwd")
    z = _mm(x1b, w["w_ple_gate"], name="mm_ple_gate")
    pp = _mm(p, w["w_ple_proj"], tb=True, name="mm_ple_proj")
    hg = _mm(x1b, w["w_up"], tb=True, b_rows=(0, DFF), name="mm_up_g", out_dtype=BF16, tm=512, tn=DFF)
    hu = _mm(x1b, w["w_up"], tb=True, b_rows=(DFF, DFF), name="mm_up_u", out_dtype=BF16, tm=512, tn=DFF)
    act = _conv_fwd(hg, hu, w["conv_wg"], w["conv_wu"], w["conv_bg"], w["conv_bu"])
    ffn = _mm(act, w["w_down"], name="mm_down", tm=512, tk=DFF)
    h2, x2, x2b = _rowwise(_f_ln2, [x1, ffn, z, pp], [w["ln2_g"], w["ln2_b"]], [(D, F32), (D, F32), (D, BF16)], [], name="ln2_fwd")
    saved = dict(xb=xb, proj=proj, qkvs=qkvs, attn=attn, lse=lse, ret_raw=ret_raw, states=states, r=r, ap=ap, rp=rp,
                 merged=merged, h1=h1, x1b=x1b, z=z, pp=pp, hg=hg, hu=hu, act=act, h2=h2, p=p)
    return x2, x2b, saved, w


def _after(fn, token):
    return fn if token is None else (lambda *a: fn(*a[:-1]))


def _layer_bwd(dys, w, sv, cos, sin, rconsts, hooks):
    gr = {}
    proj = sv["proj"]
    call = lambda key, *a: hooks[key](*a) if key in hooks else None
    held = lambda token: [] if token is None else [token]
    token = hooks.get("token")
    dh2, dh2b, dpp, dz, gr["ln2_g"], gr["ln2_b"] = _rowwise(
        _after(_f_ln2_bwd, token), list(dys) + [sv["h2"], sv["z"], sv["pp"]], [w["ln2_g"]] + held(token),
        [(D, F32), (D, BF16), (D, BF16), (D, BF16)], [(1, D), (1, D)], name="ln2_bwd")
    d_act = _mm(dh2b, w["w_down"], tb=True, name="mm_down_dx", out_dtype=BF16, tm=512, tn=DFF)
    gr["w_down"] = _mm(sv["act"], dh2b, ta=True, name="mm_down_dw", tm=DFF // 2)
    dcg, dcu, gwg, gwu, gbg, gbu = _conv_bwd_pre(d_act, sv["hg"], sv["hu"], w["conv_wg"], w["conv_wu"], w["conv_bg"], w["conv_bu"])
    token = call("after_ffn", dcg)
    gr["conv_w"] = jnp.concatenate([gwg, gwu], axis=1)
    gr["conv_b"] = jnp.concatenate([gbg, gbu], axis=1)
    dhg = _conv_bwd_in(dcg, w["conv_wg"], "conv_bwd_in_g")
    dhu = _conv_bwd_in(dcu, w["conv_wu"], "conv_bwd_in_u")
    gw_up = _mm(dhg, sv["x1b"], ta=True, name="mm_up_g_dw", tm=DFF // 2, out_rows=(0, 2 * DFF))
    gr["w_up"] = _mm(dhu, sv["x1b"], ta=True, name="mm_up_u_dw", tm=DFF // 2, out_rows=(DFF, 2 * DFF), into=gw_up)
    dx1 = _mm(dhg, w["w_up"], b_rows=(0, DFF), name="mm_up_g_dx", add=dh2, add_scale=ALPHA, tm=512, tk=DFF)
    dx1 = _mm(dhu, w["w_up"], b_rows=(DFF, DFF), name="mm_up_u_dx", add=dx1, tm=512, tk=DFF)
    gr["w_ple_proj"] = _mm(dpp, sv["p"], ta=True, name="mm_ple_proj_dw")
    gr["w_ple_gate"] = _mm(sv["x1b"], dz, ta=True, name="mm_ple_gate_dw")
    dx1 = _mm(dz, w["w_ple_gate"], tb=True, name="mm_ple_gate_dx", add=dx1)
    dh1, dh1b, gr["ln1_g"], gr["ln1_b"] = _rowwise(_after(_f_ln_bwd, token), [dx1, sv["h1"]], [w["ln1_g"]] + held(token),
                                                   [(D, F32), (D, BF16)], [(1, D), (1, D)], name="ln1_bwd")
    d_merged = _mm(dh1b, w["w_out"], tb=True, name="mm_out_dx", out_dtype=BF16)
    gr["w_out"] = _mm(sv["merged"], dh1b, ta=True, name="mm_out_dw")
    rg_win = (proj, RH * RDV, OFF_RG // (RH * RDV))
    ga_win, gr_win = (proj, D, OFF_GA // D), (proj, D, OFF_GR // D)
    dap, drp, d_rest = _rowwise(_f_gate_bwd, [d_merged, sv["ap"], sv["rp"], ga_win, gr_win], [],
                                [(D, BF16), (D, BF16), (2 * D, BF16, N_REST, OFF_GA // (2 * D), None)], [], name="gate_bwd")
    d_attn = _mm(dap, w["w_attn_proj"], tb=True, name="mm_attn_proj_dx", out_dtype=BF16)
    gr["w_attn_proj"] = _mm(sv["attn"], dap, ta=True, name="mm_attn_proj_dw")
    d_r = _mm(drp, w["w_ret_proj"], tb=True, name="mm_ret_proj_dx", out_dtype=BF16, tn=2048)
    gr["w_ret_proj"] = _mm(sv["r"], drp, ta=True, name="mm_ret_proj_dw", tm=2048)
    token = call("early_grads", gr)
    d_ret, d_rest, gr["ret_gn_g"], gr["ret_gn_b"] = _rowwise(
        _after(_f_gn_bwd, token), [d_r, sv["ret_raw"], rg_win], [w["ret_gn_g"], w["ret_gn_b"]] + held(token),
        [(RH * RDV, BF16), (RH * RDV, BF16, N_REST, OFF_RG // (RH * RDV), d_rest)],
        [(1, RH * RDV), (1, RH * RDV)], name="gn_bwd", tm=256)
    d_rest = _ret_bwd(proj, cos, sin, rconsts, sv["states"], d_ret, d_rest)
    token = call("after_ret", d_rest)
    (delta,) = _rowwise(_after(_f_delta, token), [d_attn, sv["attn"]], held(token), [(HD, F32)], [], name="attn_delta")
    gw_in, dqkvs = None, []
    for g, dil in enumerate(DILATIONS):
        dqkvs.append(_attn_bwd(sv["qkvs"][g], _to_head_residues(d_attn, dil), _to_residues(sv["lse"], dil),
                               _to_residues(delta, dil), dil, f"attn_bwd_g{g}"))
        gw_in = _qkv_dw(dqkvs[g], _x_view(sv["xb"], dil), g, dil, f"mm_qkv{g}_dw", into=gw_in)
    gw_in = _mm(d_rest, sv["xb"], ta=True, name="mm_proj_dw", out_rows=(N_ATT, N_IN), into=gw_in, blocks8=True)
    gr["w_in"] = gw_in.reshape(N_IN, D)
    token = call("w_in_ready", gr["w_in"])
    dx0 = _mm(d_rest, w["win"], b_rows=(N_ATT, N_REST), name="mm_proj_dx", add=dh1, add_scale=ALPHA, after=token)
    dx_parts = []
    for g, dil in enumerate(DILATIONS):
        if dil == 1:
            dx0 = _qkv_dx(dqkvs[g], w["win"], g, dil, f"mm_qkv{g}_dx", F32, add=dx0)
            token = call("after_dx0", dx0)
        else:
            dx_parts.append(_qkv_dx(dqkvs[g], w["win"], g, dil, f"mm_qkv{g}_dx", BF16, after=token).reshape(dx0.shape))
    return [dx0] + dx_parts, gr


def _local_step(x, p, positions, target, ws, own_hooks=None, on_grads=None):
    half = RDK // 2
    freq = jnp.power(ROPE_BASE, -jnp.arange(half, dtype=F32) / half)
    ang = positions.astype(F32)[:, None] * freq[None, :]
    cos, sin = jnp.cos(ang), jnp.sin(ang)
    rconsts = _ret_consts()
    xb = x.astype(BF16)
    saved, ws = [], list(ws)
    for l in range(DEPTH):
        first, late = ws[l] if isinstance(ws[l], tuple) else (ws[l], None)
        if callable(first):
            first = first(x)
        x, xb, sv, ws[l] = _layer_fwd(x, xb, p[l], first, cos, sin, rconsts, late)
        saved.append(sv)
    dy, loss_vec = _rowwise(_f_loss, [x, target], [], [(D, F32)], [(1, D)], name="loss")
    dys, grads = [dy], [None] * DEPTH
    from_above = {}
    for l in reversed(range(DEPTH)):
        hooks = {**from_above, **(own_hooks(l) if own_hooks else {})}
        dys, grads[l] = _layer_bwd(dys, ws[l], saved[l], cos, sin, rconsts, hooks)
        from_above = on_grads(l, grads[l]) if on_grads else {}
    (grad_x,) = _rowwise(_f_sum, dys, [], [(D, F32)], [], name="grad_x_sum")
    return loss_vec, grad_x, grads


def _pack_rows(arrs):
    parts, where, off = [], [], 0
    for t in arrs:
        t = t.reshape(-1, D)
        rows = t.shape[0]
        padded = -(-rows // 8) * 8
        parts.append(jnp.pad(t, ((0, padded - rows), (0, 0))))
        where.append((off, rows))
        off += padded
    return jnp.concatenate(parts, axis=0), where


FIRST = ("w_in",)
LATER = tuple(n for n in BIG if n not in FIRST)


def _first_weights(g, l, W):
    w = dict(win=g["w_in"].reshape(N_IN, D))
    for n in ("ret_gn_g", "ret_gn_b", "ln1_g", "ln1_b", "ln2_g", "ln2_b"):
        w[n] = W[n][l][None, :]
    return w


def _later_weights(g, l, conv_w_all, conv_b):
    w = dict(w_up=g["w_up"].reshape(2 * DFF, D), w_ple_proj=g["w_ple_proj"].reshape(D, PLE),
             w_attn_proj=g["w_attn_proj"].reshape(D, D), w_ret_proj=g["w_ret_proj"].reshape(RH * RDV, D),
             w_out=g["w_out"].reshape(D, D), w_down=g["w_down"].reshape(DFF, D), w_ple_gate=g["w_ple_gate"].reshape(D, D))
    w["conv_wg"], w["conv_wu"] = conv_w_all[l][:, :DFF], conv_w_all[l][:, DFF:]
    w["conv_bg"], w["conv_bu"] = conv_b[l][None, :DFF], conv_b[l][None, DFF:]
    return w


def _layer_weights(g, l, conv_w_all, conv_b, W):
    return {**_first_weights(g, l, W), **_later_weights(g, l, conv_w_all, conv_b)}


def kernel(x, p, positions, w_in, w_attn_proj, w_ret_proj, ret_gn_g, ret_gn_b, w_out, ln1_g, ln1_b, w_up, conv_w, conv_b, w_down, w_ple_gate, w_ple_proj, ln2_g, ln2_b, loss_target, m_w_in, m_w_attn_proj, m_w_ret_proj, m_ret_gn_g, m_ret_gn_b, m_w_out, m_ln1_g, m_ln1_b, m_w_up, m_conv_w, m_conv_b, m_w_down, m_w_ple_gate, m_w_ple_proj, m_ln2_g, m_ln2_b, v_w_in, v_w_attn_proj, v_w_ret_proj, v_ret_gn_g, v_ret_gn_b, v_w_out, v_ln1_g, v_ln1_b, v_w_up, v_conv_w, v_conv_b, v_w_down, v_w_ple_gate, v_w_ple_proj, v_ln2_g, v_ln2_b):
    W = dict(w_in=w_in, w_attn_proj=w_attn_proj, w_ret_proj=w_ret_proj, ret_gn_g=ret_gn_g, ret_gn_b=ret_gn_b, w_out=w_out,
             ln1_g=ln1_g, ln1_b=ln1_b, w_up=w_up, conv_w=conv_w, conv_b=conv_b, w_down=w_down, w_ple_gate=w_ple_gate,
             w_ple_proj=w_ple_proj, ln2_g=ln2_g, ln2_b=ln2_b)
    M = dict(w_in=m_w_in, w_attn_proj=m_w_attn_proj, w_ret_proj=m_w_ret_proj, ret_gn_g=m_ret_gn_g, ret_gn_b=m_ret_gn_b,
             w_out=m_w_out, ln1_g=m_ln1_g, ln1_b=m_ln1_b, w_up=m_w_up, conv_w=m_conv_w, conv_b=m_conv_b, w_down=m_w_down,
             w_ple_gate=m_w_ple_gate, w_ple_proj=m_w_ple_proj, ln2_g=m_ln2_g, ln2_b=m_ln2_b)
    V = dict(w_in=v_w_in, w_attn_proj=v_w_attn_proj, w_ret_proj=v_w_ret_proj, ret_gn_g=v_ret_gn_g, ret_gn_b=v_ret_gn_b,
             w_out=v_w_out, ln1_g=v_ln1_g, ln1_b=v_ln1_b, w_up=v_w_up, conv_w=v_conv_w, conv_b=v_conv_b, w_down=v_w_down,
             w_ple_gate=v_w_ple_gate, w_ple_proj=v_w_ple_proj, ln2_g=v_ln2_g, ln2_b=v_ln2_b)

    me = 4 * lax.axis_index("x") + 2 * lax.axis_index("y") + lax.axis_index("c")
    shard = lambda n, l: (W[n][l].T if n in COL_SHARDED else W[n][l]).astype(BF16)
    landing = lambda ts: [lax.dynamic_update_index_in_dim(lax.empty((N_DEV,) + t.shape, t.dtype), t, me, 0) for t in ts]
    first0 = _gather_many([shard(n, 0) for n in FIRST], "gather_first_l0")
    later0 = [shard(n, 0) for n in LATER] + [conv_w]
    flight0, token0 = _exchange_start("gather", later0, landing(later0), "gather_later_l0_start", after=first0[0])
    all1 = [shard(n, 1) for n in BIG]
    flight1, token1 = _exchange_start("gather", all1, landing(all1), "gather_weights_l1_start", after=token0)
    conv_w_all = []

    def later_first_layer(after):
        _, got = _exchange_wait(flight0, after, "gather_later_l0_wait")
        conv_w_all.append(got[-1].transpose(1, 2, 0, 3).reshape(DEPTH, 3, 2 * DFF))
        return _later_weights(dict(zip(LATER, got)), 0, conv_w_all[0], conv_b)

    def second_layer(after):
        _, got = _exchange_wait(flight1, after, "gather_weights_l1_wait")
        return _layer_weights(dict(zip(BIG, got)), 1, conv_w_all[0], conv_b, W)

    core = lax.axis_index("c").astype(jnp.int32).reshape(1)
    chip = (2 * lax.axis_index("x") + lax.axis_index("y")).astype(jnp.int32).reshape(1)
    empty_like = lambda ts, slots: [lax.empty((slots,) + t.shape[1:], t.dtype) for t in ts]
    chip32, far = [{} for _ in range(DEPTH)], [{} for _ in range(DEPTH)]
    pending = []

    def reduction(l, names, tag):
        state = {}

        def start(g):
            mine = [g[n].reshape((N_DEV, -1) + g[n].shape[1:]) for n in names]
            state["cores"], token = _exchange_start("cores", mine, empty_like(mine, 4), f"exchange_cores_{tag}_start")
            return token

        def onward(after):
            mine, theirs = _exchange_wait(state["cores"], after, f"exchange_cores_{tag}_wait")
            sums = [_pair_sum(a, b, core, f"pair_sum_l{l}_{n}") for a, b, n in zip(mine, theirs, names)]
            for n, s in zip(names, sums):
                chip32[l][n] = s[0]
            sent = [s[0 if n in F32_OVER_ICI else 1] for s, n in zip(sums, names)]
            flight, token = _exchange_start("chips", sent, empty_like(sent, 3), f"exchange_chips_{tag}_start")
            pending.append((l, names, flight, tag))
            return token

        return start, onward

    def on_grads(l, g):
        if l == 0:
            return {}
        start, onward = reduction(l, BIG, f"l{l}")
        return dict(token=start(g), after_ffn=onward)

    def own_hooks(l):
        if l != 0:
            return {}
        start_e, onward_e = reduction(0, LATER, "l0_later")
        start_w, onward_w = reduction(0, FIRST, "l0_first")
        return dict(early_grads=start_e, after_ret=onward_e, w_in_ready=lambda gw: start_w({"w_in": gw}), after_dx0=onward_w)

    ws = [(_first_weights(dict(zip(FIRST, first0)), 0, W), later_first_layer), second_layer]
    loss_vec, grad_x, grads = _local_step(x[0] + token1[0, 0], p[:, 0], positions[0], loss_target[0], ws, own_hooks, on_grads)
    loss = lax.psum(jnp.sum(loss_vec), ("x", "y", "c"))
    for l, names, flight, tag in pending:
        _, got = _exchange_wait(flight, grad_x, f"exchange_chips_{tag}_wait")
        far[l].update(zip(names, got))
    G, DW, NM, NV = ({} for _ in range(4))
    for n in BIG:
        chip32_n = [chip32[l][n] for l in range(DEPTH)]
        far_n = [far[l][n] for l in range(DEPTH)]
        if n in COL_SHARDED:
            G[n] = _reduce_tail(chip32_n, far_n, chip, f"reduced_{n}")[0].transpose(0, 2, 1)
            R2, C2 = DEPTH * W[n].shape[1], W[n].shape[2]
            res = _adamw(*(t.reshape(R2, C2) for t in (G[n], W[n], M[n], V[n])), f"adamw_{n}")
            DW[n], NM[n], NV[n] = (t.reshape(W[n].shape) for t in res)
        else:
            G[n], DW[n], NM[n], NV[n] = _reduce_tail(chip32_n, far_n, chip, f"adamw_{n}", wmv=(W[n], M[n], V[n]))

    small_names = SMALL + ("conv_w",)
    g_small, where = _pack_rows([jnp.stack([grads[l][n] for l in range(DEPTH)]) for n in small_names])
    (g_all,) = _gather_many([g_small], "gather_small_grads")
    g_small = _sum_slots(g_all, "sum_small_grads")
    for n, (off, rows) in zip(SMALL, where):
        G[n] = g_small[off:off + rows].reshape(W[n].shape)
    off, rows = where[-1]
    g_cw = g_small[off:off + rows].reshape(DEPTH, 3, N_DEV, conv_w.shape[2])
    G["conv_w"] = lax.dynamic_index_in_dim(g_cw, me, axis=2, keepdims=False)
    packed = [_pack_rows([d[n] for n in SMALL]) for d in (G, W, M, V)]
    small_out = _adamw(*(t for t, _ in packed), "adamw_small")
    for res, dst in zip(small_out, (DW, NM, NV)):
        for n, (off, rows) in zip(SMALL, packed[0][1]):
            dst[n] = res[off:off + rows].reshape(W[n].shape)
    two_d = lambda t: t.reshape(DEPTH * 3, conv_w.shape[2])
    cw_out = _adamw(two_d(G["conv_w"]), two_d(conv_w), two_d(m_conv_w), two_d(v_conv_w), "adamw_conv_w")
    for res, dst in zip(cw_out, (DW, NM, NV)):
        dst["conv_w"] = res.reshape(conv_w.shape)

    return (loss, grad_x[None], *[G[n] for n in WEIGHTS], *[DW[n] for n in WEIGHTS], *[NM[n] for n in WEIGHTS],
            *[NV[n] for n in WEIGHTS])
```

```python
import math

import numpy as np
import jax
import jax.numpy as jnp
from jax import lax
from jax.experimental import pallas as pl
from jax.experimental.pallas import tpu as pltpu

F32, BF16 = jnp.float32, jnp.bfloat16

D = 1024
DEPTH = 2
N_DEV = 8
HD = 128
NH = 8
DILATIONS = (1, 4, 16)
SPAN = 128
N_ATT = 3 * 3 * NH * HD
RH, RDK, RDV = 4, 256, 512
CH = 128
DFF = 2816
PLE = 256
N_IN = 17408
N_REST = N_IN - N_ATT
OFF_RQ, OFF_RK, OFF_RV, OFF_RG, OFF_GA, OFF_GR = 0, 1024, 2048, 4096, 6144, 7168
ALPHA = (2 * DEPTH) ** 0.25
LN_EPS, GN_EPS = 1e-5, 1e-6
ROPE_BASE = 10000.0
LR, B1, B2, EPS, WD, STEP = 0.001, 0.9, 0.999, 1e-8, 0.01, 10
VMEM_LIMIT = 48 * 1024 * 1024
NEG = -1e30

BIG = ("w_in", "w_attn_proj", "w_ret_proj", "w_out", "w_up", "w_down", "w_ple_gate", "w_ple_proj")
COL_SHARDED = ("w_in", "w_up", "w_ple_proj")
F32_OVER_ICI = ("w_attn_proj", "w_out", "w_ple_gate", "w_ple_proj")
SMALL = ("ret_gn_g", "ret_gn_b", "ln1_g", "ln1_b", "conv_b", "ln2_g", "ln2_b")
WEIGHTS = ("w_in", "w_attn_proj", "w_ret_proj", "ret_gn_g", "ret_gn_b", "w_out", "ln1_g", "ln1_b", "w_up",
           "conv_w", "conv_b", "w_down", "w_ple_gate", "w_ple_proj", "ln2_g", "ln2_b")


def _tile(n, cap, mult=128):
    if n <= cap:
        return n
    t = (cap // mult) * mult
    while n % t:
        t -= mult
    return t


def _cparams(sem):
    return pltpu.CompilerParams(dimension_semantics=sem, vmem_limit_bytes=VMEM_LIMIT)


def _dot(a, b, ca, cb):
    return lax.dot_general(a, b, (((ca,), (cb,)), ((), ())), preferred_element_type=F32)


def _bdot(a, b, ca, cb):
    return lax.dot_general(a, b, (((ca,), (cb,)), ((0,), (0,))), preferred_element_type=F32)


def _mm(a, b, *, name, ta=False, tb=False, out_dtype=F32, add=None, add_scale=1.0, tm=1024, tn=1024, tk=1024,
        b_rows=None, out_rows=None, into=None, blocks8=False, after=None):
    M, K = (a.shape[1], a.shape[0]) if ta else a.shape
    b_first, b_count = b_rows if b_rows else (0, b.shape[0])
    N = b_count if tb else b.shape[1]
    assert K == (b.shape[1] if tb else b_count)
    tm, tn, tk = _tile(M, tm), _tile(N, tn), _tile(K, tk)
    nk = K // tk
    o_first, o_total = out_rows if out_rows else (0, M)
    jb, kb, io = (b_first // tn, 0, o_first // tm) if tb else (0, b_first // tk, o_first // tm)
    assert b_first % (tn if tb else tk) == 0 and o_first % tm == 0 and (add is None or out_rows is None)

    def body(*refs):
        if add is None:
            a_ref, b_ref = refs[:2]
        else:
            a_ref, b_ref, add_ref = refs[:3]
        o_ref, acc_ref = refs[-2:]
        k = pl.program_id(2)

        @pl.when(k == 0)
        def _():
            acc_ref[...] = jnp.zeros_like(acc_ref)

        acc_ref[...] += _dot(a_ref[...].astype(BF16), b_ref[...].astype(BF16), 0 if ta else 1, 1 if tb else 0)

        @pl.when(k == nk - 1)
        def _():
            r = acc_ref[...]
            if add is not None:
                r = r + add_scale * add_ref[...].astype(F32)
            o_ref[...] = r.astype(out_dtype).reshape(o_ref.shape)

    a_spec = pl.BlockSpec((tk, tm), lambda i, j, k: (k, i)) if ta else pl.BlockSpec((tm, tk), lambda i, j, k: (i, k))
    if tb:
        b_spec = pl.BlockSpec((tn, tk), lambda i, j, k: (j + jb, k))
    else:
        b_spec = pl.BlockSpec((tk, tn), lambda i, j, k: (k + kb, j))
    if blocks8:
        assert tm == 1024
        o_spec = pl.BlockSpec((1, 8, 128, tn), lambda i, j, k: (i + io, 0, 0, j))
        o_shape = (o_total // tm, 8, 128, N)
    else:
        o_spec = pl.BlockSpec((tm, tn), lambda i, j, k: (i + io, j))
        o_shape = (o_total, N)
    in_specs, args, aliases = [a_spec, b_spec], [a, b], {}
    if add is not None:
        in_specs.append(o_spec)
        args.append(add)
    if after is not None:
        in_specs.append(pl.BlockSpec(memory_space=pl.ANY))
        args.append(after)
    if into is not None:
        aliases = {len(args): 0}
        in_specs.append(pl.BlockSpec(memory_space=pl.ANY))
        args.append(into)
    return pl.pallas_call(
        body, name=name, grid=(M // tm, N // tn, nk), in_specs=in_specs, out_specs=o_spec,
        out_shape=jax.ShapeDtypeStruct(o_shape, out_dtype), scratch_shapes=[pltpu.VMEM((tm, tn), F32)],
        input_output_aliases=aliases, compiler_params=_cparams(("parallel", "parallel", "arbitrary")),
    )(*args)


def _rowwise(fn, rows, pars, outs, accs, *, name, tm=512):
    first = rows[0][0] if isinstance(rows[0], tuple) else rows[0]
    S = first.shape[-2]
    tm = _tile(S, tm, 16)
    n_r, n_p, n_o = len(rows), len(pars), len(outs)
    outs = [o if len(o) == 5 else (o[0], o[1], o[0], 0, None) for o in outs]
    intos = [(k, o[4]) for k, o in enumerate(outs) if o[4] is not None]
    n_i = len(intos)
    dilated = {k: r[2] for k, r in enumerate(rows) if isinstance(r, tuple) and r[0] == "dilated"}

    def body(*refs):
        i = pl.program_id(0)
        vals = [r[...] for r in refs[:n_r + n_p]]
        for scr, (k, d) in zip(refs[len(refs) - len(dilated):], dilated.items()):
            w = vals[k].shape[1] // d
            for r in range(d):
                for c in range(w // 128):
                    scr.at[c][pl.ds(r, tm // d, stride=d), :] = vals[k][:, r * w + c * 128:r * w + (c + 1) * 128].astype(F32)
            vals[k] = jnp.concatenate([scr[c] for c in range(w // 128)], axis=1)
        res = fn(*vals)
        if not isinstance(res, (tuple, list)):
            res = (res,)
        o_refs = refs[n_r + n_p + n_i:n_r + n_p + n_i + n_o]
        a_refs = refs[n_r + n_p + n_i + n_o:len(refs) - len(dilated)]
        for r, v in zip(o_refs, res[:n_o]):
            r[...] = v.astype(r.dtype)
        if a_refs:
            @pl.when(i == 0)
            def _():
                for r in a_refs:
                    r[...] = jnp.zeros_like(r)

            for r, v in zip(a_refs, res[n_o:]):
                r[...] += v

    in_specs, args = [], []
    for r in rows:
        if isinstance(r, tuple) and r[0] == "dilated":
            _, arr, d = r
            in_specs.append(pl.BlockSpec((tm // d, arr.shape[1]), lambda i: (i, 0)))
        elif isinstance(r, tuple):
            arr, w, cb = r
            in_specs.append(pl.BlockSpec((tm, w), lambda i, cb=cb: (i, cb)))
        elif r.ndim == 3:
            arr = r
            in_specs.append(pl.BlockSpec((arr.shape[0], tm, arr.shape[2]), lambda i: (0, i, 0)))
        else:
            arr = r
            in_specs.append(pl.BlockSpec((tm, arr.shape[1]), lambda i: (i, 0)))
        args.append(arr)
    for p_ in pars:
        in_specs.append(pl.BlockSpec(p_.shape, lambda i: (0, 0)))
        args.append(p_)
    aliases = {}
    for k, arr in intos:
        aliases[len(args)] = k
        in_specs.append(pl.BlockSpec(memory_space=pl.ANY))
        args.append(arr)
    out_shape = [jax.ShapeDtypeStruct((S, o[2]), o[1]) for o in outs] + [jax.ShapeDtypeStruct(a, F32) for a in accs]
    out_specs = [pl.BlockSpec((tm, o[0]), lambda i, cb=o[3]: (i, cb)) for o in outs] + [pl.BlockSpec(a, lambda i: (0, 0)) for a in accs]
    scratch = [pltpu.VMEM((rows[k][1].shape[1] // d // 128, tm, 128), F32) for k, d in dilated.items()]
    return pl.pallas_call(
        body, name=name, grid=(S // tm,), in_specs=in_specs, out_specs=out_specs, out_shape=out_shape,
        scratch_shapes=scratch, input_output_aliases=aliases,
        compiler_params=_cparams(("arbitrary",) if accs else ("parallel",)),
    )(*args)


def _norm(h, eps):
    mu = jnp.mean(h, -1, keepdims=True)
    d = h - mu
    rstd = lax.rsqrt(jnp.mean(d * d, -1, keepdims=True) + eps)
    return d * rstd, rstd


def _norm_bwd(dxh, xh, rstd):
    return rstd * (dxh - jnp.mean(dxh, -1, keepdims=True) - xh * jnp.mean(dxh * xh, -1, keepdims=True))


def _sig(x):
    return 1.0 / (1.0 + jnp.exp(-x))


_GELU_C = math.sqrt(2.0 / math.pi)


def _gelu(x, with_grad=False):
    x2 = x * x
    t = jnp.tanh(x * (_GELU_C + (_GELU_C * 0.044715) * x2))
    half_x, one_t = 0.5 * x, 1.0 + t
    if not with_grad:
        return half_x * one_t
    return half_x * one_t, 0.5 * one_t + half_x * (1.0 - t * t) * (_GELU_C + (3 * _GELU_C * 0.044715) * x2)


def _f_ln1(x, mix, g, b):
    h = ALPHA * x + mix
    xh, _ = _norm(h, LN_EPS)
    y = xh * g + b
    return h, y, y


def _f_ln2(x, ffn, z, pp, g, b):
    h = ALPHA * x + ffn + _sig(z) * pp
    xh, _ = _norm(h, LN_EPS)
    y = xh * g + b
    return h, y, y


def _f_ln_bwd(*args):
    *dys, h, g = args
    dy = dys[0]
    for t in dys[1:]:
        dy = dy + t
    xh, rstd = _norm(h, LN_EPS)
    dh = _norm_bwd(dy * g, xh, rstd)
    return dh, dh, jnp.sum(dy * xh, 0, keepdims=True), jnp.sum(dy, 0, keepdims=True)


def _f_sum(*ts):
    r = ts[0]
    for t in ts[1:]:
        r = r + t
    return r


def _f_loss(y, t):
    e = y - t
    return e * (1.0 / D), jnp.sum(e * e, 0, keepdims=True) * (0.5 / D)


def _head_col(c, h):
    lane = lax.broadcasted_iota(jnp.int32, c.shape, 1)
    return jnp.sum(jnp.where(lane == h, c, 0.0), -1, keepdims=True)


def _f_combine(o0, o1, o2, l0, l1, l2):
    m = jnp.maximum(jnp.maximum(l0, l1), l2)
    e0, e1, e2 = jnp.exp(l0 - m), jnp.exp(l1 - m), jnp.exp(l2 - m)
    den = e0 + e1 + e2
    inv = 1.0 / den
    w0, w1, w2 = e0 * inv, e1 * inv, e2 * inv
    parts = [_head_col(w0, h) * o0[h].astype(F32) + _head_col(w1, h) * o1[h].astype(F32) + _head_col(w2, h) * o2[h].astype(F32)
             for h in range(NH)]
    return jnp.concatenate(parts, axis=1), m + jnp.log(den)


def _f_delta(da, a):
    lane = lax.broadcasted_iota(jnp.int32, (da.shape[0], HD), 1)
    out = jnp.zeros((da.shape[0], HD), F32)
    for h in range(NH):
        sl = slice(h * HD, (h + 1) * HD)
        s = jnp.sum(da[:, sl].astype(F32) * a[:, sl].astype(F32), -1, keepdims=True)
        out = jnp.where(lane == h, s, out)
    return out


def _f_gate(ap, rp, ga, gr):
    return _sig(ga.astype(F32)) * ap.astype(F32) + _sig(gr.astype(F32)) * rp.astype(F32)


def _f_gate_bwd(dm, ap, rp, ga, gr):
    dm = dm.astype(F32)
    sa, sr = _sig(ga.astype(F32)), _sig(gr.astype(F32))
    dga, dgr = dm * ap.astype(F32) * sa * (1.0 - sa), dm * rp.astype(F32) * sr * (1.0 - sr)
    return dm * sa, dm * sr, jnp.concatenate([dga, dgr], axis=1)


def _f_gn(y, rg, g, b):
    y, rg = y.astype(F32), rg.astype(F32)
    parts = []
    for h in range(RH):
        sl = slice(h * RDV, (h + 1) * RDV)
        xh, _ = _norm(y[:, sl], GN_EPS)
        parts.append(xh * g[:, sl] + b[:, sl])
    return rg * _sig(rg) * jnp.concatenate(parts, axis=1)


def _f_gn_bwd(dr, y, rg, g, b):
    dr, y, rg = dr.astype(F32), y.astype(F32), rg.astype(F32)
    s = _sig(rg)
    d_out = dr * rg * s
    dys, outs, xhs = [], [], []
    for h in range(RH):
        sl = slice(h * RDV, (h + 1) * RDV)
        xh, rstd = _norm(y[:, sl], GN_EPS)
        xhs.append(xh)
        outs.append(xh * g[:, sl] + b[:, sl])
        dys.append(_norm_bwd(d_out[:, sl] * g[:, sl], xh, rstd))
    xh, out = jnp.concatenate(xhs, axis=1), jnp.concatenate(outs, axis=1)
    d_rg = dr * out * s * (1.0 + rg * (1.0 - s))
    return jnp.concatenate(dys, axis=1), d_rg, jnp.sum(d_out * xh, 0, keepdims=True), jnp.sum(d_out, 0, keepdims=True)


def _f_ln2_bwd(*args):
    *dys, h, z, pp, g = args
    dh, dhb, dg, db = _f_ln_bwd(*dys, h, g)
    s = _sig(z)
    return dh, dhb, dh * s, dh * pp * s * (1.0 - s), dg, db


QKV = 3 * HD


def _to_tokens(t, d):
    if d == 1:
        return t
    *lead, S, C = t.shape
    n = len(lead)
    perm = tuple(range(n)) + (n + 1, n, n + 2)
    return t.reshape(*lead, d, S // d, C).transpose(perm).reshape(*lead, S, C)


def _to_residues(t, d):
    if d == 1:
        return t
    S, C = t.shape
    return t.reshape(S // d, d, C).transpose(1, 0, 2).reshape(S, C)


def _to_head_residues(t, d):
    S = t.shape[0]
    return t.reshape(S // d, d, NH, HD).transpose(2, 1, 0, 3).reshape(NH, S, HD)


def _w_qkv_specs(g):
    return [pl.BlockSpec((D, D), lambda *i, t=t: (3 * g + t, 0)) for t in range(3)]


def _qkv_fwd(xv, win, g, dil, name):
    Sd = xv.shape[0]
    S = Sd * dil
    tm = min(512, Sd)
    nma = Sd // tm

    def body(a_ref, wq_ref, wk_ref, wv_ref, o_ref):
        a = a_ref[...]
        q, k, v = (_dot(a, w_ref[...], 1, 1).astype(BF16) for w_ref in (wq_ref, wk_ref, wv_ref))
        for h in range(NH):
            sl = slice(h * HD, (h + 1) * HD)
            o_ref[h] = jnp.concatenate([q[:, sl], k[:, sl], v[:, sl]], axis=1)

    return pl.pallas_call(
        body, name=name, grid=(S // tm,),
        in_specs=[pl.BlockSpec((tm, D), lambda i: (i % nma, i // nma))] + _w_qkv_specs(g),
        out_specs=pl.BlockSpec((NH, tm, QKV), lambda i: (0, i, 0)), out_shape=jax.ShapeDtypeStruct((NH, S, QKV), BF16),
        compiler_params=_cparams(("parallel",)),
    )(xv, win, win, win)


def _qkv_dx(dqkv, win, g, dil, name, out_dtype, add=None, after=None):
    S = dqkv.shape[1]
    Sd = S // dil
    tm = min(512, Sd)
    nmo = Sd // tm

    def body(*refs):
        a_ref, wq_ref, wk_ref, wv_ref = refs[:4]
        o_ref = refs[-1]
        acc = None
        for t, w_ref in enumerate((wq_ref, wk_ref, wv_ref)):
            d = jnp.concatenate([a_ref[h][:, t * HD:(t + 1) * HD] for h in range(NH)], axis=1)
            part = _dot(d, w_ref[...], 1, 0)
            acc = part if acc is None else acc + part
        if add is not None:
            acc = acc + refs[4][...]
        o_ref[...] = acc.astype(out_dtype)

    o_spec = pl.BlockSpec((tm, D), lambda i: (i % nmo, i // nmo))
    in_specs = [pl.BlockSpec((NH, tm, QKV), lambda i: (0, i, 0))] + _w_qkv_specs(g)
    args = [dqkv, win, win, win]
    if add is not None:
        assert dil == 1
        in_specs.append(o_spec)
        args.append(add)
    if after is not None:
        in_specs.append(pl.BlockSpec(memory_space=pl.ANY))
        args.append(after)
    return pl.pallas_call(
        body, name=name, grid=(S // tm,), in_specs=in_specs, out_specs=o_spec,
        out_shape=jax.ShapeDtypeStruct((Sd, dil * D), out_dtype), compiler_params=_cparams(("parallel",)),
    )(*args)


GW_IN_BLOCKS = (N_IN // D, NH, HD, D)


def _qkv_dw(dqkv, xv, g, dil, name, into=None):
    S = dqkv.shape[1]
    Sd = S // dil
    tk = min(1024, Sd)
    nkb, nk = Sd // tk, S // tk
    hh = NH // 2

    def body(*refs):
        a_ref, b_ref = refs[:2]
        o_ref, acc_ref = refs[-2:]
        k = pl.program_id(1)

        @pl.when(k == 0)
        def _():
            acc_ref[...] = jnp.zeros_like(acc_ref)

        b = b_ref[...]
        for h in range(hh):
            acc_ref[h * QKV:(h + 1) * QKV, :] += _dot(a_ref[h], b, 0, 0)

        @pl.when(k == nk - 1)
        def _():
            for h in range(hh):
                for t in range(3):
                    o_ref[t, h] = acc_ref[h * QKV + t * HD:h * QKV + (t + 1) * HD, :]

    in_specs = [pl.BlockSpec((hh, tk, QKV), lambda j, k: (j, k, 0)), pl.BlockSpec((tk, D), lambda j, k: (k % nkb, k // nkb))]
    args, aliases = [dqkv, xv], {}
    if into is not None:
        aliases = {2: 0}
        in_specs.append(pl.BlockSpec(memory_space=pl.ANY))
        args.append(into)
    return pl.pallas_call(
        body, name=name, grid=(2, nk), in_specs=in_specs,
        out_specs=pl.BlockSpec((3, hh, HD, D), lambda j, k: (g, j, 0, 0)), out_shape=jax.ShapeDtypeStruct(GW_IN_BLOCKS, F32),
        input_output_aliases=aliases, scratch_shapes=[pltpu.VMEM((hh * QKV, D), F32)],
        compiler_params=_cparams(("parallel", "arbitrary")),
    )(*args)


def _band(nb, first_valid, last_valid=None):
    b = lax.broadcasted_iota(jnp.int32, (nb, SPAN, SPAN), 0)
    row = lax.broadcasted_iota(jnp.int32, (nb, SPAN, SPAN), 1)
    col = lax.broadcasted_iota(jnp.int32, (nb, SPAN, SPAN), 2)
    off = jnp.where(b == 0, jnp.where(first_valid, 0, 2 * SPAN), 0)
    if last_valid is not None:
        off = off + jnp.where(b == nb - 1, jnp.where(last_valid, 0, 2 * SPAN), 0)
    return col <= row, col >= row + off


def _attn_tiles(S, dil):
    Sd = S // dil
    T = min(1024, Sd)
    hp = min(NH, max(1, (S // T) * NH // 16))
    return Sd, T, T // SPAN, Sd // T, hp


def _attn_fwd(qkv, dil, name):
    S = qkv.shape[1]
    Sd, T, nsub, nib, hp = _attn_tiles(S, dil)
    scale = HD ** -0.5

    def body(c_ref, p_ref, o_ref, l_ref):
        ib, hb = pl.program_id(1), pl.program_id(2)
        m_cur, m_prev = _band(nsub, ib > 0)
        lane = lax.broadcasted_iota(jnp.int32, (T, HD), 1)

        @pl.when(hb == 0)
        def _():
            l_ref[...] = jnp.zeros_like(l_ref)

        lses = l_ref[...]
        for hh in range(hp):
            blk, hal = c_ref[hh], p_ref[hh]
            q, k, v = blk[:, :HD], blk[:, HD:2 * HD], blk[:, 2 * HD:]
            if nsub > 1:
                kp = jnp.concatenate([hal[:, HD:2 * HD], k[:T - SPAN]], axis=0)
                vp = jnp.concatenate([hal[:, 2 * HD:], v[:T - SPAN]], axis=0)
            else:
                kp, vp = hal[:, HD:2 * HD], hal[:, 2 * HD:]
            q3, k3, v3, kp3, vp3 = (t.reshape(nsub, SPAN, HD) for t in (q, k, v, kp, vp))
            sc = jnp.where(m_cur, _bdot(q3, k3, 2, 2) * scale, NEG)
            sp = jnp.where(m_prev, _bdot(q3, kp3, 2, 2) * scale, NEG)
            m = jnp.maximum(jnp.max(sc, -1, keepdims=True), jnp.max(sp, -1, keepdims=True))
            pc, pp = jnp.exp(sc - m), jnp.exp(sp - m)
            den = jnp.sum(pc, -1, keepdims=True) + jnp.sum(pp, -1, keepdims=True)
            o = (_bdot(pc.astype(BF16), v3, 2, 1) + _bdot(pp.astype(BF16), vp3, 2, 1)) * (1.0 / den)
            o_ref[hh] = o.reshape(T, HD).astype(BF16)
            lses = jnp.where(lane == hb * hp + hh, (m + jnp.log(den)).reshape(T, 1), lses)
        l_ref[...] = lses

    cur = pl.BlockSpec((hp, T, QKV), lambda r, ib, h: (h, r * nib + ib, 0))
    prev = pl.BlockSpec((hp, SPAN, QKV), lambda r, ib, h: (h, r * (Sd // SPAN) + jnp.maximum(ib * nsub - 1, 0), 0))
    return pl.pallas_call(
        body, name=name, grid=(dil, nib, NH // hp), in_specs=[cur, prev],
        out_specs=[pl.BlockSpec((hp, T, HD), lambda r, ib, h: (h, r * nib + ib, 0)),
                   pl.BlockSpec((T, HD), lambda r, ib, h: (r * nib + ib, 0))],
        out_shape=[jax.ShapeDtypeStruct((NH, S, HD), BF16), jax.ShapeDtypeStruct((S, HD), F32)],
        compiler_params=_cparams(("parallel", "parallel", "arbitrary")),
    )(qkv, qkv)


def _attn_bwd(qkv, d_attn, lse, delta, dil, name):
    S = qkv.shape[1]
    Sd, T, nsub, nib, hp = _attn_tiles(S, dil)
    scale = HD ** -0.5
    ne = nsub + 1

    def body(c_ref, p_ref, n_ref, do_ref, don_ref, l_ref, ln_ref, dl_ref, dln_ref, o_ref):
        ib, hb = pl.program_id(1), pl.program_id(2)
        _, m_prev = _band(ne, ib > 0, ib < nib - 1)
        m_cur, _ = _band(nsub, True)
        for hh in range(hp):
            h = hb * hp + hh
            blk, hal, nxt = c_ref[hh], p_ref[hh], n_ref[hh]
            q, k, v = blk[:, :HD], blk[:, HD:2 * HD], blk[:, 2 * HD:]
            do = do_ref[hh]
            l, dl = _head_col(l_ref[...], h), _head_col(dl_ref[...], h)
            qe = jnp.concatenate([q, nxt[:, :HD]], axis=0).reshape(ne, SPAN, HD)
            doe = jnp.concatenate([do, don_ref[hh]], axis=0).reshape(ne, SPAN, HD)
            le = jnp.concatenate([l, _head_col(ln_ref[...], h)], axis=0).reshape(ne, SPAN, 1)
            dle = jnp.concatenate([dl, _head_col(dln_ref[...], h)], axis=0).reshape(ne, SPAN, 1)
            kpe = jnp.concatenate([hal[:, HD:2 * HD], k], axis=0).reshape(ne, SPAN, HD)
            vpe = jnp.concatenate([hal[:, 2 * HD:], v], axis=0).reshape(ne, SPAN, HD)
            p = jnp.where(m_prev, jnp.exp(_bdot(qe, kpe, 2, 2) * scale - le), 0.0)
            ds = (p * (_bdot(doe, vpe, 2, 2) - dle)).astype(BF16)
            dq = _bdot(ds, kpe, 2, 1)[:nsub]
            dk = _bdot(ds, qe, 1, 1)[1:]
            dv = _bdot(p.astype(BF16), doe, 1, 1)[1:]
            q3, k3, v3, do3 = (t.reshape(nsub, SPAN, HD) for t in (q, k, v, do))
            l3, dl3 = l.reshape(nsub, SPAN, 1), dl.reshape(nsub, SPAN, 1)
            p = jnp.where(m_cur, jnp.exp(_bdot(q3, k3, 2, 2) * scale - l3), 0.0)
            ds = (p * (_bdot(do3, v3, 2, 2) - dl3)).astype(BF16)
            dq = (dq + _bdot(ds, k3, 2, 1)) * scale
            dk = (dk + _bdot(ds, q3, 1, 1)) * scale
            dv = dv + _bdot(p.astype(BF16), do3, 1, 1)
            o_ref[hh] = jnp.concatenate([t.reshape(T, HD) for t in (dq, dk, dv)], axis=1).astype(BF16)

    nb = Sd // SPAN
    row = lambda r, ib: r * nib + ib
    prow = lambda r, ib: r * nb + jnp.maximum(ib * nsub - 1, 0)
    nrow = lambda r, ib: r * nb + jnp.minimum((ib + 1) * nsub, nb - 1)
    cur3 = pl.BlockSpec((hp, T, QKV), lambda r, ib, h: (h, row(r, ib), 0))
    prev3 = pl.BlockSpec((hp, SPAN, QKV), lambda r, ib, h: (h, prow(r, ib), 0))
    next3 = pl.BlockSpec((hp, SPAN, QKV), lambda r, ib, h: (h, nrow(r, ib), 0))
    cur1 = pl.BlockSpec((hp, T, HD), lambda r, ib, h: (h, row(r, ib), 0))
    next1 = pl.BlockSpec((hp, SPAN, HD), lambda r, ib, h: (h, nrow(r, ib), 0))
    curc = pl.BlockSpec((T, HD), lambda r, ib, h: (row(r, ib), 0))
    nextc = pl.BlockSpec((SPAN, HD), lambda r, ib, h: (nrow(r, ib), 0))
    return pl.pallas_call(
        body, name=name, grid=(dil, nib, NH // hp),
        in_specs=[cur3, prev3, next3, cur1, next1, curc, nextc, curc, nextc], out_specs=cur3,
        out_shape=jax.ShapeDtypeStruct((NH, S, QKV), BF16),
        compiler_params=_cparams(("parallel", "parallel", "parallel")),
    )(qkv, qkv, qkv, d_attn, d_attn, lse, lse, delta, delta)


def _ret_consts():
    lg = np.log1p(-np.exp2(-5.0 - np.arange(RH, dtype=np.float64)))
    idx = np.arange(CH, dtype=np.float64)
    rel = idx[:, None] - idx[None, :]
    intra = np.where(rel >= 0, np.exp(lg[:, None, None] * np.maximum(rel, 0.0)), 0.0)
    qd = np.exp(lg[:, None] * (idx + 1.0))
    kd = np.exp(lg[:, None] * (CH - 1.0 - idx))
    cd = np.exp(lg * CH)
    wide = lambda t: np.broadcast_to(t[:, :, None], (RH, t.shape[1], RDV))
    return (jnp.asarray(intra, F32), jnp.asarray(wide(qd), F32), jnp.asarray(wide(kd), F32),
            jnp.asarray(np.broadcast_to(cd[:, None, None], (RH, 1, RDV)), F32))


def _rot(t, c, s):
    t1, t2 = t[:, :RDK // 2], t[:, RDK // 2:]
    return jnp.concatenate([t1 * c - t2 * s, t1 * s + t2 * c], axis=1)


def _unrot(d, c, s):
    d1, d2 = d[:, :RDK // 2], d[:, RDK // 2:]
    return jnp.concatenate([d1 * c + d2 * s, d2 * c - d1 * s], axis=1)


RCH = 4


def _ret_specs(nmap):
    rows = RCH * CH
    q = pl.BlockSpec((rows, RH * RDK), lambda n: (nmap(n), OFF_RQ // (RH * RDK)))
    k = pl.BlockSpec((rows, RH * RDK), lambda n: (nmap(n), OFF_RK // (RH * RDK)))
    v = pl.BlockSpec((rows, RH * RDV), lambda n: (nmap(n), OFF_RV // (RH * RDV)))
    cs = pl.BlockSpec((rows, RDK // 2), lambda n: (nmap(n), 0))
    dmat = pl.BlockSpec((RH, CH, CH), lambda n: (0, 0, 0))
    dvec = pl.BlockSpec((RH, CH, RDV), lambda n: (0, 0, 0))
    cdv = pl.BlockSpec((RH, 1, RDV), lambda n: (0, 0, 0))
    state = pl.BlockSpec((RH, RCH, RDK, RDV), lambda n: (0, nmap(n), 0, 0))
    out = pl.BlockSpec((rows, RH * RDV), lambda n: (nmap(n), 0))
    return [q, k, v, cs, cs, dmat, dvec, dvec, cdv], state, out


def _ret_fwd(proj, cos, sin, consts):
    S = proj.shape[0]
    nc = S // CH

    def body(q_ref, k_ref, v_ref, c_ref, s_ref, d_ref, qd_ref, kd_ref, cd_ref, o_ref, st_ref, state):
        @pl.when(pl.program_id(0) == 0)
        def _():
            state[...] = jnp.zeros_like(state)

        for ci in range(RCH):
            rows = slice(ci * CH, (ci + 1) * CH)
            c, s = c_ref[rows, :], s_ref[rows, :]
            for h in range(RH):
                qk, vv = slice(h * RDK, (h + 1) * RDK), slice(h * RDV, (h + 1) * RDV)
                qb = _rot(q_ref[rows, qk].astype(F32), c, s).astype(BF16)
                kb = (_rot(k_ref[rows, qk].astype(F32), c, s) * (RDK ** -0.5)).astype(BF16)
                vb = v_ref[rows, vv]
                sb = state[h].astype(BF16)
                st_ref[h, ci] = sb
                a = (_dot(qb, kb, 1, 1) * d_ref[h]).astype(BF16)
                o_ref[rows, vv] = (_dot(a, vb, 1, 0) + _dot(qb, sb, 1, 0) * qd_ref[h]).astype(BF16)
                vk = (vb.astype(F32) * kd_ref[h]).astype(BF16)
                state[h] = cd_ref[h] * state[h] + _dot(kb, vk, 0, 0)

    ins, state_spec, out_spec = _ret_specs(lambda n: n)
    return pl.pallas_call(
        body, name="ret_fwd", grid=(nc // RCH,), in_specs=ins, out_specs=[out_spec, state_spec],
        out_shape=[jax.ShapeDtypeStruct((S, RH * RDV), BF16), jax.ShapeDtypeStruct((RH, nc, RDK, RDV), BF16)],
        scratch_shapes=[pltpu.VMEM((RH, RDK, RDV), F32)],
        compiler_params=_cparams(("arbitrary",)),
    )(proj, proj, proj, cos, sin, *consts)


def _ret_bwd(proj, cos, sin, consts, states, d_ret, d_rest):
    S = proj.shape[0]
    nc = S // CH

    def body(q_ref, k_ref, v_ref, c_ref, s_ref, d_ref, qd_ref, kd_ref, cd_ref, st_ref, do_ref, _, o_ref, dstate):
        @pl.when(pl.program_id(0) == 0)
        def _():
            dstate[...] = jnp.zeros_like(dstate)

        for ci in reversed(range(RCH)):
            rows = slice(ci * CH, (ci + 1) * CH)
            c, s = c_ref[rows, :], s_ref[rows, :]
            for h in range(RH):
                qk, vv = slice(h * RDK, (h + 1) * RDK), slice(h * RDV, (h + 1) * RDV)
                qb = _rot(q_ref[rows, qk].astype(F32), c, s).astype(BF16)
                kb = (_rot(k_ref[rows, qk].astype(F32), c, s) * (RDK ** -0.5)).astype(BF16)
                vb, sb, do = v_ref[rows, vv], st_ref[h, ci], do_ref[rows, vv]
                dmat, qd, kd = d_ref[h], qd_ref[h], kd_ref[h]
                a = (_dot(qb, kb, 1, 1) * dmat).astype(BF16)
                doq = (do.astype(F32) * qd).astype(BF16)
                dsb = dstate[h].astype(BF16)
                vk = (vb.astype(F32) * kd).astype(BF16)
                o_ref[rows, OFF_RV + h * RDV:OFF_RV + (h + 1) * RDV] = (_dot(a, do, 0, 0) + _dot(kb, dsb, 1, 0) * kd).astype(BF16)
                da = (_dot(do, vb, 1, 1) * dmat).astype(BF16)
                dq = _dot(da, kb, 1, 0) + _dot(doq, sb, 1, 1)
                dk = (_dot(da, qb, 0, 0) + _dot(vk, dsb, 1, 1)) * (RDK ** -0.5)
                o_ref[rows, OFF_RQ + h * RDK:OFF_RQ + (h + 1) * RDK] = _unrot(dq, c, s).astype(BF16)
                o_ref[rows, OFF_RK + h * RDK:OFF_RK + (h + 1) * RDK] = _unrot(dk, c, s).astype(BF16)
                dstate[h] = cd_ref[h] * dstate[h] + _dot(qb, doq, 0, 0)

    nsteps = nc // RCH
    rev = lambda n: nsteps - 1 - n
    ins, state_spec, out_spec = _ret_specs(rev)
    return pl.pallas_call(
        body, name="ret_bwd", grid=(nsteps,), in_specs=ins + [state_spec, out_spec, pl.BlockSpec(memory_space=pl.ANY)],
        out_specs=pl.BlockSpec((RCH * CH, OFF_RG), lambda n: (rev(n), 0)),
        out_shape=jax.ShapeDtypeStruct(d_rest.shape, BF16), input_output_aliases={11: 0},
        scratch_shapes=[pltpu.VMEM((RH, RDK, RDV), F32)],
        compiler_params=_cparams(("arbitrary",)),
    )(proj, proj, proj, cos, sin, *consts, states, d_ret, d_rest)


CW = 256
HALO = 16


def _shift_down(v, halo, k):
    rolled = pltpu.roll(v, k, 0)
    hr = pltpu.roll(halo, k, 0)[0:8]
    row = lax.broadcasted_iota(jnp.int32, hr.shape, 0)
    return jnp.concatenate([jnp.where(row < k, hr, rolled[0:8]), rolled[8:]], axis=0)


def _shift_up(v, halo, k):
    T = v.shape[0]
    rolled = pltpu.roll(v, T - k, 0)
    hr = pltpu.roll(halo, 8 - k, 0)[0:8]
    row = lax.broadcasted_iota(jnp.int32, hr.shape, 0)
    return jnp.concatenate([rolled[:T - 8], jnp.where(row >= 8 - k, hr, rolled[T - 8:])], axis=0)


def _conv_taps(h_ref, hp_ref, first):
    h = h_ref[...].astype(F32)
    hp = hp_ref[...].astype(F32) * jnp.where(first, 0.0, 1.0)
    return _shift_down(h, hp, 2), _shift_down(h, hp, 1), h


def _conv_specs(S, T, cw=CW):
    nj = DFF // cw
    cur = pl.BlockSpec((T, cw), lambda j, i: (i, j))
    prev = pl.BlockSpec((HALO, cw), lambda j, i: (jnp.maximum(i * (T // HALO) - 1, 0), j))
    nxt = pl.BlockSpec((HALO, cw), lambda j, i: (jnp.minimum((i + 1) * (T // HALO), S // HALO - 1), j))
    w = pl.BlockSpec((3, cw), lambda j, i: (0, j))
    b = pl.BlockSpec((1, cw), lambda j, i: (0, j))
    return nj, cur, prev, nxt, w, b


def _conv_fwd(hg, hu, wg, wu, bg, bu):
    S = hg.shape[0]
    T = min(1024, S)
    nj, cur, prev, _, w, b = _conv_specs(S, T)

    def body(hg_ref, hu_ref, hgp_ref, hup_ref, wg_ref, wu_ref, bg_ref, bu_ref, o_ref):
        first = pl.program_id(1) == 0
        g2, g1, g0 = _conv_taps(hg_ref, hgp_ref, first)
        u2, u1, u0 = _conv_taps(hu_ref, hup_ref, first)
        cg = wg_ref[0:1, :] * g2 + wg_ref[1:2, :] * g1 + wg_ref[2:3, :] * g0 + bg_ref[...]
        cu = wu_ref[0:1, :] * u2 + wu_ref[1:2, :] * u1 + wu_ref[2:3, :] * u0 + bu_ref[...]
        o_ref[...] = (_gelu(cg) * cu).astype(BF16)

    return pl.pallas_call(
        body, name="conv_fwd", grid=(nj, S // T), in_specs=[cur, cur, prev, prev, w, w, b, b], out_specs=cur,
        out_shape=jax.ShapeDtypeStruct((S, DFF), BF16), compiler_params=_cparams(("parallel", "parallel")),
    )(hg, hu, hg, hu, wg, wu, bg, bu)


def _conv_bwd_pre(d_act, hg, hu, wg, wu, bg, bu):
    S = hg.shape[0]
    T = min(1024, S)
    nj, cur, prev, _, w, b = _conv_specs(S, T)

    def body(da_ref, hg_ref, hu_ref, hgp_ref, hup_ref, wg_ref, wu_ref, bg_ref, bu_ref,
             dcg_ref, dcu_ref, gwg_ref, gwu_ref, gbg_ref, gbu_ref):
        first = pl.program_id(1) == 0
        g2, g1, g0 = _conv_taps(hg_ref, hgp_ref, first)
        u2, u1, u0 = _conv_taps(hu_ref, hup_ref, first)
        cg = wg_ref[0:1, :] * g2 + wg_ref[1:2, :] * g1 + wg_ref[2:3, :] * g0 + bg_ref[...]
        cu = wu_ref[0:1, :] * u2 + wu_ref[1:2, :] * u1 + wu_ref[2:3, :] * u0 + bu_ref[...]
        da = da_ref[...].astype(F32)
        gl, dgl = _gelu(cg, with_grad=True)
        dcg = da * cu * dgl
        dcu = da * gl
        dcg_ref[...] = dcg.astype(BF16)
        dcu_ref[...] = dcu.astype(BF16)

        @pl.when(first)
        def _():
            for r in (gwg_ref, gwu_ref, gbg_ref, gbu_ref):
                r[...] = jnp.zeros_like(r)

        for r, d, taps in ((gwg_ref, dcg, (g2, g1, g0)), (gwu_ref, dcu, (u2, u1, u0))):
            for j in range(3):
                r[j:j + 1, :] += jnp.sum(d * taps[j], 0, keepdims=True)
        gbg_ref[...] += jnp.sum(dcg, 0, keepdims=True)
        gbu_ref[...] += jnp.sum(dcu, 0, keepdims=True)

    return pl.pallas_call(
        body, name="conv_bwd_pre", grid=(nj, S // T), in_specs=[cur, cur, cur, prev, prev, w, w, b, b],
        out_specs=[cur, cur, w, w, b, b],
        out_shape=[jax.ShapeDtypeStruct((S, DFF), BF16)] * 2 + [jax.ShapeDtypeStruct((3, DFF), F32)] * 2
        + [jax.ShapeDtypeStruct((1, DFF), F32)] * 2,
        compiler_params=_cparams(("parallel", "arbitrary")),
    )(d_act, hg, hu, hg, hu, wg, wu, bg, bu)


def _conv_bwd_in(dc, w, name):
    S = dc.shape[0]
    T = min(512, S)
    nj, cur, _, nxt, wspec, _ = _conv_specs(S, T, DFF // 2)
    nt = S // T

    def body(dc_ref, dn_ref, w_ref, o_ref):
        d = dc_ref[...].astype(F32)
        dn = dn_ref[...].astype(F32) * jnp.where(pl.program_id(1) == nt - 1, 0.0, 1.0)
        o_ref[...] = (w_ref[2:3, :] * d + w_ref[1:2, :] * _shift_up(d, dn, 1) + w_ref[0:1, :] * _shift_up(d, dn, 2)).astype(BF16)

    return pl.pallas_call(
        body, name=name, grid=(nj, nt), in_specs=[cur, nxt, wspec], out_specs=cur,
        out_shape=jax.ShapeDtypeStruct((S, DFF), BF16), compiler_params=_cparams(("parallel", "parallel")),
    )(dc, dc, w)


def _adam_math(g, w, m, v):
    m = B1 * m + (1.0 - B1) * g
    v = B2 * v + (1.0 - B2) * (g * g)
    m_hat = m / (1.0 - B1 ** STEP)
    v_hat = v / (1.0 - B2 ** STEP)
    return -LR * (m_hat / (jnp.sqrt(v_hat) + EPS) + WD * w), m, v


def _reduce_tail(chip32, far, chip, name, wmv=None):
    L = len(chip32)
    _, R, C = chip32[0].shape
    tr = _tile(R, 256, 16)
    nr = R // tr

    def body(chip_ref, *refs):
        own_refs, far_refs, rest = refs[:L], refs[L:2 * L], refs[2 * L:]
        outs = rest[3:] if wmv else rest
        for ll in range(L):
            @pl.when(pl.program_id(0) == ll)
            def _(ll=ll):
                g = own_refs[ll][...]
                for s in range(3):
                    g = g + far_refs[ll][s].astype(F32)
                outs[0][...] = g
                if wmv:
                    outs[1][...], outs[2][...], outs[3][...] = _adam_math(g, rest[0][...], rest[1][...], rest[2][...])

    def rows(ll):
        return lambda l, i: jnp.where(l == ll, i, jnp.where(l < ll, 0, nr - 1))

    blk = pl.BlockSpec((None, tr, C), lambda l, i, ch: (l, i, 0))
    in_specs = [pl.BlockSpec((None, tr, C), lambda l, i, ch, f=rows(ll): (ch[0], f(l, i), 0)) for ll in range(L)]
    in_specs += [pl.BlockSpec((3, tr, C), lambda l, i, ch, f=rows(ll): (0, f(l, i), 0)) for ll in range(L)]
    args = list(chip32) + list(far)
    n_out = 1
    if wmv:
        in_specs += [blk] * 3
        args += list(wmv)
        n_out = 4
    return pl.pallas_call(
        body, name=name,
        grid_spec=pltpu.PrefetchScalarGridSpec(num_scalar_prefetch=1, grid=(L, nr), in_specs=in_specs, out_specs=[blk] * n_out),
        out_shape=[jax.ShapeDtypeStruct((L, R, C), F32)] * n_out, compiler_params=_cparams(("arbitrary", "arbitrary")),
    )(chip, *args)


def _adamw(g, w, m, v, name):
    R, C = g.shape
    tr = _tile(R, 128, 8)

    def body(g_ref, w_ref, m_ref, v_ref, d_ref, nm_ref, nv_ref):
        d_ref[...], nm_ref[...], nv_ref[...] = _adam_math(g_ref[...], w_ref[...], m_ref[...], v_ref[...])

    blk = pl.BlockSpec((tr, C), lambda i: (i, 0))
    return pl.pallas_call(
        body, name=name, grid=(R // tr,), in_specs=[blk] * 4, out_specs=[blk] * 3,
        out_shape=[jax.ShapeDtypeStruct(g.shape, F32)] * 3, compiler_params=_cparams(("parallel",)),
    )(g, w, m, v)


def _pair_sum(x, recv, core, name):
    _, R, C = x.shape
    tr = _tile(R, 600, 16)

    def body(core_ref, x_ref, r_ref, o32_ref, o16_ref):
        s = x_ref[...] + r_ref[...]
        o32_ref[...] = s
        o16_ref[...] = s.astype(BF16)

    blk = pl.BlockSpec((None, tr, C), lambda q, i, c: (q, i, 0))
    mine = pl.BlockSpec((None, None, tr, C), lambda q, i, c: (q, c[0], i, 0))
    return pl.pallas_call(
        body, name=name,
        grid_spec=pltpu.PrefetchScalarGridSpec(num_scalar_prefetch=1, grid=(4, R // tr), in_specs=[mine, blk], out_specs=[blk, blk]),
        out_shape=[jax.ShapeDtypeStruct((4, R, C), F32), jax.ShapeDtypeStruct((4, R, C), BF16)],
        compiler_params=_cparams(("parallel", "parallel")),
    )(core, x.reshape(4, 2, R, C), recv)


def _sum_slots(x, name):
    def body(x_ref, o_ref):
        g = x_ref[0]
        for s in range(1, x.shape[0]):
            g = g + x_ref[s]
        o_ref[...] = g

    return pl.pallas_call(body, name=name, out_shape=jax.ShapeDtypeStruct(x.shape[1:], F32))(x)


MESH = pl.DeviceIdType.MESH
_HBM = pl.BlockSpec(memory_space=pltpu.HBM)


def _dma_sems(n):
    return pltpu.SemaphoreType.DMA((n,))


def _gather_many(xs, name):
    n = len(xs)

    def body(*refs):
        x_refs, out_refs = refs[:n], refs[n:2 * n]
        send_sems, recv_sems, local_sems = refs[2 * n:]
        ax, ay, ac = lax.axis_index("x"), lax.axis_index("y"), lax.axis_index("c")
        me, sibling = (ax, ay, ac), (ax, ay, 1 - ac)
        chips = [(1 - ax, ay), (ax, 1 - ay), (1 - ax, 1 - ay)]

        def copy(a, k, block, to, own=False):
            slot = out_refs[a].at[4 * block[0] + 2 * block[1] + block[2]]
            return pltpu.make_async_remote_copy(
                src_ref=x_refs[a] if own else slot, dst_ref=slot, send_sem=send_sems.at[7 * a + k],
                recv_sem=recv_sems.at[7 * a + k], device_id=to, device_id_type=MESH)

        mine = [pltpu.make_async_copy(x_refs[a], out_refs[a].at[4 * ax + 2 * ay + ac], local_sems.at[a]) for a in range(n)]
        first = [copy(a, 0, me, sibling, own=True) for a in range(n)]
        first += [copy(a, 1 + j, me, (*chip, ac), own=True) for j, chip in enumerate(chips) for a in range(n)]
        for cp in mine + first:
            cp.start()
        passed = []
        for j, chip in enumerate(chips):
            for a in range(n):
                copy(a, 1 + j, (*chip, ac), me).wait_recv()
                cp = copy(a, 4 + j, (*chip, ac), sibling)
                cp.start()
                passed.append(cp)
        for a in range(n):
            copy(a, 0, sibling, me).wait_recv()
            for j, chip in enumerate(chips):
                copy(a, 4 + j, (*chip, 1 - ac), me).wait_recv()
        for cp in first + passed:
            cp.wait_send()
        for cp in mine:
            cp.wait()

    return pl.pallas_call(
        body, name=name, out_shape=[jax.ShapeDtypeStruct((N_DEV,) + x.shape, x.dtype) for x in xs],
        in_specs=[_HBM] * n, out_specs=[_HBM] * n, scratch_shapes=[_dma_sems(7 * n), _dma_sems(7 * n), _dma_sems(n)],
    )(*xs)


_SEM = pl.BlockSpec(memory_space=pltpu.SEMAPHORE)
_EFFECT = pltpu.SideEffectType.DATAFLOW_SIDE_EFFECTING


def _peer(k):
    ax, ay, ac = lax.axis_index("x"), lax.axis_index("y"), lax.axis_index("c")
    px = 1 - ax if k & 4 else ax
    py = 1 - ay if k & 2 else ay
    pc = 1 - ac if k & 1 else ac
    return (px, py, pc), 4 * px + 2 * py + pc


def _build_gather(x_refs, land_refs, send_sems, recv_sems, waiting):
    _, me = _peer(0)
    copies = []
    for a in range(len(x_refs)):
        for k in range(1, N_DEV):
            peer, slot = _peer(k)
            copies.append(pltpu.make_async_remote_copy(
                src_ref=x_refs[a], dst_ref=land_refs[a].at[slot if waiting else me], send_sem=send_sems.at[7 * a + k - 1],
                recv_sem=recv_sems.at[7 * a + k - 1], device_id=peer, device_id_type=MESH))
    return copies


def _build_cores(x_refs, land_refs, send_sems, recv_sems, waiting):
    ax, ay, ac = lax.axis_index("x"), lax.axis_index("y"), lax.axis_index("c")
    copies = []
    for a in range(len(x_refs)):
        for q in range(4):
            copies.append(pltpu.make_async_remote_copy(
                src_ref=x_refs[a].at[2 * q + 1 - ac], dst_ref=land_refs[a].at[q], send_sem=send_sems.at[4 * a + q],
                recv_sem=recv_sems.at[4 * a + q], device_id=(ax, ay, 1 - ac), device_id_type=MESH))
    return copies


def _build_chips(p_refs, land_refs, send_sems, recv_sems, waiting):
    ax, ay, ac = lax.axis_index("x"), lax.axis_index("y"), lax.axis_index("c")
    copies = []
    for a in range(len(p_refs)):
        for k in range(1, 4):
            px = 1 - ax if k & 2 else ax
            py = 1 - ay if k & 1 else ay
            copies.append(pltpu.make_async_remote_copy(
                src_ref=p_refs[a].at[2 * px + py], dst_ref=land_refs[a].at[k - 1], send_sem=send_sems.at[3 * a + k - 1],
                recv_sem=recv_sems.at[3 * a + k - 1], device_id=(px, py, ac), device_id_type=MESH))
    return copies


_EXCHANGES = {"gather": (_build_gather, 7, N_DEV), "cores": (_build_cores, 4, 4), "chips": (_build_chips, 3, 3)}


def _exchange_start(kind, xs, lands, name, after=None):
    build, per, _ = _EXCHANGES[kind]
    n = len(xs)

    def body(*refs):
        for cp in build(refs[:n], refs[n:2 * n], refs[-2 * n - 3], refs[-2 * n - 2], False):
            cp.start()
        refs[-1][...] = jnp.zeros_like(refs[-1])

    hbm = lambda t: pltpu.HBM(t.shape, t.dtype)
    args = [pltpu.with_memory_space_constraint(t, pltpu.HBM) for t in list(xs) + list(lands)]
    in_specs = [_HBM] * (2 * n)
    if after is not None:
        args.append(after)
        in_specs.append(pl.BlockSpec(memory_space=pl.ANY))
    outs = pl.pallas_call(
        body, name=name,
        out_shape=(_dma_sems(per * n), _dma_sems(per * n), *[hbm(t) for t in xs], *[hbm(t) for t in lands],
                   jax.ShapeDtypeStruct((8, 128), F32)),
        in_specs=in_specs, out_specs=(_SEM, _SEM, *[_HBM] * (2 * n), pl.BlockSpec(memory_space=pltpu.VMEM)),
        input_output_aliases={a: 2 + a for a in range(2 * n)},
        compiler_params=pltpu.CompilerParams(has_side_effects=_EFFECT),
    )(*args)
    return (kind, outs[0], outs[1], outs[2:2 + n], outs[2 + n:2 + 2 * n]), outs[-1]


def _exchange_wait(flight, after, name):
    kind, send_sems, recv_sems, xs, lands = flight
    build = _EXCHANGES[kind][0]
    n = len(xs)

    def body(*refs):
        for cp in build(refs[:n], refs[n:2 * n], refs[2 * n], refs[2 * n + 1], True):
            cp.wait_send()
            cp.wait_recv()

    hbm = lambda t: pltpu.HBM(t.shape, t.dtype)
    outs = pl.pallas_call(
        body, name=name, out_shape=(*[hbm(t) for t in xs], *[hbm(t) for t in lands]),
        in_specs=[_HBM] * (2 * n) + [_SEM, _SEM, pl.BlockSpec(memory_space=pl.ANY)], out_specs=[_HBM] * (2 * n),
        input_output_aliases={a: a for a in range(2 * n)}, compiler_params=pltpu.CompilerParams(has_side_effects=_EFFECT),
    )(*xs, *lands, send_sems, recv_sems, after)
    return outs[:n], outs[n:]


def _x_view(xb, d):
    return xb if d == 1 else xb.reshape(xb.shape[0] // d, d * xb.shape[1])


def _layer_fwd(x, xb, p, w, cos, sin, rconsts, late=None):
    proj = _mm(xb, w["win"], tb=True, b_rows=(N_ATT, N_REST), name="mm_proj", out_dtype=BF16)
    qkvs, ogs, lgs = [], [], []
    for g, dil in enumerate(DILATIONS):
        qkv = _qkv_fwd(_x_view(xb, dil), w["win"], g, dil, f"mm_qkv{g}")
        o, l = _attn_fwd(qkv, dil, f"attn_fwd_g{g}")
        qkvs.append(qkv)
        ogs.append(_to_tokens(o, dil))
        lgs.append(_to_tokens(l, dil))
    attn, lse = _rowwise(_f_combine, ogs + lgs, [], [(D, BF16), (HD, F32)], [], name="attn_combine")
    ret_raw, states = _ret_fwd(proj, cos, sin, rconsts)
    rg_win = (proj, RH * RDV, OFF_RG // (RH * RDV))
    ga_win, gr_win = (proj, D, OFF_GA // D), (proj, D, OFF_GR // D)
    (r,) = _rowwise(_f_gn, [ret_raw, rg_win], [w["ret_gn_g"], w["ret_gn_b"]], [(RH * RDV, BF16)], [], name="gn_fwd", tm=256)
    if late is not None:
        w = {**w, **late(r)}
    ap = _mm(attn, w["w_attn_proj"], name="mm_attn_proj", out_dtype=BF16)
    rp = _mm(r, w["w_ret_proj"], name="mm_ret_proj", out_dtype=BF16, tk=2048)
    (merged,) = _rowwise(_f_gate, [ap, rp, ga_win, gr_win], [], [(D, BF16)], [], name="gate_fwd")
    mix = _mm(merged, w["w_out"], name="mm_out")
    h1, x1, x1b = _rowwise(_f_ln1, [x, mix], [w["ln1_g"], w["ln1_b"]], [(D, F32), (D, F32), (D, BF16)], [], name="ln1_fwd")
    z = _mm(x1b, w["w_ple_gate"], name="mm_ple_gate")
    pp = _mm(p, w["w_ple_proj"], tb=True, name="mm_ple_proj")
    hg = _mm(x1b, w["w_up"], tb=True, b_rows=(0, DFF), name="mm_up_g", out_dtype=BF16, tm=512, tn=DFF)
    hu = _mm(x1b, w["w_up"], tb=True, b_rows=(DFF, DFF), name="mm_up_u", out_dtype=BF16, tm=512, tn=DFF)
    act = _conv_fwd(hg, hu, w["conv_wg"], w["conv_wu"], w["conv_bg"], w["conv_bu"])
    ffn = _mm(act, w["w_down"], name="mm_down", tm=512, tk=DFF)
    h2, x2, x2b = _rowwise(_f_ln2, [x1, ffn, z, pp], [w["ln2_g"], w["ln2_b"]], [(D, F32), (D, F32), (D, BF16)], [], name="ln2_fwd")
    saved = dict(xb=xb, proj=proj, qkvs=qkvs, attn=attn, lse=lse, ret_raw=ret_raw, states=states, r=r, ap=ap, rp=rp,
                 merged=merged, h1=h1, x1b=x1b, z=z, pp=pp, hg=hg, hu=hu, act=act, h2=h2, p=p)
    return x2, x2b, saved, w


def _after(fn, token):
    return fn if token is None else (lambda *a: fn(*a[:-1]))


def _layer_bwd(dys, w, sv, cos, sin, rconsts, hooks):
    gr = {}
    proj = sv["proj"]
    call = lambda key, *a: hooks[key](*a) if key in hooks else None
    held = lambda token: [] if token is None else [token]
    token = hooks.get("token")
    dh2, dh2b, dpp, dz, gr["ln2_g"], gr["ln2_b"] = _rowwise(
        _after(_f_ln2_bwd, token), list(dys) + [sv["h2"], sv["z"], sv["pp"]], [w["ln2_g"]] + held(token),
        [(D, F32), (D, BF16), (D, BF16), (D, BF16)], [(1, D), (1, D)], name="ln2_bwd")
    d_act = _mm(dh2b, w["w_down"], tb=True, name="mm_down_dx", out_dtype=BF16, tm=512, tn=DFF)
    gr["w_down"] = _mm(sv["act"], dh2b, ta=True, name="mm_down_dw", tm=DFF // 2)
    dcg, dcu, gwg, gwu, gbg, gbu = _conv_bwd_pre(d_act, sv["hg"], sv["hu"], w["conv_wg"], w["conv_wu"], w["conv_bg"], w["conv_bu"])
    token = call("after_ffn", dcg)
    gr["conv_w"] = jnp.concatenate([gwg, gwu], axis=1)
    gr["conv_b"] = jnp.concatenate([gbg, gbu], axis=1)
    dhg = _conv_bwd_in(dcg, w["conv_wg"], "conv_bwd_in_g")
    dhu = _conv_bwd_in(dcu, w["conv_wu"], "conv_bwd_in_u")
    gw_up = _mm(dhg, sv["x1b"], ta=True, name="mm_up_g_dw", tm=DFF // 2, out_rows=(0, 2 * DFF))
    gr["w_up"] = _mm(dhu, sv["x1b"], ta=True, name="mm_up_u_dw", tm=DFF // 2, out_rows=(DFF, 2 * DFF), into=gw_up)
    dx1 = _mm(dhg, w["w_up"], b_rows=(0, DFF), name="mm_up_g_dx", add=dh2, add_scale=ALPHA, tm=512, tk=DFF)
    dx1 = _mm(dhu, w["w_up"], b_rows=(DFF, DFF), name="mm_up_u_dx", add=dx1, tm=512, tk=DFF)
    gr["w_ple_proj"] = _mm(dpp, sv["p"], ta=True, name="mm_ple_proj_dw")
    gr["w_ple_gate"] = _mm(sv["x1b"], dz, ta=True, name="mm_ple_gate_dw")
    dx1 = _mm(dz, w["w_ple_gate"], tb=True, name="mm_ple_gate_dx", add=dx1)
    dh1, dh1b, gr["ln1_g"], gr["ln1_b"] = _rowwise(_after(_f_ln_bwd, token), [dx1, sv["h1"]], [w["ln1_g"]] + held(token),
                                                   [(D, F32), (D, BF16)], [(1, D), (1, D)], name="ln1_bwd")
    d_merged = _mm(dh1b, w["w_out"], tb=True, name="mm_out_dx", out_dtype=BF16)
    gr["w_out"] = _mm(sv["merged"], dh1b, ta=True, name="mm_out_dw")
    rg_win = (proj, RH * RDV, OFF_RG // (RH * RDV))
    ga_win, gr_win = (proj, D, OFF_GA // D), (proj, D, OFF_GR // D)
    dap, drp, d_rest = _rowwise(_f_gate_bwd, [d_merged, sv["ap"], sv["rp"], ga_win, gr_win], [],
                                [(D, BF16), (D, BF16), (2 * D, BF16, N_REST, OFF_GA // (2 * D), None)], [], name="gate_bwd")
    d_attn = _mm(dap, w["w_attn_proj"], tb=True, name="mm_attn_proj_dx", out_dtype=BF16)
    gr["w_attn_proj"] = _mm(sv["attn"], dap, ta=True, name="mm_attn_proj_dw")
    d_r = _mm(drp, w["w_ret_proj"], tb=True, name="mm_ret_proj_dx", out_dtype=BF16, tn=2048)
    gr["w_ret_proj"] = _mm(sv["r"], drp, ta=True, name="mm_ret_proj_dw", tm=2048)
    token = call("early_grads", gr)
    d_ret, d_rest, gr["ret_gn_g"], gr["ret_gn_b"] = _rowwise(
        _after(_f_gn_bwd, token), [d_r, sv["ret_raw"], rg_win], [w["ret_gn_g"], w["ret_gn_b"]] + held(token),
        [(RH * RDV, BF16), (RH * RDV, BF16, N_REST, OFF_RG // (RH * RDV), d_rest)],
        [(1, RH * RDV), (1, RH * RDV)], name="gn_bwd", tm=256)
    d_rest = _ret_bwd(proj, cos, sin, rconsts, sv["states"], d_ret, d_rest)
    token = call("after_ret", d_rest)
    (delta,) = _rowwise(_after(_f_delta, token), [d_attn, sv["attn"]], held(token), [(HD, F32)], [], name="attn_delta")
    gw_in, dqkvs = None, []
    for g, dil in enumerate(DILATIONS):
        dqkvs.append(_attn_bwd(sv["qkvs"][g], _to_head_residues(d_attn, dil), _to_residues(sv["lse"], dil),
                               _to_residues(delta, dil), dil, f"attn_bwd_g{g}"))
        gw_in = _qkv_dw(dqkvs[g], _x_view(sv["xb"], dil), g, dil, f"mm_qkv{g}_dw", into=gw_in)
    gw_in = _mm(d_rest, sv["xb"], ta=True, name="mm_proj_dw", out_rows=(N_ATT, N_IN), into=gw_in, blocks8=True)
    gr["w_in"] = gw_in.reshape(N_IN, D)
    token = call("w_in_ready", gr["w_in"])
    dx0 = _mm(d_rest, w["win"], b_rows=(N_ATT, N_REST), name="mm_proj_dx", add=dh1, add_scale=ALPHA, after=token)
    dx_parts = []
    for g, dil in enumerate(DILATIONS):
        if dil == 1:
            dx0 = _qkv_dx(dqkvs[g], w["win"], g, dil, f"mm_qkv{g}_dx", F32, add=dx0)
            token = call("after_dx0", dx0)
        else:
            dx_parts.append(("dilated", _qkv_dx(dqkvs[g], w["win"], g, dil, f"mm_qkv{g}_dx", BF16, after=token), dil))
    return [dx0] + dx_parts, gr


def _local_step(x, p, positions, target, ws, own_hooks=None, on_grads=None):
    half = RDK // 2
    freq = jnp.power(ROPE_BASE, -jnp.arange(half, dtype=F32) / half)
    ang = positions.astype(F32)[:, None] * freq[None, :]
    cos, sin = jnp.cos(ang), jnp.sin(ang)
    rconsts = _ret_consts()
    xb = x.astype(BF16)
    saved, ws = [], list(ws)
    for l in range(DEPTH):
        first, late = ws[l] if isinstance(ws[l], tuple) else (ws[l], None)
        if callable(first):
            first = first(x)
        x, xb, sv, ws[l] = _layer_fwd(x, xb, p[l], first, cos, sin, rconsts, late)
        saved.append(sv)
    dy, loss_vec = _rowwise(_f_loss, [x, target], [], [(D, F32)], [(1, D)], name="loss")
    dys, grads = [dy], [None] * DEPTH
    from_above = {}
    for l in reversed(range(DEPTH)):
        hooks = {**from_above, **(own_hooks(l) if own_hooks else {})}
        dys, grads[l] = _layer_bwd(dys, ws[l], saved[l], cos, sin, rconsts, hooks)
        from_above = on_grads(l, grads[l]) if on_grads else {}
    (grad_x,) = _rowwise(_f_sum, dys, [], [(D, F32)], [], name="grad_x_sum")
    return loss_vec, grad_x, grads


def _pack_rows(arrs):
    parts, where, off = [], [], 0
    for t in arrs:
        t = t.reshape(-1, D)
        rows = t.shape[0]
        padded = -(-rows // 8) * 8
        parts.append(jnp.pad(t, ((0, padded - rows), (0, 0))))
        where.append((off, rows))
        off += padded
    return jnp.concatenate(parts, axis=0), where


FIRST = ("w_in",)
LATER = tuple(n for n in BIG if n not in FIRST)


def _first_weights(g, l, W):
    w = dict(win=g["w_in"].reshape(N_IN, D))
    for n in ("ret_gn_g", "ret_gn_b", "ln1_g", "ln1_b", "ln2_g", "ln2_b"):
        w[n] = W[n][l][None, :]
    return w


def _later_weights(g, l, conv_w_all, conv_b):
    w = dict(w_up=g["w_up"].reshape(2 * DFF, D), w_ple_proj=g["w_ple_proj"].reshape(D, PLE),
             w_attn_proj=g["w_attn_proj"].reshape(D, D), w_ret_proj=g["w_ret_proj"].reshape(RH * RDV, D),
             w_out=g["w_out"].reshape(D, D), w_down=g["w_down"].reshape(DFF, D), w_ple_gate=g["w_ple_gate"].reshape(D, D))
    w["conv_wg"], w["conv_wu"] = conv_w_all[l][:, :DFF], conv_w_all[l][:, DFF:]
    w["conv_bg"], w["conv_bu"] = conv_b[l][None, :DFF], conv_b[l][None, DFF:]
    return w


def _layer_weights(g, l, conv_w_all, conv_b, W):
    return {**_first_weights(g, l, W), **_later_weights(g, l, conv_w_all, conv_b)}


def kernel(x, p, positions, w_in, w_attn_proj, w_ret_proj, ret_gn_g, ret_gn_b, w_out, ln1_g, ln1_b, w_up, conv_w, conv_b, w_down, w_ple_gate, w_ple_proj, ln2_g, ln2_b, loss_target, m_w_in, m_w_attn_proj, m_w_ret_proj, m_ret_gn_g, m_ret_gn_b, m_w_out, m_ln1_g, m_ln1_b, m_w_up, m_conv_w, m_conv_b, m_w_down, m_w_ple_gate, m_w_ple_proj, m_ln2_g, m_ln2_b, v_w_in, v_w_attn_proj, v_w_ret_proj, v_ret_gn_g, v_ret_gn_b, v_w_out, v_ln1_g, v_ln1_b, v_w_up, v_conv_w, v_conv_b, v_w_down, v_w_ple_gate, v_w_ple_proj, v_ln2_g, v_ln2_b):
    W = dict(w_in=w_in, w_attn_proj=w_attn_proj, w_ret_proj=w_ret_proj, ret_gn_g=ret_gn_g, ret_gn_b=ret_gn_b, w_out=w_out,
             ln1_g=ln1_g, ln1_b=ln1_b, w_up=w_up, conv_w=conv_w, conv_b=conv_b, w_down=w_down, w_ple_gate=w_ple_gate,
             w_ple_proj=w_ple_proj, ln2_g=ln2_g, ln2_b=ln2_b)
    M = dict(w_in=m_w_in, w_attn_proj=m_w_attn_proj, w_ret_proj=m_w_ret_proj, ret_gn_g=m_ret_gn_g, ret_gn_b=m_ret_gn_b,
             w_out=m_w_out, ln1_g=m_ln1_g, ln1_b=m_ln1_b, w_up=m_w_up, conv_w=m_conv_w, conv_b=m_conv_b, w_down=m_w_down,
             w_ple_gate=m_w_ple_gate, w_ple_proj=m_w_ple_proj, ln2_g=m_ln2_g, ln2_b=m_ln2_b)
    V = dict(w_in=v_w_in, w_attn_proj=v_w_attn_proj, w_ret_proj=v_w_ret_proj, ret_gn_g=v_ret_gn_g, ret_gn_b=v_ret_gn_b,
             w_out=v_w_out, ln1_g=v_ln1_g, ln1_b=v_ln1_b, w_up=v_w_up, conv_w=v_conv_w, conv_b=v_conv_b, w_down=v_w_down,
             w_ple_gate=v_w_ple_gate, w_ple_proj=v_w_ple_proj, ln2_g=v_ln2_g, ln2_b=v_ln2_b)

    me = 4 * lax.axis_index("x") + 2 * lax.axis_index("y") + lax.axis_index("c")
    shard = lambda n, l: (W[n][l].T if n in COL_SHARDED else W[n][l]).astype(BF16)
    landing = lambda ts: [lax.dynamic_update_index_in_dim(lax.empty((N_DEV,) + t.shape, t.dtype), t, me, 0) for t in ts]
    first0 = _gather_many([shard(n, 0) for n in FIRST], "gather_first_l0")
    later0 = [shard(n, 0) for n in LATER] + [conv_w]
    flight0, token0 = _exchange_start("gather", later0, landing(later0), "gather_later_l0_start", after=first0[0])
    all1 = [shard(n, 1) for n in BIG]
    flight1, token1 = _exchange_start("gather", all1, landing(all1), "gather_weights_l1_start", after=token0)
    conv_w_all = []

    def later_first_layer(after):
        _, got = _exchange_wait(flight0, after, "gather_later_l0_wait")
        conv_w_all.append(got[-1].transpose(1, 2, 0, 3).reshape(DEPTH, 3, 2 * DFF))
        return _later_weights(dict(zip(LATER, got)), 0, conv_w_all[0], conv_b)

    def second_layer(after):
        _, got = _exchange_wait(flight1, after, "gather_weights_l1_wait")
        return _layer_weights(dict(zip(BIG, got)), 1, conv_w_all[0], conv_b, W)

    core = lax.axis_index("c").astype(jnp.int32).reshape(1)
    chip = (2 * lax.axis_index("x") + lax.axis_index("y")).astype(jnp.int32).reshape(1)
    empty_like = lambda ts, slots: [lax.empty((slots,) + t.shape[1:], t.dtype) for t in ts]
    chip32, far = [{} for _ in range(DEPTH)], [{} for _ in range(DEPTH)]
    pending = []

    def reduction(l, names, tag):
        state = {}

        def start(g):
            mine = [g[n].reshape((N_DEV, -1) + g[n].shape[1:]) for n in names]
            state["cores"], token = _exchange_start("cores", mine, empty_like(mine, 4), f"exchange_cores_{tag}_start")
            return token

        def onward(after):
            mine, theirs = _exchange_wait(state["cores"], after, f"exchange_cores_{tag}_wait")
            sums = [_pair_sum(a, b, core, f"pair_sum_l{l}_{n}") for a, b, n in zip(mine, theirs, names)]
            for n, s in zip(names, sums):
                chip32[l][n] = s[0]
            sent = [s[0 if n in F32_OVER_ICI else 1] for s, n in zip(sums, names)]
            flight, token = _exchange_start("chips", sent, empty_like(sent, 3), f"exchange_chips_{tag}_start")
            pending.append((l, names, flight, tag))
            return token

        return start, onward

    def on_grads(l, g):
        if l == 0:
            return {}
        start, onward = reduction(l, BIG, f"l{l}")
        return dict(token=start(g), after_ffn=onward)

    def own_hooks(l):
        if l != 0:
            return {}
        start_e, onward_e = reduction(0, LATER, "l0_later")
        start_w, onward_w = reduction(0, FIRST, "l0_first")
        return dict(early_grads=start_e, after_ret=onward_e, w_in_ready=lambda gw: start_w({"w_in": gw}), after_dx0=onward_w)

    ws = [(_first_weights(dict(zip(FIRST, first0)), 0, W), later_first_layer), second_layer]
    loss_vec, grad_x, grads = _local_step(x[0] + token1[0, 0], p[:, 0], positions[0], loss_target[0], ws, own_hooks, on_grads)
    loss = lax.psum(jnp.sum(loss_vec), ("x", "y", "c"))
    for l, names, flight, tag in pending:
        _, got = _exchange_wait(flight, grad_x, f"exchange_chips_{tag}_wait")
        far[l].update(zip(names, got))
    G, DW, NM, NV = ({} for _ in range(4))
    for n in BIG:
        chip32_n = [chip32[l][n] for l in range(DEPTH)]
        far_n = [far[l][n] for l in range(DEPTH)]
        if n in COL_SHARDED:
            G[n] = _reduce_tail(chip32_n, far_n, chip, f"reduced_{n}")[0].transpose(0, 2, 1)
            R2, C2 = DEPTH * W[n].shape[1], W[n].shape[2]
            res = _adamw(*(t.reshape(R2, C2) for t in (G[n], W[n], M[n], V[n])), f"adamw_{n}")
            DW[n], NM[n], NV[n] = (t.reshape(W[n].shape) for t in res)
        else:
            G[n], DW[n], NM[n], NV[n] = _reduce_tail(chip32_n, far_n, chip, f"adamw_{n}", wmv=(W[n], M[n], V[n]))

    small_names = SMALL + ("conv_w",)
    g_small, where = _pack_rows([jnp.stack([grads[l][n] for l in range(DEPTH)]) for n in small_names])
    (g_all,) = _gather_many([g_small], "gather_small_grads")
    g_small = _sum_slots(g_all, "sum_small_grads")
    for n, (off, rows) in zip(SMALL, where):
        G[n] = g_small[off:off + rows].reshape(W[n].shape)
    off, rows = where[-1]
    g_cw = g_small[off:off + rows].reshape(DEPTH, 3, N_DEV, conv_w.shape[2])
    G["conv_w"] = lax.dynamic_index_in_dim(g_cw, me, axis=2, keepdims=False)
    packed = [_pack_rows([d[n] for n in SMALL]) for d in (G, W, M, V)]
    small_out = _adamw(*(t for t, _ in packed), "adamw_small")
    for res, dst in zip(small_out, (DW, NM, NV)):
        for n, (off, rows) in zip(SMALL, packed[0][1]):
            dst[n] = res[off:off + rows].reshape(W[n].shape)
    two_d = lambda t: t.reshape(DEPTH * 3, conv_w.shape[2])
    cw_out = _adamw(two_d(G["conv_w"]), two_d(conv_w), two_d(m_conv_w), two_d(v_conv_w), "adamw_conv_w")
    for res, dst in zip(cw_out, (DW, NM, NV)):
        dst["conv_w"] = res.reshape(conv_w.shape)

    return (loss, grad_x[None], *[G[n] for n in WEIGHTS], *[DW[n] for n in WEIGHTS], *[NM[n] for n in WEIGHTS],
            *[NV[n] for n in WEIGHTS])
```

```python
import math

import numpy as np
import jax
import jax.numpy as jnp
from jax import lax
from jax.experimental import pallas as pl
from jax.experimental.pallas import tpu as pltpu

F32, BF16 = jnp.float32, jnp.bfloat16

D = 1024
DEPTH = 2
N_DEV = 8
HD = 128
NH = 8
DILATIONS = (1, 4, 16)
SPAN = 128
N_ATT = 3 * 3 * NH * HD
RH, RDK, RDV = 4, 256, 512
CH = 128
DFF = 2816
PLE = 256
N_IN = 17408
N_REST = N_IN - N_ATT
OFF_RQ, OFF_RK, OFF_RV, OFF_RG, OFF_GA, OFF_GR = 0, 1024, 2048, 4096, 6144, 7168
ALPHA = (2 * DEPTH) ** 0.25
LN_EPS, GN_EPS = 1e-5, 1e-6
ROPE_BASE = 10000.0
LR, B1, B2, EPS, WD, STEP = 0.001, 0.9, 0.999, 1e-8, 0.01, 10
VMEM_LIMIT = 48 * 1024 * 1024
NEG = -1e30

BIG = ("w_in", "w_attn_proj", "w_ret_proj", "w_out", "w_up", "w_down", "w_ple_gate", "w_ple_proj")
COL_SHARDED = ("w_in", "w_up", "w_ple_proj")
F32_OVER_ICI = ("w_attn_proj", "w_out", "w_ple_gate", "w_ple_proj")
SMALL = ("ret_gn_g", "ret_gn_b", "ln1_g", "ln1_b", "conv_b", "ln2_g", "ln2_b")
WEIGHTS = ("w_in", "w_attn_proj", "w_ret_proj", "ret_gn_g", "ret_gn_b", "w_out", "ln1_g", "ln1_b", "w_up",
           "conv_w", "conv_b", "w_down", "w_ple_gate", "w_ple_proj", "ln2_g", "ln2_b")


def _tile(n, cap, mult=128):
    if n <= cap:
        return n
    t = (cap // mult) * mult
    while n % t:
        t -= mult
    return t


def _cparams(sem):
    return pltpu.CompilerParams(dimension_semantics=sem, vmem_limit_bytes=VMEM_LIMIT)


def _dot(a, b, ca, cb):
    return lax.dot_general(a, b, (((ca,), (cb,)), ((), ())), preferred_element_type=F32)


def _bdot(a, b, ca, cb):
    return lax.dot_general(a, b, (((ca,), (cb,)), ((0,), (0,))), preferred_element_type=F32)


def _mm(a, b, *, name, ta=False, tb=False, out_dtype=F32, add=None, add_scale=1.0, tm=1024, tn=1024, tk=1024,
        b_rows=None, out_rows=None, into=None, blocks8=False, after=None):
    M, K = (a.shape[1], a.shape[0]) if ta else a.shape
    b_first, b_count = b_rows if b_rows else (0, b.shape[0])
    N = b_count if tb else b.shape[1]
    assert K == (b.shape[1] if tb else b_count)
    tm, tn, tk = _tile(M, tm), _tile(N, tn), _tile(K, tk)
    nk = K // tk
    o_first, o_total = out_rows if out_rows else (0, M)
    jb, kb, io = (b_first // tn, 0, o_first // tm) if tb else (0, b_first // tk, o_first // tm)
    assert b_first % (tn if tb else tk) == 0 and o_first % tm == 0 and (add is None or out_rows is None)

    def body(*refs):
        if add is None:
            a_ref, b_ref = refs[:2]
        else:
            a_ref, b_ref, add_ref = refs[:3]
        o_ref, acc_ref = refs[-2:]
        k = pl.program_id(2)

        @pl.when(k == 0)
        def _():
            acc_ref[...] = jnp.zeros_like(acc_ref)

        acc_ref[...] += _dot(a_ref[...].astype(BF16), b_ref[...].astype(BF16), 0 if ta else 1, 1 if tb else 0)

        @pl.when(k == nk - 1)
        def _():
            r = acc_ref[...]
            if add is not None:
                r = r + add_scale * add_ref[...].astype(F32)
            o_ref[...] = r.astype(out_dtype).reshape(o_ref.shape)

    a_spec = pl.BlockSpec((tk, tm), lambda i, j, k: (k, i)) if ta else pl.BlockSpec((tm, tk), lambda i, j, k: (i, k))
    if tb:
        b_spec = pl.BlockSpec((tn, tk), lambda i, j, k: (j + jb, k))
    else:
        b_spec = pl.BlockSpec((tk, tn), lambda i, j, k: (k + kb, j))
    if blocks8:
        assert tm == 1024
        o_spec = pl.BlockSpec((1, 8, 128, tn), lambda i, j, k: (i + io, 0, 0, j))
        o_shape = (o_total // tm, 8, 128, N)
    else:
        o_spec = pl.BlockSpec((tm, tn), lambda i, j, k: (i + io, j))
        o_shape = (o_total, N)
    in_specs, args, aliases = [a_spec, b_spec], [a, b], {}
    if add is not None:
        in_specs.append(o_spec)
        args.append(add)
    if after is not None:
        in_specs.append(pl.BlockSpec(memory_space=pl.ANY))
        args.append(after)
    if into is not None:
        aliases = {len(args): 0}
        in_specs.append(pl.BlockSpec(memory_space=pl.ANY))
        args.append(into)
    return pl.pallas_call(
        body, name=name, grid=(M // tm, N // tn, nk), in_specs=in_specs, out_specs=o_spec,
        out_shape=jax.ShapeDtypeStruct(o_shape, out_dtype), scratch_shapes=[pltpu.VMEM((tm, tn), F32)],
        input_output_aliases=aliases, compiler_params=_cparams(("parallel", "parallel", "arbitrary")),
    )(*args)


def _rowwise(fn, rows, pars, outs, accs, *, name, tm=512):
    first = rows[0][0] if isinstance(rows[0], tuple) else rows[0]
    S = first.shape[-2]
    tm = _tile(S, tm, 16)
    n_r, n_p, n_o = len(rows), len(pars), len(outs)
    views_out = {k: o[3] for k, o in enumerate(outs) if o[0] == "dilated"}
    outs = [(o[1], o[2], o[1], 0, None) if o[0] == "dilated" else o if len(o) == 5 else (o[0], o[1], o[0], 0, None) for o in outs]
    intos = [(k, o[4]) for k, o in enumerate(outs) if o[4] is not None]
    n_i = len(intos)
    dilated = {k: r[2] for k, r in enumerate(rows) if isinstance(r, tuple) and r[0] == "dilated"}
    n_s = len(dilated) + len(views_out)

    def body(*refs):
        i = pl.program_id(0)
        scratch = refs[len(refs) - n_s:]
        vals = [r[...] for r in refs[:n_r + n_p]]
        for scr, (k, d) in zip(scratch, dilated.items()):
            w = vals[k].shape[1] // d
            for r in range(d):
                for c in range(w // 128):
                    scr.at[c][pl.ds(r, tm // d, stride=d), :] = vals[k][:, r * w + c * 128:r * w + (c + 1) * 128].astype(F32)
            vals[k] = jnp.concatenate([scr[c] for c in range(w // 128)], axis=1)
        res = fn(*vals)
        if not isinstance(res, (tuple, list)):
            res = (res,)
        res = list(res)
        o_refs = refs[n_r + n_p + n_i:n_r + n_p + n_i + n_o]
        a_refs = refs[n_r + n_p + n_i + n_o:len(refs) - n_s]
        for scr, (k, d) in zip(scratch[len(dilated):], views_out.items()):
            v = res[k].astype(F32)
            nc = v.shape[1] // 128
            for c in range(nc):
                scr[c] = v[:, c * 128:(c + 1) * 128]
            res[k] = jnp.concatenate([scr.at[c][pl.ds(r, tm // d, stride=d), :] for r in range(d) for c in range(nc)], axis=1)
        for r, v in zip(o_refs, res[:n_o]):
            r[...] = v.astype(r.dtype)
        if a_refs:
            @pl.when(i == 0)
            def _():
                for r in a_refs:
                    r[...] = jnp.zeros_like(r)

            for r, v in zip(a_refs, res[n_o:]):
                r[...] += v

    in_specs, args = [], []
    for r in rows:
        if isinstance(r, tuple) and r[0] == "dilated":
            _, arr, d = r
            in_specs.append(pl.BlockSpec((tm // d, arr.shape[1]), lambda i: (i, 0)))
        elif isinstance(r, tuple):
            arr, w, cb = r
            in_specs.append(pl.BlockSpec((tm, w), lambda i, cb=cb: (i, cb)))
        elif r.ndim == 3:
            arr = r
            in_specs.append(pl.BlockSpec((arr.shape[0], tm, arr.shape[2]), lambda i: (0, i, 0)))
        else:
            arr = r
            in_specs.append(pl.BlockSpec((tm, arr.shape[1]), lambda i: (i, 0)))
        args.append(arr)
    for p_ in pars:
        in_specs.append(pl.BlockSpec(p_.shape, lambda i: (0, 0)))
        args.append(p_)
    aliases = {}
    for k, arr in intos:
        aliases[len(args)] = k
        in_specs.append(pl.BlockSpec(memory_space=pl.ANY))
        args.append(arr)
    out_shape = [jax.ShapeDtypeStruct((S // views_out.get(k, 1), views_out.get(k, 1) * o[2]), o[1]) for k, o in enumerate(outs)]
    out_shape += [jax.ShapeDtypeStruct(a, F32) for a in accs]
    out_specs = [pl.BlockSpec((tm // views_out.get(k, 1), views_out.get(k, 1) * o[0]), lambda i, cb=o[3]: (i, cb))
                 for k, o in enumerate(outs)] + [pl.BlockSpec(a, lambda i: (0, 0)) for a in accs]
    scratch = [pltpu.VMEM((rows[k][1].shape[1] // d // 128, tm, 128), F32) for k, d in dilated.items()]
    scratch += [pltpu.VMEM((outs[k][0] // 128, tm, 128), F32) for k in views_out]
    return pl.pallas_call(
        body, name=name, grid=(S // tm,), in_specs=in_specs, out_specs=out_specs, out_shape=out_shape,
        scratch_shapes=scratch, input_output_aliases=aliases,
        compiler_params=_cparams(("arbitrary",) if accs else ("parallel",)),
    )(*args)


def _norm(h, eps):
    mu = jnp.mean(h, -1, keepdims=True)
    d = h - mu
    rstd = lax.rsqrt(jnp.mean(d * d, -1, keepdims=True) + eps)
    return d * rstd, rstd


def _norm_bwd(dxh, xh, rstd):
    return rstd * (dxh - jnp.mean(dxh, -1, keepdims=True) - xh * jnp.mean(dxh * xh, -1, keepdims=True))


def _sig(x):
    return 1.0 / (1.0 + jnp.exp(-x))


_GELU_C = math.sqrt(2.0 / math.pi)


def _gelu(x, with_grad=False):
    x2 = x * x
    t = jnp.tanh(x * (_GELU_C + (_GELU_C * 0.044715) * x2))
    half_x, one_t = 0.5 * x, 1.0 + t
    if not with_grad:
        return half_x * one_t
    return half_x * one_t, 0.5 * one_t + half_x * (1.0 - t * t) * (_GELU_C + (3 * _GELU_C * 0.044715) * x2)


def _f_ln1(x, mix, g, b):
    h = ALPHA * x + mix
    xh, _ = _norm(h, LN_EPS)
    y = xh * g + b
    return h, y, y


def _f_ln2(x, ffn, z, pp, g, b):
    h = ALPHA * x + ffn + _sig(z) * pp
    xh, _ = _norm(h, LN_EPS)
    return h, xh * g + b


def _f_ln_bwd(*args):
    *dys, h, g = args
    dy = dys[0]
    for t in dys[1:]:
        dy = dy + t
    xh, rstd = _norm(h, LN_EPS)
    dh = _norm_bwd(dy * g, xh, rstd)
    return dh, dh, jnp.sum(dy * xh, 0, keepdims=True), jnp.sum(dy, 0, keepdims=True)


def _f_sum(*ts):
    r = ts[0]
    for t in ts[1:]:
        r = r + t
    return r


def _f_loss(y, t):
    e = y - t
    return e * (1.0 / D), jnp.sum(e * e, 0, keepdims=True) * (0.5 / D)


def _head_col(c, h):
    lane = lax.broadcasted_iota(jnp.int32, c.shape, 1)
    return jnp.sum(jnp.where(lane == h, c, 0.0), -1, keepdims=True)


def _f_combine(o0, o1, o2, l0, l1, l2):
    m = jnp.maximum(jnp.maximum(l0, l1), l2)
    e0, e1, e2 = jnp.exp(l0 - m), jnp.exp(l1 - m), jnp.exp(l2 - m)
    den = e0 + e1 + e2
    inv = 1.0 / den
    w0, w1, w2 = e0 * inv, e1 * inv, e2 * inv
    parts = [_head_col(w0, h) * o0[h].astype(F32) + _head_col(w1, h) * o1[h].astype(F32) + _head_col(w2, h) * o2[h].astype(F32)
             for h in range(NH)]
    return jnp.concatenate(parts, axis=1), m + jnp.log(den)


def _f_delta(da, a):
    lane = lax.broadcasted_iota(jnp.int32, (da.shape[0], HD), 1)
    out = jnp.zeros((da.shape[0], HD), F32)
    for h in range(NH):
        sl = slice(h * HD, (h + 1) * HD)
        s = jnp.sum(da[:, sl].astype(F32) * a[:, sl].astype(F32), -1, keepdims=True)
        out = jnp.where(lane == h, s, out)
    return out


def _f_gate(ap, rp, ga, gr):
    return _sig(ga.astype(F32)) * ap.astype(F32) + _sig(gr.astype(F32)) * rp.astype(F32)


def _f_gate_bwd(dm, ap, rp, ga, gr):
    dm = dm.astype(F32)
    sa, sr = _sig(ga.astype(F32)), _sig(gr.astype(F32))
    dga, dgr = dm * ap.astype(F32) * sa * (1.0 - sa), dm * rp.astype(F32) * sr * (1.0 - sr)
    return dm * sa, dm * sr, jnp.concatenate([dga, dgr], axis=1)


def _f_gn(y, rg, g, b):
    y, rg = y.astype(F32), rg.astype(F32)
    parts = []
    for h in range(RH):
        sl = slice(h * RDV, (h + 1) * RDV)
        xh, _ = _norm(y[:, sl], GN_EPS)
        parts.append(xh * g[:, sl] + b[:, sl])
    return rg * _sig(rg) * jnp.concatenate(parts, axis=1)


def _f_gn_bwd(dr, y, rg, g, b):
    dr, y, rg = dr.astype(F32), y.astype(F32), rg.astype(F32)
    s = _sig(rg)
    d_out = dr * rg * s
    dys, outs, xhs = [], [], []
    for h in range(RH):
        sl = slice(h * RDV, (h + 1) * RDV)
        xh, rstd = _norm(y[:, sl], GN_EPS)
        xhs.append(xh)
        outs.append(xh * g[:, sl] + b[:, sl])
        dys.append(_norm_bwd(d_out[:, sl] * g[:, sl], xh, rstd))
    xh, out = jnp.concatenate(xhs, axis=1), jnp.concatenate(outs, axis=1)
    d_rg = dr * out * s * (1.0 + rg * (1.0 - s))
    return jnp.concatenate(dys, axis=1), d_rg, jnp.sum(d_out * xh, 0, keepdims=True), jnp.sum(d_out, 0, keepdims=True)


def _f_ln2_bwd(*args):
    *dys, h, z, pp, g = args
    dh, dhb, dg, db = _f_ln_bwd(*dys, h, g)
    s = _sig(z)
    return dh, dhb, dh * s, dh * pp * s * (1.0 - s), dg, db


QKV = 3 * HD


def _to_tokens(t, d):
    if d == 1:
        return t
    *lead, S, C = t.shape
    n = len(lead)
    perm = tuple(range(n)) + (n + 1, n, n + 2)
    return t.reshape(*lead, d, S // d, C).transpose(perm).reshape(*lead, S, C)


def _to_residues(t, d):
    if d == 1:
        return t
    S, C = t.shape
    return t.reshape(S // d, d, C).transpose(1, 0, 2).reshape(S, C)


def _to_head_residues(t, d):
    S = t.shape[0]
    return t.reshape(S // d, d, NH, HD).transpose(2, 1, 0, 3).reshape(NH, S, HD)


def _w_qkv_specs(g):
    return [pl.BlockSpec((D, D), lambda *i, t=t: (3 * g + t, 0)) for t in range(3)]


def _qkv_fwd(xv, win, g, dil, name):
    Sd = xv.shape[0]
    S = Sd * dil
    tm = min(512, Sd)
    nma = Sd // tm

    def body(a_ref, wq_ref, wk_ref, wv_ref, o_ref):
        a = a_ref[...]
        q, k, v = (_dot(a, w_ref[...], 1, 1).astype(BF16) for w_ref in (wq_ref, wk_ref, wv_ref))
        for h in range(NH):
            sl = slice(h * HD, (h + 1) * HD)
            o_ref[h] = jnp.concatenate([q[:, sl], k[:, sl], v[:, sl]], axis=1)

    return pl.pallas_call(
        body, name=name, grid=(S // tm,),
        in_specs=[pl.BlockSpec((tm, D), lambda i: (i % nma, i // nma))] + _w_qkv_specs(g),
        out_specs=pl.BlockSpec((NH, tm, QKV), lambda i: (0, i, 0)), out_shape=jax.ShapeDtypeStruct((NH, S, QKV), BF16),
        compiler_params=_cparams(("parallel",)),
    )(xv, win, win, win)


def _qkv_dx(dqkv, win, g, dil, name, out_dtype, add=None, after=None):
    S = dqkv.shape[1]
    Sd = S // dil
    tm = min(512, Sd)
    nmo = Sd // tm

    def body(*refs):
        a_ref, wq_ref, wk_ref, wv_ref = refs[:4]
        o_ref = refs[-1]
        acc = None
        for t, w_ref in enumerate((wq_ref, wk_ref, wv_ref)):
            d = jnp.concatenate([a_ref[h][:, t * HD:(t + 1) * HD] for h in range(NH)], axis=1)
            part = _dot(d, w_ref[...], 1, 0)
            acc = part if acc is None else acc + part
        if add is not None:
            acc = acc + refs[4][...]
        o_ref[...] = acc.astype(out_dtype)

    o_spec = pl.BlockSpec((tm, D), lambda i: (i % nmo, i // nmo))
    in_specs = [pl.BlockSpec((NH, tm, QKV), lambda i: (0, i, 0))] + _w_qkv_specs(g)
    args = [dqkv, win, win, win]
    if add is not None:
        assert dil == 1
        in_specs.append(o_spec)
        args.append(add)
    if after is not None:
        in_specs.append(pl.BlockSpec(memory_space=pl.ANY))
        args.append(after)
    return pl.pallas_call(
        body, name=name, grid=(S // tm,), in_specs=in_specs, out_specs=o_spec,
        out_shape=jax.ShapeDtypeStruct((Sd, dil * D), out_dtype), compiler_params=_cparams(("parallel",)),
    )(*args)


GW_IN_BLOCKS = (N_IN // D, NH, HD, D)


def _qkv_dw(dqkv, xv, g, dil, name, into=None):
    S = dqkv.shape[1]
    Sd = S // dil
    tk = min(1024, Sd)
    nkb, nk = Sd // tk, S // tk
    hh = NH // 2

    def body(*refs):
        a_ref, b_ref = refs[:2]
        o_ref, acc_ref = refs[-2:]
        k = pl.program_id(1)

        @pl.when(k == 0)
        def _():
            acc_ref[...] = jnp.zeros_like(acc_ref)

        b = b_ref[...]
        for h in range(hh):
            acc_ref[h * QKV:(h + 1) * QKV, :] += _dot(a_ref[h], b, 0, 0)

        @pl.when(k == nk - 1)
        def _():
            for h in range(hh):
                for t in range(3):
                    o_ref[t, h] = acc_ref[h * QKV + t * HD:h * QKV + (t + 1) * HD, :]

    in_specs = [pl.BlockSpec((hh, tk, QKV), lambda j, k: (j, k, 0)), pl.BlockSpec((tk, D), lambda j, k: (k % nkb, k // nkb))]
    args, aliases = [dqkv, xv], {}
    if into is not None:
        aliases = {2: 0}
        in_specs.append(pl.BlockSpec(memory_space=pl.ANY))
        args.append(into)
    return pl.pallas_call(
        body, name=name, grid=(2, nk), in_specs=in_specs,
        out_specs=pl.BlockSpec((3, hh, HD, D), lambda j, k: (g, j, 0, 0)), out_shape=jax.ShapeDtypeStruct(GW_IN_BLOCKS, F32),
        input_output_aliases=aliases, scratch_shapes=[pltpu.VMEM((hh * QKV, D), F32)],
        compiler_params=_cparams(("parallel", "arbitrary")),
    )(*args)


def _band(nb, first_valid, last_valid=None):
    b = lax.broadcasted_iota(jnp.int32, (nb, SPAN, SPAN), 0)
    row = lax.broadcasted_iota(jnp.int32, (nb, SPAN, SPAN), 1)
    col = lax.broadcasted_iota(jnp.int32, (nb, SPAN, SPAN), 2)
    off = jnp.where(b == 0, jnp.where(first_valid, 0, 2 * SPAN), 0)
    if last_valid is not None:
        off = off + jnp.where(b == nb - 1, jnp.where(last_valid, 0, 2 * SPAN), 0)
    return col <= row, col >= row + off


def _attn_tiles(S, dil):
    Sd = S // dil
    T = min(1024, Sd)
    hp = min(NH, max(1, (S // T) * NH // 16))
    return Sd, T, T // SPAN, Sd // T, hp


def _attn_fwd(qkv, dil, name):
    S = qkv.shape[1]
    Sd, T, nsub, nib, hp = _attn_tiles(S, dil)
    scale = HD ** -0.5

    def body(c_ref, p_ref, o_ref, l_ref):
        ib, hb = pl.program_id(1), pl.program_id(2)
        m_cur, m_prev = _band(nsub, ib > 0)
        lane = lax.broadcasted_iota(jnp.int32, (T, HD), 1)

        @pl.when(hb == 0)
        def _():
            l_ref[...] = jnp.zeros_like(l_ref)

        lses = l_ref[...]
        for hh in range(hp):
            blk, hal = c_ref[hh], p_ref[hh]
            q, k, v = blk[:, :HD], blk[:, HD:2 * HD], blk[:, 2 * HD:]
            if nsub > 1:
                kp = jnp.concatenate([hal[:, HD:2 * HD], k[:T - SPAN]], axis=0)
                vp = jnp.concatenate([hal[:, 2 * HD:], v[:T - SPAN]], axis=0)
            else:
                kp, vp = hal[:, HD:2 * HD], hal[:, 2 * HD:]
            q3, k3, v3, kp3, vp3 = (t.reshape(nsub, SPAN, HD) for t in (q, k, v, kp, vp))
            sc = jnp.where(m_cur, _bdot(q3, k3, 2, 2) * scale, NEG)
            sp = jnp.where(m_prev, _bdot(q3, kp3, 2, 2) * scale, NEG)
            m = jnp.maximum(jnp.max(sc, -1, keepdims=True), jnp.max(sp, -1, keepdims=True))
            pc, pp = jnp.exp(sc - m), jnp.exp(sp - m)
            den = jnp.sum(pc, -1, keepdims=True) + jnp.sum(pp, -1, keepdims=True)
            o = (_bdot(pc.astype(BF16), v3, 2, 1) + _bdot(pp.astype(BF16), vp3, 2, 1)) * (1.0 / den)
            o_ref[hh] = o.reshape(T, HD).astype(BF16)
            lses = jnp.where(lane == hb * hp + hh, (m + jnp.log(den)).reshape(T, 1), lses)
        l_ref[...] = lses

    cur = pl.BlockSpec((hp, T, QKV), lambda r, ib, h: (h, r * nib + ib, 0))
    prev = pl.BlockSpec((hp, SPAN, QKV), lambda r, ib, h: (h, r * (Sd // SPAN) + jnp.maximum(ib * nsub - 1, 0), 0))
    return pl.pallas_call(
        body, name=name, grid=(dil, nib, NH // hp), in_specs=[cur, prev],
        out_specs=[pl.BlockSpec((hp, T, HD), lambda r, ib, h: (h, r * nib + ib, 0)),
                   pl.BlockSpec((T, HD), lambda r, ib, h: (r * nib + ib, 0))],
        out_shape=[jax.ShapeDtypeStruct((NH, S, HD), BF16), jax.ShapeDtypeStruct((S, HD), F32)],
        compiler_params=_cparams(("parallel", "parallel", "arbitrary")),
    )(qkv, qkv)


def _attn_bwd(qkv, d_attn, lse, delta, dil, name):
    S = qkv.shape[1]
    Sd, T, nsub, nib, hp = _attn_tiles(S, dil)
    scale = HD ** -0.5
    ne = nsub + 1

    def body(c_ref, p_ref, n_ref, do_ref, don_ref, l_ref, ln_ref, dl_ref, dln_ref, o_ref):
        ib, hb = pl.program_id(1), pl.program_id(2)
        _, m_prev = _band(ne, ib > 0, ib < nib - 1)
        m_cur, _ = _band(nsub, True)
        for hh in range(hp):
            h = hb * hp + hh
            blk, hal, nxt = c_ref[hh], p_ref[hh], n_ref[hh]
            q, k, v = blk[:, :HD], blk[:, HD:2 * HD], blk[:, 2 * HD:]
            do = do_ref[hh]
            l, dl = _head_col(l_ref[...], h), _head_col(dl_ref[...], h)
            qe = jnp.concatenate([q, nxt[:, :HD]], axis=0).reshape(ne, SPAN, HD)
            doe = jnp.concatenate([do, don_ref[hh]], axis=0).reshape(ne, SPAN, HD)
            le = jnp.concatenate([l, _head_col(ln_ref[...], h)], axis=0).reshape(ne, SPAN, 1)
            dle = jnp.concatenate([dl, _head_col(dln_ref[...], h)], axis=0).reshape(ne, SPAN, 1)
            kpe = jnp.concatenate([hal[:, HD:2 * HD], k], axis=0).reshape(ne, SPAN, HD)
            vpe = jnp.concatenate([hal[:, 2 * HD:], v], axis=0).reshape(ne, SPAN, HD)
            p = jnp.where(m_prev, jnp.exp(_bdot(qe, kpe, 2, 2) * scale - le), 0.0)
            ds = (p * (_bdot(doe, vpe, 2, 2) - dle)).astype(BF16)
            dq = _bdot(ds, kpe, 2, 1)[:nsub]
            dk = _bdot(ds, qe, 1, 1)[1:]
            dv = _bdot(p.astype(BF16), doe, 1, 1)[1:]
            q3, k3, v3, do3 = (t.reshape(nsub, SPAN, HD) for t in (q, k, v, do))
            l3, dl3 = l.reshape(nsub, SPAN, 1), dl.reshape(nsub, SPAN, 1)
            p = jnp.where(m_cur, jnp.exp(_bdot(q3, k3, 2, 2) * scale - l3), 0.0)
            ds = (p * (_bdot(do3, v3, 2, 2) - dl3)).astype(BF16)
            dq = (dq + _bdot(ds, k3, 2, 1)) * scale
            dk = (dk + _bdot(ds, q3, 1, 1)) * scale
            dv = dv + _bdot(p.astype(BF16), do3, 1, 1)
            o_ref[hh] = jnp.concatenate([t.reshape(T, HD) for t in (dq, dk, dv)], axis=1).astype(BF16)

    nb = Sd // SPAN
    row = lambda r, ib: r * nib + ib
    prow = lambda r, ib: r * nb + jnp.maximum(ib * nsub - 1, 0)
    nrow = lambda r, ib: r * nb + jnp.minimum((ib + 1) * nsub, nb - 1)
    cur3 = pl.BlockSpec((hp, T, QKV), lambda r, ib, h: (h, row(r, ib), 0))
    prev3 = pl.BlockSpec((hp, SPAN, QKV), lambda r, ib, h: (h, prow(r, ib), 0))
    next3 = pl.BlockSpec((hp, SPAN, QKV), lambda r, ib, h: (h, nrow(r, ib), 0))
    cur1 = pl.BlockSpec((hp, T, HD), lambda r, ib, h: (h, row(r, ib), 0))
    next1 = pl.BlockSpec((hp, SPAN, HD), lambda r, ib, h: (h, nrow(r, ib), 0))
    curc = pl.BlockSpec((T, HD), lambda r, ib, h: (row(r, ib), 0))
    nextc = pl.BlockSpec((SPAN, HD), lambda r, ib, h: (nrow(r, ib), 0))
    return pl.pallas_call(
        body, name=name, grid=(dil, nib, NH // hp),
        in_specs=[cur3, prev3, next3, cur1, next1, curc, nextc, curc, nextc], out_specs=cur3,
        out_shape=jax.ShapeDtypeStruct((NH, S, QKV), BF16),
        compiler_params=_cparams(("parallel", "parallel", "parallel")),
    )(qkv, qkv, qkv, d_attn, d_attn, lse, lse, delta, delta)


def _ret_consts():
    lg = np.log1p(-np.exp2(-5.0 - np.arange(RH, dtype=np.float64)))
    idx = np.arange(CH, dtype=np.float64)
    rel = idx[:, None] - idx[None, :]
    intra = np.where(rel >= 0, np.exp(lg[:, None, None] * np.maximum(rel, 0.0)), 0.0)
    qd = np.exp(lg[:, None] * (idx + 1.0))
    kd = np.exp(lg[:, None] * (CH - 1.0 - idx))
    cd = np.exp(lg * CH)
    wide = lambda t: np.broadcast_to(t[:, :, None], (RH, t.shape[1], RDV))
    return (jnp.asarray(intra, F32), jnp.asarray(wide(qd), F32), jnp.asarray(wide(kd), F32),
            jnp.asarray(np.broadcast_to(cd[:, None, None], (RH, 1, RDV)), F32))


def _rot(t, c, s):
    t1, t2 = t[:, :RDK // 2], t[:, RDK // 2:]
    return jnp.concatenate([t1 * c - t2 * s, t1 * s + t2 * c], axis=1)


def _unrot(d, c, s):
    d1, d2 = d[:, :RDK // 2], d[:, RDK // 2:]
    return jnp.concatenate([d1 * c + d2 * s, d2 * c - d1 * s], axis=1)


RCH = 4


def _ret_specs(nmap):
    rows = RCH * CH
    q = pl.BlockSpec((rows, RH * RDK), lambda n: (nmap(n), OFF_RQ // (RH * RDK)))
    k = pl.BlockSpec((rows, RH * RDK), lambda n: (nmap(n), OFF_RK // (RH * RDK)))
    v = pl.BlockSpec((rows, RH * RDV), lambda n: (nmap(n), OFF_RV // (RH * RDV)))
    cs = pl.BlockSpec((rows, RDK // 2), lambda n: (nmap(n), 0))
    dmat = pl.BlockSpec((RH, CH, CH), lambda n: (0, 0, 0))
    dvec = pl.BlockSpec((RH, CH, RDV), lambda n: (0, 0, 0))
    cdv = pl.BlockSpec((RH, 1, RDV), lambda n: (0, 0, 0))
    state = pl.BlockSpec((RH, RCH, RDK, RDV), lambda n: (0, nmap(n), 0, 0))
    out = pl.BlockSpec((rows, RH * RDV), lambda n: (nmap(n), 0))
    return [q, k, v, cs, cs, dmat, dvec, dvec, cdv], state, out


def _ret_fwd(proj, cos, sin, consts):
    S = proj.shape[0]
    nc = S // CH

    def body(q_ref, k_ref, v_ref, c_ref, s_ref, d_ref, qd_ref, kd_ref, cd_ref, o_ref, st_ref, state):
        @pl.when(pl.program_id(0) == 0)
        def _():
            state[...] = jnp.zeros_like(state)

        for ci in range(RCH):
            rows = slice(ci * CH, (ci + 1) * CH)
            c, s = c_ref[rows, :], s_ref[rows, :]
            for h in range(RH):
                qk, vv = slice(h * RDK, (h + 1) * RDK), slice(h * RDV, (h + 1) * RDV)
                qb = _rot(q_ref[rows, qk].astype(F32), c, s).astype(BF16)
                kb = (_rot(k_ref[rows, qk].astype(F32), c, s) * (RDK ** -0.5)).astype(BF16)
                vb = v_ref[rows, vv]
                sb = state[h].astype(BF16)
                st_ref[h, ci] = sb
                a = (_dot(qb, kb, 1, 1) * d_ref[h]).astype(BF16)
                o_ref[rows, vv] = (_dot(a, vb, 1, 0) + _dot(qb, sb, 1, 0) * qd_ref[h]).astype(BF16)
                vk = (vb.astype(F32) * kd_ref[h]).astype(BF16)
                state[h] = cd_ref[h] * state[h] + _dot(kb, vk, 0, 0)

    ins, state_spec, out_spec = _ret_specs(lambda n: n)
    return pl.pallas_call(
        body, name="ret_fwd", grid=(nc // RCH,), in_specs=ins, out_specs=[out_spec, state_spec],
        out_shape=[jax.ShapeDtypeStruct((S, RH * RDV), BF16), jax.ShapeDtypeStruct((RH, nc, RDK, RDV), BF16)],
        scratch_shapes=[pltpu.VMEM((RH, RDK, RDV), F32)],
        compiler_params=_cparams(("arbitrary",)),
    )(proj, proj, proj, cos, sin, *consts)


def _ret_bwd(proj, cos, sin, consts, states, d_ret, d_rest):
    S = proj.shape[0]
    nc = S // CH

    def body(q_ref, k_ref, v_ref, c_ref, s_ref, d_ref, qd_ref, kd_ref, cd_ref, st_ref, do_ref, _, o_ref, dstate):
        @pl.when(pl.program_id(0) == 0)
        def _():
            dstate[...] = jnp.zeros_like(dstate)

        for ci in reversed(range(RCH)):
            rows = slice(ci * CH, (ci + 1) * CH)
            c, s = c_ref[rows, :], s_ref[rows, :]
            for h in range(RH):
                qk, vv = slice(h * RDK, (h + 1) * RDK), slice(h * RDV, (h + 1) * RDV)
                qb = _rot(q_ref[rows, qk].astype(F32), c, s).astype(BF16)
                kb = (_rot(k_ref[rows, qk].astype(F32), c, s) * (RDK ** -0.5)).astype(BF16)
                vb, sb, do = v_ref[rows, vv], st_ref[h, ci], do_ref[rows, vv]
                dmat, qd, kd = d_ref[h], qd_ref[h], kd_ref[h]
                a = (_dot(qb, kb, 1, 1) * dmat).astype(BF16)
                doq = (do.astype(F32) * qd).astype(BF16)
                dsb = dstate[h].astype(BF16)
                vk = (vb.astype(F32) * kd).astype(BF16)
                o_ref[rows, OFF_RV + h * RDV:OFF_RV + (h + 1) * RDV] = (_dot(a, do, 0, 0) + _dot(kb, dsb, 1, 0) * kd).astype(BF16)
                da = (_dot(do, vb, 1, 1) * dmat).astype(BF16)
                dq = _dot(da, kb, 1, 0) + _dot(doq, sb, 1, 1)
                dk = (_dot(da, qb, 0, 0) + _dot(vk, dsb, 1, 1)) * (RDK ** -0.5)
                o_ref[rows, OFF_RQ + h * RDK:OFF_RQ + (h + 1) * RDK] = _unrot(dq, c, s).astype(BF16)
                o_ref[rows, OFF_RK + h * RDK:OFF_RK + (h + 1) * RDK] = _unrot(dk, c, s).astype(BF16)
                dstate[h] = cd_ref[h] * dstate[h] + _dot(qb, doq, 0, 0)

    nsteps = nc // RCH
    rev = lambda n: nsteps - 1 - n
    ins, state_spec, out_spec = _ret_specs(rev)
    return pl.pallas_call(
        body, name="ret_bwd", grid=(nsteps,), in_specs=ins + [state_spec, out_spec, pl.BlockSpec(memory_space=pl.ANY)],
        out_specs=pl.BlockSpec((RCH * CH, OFF_RG), lambda n: (rev(n), 0)),
        out_shape=jax.ShapeDtypeStruct(d_rest.shape, BF16), input_output_aliases={11: 0},
        scratch_shapes=[pltpu.VMEM((RH, RDK, RDV), F32)],
        compiler_params=_cparams(("arbitrary",)),
    )(proj, proj, proj, cos, sin, *consts, states, d_ret, d_rest)


CW = 256
HALO = 16


def _shift_down(v, halo, k):
    rolled = pltpu.roll(v, k, 0)
    hr = pltpu.roll(halo, k, 0)[0:8]
    row = lax.broadcasted_iota(jnp.int32, hr.shape, 0)
    return jnp.concatenate([jnp.where(row < k, hr, rolled[0:8]), rolled[8:]], axis=0)


def _shift_up(v, halo, k):
    T = v.shape[0]
    rolled = pltpu.roll(v, T - k, 0)
    hr = pltpu.roll(halo, 8 - k, 0)[0:8]
    row = lax.broadcasted_iota(jnp.int32, hr.shape, 0)
    return jnp.concatenate([rolled[:T - 8], jnp.where(row >= 8 - k, hr, rolled[T - 8:])], axis=0)


def _conv_taps(h_ref, hp_ref, first):
    h = h_ref[...].astype(F32)
    hp = hp_ref[...].astype(F32) * jnp.where(first, 0.0, 1.0)
    return _shift_down(h, hp, 2), _shift_down(h, hp, 1), h


def _conv_specs(S, T, cw=CW):
    nj = DFF // cw
    cur = pl.BlockSpec((T, cw), lambda j, i: (i, j))
    prev = pl.BlockSpec((HALO, cw), lambda j, i: (jnp.maximum(i * (T // HALO) - 1, 0), j))
    nxt = pl.BlockSpec((HALO, cw), lambda j, i: (jnp.minimum((i + 1) * (T // HALO), S // HALO - 1), j))
    w = pl.BlockSpec((3, cw), lambda j, i: (0, j))
    b = pl.BlockSpec((1, cw), lambda j, i: (0, j))
    return nj, cur, prev, nxt, w, b


def _conv_fwd(hg, hu, wg, wu, bg, bu):
    S = hg.shape[0]
    T = min(1024, S)
    nj, cur, prev, _, w, b = _conv_specs(S, T)

    def body(hg_ref, hu_ref, hgp_ref, hup_ref, wg_ref, wu_ref, bg_ref, bu_ref, o_ref):
        first = pl.program_id(1) == 0
        g2, g1, g0 = _conv_taps(hg_ref, hgp_ref, first)
        u2, u1, u0 = _conv_taps(hu_ref, hup_ref, first)
        cg = wg_ref[0:1, :] * g2 + wg_ref[1:2, :] * g1 + wg_ref[2:3, :] * g0 + bg_ref[...]
        cu = wu_ref[0:1, :] * u2 + wu_ref[1:2, :] * u1 + wu_ref[2:3, :] * u0 + bu_ref[...]
        o_ref[...] = (_gelu(cg) * cu).astype(BF16)

    return pl.pallas_call(
        body, name="conv_fwd", grid=(nj, S // T), in_specs=[cur, cur, prev, prev, w, w, b, b], out_specs=cur,
        out_shape=jax.ShapeDtypeStruct((S, DFF), BF16), compiler_params=_cparams(("parallel", "parallel")),
    )(hg, hu, hg, hu, wg, wu, bg, bu)


def _conv_bwd_pre(d_act, hg, hu, wg, wu, bg, bu):
    S = hg.shape[0]
    T = min(1024, S)
    nj, cur, prev, _, w, b = _conv_specs(S, T)

    def body(da_ref, hg_ref, hu_ref, hgp_ref, hup_ref, wg_ref, wu_ref, bg_ref, bu_ref,
             dcg_ref, dcu_ref, gwg_ref, gwu_ref, gbg_ref, gbu_ref):
        first = pl.program_id(1) == 0
        g2, g1, g0 = _conv_taps(hg_ref, hgp_ref, first)
        u2, u1, u0 = _conv_taps(hu_ref, hup_ref, first)
        cg = wg_ref[0:1, :] * g2 + wg_ref[1:2, :] * g1 + wg_ref[2:3, :] * g0 + bg_ref[...]
        cu = wu_ref[0:1, :] * u2 + wu_ref[1:2, :] * u1 + wu_ref[2:3, :] * u0 + bu_ref[...]
        da = da_ref[...].astype(F32)
        gl, dgl = _gelu(cg, with_grad=True)
        dcg = da * cu * dgl
        dcu = da * gl
        dcg_ref[...] = dcg.astype(BF16)
        dcu_ref[...] = dcu.astype(BF16)

        @pl.when(first)
        def _():
            for r in (gwg_ref, gwu_ref, gbg_ref, gbu_ref):
                r[...] = jnp.zeros_like(r)

        for r, d, taps in ((gwg_ref, dcg, (g2, g1, g0)), (gwu_ref, dcu, (u2, u1, u0))):
            for j in range(3):
                r[j:j + 1, :] += jnp.sum(d * taps[j], 0, keepdims=True)
        gbg_ref[...] += jnp.sum(dcg, 0, keepdims=True)
        gbu_ref[...] += jnp.sum(dcu, 0, keepdims=True)

    return pl.pallas_call(
        body, name="conv_bwd_pre", grid=(nj, S // T), in_specs=[cur, cur, cur, prev, prev, w, w, b, b],
        out_specs=[cur, cur, w, w, b, b],
        out_shape=[jax.ShapeDtypeStruct((S, DFF), BF16)] * 2 + [jax.ShapeDtypeStruct((3, DFF), F32)] * 2
        + [jax.ShapeDtypeStruct((1, DFF), F32)] * 2,
        compiler_params=_cparams(("parallel", "arbitrary")),
    )(d_act, hg, hu, hg, hu, wg, wu, bg, bu)


def _conv_bwd_in(dc, w, name):
    S = dc.shape[0]
    T = min(512, S)
    nj, cur, _, nxt, wspec, _ = _conv_specs(S, T, DFF // 2)
    nt = S // T

    def body(dc_ref, dn_ref, w_ref, o_ref):
        d = dc_ref[...].astype(F32)
        dn = dn_ref[...].astype(F32) * jnp.where(pl.program_id(1) == nt - 1, 0.0, 1.0)
        o_ref[...] = (w_ref[2:3, :] * d + w_ref[1:2, :] * _shift_up(d, dn, 1) + w_ref[0:1, :] * _shift_up(d, dn, 2)).astype(BF16)

    return pl.pallas_call(
        body, name=name, grid=(nj, nt), in_specs=[cur, nxt, wspec], out_specs=cur,
        out_shape=jax.ShapeDtypeStruct((S, DFF), BF16), compiler_params=_cparams(("parallel", "parallel")),
    )(dc, dc, w)


def _adam_math(g, w, m, v):
    m = B1 * m + (1.0 - B1) * g
    v = B2 * v + (1.0 - B2) * (g * g)
    m_hat = m / (1.0 - B1 ** STEP)
    v_hat = v / (1.0 - B2 ** STEP)
    return -LR * (m_hat / (jnp.sqrt(v_hat) + EPS) + WD * w), m, v


def _reduce_tail(chip32, far, chip, name, wmv=None):
    L = len(chip32)
    _, R, C = chip32[0].shape
    tr = _tile(R, 256, 16)
    nr = R // tr

    def body(chip_ref, *refs):
        own_refs, far_refs, rest = refs[:L], refs[L:2 * L], refs[2 * L:]
        outs = rest[3:] if wmv else rest
        for ll in range(L):
            @pl.when(pl.program_id(0) == ll)
            def _(ll=ll):
                g = own_refs[ll][...]
                for s in range(3):
                    g = g + far_refs[ll][s].astype(F32)
                outs[0][...] = g
                if wmv:
                    outs[1][...], outs[2][...], outs[3][...] = _adam_math(g, rest[0][...], rest[1][...], rest[2][...])

    def rows(ll):
        return lambda l, i: jnp.where(l == ll, i, jnp.where(l < ll, 0, nr - 1))

    blk = pl.BlockSpec((None, tr, C), lambda l, i, ch: (l, i, 0))
    in_specs = [pl.BlockSpec((None, tr, C), lambda l, i, ch, f=rows(ll): (ch[0], f(l, i), 0)) for ll in range(L)]
    in_specs += [pl.BlockSpec((3, tr, C), lambda l, i, ch, f=rows(ll): (0, f(l, i), 0)) for ll in range(L)]
    args = list(chip32) + list(far)
    n_out = 1
    if wmv:
        in_specs += [blk] * 3
        args += list(wmv)
        n_out = 4
    return pl.pallas_call(
        body, name=name,
        grid_spec=pltpu.PrefetchScalarGridSpec(num_scalar_prefetch=1, grid=(L, nr), in_specs=in_specs, out_specs=[blk] * n_out),
        out_shape=[jax.ShapeDtypeStruct((L, R, C), F32)] * n_out, compiler_params=_cparams(("arbitrary", "arbitrary")),
    )(chip, *args)


def _adamw(g, w, m, v, name):
    R, C = g.shape
    tr = _tile(R, 128, 8)

    def body(g_ref, w_ref, m_ref, v_ref, d_ref, nm_ref, nv_ref):
        d_ref[...], nm_ref[...], nv_ref[...] = _adam_math(g_ref[...], w_ref[...], m_ref[...], v_ref[...])

    blk = pl.BlockSpec((tr, C), lambda i: (i, 0))
    return pl.pallas_call(
        body, name=name, grid=(R // tr,), in_specs=[blk] * 4, out_specs=[blk] * 3,
        out_shape=[jax.ShapeDtypeStruct(g.shape, F32)] * 3, compiler_params=_cparams(("parallel",)),
    )(g, w, m, v)


def _pair_sum(x, recv, core, name):
    _, R, C = x.shape
    tr = _tile(R, 600, 16)

    def body(core_ref, x_ref, r_ref, o32_ref, o16_ref):
        s = x_ref[...] + r_ref[...]
        o32_ref[...] = s
        o16_ref[...] = s.astype(BF16)

    blk = pl.BlockSpec((None, tr, C), lambda q, i, c: (q, i, 0))
    mine = pl.BlockSpec((None, None, tr, C), lambda q, i, c: (q, c[0], i, 0))
    return pl.pallas_call(
        body, name=name,
        grid_spec=pltpu.PrefetchScalarGridSpec(num_scalar_prefetch=1, grid=(4, R // tr), in_specs=[mine, blk], out_specs=[blk, blk]),
        out_shape=[jax.ShapeDtypeStruct((4, R, C), F32), jax.ShapeDtypeStruct((4, R, C), BF16)],
        compiler_params=_cparams(("parallel", "parallel")),
    )(core, x.reshape(4, 2, R, C), recv)


def _sum_slots(x, name):
    def body(x_ref, o_ref):
        g = x_ref[0]
        for s in range(1, x.shape[0]):
            g = g + x_ref[s]
        o_ref[...] = g

    return pl.pallas_call(body, name=name, out_shape=jax.ShapeDtypeStruct(x.shape[1:], F32))(x)


MESH = pl.DeviceIdType.MESH
_HBM = pl.BlockSpec(memory_space=pltpu.HBM)


def _dma_sems(n):
    return pltpu.SemaphoreType.DMA((n,))


def _gather_many(xs, name):
    n = len(xs)

    def body(*refs):
        x_refs, out_refs = refs[:n], refs[n:2 * n]
        send_sems, recv_sems, local_sems = refs[2 * n:]
        ax, ay, ac = lax.axis_index("x"), lax.axis_index("y"), lax.axis_index("c")
        me, sibling = (ax, ay, ac), (ax, ay, 1 - ac)
        chips = [(1 - ax, ay), (ax, 1 - ay), (1 - ax, 1 - ay)]

        def copy(a, k, block, to, own=False):
            slot = out_refs[a].at[4 * block[0] + 2 * block[1] + block[2]]
            return pltpu.make_async_remote_copy(
                src_ref=x_refs[a] if own else slot, dst_ref=slot, send_sem=send_sems.at[7 * a + k],
                recv_sem=recv_sems.at[7 * a + k], device_id=to, device_id_type=MESH)

        mine = [pltpu.make_async_copy(x_refs[a], out_refs[a].at[4 * ax + 2 * ay + ac], local_sems.at[a]) for a in range(n)]
        first = [copy(a, 0, me, sibling, own=True) for a in range(n)]
        first += [copy(a, 1 + j, me, (*chip, ac), own=True) for j, chip in enumerate(chips) for a in range(n)]
        for cp in mine + first:
            cp.start()
        passed = []
        for j, chip in enumerate(chips):
            for a in range(n):
                copy(a, 1 + j, (*chip, ac), me).wait_recv()
                cp = copy(a, 4 + j, (*chip, ac), sibling)
                cp.start()
                passed.append(cp)
        for a in range(n):
            copy(a, 0, sibling, me).wait_recv()
            for j, chip in enumerate(chips):
                copy(a, 4 + j, (*chip, 1 - ac), me).wait_recv()
        for cp in first + passed:
            cp.wait_send()
        for cp in mine:
            cp.wait()

    return pl.pallas_call(
        body, name=name, out_shape=[jax.ShapeDtypeStruct((N_DEV,) + x.shape, x.dtype) for x in xs],
        in_specs=[_HBM] * n, out_specs=[_HBM] * n, scratch_shapes=[_dma_sems(7 * n), _dma_sems(7 * n), _dma_sems(n)],
    )(*xs)


_SEM = pl.BlockSpec(memory_space=pltpu.SEMAPHORE)
_EFFECT = pltpu.SideEffectType.DATAFLOW_SIDE_EFFECTING


def _peer(k):
    ax, ay, ac = lax.axis_index("x"), lax.axis_index("y"), lax.axis_index("c")
    px = 1 - ax if k & 4 else ax
    py = 1 - ay if k & 2 else ay
    pc = 1 - ac if k & 1 else ac
    return (px, py, pc), 4 * px + 2 * py + pc


def _build_gather(x_refs, land_refs, send_sems, recv_sems, waiting):
    _, me = _peer(0)
    copies = []
    for a in range(len(x_refs)):
        for k in range(1, N_DEV):
            peer, slot = _peer(k)
            copies.append(pltpu.make_async_remote_copy(
                src_ref=x_refs[a], dst_ref=land_refs[a].at[slot if waiting else me], send_sem=send_sems.at[7 * a + k - 1],
                recv_sem=recv_sems.at[7 * a + k - 1], device_id=peer, device_id_type=MESH))
    return copies


def _build_cores(x_refs, land_refs, send_sems, recv_sems, waiting):
    ax, ay, ac = lax.axis_index("x"), lax.axis_index("y"), lax.axis_index("c")
    copies = []
    for a in range(len(x_refs)):
        for q in range(4):
            copies.append(pltpu.make_async_remote_copy(
                src_ref=x_refs[a].at[2 * q + 1 - ac], dst_ref=land_refs[a].at[q], send_sem=send_sems.at[4 * a + q],
                recv_sem=recv_sems.at[4 * a + q], device_id=(ax, ay, 1 - ac), device_id_type=MESH))
    return copies


def _build_chips(p_refs, land_refs, send_sems, recv_sems, waiting):
    ax, ay, ac = lax.axis_index("x"), lax.axis_index("y"), lax.axis_index("c")
    copies = []
    for a in range(len(p_refs)):
        for k in range(1, 4):
            px = 1 - ax if k & 2 else ax
            py = 1 - ay if k & 1 else ay
            copies.append(pltpu.make_async_remote_copy(
                src_ref=p_refs[a].at[2 * px + py], dst_ref=land_refs[a].at[k - 1], send_sem=send_sems.at[3 * a + k - 1],
                recv_sem=recv_sems.at[3 * a + k - 1], device_id=(px, py, ac), device_id_type=MESH))
    return copies


_EXCHANGES = {"gather": (_build_gather, 7, N_DEV), "cores": (_build_cores, 4, 4), "chips": (_build_chips, 3, 3)}


def _exchange_start(kind, xs, lands, name, after=None):
    build, per, _ = _EXCHANGES[kind]
    n = len(xs)

    def body(*refs):
        for cp in build(refs[:n], refs[n:2 * n], refs[-2 * n - 3], refs[-2 * n - 2], False):
            cp.start()
        refs[-1][...] = jnp.zeros_like(refs[-1])

    hbm = lambda t: pltpu.HBM(t.shape, t.dtype)
    args = [pltpu.with_memory_space_constraint(t, pltpu.HBM) for t in list(xs) + list(lands)]
    in_specs = [_HBM] * (2 * n)
    if after is not None:
        args.append(after)
        in_specs.append(pl.BlockSpec(memory_space=pl.ANY))
    outs = pl.pallas_call(
        body, name=name,
        out_shape=(_dma_sems(per * n), _dma_sems(per * n), *[hbm(t) for t in xs], *[hbm(t) for t in lands],
                   jax.ShapeDtypeStruct((8, 128), F32)),
        in_specs=in_specs, out_specs=(_SEM, _SEM, *[_HBM] * (2 * n), pl.BlockSpec(memory_space=pltpu.VMEM)),
        input_output_aliases={a: 2 + a for a in range(2 * n)},
        compiler_params=pltpu.CompilerParams(has_side_effects=_EFFECT),
    )(*args)
    return (kind, outs[0], outs[1], outs[2:2 + n], outs[2 + n:2 + 2 * n]), outs[-1]


def _exchange_wait(flight, after, name):
    kind, send_sems, recv_sems, xs, lands = flight
    build = _EXCHANGES[kind][0]
    n = len(xs)

    def body(*refs):
        for cp in build(refs[:n], refs[n:2 * n], refs[2 * n], refs[2 * n + 1], True):
            cp.wait_send()
            cp.wait_recv()

    hbm = lambda t: pltpu.HBM(t.shape, t.dtype)
    outs = pl.pallas_call(
        body, name=name, out_shape=(*[hbm(t) for t in xs], *[hbm(t) for t in lands]),
        in_specs=[_HBM] * (2 * n) + [_SEM, _SEM, pl.BlockSpec(memory_space=pl.ANY)], out_specs=[_HBM] * (2 * n),
        input_output_aliases={a: a for a in range(2 * n)}, compiler_params=pltpu.CompilerParams(has_side_effects=_EFFECT),
    )(*xs, *lands, send_sems, recv_sems, after)
    return outs[:n], outs[n:]


def _bf16_views(fn, dilations):
    extra = [(D, BF16) if d == 1 else ("dilated", D, BF16, d) for d in dilations]

    def wrapped(*a):
        res = fn(*a)
        res = tuple(res) if isinstance(res, (tuple, list)) else (res,)
        return res + (res[-1],) * len(extra)

    return extra, wrapped


def _layer_fwd(x, xb, p, w, cos, sin, rconsts, late=None, views_out=True):
    proj = _mm(xb[1], w["win"], tb=True, b_rows=(N_ATT, N_REST), name="mm_proj", out_dtype=BF16)
    qkvs, ogs, lgs = [], [], []
    for g, dil in enumerate(DILATIONS):
        qkv = _qkv_fwd(xb[dil], w["win"], g, dil, f"mm_qkv{g}")
        o, l = _attn_fwd(qkv, dil, f"attn_fwd_g{g}")
        qkvs.append(qkv)
        ogs.append(_to_tokens(o, dil))
        lgs.append(_to_tokens(l, dil))
    attn, lse = _rowwise(_f_combine, ogs + lgs, [], [(D, BF16), (HD, F32)], [], name="attn_combine")
    ret_raw, states = _ret_fwd(proj, cos, sin, rconsts)
    rg_win = (proj, RH * RDV, OFF_RG // (RH * RDV))
    ga_win, gr_win = (proj, D, OFF_GA // D), (proj, D, OFF_GR // D)
    (r,) = _rowwise(_f_gn, [ret_raw, rg_win], [w["ret_gn_g"], w["ret_gn_b"]], [(RH * RDV, BF16)], [], name="gn_fwd", tm=256)
    if late is not None:
        w = {**w, **late(r)}
    ap = _mm(attn, w["w_attn_proj"], name="mm_attn_proj", out_dtype=BF16)
    rp = _mm(r, w["w_ret_proj"], name="mm_ret_proj", out_dtype=BF16, tk=2048)
    (merged,) = _rowwise(_f_gate, [ap, rp, ga_win, gr_win], [], [(D, BF16)], [], name="gate_fwd")
    mix = _mm(merged, w["w_out"], name="mm_out")
    h1, x1, x1b = _rowwise(_f_ln1, [x, mix], [w["ln1_g"], w["ln1_b"]], [(D, F32), (D, F32), (D, BF16)], [], name="ln1_fwd")
    z = _mm(x1b, w["w_ple_gate"], name="mm_ple_gate")
    pp = _mm(p, w["w_ple_proj"], tb=True, name="mm_ple_proj")
    hg = _mm(x1b, w["w_up"], tb=True, b_rows=(0, DFF), name="mm_up_g", out_dtype=BF16, tm=512, tn=DFF)
    hu = _mm(x1b, w["w_up"], tb=True, b_rows=(DFF, DFF), name="mm_up_u", out_dtype=BF16, tm=512, tn=DFF)
    act = _conv_fwd(hg, hu, w["conv_wg"], w["conv_wu"], w["conv_bg"], w["conv_bu"])
    ffn = _mm(act, w["w_down"], name="mm_down", tm=512, tk=DFF)
    dils = DILATIONS if views_out else (1,)
    extra, fn = _bf16_views(_f_ln2, dils)
    h2, x2, *x2b = _rowwise(fn, [x1, ffn, z, pp], [w["ln2_g"], w["ln2_b"]], [(D, F32), (D, F32)] + extra, [], name="ln2_fwd")
    x2b = dict(zip(dils, x2b))
    saved = dict(xb=xb, proj=proj, qkvs=qkvs, attn=attn, lse=lse, ret_raw=ret_raw, states=states, r=r, ap=ap, rp=rp,
                 merged=merged, h1=h1, x1b=x1b, z=z, pp=pp, hg=hg, hu=hu, act=act, h2=h2, p=p)
    return x2, x2b, saved, w


def _after(fn, token):
    return fn if token is None else (lambda *a: fn(*a[:-1]))


def _layer_bwd(dys, w, sv, cos, sin, rconsts, hooks):
    gr = {}
    proj = sv["proj"]
    call = lambda key, *a: hooks[key](*a) if key in hooks else None
    held = lambda token: [] if token is None else [token]
    token = hooks.get("token")
    dh2, dh2b, dpp, dz, gr["ln2_g"], gr["ln2_b"] = _rowwise(
        _after(_f_ln2_bwd, token), list(dys) + [sv["h2"], sv["z"], sv["pp"]], [w["ln2_g"]] + held(token),
        [(D, F32), (D, BF16), (D, BF16), (D, BF16)], [(1, D), (1, D)], name="ln2_bwd")
    d_act = _mm(dh2b, w["w_down"], tb=True, name="mm_down_dx", out_dtype=BF16, tm=512, tn=DFF)
    gr["w_down"] = _mm(sv["act"], dh2b, ta=True, name="mm_down_dw", tm=DFF // 2)
    dcg, dcu, gwg, gwu, gbg, gbu = _conv_bwd_pre(d_act, sv["hg"], sv["hu"], w["conv_wg"], w["conv_wu"], w["conv_bg"], w["conv_bu"])
    token = call("after_ffn", dcg)
    gr["conv_w"] = jnp.concatenate([gwg, gwu], axis=1)
    gr["conv_b"] = jnp.concatenate([gbg, gbu], axis=1)
    dhg = _conv_bwd_in(dcg, w["conv_wg"], "conv_bwd_in_g")
    dhu = _conv_bwd_in(dcu, w["conv_wu"], "conv_bwd_in_u")
    gw_up = _mm(dhg, sv["x1b"], ta=True, name="mm_up_g_dw", tm=DFF // 2, out_rows=(0, 2 * DFF))
    gr["w_up"] = _mm(dhu, sv["x1b"], ta=True, name="mm_up_u_dw", tm=DFF // 2, out_rows=(DFF, 2 * DFF), into=gw_up)
    dx1 = _mm(dhg, w["w_up"], b_rows=(0, DFF), name="mm_up_g_dx", add=dh2, add_scale=ALPHA, tm=512, tk=DFF)
    dx1 = _mm(dhu, w["w_up"], b_rows=(DFF, DFF), name="mm_up_u_dx", add=dx1, tm=512, tk=DFF)
    gr["w_ple_proj"] = _mm(dpp, sv["p"], ta=True, name="mm_ple_proj_dw")
    gr["w_ple_gate"] = _mm(sv["x1b"], dz, ta=True, name="mm_ple_gate_dw")
    dx1 = _mm(dz, w["w_ple_gate"], tb=True, name="mm_ple_gate_dx", add=dx1)
    dh1, dh1b, gr["ln1_g"], gr["ln1_b"] = _rowwise(_after(_f_ln_bwd, token), [dx1, sv["h1"]], [w["ln1_g"]] + held(token),
                                                   [(D, F32), (D, BF16)], [(1, D), (1, D)], name="ln1_bwd")
    d_merged = _mm(dh1b, w["w_out"], tb=True, name="mm_out_dx", out_dtype=BF16)
    gr["w_out"] = _mm(sv["merged"], dh1b, ta=True, name="mm_out_dw")
    rg_win = (proj, RH * RDV, OFF_RG // (RH * RDV))
    ga_win, gr_win = (proj, D, OFF_GA // D), (proj, D, OFF_GR // D)
    dap, drp, d_rest = _rowwise(_f_gate_bwd, [d_merged, sv["ap"], sv["rp"], ga_win, gr_win], [],
                                [(D, BF16), (D, BF16), (2 * D, BF16, N_REST, OFF_GA // (2 * D), None)], [], name="gate_bwd")
    d_attn = _mm(dap, w["w_attn_proj"], tb=True, name="mm_attn_proj_dx", out_dtype=BF16)
    gr["w_attn_proj"] = _mm(sv["attn"], dap, ta=True, name="mm_attn_proj_dw")
    d_r = _mm(drp, w["w_ret_proj"], tb=True, name="mm_ret_proj_dx", out_dtype=BF16, tn=2048)
    gr["w_ret_proj"] = _mm(sv["r"], drp, ta=True, name="mm_ret_proj_dw", tm=2048)
    token = call("early_grads", gr)
    d_ret, d_rest, gr["ret_gn_g"], gr["ret_gn_b"] = _rowwise(
        _after(_f_gn_bwd, token), [d_r, sv["ret_raw"], rg_win], [w["ret_gn_g"], w["ret_gn_b"]] + held(token),
        [(RH * RDV, BF16), (RH * RDV, BF16, N_REST, OFF_RG // (RH * RDV), d_rest)],
        [(1, RH * RDV), (1, RH * RDV)], name="gn_bwd", tm=256)
    d_rest = _ret_bwd(proj, cos, sin, rconsts, sv["states"], d_ret, d_rest)
    token = call("after_ret", d_rest)
    (delta,) = _rowwise(_after(_f_delta, token), [d_attn, sv["attn"]], held(token), [(HD, F32)], [], name="attn_delta")
    gw_in, dqkvs = None, []
    for g, dil in enumerate(DILATIONS):
        dqkvs.append(_attn_bwd(sv["qkvs"][g], _to_head_residues(d_attn, dil), _to_residues(sv["lse"], dil),
                               _to_residues(delta, dil), dil, f"attn_bwd_g{g}"))
        gw_in = _qkv_dw(dqkvs[g], sv["xb"][dil], g, dil, f"mm_qkv{g}_dw", into=gw_in)
    gw_in = _mm(d_rest, sv["xb"][1], ta=True, name="mm_proj_dw", out_rows=(N_ATT, N_IN), into=gw_in, blocks8=True)
    gr["w_in"] = gw_in.reshape(N_IN, D)
    token = call("w_in_ready", gr["w_in"])
    dx0 = _mm(d_rest, w["win"], b_rows=(N_ATT, N_REST), name="mm_proj_dx", add=dh1, add_scale=ALPHA, after=token)
    dx_parts = []
    for g, dil in enumerate(DILATIONS):
        if dil == 1:
            dx0 = _qkv_dx(dqkvs[g], w["win"], g, dil, f"mm_qkv{g}_dx", F32, add=dx0)
            token = call("after_dx0", dx0)
        else:
            dx_parts.append(("dilated", _qkv_dx(dqkvs[g], w["win"], g, dil, f"mm_qkv{g}_dx", BF16, after=token), dil))
    return [dx0] + dx_parts, gr


def _local_step(x, p, positions, target, ws, own_hooks=None, on_grads=None):
    half = RDK // 2
    freq = jnp.power(ROPE_BASE, -jnp.arange(half, dtype=F32) / half)
    ang = positions.astype(F32)[:, None] * freq[None, :]
    cos, sin = jnp.cos(ang), jnp.sin(ang)
    rconsts = _ret_consts()
    extra, _ = _bf16_views(None, DILATIONS)
    xb = dict(zip(DILATIONS, _rowwise(lambda v: (v,) * len(extra), [x], [], extra, [], name="cast_x")))
    saved, ws = [], list(ws)
    for l in range(DEPTH):
        first, late = ws[l] if isinstance(ws[l], tuple) else (ws[l], None)
        if callable(first):
            first = first(x)
        x, xb, sv, ws[l] = _layer_fwd(x, xb, p[l], first, cos, sin, rconsts, late, views_out=l < DEPTH - 1)
        saved.append(sv)
    dy, loss_vec = _rowwise(_f_loss, [x, target], [], [(D, F32)], [(1, D)], name="loss")
    dys, grads = [dy], [None] * DEPTH
    from_above = {}
    for l in reversed(range(DEPTH)):
        hooks = {**from_above, **(own_hooks(l) if own_hooks else {})}
        dys, grads[l] = _layer_bwd(dys, ws[l], saved[l], cos, sin, rconsts, hooks)
        from_above = on_grads(l, grads[l]) if on_grads else {}
    (grad_x,) = _rowwise(_f_sum, dys, [], [(D, F32)], [], name="grad_x_sum")
    return loss_vec, grad_x, grads


def _pack_rows(arrs):
    parts, where, off = [], [], 0
    for t in arrs:
        t = t.reshape(-1, D)
        rows = t.shape[0]
        padded = -(-rows // 8) * 8
        parts.append(jnp.pad(t, ((0, padded - rows), (0, 0))))
        where.append((off, rows))
        off += padded
    return jnp.concatenate(parts, axis=0), where


FIRST = ("w_in",)
LATER = tuple(n for n in BIG if n not in FIRST)


def _first_weights(g, l, W):
    w = dict(win=g["w_in"].reshape(N_IN, D))
    for n in ("ret_gn_g", "ret_gn_b", "ln1_g", "ln1_b", "ln2_g", "ln2_b"):
        w[n] = W[n][l][None, :]
    return w


def _later_weights(g, l, conv_w_all, conv_b):
    w = dict(w_up=g["w_up"].reshape(2 * DFF, D), w_ple_proj=g["w_ple_proj"].reshape(D, PLE),
             w_attn_proj=g["w_attn_proj"].reshape(D, D), w_ret_proj=g["w_ret_proj"].reshape(RH * RDV, D),
             w_out=g["w_out"].reshape(D, D), w_down=g["w_down"].reshape(DFF, D), w_ple_gate=g["w_ple_gate"].reshape(D, D))
    w["conv_wg"], w["conv_wu"] = conv_w_all[l][:, :DFF], conv_w_all[l][:, DFF:]
    w["conv_bg"], w["conv_bu"] = conv_b[l][None, :DFF], conv_b[l][None, DFF:]
    return w


def _layer_weights(g, l, conv_w_all, conv_b, W):
    return {**_first_weights(g, l, W), **_later_weights(g, l, conv_w_all, conv_b)}


def kernel(x, p, positions, w_in, w_attn_proj, w_ret_proj, ret_gn_g, ret_gn_b, w_out, ln1_g, ln1_b, w_up, conv_w, conv_b, w_down, w_ple_gate, w_ple_proj, ln2_g, ln2_b, loss_target, m_w_in, m_w_attn_proj, m_w_ret_proj, m_ret_gn_g, m_ret_gn_b, m_w_out, m_ln1_g, m_ln1_b, m_w_up, m_conv_w, m_conv_b, m_w_down, m_w_ple_gate, m_w_ple_proj, m_ln2_g, m_ln2_b, v_w_in, v_w_attn_proj, v_w_ret_proj, v_ret_gn_g, v_ret_gn_b, v_w_out, v_ln1_g, v_ln1_b, v_w_up, v_conv_w, v_conv_b, v_w_down, v_w_ple_gate, v_w_ple_proj, v_ln2_g, v_ln2_b):
    W = dict(w_in=w_in, w_attn_proj=w_attn_proj, w_ret_proj=w_ret_proj, ret_gn_g=ret_gn_g, ret_gn_b=ret_gn_b, w_out=w_out,
             ln1_g=ln1_g, ln1_b=ln1_b, w_up=w_up, conv_w=conv_w, conv_b=conv_b, w_down=w_down, w_ple_gate=w_ple_gate,
             w_ple_proj=w_ple_proj, ln2_g=ln2_g, ln2_b=ln2_b)
    M = dict(w_in=m_w_in, w_attn_proj=m_w_attn_proj, w_ret_proj=m_w_ret_proj, ret_gn_g=m_ret_gn_g, ret_gn_b=m_ret_gn_b,
             w_out=m_w_out, ln1_g=m_ln1_g, ln1_b=m_ln1_b, w_up=m_w_up, conv_w=m_conv_w, conv_b=m_conv_b, w_down=m_w_down,
             w_ple_gate=m_w_ple_gate, w_ple_proj=m_w_ple_proj, ln2_g=m_ln2_g, ln2_b=m_ln2_b)
    V = dict(w_in=v_w_in, w_attn_proj=v_w_attn_proj, w_ret_proj=v_w_ret_proj, ret_gn_g=v_ret_gn_g, ret_gn_b=v_ret_gn_b,
             w_out=v_w_out, ln1_g=v_ln1_g, ln1_b=v_ln1_b, w_up=v_w_up, conv_w=v_conv_w, conv_b=v_conv_b, w_down=v_w_down,
             w_ple_gate=v_w_ple_gate, w_ple_proj=v_w_ple_proj, ln2_g=v_ln2_g, ln2_b=v_ln2_b)

    me = 4 * lax.axis_index("x") + 2 * lax.axis_index("y") + lax.axis_index("c")
    shard = lambda n, l: (W[n][l].T if n in COL_SHARDED else W[n][l]).astype(BF16)
    landing = lambda ts: [lax.dynamic_update_index_in_dim(lax.empty((N_DEV,) + t.shape, t.dtype), t, me, 0) for t in ts]
    first0 = _gather_many([shard(n, 0) for n in FIRST], "gather_first_l0")
    later0 = [shard(n, 0) for n in LATER] + [conv_w]
    flight0, token0 = _exchange_start("gather", later0, landing(later0), "gather_later_l0_start", after=first0[0])
    all1 = [shard(n, 1) for n in BIG]
    flight1, token1 = _exchange_start("gather", all1, landing(all1), "gather_weights_l1_start", after=token0)
    conv_w_all = []

    def later_first_layer(after):
        _, got = _exchange_wait(flight0, after, "gather_later_l0_wait")
        conv_w_all.append(got[-1].transpose(1, 2, 0, 3).reshape(DEPTH, 3, 2 * DFF))
        return _later_weights(dict(zip(LATER, got)), 0, conv_w_all[0], conv_b)

    def second_layer(after):
        _, got = _exchange_wait(flight1, after, "gather_weights_l1_wait")
        return _layer_weights(dict(zip(BIG, got)), 1, conv_w_all[0], conv_b, W)

    core = lax.axis_index("c").astype(jnp.int32).reshape(1)
    chip = (2 * lax.axis_index("x") + lax.axis_index("y")).astype(jnp.int32).reshape(1)
    empty_like = lambda ts, slots: [lax.empty((slots,) + t.shape[1:], t.dtype) for t in ts]
    chip32, far = [{} for _ in range(DEPTH)], [{} for _ in range(DEPTH)]
    pending = []

    def reduction(l, names, tag):
        state = {}

        def start(g):
            mine = [g[n].reshape((N_DEV, -1) + g[n].shape[1:]) for n in names]
            state["cores"], token = _exchange_start("cores", mine, empty_like(mine, 4), f"exchange_cores_{tag}_start")
            return token

        def onward(after):
            mine, theirs = _exchange_wait(state["cores"], after, f"exchange_cores_{tag}_wait")
            sums = [_pair_sum(a, b, core, f"pair_sum_l{l}_{n}") for a, b, n in zip(mine, theirs, names)]
            for n, s in zip(names, sums):
                chip32[l][n] = s[0]
            sent = [s[0 if n in F32_OVER_ICI else 1] for s, n in zip(sums, names)]
            flight, token = _exchange_start("chips", sent, empty_like(sent, 3), f"exchange_chips_{tag}_start")
            pending.append((l, names, flight, tag))
            return token

        return start, onward

    def on_grads(l, g):
        if l == 0:
            return {}
        start, onward = reduction(l, BIG, f"l{l}")
        return dict(token=start(g), after_ffn=onward)

    def own_hooks(l):
        if l != 0:
            return {}
        start_e, onward_e = reduction(0, LATER, "l0_later")
        start_w, onward_w = reduction(0, FIRST, "l0_first")
        return dict(early_grads=start_e, after_ret=onward_e, w_in_ready=lambda gw: start_w({"w_in": gw}), after_dx0=onward_w)

    ws = [(_first_weights(dict(zip(FIRST, first0)), 0, W), later_first_layer), second_layer]
    loss_vec, grad_x, grads = _local_step(x[0] + token1[0, 0], p[:, 0], positions[0], loss_target[0], ws, own_hooks, on_grads)
    loss = lax.psum(jnp.sum(loss_vec), ("x", "y", "c"))
    for l, names, flight, tag in pending:
        _, got = _exchange_wait(flight, grad_x, f"exchange_chips_{tag}_wait")
        far[l].update(zip(names, got))
    G, DW, NM, NV = ({} for _ in range(4))
    for n in BIG:
        chip32_n = [chip32[l][n] for l in range(DEPTH)]
        far_n = [far[l][n] for l in range(DEPTH)]
        if n in COL_SHARDED:
            G[n] = _reduce_tail(chip32_n, far_n, chip, f"reduced_{n}")[0].transpose(0, 2, 1)
            R2, C2 = DEPTH * W[n].shape[1], W[n].shape[2]
            res = _adamw(*(t.reshape(R2, C2) for t in (G[n], W[n], M[n], V[n])), f"adamw_{n}")
            DW[n], NM[n], NV[n] = (t.reshape(W[n].shape) for t in res)
        else:
            G[n], DW[n], NM[n], NV[n] = _reduce_tail(chip32_n, far_n, chip, f"adamw_{n}", wmv=(W[n], M[n], V[n]))

    small_names = SMALL + ("conv_w",)
    g_small, where = _pack_rows([jnp.stack([grads[l][n] for l in range(DEPTH)]) for n in small_names])
    (g_all,) = _gather_many([g_small], "gather_small_grads")
    g_small = _sum_slots(g_all, "sum_small_grads")
    for n, (off, rows) in zip(SMALL, where):
        G[n] = g_small[off:off + rows].reshape(W[n].shape)
    off, rows = where[-1]
    g_cw = g_small[off:off + rows].reshape(DEPTH, 3, N_DEV, conv_w.shape[2])
    G["conv_w"] = lax.dynamic_index_in_dim(g_cw, me, axis=2, keepdims=False)
    packed = [_pack_rows([d[n] for n in SMALL]) for d in (G, W, M, V)]
    small_out = _adamw(*(t for t, _ in packed), "adamw_small")
    for res, dst in zip(small_out, (DW, NM, NV)):
        for n, (off, rows) in zip(SMALL, packed[0][1]):
            dst[n] = res[off:off + rows].reshape(W[n].shape)
    two_d = lambda t: t.reshape(DEPTH * 3, conv_w.shape[2])
    cw_out = _adamw(two_d(G["conv_w"]), two_d(conv_w), two_d(m_conv_w), two_d(v_conv_w), "adamw_conv_w")
    for res, dst in zip(cw_out, (DW, NM, NV)):
        dst["conv_w"] = res.reshape(conv_w.shape)

    return (loss, grad_x[None], *[G[n] for n in WEIGHTS], *[DW[n] for n in WEIGHTS], *[NM[n] for n in WEIGHTS],
            *[NV[n] for n in WEIGHTS])
```

```python
import math

import numpy as np
import jax
import jax.numpy as jnp
from jax import lax
from jax.experimental import pallas as pl
from jax.experimental.pallas import tpu as pltpu

F32, BF16 = jnp.float32, jnp.bfloat16

D = 1024
DEPTH = 2
N_DEV = 8
HD = 128
NH = 8
DILATIONS = (1, 4, 16)
SPAN = 128
N_ATT = 3 * 3 * NH * HD
RH, RDK, RDV = 4, 256, 512
CH = 128
DFF = 2816
PLE = 256
N_IN = 17408
N_REST = N_IN - N_ATT
OFF_RQ, OFF_RK, OFF_RV, OFF_RG, OFF_GA, OFF_GR = 0, 1024, 2048, 4096, 6144, 7168
ALPHA = (2 * DEPTH) ** 0.25
LN_EPS, GN_EPS = 1e-5, 1e-6
ROPE_BASE = 10000.0
LR, B1, B2, EPS, WD, STEP = 0.001, 0.9, 0.999, 1e-8, 0.01, 10
VMEM_LIMIT = 48 * 1024 * 1024
NEG = -1e30

BIG = ("w_in", "w_attn_proj", "w_ret_proj", "w_out", "w_up", "w_down", "w_ple_gate", "w_ple_proj")
COL_SHARDED = ("w_in", "w_up", "w_ple_proj")
F32_OVER_ICI = ("w_attn_proj", "w_out", "w_ple_gate", "w_ple_proj")
SMALL = ("ret_gn_g", "ret_gn_b", "ln1_g", "ln1_b", "conv_b", "ln2_g", "ln2_b")
WEIGHTS = ("w_in", "w_attn_proj", "w_ret_proj", "ret_gn_g", "ret_gn_b", "w_out", "ln1_g", "ln1_b", "w_up",
           "conv_w", "conv_b", "w_down", "w_ple_gate", "w_ple_proj", "ln2_g", "ln2_b")


def _tile(n, cap, mult=128):
    if n <= cap:
        return n
    t = (cap // mult) * mult
    while n % t:
        t -= mult
    return t


def _cparams(sem):
    return pltpu.CompilerParams(dimension_semantics=sem, vmem_limit_bytes=VMEM_LIMIT)


def _dot(a, b, ca, cb):
    return lax.dot_general(a, b, (((ca,), (cb,)), ((), ())), preferred_element_type=F32)


def _bdot(a, b, ca, cb):
    return lax.dot_general(a, b, (((ca,), (cb,)), ((0,), (0,))), preferred_element_type=F32)


def _mm(a, b, *, name, ta=False, tb=False, out_dtype=F32, add=None, add_scale=1.0, tm=1024, tn=1024, tk=1024,
        b_rows=None, out_rows=None, into=None, blocks8=False, after=None):
    M, K = (a.shape[1], a.shape[0]) if ta else a.shape
    b_first, b_count = b_rows if b_rows else (0, b.shape[0])
    N = b_count if tb else b.shape[1]
    assert K == (b.shape[1] if tb else b_count)
    tm, tn, tk = _tile(M, tm), _tile(N, tn), _tile(K, tk)
    nk = K // tk
    o_first, o_total = out_rows if out_rows else (0, M)
    jb, kb, io = (b_first // tn, 0, o_first // tm) if tb else (0, b_first // tk, o_first // tm)
    assert b_first % (tn if tb else tk) == 0 and o_first % tm == 0 and (add is None or out_rows is None)

    def body(*refs):
        if add is None:
            a_ref, b_ref = refs[:2]
        else:
            a_ref, b_ref, add_ref = refs[:3]
        o_ref, acc_ref = refs[-2:]
        k = pl.program_id(2)

        @pl.when(k == 0)
        def _():
            acc_ref[...] = jnp.zeros_like(acc_ref)

        acc_ref[...] += _dot(a_ref[...].astype(BF16), b_ref[...].astype(BF16), 0 if ta else 1, 1 if tb else 0)

        @pl.when(k == nk - 1)
        def _():
            r = acc_ref[...]
            if add is not None:
                r = r + add_scale * add_ref[...].astype(F32)
            o_ref[...] = r.astype(out_dtype).reshape(o_ref.shape)

    a_spec = pl.BlockSpec((tk, tm), lambda i, j, k: (k, i)) if ta else pl.BlockSpec((tm, tk), lambda i, j, k: (i, k))
    if tb:
        b_spec = pl.BlockSpec((tn, tk), lambda i, j, k: (j + jb, k))
    else:
        b_spec = pl.BlockSpec((tk, tn), lambda i, j, k: (k + kb, j))
    if blocks8:
        assert tm == 1024
        o_spec = pl.BlockSpec((1, 8, 128, tn), lambda i, j, k: (i + io, 0, 0, j))
        o_shape = (o_total // tm, 8, 128, N)
    else:
        o_spec = pl.BlockSpec((tm, tn), lambda i, j, k: (i + io, j))
        o_shape = (o_total, N)
    in_specs, args, aliases = [a_spec, b_spec], [a, b], {}
    if add is not None:
        in_specs.append(o_spec)
        args.append(add)
    if after is not None:
        in_specs.append(pl.BlockSpec(memory_space=pl.ANY))
        args.append(after)
    if into is not None:
        aliases = {len(args): 0}
        in_specs.append(pl.BlockSpec(memory_space=pl.ANY))
        args.append(into)
    return pl.pallas_call(
        body, name=name, grid=(M // tm, N // tn, nk), in_specs=in_specs, out_specs=o_spec,
        out_shape=jax.ShapeDtypeStruct(o_shape, out_dtype), scratch_shapes=[pltpu.VMEM((tm, tn), F32)],
        input_output_aliases=aliases, compiler_params=_cparams(("parallel", "parallel", "arbitrary")),
    )(*args)


def _rowwise(fn, rows, pars, outs, accs, *, name, tm=512):
    first = rows[0][0] if isinstance(rows[0], tuple) else rows[0]
    S = first.shape[-2]
    tm = _tile(S, tm, 16)
    n_r, n_p, n_o = len(rows), len(pars), len(outs)
    views_out = {k: o[3] for k, o in enumerate(outs) if o[0] == "dilated"}
    outs = [(o[1], o[2], o[1], 0, None) if o[0] == "dilated" else o if len(o) == 5 else (o[0], o[1], o[0], 0, None) for o in outs]
    intos = [(k, o[4]) for k, o in enumerate(outs) if o[4] is not None]
    n_i = len(intos)
    dilated = {k: r[2] for k, r in enumerate(rows) if isinstance(r, tuple) and r[0] == "dilated"}
    n_s = len(dilated) + len(views_out)

    def body(*refs):
        i = pl.program_id(0)
        scratch = refs[len(refs) - n_s:]
        vals = [r[...] for r in refs[:n_r + n_p]]
        for scr, (k, d) in zip(scratch, dilated.items()):
            w = vals[k].shape[1] // d
            for r in range(d):
                for c in range(w // 128):
                    scr.at[c][pl.ds(r, tm // d, stride=d), :] = vals[k][:, r * w + c * 128:r * w + (c + 1) * 128].astype(F32)
            vals[k] = jnp.concatenate([scr[c] for c in range(w // 128)], axis=1)
        res = fn(*vals)
        if not isinstance(res, (tuple, list)):
            res = (res,)
        res = list(res)
        o_refs = refs[n_r + n_p + n_i:n_r + n_p + n_i + n_o]
        a_refs = refs[n_r + n_p + n_i + n_o:len(refs) - n_s]
        for scr, (k, d) in zip(scratch[len(dilated):], views_out.items()):
            v = res[k].astype(F32)
            nc = v.shape[1] // 128
            for c in range(nc):
                scr[c] = v[:, c * 128:(c + 1) * 128]
            res[k] = jnp.concatenate([scr.at[c][pl.ds(r, tm // d, stride=d), :] for r in range(d) for c in range(nc)], axis=1)
        for r, v in zip(o_refs, res[:n_o]):
            r[...] = v.astype(r.dtype)
        if a_refs:
            @pl.when(i == 0)
            def _():
                for r in a_refs:
                    r[...] = jnp.zeros_like(r)

            for r, v in zip(a_refs, res[n_o:]):
                r[...] += v

    in_specs, args = [], []
    for r in rows:
        if isinstance(r, tuple) and r[0] == "dilated":
            _, arr, d = r
            in_specs.append(pl.BlockSpec((tm // d, arr.shape[1]), lambda i: (i, 0)))
        elif isinstance(r, tuple):
            arr, w, cb = r
            in_specs.append(pl.BlockSpec((tm, w), lambda i, cb=cb: (i, cb)))
        elif r.ndim == 3:
            arr = r
            in_specs.append(pl.BlockSpec((arr.shape[0], tm, arr.shape[2]), lambda i: (0, i, 0)))
        else:
            arr = r
            in_specs.append(pl.BlockSpec((tm, arr.shape[1]), lambda i: (i, 0)))
        args.append(arr)
    for p_ in pars:
        in_specs.append(pl.BlockSpec(p_.shape, lambda i: (0, 0)))
        args.append(p_)
    aliases = {}
    for k, arr in intos:
        aliases[len(args)] = k
        in_specs.append(pl.BlockSpec(memory_space=pl.ANY))
        args.append(arr)
    out_shape = [jax.ShapeDtypeStruct((S // views_out.get(k, 1), views_out.get(k, 1) * o[2]), o[1]) for k, o in enumerate(outs)]
    out_shape += [jax.ShapeDtypeStruct(a, F32) for a in accs]
    out_specs = [pl.BlockSpec((tm // views_out.get(k, 1), views_out.get(k, 1) * o[0]), lambda i, cb=o[3]: (i, cb))
                 for k, o in enumerate(outs)] + [pl.BlockSpec(a, lambda i: (0, 0)) for a in accs]
    scratch = [pltpu.VMEM((rows[k][1].shape[1] // d // 128, tm, 128), F32) for k, d in dilated.items()]
    scratch += [pltpu.VMEM((outs[k][0] // 128, tm, 128), F32) for k in views_out]
    return pl.pallas_call(
        body, name=name, grid=(S // tm,), in_specs=in_specs, out_specs=out_specs, out_shape=out_shape,
        scratch_shapes=scratch, input_output_aliases=aliases,
        compiler_params=_cparams(("arbitrary",) if accs else ("parallel",)),
    )(*args)


def _norm(h, eps):
    mu = jnp.mean(h, -1, keepdims=True)
    d = h - mu
    rstd = lax.rsqrt(jnp.mean(d * d, -1, keepdims=True) + eps)
    return d * rstd, rstd


def _norm_bwd(dxh, xh, rstd):
    return rstd * (dxh - jnp.mean(dxh, -1, keepdims=True) - xh * jnp.mean(dxh * xh, -1, keepdims=True))


def _sig(x):
    return 1.0 / (1.0 + jnp.exp(-x))


_GELU_C = math.sqrt(2.0 / math.pi)


def _gelu(x, with_grad=False):
    x2 = x * x
    t = jnp.tanh(x * (_GELU_C + (_GELU_C * 0.044715) * x2))
    half_x, one_t = 0.5 * x, 1.0 + t
    if not with_grad:
        return half_x * one_t
    return half_x * one_t, 0.5 * one_t + half_x * (1.0 - t * t) * (_GELU_C + (3 * _GELU_C * 0.044715) * x2)


def _f_ln1(x, mix, g, b):
    h = ALPHA * x + mix
    xh, _ = _norm(h, LN_EPS)
    y = xh * g + b
    return h, y, y


def _f_ln2(x, ffn, z, pp, g, b):
    h = ALPHA * x + ffn + _sig(z) * pp
    xh, _ = _norm(h, LN_EPS)
    return h, xh * g + b


def _f_ln_bwd(*args):
    *dys, h, g = args
    dy = dys[0]
    for t in dys[1:]:
        dy = dy + t
    xh, rstd = _norm(h, LN_EPS)
    dh = _norm_bwd(dy * g, xh, rstd)
    return dh, dh, jnp.sum(dy * xh, 0, keepdims=True), jnp.sum(dy, 0, keepdims=True)


def _f_sum(*ts):
    r = ts[0]
    for t in ts[1:]:
        r = r + t
    return r


def _f_loss(y, t):
    e = y - t
    return e * (1.0 / D), jnp.sum(e * e, 0, keepdims=True) * (0.5 / D)


def _head_col(c, h):
    lane = lax.broadcasted_iota(jnp.int32, c.shape, 1)
    return jnp.sum(jnp.where(lane == h, c, 0.0), -1, keepdims=True)


def _f_combine(o0, o1, o2, l0, l1, l2):
    m = jnp.maximum(jnp.maximum(l0, l1), l2)
    e0, e1, e2 = jnp.exp(l0 - m), jnp.exp(l1 - m), jnp.exp(l2 - m)
    den = e0 + e1 + e2
    inv = 1.0 / den
    w0, w1, w2 = e0 * inv, e1 * inv, e2 * inv
    parts = [_head_col(w0, h) * o0[h].astype(F32) + _head_col(w1, h) * o1[h].astype(F32) + _head_col(w2, h) * o2[h].astype(F32)
             for h in range(NH)]
    return jnp.concatenate(parts, axis=1), m + jnp.log(den)


def _f_delta(da, a):
    lane = lax.broadcasted_iota(jnp.int32, (da.shape[0], HD), 1)
    out = jnp.zeros((da.shape[0], HD), F32)
    for h in range(NH):
        sl = slice(h * HD, (h + 1) * HD)
        s = jnp.sum(da[:, sl].astype(F32) * a[:, sl].astype(F32), -1, keepdims=True)
        out = jnp.where(lane == h, s, out)
    return out


def _f_gate(ap, rp, ga, gr):
    return _sig(ga.astype(F32)) * ap.astype(F32) + _sig(gr.astype(F32)) * rp.astype(F32)


def _f_gate_bwd(dm, ap, rp, ga, gr):
    dm = dm.astype(F32)
    sa, sr = _sig(ga.astype(F32)), _sig(gr.astype(F32))
    dga, dgr = dm * ap.astype(F32) * sa * (1.0 - sa), dm * rp.astype(F32) * sr * (1.0 - sr)
    return dm * sa, dm * sr, jnp.concatenate([dga, dgr], axis=1)


def _f_gn(y, rg, g, b):
    y, rg = y.astype(F32), rg.astype(F32)
    parts = []
    for h in range(RH):
        sl = slice(h * RDV, (h + 1) * RDV)
        xh, _ = _norm(y[:, sl], GN_EPS)
        parts.append(xh * g[:, sl] + b[:, sl])
    return rg * _sig(rg) * jnp.concatenate(parts, axis=1)


def _f_gn_bwd(dr, y, rg, g, b):
    dr, y, rg = dr.astype(F32), y.astype(F32), rg.astype(F32)
    s = _sig(rg)
    d_out = dr * rg * s
    dys, outs, xhs = [], [], []
    for h in range(RH):
        sl = slice(h * RDV, (h + 1) * RDV)
        xh, rstd = _norm(y[:, sl], GN_EPS)
        xhs.append(xh)
        outs.append(xh * g[:, sl] + b[:, sl])
        dys.append(_norm_bwd(d_out[:, sl] * g[:, sl], xh, rstd))
    xh, out = jnp.concatenate(xhs, axis=1), jnp.concatenate(outs, axis=1)
    d_rg = dr * out * s * (1.0 + rg * (1.0 - s))
    return jnp.concatenate(dys, axis=1), d_rg, jnp.sum(d_out * xh, 0, keepdims=True), jnp.sum(d_out, 0, keepdims=True)


def _f_ln2_bwd(*args):
    *dys, h, z, pp, g = args
    dh, dhb, dg, db = _f_ln_bwd(*dys, h, g)
    s = _sig(z)
    return dh, dhb, dh * s, dh * pp * s * (1.0 - s), dg, db


QKV = 3 * HD


def _to_tokens(t, d):
    if d == 1:
        return t
    *lead, S, C = t.shape
    n = len(lead)
    perm = tuple(range(n)) + (n + 1, n, n + 2)
    return t.reshape(*lead, d, S // d, C).transpose(perm).reshape(*lead, S, C)


def _to_residues(t, d):
    if d == 1:
        return t
    S, C = t.shape
    return t.reshape(S // d, d, C).transpose(1, 0, 2).reshape(S, C)


def _to_head_residues(t, d):
    S = t.shape[0]
    return t.reshape(S // d, d, NH, HD).transpose(2, 1, 0, 3).reshape(NH, S, HD)


def _w_qkv_specs(g):
    return [pl.BlockSpec((D, D), lambda *i, t=t: (3 * g + t, 0)) for t in range(3)]


def _qkv_fwd(xv, win, g, dil, name):
    Sd = xv.shape[0]
    S = Sd * dil
    tm = min(512, Sd)
    nma = Sd // tm

    def body(a_ref, wq_ref, wk_ref, wv_ref, o_ref):
        a = a_ref[...]
        q, k, v = (_dot(a, w_ref[...], 1, 1).astype(BF16) for w_ref in (wq_ref, wk_ref, wv_ref))
        for h in range(NH):
            sl = slice(h * HD, (h + 1) * HD)
            o_ref[h] = jnp.concatenate([q[:, sl], k[:, sl], v[:, sl]], axis=1)

    return pl.pallas_call(
        body, name=name, grid=(S // tm,),
        in_specs=[pl.BlockSpec((tm, D), lambda i: (i % nma, i // nma))] + _w_qkv_specs(g),
        out_specs=pl.BlockSpec((NH, tm, QKV), lambda i: (0, i, 0)), out_shape=jax.ShapeDtypeStruct((NH, S, QKV), BF16),
        compiler_params=_cparams(("parallel",)),
    )(xv, win, win, win)


def _qkv_dx(dqkv, win, g, dil, name, out_dtype, add=None, after=None):
    S = dqkv.shape[1]
    Sd = S // dil
    tm = min(512, Sd)
    nmo = Sd // tm

    def body(*refs):
        a_ref, wq_ref, wk_ref, wv_ref = refs[:4]
        o_ref = refs[-1]
        acc = None
        for t, w_ref in enumerate((wq_ref, wk_ref, wv_ref)):
            d = jnp.concatenate([a_ref[h][:, t * HD:(t + 1) * HD] for h in range(NH)], axis=1)
            part = _dot(d, w_ref[...], 1, 0)
            acc = part if acc is None else acc + part
        if add is not None:
            acc = acc + refs[4][...]
        o_ref[...] = acc.astype(out_dtype)

    o_spec = pl.BlockSpec((tm, D), lambda i: (i % nmo, i // nmo))
    in_specs = [pl.BlockSpec((NH, tm, QKV), lambda i: (0, i, 0))] + _w_qkv_specs(g)
    args = [dqkv, win, win, win]
    if add is not None:
        assert dil == 1
        in_specs.append(o_spec)
        args.append(add)
    if after is not None:
        in_specs.append(pl.BlockSpec(memory_space=pl.ANY))
        args.append(after)
    return pl.pallas_call(
        body, name=name, grid=(S // tm,), in_specs=in_specs, out_specs=o_spec,
        out_shape=jax.ShapeDtypeStruct((Sd, dil * D), out_dtype), compiler_params=_cparams(("parallel",)),
    )(*args)


GW_IN_BLOCKS = (N_IN // D, NH, HD, D)


def _qkv_dw(dqkv, xv, g, dil, name, into=None):
    S = dqkv.shape[1]
    Sd = S // dil
    tk = min(1024, Sd)
    nkb, nk = Sd // tk, S // tk
    hh = NH // 2

    def body(*refs):
        a_ref, b_ref = refs[:2]
        o_ref, acc_ref = refs[-2:]
        k = pl.program_id(1)

        @pl.when(k == 0)
        def _():
            acc_ref[...] = jnp.zeros_like(acc_ref)

        b = b_ref[...]
        for h in range(hh):
            acc_ref[h * QKV:(h + 1) * QKV, :] += _dot(a_ref[h], b, 0, 0)

        @pl.when(k == nk - 1)
        def _():
            for h in range(hh):
                for t in range(3):
                    o_ref[t, h] = acc_ref[h * QKV + t * HD:h * QKV + (t + 1) * HD, :]

    in_specs = [pl.BlockSpec((hh, tk, QKV), lambda j, k: (j, k, 0)), pl.BlockSpec((tk, D), lambda j, k: (k % nkb, k // nkb))]
    args, aliases = [dqkv, xv], {}
    if into is not None:
        aliases = {2: 0}
        in_specs.append(pl.BlockSpec(memory_space=pl.ANY))
        args.append(into)
    return pl.pallas_call(
        body, name=name, grid=(2, nk), in_specs=in_specs,
        out_specs=pl.BlockSpec((3, hh, HD, D), lambda j, k: (g, j, 0, 0)), out_shape=jax.ShapeDtypeStruct(GW_IN_BLOCKS, F32),
        input_output_aliases=aliases, scratch_shapes=[pltpu.VMEM((hh * QKV, D), F32)],
        compiler_params=_cparams(("parallel", "arbitrary")),
    )(*args)


def _band(nb, first_valid, last_valid=None):
    b = lax.broadcasted_iota(jnp.int32, (nb, SPAN, SPAN), 0)
    row = lax.broadcasted_iota(jnp.int32, (nb, SPAN, SPAN), 1)
    col = lax.broadcasted_iota(jnp.int32, (nb, SPAN, SPAN), 2)
    off = jnp.where(b == 0, jnp.where(first_valid, 0, 2 * SPAN), 0)
    if last_valid is not None:
        off = off + jnp.where(b == nb - 1, jnp.where(last_valid, 0, 2 * SPAN), 0)
    return col <= row, col >= row + off


def _attn_tiles(S, dil):
    Sd = S // dil
    T = min(1024, Sd)
    hp = min(NH, max(1, (S // T) * NH // 16))
    return Sd, T, T // SPAN, Sd // T, hp


def _attn_fwd(qkv, dil, name):
    S = qkv.shape[1]
    Sd, T, nsub, nib, hp = _attn_tiles(S, dil)
    scale = HD ** -0.5

    def body(c_ref, p_ref, o_ref, l_ref):
        ib, hb = pl.program_id(1), pl.program_id(2)
        m_cur, m_prev = _band(nsub, ib > 0)
        lane = lax.broadcasted_iota(jnp.int32, (T, HD), 1)

        @pl.when(hb == 0)
        def _():
            l_ref[...] = jnp.zeros_like(l_ref)

        lses = l_ref[...]
        for hh in range(hp):
            blk, hal = c_ref[hh], p_ref[hh]
            q, k, v = blk[:, :HD], blk[:, HD:2 * HD], blk[:, 2 * HD:]
            if nsub > 1:
                kp = jnp.concatenate([hal[:, HD:2 * HD], k[:T - SPAN]], axis=0)
                vp = jnp.concatenate([hal[:, 2 * HD:], v[:T - SPAN]], axis=0)
            else:
                kp, vp = hal[:, HD:2 * HD], hal[:, 2 * HD:]
            q3, k3, v3, kp3, vp3 = (t.reshape(nsub, SPAN, HD) for t in (q, k, v, kp, vp))
            sc = jnp.where(m_cur, _bdot(q3, k3, 2, 2) * scale, NEG)
            sp = jnp.where(m_prev, _bdot(q3, kp3, 2, 2) * scale, NEG)
            m = jnp.maximum(jnp.max(sc, -1, keepdims=True), jnp.max(sp, -1, keepdims=True))
            pc, pp = jnp.exp(sc - m), jnp.exp(sp - m)
            den = jnp.sum(pc, -1, keepdims=True) + jnp.sum(pp, -1, keepdims=True)
            o = (_bdot(pc.astype(BF16), v3, 2, 1) + _bdot(pp.astype(BF16), vp3, 2, 1)) * (1.0 / den)
            o_ref[hh] = o.reshape(T, HD).astype(BF16)
            lses = jnp.where(lane == hb * hp + hh, (m + jnp.log(den)).reshape(T, 1), lses)
        l_ref[...] = lses

    cur = pl.BlockSpec((hp, T, QKV), lambda r, ib, h: (h, r * nib + ib, 0))
    prev = pl.BlockSpec((hp, SPAN, QKV), lambda r, ib, h: (h, r * (Sd // SPAN) + jnp.maximum(ib * nsub - 1, 0), 0))
    return pl.pallas_call(
        body, name=name, grid=(dil, nib, NH // hp), in_specs=[cur, prev],
        out_specs=[pl.BlockSpec((hp, T, HD), lambda r, ib, h: (h, r * nib + ib, 0)),
                   pl.BlockSpec((T, HD), lambda r, ib, h: (r * nib + ib, 0))],
        out_shape=[jax.ShapeDtypeStruct((NH, S, HD), BF16), jax.ShapeDtypeStruct((S, HD), F32)],
        compiler_params=_cparams(("parallel", "parallel", "arbitrary")),
    )(qkv, qkv)


def _attn_bwd(qkv, d_attn, lse, delta, dil, name):
    S = qkv.shape[1]
    Sd, T, nsub, nib, hp = _attn_tiles(S, dil)
    scale = HD ** -0.5
    ne = nsub + 1

    def body(c_ref, p_ref, n_ref, do_ref, don_ref, l_ref, ln_ref, dl_ref, dln_ref, o_ref):
        ib, hb = pl.program_id(1), pl.program_id(2)
        _, m_prev = _band(ne, ib > 0, ib < nib - 1)
        m_cur, _ = _band(nsub, True)
        for hh in range(hp):
            h = hb * hp + hh
            blk, hal, nxt = c_ref[hh], p_ref[hh], n_ref[hh]
            q, k, v = blk[:, :HD], blk[:, HD:2 * HD], blk[:, 2 * HD:]
            do = do_ref[hh]
            l, dl = _head_col(l_ref[...], h), _head_col(dl_ref[...], h)
            qe = jnp.concatenate([q, nxt[:, :HD]], axis=0).reshape(ne, SPAN, HD)
            doe = jnp.concatenate([do, don_ref[hh]], axis=0).reshape(ne, SPAN, HD)
            le = jnp.concatenate([l, _head_col(ln_ref[...], h)], axis=0).reshape(ne, SPAN, 1)
            dle = jnp.concatenate([dl, _head_col(dln_ref[...], h)], axis=0).reshape(ne, SPAN, 1)
            kpe = jnp.concatenate([hal[:, HD:2 * HD], k], axis=0).reshape(ne, SPAN, HD)
            vpe = jnp.concatenate([hal[:, 2 * HD:], v], axis=0).reshape(ne, SPAN, HD)
            p = jnp.where(m_prev, jnp.exp(_bdot(qe, kpe, 2, 2) * scale - le), 0.0)
            ds = (p * (_bdot(doe, vpe, 2, 2) - dle)).astype(BF16)
            dq = _bdot(ds, kpe, 2, 1)[:nsub]
            dk = _bdot(ds, qe, 1, 1)[1:]
            dv = _bdot(p.astype(BF16), doe, 1, 1)[1:]
            q3, k3, v3, do3 = (t.reshape(nsub, SPAN, HD) for t in (q, k, v, do))
            l3, dl3 = l.reshape(nsub, SPAN, 1), dl.reshape(nsub, SPAN, 1)
            p = jnp.where(m_cur, jnp.exp(_bdot(q3, k3, 2, 2) * scale - l3), 0.0)
            ds = (p * (_bdot(do3, v3, 2, 2) - dl3)).astype(BF16)
            dq = (dq + _bdot(ds, k3, 2, 1)) * scale
            dk = (dk + _bdot(ds, q3, 1, 1)) * scale
            dv = dv + _bdot(p.astype(BF16), do3, 1, 1)
            o_ref[hh] = jnp.concatenate([t.reshape(T, HD) for t in (dq, dk, dv)], axis=1).astype(BF16)

    nb = Sd // SPAN
    row = lambda r, ib: r * nib + ib
    prow = lambda r, ib: r * nb + jnp.maximum(ib * nsub - 1, 0)
    nrow = lambda r, ib: r * nb + jnp.minimum((ib + 1) * nsub, nb - 1)
    cur3 = pl.BlockSpec((hp, T, QKV), lambda r, ib, h: (h, row(r, ib), 0))
    prev3 = pl.BlockSpec((hp, SPAN, QKV), lambda r, ib, h: (h, prow(r, ib), 0))
    next3 = pl.BlockSpec((hp, SPAN, QKV), lambda r, ib, h: (h, nrow(r, ib), 0))
    cur1 = pl.BlockSpec((hp, T, HD), lambda r, ib, h: (h, row(r, ib), 0))
    next1 = pl.BlockSpec((hp, SPAN, HD), lambda r, ib, h: (h, nrow(r, ib), 0))
    curc = pl.BlockSpec((T, HD), lambda r, ib, h: (row(r, ib), 0))
    nextc = pl.BlockSpec((SPAN, HD), lambda r, ib, h: (nrow(r, ib), 0))
    return pl.pallas_call(
        body, name=name, grid=(dil, nib, NH // hp),
        in_specs=[cur3, prev3, next3, cur1, next1, curc, nextc, curc, nextc], out_specs=cur3,
        out_shape=jax.ShapeDtypeStruct((NH, S, QKV), BF16),
        compiler_params=_cparams(("parallel", "parallel", "parallel")),
    )(qkv, qkv, qkv, d_attn, d_attn, lse, lse, delta, delta)


def _ret_consts():
    lg = np.log1p(-np.exp2(-5.0 - np.arange(RH, dtype=np.float64)))
    idx = np.arange(CH, dtype=np.float64)
    rel = idx[:, None] - idx[None, :]
    intra = np.where(rel >= 0, np.exp(lg[:, None, None] * np.maximum(rel, 0.0)), 0.0)
    qd = np.exp(lg[:, None] * (idx + 1.0))
    kd = np.exp(lg[:, None] * (CH - 1.0 - idx))
    cd = np.exp(lg * CH)
    wide = lambda t: np.broadcast_to(t[:, :, None], (RH, t.shape[1], RDV))
    return (jnp.asarray(intra, F32), jnp.asarray(wide(qd), F32), jnp.asarray(wide(kd), F32),
            jnp.asarray(np.broadcast_to(cd[:, None, None], (RH, 1, RDV)), F32))


def _rot(t, c, s):
    t1, t2 = t[:, :RDK // 2], t[:, RDK // 2:]
    return jnp.concatenate([t1 * c - t2 * s, t1 * s + t2 * c], axis=1)


def _unrot(d, c, s):
    d1, d2 = d[:, :RDK // 2], d[:, RDK // 2:]
    return jnp.concatenate([d1 * c + d2 * s, d2 * c - d1 * s], axis=1)


RCH = 4


def _ret_specs(nmap):
    rows = RCH * CH
    q = pl.BlockSpec((rows, RH * RDK), lambda n: (nmap(n), OFF_RQ // (RH * RDK)))
    k = pl.BlockSpec((rows, RH * RDK), lambda n: (nmap(n), OFF_RK // (RH * RDK)))
    v = pl.BlockSpec((rows, RH * RDV), lambda n: (nmap(n), OFF_RV // (RH * RDV)))
    cs = pl.BlockSpec((rows, RDK // 2), lambda n: (nmap(n), 0))
    dmat = pl.BlockSpec((RH, CH, CH), lambda n: (0, 0, 0))
    dvec = pl.BlockSpec((RH, CH, RDV), lambda n: (0, 0, 0))
    cdv = pl.BlockSpec((RH, 1, RDV), lambda n: (0, 0, 0))
    state = pl.BlockSpec((RH, RCH, RDK, RDV), lambda n: (0, nmap(n), 0, 0))
    out = pl.BlockSpec((rows, RH * RDV), lambda n: (nmap(n), 0))
    return [q, k, v, cs, cs, dmat, dvec, dvec, cdv], state, out


def _ret_fwd(proj, cos, sin, consts):
    S = proj.shape[0]
    nc = S // CH

    def body(q_ref, k_ref, v_ref, c_ref, s_ref, d_ref, qd_ref, kd_ref, cd_ref, o_ref, st_ref, state):
        @pl.when(pl.program_id(0) == 0)
        def _():
            state[...] = jnp.zeros_like(state)

        for ci in range(RCH):
            rows = slice(ci * CH, (ci + 1) * CH)
            c, s = c_ref[rows, :], s_ref[rows, :]
            for h in range(RH):
                qk, vv = slice(h * RDK, (h + 1) * RDK), slice(h * RDV, (h + 1) * RDV)
                qb = _rot(q_ref[rows, qk].astype(F32), c, s).astype(BF16)
                kb = (_rot(k_ref[rows, qk].astype(F32), c, s) * (RDK ** -0.5)).astype(BF16)
                vb = v_ref[rows, vv]
                sb = state[h].astype(BF16)
                st_ref[h, ci] = sb
                a = (_dot(qb, kb, 1, 1) * d_ref[h]).astype(BF16)
                o_ref[rows, vv] = (_dot(a, vb, 1, 0) + _dot(qb, sb, 1, 0) * qd_ref[h]).astype(BF16)
                vk = (vb.astype(F32) * kd_ref[h]).astype(BF16)
                state[h] = cd_ref[h] * state[h] + _dot(kb, vk, 0, 0)

    ins, state_spec, out_spec = _ret_specs(lambda n: n)
    return pl.pallas_call(
        body, name="ret_fwd", grid=(nc // RCH,), in_specs=ins, out_specs=[out_spec, state_spec],
        out_shape=[jax.ShapeDtypeStruct((S, RH * RDV), BF16), jax.ShapeDtypeStruct((RH, nc, RDK, RDV), BF16)],
        scratch_shapes=[pltpu.VMEM((RH, RDK, RDV), F32)],
        compiler_params=_cparams(("arbitrary",)),
    )(proj, proj, proj, cos, sin, *consts)


def _ret_bwd(proj, cos, sin, consts, states, d_ret, d_rest):
    S = proj.shape[0]
    nc = S // CH

    def body(q_ref, k_ref, v_ref, c_ref, s_ref, d_ref, qd_ref, kd_ref, cd_ref, st_ref, do_ref, _, o_ref, dstate):
        @pl.when(pl.program_id(0) == 0)
        def _():
            dstate[...] = jnp.zeros_like(dstate)

        for ci in reversed(range(RCH)):
            rows = slice(ci * CH, (ci + 1) * CH)
            c, s = c_ref[rows, :], s_ref[rows, :]
            for h in range(RH):
                qk, vv = slice(h * RDK, (h + 1) * RDK), slice(h * RDV, (h + 1) * RDV)
                qb = _rot(q_ref[rows, qk].astype(F32), c, s).astype(BF16)
                kb = (_rot(k_ref[rows, qk].astype(F32), c, s) * (RDK ** -0.5)).astype(BF16)
                vb, sb, do = v_ref[rows, vv], st_ref[h, ci], do_ref[rows, vv]
                dmat, qd, kd = d_ref[h], qd_ref[h], kd_ref[h]
                a = (_dot(qb, kb, 1, 1) * dmat).astype(BF16)
                doq = (do.astype(F32) * qd).astype(BF16)
                dsb = dstate[h].astype(BF16)
                vk = (vb.astype(F32) * kd).astype(BF16)
                o_ref[rows, OFF_RV + h * RDV:OFF_RV + (h + 1) * RDV] = (_dot(a, do, 0, 0) + _dot(kb, dsb, 1, 0) * kd).astype(BF16)
                da = (_dot(do, vb, 1, 1) * dmat).astype(BF16)
                dq = _dot(da, kb, 1, 0) + _dot(doq, sb, 1, 1)
                dk = (_dot(da, qb, 0, 0) + _dot(vk, dsb, 1, 1)) * (RDK ** -0.5)
                o_ref[rows, OFF_RQ + h * RDK:OFF_RQ + (h + 1) * RDK] = _unrot(dq, c, s).astype(BF16)
                o_ref[rows, OFF_RK + h * RDK:OFF_RK + (h + 1) * RDK] = _unrot(dk, c, s).astype(BF16)
                dstate[h] = cd_ref[h] * dstate[h] + _dot(qb, doq, 0, 0)

    nsteps = nc // RCH
    rev = lambda n: nsteps - 1 - n
    ins, state_spec, out_spec = _ret_specs(rev)
    return pl.pallas_call(
        body, name="ret_bwd", grid=(nsteps,), in_specs=ins + [state_spec, out_spec, pl.BlockSpec(memory_space=pl.ANY)],
        out_specs=pl.BlockSpec((RCH * CH, OFF_RG), lambda n: (rev(n), 0)),
        out_shape=jax.ShapeDtypeStruct(d_rest.shape, BF16), input_output_aliases={11: 0},
        scratch_shapes=[pltpu.VMEM((RH, RDK, RDV), F32)],
        compiler_params=_cparams(("arbitrary",)),
    )(proj, proj, proj, cos, sin, *consts, states, d_ret, d_rest)


CW = 256
HALO = 16


def _shift_down(v, halo, k):
    rolled = pltpu.roll(v, k, 0)
    hr = pltpu.roll(halo, k, 0)[0:8]
    row = lax.broadcasted_iota(jnp.int32, hr.shape, 0)
    return jnp.concatenate([jnp.where(row < k, hr, rolled[0:8]), rolled[8:]], axis=0)


def _shift_up(v, halo, k):
    T = v.shape[0]
    rolled = pltpu.roll(v, T - k, 0)
    hr = pltpu.roll(halo, 8 - k, 0)[0:8]
    row = lax.broadcasted_iota(jnp.int32, hr.shape, 0)
    return jnp.concatenate([rolled[:T - 8], jnp.where(row >= 8 - k, hr, rolled[T - 8:])], axis=0)


def _conv_taps(h_ref, hp_ref, first):
    h = h_ref[...].astype(F32)
    hp = hp_ref[...].astype(F32) * jnp.where(first, 0.0, 1.0)
    return _shift_down(h, hp, 2), _shift_down(h, hp, 1), h


def _conv_specs(S, T, cw=CW):
    nj = DFF // cw
    cur = pl.BlockSpec((T, cw), lambda j, i: (i, j))
    prev = pl.BlockSpec((HALO, cw), lambda j, i: (jnp.maximum(i * (T // HALO) - 1, 0), j))
    nxt = pl.BlockSpec((HALO, cw), lambda j, i: (jnp.minimum((i + 1) * (T // HALO), S // HALO - 1), j))
    w = pl.BlockSpec((3, cw), lambda j, i: (0, j))
    b = pl.BlockSpec((1, cw), lambda j, i: (0, j))
    return nj, cur, prev, nxt, w, b


def _conv_fwd(hg, hu, wg, wu, bg, bu):
    S = hg.shape[0]
    T = min(1024, S)
    nj, cur, prev, _, w, b = _conv_specs(S, T)

    def body(hg_ref, hu_ref, hgp_ref, hup_ref, wg_ref, wu_ref, bg_ref, bu_ref, o_ref):
        first = pl.program_id(1) == 0
        g2, g1, g0 = _conv_taps(hg_ref, hgp_ref, first)
        u2, u1, u0 = _conv_taps(hu_ref, hup_ref, first)
        cg = wg_ref[0:1, :] * g2 + wg_ref[1:2, :] * g1 + wg_ref[2:3, :] * g0 + bg_ref[...]
        cu = wu_ref[0:1, :] * u2 + wu_ref[1:2, :] * u1 + wu_ref[2:3, :] * u0 + bu_ref[...]
        o_ref[...] = (_gelu(cg) * cu).astype(BF16)

    return pl.pallas_call(
        body, name="conv_fwd", grid=(nj, S // T), in_specs=[cur, cur, prev, prev, w, w, b, b], out_specs=cur,
        out_shape=jax.ShapeDtypeStruct((S, DFF), BF16), compiler_params=_cparams(("parallel", "parallel")),
    )(hg, hu, hg, hu, wg, wu, bg, bu)


def _conv_bwd_pre(d_act, hg, hu, wg, wu, bg, bu):
    S = hg.shape[0]
    T = min(1024, S)
    nj, cur, prev, _, w, b = _conv_specs(S, T)

    def body(da_ref, hg_ref, hu_ref, hgp_ref, hup_ref, wg_ref, wu_ref, bg_ref, bu_ref,
             dcg_ref, dcu_ref, gwg_ref, gwu_ref, gbg_ref, gbu_ref):
        first = pl.program_id(1) == 0
        g2, g1, g0 = _conv_taps(hg_ref, hgp_ref, first)
        u2, u1, u0 = _conv_taps(hu_ref, hup_ref, first)
        cg = wg_ref[0:1, :] * g2 + wg_ref[1:2, :] * g1 + wg_ref[2:3, :] * g0 + bg_ref[...]
        cu = wu_ref[0:1, :] * u2 + wu_ref[1:2, :] * u1 + wu_ref[2:3, :] * u0 + bu_ref[...]
        da = da_ref[...].astype(F32)
        gl, dgl = _gelu(cg, with_grad=True)
        dcg = da * cu * dgl
        dcu = da * gl
        dcg_ref[...] = dcg.astype(BF16)
        dcu_ref[...] = dcu.astype(BF16)

        @pl.when(first)
        def _():
            for r in (gwg_ref, gwu_ref, gbg_ref, gbu_ref):
                r[...] = jnp.zeros_like(r)

        for r, d, taps in ((gwg_ref, dcg, (g2, g1, g0)), (gwu_ref, dcu, (u2, u1, u0))):
            for j in range(3):
                r[j:j + 1, :] += jnp.sum(d * taps[j], 0, keepdims=True)
        gbg_ref[...] += jnp.sum(dcg, 0, keepdims=True)
        gbu_ref[...] += jnp.sum(dcu, 0, keepdims=True)

    return pl.pallas_call(
        body, name="conv_bwd_pre", grid=(nj, S // T), in_specs=[cur, cur, cur, prev, prev, w, w, b, b],
        out_specs=[cur, cur, w, w, b, b],
        out_shape=[jax.ShapeDtypeStruct((S, DFF), BF16)] * 2 + [jax.ShapeDtypeStruct((3, DFF), F32)] * 2
        + [jax.ShapeDtypeStruct((1, DFF), F32)] * 2,
        compiler_params=_cparams(("parallel", "arbitrary")),
    )(d_act, hg, hu, hg, hu, wg, wu, bg, bu)


def _conv_bwd_in(dc, w, name):
    S = dc.shape[0]
    T = min(512, S)
    nj, cur, _, nxt, wspec, _ = _conv_specs(S, T, DFF // 2)
    nt = S // T

    def body(dc_ref, dn_ref, w_ref, o_ref):
        d = dc_ref[...].astype(F32)
        dn = dn_ref[...].astype(F32) * jnp.where(pl.program_id(1) == nt - 1, 0.0, 1.0)
        o_ref[...] = (w_ref[2:3, :] * d + w_ref[1:2, :] * _shift_up(d, dn, 1) + w_ref[0:1, :] * _shift_up(d, dn, 2)).astype(BF16)

    return pl.pallas_call(
        body, name=name, grid=(nj, nt), in_specs=[cur, nxt, wspec], out_specs=cur,
        out_shape=jax.ShapeDtypeStruct((S, DFF), BF16), compiler_params=_cparams(("parallel", "parallel")),
    )(dc, dc, w)


def _adam_math(g, w, m, v):
    m = B1 * m + (1.0 - B1) * g
    v = B2 * v + (1.0 - B2) * (g * g)
    m_hat = m / (1.0 - B1 ** STEP)
    v_hat = v / (1.0 - B2 ** STEP)
    return -LR * (m_hat / (jnp.sqrt(v_hat) + EPS) + WD * w), m, v


def _reduce_tail(chip32, far, chip, name, wmv=None):
    L = len(chip32)
    _, R, C = chip32[0].shape
    tr = _tile(R, 256, 16)
    nr = R // tr

    def body(chip_ref, *refs):
        own_refs, far_refs, rest = refs[:L], refs[L:2 * L], refs[2 * L:]
        outs = rest[3:] if wmv else rest
        for ll in range(L):
            @pl.when(pl.program_id(0) == ll)
            def _(ll=ll):
                g = own_refs[ll][...]
                for s in range(3):
                    g = g + far_refs[ll][s].astype(F32)
                outs[0][...] = g
                if wmv:
                    outs[1][...], outs[2][...], outs[3][...] = _adam_math(g, rest[0][...], rest[1][...], rest[2][...])

    def rows(ll):
        return lambda l, i: jnp.where(l == ll, i, jnp.where(l < ll, 0, nr - 1))

    blk = pl.BlockSpec((None, tr, C), lambda l, i, ch: (l, i, 0))
    in_specs = [pl.BlockSpec((None, tr, C), lambda l, i, ch, f=rows(ll): (ch[0], f(l, i), 0)) for ll in range(L)]
    in_specs += [pl.BlockSpec((3, tr, C), lambda l, i, ch, f=rows(ll): (0, f(l, i), 0)) for ll in range(L)]
    args = list(chip32) + list(far)
    n_out = 1
    if wmv:
        in_specs += [blk] * 3
        args += list(wmv)
        n_out = 4
    return pl.pallas_call(
        body, name=name,
        grid_spec=pltpu.PrefetchScalarGridSpec(num_scalar_prefetch=1, grid=(L, nr), in_specs=in_specs, out_specs=[blk] * n_out),
        out_shape=[jax.ShapeDtypeStruct((L, R, C), F32)] * n_out, compiler_params=_cparams(("arbitrary", "arbitrary")),
    )(chip, *args)


def _adamw(g, w, m, v, name):
    R, C = g.shape
    tr = _tile(R, 128, 8)

    def body(g_ref, w_ref, m_ref, v_ref, d_ref, nm_ref, nv_ref):
        d_ref[...], nm_ref[...], nv_ref[...] = _adam_math(g_ref[...], w_ref[...], m_ref[...], v_ref[...])

    blk = pl.BlockSpec((tr, C), lambda i: (i, 0))
    return pl.pallas_call(
        body, name=name, grid=(R // tr,), in_specs=[blk] * 4, out_specs=[blk] * 3,
        out_shape=[jax.ShapeDtypeStruct(g.shape, F32)] * 3, compiler_params=_cparams(("parallel",)),
    )(g, w, m, v)


def _pair_sum(x, recv, core, name):
    _, R, C = x.shape
    tr = _tile(R, 600, 16)

    def body(core_ref, x_ref, r_ref, o32_ref, o16_ref):
        s = x_ref[...] + r_ref[...]
        o32_ref[...] = s
        o16_ref[...] = s.astype(BF16)

    blk = pl.BlockSpec((None, tr, C), lambda q, i, c: (q, i, 0))
    mine = pl.BlockSpec((None, None, tr, C), lambda q, i, c: (q, c[0], i, 0))
    return pl.pallas_call(
        body, name=name,
        grid_spec=pltpu.PrefetchScalarGridSpec(num_scalar_prefetch=1, grid=(4, R // tr), in_specs=[mine, blk], out_specs=[blk, blk]),
        out_shape=[jax.ShapeDtypeStruct((4, R, C), F32), jax.ShapeDtypeStruct((4, R, C), BF16)],
        compiler_params=_cparams(("parallel", "parallel")),
    )(core, x.reshape(4, 2, R, C), recv)


def _sum_slots(x, name):
    def body(x_ref, o_ref):
        g = x_ref[0]
        for s in range(1, x.shape[0]):
            g = g + x_ref[s]
        o_ref[...] = g

    return pl.pallas_call(body, name=name, out_shape=jax.ShapeDtypeStruct(x.shape[1:], F32))(x)


MESH = pl.DeviceIdType.MESH
_HBM = pl.BlockSpec(memory_space=pltpu.HBM)


def _dma_sems(n):
    return pltpu.SemaphoreType.DMA((n,))


def _gather_many(xs, name):
    n = len(xs)

    def body(*refs):
        x_refs, out_refs = refs[:n], refs[n:2 * n]
        send_sems, recv_sems, local_sems = refs[2 * n:]
        ax, ay, ac = lax.axis_index("x"), lax.axis_index("y"), lax.axis_index("c")
        me, sibling = (ax, ay, ac), (ax, ay, 1 - ac)
        chips = [(1 - ax, ay), (ax, 1 - ay), (1 - ax, 1 - ay)]

        def copy(a, k, block, to, own=False):
            slot = out_refs[a].at[4 * block[0] + 2 * block[1] + block[2]]
            return pltpu.make_async_remote_copy(
                src_ref=x_refs[a] if own else slot, dst_ref=slot, send_sem=send_sems.at[7 * a + k],
                recv_sem=recv_sems.at[7 * a + k], device_id=to, device_id_type=MESH)

        mine = [pltpu.make_async_copy(x_refs[a], out_refs[a].at[4 * ax + 2 * ay + ac], local_sems.at[a]) for a in range(n)]
        first = [copy(a, 0, me, sibling, own=True) for a in range(n)]
        first += [copy(a, 1 + j, me, (*chip, ac), own=True) for j, chip in enumerate(chips) for a in range(n)]
        for cp in mine + first:
            cp.start()
        passed = []
        for j, chip in enumerate(chips):
            for a in range(n):
                copy(a, 1 + j, (*chip, ac), me).wait_recv()
                cp = copy(a, 4 + j, (*chip, ac), sibling)
                cp.start()
                passed.append(cp)
        for a in range(n):
            copy(a, 0, sibling, me).wait_recv()
            for j, chip in enumerate(chips):
                copy(a, 4 + j, (*chip, 1 - ac), me).wait_recv()
        for cp in first + passed:
            cp.wait_send()
        for cp in mine:
            cp.wait()

    return pl.pallas_call(
        body, name=name, out_shape=[jax.ShapeDtypeStruct((N_DEV,) + x.shape, x.dtype) for x in xs],
        in_specs=[_HBM] * n, out_specs=[_HBM] * n, scratch_shapes=[_dma_sems(7 * n), _dma_sems(7 * n), _dma_sems(n)],
    )(*xs)


_SEM = pl.BlockSpec(memory_space=pltpu.SEMAPHORE)
_EFFECT = pltpu.SideEffectType.DATAFLOW_SIDE_EFFECTING


def _peer(k):
    ax, ay, ac = lax.axis_index("x"), lax.axis_index("y"), lax.axis_index("c")
    px = 1 - ax if k & 4 else ax
    py = 1 - ay if k & 2 else ay
    pc = 1 - ac if k & 1 else ac
    return (px, py, pc), 4 * px + 2 * py + pc


def _build_gather(x_refs, land_refs, send_sems, recv_sems, waiting):
    _, me = _peer(0)
    copies = []
    for a in range(len(x_refs)):
        for k in range(1, N_DEV):
            peer, slot = _peer(k)
            copies.append(pltpu.make_async_remote_copy(
                src_ref=x_refs[a], dst_ref=land_refs[a].at[slot if waiting else me], send_sem=send_sems.at[7 * a + k - 1],
                recv_sem=recv_sems.at[7 * a + k - 1], device_id=peer, device_id_type=MESH))
    return copies


def _build_cores(x_refs, land_refs, send_sems, recv_sems, waiting):
    ax, ay, ac = lax.axis_index("x"), lax.axis_index("y"), lax.axis_index("c")
    copies = []
    for a in range(len(x_refs)):
        for q in range(4):
            copies.append(pltpu.make_async_remote_copy(
                src_ref=x_refs[a].at[2 * q + 1 - ac], dst_ref=land_refs[a].at[q], send_sem=send_sems.at[4 * a + q],
                recv_sem=recv_sems.at[4 * a + q], device_id=(ax, ay, 1 - ac), device_id_type=MESH))
    return copies


def _build_chips(p_refs, land_refs, send_sems, recv_sems, waiting):
    ax, ay, ac = lax.axis_index("x"), lax.axis_index("y"), lax.axis_index("c")
    copies = []
    for a in range(len(p_refs)):
        for k in range(1, 4):
            px = 1 - ax if k & 2 else ax
            py = 1 - ay if k & 1 else ay
            copies.append(pltpu.make_async_remote_copy(
                src_ref=p_refs[a].at[2 * px + py], dst_ref=land_refs[a].at[k - 1], send_sem=send_sems.at[3 * a + k - 1],
                recv_sem=recv_sems.at[3 * a + k - 1], device_id=(px, py, ac), device_id_type=MESH))
    return copies


_EXCHANGES = {"gather": (_build_gather, 7, N_DEV), "cores": (_build_cores, 4, 4), "chips": (_build_chips, 3, 3)}


def _exchange_start(kind, xs, lands, name, after=None):
    build, per, _ = _EXCHANGES[kind]
    n = len(xs)

    def body(*refs):
        for cp in build(refs[:n], refs[n:2 * n], refs[-2 * n - 3], refs[-2 * n - 2], False):
            cp.start()
        refs[-1][...] = jnp.zeros_like(refs[-1])

    hbm = lambda t: pltpu.HBM(t.shape, t.dtype)
    args = [pltpu.with_memory_space_constraint(t, pltpu.HBM) for t in list(xs) + list(lands)]
    in_specs = [_HBM] * (2 * n)
    if after is not None:
        args.append(after)
        in_specs.append(pl.BlockSpec(memory_space=pl.ANY))
    outs = pl.pallas_call(
        body, name=name,
        out_shape=(_dma_sems(per * n), _dma_sems(per * n), *[hbm(t) for t in xs], *[hbm(t) for t in lands],
                   jax.ShapeDtypeStruct((8, 128), F32)),
        in_specs=in_specs, out_specs=(_SEM, _SEM, *[_HBM] * (2 * n), pl.BlockSpec(memory_space=pltpu.VMEM)),
        input_output_aliases={a: 2 + a for a in range(2 * n)},
        compiler_params=pltpu.CompilerParams(has_side_effects=_EFFECT),
    )(*args)
    return (kind, outs[0], outs[1], outs[2:2 + n], outs[2 + n:2 + 2 * n]), outs[-1]


def _exchange_wait(flight, after, name):
    kind, send_sems, recv_sems, xs, lands = flight
    build = _EXCHANGES[kind][0]
    n = len(xs)

    def body(*refs):
        for cp in build(refs[:n], refs[n:2 * n], refs[2 * n], refs[2 * n + 1], True):
            cp.wait_send()
            cp.wait_recv()

    hbm = lambda t: pltpu.HBM(t.shape, t.dtype)
    outs = pl.pallas_call(
        body, name=name, out_shape=(*[hbm(t) for t in xs], *[hbm(t) for t in lands]),
        in_specs=[_HBM] * (2 * n) + [_SEM, _SEM, pl.BlockSpec(memory_space=pl.ANY)], out_specs=[_HBM] * (2 * n),
        input_output_aliases={a: a for a in range(2 * n)}, compiler_params=pltpu.CompilerParams(has_side_effects=_EFFECT),
    )(*xs, *lands, send_sems, recv_sems, after)
    return outs[:n], outs[n:]


def _bf16_views(fn, dilations):
    extra = [(D, BF16) if d == 1 else ("dilated", D, BF16, d) for d in dilations]

    def wrapped(*a):
        res = fn(*a)
        res = tuple(res) if isinstance(res, (tuple, list)) else (res,)
        return res + (res[-1],) * len(extra)

    return extra, wrapped


def _layer_fwd(x, xb, p, w, cos, sin, rconsts, late=None, views_out=True):
    proj = _mm(xb[1], w["win"], tb=True, b_rows=(N_ATT, N_REST), name="mm_proj", out_dtype=BF16)
    qkvs, ogs, lgs = [], [], []
    for g, dil in enumerate(DILATIONS):
        qkv = _qkv_fwd(xb[dil], w["win"], g, dil, f"mm_qkv{g}")
        o, l = _attn_fwd(qkv, dil, f"attn_fwd_g{g}")
        qkvs.append(qkv)
        ogs.append(_to_tokens(o, dil))
        lgs.append(_to_tokens(l, dil))
    attn, lse = _rowwise(_f_combine, ogs + lgs, [], [(D, BF16), (HD, F32)], [], name="attn_combine")
    ret_raw, states = _ret_fwd(proj, cos, sin, rconsts)
    rg_win = (proj, RH * RDV, OFF_RG // (RH * RDV))
    ga_win, gr_win = (proj, D, OFF_GA // D), (proj, D, OFF_GR // D)
    (r,) = _rowwise(_f_gn, [ret_raw, rg_win], [w["ret_gn_g"], w["ret_gn_b"]], [(RH * RDV, BF16)], [], name="gn_fwd", tm=256)
    if late is not None:
        w = {**w, **late(r)}
    ap = _mm(attn, w["w_attn_proj"], name="mm_attn_proj", out_dtype=BF16)
    rp = _mm(r, w["w_ret_proj"], name="mm_ret_proj", out_dtype=BF16, tk=2048)
    (merged,) = _rowwise(_f_gate, [ap, rp, ga_win, gr_win], [], [(D, BF16)], [], name="gate_fwd")
    mix = _mm(merged, w["w_out"], name="mm_out")
    h1, x1, x1b = _rowwise(_f_ln1, [x, mix], [w["ln1_g"], w["ln1_b"]], [(D, F32), (D, F32), (D, BF16)], [], name="ln1_fwd")
    z = _mm(x1b, w["w_ple_gate"], name="mm_ple_gate")
    pp = _mm(p, w["w_ple_proj"], tb=True, name="mm_ple_proj")
    hg = _mm(x1b, w["w_up"], tb=True, b_rows=(0, DFF), name="mm_up_g", out_dtype=BF16, tm=512, tn=DFF)
    hu = _mm(x1b, w["w_up"], tb=True, b_rows=(DFF, DFF), name="mm_up_u", out_dtype=BF16, tm=512, tn=DFF)
    act = _conv_fwd(hg, hu, w["conv_wg"], w["conv_wu"], w["conv_bg"], w["conv_bu"])
    ffn = _mm(act, w["w_down"], name="mm_down", tm=512, tk=DFF)
    dils = DILATIONS if views_out else (1,)
    extra, fn = _bf16_views(_f_ln2, dils)
    h2, x2, *x2b = _rowwise(fn, [x1, ffn, z, pp], [w["ln2_g"], w["ln2_b"]], [(D, F32), (D, F32)] + extra, [], name="ln2_fwd")
    x2b = dict(zip(dils, x2b))
    saved = dict(xb=xb, proj=proj, qkvs=qkvs, attn=attn, lse=lse, ret_raw=ret_raw, states=states, r=r, ap=ap, rp=rp,
                 merged=merged, h1=h1, x1b=x1b, z=z, pp=pp, hg=hg, hu=hu, act=act, h2=h2, p=p)
    return x2, x2b, saved, w


def _after(fn, token):
    return fn if token is None else (lambda *a: fn(*a[:-1]))


def _layer_bwd(dys, w, sv, cos, sin, rconsts, hooks):
    gr = {}
    proj = sv["proj"]
    call = lambda key, *a: hooks[key](*a) if key in hooks else None
    held = lambda token: [] if token is None else [token]
    token = hooks.get("token")
    dh2, dh2b, dpp, dz, gr["ln2_g"], gr["ln2_b"] = _rowwise(
        _after(_f_ln2_bwd, token), list(dys) + [sv["h2"], sv["z"], sv["pp"]], [w["ln2_g"]] + held(token),
        [(D, F32), (D, BF16), (D, BF16), (D, BF16)], [(1, D), (1, D)], name="ln2_bwd")
    d_act = _mm(dh2b, w["w_down"], tb=True, name="mm_down_dx", out_dtype=BF16, tm=512, tn=DFF)
    gr["w_down"] = _mm(sv["act"], dh2b, ta=True, name="mm_down_dw", tm=DFF // 2)
    dcg, dcu, gwg, gwu, gbg, gbu = _conv_bwd_pre(d_act, sv["hg"], sv["hu"], w["conv_wg"], w["conv_wu"], w["conv_bg"], w["conv_bu"])
    token = call("after_ffn", dcg)
    gr["conv_w"] = jnp.concatenate([gwg, gwu], axis=1)
    gr["conv_b"] = jnp.concatenate([gbg, gbu], axis=1)
    dhg = _conv_bwd_in(dcg, w["conv_wg"], "conv_bwd_in_g")
    dhu = _conv_bwd_in(dcu, w["conv_wu"], "conv_bwd_in_u")
    gw_up = _mm(dhg, sv["x1b"], ta=True, name="mm_up_g_dw", tm=DFF // 2, out_rows=(0, 2 * DFF))
    gr["w_up"] = _mm(dhu, sv["x1b"], ta=True, name="mm_up_u_dw", tm=DFF // 2, out_rows=(DFF, 2 * DFF), into=gw_up)
    dx1 = _mm(dhg, w["w_up"], b_rows=(0, DFF), name="mm_up_g_dx", add=dh2, add_scale=ALPHA, tm=512, tk=DFF)
    dx1 = _mm(dhu, w["w_up"], b_rows=(DFF, DFF), name="mm_up_u_dx", add=dx1, tm=512, tk=DFF)
    gr["w_ple_proj"] = _mm(dpp, sv["p"], ta=True, name="mm_ple_proj_dw")
    gr["w_ple_gate"] = _mm(sv["x1b"], dz, ta=True, name="mm_ple_gate_dw")
    dx1 = _mm(dz, w["w_ple_gate"], tb=True, name="mm_ple_gate_dx", add=dx1)
    dh1, dh1b, gr["ln1_g"], gr["ln1_b"] = _rowwise(_after(_f_ln_bwd, token), [dx1, sv["h1"]], [w["ln1_g"]] + held(token),
                                                   [(D, F32), (D, BF16)], [(1, D), (1, D)], name="ln1_bwd")
    d_merged = _mm(dh1b, w["w_out"], tb=True, name="mm_out_dx", out_dtype=BF16)
    gr["w_out"] = _mm(sv["merged"], dh1b, ta=True, name="mm_out_dw")
    rg_win = (proj, RH * RDV, OFF_RG // (RH * RDV))
    ga_win, gr_win = (proj, D, OFF_GA // D), (proj, D, OFF_GR // D)
    dap, drp, d_rest = _rowwise(_f_gate_bwd, [d_merged, sv["ap"], sv["rp"], ga_win, gr_win], [],
                                [(D, BF16), (D, BF16), (2 * D, BF16, N_REST, OFF_GA // (2 * D), None)], [], name="gate_bwd")
    d_attn = _mm(dap, w["w_attn_proj"], tb=True, name="mm_attn_proj_dx", out_dtype=BF16)
    gr["w_attn_proj"] = _mm(sv["attn"], dap, ta=True, name="mm_attn_proj_dw")
    d_r = _mm(drp, w["w_ret_proj"], tb=True, name="mm_ret_proj_dx", out_dtype=BF16, tn=2048)
    gr["w_ret_proj"] = _mm(sv["r"], drp, ta=True, name="mm_ret_proj_dw", tm=2048)
    token = call("early_grads", gr)
    d_ret, d_rest, gr["ret_gn_g"], gr["ret_gn_b"] = _rowwise(
        _after(_f_gn_bwd, token), [d_r, sv["ret_raw"], rg_win], [w["ret_gn_g"], w["ret_gn_b"]] + held(token),
        [(RH * RDV, BF16), (RH * RDV, BF16, N_REST, OFF_RG // (RH * RDV), d_rest)],
        [(1, RH * RDV), (1, RH * RDV)], name="gn_bwd", tm=256)
    d_rest = _ret_bwd(proj, cos, sin, rconsts, sv["states"], d_ret, d_rest)
    token = call("after_ret", d_rest)
    (delta,) = _rowwise(_after(_f_delta, token), [d_attn, sv["attn"]], held(token), [(HD, F32)], [], name="attn_delta")
    gw_in, dqkvs = None, []
    for g, dil in enumerate(DILATIONS):
        dqkvs.append(_attn_bwd(sv["qkvs"][g], _to_head_residues(d_attn, dil), _to_residues(sv["lse"], dil),
                               _to_residues(delta, dil), dil, f"attn_bwd_g{g}"))
        gw_in = _qkv_dw(dqkvs[g], sv["xb"][dil], g, dil, f"mm_qkv{g}_dw", into=gw_in)
    gw_in = _mm(d_rest, sv["xb"][1], ta=True, name="mm_proj_dw", out_rows=(N_ATT, N_IN), into=gw_in, blocks8=True)
    gr["w_in"] = gw_in.reshape(N_IN, D)
    token = call("w_in_ready", gr["w_in"])
    dx0 = _mm(d_rest, w["win"], b_rows=(N_ATT, N_REST), name="mm_proj_dx", add=dh1, add_scale=ALPHA, after=token)
    dx_parts = []
    for g, dil in enumerate(DILATIONS):
        if dil == 1:
            dx0 = _qkv_dx(dqkvs[g], w["win"], g, dil, f"mm_qkv{g}_dx", F32, add=dx0)
            token = call("after_dx0", dx0)
        else:
            dx_parts.append(("dilated", _qkv_dx(dqkvs[g], w["win"], g, dil, f"mm_qkv{g}_dx", BF16, after=token), dil))
    return [dx0] + dx_parts, gr


def _local_step(x, p, positions, target, ws, own_hooks=None, on_grads=None, token=None):
    half = RDK // 2
    freq = jnp.power(ROPE_BASE, -jnp.arange(half, dtype=F32) / half)
    ang = positions.astype(F32)[:, None] * freq[None, :]
    cos, sin = jnp.cos(ang), jnp.sin(ang)
    rconsts = _ret_consts()
    extra, _ = _bf16_views(None, DILATIONS)
    held = [] if token is None else [token]
    xb = dict(zip(DILATIONS, _rowwise(_after(lambda v: (v,) * len(extra), token), [x], held, extra, [], name="cast_x")))
    saved, ws = [], list(ws)
    for l in range(DEPTH):
        first, late = ws[l] if isinstance(ws[l], tuple) else (ws[l], None)
        if callable(first):
            first = first(x)
        x, xb, sv, ws[l] = _layer_fwd(x, xb, p[l], first, cos, sin, rconsts, late, views_out=l < DEPTH - 1)
        saved.append(sv)
    dy, loss_vec = _rowwise(_f_loss, [x, target], [], [(D, F32)], [(1, D)], name="loss")
    dys, grads = [dy], [None] * DEPTH
    from_above = {}
    for l in reversed(range(DEPTH)):
        hooks = {**from_above, **(own_hooks(l) if own_hooks else {})}
        dys, grads[l] = _layer_bwd(dys, ws[l], saved[l], cos, sin, rconsts, hooks)
        from_above = on_grads(l, grads[l]) if on_grads else {}
    (grad_x,) = _rowwise(_f_sum, dys, [], [(D, F32)], [], name="grad_x_sum")
    return loss_vec, grad_x, grads


def _pack_rows(arrs):
    parts, where, off = [], [], 0
    for t in arrs:
        t = t.reshape(-1, D)
        rows = t.shape[0]
        padded = -(-rows // 8) * 8
        parts.append(jnp.pad(t, ((0, padded - rows), (0, 0))))
        where.append((off, rows))
        off += padded
    return jnp.concatenate(parts, axis=0), where


FIRST = ("w_in",)
LATER = tuple(n for n in BIG if n not in FIRST)


def _first_weights(g, l, W):
    w = dict(win=g["w_in"].reshape(N_IN, D))
    for n in ("ret_gn_g", "ret_gn_b", "ln1_g", "ln1_b", "ln2_g", "ln2_b"):
        w[n] = W[n][l][None, :]
    return w


def _later_weights(g, l, conv_w_all, conv_b):
    w = dict(w_up=g["w_up"].reshape(2 * DFF, D), w_ple_proj=g["w_ple_proj"].reshape(D, PLE),
             w_attn_proj=g["w_attn_proj"].reshape(D, D), w_ret_proj=g["w_ret_proj"].reshape(RH * RDV, D),
             w_out=g["w_out"].reshape(D, D), w_down=g["w_down"].reshape(DFF, D), w_ple_gate=g["w_ple_gate"].reshape(D, D))
    w["conv_wg"], w["conv_wu"] = conv_w_all[l][:, :DFF], conv_w_all[l][:, DFF:]
    w["conv_bg"], w["conv_bu"] = conv_b[l][None, :DFF], conv_b[l][None, DFF:]
    return w


def _layer_weights(g, l, conv_w_all, conv_b, W):
    return {**_first_weights(g, l, W), **_later_weights(g, l, conv_w_all, conv_b)}


def kernel(x, p, positions, w_in, w_attn_proj, w_ret_proj, ret_gn_g, ret_gn_b, w_out, ln1_g, ln1_b, w_up, conv_w, conv_b, w_down, w_ple_gate, w_ple_proj, ln2_g, ln2_b, loss_target, m_w_in, m_w_attn_proj, m_w_ret_proj, m_ret_gn_g, m_ret_gn_b, m_w_out, m_ln1_g, m_ln1_b, m_w_up, m_conv_w, m_conv_b, m_w_down, m_w_ple_gate, m_w_ple_proj, m_ln2_g, m_ln2_b, v_w_in, v_w_attn_proj, v_w_ret_proj, v_ret_gn_g, v_ret_gn_b, v_w_out, v_ln1_g, v_ln1_b, v_w_up, v_conv_w, v_conv_b, v_w_down, v_w_ple_gate, v_w_ple_proj, v_ln2_g, v_ln2_b):
    W = dict(w_in=w_in, w_attn_proj=w_attn_proj, w_ret_proj=w_ret_proj, ret_gn_g=ret_gn_g, ret_gn_b=ret_gn_b, w_out=w_out,
             ln1_g=ln1_g, ln1_b=ln1_b, w_up=w_up, conv_w=conv_w, conv_b=conv_b, w_down=w_down, w_ple_gate=w_ple_gate,
             w_ple_proj=w_ple_proj, ln2_g=ln2_g, ln2_b=ln2_b)
    M = dict(w_in=m_w_in, w_attn_proj=m_w_attn_proj, w_ret_proj=m_w_ret_proj, ret_gn_g=m_ret_gn_g, ret_gn_b=m_ret_gn_b,
             w_out=m_w_out, ln1_g=m_ln1_g, ln1_b=m_ln1_b, w_up=m_w_up, conv_w=m_conv_w, conv_b=m_conv_b, w_down=m_w_down,
             w_ple_gate=m_w_ple_gate, w_ple_proj=m_w_ple_proj, ln2_g=m_ln2_g, ln2_b=m_ln2_b)
    V = dict(w_in=v_w_in, w_attn_proj=v_w_attn_proj, w_ret_proj=v_w_ret_proj, ret_gn_g=v_ret_gn_g, ret_gn_b=v_ret_gn_b,
             w_out=v_w_out, ln1_g=v_ln1_g, ln1_b=v_ln1_b, w_up=v_w_up, conv_w=v_conv_w, conv_b=v_conv_b, w_down=v_w_down,
             w_ple_gate=v_w_ple_gate, w_ple_proj=v_w_ple_proj, ln2_g=v_ln2_g, ln2_b=v_ln2_b)

    me = 4 * lax.axis_index("x") + 2 * lax.axis_index("y") + lax.axis_index("c")
    shard = lambda n, l: (W[n][l].T if n in COL_SHARDED else W[n][l]).astype(BF16)
    landing = lambda ts: [lax.dynamic_update_index_in_dim(lax.empty((N_DEV,) + t.shape, t.dtype), t, me, 0) for t in ts]
    first0 = _gather_many([shard(n, 0) for n in FIRST], "gather_first_l0")
    later0 = [shard(n, 0) for n in LATER] + [conv_w]
    flight0, token0 = _exchange_start("gather", later0, landing(later0), "gather_later_l0_start", after=first0[0])
    all1 = [shard(n, 1) for n in BIG]
    flight1, token1 = _exchange_start("gather", all1, landing(all1), "gather_weights_l1_start", after=token0)
    conv_w_all = []

    def later_first_layer(after):
        _, got = _exchange_wait(flight0, after, "gather_later_l0_wait")
        conv_w_all.append(got[-1].transpose(1, 2, 0, 3).reshape(DEPTH, 3, 2 * DFF))
        return _later_weights(dict(zip(LATER, got)), 0, conv_w_all[0], conv_b)

    def second_layer(after):
        _, got = _exchange_wait(flight1, after, "gather_weights_l1_wait")
        return _layer_weights(dict(zip(BIG, got)), 1, conv_w_all[0], conv_b, W)

    core = lax.axis_index("c").astype(jnp.int32).reshape(1)
    chip = (2 * lax.axis_index("x") + lax.axis_index("y")).astype(jnp.int32).reshape(1)
    empty_like = lambda ts, slots: [lax.empty((slots,) + t.shape[1:], t.dtype) for t in ts]
    chip32, far = [{} for _ in range(DEPTH)], [{} for _ in range(DEPTH)]
    pending = []

    def reduction(l, names, tag):
        state = {}

        def start(g):
            mine = [g[n].reshape((N_DEV, -1) + g[n].shape[1:]) for n in names]
            state["cores"], token = _exchange_start("cores", mine, empty_like(mine, 4), f"exchange_cores_{tag}_start")
            return token

        def onward(after):
            mine, theirs = _exchange_wait(state["cores"], after, f"exchange_cores_{tag}_wait")
            sums = [_pair_sum(a, b, core, f"pair_sum_l{l}_{n}") for a, b, n in zip(mine, theirs, names)]
            for n, s in zip(names, sums):
                chip32[l][n] = s[0]
            sent = [s[0 if n in F32_OVER_ICI else 1] for s, n in zip(sums, names)]
            flight, token = _exchange_start("chips", sent, empty_like(sent, 3), f"exchange_chips_{tag}_start")
            pending.append((l, names, flight, tag))
            return token

        return start, onward

    def on_grads(l, g):
        if l == 0:
            return {}
        start, onward = reduction(l, BIG, f"l{l}")
        return dict(token=start(g), after_ffn=onward)

    def own_hooks(l):
        if l != 0:
            return {}
        start_e, onward_e = reduction(0, LATER, "l0_later")
        start_w, onward_w = reduction(0, FIRST, "l0_first")
        return dict(early_grads=start_e, after_ret=onward_e, w_in_ready=lambda gw: start_w({"w_in": gw}), after_dx0=onward_w)

    ws = [(_first_weights(dict(zip(FIRST, first0)), 0, W), later_first_layer), second_layer]
    loss_vec, grad_x, grads = _local_step(x[0], p[:, 0], positions[0], loss_target[0], ws, own_hooks, on_grads, token1)
    loss = lax.psum(jnp.sum(loss_vec), ("x", "y", "c"))
    for l, names, flight, tag in pending:
        _, got = _exchange_wait(flight, grad_x, f"exchange_chips_{tag}_wait")
        far[l].update(zip(names, got))
    G, DW, NM, NV = ({} for _ in range(4))
    for n in BIG:
        chip32_n = [chip32[l][n] for l in range(DEPTH)]
        far_n = [far[l][n] for l in range(DEPTH)]
        if n in COL_SHARDED:
            G[n] = _reduce_tail(chip32_n, far_n, chip, f"reduced_{n}")[0].transpose(0, 2, 1)
            R2, C2 = DEPTH * W[n].shape[1], W[n].shape[2]
            res = _adamw(*(t.reshape(R2, C2) for t in (G[n], W[n], M[n], V[n])), f"adamw_{n}")
            DW[n], NM[n], NV[n] = (t.reshape(W[n].shape) for t in res)
        else:
            G[n], DW[n], NM[n], NV[n] = _reduce_tail(chip32_n, far_n, chip, f"adamw_{n}", wmv=(W[n], M[n], V[n]))

    small_names = SMALL + ("conv_w",)
    g_small, where = _pack_rows([jnp.stack([grads[l][n] for l in range(DEPTH)]) for n in small_names])
    (g_all,) = _gather_many([g_small], "gather_small_grads")
    g_small = _sum_slots(g_all, "sum_small_grads")
    for n, (off, rows) in zip(SMALL, where):
        G[n] = g_small[off:off + rows].reshape(W[n].shape)
    off, rows = where[-1]
    g_cw = g_small[off:off + rows].reshape(DEPTH, 3, N_DEV, conv_w.shape[2])
    G["conv_w"] = lax.dynamic_index_in_dim(g_cw, me, axis=2, keepdims=False)
    packed = [_pack_rows([d[n] for n in SMALL]) for d in (G, W, M, V)]
    small_out = _adamw(*(t for t, _ in packed), "adamw_small")
    for res, dst in zip(small_out, (DW, NM, NV)):
        for n, (off, rows) in zip(SMALL, packed[0][1]):
            dst[n] = res[off:off + rows].reshape(W[n].shape)
    two_d = lambda t: t.reshape(DEPTH * 3, conv_w.shape[2])
    cw_out = _adamw(two_d(G["conv_w"]), two_d(conv_w), two_d(m_conv_w), two_d(v_conv_w), "adamw_conv_w")
    for res, dst in zip(cw_out, (DW, NM, NV)):
        dst["conv_w"] = res.reshape(conv_w.shape)

    return (loss, grad_x[None], *[G[n] for n in WEIGHTS], *[DW[n] for n in WEIGHTS], *[NM[n] for n in WEIGHTS],
            *[NV[n] for n in WEIGHTS])
```

```python
import math

import numpy as np
import jax
import jax.numpy as jnp
from jax import lax
from jax.experimental import pallas as pl
from jax.experimental.pallas import tpu as pltpu

F32, BF16 = jnp.float32, jnp.bfloat16

D = 1024
DEPTH = 2
N_DEV = 8
HD = 128
NH = 8
DILATIONS = (1, 4, 16)
SPAN = 128
N_ATT = 3 * 3 * NH * HD
RH, RDK, RDV = 4, 256, 512
CH = 128
DFF = 2816
PLE = 256
N_IN = 17408
N_REST = N_IN - N_ATT
OFF_RQ, OFF_RK, OFF_RV, OFF_RG, OFF_GA, OFF_GR = 0, 1024, 2048, 4096, 6144, 7168
ALPHA = (2 * DEPTH) ** 0.25
LN_EPS, GN_EPS = 1e-5, 1e-6
ROPE_BASE = 10000.0
LR, B1, B2, EPS, WD, STEP = 0.001, 0.9, 0.999, 1e-8, 0.01, 10
VMEM_LIMIT = 48 * 1024 * 1024
NEG = -1e30

BIG = ("w_in", "w_attn_proj", "w_ret_proj", "w_out", "w_up", "w_down", "w_ple_gate", "w_ple_proj")
COL_SHARDED = ("w_in", "w_up", "w_ple_proj")
F32_OVER_ICI = ("w_attn_proj", "w_out", "w_ple_gate", "w_ple_proj")
SMALL = ("ret_gn_g", "ret_gn_b", "ln1_g", "ln1_b", "conv_b", "ln2_g", "ln2_b")
WEIGHTS = ("w_in", "w_attn_proj", "w_ret_proj", "ret_gn_g", "ret_gn_b", "w_out", "ln1_g", "ln1_b", "w_up",
           "conv_w", "conv_b", "w_down", "w_ple_gate", "w_ple_proj", "ln2_g", "ln2_b")


def _tile(n, cap, mult=128):
    if n <= cap:
        return n
    t = (cap // mult) * mult
    while n % t:
        t -= mult
    return t


def _cparams(sem):
    return pltpu.CompilerParams(dimension_semantics=sem, vmem_limit_bytes=VMEM_LIMIT)


def _dot(a, b, ca, cb):
    return lax.dot_general(a, b, (((ca,), (cb,)), ((), ())), preferred_element_type=F32)


def _bdot(a, b, ca, cb):
    return lax.dot_general(a, b, (((ca,), (cb,)), ((0,), (0,))), preferred_element_type=F32)


def _mm(a, b, *, name, ta=False, tb=False, out_dtype=F32, add=None, add_scale=1.0, tm=1024, tn=1024, tk=1024,
        b_rows=None, out_rows=None, into=None, blocks8=False, after=None):
    M, K = (a.shape[1], a.shape[0]) if ta else a.shape
    b_first, b_count = b_rows if b_rows else (0, b.shape[0])
    N = b_count if tb else b.shape[1]
    assert K == (b.shape[1] if tb else b_count)
    tm, tn, tk = _tile(M, tm), _tile(N, tn), _tile(K, tk)
    nk = K // tk
    o_first, o_total = out_rows if out_rows else (0, M)
    jb, kb, io = (b_first // tn, 0, o_first // tm) if tb else (0, b_first // tk, o_first // tm)
    assert b_first % (tn if tb else tk) == 0 and o_first % tm == 0 and (add is None or out_rows is None)

    def body(*refs):
        if add is None:
            a_ref, b_ref = refs[:2]
        else:
            a_ref, b_ref, add_ref = refs[:3]
        o_ref, acc_ref = refs[-2:]
        k = pl.program_id(2)

        @pl.when(k == 0)
        def _():
            acc_ref[...] = jnp.zeros_like(acc_ref)

        acc_ref[...] += _dot(a_ref[...].astype(BF16), b_ref[...].astype(BF16), 0 if ta else 1, 1 if tb else 0)

        @pl.when(k == nk - 1)
        def _():
            r = acc_ref[...]
            if add is not None:
                r = r + add_scale * add_ref[...].astype(F32)
            o_ref[...] = r.astype(out_dtype).reshape(o_ref.shape)

    a_spec = pl.BlockSpec((tk, tm), lambda i, j, k: (k, i)) if ta else pl.BlockSpec((tm, tk), lambda i, j, k: (i, k))
    if tb:
        b_spec = pl.BlockSpec((tn, tk), lambda i, j, k: (j + jb, k))
    else:
        b_spec = pl.BlockSpec((tk, tn), lambda i, j, k: (k + kb, j))
    if blocks8:
        assert tm == 1024
        o_spec = pl.BlockSpec((1, 8, 128, tn), lambda i, j, k: (i + io, 0, 0, j))
        o_shape = (o_total // tm, 8, 128, N)
    else:
        o_spec = pl.BlockSpec((tm, tn), lambda i, j, k: (i + io, j))
        o_shape = (o_total, N)
    in_specs, args, aliases = [a_spec, b_spec], [a, b], {}
    if add is not None:
        in_specs.append(o_spec)
        args.append(add)
    if after is not None:
        in_specs.append(pl.BlockSpec(memory_space=pl.ANY))
        args.append(after)
    if into is not None:
        aliases = {len(args): 0}
        in_specs.append(pl.BlockSpec(memory_space=pl.ANY))
        args.append(into)
    return pl.pallas_call(
        body, name=name, grid=(M // tm, N // tn, nk), in_specs=in_specs, out_specs=o_spec,
        out_shape=jax.ShapeDtypeStruct(o_shape, out_dtype), scratch_shapes=[pltpu.VMEM((tm, tn), F32)],
        input_output_aliases=aliases, compiler_params=_cparams(("parallel", "parallel", "arbitrary")),
    )(*args)


def _rowwise(fn, rows, pars, outs, accs, *, name, tm=512):
    first = rows[0][0] if isinstance(rows[0], tuple) else rows[0]
    S = first.shape[-2]
    tm = _tile(S, tm, 16)
    n_r, n_p, n_o = len(rows), len(pars), len(outs)
    views_out = {k: o[3] for k, o in enumerate(outs) if o[0] == "dilated"}
    outs = [(o[1], o[2], o[1], 0, None) if o[0] == "dilated" else o if len(o) == 5 else (o[0], o[1], o[0], 0, None) for o in outs]
    intos = [(k, o[4]) for k, o in enumerate(outs) if o[4] is not None]
    n_i = len(intos)
    dilated = {k: r[2] for k, r in enumerate(rows) if isinstance(r, tuple) and r[0] == "dilated"}
    n_s = len(dilated) + len(views_out)

    def body(*refs):
        i = pl.program_id(0)
        scratch = refs[len(refs) - n_s:]
        vals = [r[...] for r in refs[:n_r + n_p]]
        for scr, (k, d) in zip(scratch, dilated.items()):
            w = vals[k].shape[1] // d
            for r in range(d):
                for c in range(w // 128):
                    scr.at[c][pl.ds(r, tm // d, stride=d), :] = vals[k][:, r * w + c * 128:r * w + (c + 1) * 128].astype(F32)
            vals[k] = jnp.concatenate([scr[c] for c in range(w // 128)], axis=1)
        res = fn(*vals)
        if not isinstance(res, (tuple, list)):
            res = (res,)
        res = list(res)
        o_refs = refs[n_r + n_p + n_i:n_r + n_p + n_i + n_o]
        a_refs = refs[n_r + n_p + n_i + n_o:len(refs) - n_s]
        for scr, (k, d) in zip(scratch[len(dilated):], views_out.items()):
            v = res[k].astype(F32)
            nc = v.shape[1] // 128
            for c in range(nc):
                scr[c] = v[:, c * 128:(c + 1) * 128]
            res[k] = jnp.concatenate([scr.at[c][pl.ds(r, tm // d, stride=d), :] for r in range(d) for c in range(nc)], axis=1)
        for r, v in zip(o_refs, res[:n_o]):
            r[...] = v.astype(r.dtype)
        if a_refs:
            @pl.when(i == 0)
            def _():
                for r in a_refs:
                    r[...] = jnp.zeros_like(r)

            for r, v in zip(a_refs, res[n_o:]):
                r[...] += v

    in_specs, args = [], []
    for r in rows:
        if isinstance(r, tuple) and r[0] == "dilated":
            _, arr, d = r
            in_specs.append(pl.BlockSpec((tm // d, arr.shape[1]), lambda i: (i, 0)))
        elif isinstance(r, tuple):
            arr, w, cb = r
            in_specs.append(pl.BlockSpec((tm, w), lambda i, cb=cb: (i, cb)))
        elif r.ndim == 3:
            arr = r
            in_specs.append(pl.BlockSpec((arr.shape[0], tm, arr.shape[2]), lambda i: (0, i, 0)))
        else:
            arr = r
            in_specs.append(pl.BlockSpec((tm, arr.shape[1]), lambda i: (i, 0)))
        args.append(arr)
    for p_ in pars:
        in_specs.append(pl.BlockSpec(p_.shape, lambda i: (0, 0)))
        args.append(p_)
    aliases = {}
    for k, arr in intos:
        aliases[len(args)] = k
        in_specs.append(pl.BlockSpec(memory_space=pl.ANY))
        args.append(arr)
    out_shape = [jax.ShapeDtypeStruct((S // views_out.get(k, 1), views_out.get(k, 1) * o[2]), o[1]) for k, o in enumerate(outs)]
    out_shape += [jax.ShapeDtypeStruct(a, F32) for a in accs]
    out_specs = [pl.BlockSpec((tm // views_out.get(k, 1), views_out.get(k, 1) * o[0]), lambda i, cb=o[3]: (i, cb))
                 for k, o in enumerate(outs)] + [pl.BlockSpec(a, lambda i: (0, 0)) for a in accs]
    scratch = [pltpu.VMEM((rows[k][1].shape[1] // d // 128, tm, 128), F32) for k, d in dilated.items()]
    scratch += [pltpu.VMEM((outs[k][0] // 128, tm, 128), F32) for k in views_out]
    return pl.pallas_call(
        body, name=name, grid=(S // tm,), in_specs=in_specs, out_specs=out_specs, out_shape=out_shape,
        scratch_shapes=scratch, input_output_aliases=aliases,
        compiler_params=_cparams(("arbitrary",) if accs else ("parallel",)),
    )(*args)


def _norm(h, eps):
    mu = jnp.mean(h, -1, keepdims=True)
    d = h - mu
    rstd = lax.rsqrt(jnp.mean(d * d, -1, keepdims=True) + eps)
    return d * rstd, rstd


def _norm_bwd(dxh, xh, rstd):
    return rstd * (dxh - jnp.mean(dxh, -1, keepdims=True) - xh * jnp.mean(dxh * xh, -1, keepdims=True))


def _sig(x):
    return 1.0 / (1.0 + jnp.exp(-x))


_GELU_C = math.sqrt(2.0 / math.pi)


def _gelu(x, with_grad=False):
    x2 = x * x
    t = jnp.tanh(x * (_GELU_C + (_GELU_C * 0.044715) * x2))
    half_x, one_t = 0.5 * x, 1.0 + t
    if not with_grad:
        return half_x * one_t
    return half_x * one_t, 0.5 * one_t + half_x * (1.0 - t * t) * (_GELU_C + (3 * _GELU_C * 0.044715) * x2)


def _f_ln1(x, mix, g, b):
    h = ALPHA * x + mix
    xh, _ = _norm(h, LN_EPS)
    y = xh * g + b
    return h, y, y


def _mm_norm(a, w, x, g, b, name):
    S, K = a.shape
    tm = min(512, S)

    def body(a_ref, w_ref, x_ref, g_ref, b_ref, h_ref, y_ref, yb_ref):
        h, y, _ = _f_ln1(x_ref[...], _dot(a_ref[...], w_ref[...], 1, 0), g_ref[...], b_ref[...])
        h_ref[...] = h
        y_ref[...] = y
        yb_ref[...] = y.astype(BF16)

    row = lambda width: pl.BlockSpec((tm, width), lambda i: (i, 0))
    whole = lambda t: pl.BlockSpec(t.shape, lambda i: (0, 0))
    return pl.pallas_call(
        body, name=name, grid=(S // tm,), in_specs=[row(K), whole(w), row(D), whole(g), whole(b)], out_specs=[row(D)] * 3,
        out_shape=[jax.ShapeDtypeStruct((S, D), F32)] * 2 + [jax.ShapeDtypeStruct((S, D), BF16)],
        compiler_params=_cparams(("parallel",)),
    )(a, w, x, g, b)


def _f_ln2(x, ffn, z, pp, g, b):
    h = ALPHA * x + ffn + _sig(z) * pp
    xh, _ = _norm(h, LN_EPS)
    return h, xh * g + b


def _f_ln_bwd(*args):
    *dys, h, g = args
    dy = dys[0]
    for t in dys[1:]:
        dy = dy + t
    xh, rstd = _norm(h, LN_EPS)
    dh = _norm_bwd(dy * g, xh, rstd)
    return dh, dh, jnp.sum(dy * xh, 0, keepdims=True), jnp.sum(dy, 0, keepdims=True)


def _f_sum(*ts):
    r = ts[0]
    for t in ts[1:]:
        r = r + t
    return r


def _f_loss(y, t):
    e = y - t
    return e * (1.0 / D), jnp.sum(e * e, 0, keepdims=True) * (0.5 / D)


def _head_col(c, h):
    lane = lax.broadcasted_iota(jnp.int32, c.shape, 1)
    return jnp.sum(jnp.where(lane == h, c, 0.0), -1, keepdims=True)


def _f_combine(o0, o1, o2, l0, l1, l2):
    m = jnp.maximum(jnp.maximum(l0, l1), l2)
    e0, e1, e2 = jnp.exp(l0 - m), jnp.exp(l1 - m), jnp.exp(l2 - m)
    den = e0 + e1 + e2
    inv = 1.0 / den
    w0, w1, w2 = e0 * inv, e1 * inv, e2 * inv
    parts = [_head_col(w0, h) * o0[h].astype(F32) + _head_col(w1, h) * o1[h].astype(F32) + _head_col(w2, h) * o2[h].astype(F32)
             for h in range(NH)]
    return jnp.concatenate(parts, axis=1), m + jnp.log(den)


def _f_delta(da, a):
    lane = lax.broadcasted_iota(jnp.int32, (da.shape[0], HD), 1)
    out = jnp.zeros((da.shape[0], HD), F32)
    for h in range(NH):
        sl = slice(h * HD, (h + 1) * HD)
        s = jnp.sum(da[:, sl].astype(F32) * a[:, sl].astype(F32), -1, keepdims=True)
        out = jnp.where(lane == h, s, out)
    return out


def _f_gate(ap, rp, ga, gr):
    return _sig(ga.astype(F32)) * ap.astype(F32) + _sig(gr.astype(F32)) * rp.astype(F32)


def _f_gate_bwd(dm, ap, rp, ga, gr):
    dm = dm.astype(F32)
    sa, sr = _sig(ga.astype(F32)), _sig(gr.astype(F32))
    dga, dgr = dm * ap.astype(F32) * sa * (1.0 - sa), dm * rp.astype(F32) * sr * (1.0 - sr)
    return dm * sa, dm * sr, jnp.concatenate([dga, dgr], axis=1)


def _f_gn(y, rg, g, b):
    y, rg = y.astype(F32), rg.astype(F32)
    parts = []
    for h in range(RH):
        sl = slice(h * RDV, (h + 1) * RDV)
        xh, _ = _norm(y[:, sl], GN_EPS)
        parts.append(xh * g[:, sl] + b[:, sl])
    return rg * _sig(rg) * jnp.concatenate(parts, axis=1)


def _f_gn_bwd(dr, y, rg, g, b):
    dr, y, rg = dr.astype(F32), y.astype(F32), rg.astype(F32)
    s = _sig(rg)
    d_out = dr * rg * s
    dys, outs, xhs = [], [], []
    for h in range(RH):
        sl = slice(h * RDV, (h + 1) * RDV)
        xh, rstd = _norm(y[:, sl], GN_EPS)
        xhs.append(xh)
        outs.append(xh * g[:, sl] + b[:, sl])
        dys.append(_norm_bwd(d_out[:, sl] * g[:, sl], xh, rstd))
    xh, out = jnp.concatenate(xhs, axis=1), jnp.concatenate(outs, axis=1)
    d_rg = dr * out * s * (1.0 + rg * (1.0 - s))
    return jnp.concatenate(dys, axis=1), d_rg, jnp.sum(d_out * xh, 0, keepdims=True), jnp.sum(d_out, 0, keepdims=True)


def _f_ln2_bwd(*args):
    *dys, h, z, pp, g = args
    dh, dhb, dg, db = _f_ln_bwd(*dys, h, g)
    s = _sig(z)
    return dh, dhb, dh * s, dh * pp * s * (1.0 - s), dg, db


QKV = 3 * HD


def _to_tokens(t, d):
    if d == 1:
        return t
    *lead, S, C = t.shape
    n = len(lead)
    perm = tuple(range(n)) + (n + 1, n, n + 2)
    return t.reshape(*lead, d, S // d, C).transpose(perm).reshape(*lead, S, C)


def _to_residues(t, d):
    if d == 1:
        return t
    S, C = t.shape
    return t.reshape(S // d, d, C).transpose(1, 0, 2).reshape(S, C)


def _to_head_residues(t, d):
    S = t.shape[0]
    return t.reshape(S // d, d, NH, HD).transpose(2, 1, 0, 3).reshape(NH, S, HD)


def _w_qkv_specs(g):
    return [pl.BlockSpec((D, D), lambda *i, t=t: (3 * g + t, 0)) for t in range(3)]


def _qkv_fwd(xv, win, g, dil, name):
    Sd = xv.shape[0]
    S = Sd * dil
    tm = min(512, Sd)
    nma = Sd // tm

    def body(a_ref, wq_ref, wk_ref, wv_ref, o_ref):
        a = a_ref[...]
        q, k, v = (_dot(a, w_ref[...], 1, 1).astype(BF16) for w_ref in (wq_ref, wk_ref, wv_ref))
        for h in range(NH):
            sl = slice(h * HD, (h + 1) * HD)
            o_ref[h] = jnp.concatenate([q[:, sl], k[:, sl], v[:, sl]], axis=1)

    return pl.pallas_call(
        body, name=name, grid=(S // tm,),
        in_specs=[pl.BlockSpec((tm, D), lambda i: (i % nma, i // nma))] + _w_qkv_specs(g),
        out_specs=pl.BlockSpec((NH, tm, QKV), lambda i: (0, i, 0)), out_shape=jax.ShapeDtypeStruct((NH, S, QKV), BF16),
        compiler_params=_cparams(("parallel",)),
    )(xv, win, win, win)


def _qkv_dx(dqkv, win, g, dil, name, out_dtype, add=None, after=None):
    S = dqkv.shape[1]
    Sd = S // dil
    tm = min(512, Sd)
    nmo = Sd // tm

    def body(*refs):
        a_ref, wq_ref, wk_ref, wv_ref = refs[:4]
        o_ref = refs[-1]
        acc = None
        for t, w_ref in enumerate((wq_ref, wk_ref, wv_ref)):
            d = jnp.concatenate([a_ref[h][:, t * HD:(t + 1) * HD] for h in range(NH)], axis=1)
            part = _dot(d, w_ref[...], 1, 0)
            acc = part if acc is None else acc + part
        if add is not None:
            acc = acc + refs[4][...]
        o_ref[...] = acc.astype(out_dtype)

    o_spec = pl.BlockSpec((tm, D), lambda i: (i % nmo, i // nmo))
    in_specs = [pl.BlockSpec((NH, tm, QKV), lambda i: (0, i, 0))] + _w_qkv_specs(g)
    args = [dqkv, win, win, win]
    if add is not None:
        assert dil == 1
        in_specs.append(o_spec)
        args.append(add)
    if after is not None:
        in_specs.append(pl.BlockSpec(memory_space=pl.ANY))
        args.append(after)
    return pl.pallas_call(
        body, name=name, grid=(S // tm,), in_specs=in_specs, out_specs=o_spec,
        out_shape=jax.ShapeDtypeStruct((Sd, dil * D), out_dtype), compiler_params=_cparams(("parallel",)),
    )(*args)


GW_IN_BLOCKS = (N_IN // D, NH, HD, D)


def _qkv_dw(dqkv, xv, g, dil, name, into=None):
    S = dqkv.shape[1]
    Sd = S // dil
    tk = min(1024, Sd)
    nkb, nk = Sd // tk, S // tk
    hh = NH // 2

    def body(*refs):
        a_ref, b_ref = refs[:2]
        o_ref, acc_ref = refs[-2:]
        k = pl.program_id(1)

        @pl.when(k == 0)
        def _():
            acc_ref[...] = jnp.zeros_like(acc_ref)

        b = b_ref[...]
        for h in range(hh):
            acc_ref[h * QKV:(h + 1) * QKV, :] += _dot(a_ref[h], b, 0, 0)

        @pl.when(k == nk - 1)
        def _():
            for h in range(hh):
                for t in range(3):
                    o_ref[t, h] = acc_ref[h * QKV + t * HD:h * QKV + (t + 1) * HD, :]

    in_specs = [pl.BlockSpec((hh, tk, QKV), lambda j, k: (j, k, 0)), pl.BlockSpec((tk, D), lambda j, k: (k % nkb, k // nkb))]
    args, aliases = [dqkv, xv], {}
    if into is not None:
        aliases = {2: 0}
        in_specs.append(pl.BlockSpec(memory_space=pl.ANY))
        args.append(into)
    return pl.pallas_call(
        body, name=name, grid=(2, nk), in_specs=in_specs,
        out_specs=pl.BlockSpec((3, hh, HD, D), lambda j, k: (g, j, 0, 0)), out_shape=jax.ShapeDtypeStruct(GW_IN_BLOCKS, F32),
        input_output_aliases=aliases, scratch_shapes=[pltpu.VMEM((hh * QKV, D), F32)],
        compiler_params=_cparams(("parallel", "arbitrary")),
    )(*args)


def _band(nb, first_valid, last_valid=None):
    b = lax.broadcasted_iota(jnp.int32, (nb, SPAN, SPAN), 0)
    row = lax.broadcasted_iota(jnp.int32, (nb, SPAN, SPAN), 1)
    col = lax.broadcasted_iota(jnp.int32, (nb, SPAN, SPAN), 2)
    off = jnp.where(b == 0, jnp.where(first_valid, 0, 2 * SPAN), 0)
    if last_valid is not None:
        off = off + jnp.where(b == nb - 1, jnp.where(last_valid, 0, 2 * SPAN), 0)
    return col <= row, col >= row + off


def _attn_tiles(S, dil):
    Sd = S // dil
    T = min(1024, Sd)
    hp = min(NH, max(1, (S // T) * NH // 16))
    return Sd, T, T // SPAN, Sd // T, hp


def _attn_fwd(qkv, dil, name):
    S = qkv.shape[1]
    Sd, T, nsub, nib, hp = _attn_tiles(S, dil)
    scale = HD ** -0.5

    def body(c_ref, p_ref, o_ref, l_ref):
        ib, hb = pl.program_id(1), pl.program_id(2)
        m_cur, m_prev = _band(nsub, ib > 0)
        lane = lax.broadcasted_iota(jnp.int32, (T, HD), 1)

        @pl.when(hb == 0)
        def _():
            l_ref[...] = jnp.zeros_like(l_ref)

        lses = l_ref[...]
        for hh in range(hp):
            blk, hal = c_ref[hh], p_ref[hh]
            q, k, v = blk[:, :HD], blk[:, HD:2 * HD], blk[:, 2 * HD:]
            if nsub > 1:
                kp = jnp.concatenate([hal[:, HD:2 * HD], k[:T - SPAN]], axis=0)
                vp = jnp.concatenate([hal[:, 2 * HD:], v[:T - SPAN]], axis=0)
            else:
                kp, vp = hal[:, HD:2 * HD], hal[:, 2 * HD:]
            q3, k3, v3, kp3, vp3 = (t.reshape(nsub, SPAN, HD) for t in (q, k, v, kp, vp))
            sc = jnp.where(m_cur, _bdot(q3, k3, 2, 2) * scale, NEG)
            sp = jnp.where(m_prev, _bdot(q3, kp3, 2, 2) * scale, NEG)
            m = jnp.maximum(jnp.max(sc, -1, keepdims=True), jnp.max(sp, -1, keepdims=True))
            pc, pp = jnp.exp(sc - m), jnp.exp(sp - m)
            den = jnp.sum(pc, -1, keepdims=True) + jnp.sum(pp, -1, keepdims=True)
            o = (_bdot(pc.astype(BF16), v3, 2, 1) + _bdot(pp.astype(BF16), vp3, 2, 1)) * (1.0 / den)
            o_ref[hh] = o.reshape(T, HD).astype(BF16)
            lses = jnp.where(lane == hb * hp + hh, (m + jnp.log(den)).reshape(T, 1), lses)
        l_ref[...] = lses

    cur = pl.BlockSpec((hp, T, QKV), lambda r, ib, h: (h, r * nib + ib, 0))
    prev = pl.BlockSpec((hp, SPAN, QKV), lambda r, ib, h: (h, r * (Sd // SPAN) + jnp.maximum(ib * nsub - 1, 0), 0))
    return pl.pallas_call(
        body, name=name, grid=(dil, nib, NH // hp), in_specs=[cur, prev],
        out_specs=[pl.BlockSpec((hp, T, HD), lambda r, ib, h: (h, r * nib + ib, 0)),
                   pl.BlockSpec((T, HD), lambda r, ib, h: (r * nib + ib, 0))],
        out_shape=[jax.ShapeDtypeStruct((NH, S, HD), BF16), jax.ShapeDtypeStruct((S, HD), F32)],
        compiler_params=_cparams(("parallel", "parallel", "arbitrary")),
    )(qkv, qkv)


def _attn_bwd(qkv, d_attn, lse, delta, dil, name):
    S = qkv.shape[1]
    Sd, T, nsub, nib, hp = _attn_tiles(S, dil)
    scale = HD ** -0.5
    ne = nsub + 1

    def body(c_ref, p_ref, n_ref, do_ref, don_ref, l_ref, ln_ref, dl_ref, dln_ref, o_ref):
        ib, hb = pl.program_id(1), pl.program_id(2)
        _, m_prev = _band(ne, ib > 0, ib < nib - 1)
        m_cur, _ = _band(nsub, True)
        for hh in range(hp):
            h = hb * hp + hh
            blk, hal, nxt = c_ref[hh], p_ref[hh], n_ref[hh]
            q, k, v = blk[:, :HD], blk[:, HD:2 * HD], blk[:, 2 * HD:]
            do = do_ref[hh]
            l, dl = _head_col(l_ref[...], h), _head_col(dl_ref[...], h)
            qe = jnp.concatenate([q, nxt[:, :HD]], axis=0).reshape(ne, SPAN, HD)
            doe = jnp.concatenate([do, don_ref[hh]], axis=0).reshape(ne, SPAN, HD)
            le = jnp.concatenate([l, _head_col(ln_ref[...], h)], axis=0).reshape(ne, SPAN, 1)
            dle = jnp.concatenate([dl, _head_col(dln_ref[...], h)], axis=0).reshape(ne, SPAN, 1)
            kpe = jnp.concatenate([hal[:, HD:2 * HD], k], axis=0).reshape(ne, SPAN, HD)
            vpe = jnp.concatenate([hal[:, 2 * HD:], v], axis=0).reshape(ne, SPAN, HD)
            p = jnp.where(m_prev, jnp.exp(_bdot(qe, kpe, 2, 2) * scale - le), 0.0)
            ds = (p * (_bdot(doe, vpe, 2, 2) - dle)).astype(BF16)
            dq = _bdot(ds, kpe, 2, 1)[:nsub]
            dk = _bdot(ds, qe, 1, 1)[1:]
            dv = _bdot(p.astype(BF16), doe, 1, 1)[1:]
            q3, k3, v3, do3 = (t.reshape(nsub, SPAN, HD) for t in (q, k, v, do))
            l3, dl3 = l.reshape(nsub, SPAN, 1), dl.reshape(nsub, SPAN, 1)
            p = jnp.where(m_cur, jnp.exp(_bdot(q3, k3, 2, 2) * scale - l3), 0.0)
            ds = (p * (_bdot(do3, v3, 2, 2) - dl3)).astype(BF16)
            dq = (dq + _bdot(ds, k3, 2, 1)) * scale
            dk = (dk + _bdot(ds, q3, 1, 1)) * scale
            dv = dv + _bdot(p.astype(BF16), do3, 1, 1)
            o_ref[hh] = jnp.concatenate([t.reshape(T, HD) for t in (dq, dk, dv)], axis=1).astype(BF16)

    nb = Sd // SPAN
    row = lambda r, ib: r * nib + ib
    prow = lambda r, ib: r * nb + jnp.maximum(ib * nsub - 1, 0)
    nrow = lambda r, ib: r * nb + jnp.minimum((ib + 1) * nsub, nb - 1)
    cur3 = pl.BlockSpec((hp, T, QKV), lambda r, ib, h: (h, row(r, ib), 0))
    prev3 = pl.BlockSpec((hp, SPAN, QKV), lambda r, ib, h: (h, prow(r, ib), 0))
    next3 = pl.BlockSpec((hp, SPAN, QKV), lambda r, ib, h: (h, nrow(r, ib), 0))
    cur1 = pl.BlockSpec((hp, T, HD), lambda r, ib, h: (h, row(r, ib), 0))
    next1 = pl.BlockSpec((hp, SPAN, HD), lambda r, ib, h: (h, nrow(r, ib), 0))
    curc = pl.BlockSpec((T, HD), lambda r, ib, h: (row(r, ib), 0))
    nextc = pl.BlockSpec((SPAN, HD), lambda r, ib, h: (nrow(r, ib), 0))
    return pl.pallas_call(
        body, name=name, grid=(dil, nib, NH // hp),
        in_specs=[cur3, prev3, next3, cur1, next1, curc, nextc, curc, nextc], out_specs=cur3,
        out_shape=jax.ShapeDtypeStruct((NH, S, QKV), BF16),
        compiler_params=_cparams(("parallel", "parallel", "parallel")),
    )(qkv, qkv, qkv, d_attn, d_attn, lse, lse, delta, delta)


def _ret_consts():
    lg = np.log1p(-np.exp2(-5.0 - np.arange(RH, dtype=np.float64)))
    idx = np.arange(CH, dtype=np.float64)
    rel = idx[:, None] - idx[None, :]
    intra = np.where(rel >= 0, np.exp(lg[:, None, None] * np.maximum(rel, 0.0)), 0.0)
    qd = np.exp(lg[:, None] * (idx + 1.0))
    kd = np.exp(lg[:, None] * (CH - 1.0 - idx))
    cd = np.exp(lg * CH)
    wide = lambda t: np.broadcast_to(t[:, :, None], (RH, t.shape[1], RDV))
    return (jnp.asarray(intra, F32), jnp.asarray(wide(qd), F32), jnp.asarray(wide(kd), F32),
            jnp.asarray(np.broadcast_to(cd[:, None, None], (RH, 1, RDV)), F32))


def _rot(t, c, s):
    t1, t2 = t[:, :RDK // 2], t[:, RDK // 2:]
    return jnp.concatenate([t1 * c - t2 * s, t1 * s + t2 * c], axis=1)


def _unrot(d, c, s):
    d1, d2 = d[:, :RDK // 2], d[:, RDK // 2:]
    return jnp.concatenate([d1 * c + d2 * s, d2 * c - d1 * s], axis=1)


RCH = 4


def _ret_specs(nmap):
    rows = RCH * CH
    q = pl.BlockSpec((rows, RH * RDK), lambda n: (nmap(n), OFF_RQ // (RH * RDK)))
    k = pl.BlockSpec((rows, RH * RDK), lambda n: (nmap(n), OFF_RK // (RH * RDK)))
    v = pl.BlockSpec((rows, RH * RDV), lambda n: (nmap(n), OFF_RV // (RH * RDV)))
    cs = pl.BlockSpec((rows, RDK // 2), lambda n: (nmap(n), 0))
    dmat = pl.BlockSpec((RH, CH, CH), lambda n: (0, 0, 0))
    dvec = pl.BlockSpec((RH, CH, RDV), lambda n: (0, 0, 0))
    cdv = pl.BlockSpec((RH, 1, RDV), lambda n: (0, 0, 0))
    state = pl.BlockSpec((RH, RCH, RDK, RDV), lambda n: (0, nmap(n), 0, 0))
    out = pl.BlockSpec((rows, RH * RDV), lambda n: (nmap(n), 0))
    return [q, k, v, cs, cs, dmat, dvec, dvec, cdv], state, out


def _ret_fwd(proj, cos, sin, consts):
    S = proj.shape[0]
    nc = S // CH

    def body(q_ref, k_ref, v_ref, c_ref, s_ref, d_ref, qd_ref, kd_ref, cd_ref, o_ref, st_ref, state):
        @pl.when(pl.program_id(0) == 0)
        def _():
            state[...] = jnp.zeros_like(state)

        for ci in range(RCH):
            rows = slice(ci * CH, (ci + 1) * CH)
            c, s = c_ref[rows, :], s_ref[rows, :]
            for h in range(RH):
                qk, vv = slice(h * RDK, (h + 1) * RDK), slice(h * RDV, (h + 1) * RDV)
                qb = _rot(q_ref[rows, qk].astype(F32), c, s).astype(BF16)
                kb = (_rot(k_ref[rows, qk].astype(F32), c, s) * (RDK ** -0.5)).astype(BF16)
                vb = v_ref[rows, vv]
                sb = state[h].astype(BF16)
                st_ref[h, ci] = sb
                a = (_dot(qb, kb, 1, 1) * d_ref[h]).astype(BF16)
                o_ref[rows, vv] = (_dot(a, vb, 1, 0) + _dot(qb, sb, 1, 0) * qd_ref[h]).astype(BF16)
                vk = (vb.astype(F32) * kd_ref[h]).astype(BF16)
                state[h] = cd_ref[h] * state[h] + _dot(kb, vk, 0, 0)

    ins, state_spec, out_spec = _ret_specs(lambda n: n)
    return pl.pallas_call(
        body, name="ret_fwd", grid=(nc // RCH,), in_specs=ins, out_specs=[out_spec, state_spec],
        out_shape=[jax.ShapeDtypeStruct((S, RH * RDV), BF16), jax.ShapeDtypeStruct((RH, nc, RDK, RDV), BF16)],
        scratch_shapes=[pltpu.VMEM((RH, RDK, RDV), F32)],
        compiler_params=_cparams(("arbitrary",)),
    )(proj, proj, proj, cos, sin, *consts)


def _ret_bwd(proj, cos, sin, consts, states, d_ret, d_rest):
    S = proj.shape[0]
    nc = S // CH

    def body(q_ref, k_ref, v_ref, c_ref, s_ref, d_ref, qd_ref, kd_ref, cd_ref, st_ref, do_ref, _, o_ref, dstate):
        @pl.when(pl.program_id(0) == 0)
        def _():
            dstate[...] = jnp.zeros_like(dstate)

        for ci in reversed(range(RCH)):
            rows = slice(ci * CH, (ci + 1) * CH)
            c, s = c_ref[rows, :], s_ref[rows, :]
            for h in range(RH):
                qk, vv = slice(h * RDK, (h + 1) * RDK), slice(h * RDV, (h + 1) * RDV)
                qb = _rot(q_ref[rows, qk].astype(F32), c, s).astype(BF16)
                kb = (_rot(k_ref[rows, qk].astype(F32), c, s) * (RDK ** -0.5)).astype(BF16)
                vb, sb, do = v_ref[rows, vv], st_ref[h, ci], do_ref[rows, vv]
                dmat, qd, kd = d_ref[h], qd_ref[h], kd_ref[h]
                a = (_dot(qb, kb, 1, 1) * dmat).astype(BF16)
                doq = (do.astype(F32) * qd).astype(BF16)
                dsb = dstate[h].astype(BF16)
                vk = (vb.astype(F32) * kd).astype(BF16)
                o_ref[rows, OFF_RV + h * RDV:OFF_RV + (h + 1) * RDV] = (_dot(a, do, 0, 0) + _dot(kb, dsb, 1, 0) * kd).astype(BF16)
                da = (_dot(do, vb, 1, 1) * dmat).astype(BF16)
                dq = _dot(da, kb, 1, 0) + _dot(doq, sb, 1, 1)
                dk = (_dot(da, qb, 0, 0) + _dot(vk, dsb, 1, 1)) * (RDK ** -0.5)
                o_ref[rows, OFF_RQ + h * RDK:OFF_RQ + (h + 1) * RDK] = _unrot(dq, c, s).astype(BF16)
                o_ref[rows, OFF_RK + h * RDK:OFF_RK + (h + 1) * RDK] = _unrot(dk, c, s).astype(BF16)
                dstate[h] = cd_ref[h] * dstate[h] + _dot(qb, doq, 0, 0)

    nsteps = nc // RCH
    rev = lambda n: nsteps - 1 - n
    ins, state_spec, out_spec = _ret_specs(rev)
    return pl.pallas_call(
        body, name="ret_bwd", grid=(nsteps,), in_specs=ins + [state_spec, out_spec, pl.BlockSpec(memory_space=pl.ANY)],
        out_specs=pl.BlockSpec((RCH * CH, OFF_RG), lambda n: (rev(n), 0)),
        out_shape=jax.ShapeDtypeStruct(d_rest.shape, BF16), input_output_aliases={11: 0},
        scratch_shapes=[pltpu.VMEM((RH, RDK, RDV), F32)],
        compiler_params=_cparams(("arbitrary",)),
    )(proj, proj, proj, cos, sin, *consts, states, d_ret, d_rest)


CW = 256
HALO = 16


def _shift_down(v, halo, k):
    rolled = pltpu.roll(v, k, 0)
    hr = pltpu.roll(halo, k, 0)[0:8]
    row = lax.broadcasted_iota(jnp.int32, hr.shape, 0)
    return jnp.concatenate([jnp.where(row < k, hr, rolled[0:8]), rolled[8:]], axis=0)


def _shift_up(v, halo, k):
    T = v.shape[0]
    rolled = pltpu.roll(v, T - k, 0)
    hr = pltpu.roll(halo, 8 - k, 0)[0:8]
    row = lax.broadcasted_iota(jnp.int32, hr.shape, 0)
    return jnp.concatenate([rolled[:T - 8], jnp.where(row >= 8 - k, hr, rolled[T - 8:])], axis=0)


def _conv_taps(h_ref, hp_ref, first):
    h = h_ref[...].astype(F32)
    hp = hp_ref[...].astype(F32) * jnp.where(first, 0.0, 1.0)
    return _shift_down(h, hp, 2), _shift_down(h, hp, 1), h


def _conv_specs(S, T, cw=CW):
    nj = DFF // cw
    cur = pl.BlockSpec((T, cw), lambda j, i: (i, j))
    prev = pl.BlockSpec((HALO, cw), lambda j, i: (jnp.maximum(i * (T // HALO) - 1, 0), j))
    nxt = pl.BlockSpec((HALO, cw), lambda j, i: (jnp.minimum((i + 1) * (T // HALO), S // HALO - 1), j))
    w = pl.BlockSpec((3, cw), lambda j, i: (0, j))
    b = pl.BlockSpec((1, cw), lambda j, i: (0, j))
    return nj, cur, prev, nxt, w, b


def _conv_fwd(hg, hu, wg, wu, bg, bu):
    S = hg.shape[0]
    T = min(1024, S)
    nj, cur, prev, _, w, b = _conv_specs(S, T)

    def body(hg_ref, hu_ref, hgp_ref, hup_ref, wg_ref, wu_ref, bg_ref, bu_ref, o_ref):
        first = pl.program_id(1) == 0
        g2, g1, g0 = _conv_taps(hg_ref, hgp_ref, first)
        u2, u1, u0 = _conv_taps(hu_ref, hup_ref, first)
        cg = wg_ref[0:1, :] * g2 + wg_ref[1:2, :] * g1 + wg_ref[2:3, :] * g0 + bg_ref[...]
        cu = wu_ref[0:1, :] * u2 + wu_ref[1:2, :] * u1 + wu_ref[2:3, :] * u0 + bu_ref[...]
        o_ref[...] = (_gelu(cg) * cu).astype(BF16)

    return pl.pallas_call(
        body, name="conv_fwd", grid=(nj, S // T), in_specs=[cur, cur, prev, prev, w, w, b, b], out_specs=cur,
        out_shape=jax.ShapeDtypeStruct((S, DFF), BF16), compiler_params=_cparams(("parallel", "parallel")),
    )(hg, hu, hg, hu, wg, wu, bg, bu)


def _conv_bwd_pre(d_act, hg, hu, wg, wu, bg, bu):
    S = hg.shape[0]
    T = min(1024, S)
    nj, cur, prev, _, w, b = _conv_specs(S, T)

    def body(da_ref, hg_ref, hu_ref, hgp_ref, hup_ref, wg_ref, wu_ref, bg_ref, bu_ref,
             dcg_ref, dcu_ref, gwg_ref, gwu_ref, gbg_ref, gbu_ref):
        first = pl.program_id(1) == 0
        g2, g1, g0 = _conv_taps(hg_ref, hgp_ref, first)
        u2, u1, u0 = _conv_taps(hu_ref, hup_ref, first)
        cg = wg_ref[0:1, :] * g2 + wg_ref[1:2, :] * g1 + wg_ref[2:3, :] * g0 + bg_ref[...]
        cu = wu_ref[0:1, :] * u2 + wu_ref[1:2, :] * u1 + wu_ref[2:3, :] * u0 + bu_ref[...]
        da = da_ref[...].astype(F32)
        gl, dgl = _gelu(cg, with_grad=True)
        dcg = da * cu * dgl
        dcu = da * gl
        dcg_ref[...] = dcg.astype(BF16)
        dcu_ref[...] = dcu.astype(BF16)

        @pl.when(first)
        def _():
            for r in (gwg_ref, gwu_ref, gbg_ref, gbu_ref):
                r[...] = jnp.zeros_like(r)

        for r, d, taps in ((gwg_ref, dcg, (g2, g1, g0)), (gwu_ref, dcu, (u2, u1, u0))):
            for j in range(3):
                r[j:j + 1, :] += jnp.sum(d * taps[j], 0, keepdims=True)
        gbg_ref[...] += jnp.sum(dcg, 0, keepdims=True)
        gbu_ref[...] += jnp.sum(dcu, 0, keepdims=True)

    return pl.pallas_call(
        body, name="conv_bwd_pre", grid=(nj, S // T), in_specs=[cur, cur, cur, prev, prev, w, w, b, b],
        out_specs=[cur, cur, w, w, b, b],
        out_shape=[jax.ShapeDtypeStruct((S, DFF), BF16)] * 2 + [jax.ShapeDtypeStruct((3, DFF), F32)] * 2
        + [jax.ShapeDtypeStruct((1, DFF), F32)] * 2,
        compiler_params=_cparams(("parallel", "arbitrary")),
    )(d_act, hg, hu, hg, hu, wg, wu, bg, bu)


def _conv_bwd_in(dc, w, name):
    S = dc.shape[0]
    T = min(512, S)
    nj, cur, _, nxt, wspec, _ = _conv_specs(S, T, DFF // 2)
    nt = S // T

    def body(dc_ref, dn_ref, w_ref, o_ref):
        d = dc_ref[...].astype(F32)
        dn = dn_ref[...].astype(F32) * jnp.where(pl.program_id(1) == nt - 1, 0.0, 1.0)
        o_ref[...] = (w_ref[2:3, :] * d + w_ref[1:2, :] * _shift_up(d, dn, 1) + w_ref[0:1, :] * _shift_up(d, dn, 2)).astype(BF16)

    return pl.pallas_call(
        body, name=name, grid=(nj, nt), in_specs=[cur, nxt, wspec], out_specs=cur,
        out_shape=jax.ShapeDtypeStruct((S, DFF), BF16), compiler_params=_cparams(("parallel", "parallel")),
    )(dc, dc, w)


def _adam_math(g, w, m, v):
    m = B1 * m + (1.0 - B1) * g
    v = B2 * v + (1.0 - B2) * (g * g)
    m_hat = m / (1.0 - B1 ** STEP)
    v_hat = v / (1.0 - B2 ** STEP)
    return -LR * (m_hat / (jnp.sqrt(v_hat) + EPS) + WD * w), m, v


def _reduce_tail(chip32, far, chip, name, wmv=None):
    L = len(chip32)
    _, R, C = chip32[0].shape
    tr = _tile(R, 256, 16)
    nr = R // tr

    def body(chip_ref, *refs):
        own_refs, far_refs, rest = refs[:L], refs[L:2 * L], refs[2 * L:]
        outs = rest[3:] if wmv else rest
        for ll in range(L):
            @pl.when(pl.program_id(0) == ll)
            def _(ll=ll):
                g = own_refs[ll][...]
                for s in range(3):
                    g = g + far_refs[ll][s].astype(F32)
                outs[0][...] = g
                if wmv:
                    outs[1][...], outs[2][...], outs[3][...] = _adam_math(g, rest[0][...], rest[1][...], rest[2][...])

    def rows(ll):
        return lambda l, i: jnp.where(l == ll, i, jnp.where(l < ll, 0, nr - 1))

    blk = pl.BlockSpec((None, tr, C), lambda l, i, ch: (l, i, 0))
    in_specs = [pl.BlockSpec((None, tr, C), lambda l, i, ch, f=rows(ll): (ch[0], f(l, i), 0)) for ll in range(L)]
    in_specs += [pl.BlockSpec((3, tr, C), lambda l, i, ch, f=rows(ll): (0, f(l, i), 0)) for ll in range(L)]
    args = list(chip32) + list(far)
    n_out = 1
    if wmv:
        in_specs += [blk] * 3
        args += list(wmv)
        n_out = 4
    return pl.pallas_call(
        body, name=name,
        grid_spec=pltpu.PrefetchScalarGridSpec(num_scalar_prefetch=1, grid=(L, nr), in_specs=in_specs, out_specs=[blk] * n_out),
        out_shape=[jax.ShapeDtypeStruct((L, R, C), F32)] * n_out, compiler_params=_cparams(("arbitrary", "arbitrary")),
    )(chip, *args)


def _adamw(g, w, m, v, name):
    R, C = g.shape
    tr = _tile(R, 128, 8)

    def body(g_ref, w_ref, m_ref, v_ref, d_ref, nm_ref, nv_ref):
        d_ref[...], nm_ref[...], nv_ref[...] = _adam_math(g_ref[...], w_ref[...], m_ref[...], v_ref[...])

    blk = pl.BlockSpec((tr, C), lambda i: (i, 0))
    return pl.pallas_call(
        body, name=name, grid=(R // tr,), in_specs=[blk] * 4, out_specs=[blk] * 3,
        out_shape=[jax.ShapeDtypeStruct(g.shape, F32)] * 3, compiler_params=_cparams(("parallel",)),
    )(g, w, m, v)


def _pair_sum(x, recv, core, name):
    _, R, C = x.shape
    tr = _tile(R, 600, 16)

    def body(core_ref, x_ref, r_ref, o32_ref, o16_ref):
        s = x_ref[...] + r_ref[...]
        o32_ref[...] = s
        o16_ref[...] = s.astype(BF16)

    blk = pl.BlockSpec((None, tr, C), lambda q, i, c: (q, i, 0))
    mine = pl.BlockSpec((None, None, tr, C), lambda q, i, c: (q, c[0], i, 0))
    return pl.pallas_call(
        body, name=name,
        grid_spec=pltpu.PrefetchScalarGridSpec(num_scalar_prefetch=1, grid=(4, R // tr), in_specs=[mine, blk], out_specs=[blk, blk]),
        out_shape=[jax.ShapeDtypeStruct((4, R, C), F32), jax.ShapeDtypeStruct((4, R, C), BF16)],
        compiler_params=_cparams(("parallel", "parallel")),
    )(core, x.reshape(4, 2, R, C), recv)


def _sum_slots(x, name):
    def body(x_ref, o_ref):
        g = x_ref[0]
        for s in range(1, x.shape[0]):
            g = g + x_ref[s]
        o_ref[...] = g

    return pl.pallas_call(body, name=name, out_shape=jax.ShapeDtypeStruct(x.shape[1:], F32))(x)


MESH = pl.DeviceIdType.MESH
_HBM = pl.BlockSpec(memory_space=pltpu.HBM)


def _dma_sems(n):
    return pltpu.SemaphoreType.DMA((n,))


def _gather_many(xs, name):
    n = len(xs)

    def body(*refs):
        x_refs, out_refs = refs[:n], refs[n:2 * n]
        send_sems, recv_sems, local_sems = refs[2 * n:]
        ax, ay, ac = lax.axis_index("x"), lax.axis_index("y"), lax.axis_index("c")
        me, sibling = (ax, ay, ac), (ax, ay, 1 - ac)
        chips = [(1 - ax, ay), (ax, 1 - ay), (1 - ax, 1 - ay)]

        def copy(a, k, block, to, own=False):
            slot = out_refs[a].at[4 * block[0] + 2 * block[1] + block[2]]
            return pltpu.make_async_remote_copy(
                src_ref=x_refs[a] if own else slot, dst_ref=slot, send_sem=send_sems.at[7 * a + k],
                recv_sem=recv_sems.at[7 * a + k], device_id=to, device_id_type=MESH)

        mine = [pltpu.make_async_copy(x_refs[a], out_refs[a].at[4 * ax + 2 * ay + ac], local_sems.at[a]) for a in range(n)]
        first = [copy(a, 0, me, sibling, own=True) for a in range(n)]
        first += [copy(a, 1 + j, me, (*chip, ac), own=True) for j, chip in enumerate(chips) for a in range(n)]
        for cp in mine + first:
            cp.start()
        passed = []
        for j, chip in enumerate(chips):
            for a in range(n):
                copy(a, 1 + j, (*chip, ac), me).wait_recv()
                cp = copy(a, 4 + j, (*chip, ac), sibling)
                cp.start()
                passed.append(cp)
        for a in range(n):
            copy(a, 0, sibling, me).wait_recv()
            for j, chip in enumerate(chips):
                copy(a, 4 + j, (*chip, 1 - ac), me).wait_recv()
        for cp in first + passed:
            cp.wait_send()
        for cp in mine:
            cp.wait()

    return pl.pallas_call(
        body, name=name, out_shape=[jax.ShapeDtypeStruct((N_DEV,) + x.shape, x.dtype) for x in xs],
        in_specs=[_HBM] * n, out_specs=[_HBM] * n, scratch_shapes=[_dma_sems(7 * n), _dma_sems(7 * n), _dma_sems(n)],
    )(*xs)


_SEM = pl.BlockSpec(memory_space=pltpu.SEMAPHORE)
_EFFECT = pltpu.SideEffectType.DATAFLOW_SIDE_EFFECTING


def _peer(k):
    ax, ay, ac = lax.axis_index("x"), lax.axis_index("y"), lax.axis_index("c")
    px = 1 - ax if k & 4 else ax
    py = 1 - ay if k & 2 else ay
    pc = 1 - ac if k & 1 else ac
    return (px, py, pc), 4 * px + 2 * py + pc


def _build_gather(x_refs, land_refs, send_sems, recv_sems, waiting):
    _, me = _peer(0)
    copies = []
    for a in range(len(x_refs)):
        for k in range(1, N_DEV):
            peer, slot = _peer(k)
            copies.append(pltpu.make_async_remote_copy(
                src_ref=x_refs[a], dst_ref=land_refs[a].at[slot if waiting else me], send_sem=send_sems.at[7 * a + k - 1],
                recv_sem=recv_sems.at[7 * a + k - 1], device_id=peer, device_id_type=MESH))
    return copies


def _build_cores(x_refs, land_refs, send_sems, recv_sems, waiting):
    ax, ay, ac = lax.axis_index("x"), lax.axis_index("y"), lax.axis_index("c")
    copies = []
    for a in range(len(x_refs)):
        for q in range(4):
            copies.append(pltpu.make_async_remote_copy(
                src_ref=x_refs[a].at[2 * q + 1 - ac], dst_ref=land_refs[a].at[q], send_sem=send_sems.at[4 * a + q],
                recv_sem=recv_sems.at[4 * a + q], device_id=(ax, ay, 1 - ac), device_id_type=MESH))
    return copies


def _build_chips(p_refs, land_refs, send_sems, recv_sems, waiting):
    ax, ay, ac = lax.axis_index("x"), lax.axis_index("y"), lax.axis_index("c")
    copies = []
    for a in range(len(p_refs)):
        for k in range(1, 4):
            px = 1 - ax if k & 2 else ax
            py = 1 - ay if k & 1 else ay
            copies.append(pltpu.make_async_remote_copy(
                src_ref=p_refs[a].at[2 * px + py], dst_ref=land_refs[a].at[k - 1], send_sem=send_sems.at[3 * a + k - 1],
                recv_sem=recv_sems.at[3 * a + k - 1], device_id=(px, py, ac), device_id_type=MESH))
    return copies


_EXCHANGES = {"gather": (_build_gather, 7, N_DEV), "cores": (_build_cores, 4, 4), "chips": (_build_chips, 3, 3)}


def _exchange_start(kind, xs, lands, name, after=None):
    build, per, _ = _EXCHANGES[kind]
    n = len(xs)

    def body(*refs):
        for cp in build(refs[:n], refs[n:2 * n], refs[-2 * n - 3], refs[-2 * n - 2], False):
            cp.start()
        refs[-1][...] = jnp.zeros_like(refs[-1])

    hbm = lambda t: pltpu.HBM(t.shape, t.dtype)
    args = [pltpu.with_memory_space_constraint(t, pltpu.HBM) for t in list(xs) + list(lands)]
    in_specs = [_HBM] * (2 * n)
    if after is not None:
        args.append(after)
        in_specs.append(pl.BlockSpec(memory_space=pl.ANY))
    outs = pl.pallas_call(
        body, name=name,
        out_shape=(_dma_sems(per * n), _dma_sems(per * n), *[hbm(t) for t in xs], *[hbm(t) for t in lands],
                   jax.ShapeDtypeStruct((8, 128), F32)),
        in_specs=in_specs, out_specs=(_SEM, _SEM, *[_HBM] * (2 * n), pl.BlockSpec(memory_space=pltpu.VMEM)),
        input_output_aliases={a: 2 + a for a in range(2 * n)},
        compiler_params=pltpu.CompilerParams(has_side_effects=_EFFECT),
    )(*args)
    return (kind, outs[0], outs[1], outs[2:2 + n], outs[2 + n:2 + 2 * n]), outs[-1]


def _exchange_wait(flight, after, name):
    kind, send_sems, recv_sems, xs, lands = flight
    build = _EXCHANGES[kind][0]
    n = len(xs)

    def body(*refs):
        for cp in build(refs[:n], refs[n:2 * n], refs[2 * n], refs[2 * n + 1], True):
            cp.wait_send()
            cp.wait_recv()

    hbm = lambda t: pltpu.HBM(t.shape, t.dtype)
    outs = pl.pallas_call(
        body, name=name, out_shape=(*[hbm(t) for t in xs], *[hbm(t) for t in lands]),
        in_specs=[_HBM] * (2 * n) + [_SEM, _SEM, pl.BlockSpec(memory_space=pl.ANY)], out_specs=[_HBM] * (2 * n),
        input_output_aliases={a: a for a in range(2 * n)}, compiler_params=pltpu.CompilerParams(has_side_effects=_EFFECT),
    )(*xs, *lands, send_sems, recv_sems, after)
    return outs[:n], outs[n:]


def _bf16_views(fn, dilations):
    extra = [(D, BF16) if d == 1 else ("dilated", D, BF16, d) for d in dilations]

    def wrapped(*a):
        res = fn(*a)
        res = tuple(res) if isinstance(res, (tuple, list)) else (res,)
        return res + (res[-1],) * len(extra)

    return extra, wrapped


def _layer_fwd(x, xb, p, w, cos, sin, rconsts, late=None, views_out=True):
    proj = _mm(xb[1], w["win"], tb=True, b_rows=(N_ATT, N_REST), name="mm_proj", out_dtype=BF16)
    qkvs, ogs, lgs = [], [], []
    for g, dil in enumerate(DILATIONS):
        qkv = _qkv_fwd(xb[dil], w["win"], g, dil, f"mm_qkv{g}")
        o, l = _attn_fwd(qkv, dil, f"attn_fwd_g{g}")
        qkvs.append(qkv)
        ogs.append(_to_tokens(o, dil))
        lgs.append(_to_tokens(l, dil))
    attn, lse = _rowwise(_f_combine, ogs + lgs, [], [(D, BF16), (HD, F32)], [], name="attn_combine")
    ret_raw, states = _ret_fwd(proj, cos, sin, rconsts)
    rg_win = (proj, RH * RDV, OFF_RG // (RH * RDV))
    ga_win, gr_win = (proj, D, OFF_GA // D), (proj, D, OFF_GR // D)
    (r,) = _rowwise(_f_gn, [ret_raw, rg_win], [w["ret_gn_g"], w["ret_gn_b"]], [(RH * RDV, BF16)], [], name="gn_fwd", tm=256)
    if late is not None:
        w = {**w, **late(r)}
    ap = _mm(attn, w["w_attn_proj"], name="mm_attn_proj", out_dtype=BF16)
    rp = _mm(r, w["w_ret_proj"], name="mm_ret_proj", out_dtype=BF16, tk=2048)
    (merged,) = _rowwise(_f_gate, [ap, rp, ga_win, gr_win], [], [(D, BF16)], [], name="gate_fwd")
    h1, x1, x1b = _mm_norm(merged, w["w_out"], x, w["ln1_g"], w["ln1_b"], "mm_out_ln1")
    z = _mm(x1b, w["w_ple_gate"], name="mm_ple_gate")
    pp = _mm(p, w["w_ple_proj"], tb=True, name="mm_ple_proj")
    hg = _mm(x1b, w["w_up"], tb=True, b_rows=(0, DFF), name="mm_up_g", out_dtype=BF16, tm=512, tn=DFF)
    hu = _mm(x1b, w["w_up"], tb=True, b_rows=(DFF, DFF), name="mm_up_u", out_dtype=BF16, tm=512, tn=DFF)
    act = _conv_fwd(hg, hu, w["conv_wg"], w["conv_wu"], w["conv_bg"], w["conv_bu"])
    ffn = _mm(act, w["w_down"], name="mm_down", tm=512, tk=DFF)
    dils = DILATIONS if views_out else (1,)
    extra, fn = _bf16_views(_f_ln2, dils)
    h2, x2, *x2b = _rowwise(fn, [x1, ffn, z, pp], [w["ln2_g"], w["ln2_b"]], [(D, F32), (D, F32)] + extra, [], name="ln2_fwd")
    x2b = dict(zip(dils, x2b))
    saved = dict(xb=xb, proj=proj, qkvs=qkvs, attn=attn, lse=lse, ret_raw=ret_raw, states=states, r=r, ap=ap, rp=rp,
                 merged=merged, h1=h1, x1b=x1b, z=z, pp=pp, hg=hg, hu=hu, act=act, h2=h2, p=p)
    return x2, x2b, saved, w


def _after(fn, token):
    return fn if token is None else (lambda *a: fn(*a[:-1]))


def _layer_bwd(dys, w, sv, cos, sin, rconsts, hooks):
    gr = {}
    proj = sv["proj"]
    call = lambda key, *a: hooks[key](*a) if key in hooks else None
    held = lambda token: [] if token is None else [token]
    token = hooks.get("token")
    dh2, dh2b, dpp, dz, gr["ln2_g"], gr["ln2_b"] = _rowwise(
        _after(_f_ln2_bwd, token), list(dys) + [sv["h2"], sv["z"], sv["pp"]], [w["ln2_g"]] + held(token),
        [(D, F32), (D, BF16), (D, BF16), (D, BF16)], [(1, D), (1, D)], name="ln2_bwd")
    d_act = _mm(dh2b, w["w_down"], tb=True, name="mm_down_dx", out_dtype=BF16, tm=512, tn=DFF)
    gr["w_down"] = _mm(sv["act"], dh2b, ta=True, name="mm_down_dw", tm=DFF // 2)
    dcg, dcu, gwg, gwu, gbg, gbu = _conv_bwd_pre(d_act, sv["hg"], sv["hu"], w["conv_wg"], w["conv_wu"], w["conv_bg"], w["conv_bu"])
    token = call("after_ffn", dcg)
    gr["conv_w"] = jnp.concatenate([gwg, gwu], axis=1)
    gr["conv_b"] = jnp.concatenate([gbg, gbu], axis=1)
    dhg = _conv_bwd_in(dcg, w["conv_wg"], "conv_bwd_in_g")
    dhu = _conv_bwd_in(dcu, w["conv_wu"], "conv_bwd_in_u")
    gw_up = _mm(dhg, sv["x1b"], ta=True, name="mm_up_g_dw", tm=DFF // 2, out_rows=(0, 2 * DFF))
    gr["w_up"] = _mm(dhu, sv["x1b"], ta=True, name="mm_up_u_dw", tm=DFF // 2, out_rows=(DFF, 2 * DFF), into=gw_up)
    dx1 = _mm(dhg, w["w_up"], b_rows=(0, DFF), name="mm_up_g_dx", add=dh2, add_scale=ALPHA, tm=512, tk=DFF)
    dx1 = _mm(dhu, w["w_up"], b_rows=(DFF, DFF), name="mm_up_u_dx", add=dx1, tm=512, tk=DFF)
    gr["w_ple_proj"] = _mm(dpp, sv["p"], ta=True, name="mm_ple_proj_dw")
    gr["w_ple_gate"] = _mm(sv["x1b"], dz, ta=True, name="mm_ple_gate_dw")
    dx1 = _mm(dz, w["w_ple_gate"], tb=True, name="mm_ple_gate_dx", add=dx1)
    dh1, dh1b, gr["ln1_g"], gr["ln1_b"] = _rowwise(_after(_f_ln_bwd, token), [dx1, sv["h1"]], [w["ln1_g"]] + held(token),
                                                   [(D, F32), (D, BF16)], [(1, D), (1, D)], name="ln1_bwd")
    d_merged = _mm(dh1b, w["w_out"], tb=True, name="mm_out_dx", out_dtype=BF16)
    gr["w_out"] = _mm(sv["merged"], dh1b, ta=True, name="mm_out_dw")
    rg_win = (proj, RH * RDV, OFF_RG // (RH * RDV))
    ga_win, gr_win = (proj, D, OFF_GA // D), (proj, D, OFF_GR // D)
    dap, drp, d_rest = _rowwise(_f_gate_bwd, [d_merged, sv["ap"], sv["rp"], ga_win, gr_win], [],
                                [(D, BF16), (D, BF16), (2 * D, BF16, N_REST, OFF_GA // (2 * D), None)], [], name="gate_bwd")
    d_attn = _mm(dap, w["w_attn_proj"], tb=True, name="mm_attn_proj_dx", out_dtype=BF16)
    gr["w_attn_proj"] = _mm(sv["attn"], dap, ta=True, name="mm_attn_proj_dw")
    d_r = _mm(drp, w["w_ret_proj"], tb=True, name="mm_ret_proj_dx", out_dtype=BF16, tn=2048)
    gr["w_ret_proj"] = _mm(sv["r"], drp, ta=True, name="mm_ret_proj_dw", tm=2048)
    token = call("early_grads", gr)
    d_ret, d_rest, gr["ret_gn_g"], gr["ret_gn_b"] = _rowwise(
        _after(_f_gn_bwd, token), [d_r, sv["ret_raw"], rg_win], [w["ret_gn_g"], w["ret_gn_b"]] + held(token),
        [(RH * RDV, BF16), (RH * RDV, BF16, N_REST, OFF_RG // (RH * RDV), d_rest)],
        [(1, RH * RDV), (1, RH * RDV)], name="gn_bwd", tm=256)
    d_rest = _ret_bwd(proj, cos, sin, rconsts, sv["states"], d_ret, d_rest)
    token = call("after_ret", d_rest)
    (delta,) = _rowwise(_after(_f_delta, token), [d_attn, sv["attn"]], held(token), [(HD, F32)], [], name="attn_delta")
    gw_in, dqkvs = None, []
    for g, dil in enumerate(DILATIONS):
        dqkvs.append(_attn_bwd(sv["qkvs"][g], _to_head_residues(d_attn, dil), _to_residues(sv["lse"], dil),
                               _to_residues(delta, dil), dil, f"attn_bwd_g{g}"))
        gw_in = _qkv_dw(dqkvs[g], sv["xb"][dil], g, dil, f"mm_qkv{g}_dw", into=gw_in)
    gw_in = _mm(d_rest, sv["xb"][1], ta=True, name="mm_proj_dw", out_rows=(N_ATT, N_IN), into=gw_in, blocks8=True)
    gr["w_in"] = gw_in.reshape(N_IN, D)
    token = call("w_in_ready", gr["w_in"])
    dx0 = _mm(d_rest, w["win"], b_rows=(N_ATT, N_REST), name="mm_proj_dx", add=dh1, add_scale=ALPHA, after=token)
    dx_parts = []
    for g, dil in enumerate(DILATIONS):
        if dil == 1:
            dx0 = _qkv_dx(dqkvs[g], w["win"], g, dil, f"mm_qkv{g}_dx", F32, add=dx0)
            token = call("after_dx0", dx0)
        else:
            dx_parts.append(("dilated", _qkv_dx(dqkvs[g], w["win"], g, dil, f"mm_qkv{g}_dx", BF16, after=token), dil))
    return [dx0] + dx_parts, gr


def _local_step(x, p, positions, target, ws, own_hooks=None, on_grads=None, token=None):
    half = RDK // 2
    freq = jnp.power(ROPE_BASE, -jnp.arange(half, dtype=F32) / half)
    ang = positions.astype(F32)[:, None] * freq[None, :]
    cos, sin = jnp.cos(ang), jnp.sin(ang)
    rconsts = _ret_consts()
    extra, _ = _bf16_views(None, DILATIONS)
    held = [] if token is None else [token]
    xb = dict(zip(DILATIONS, _rowwise(_after(lambda v: (v,) * len(extra), token), [x], held, extra, [], name="cast_x")))
    saved, ws = [], list(ws)
    for l in range(DEPTH):
        first, late = ws[l] if isinstance(ws[l], tuple) else (ws[l], None)
        if callable(first):
            first = first(x)
        x, xb, sv, ws[l] = _layer_fwd(x, xb, p[l], first, cos, sin, rconsts, late, views_out=l < DEPTH - 1)
        saved.append(sv)
    dy, loss_vec = _rowwise(_f_loss, [x, target], [], [(D, F32)], [(1, D)], name="loss")
    dys, grads = [dy], [None] * DEPTH
    from_above = {}
    for l in reversed(range(DEPTH)):
        hooks = {**from_above, **(own_hooks(l) if own_hooks else {})}
        dys, grads[l] = _layer_bwd(dys, ws[l], saved[l], cos, sin, rconsts, hooks)
        from_above = on_grads(l, grads[l]) if on_grads else {}
    (grad_x,) = _rowwise(_f_sum, dys, [], [(D, F32)], [], name="grad_x_sum")
    return loss_vec, grad_x, grads


def _pack_rows(arrs):
    parts, where, off = [], [], 0
    for t in arrs:
        t = t.reshape(-1, D)
        rows = t.shape[0]
        padded = -(-rows // 8) * 8
        parts.append(jnp.pad(t, ((0, padded - rows), (0, 0))))
        where.append((off, rows))
        off += padded
    return jnp.concatenate(parts, axis=0), where


FIRST = ("w_in",)
LATER = tuple(n for n in BIG if n not in FIRST)


def _first_weights(g, l, W):
    w = dict(win=g["w_in"].reshape(N_IN, D))
    for n in ("ret_gn_g", "ret_gn_b", "ln1_g", "ln1_b", "ln2_g", "ln2_b"):
        w[n] = W[n][l][None, :]
    return w


def _later_weights(g, l, conv_w_all, conv_b):
    w = dict(w_up=g["w_up"].reshape(2 * DFF, D), w_ple_proj=g["w_ple_proj"].reshape(D, PLE),
             w_attn_proj=g["w_attn_proj"].reshape(D, D), w_ret_proj=g["w_ret_proj"].reshape(RH * RDV, D),
             w_out=g["w_out"].reshape(D, D), w_down=g["w_down"].reshape(DFF, D), w_ple_gate=g["w_ple_gate"].reshape(D, D))
    w["conv_wg"], w["conv_wu"] = conv_w_all[l][:, :DFF], conv_w_all[l][:, DFF:]
    w["conv_bg"], w["conv_bu"] = conv_b[l][None, :DFF], conv_b[l][None, DFF:]
    return w


def _layer_weights(g, l, conv_w_all, conv_b, W):
    return {**_first_weights(g, l, W), **_later_weights(g, l, conv_w_all, conv_b)}


def kernel(x, p, positions, w_in, w_attn_proj, w_ret_proj, ret_gn_g, ret_gn_b, w_out, ln1_g, ln1_b, w_up, conv_w, conv_b, w_down, w_ple_gate, w_ple_proj, ln2_g, ln2_b, loss_target, m_w_in, m_w_attn_proj, m_w_ret_proj, m_ret_gn_g, m_ret_gn_b, m_w_out, m_ln1_g, m_ln1_b, m_w_up, m_conv_w, m_conv_b, m_w_down, m_w_ple_gate, m_w_ple_proj, m_ln2_g, m_ln2_b, v_w_in, v_w_attn_proj, v_w_ret_proj, v_ret_gn_g, v_ret_gn_b, v_w_out, v_ln1_g, v_ln1_b, v_w_up, v_conv_w, v_conv_b, v_w_down, v_w_ple_gate, v_w_ple_proj, v_ln2_g, v_ln2_b):
    W = dict(w_in=w_in, w_attn_proj=w_attn_proj, w_ret_proj=w_ret_proj, ret_gn_g=ret_gn_g, ret_gn_b=ret_gn_b, w_out=w_out,
             ln1_g=ln1_g, ln1_b=ln1_b, w_up=w_up, conv_w=conv_w, conv_b=conv_b, w_down=w_down, w_ple_gate=w_ple_gate,
             w_ple_proj=w_ple_proj, ln2_g=ln2_g, ln2_b=ln2_b)
    M = dict(w_in=m_w_in, w_attn_proj=m_w_attn_proj, w_ret_proj=m_w_ret_proj, ret_gn_g=m_ret_gn_g, ret_gn_b=m_ret_gn_b,
             w_out=m_w_out, ln1_g=m_ln1_g, ln1_b=m_ln1_b, w_up=m_w_up, conv_w=m_conv_w, conv_b=m_conv_b, w_down=m_w_down,
             w_ple_gate=m_w_ple_gate, w_ple_proj=m_w_ple_proj, ln2_g=m_ln2_g, ln2_b=m_ln2_b)
    V = dict(w_in=v_w_in, w_attn_proj=v_w_attn_proj, w_ret_proj=v_w_ret_proj, ret_gn_g=v_ret_gn_g, ret_gn_b=v_ret_gn_b,
             w_out=v_w_out, ln1_g=v_ln1_g, ln1_b=v_ln1_b, w_up=v_w_up, conv_w=v_conv_w, conv_b=v_conv_b, w_down=v_w_down,
             w_ple_gate=v_w_ple_gate, w_ple_proj=v_w_ple_proj, ln2_g=v_ln2_g, ln2_b=v_ln2_b)

    me = 4 * lax.axis_index("x") + 2 * lax.axis_index("y") + lax.axis_index("c")
    shard = lambda n, l: (W[n][l].T if n in COL_SHARDED else W[n][l]).astype(BF16)
    landing = lambda ts: [lax.dynamic_update_index_in_dim(lax.empty((N_DEV,) + t.shape, t.dtype), t, me, 0) for t in ts]
    first0 = _gather_many([shard(n, 0) for n in FIRST], "gather_first_l0")
    later0 = [shard(n, 0) for n in LATER] + [conv_w]
    flight0, token0 = _exchange_start("gather", later0, landing(later0), "gather_later_l0_start", after=first0[0])
    all1 = [shard(n, 1) for n in BIG]
    flight1, token1 = _exchange_start("gather", all1, landing(all1), "gather_weights_l1_start", after=token0)
    conv_w_all = []

    def later_first_layer(after):
        _, got = _exchange_wait(flight0, after, "gather_later_l0_wait")
        conv_w_all.append(got[-1].transpose(1, 2, 0, 3).reshape(DEPTH, 3, 2 * DFF))
        return _later_weights(dict(zip(LATER, got)), 0, conv_w_all[0], conv_b)

    def second_layer(after):
        _, got = _exchange_wait(flight1, after, "gather_weights_l1_wait")
        return _layer_weights(dict(zip(BIG, got)), 1, conv_w_all[0], conv_b, W)

    core = lax.axis_index("c").astype(jnp.int32).reshape(1)
    chip = (2 * lax.axis_index("x") + lax.axis_index("y")).astype(jnp.int32).reshape(1)
    empty_like = lambda ts, slots: [lax.empty((slots,) + t.shape[1:], t.dtype) for t in ts]
    chip32, far = [{} for _ in range(DEPTH)], [{} for _ in range(DEPTH)]
    pending = []

    def reduction(l, names, tag):
        state = {}

        def start(g):
            mine = [g[n].reshape((N_DEV, -1) + g[n].shape[1:]) for n in names]
            state["cores"], token = _exchange_start("cores", mine, empty_like(mine, 4), f"exchange_cores_{tag}_start")
            return token

        def onward(after):
            mine, theirs = _exchange_wait(state["cores"], after, f"exchange_cores_{tag}_wait")
            sums = [_pair_sum(a, b, core, f"pair_sum_l{l}_{n}") for a, b, n in zip(mine, theirs, names)]
            for n, s in zip(names, sums):
                chip32[l][n] = s[0]
            sent = [s[0 if n in F32_OVER_ICI else 1] for s, n in zip(sums, names)]
            flight, token = _exchange_start("chips", sent, empty_like(sent, 3), f"exchange_chips_{tag}_start")
            pending.append((l, names, flight, tag))
            return token

        return start, onward

    def on_grads(l, g):
        if l == 0:
            return {}
        start, onward = reduction(l, BIG, f"l{l}")
        return dict(token=start(g), after_ffn=onward)

    def own_hooks(l):
        if l != 0:
            return {}
        start_e, onward_e = reduction(0, LATER, "l0_later")
        start_w, onward_w = reduction(0, FIRST, "l0_first")
        return dict(early_grads=start_e, after_ret=onward_e, w_in_ready=lambda gw: start_w({"w_in": gw}), after_dx0=onward_w)

    ws = [(_first_weights(dict(zip(FIRST, first0)), 0, W), later_first_layer), second_layer]
    loss_vec, grad_x, grads = _local_step(x[0], p[:, 0], positions[0], loss_target[0], ws, own_hooks, on_grads, token1)
    loss = lax.psum(jnp.sum(loss_vec), ("x", "y", "c"))
    for l, names, flight, tag in pending:
        _, got = _exchange_wait(flight, grad_x, f"exchange_chips_{tag}_wait")
        far[l].update(zip(names, got))
    G, DW, NM, NV = ({} for _ in range(4))
    for n in BIG:
        chip32_n = [chip32[l][n] for l in range(DEPTH)]
        far_n = [far[l][n] for l in range(DEPTH)]
        if n in COL_SHARDED:
            G[n] = _reduce_tail(chip32_n, far_n, chip, f"reduced_{n}")[0].transpose(0, 2, 1)
            R2, C2 = DEPTH * W[n].shape[1], W[n].shape[2]
            res = _adamw(*(t.reshape(R2, C2) for t in (G[n], W[n], M[n], V[n])), f"adamw_{n}")
            DW[n], NM[n], NV[n] = (t.reshape(W[n].shape) for t in res)
        else:
            G[n], DW[n], NM[n], NV[n] = _reduce_tail(chip32_n, far_n, chip, f"adamw_{n}", wmv=(W[n], M[n], V[n]))

    small_names = SMALL + ("conv_w",)
    g_small, where = _pack_rows([jnp.stack([grads[l][n] for l in range(DEPTH)]) for n in small_names])
    (g_all,) = _gather_many([g_small], "gather_small_grads")
    g_small = _sum_slots(g_all, "sum_small_grads")
    for n, (off, rows) in zip(SMALL, where):
        G[n] = g_small[off:off + rows].reshape(W[n].shape)
    off, rows = where[-1]
    g_cw = g_small[off:off + rows].reshape(DEPTH, 3, N_DEV, conv_w.shape[2])
    G["conv_w"] = lax.dynamic_index_in_dim(g_cw, me, axis=2, keepdims=False)
    packed = [_pack_rows([d[n] for n in SMALL]) for d in (G, W, M, V)]
    small_out = _adamw(*(t for t, _ in packed), "adamw_small")
    for res, dst in zip(small_out, (DW, NM, NV)):
        for n, (off, rows) in zip(SMALL, packed[0][1]):
            dst[n] = res[off:off + rows].reshape(W[n].shape)
    two_d = lambda t: t.reshape(DEPTH * 3, conv_w.shape[2])
    cw_out = _adamw(two_d(G["conv_w"]), two_d(conv_w), two_d(m_conv_w), two_d(v_conv_w), "adamw_conv_w")
    for res, dst in zip(cw_out, (DW, NM, NV)):
        dst["conv_w"] = res.reshape(conv_w.shape)

    return (loss, grad_x[None], *[G[n] for n in WEIGHTS], *[DW[n] for n in WEIGHTS], *[NM[n] for n in WEIGHTS],
            *[NV[n] for n in WEIGHTS])
```
